```python
import math
import jax, jax.numpy as jnp
from jax import lax
import numpy as np

D_MODEL = 1024
BATCH = 4
SEQ = 8192
DEPTH = 2

HEAD_DIM = 64
ROPE_THETA = 10000.0
NORM_EPS = 1e-6
NEG_INF = -1e30
D_FF = 4 * D_MODEL
Q_BLOCK = 128
MAX_POS_OFFSET = 4096

MLA_HEADS = 8
MLA_Q_RANK = 384
MLA_KV_RANK = 256
MLA_NOPE = 64
MLA_ROPE = 32
MLA_V = 64

DIL_CONFIGS = ((128, 1), (512, 4), (2048, 16))
DIL_GROUPS = len(DIL_CONFIGS)
DIL_HEADS = 4

DIFF_HEADS = 4
DIFF_DIM = 64

MOBA_HEADS = 8
MOBA_BLOCK = 256
MOBA_TOPK = 3
MOBA_Q_CHUNK = 32

N_EVEN = (DEPTH + 1) // 2
N_ODD = DEPTH // 2

EVEN_IN = MLA_Q_RANK + MLA_KV_RANK + MLA_ROPE + 3 * DIL_GROUPS * DIL_HEADS * HEAD_DIM
EVEN_OUT = MLA_HEADS * MLA_V + DIL_HEADS * HEAD_DIM
ODD_IN = 3 * DIFF_HEADS * 2 * DIFF_DIM + 3 * MOBA_HEADS * HEAD_DIM
ODD_OUT = DIFF_HEADS * 2 * DIFF_DIM + MOBA_HEADS * HEAD_DIM

kernel_name = "hybrid_mla_dilated_diff_moba_adaln"


def rms_norm(x, w):
    xf = x.astype(jnp.float32)
    y = xf * lax.rsqrt(jnp.mean(xf * xf, axis=-1, keepdims=True) + NORM_EPS)
    return (y * w.astype(jnp.float32)).astype(x.dtype)


def apply_rope(x, positions):
    half = x.shape[-1] // 2
    inv_freq = ROPE_THETA ** (-jnp.arange(half, dtype=jnp.float32) / half)
    ang = positions.astype(jnp.float32)[:, :, None] * inv_freq
    bshape = ang.shape[:2] + (1,) * (x.ndim - 3) + (half,)
    cos = jnp.cos(ang).reshape(bshape)
    sin = jnp.sin(ang).reshape(bshape)
    xf = x.astype(jnp.float32)
    x1, x2 = xf[..., :half], xf[..., half:]
    return jnp.concatenate([x1 * cos - x2 * sin, x2 * cos + x1 * sin], axis=-1).astype(x.dtype)


def dense_causal_attention(q, k, v):
    b, s, h, dk = q.shape
    nq = s // Q_BLOCK
    scale = dk ** -0.5
    qb = q.reshape(b, nq, Q_BLOCK, h, dk).transpose(1, 0, 2, 3, 4)
    kpos = jnp.arange(s)

    def one_block(args):
        i, qi = args
        sc = jnp.einsum('bqhd,bkhd->bhqk', qi, k, preferred_element_type=jnp.float32) * scale
        qpos = i * Q_BLOCK + jnp.arange(Q_BLOCK)
        sc = jnp.where(qpos[:, None] >= kpos[None, :], sc, NEG_INF)
        p = jax.nn.softmax(sc, axis=-1)
        return jnp.einsum('bhqk,bkhe->bqhe', p.astype(v.dtype), v)

    out = lax.map(one_block, (jnp.arange(nq), qb))
    return out.transpose(1, 0, 2, 3, 4).reshape(b, s, h, v.shape[-1])


def diff_causal_attention(q, k, v, lam):
    b, s, h, _, d = q.shape
    nq = s // Q_BLOCK
    scale = d ** -0.5
    qb = q.reshape(b, nq, Q_BLOCK, h, 2, d).transpose(1, 0, 2, 3, 4, 5)
    kpos = jnp.arange(s)

    def one_block(args):
        i, qi = args
        sc = jnp.einsum('bqhmd,bkhmd->bmhqk', qi, k, preferred_element_type=jnp.float32) * scale
        qpos = i * Q_BLOCK + jnp.arange(Q_BLOCK)
        sc = jnp.where(qpos[:, None] >= kpos[None, :], sc, NEG_INF)
        p = jax.nn.softmax(sc, axis=-1)
        pd = p[:, 0] - lam * p[:, 1]
        return jnp.einsum('bhqk,bkhe->bqhe', pd.astype(v.dtype), v)

    out = lax.map(one_block, (jnp.arange(nq), qb))
    return out.transpose(1, 0, 2, 3, 4).reshape(b, s, h, v.shape[-1])


def sliding_window_lse(q, k, v, window):
    n, l, h, d = q.shape
    blk = window
    lp = -(-l // blk) * blk
    pad = ((0, 0), (0, lp - l), (0, 0), (0, 0))
    q, k, v = jnp.pad(q, pad), jnp.pad(k, pad), jnp.pad(v, pad)
    nb = lp // blk
    qb = q.reshape(n, nb, blk, h, d)
    kb = k.reshape(n, nb, blk, h, d)
    vb = v.reshape(n, nb, blk, h, d)
    prev_pad = ((0, 0), (1, 0), (0, 0), (0, 0), (0, 0))
    kk = jnp.concatenate([jnp.pad(kb[:, :-1], prev_pad), kb], axis=2)
    vv = jnp.concatenate([jnp.pad(vb[:, :-1], prev_pad), vb], axis=2)
    sc = jnp.einsum('nbqhd,nbkhd->nbhqk', qb, kk, preferred_element_type=jnp.float32) * d ** -0.5
    qloc = jnp.arange(blk) + blk
    kloc = jnp.arange(2 * blk)
    rel = qloc[:, None] - kloc[None, :]
    band = (rel >= 0) & (rel <= window)
    kabs = jnp.arange(nb)[:, None, None] * blk + kloc[None, None, :] - blk
    mask = band[None] & (kabs >= 0)
    sc = jnp.where(mask[None, :, None], sc, NEG_INF)
    lse = jax.nn.logsumexp(sc, axis=-1)
    p = jnp.exp(sc - lse[..., None])
    out = jnp.einsum('nbhqk,nbkhd->nbqhd', p.astype(v.dtype), vv).reshape(n, lp, h, d)[:, :l]
    lse = lse.transpose(0, 1, 3, 2).reshape(n, lp, h)[:, :l]
    return out, lse


def dilated_group(q, k, v, window, dilation):
    b, s, h, d = q.shape
    m = s // dilation

    def split(t):
        return t.reshape(b, m, dilation, h, d).transpose(0, 2, 1, 3, 4).reshape(b * dilation, m, h, d)

    out, lse = sliding_window_lse(split(q), split(k), split(v), window // dilation)
    out = out.reshape(b, dilation, m, h, d).transpose(0, 2, 1, 3, 4).reshape(b, s, h, d)
    lse = lse.reshape(b, dilation, m, h).transpose(0, 2, 1, 3).reshape(b, s, h)
    return out, lse


def dilated_mixture(q, k, v):
    outs, lses = [], []
    for g, (w, r) in enumerate(DIL_CONFIGS):
        o, l = dilated_group(q[:, :, g], k[:, :, g], v[:, :, g], w, r)
        outs.append(o)
        lses.append(l)
    alpha = jax.nn.softmax(jnp.stack(lses), axis=0)
    out = jnp.sum(alpha[..., None] * jnp.stack(outs).astype(jnp.float32), axis=0)
    return out.astype(q.dtype)


def moba_attention(q, k, v):
    b, s, h, d = q.shape
    blk = MOBA_BLOCK
    sp = -(-s // blk) * blk
    nb = sp // blk
    scale = d ** -0.5
    pad = ((0, 0), (0, sp - s), (0, 0), (0, 0))
    qp, kp, vp = jnp.pad(q, pad), jnp.pad(k, pad), jnp.pad(v, pad)
    qb = qp.reshape(b, nb, blk, h, d)
    kb = kp.reshape(b, nb, blk, h, d)
    vb = vp.reshape(b, nb, blk, h, d)

    s_own = jnp.einsum('bnqhd,bnkhd->bnhqk', qb, kb, preferred_element_type=jnp.float32) * scale
    s_own = jnp.where(jnp.tril(jnp.ones((blk, blk), dtype=bool)), s_own, NEG_INF)
    lse_own = jax.nn.logsumexp(s_own, axis=-1)
    o_own = jnp.einsum('bnhqk,bnkhd->bnqhd', jnp.exp(s_own - lse_own[..., None]).astype(v.dtype), vb)
    o_own = o_own.reshape(b, sp, h, d)[:, :s].astype(jnp.float32)
    lse_own = lse_own.transpose(0, 1, 3, 2).reshape(b, sp, h)[:, :s]

    kmean = jnp.mean(kb.astype(jnp.float32), axis=2)
    gate = jnp.einsum('bshd,bnhd->bhsn', q.astype(jnp.float32), kmean)
    qblk = jnp.arange(s) // blk
    past = jnp.arange(nb)[None, :] < qblk[:, None]
    gate = jnp.where(past, gate, NEG_INF)
    n_sel = min(MOBA_TOPK, nb)
    _, idx = lax.top_k(gate, n_sel)
    valid = idx < qblk[None, None, :, None]

    kbh = kb.transpose(0, 3, 1, 2, 4)
    vbh = vb.transpose(0, 3, 1, 2, 4)
    nc = s // MOBA_Q_CHUNK
    qc_all = q.transpose(0, 2, 1, 3).reshape(b, h, nc, MOBA_Q_CHUNK, d).transpose(2, 0, 1, 3, 4)
    ic_all = idx.reshape(b, h, nc, MOBA_Q_CHUNK, n_sel).transpose(2, 0, 1, 3, 4)
    vc_all = valid.reshape(b, h, nc, MOBA_Q_CHUNK, n_sel).transpose(2, 0, 1, 3, 4)
    bidx = jnp.arange(b)[:, None, None, None]
    hidx = jnp.arange(h)[None, :, None, None]

    def chunk(args):
        qc, ic, vc = args
        kg = kbh[bidx, hidx, ic]
        vg = vbh[bidx, hidx, ic]
        sc = jnp.einsum('bhqd,bhqjkd->bhqjk', qc, kg, preferred_element_type=jnp.float32) * scale
        sc = jnp.where(vc[..., None], sc, NEG_INF).reshape(b, h, MOBA_Q_CHUNK, n_sel * blk)
        lse = jax.nn.logsumexp(sc, axis=-1)
        p = jnp.exp(sc - lse[..., None]).reshape(b, h, MOBA_Q_CHUNK, n_sel, blk)
        o = jnp.einsum('bhqjk,bhqjkd->bhqd', p.astype(vg.dtype), vg)
        return o, lse

    o_sel, lse_sel = lax.map(chunk, (qc_all, ic_all, vc_all))
    o_sel = o_sel.transpose(1, 0, 3, 2, 4).reshape(b, s, h, d).astype(jnp.float32)
    lse_sel = lse_sel.transpose(1, 0, 3, 2).reshape(b, s, h)

    lse_tot = jnp.logaddexp(lse_own, lse_sel)
    out = (jnp.exp(lse_own - lse_tot)[..., None] * o_own
           + jnp.exp(lse_sel - lse_tot)[..., None] * o_sel)
    return out.astype(q.dtype)


def even_mixer(h, positions, w_in, w_out, q_lat_norm, kv_lat_norm, w_uq, w_ukv,
               mla_qn, mla_kn, dil_qn, dil_kn):
    b, s, _ = h.shape
    u = h @ w_in
    o1 = MLA_Q_RANK
    o2 = o1 + MLA_KV_RANK
    o3 = o2 + MLA_ROPE
    c_q, c_kv, k_r, u_dil = u[..., :o1], u[..., o1:o2], u[..., o2:o3], u[..., o3:]

    q = jnp.einsum('bsr,rhd->bshd', rms_norm(c_q, q_lat_norm), w_uq)
    kv = jnp.einsum('bsr,rhd->bshd', rms_norm(c_kv, kv_lat_norm), w_ukv)
    k_nope, v_a = kv[..., :MLA_NOPE], kv[..., MLA_NOPE:]
    k_rope = jnp.broadcast_to(k_r[:, :, None, :], (b, s, MLA_HEADS, MLA_ROPE))
    k = jnp.concatenate([k_nope, k_rope], axis=-1)
    q, k = rms_norm(q, mla_qn), rms_norm(k, mla_kn)
    q = jnp.concatenate([q[..., :MLA_NOPE], apply_rope(q[..., MLA_NOPE:], positions)], axis=-1)
    k = jnp.concatenate([k[..., :MLA_NOPE], apply_rope(k[..., MLA_NOPE:], positions)], axis=-1)
    o_a = dense_causal_attention(q, k, v_a)

    qkv = u_dil.reshape(b, s, 3, DIL_GROUPS, DIL_HEADS, HEAD_DIM)
    qd = apply_rope(rms_norm(qkv[:, :, 0], dil_qn), positions)
    kd = apply_rope(rms_norm(qkv[:, :, 1], dil_kn), positions)
    o_b = dilated_mixture(qd, kd, qkv[:, :, 2])

    o = jnp.concatenate([o_a.reshape(b, s, -1), o_b.reshape(b, s, -1)], axis=-1)
    return o @ w_out


def odd_mixer(h, positions, w_in, w_out, diff_qn, diff_kn, diff_lambda, diff_subln,
              moba_qn, moba_kn, lam_init):
    b, s, _ = h.shape
    u = h @ w_in
    nqk = DIFF_HEADS * 2 * DIFF_DIM

    qc = u[..., :nqk].reshape(b, s, DIFF_HEADS, 2, DIFF_DIM)
    kc = u[..., nqk:2 * nqk].reshape(b, s, DIFF_HEADS, 2, DIFF_DIM)
    vc = u[..., 2 * nqk:3 * nqk].reshape(b, s, DIFF_HEADS, 2 * DIFF_DIM)
    qc = apply_rope(rms_norm(qc, diff_qn), positions)
    kc = apply_rope(rms_norm(kc, diff_kn), positions)
    lv = diff_lambda.astype(jnp.float32)
    lam = jnp.exp(jnp.sum(lv[0] * lv[1])) - jnp.exp(jnp.sum(lv[2] * lv[3])) + lam_init
    o_c = diff_causal_attention(qc, kc, vc, lam)
    o_c = rms_norm(o_c, diff_subln) * (1.0 - lam_init)

    m = u[..., 3 * nqk:].reshape(b, s, 3, MOBA_HEADS, HEAD_DIM)
    qm = apply_rope(rms_norm(m[:, :, 0], moba_qn), positions)
    km = apply_rope(rms_norm(m[:, :, 1], moba_kn), positions)
    o_d = moba_attention(qm, km, m[:, :, 2])

    o = jnp.concatenate([o_c.reshape(b, s, -1), o_d.reshape(b, s, -1)], axis=-1)
    return o @ w_out


def squared_relu_mlp(h, w1, w2):
    a = jax.nn.relu(h @ w1)
    return (a * a) @ w2


def modulate(xn, shift, scale):
    return xn * (1.0 + scale[:, None, :]) + shift[:, None, :]


def setup_inputs(seed: int = 0) -> dict:
    key = jax.random.key(seed)
    ks = jax.random.split(key, 27)
    f32 = jnp.float32

    def nrm(k, shape, scale):
        return jax.random.normal(k, shape, f32) * scale

    def gain(k, shape):
        return 1.0 + 0.1 * jax.random.normal(k, shape, f32)

    offset = jax.random.randint(ks[2], (BATCH, 1), 0, MAX_POS_OFFSET, dtype=jnp.int32)
    positions = offset + jnp.arange(SEQ, dtype=jnp.int32)[None, :]
    return {
        "x": nrm(ks[0], (BATCH, SEQ, D_MODEL), 1.0),
        "c": nrm(ks[1], (BATCH, D_MODEL), 1.0),
        "positions": positions,
        "ada_w": nrm(ks[3], (DEPTH, D_MODEL, 6 * D_MODEL), 0.5 * D_MODEL ** -0.5),
        "ada_b": nrm(ks[4], (DEPTH, 6 * D_MODEL), 0.02),
        "norm_mix": gain(ks[5], (DEPTH, D_MODEL)),
        "norm_mlp": gain(ks[6], (DEPTH, D_MODEL)),
        "mlp_w1": nrm(ks[7], (DEPTH, D_MODEL, D_FF), D_MODEL ** -0.5),
        "mlp_w2": nrm(ks[8], (DEPTH, D_FF, D_MODEL), D_FF ** -0.5),
        "even_w_in": nrm(ks[9], (N_EVEN, D_MODEL, EVEN_IN), D_MODEL ** -0.5),
        "even_w_out": nrm(ks[10], (N_EVEN, EVEN_OUT, D_MODEL), EVEN_OUT ** -0.5),
        "mla_q_lat_norm": gain(ks[11], (N_EVEN, MLA_Q_RANK)),
        "mla_kv_lat_norm": gain(ks[12], (N_EVEN, MLA_KV_RANK)),
        "mla_w_uq": nrm(ks[13], (N_EVEN, MLA_Q_RANK, MLA_HEADS, MLA_NOPE + MLA_ROPE), MLA_Q_RANK ** -0.5),
        "mla_w_ukv": nrm(ks[14], (N_EVEN, MLA_KV_RANK, MLA_HEADS, MLA_NOPE + MLA_V), MLA_KV_RANK ** -0.5),
        "mla_q_norm": gain(ks[15], (N_EVEN, MLA_NOPE + MLA_ROPE)),
        "mla_k_norm": gain(ks[16], (N_EVEN, MLA_NOPE + MLA_ROPE)),
        "dil_q_norm": gain(ks[17], (N_EVEN, HEAD_DIM)),
        "dil_k_norm": gain(ks[18], (N_EVEN, HEAD_DIM)),
        "odd_w_in": nrm(ks[19], (N_ODD, D_MODEL, ODD_IN), D_MODEL ** -0.5),
        "odd_w_out": nrm(ks[20], (N_ODD, ODD_OUT, D_MODEL), ODD_OUT ** -0.5),
        "diff_q_norm": gain(ks[21], (N_ODD, DIFF_DIM)),
        "diff_k_norm": gain(ks[22], (N_ODD, DIFF_DIM)),
        "diff_lambda": nrm(ks[23], (N_ODD, 4, DIFF_DIM), 0.1),
        "diff_subln": gain(ks[24], (N_ODD, 2 * DIFF_DIM)),
        "moba_q_norm": gain(ks[25], (N_ODD, HEAD_DIM)),
        "moba_k_norm": gain(ks[26], (N_ODD, HEAD_DIM)),
    }


def reference(x, c, positions, ada_w, ada_b, norm_mix, norm_mlp, mlp_w1, mlp_w2,
              even_w_in, even_w_out, mla_q_lat_norm, mla_kv_lat_norm, mla_w_uq, mla_w_ukv,
              mla_q_norm, mla_k_norm, dil_q_norm, dil_k_norm,
              odd_w_in, odd_w_out, diff_q_norm, diff_k_norm, diff_lambda, diff_subln,
              moba_q_norm, moba_k_norm):
    cond = jax.nn.silu(c.astype(jnp.float32)).astype(x.dtype)
    for layer in range(DEPTH):
        mod = cond @ ada_w[layer] + ada_b[layer]
        sh1, sc1, g1, sh2, sc2, g2 = jnp.split(mod, 6, axis=-1)
        h = modulate(rms_norm(x, norm_mix[layer]), sh1, sc1)
        i = layer // 2
        if layer % 2 == 0:
            y = even_mixer(h, positions, even_w_in[i], even_w_out[i], mla_q_lat_norm[i],
                           mla_kv_lat_norm[i], mla_w_uq[i], mla_w_ukv[i], mla_q_norm[i],
                           mla_k_norm[i], dil_q_norm[i], dil_k_norm[i])
        else:
            lam_init = 0.8 - 0.6 * math.exp(-0.3 * layer)
            y = odd_mixer(h, positions, odd_w_in[i], odd_w_out[i], diff_q_norm[i], diff_k_norm[i],
                          diff_lambda[i], diff_subln[i], moba_q_norm[i], moba_k_norm[i], lam_init)
        x = x + g1[:, None, :] * y
        h = modulate(rms_norm(x, norm_mlp[layer]), sh2, sc2)
        x = x + g2[:, None, :] * squared_relu_mlp(h, mlp_w1[layer], mlp_w2[layer])
    return x
```

```python
import functools
import math

import jax
import jax.numpy as jnp
from jax import lax
from jax.experimental import pallas as pl
from jax.experimental.pallas import tpu as pltpu

F32 = jnp.float32
BF16 = jnp.bfloat16

LANES = 128
HEAD_DIM = 64
ROPE_THETA = 10000.0
NORM_EPS = 1e-6
NEG_INF = -1e30
LOG2E = math.log2(math.e)

MLA_HEADS = 8
MLA_Q_RANK = 384
MLA_KV_RANK = 256
MLA_NOPE = 64
MLA_ROPE = 32
MLA_QK = MLA_NOPE + MLA_ROPE
DIL_CONFIGS = ((128, 1), (512, 4), (2048, 16))
DIL_HEADS = 4
DIL_W = DIL_HEADS * HEAD_DIM
DIFF_HEADS = 4
MOBA_HEADS = 8
MOBA_BLOCK = 256
MOBA_TOPK = 3
SW_BLOCK = 128

VMEM_LIMIT = 56 * 1024 * 1024


def _cparams(sem):
    return pltpu.CompilerParams(dimension_semantics=sem, vmem_limit_bytes=VMEM_LIMIT)


def _nt_dot(a, b):
    return lax.dot_general(a, b, (((1,), (1,)), ((), ())), preferred_element_type=F32)


def _rms(x, w):
    return x * lax.rsqrt(jnp.mean(x * x, axis=-1, keepdims=True) + NORM_EPS) * w


def _lane_iota(shape):
    return lax.broadcasted_iota(jnp.int32, shape, len(shape) - 1)


def _norm_cols64(x, gain):
    lane = _lane_iota((1, LANES))
    lo = lane < HEAD_DIM
    sq = x * x
    s_lo = jnp.sum(jnp.where(lo, sq, 0.0), axis=-1, keepdims=True)
    s_hi = jnp.sum(jnp.where(lo, 0.0, sq), axis=-1, keepdims=True)
    ms = jnp.where(lo, s_lo, s_hi) * (1.0 / HEAD_DIM)
    return x * lax.rsqrt(ms + NORM_EPS) * gain


def _rope_cols64(x, cos, sin):
    lane = _lane_iota((1, LANES))
    first = (lane % HEAD_DIM) < (HEAD_DIM // 2)
    fwd = pltpu.roll(x, LANES - HEAD_DIM // 2, 1)
    bwd = pltpu.roll(x, HEAD_DIM // 2, 1)
    return x * cos + jnp.where(first, fwd, bwd) * sin


def _rope_mla(x, cos, sin):
    lane = _lane_iota((1, LANES))
    first = lane < (MLA_NOPE + MLA_ROPE // 2)
    fwd = pltpu.roll(x, LANES - MLA_ROPE // 2, 1)
    bwd = pltpu.roll(x, MLA_ROPE // 2, 1)
    return x * cos + jnp.where(first, fwd, bwd) * sin


def _adaln_kernel(c_ref, w_ref, b_ref, o_ref):
    c = c_ref[...]
    cond = c * (1.0 / (1.0 + jnp.exp(-c)))
    o_ref[0] = jnp.dot(cond, w_ref[0], preferred_element_type=F32,
                       precision=lax.Precision.HIGHEST) + b_ref[0]


def _adaln(c, ada_w, ada_b):
    depth, d, n = ada_w.shape
    b = c.shape[0]
    tn = 1024
    return pl.pallas_call(
        _adaln_kernel,
        grid=(depth, n // tn),
        in_specs=[pl.BlockSpec((b, d), lambda l, j: (0, 0)),
                  pl.BlockSpec((1, d, tn), lambda l, j: (l, 0, j)),
                  pl.BlockSpec((1, 1, tn), lambda l, j: (l, 0, j))],
        out_specs=pl.BlockSpec((1, b, tn), lambda l, j: (l, 0, j)),
        out_shape=jax.ShapeDtypeStruct((depth, b, n), F32),
        compiler_params=_cparams(("parallel", "parallel")),
        name="adaln",
    )(c, ada_w, ada_b.reshape(depth, 1, n))


def _even_in_kernel(x_ref, sh_ref, sc_ref, nw_ref, win_ref, wuq_ref, wukv_ref,
                    qlat_ref, kvlat_ref, qn_ref, kn_ref, dqn_ref, dkn_ref,
                    cm_ref, sm_ref, c64_ref, s64_ref,
                    q_out, k_out, v_out, dil_out):
    x = x_ref[0]
    h = _rms(x, nw_ref[...]) * (1.0 + sc_ref[0]) + sh_ref[0]
    u = jnp.dot(h.astype(BF16), win_ref[...], preferred_element_type=F32)

    o1 = MLA_Q_RANK
    o2 = o1 + MLA_KV_RANK
    o3 = o2 + LANES
    cqn = _rms(u[:, :o1], qlat_ref[...]).astype(BF16)
    qp = jnp.dot(cqn, wuq_ref[...], preferred_element_type=F32)
    ckvn = _rms(u[:, o1:o2], kvlat_ref[...]).astype(BF16)
    kvp = jnp.dot(ckvn, wukv_ref[...], preferred_element_type=F32)
    kr = pltpu.roll(u[:, o2:o3], MLA_NOPE, 1)

    cm, sm = cm_ref[0], sm_ref[0]
    q_scale = MLA_QK ** -0.5 * LOG2E
    for hd in range(MLA_HEADS):
        sl = slice(LANES * hd, LANES * (hd + 1))
        qh = qp[:, sl]
        qh = qh * lax.rsqrt(jnp.sum(qh * qh, -1, keepdims=True) * (1.0 / MLA_QK) + NORM_EPS) * qn_ref[...]
        q_out[0, :, sl] = (_rope_mla(qh, cm, sm) * q_scale).astype(BF16)
        kh = kvp[:, sl] + kr
        kh = kh * lax.rsqrt(jnp.sum(kh * kh, -1, keepdims=True) * (1.0 / MLA_QK) + NORM_EPS) * kn_ref[...]
        k_out[0, :, sl] = _rope_mla(kh, cm, sm).astype(BF16)
    v_out[0] = kvp[:, MLA_HEADS * LANES:].astype(BF16)

    c64, s64 = c64_ref[0], s64_ref[0]
    d_scale = HEAD_DIM ** -0.5 * LOG2E
    for g in range(len(DIL_CONFIGS)):
        base = o3 + 3 * DIL_W * g
        for col in range(DIL_W // LANES):
            a = base + LANES * col
            qc = _rope_cols64(_norm_cols64(u[:, a:a + LANES], dqn_ref[...]), c64, s64) * d_scale
            dil_out[0, :, a - o3:a - o3 + LANES] = qc.astype(BF16)
            a = base + DIL_W + LANES * col
            kc = _rope_cols64(_norm_cols64(u[:, a:a + LANES], dkn_ref[...]), c64, s64)
            dil_out[0, :, a - o3:a - o3 + LANES] = kc.astype(BF16)
        a = base + 2 * DIL_W
        dil_out[0, :, a - o3:a - o3 + DIL_W] = u[:, a:a + DIL_W].astype(BF16)


def _row(v):
    return v.reshape(1, -1).astype(F32)


def _even_in(x, sh, sc, nw, win, wuq, wukv, qlat, kvlat, qn, kn, dqn, dkn, cm, sm, c64, s64, tm):
    b, s, d = x.shape
    n_in = win.shape[1]
    n_dil = 3 * DIL_W * len(DIL_CONFIGS)
    tok = lambda w: pl.BlockSpec((1, tm, w), lambda bi, i: (bi, i, 0))
    per_b = pl.BlockSpec((1, 1, d), lambda bi, i: (bi, 0, 0))
    full = lambda a: pl.BlockSpec(a.shape, lambda bi, i: (0,) * a.ndim)
    return pl.pallas_call(
        _even_in_kernel,
        grid=(b, s // tm),
        in_specs=[tok(d), per_b, per_b, full(nw), full(win), full(wuq), full(wukv),
                  full(qlat), full(kvlat), full(qn), full(kn), full(dqn), full(dkn),
                  tok(LANES), tok(LANES), tok(LANES), tok(LANES)],
        out_specs=[tok(MLA_HEADS * LANES), tok(MLA_HEADS * LANES), tok(MLA_HEADS * HEAD_DIM), tok(n_dil)],
        out_shape=[jax.ShapeDtypeStruct((b, s, MLA_HEADS * LANES), BF16),
                   jax.ShapeDtypeStruct((b, s, MLA_HEADS * LANES), BF16),
                   jax.ShapeDtypeStruct((b, s, MLA_HEADS * HEAD_DIM), BF16),
                   jax.ShapeDtypeStruct((b, s, n_dil), BF16)],
        compiler_params=_cparams(("parallel", "arbitrary")),
        name="even_in_proj",
    )(x, sh, sc, nw, win, wuq, wukv, qlat, kvlat, qn, kn, dqn, dkn, cm, sm, c64, s64)


def _odd_in_kernel(x_ref, sh_ref, sc_ref, nw_ref, win_ref, dqn_ref, dkn_ref, mqn_ref, mkn_ref,
                   c64_ref, s64_ref,
                   qc_out, kc_out, vc_out, qm_out, qmf_out, km_out, vm_out, kmean_out):
    x = x_ref[0]
    tm = x.shape[0]
    h = _rms(x, nw_ref[...]) * (1.0 + sc_ref[0]) + sh_ref[0]
    u = jnp.dot(h.astype(BF16), win_ref[...], preferred_element_type=F32)
    c64, s64 = c64_ref[0], s64_ref[0]
    scale = HEAD_DIM ** -0.5 * LOG2E
    nqk = DIFF_HEADS * LANES
    for col in range(nqk // LANES):
        sl = slice(LANES * col, LANES * (col + 1))
        qc = _rope_cols64(_norm_cols64(u[:, sl], dqn_ref[...]), c64, s64) * scale
        qc_out[0, :, sl] = qc.astype(BF16)
        kc = _rope_cols64(_norm_cols64(u[:, nqk + LANES * col:nqk + LANES * (col + 1)], dkn_ref[...]), c64, s64)
        kc_out[0, :, sl] = kc.astype(BF16)
    vc_out[0] = u[:, 2 * nqk:3 * nqk].astype(BF16)

    nm = MOBA_HEADS * HEAD_DIM
    base = 3 * nqk
    for col in range(nm // LANES):
        sl = slice(LANES * col, LANES * (col + 1))
        qm = _rope_cols64(_norm_cols64(u[:, base + LANES * col:base + LANES * (col + 1)], mqn_ref[...]), c64, s64)
        qmf_out[0, :, sl] = qm
        qm_out[0, :, sl] = (qm * scale).astype(BF16)
        a = base + nm + LANES * col
        km = _rope_cols64(_norm_cols64(u[:, a:a + LANES], mkn_ref[...]), c64, s64)
        km_out[0, :, sl] = km.astype(BF16)
        for blk in range(tm // MOBA_BLOCK):
            rows = slice(MOBA_BLOCK * blk, MOBA_BLOCK * (blk + 1))
            kmean_out[0, 0, blk:blk + 1, sl] = jnp.mean(km[rows], axis=0, keepdims=True)
    vm_out[0] = u[:, base + 2 * nm:base + 3 * nm].astype(BF16)


def _odd_in(x, sh, sc, nw, win, dqn, dkn, mqn, mkn, c64, s64, tm):
    b, s, d = x.shape
    tok = lambda w: pl.BlockSpec((1, tm, w), lambda bi, i: (bi, i, 0))
    per_b = pl.BlockSpec((1, 1, d), lambda bi, i: (bi, 0, 0))
    full = lambda a: pl.BlockSpec(a.shape, lambda bi, i: (0,) * a.ndim)
    w = DIFF_HEADS * LANES
    nblk = tm // MOBA_BLOCK
    shp = lambda dt: jax.ShapeDtypeStruct((b, s, w), dt)
    return pl.pallas_call(
        _odd_in_kernel,
        grid=(b, s // tm),
        in_specs=[tok(d), per_b, per_b, full(nw), full(win), full(dqn), full(dkn), full(mqn), full(mkn),
                  tok(LANES), tok(LANES)],
        out_specs=[tok(w)] * 7 + [pl.BlockSpec((1, 1, nblk, w), lambda bi, i: (bi, i, 0, 0))],
        out_shape=[shp(BF16), shp(BF16), shp(BF16), shp(BF16), shp(F32), shp(BF16), shp(BF16),
                   jax.ShapeDtypeStruct((b, s // tm, nblk, w), F32)],
        compiler_params=_cparams(("parallel", "arbitrary")),
        name="odd_in_proj",
    )(x, sh, sc, nw, win, dqn, dkn, mqn, mkn, c64, s64)


def _flash_streams(qs, k_ref, v_ref, k_slices, v_slices, m_scr, l_scr, acc_scr, i, tq, mask_fns):
    n = len(qs)
    row = lax.broadcasted_iota(jnp.int32, (tq, tq), 0)
    col = lax.broadcasted_iota(jnp.int32, (tq, tq), 1)
    causal = row >= col
    d0 = pl.multiple_of(i * tq, tq)
    for s in range(n):
        k = k_ref[0, pl.ds(d0, tq), k_slices[s]]
        v = v_ref[0, pl.ds(d0, tq), v_slices[s]]
        sc = jnp.where(causal, _nt_dot(qs[s], k), NEG_INF)
        m = jnp.max(sc, axis=-1, keepdims=True)
        p = jnp.exp2(sc - m)
        m_scr[s] = m
        l_scr[s] = jnp.sum(p, axis=-1, keepdims=True)
        acc_scr[s] = jnp.dot(p.astype(BF16), v, preferred_element_type=F32)

    def body(j, carry):
        j0 = pl.multiple_of(j * tq, tq)
        for s in range(n):
            k = k_ref[0, pl.ds(j0, tq), k_slices[s]]
            v = v_ref[0, pl.ds(j0, tq), v_slices[s]]
            sc = _nt_dot(qs[s], k)
            if mask_fns[s] is not None:
                sc = jnp.where(mask_fns[s](j), sc, NEG_INF)
            m_prev = m_scr[s]
            m_new = jnp.maximum(m_prev, jnp.max(sc, axis=-1, keepdims=True))
            alpha = jnp.exp2(m_prev - m_new)
            p = jnp.exp2(sc - m_new)
            l_scr[s] = alpha * l_scr[s] + jnp.sum(p, axis=-1, keepdims=True)
            acc_scr[s] = alpha * acc_scr[s] + jnp.dot(p.astype(BF16), v, preferred_element_type=F32)
            m_scr[s] = m_new
        return carry

    lax.fori_loop(0, i, body, 0)


def _half_mask(x, half):
    lane = _lane_iota((1, LANES))
    return jnp.where((lane >= HEAD_DIM) == bool(half), x, jnp.zeros_like(x))


def _mla_kernel(q_ref, k_ref, v_ref, o_ref, m_scr, l_scr, acc_scr, *, tq):
    i = pl.program_id(2)
    qs = [q_ref[0, :, LANES * s:LANES * (s + 1)] for s in range(2)]
    ks = [slice(LANES * s, LANES * (s + 1)) for s in range(2)]
    vs = [slice(0, LANES)] * 2
    _flash_streams(qs, k_ref, v_ref, ks, vs, m_scr, l_scr, acc_scr, i, tq, [None, None])
    lane = _lane_iota((1, LANES))
    o0 = acc_scr[0] / l_scr[0]
    o1 = acc_scr[1] / l_scr[1]
    o_ref[0] = jnp.where(lane < HEAD_DIM, o0, o1).astype(o_ref.dtype)


def _mla_attention(q, k, v, tq):
    b, s, _ = q.shape
    pairs = MLA_HEADS // 2
    return pl.pallas_call(
        functools.partial(_mla_kernel, tq=tq),
        grid=(b, pairs, s // tq),
        in_specs=[pl.BlockSpec((1, tq, 2 * LANES), lambda bi, p, i: (bi, i, p)),
                  pl.BlockSpec((1, s, 2 * LANES), lambda bi, p, i: (bi, 0, p)),
                  pl.BlockSpec((1, s, LANES), lambda bi, p, i: (bi, 0, p))],
        out_specs=pl.BlockSpec((1, tq, LANES), lambda bi, p, i: (bi, i, p)),
        out_shape=jax.ShapeDtypeStruct((b, s, MLA_HEADS * HEAD_DIM), BF16),
        scratch_shapes=[pltpu.VMEM((2, tq, 1), F32), pltpu.VMEM((2, tq, 1), F32),
                        pltpu.VMEM((2, tq, LANES), F32)],
        compiler_params=_cparams(("parallel", "parallel", "arbitrary")),
        name="mla_attention",
    )(q, k, v)


def _diff_kernel(q_ref, k_ref, v_ref, lam_ref, sub_ref, o_ref, m_scr, l_scr, acc_scr, *, tq, lam_init):
    i = pl.program_id(2)
    q = q_ref[0]
    qs = [_half_mask(q, 0), _half_mask(q, 1)]
    full = slice(0, LANES)
    _flash_streams(qs, k_ref, v_ref, [full, full], [full, full], m_scr, l_scr, acc_scr, i, tq, [None, None])
    lv = lam_ref[...]
    lam = (jnp.exp(jnp.sum(lv[0:1] * lv[1:2], keepdims=True))
           - jnp.exp(jnp.sum(lv[2:3] * lv[3:4], keepdims=True)) + lam_init)
    o = acc_scr[0] / l_scr[0] - lam * (acc_scr[1] / l_scr[1])
    o_ref[0] = (_rms(o, sub_ref[...]) * (1.0 - lam_init)).astype(o_ref.dtype)


def _diff_attention(q, k, v, lam_rows, subln, lam_init, tq):
    b, s, _ = q.shape
    return pl.pallas_call(
        functools.partial(_diff_kernel, tq=tq, lam_init=lam_init),
        grid=(b, DIFF_HEADS, s // tq),
        in_specs=[pl.BlockSpec((1, tq, LANES), lambda bi, h, i: (bi, i, h)),
                  pl.BlockSpec((1, s, LANES), lambda bi, h, i: (bi, 0, h)),
                  pl.BlockSpec((1, s, LANES), lambda bi, h, i: (bi, 0, h)),
                  pl.BlockSpec(lam_rows.shape, lambda bi, h, i: (0, 0)),
                  pl.BlockSpec(subln.shape, lambda bi, h, i: (0, 0))],
        out_specs=pl.BlockSpec((1, tq, LANES), lambda bi, h, i: (bi, i, h)),
        out_shape=jax.ShapeDtypeStruct((b, s, DIFF_HEADS * LANES), BF16),
        scratch_shapes=[pltpu.VMEM((2, tq, 1), F32), pltpu.VMEM((2, tq, 1), F32),
                        pltpu.VMEM((2, tq, LANES), F32)],
        compiler_params=_cparams(("parallel", "parallel", "arbitrary")),
        name="diff_attention",
    )(q, k, v, lam_rows, subln)


def _moba_kernel(q_ref, qf_ref, k_ref, v_ref, kmean_ref, o_ref, m_scr, l_scr, acc_scr, *, tq):
    i = pl.program_id(2)
    q = q_ref[0]
    qf = qf_ref[0]
    kmean = kmean_ref[0]
    nb = kmean.shape[0]
    bcol = lax.broadcasted_iota(jnp.int32, (tq, nb), 1).astype(F32)
    past = bcol < i.astype(F32)
    qs, mask_fns = [], []
    for half in range(2):
        gate = lax.dot_general(_half_mask(qf, half), kmean, (((1,), (1,)), ((), ())),
                               preferred_element_type=F32, precision=lax.Precision.HIGHEST)
        gate = jnp.where(past, gate, NEG_INF)
        sel = jnp.zeros((tq, nb), F32)
        for _ in range(MOBA_TOPK):
            top = jnp.max(gate, axis=-1, keepdims=True)
            first = jnp.min(jnp.where(gate == top, bcol, float(nb)), axis=-1, keepdims=True)
            pick = bcol == first
            sel = jnp.where(pick, 1.0, sel)
            gate = jnp.where(pick, 2 * NEG_INF, gate)
        sel = jnp.where(past, sel, 0.0)
        qs.append(_half_mask(q, half))
        mask_fns.append(lambda j, sel=sel: jnp.max(jnp.where(bcol == j.astype(F32), sel, 0.0),
                                                   axis=-1, keepdims=True) > 0.5)
    full = slice(0, LANES)
    _flash_streams(qs, k_ref, v_ref, [full, full], [full, full], m_scr, l_scr, acc_scr, i, tq, mask_fns)
    lane = _lane_iota((1, LANES))
    o0 = acc_scr[0] / l_scr[0]
    o1 = acc_scr[1] / l_scr[1]
    o_ref[0] = jnp.where(lane < HEAD_DIM, o0, o1).astype(o_ref.dtype)


def _moba_attention(q, qf, k, v, kmean):
    b, s, w = q.shape
    tq = MOBA_BLOCK
    nb = kmean.shape[1]
    return pl.pallas_call(
        functools.partial(_moba_kernel, tq=tq),
        grid=(b, w // LANES, s // tq),
        in_specs=[pl.BlockSpec((1, tq, LANES), lambda bi, p, i: (bi, i, p)),
                  pl.BlockSpec((1, tq, LANES), lambda bi, p, i: (bi, i, p)),
                  pl.BlockSpec((1, s, LANES), lambda bi, p, i: (bi, 0, p)),
                  pl.BlockSpec((1, s, LANES), lambda bi, p, i: (bi, 0, p)),
                  pl.BlockSpec((1, nb, LANES), lambda bi, p, i: (bi, 0, p))],
        out_specs=pl.BlockSpec((1, tq, LANES), lambda bi, p, i: (bi, i, p)),
        out_shape=jax.ShapeDtypeStruct((b, s, w), BF16),
        scratch_shapes=[pltpu.VMEM((2, tq, 1), F32), pltpu.VMEM((2, tq, 1), F32),
                        pltpu.VMEM((2, tq, LANES), F32)],
        compiler_params=_cparams(("parallel", "parallel", "arbitrary")),
        name="moba_attention",
    )(q, qf, k, v, kmean)


def _sw_kernel(q_ref, kc_ref, vc_ref, kp_ref, vp_ref, o_ref, l_ref, *, nbk):
    i = pl.program_id(2)
    blk = SW_BLOCK
    qi = lax.broadcasted_iota(jnp.int32, (blk, 2 * blk), 0)
    kj = lax.broadcasted_iota(jnp.int32, (blk, 2 * blk), 1)
    band = (kj >= qi) & (kj <= qi + blk)
    lane = _lane_iota((1, LANES))
    for n in range(nbk):
        rows = slice(blk * n, blk * (n + 1))
        q = q_ref[0, rows, :]
        if n == 0:
            kprev, vprev = kp_ref[0], vp_ref[0]
            valid = band & ((kj >= blk) | (i > 0))
        else:
            prev = slice(blk * (n - 1), blk * n)
            kprev, vprev = kc_ref[0, prev, :], vc_ref[0, prev, :]
            valid = band
        kk = jnp.concatenate([kprev, kc_ref[0, rows, :]], axis=0)
        vv = jnp.concatenate([vprev, vc_ref[0, rows, :]], axis=0)
        for pair in range(DIL_W // LANES):
            sl = slice(LANES * pair, LANES * (pair + 1))
            outs, lses = [], []
            for half in range(2):
                sc = jnp.where(valid, _nt_dot(_half_mask(q[:, sl], half), kk[:, sl]), NEG_INF)
                m = jnp.max(sc, axis=-1, keepdims=True)
                p = jnp.exp2(sc - m)
                l = jnp.sum(p, axis=-1, keepdims=True)
                outs.append(jnp.dot(p.astype(BF16), vv[:, sl], preferred_element_type=F32) / l)
                lses.append(m + jnp.log2(l))
            o_ref[0, rows, sl] = jnp.where(lane < HEAD_DIM, outs[0], outs[1])
            l_ref[0, rows, sl] = jnp.where(lane < HEAD_DIM, lses[0], lses[1])


def _sliding_window(dil, g, nbk):
    b, s, n_dil = dil.shape
    _, r = DIL_CONFIGS[g]
    m = s // r
    nbk = min(nbk, m // SW_BLOCK)
    rows = SW_BLOCK * nbk
    ncol = n_dil // DIL_W
    view = dil.reshape(b, m, r * n_dil)
    cq = lambda c: c * ncol + 3 * g
    cur = lambda off: pl.BlockSpec((1, rows, DIL_W), lambda bi, c, i: (bi, i, cq(c) + off))
    prev = lambda off: pl.BlockSpec((1, SW_BLOCK, DIL_W),
                                    lambda bi, c, i: (bi, jnp.maximum(i * nbk - 1, 0), cq(c) + off))
    out_spec = pl.BlockSpec((1, rows, DIL_W), lambda bi, c, i: (bi, i, c))
    o, l = pl.pallas_call(
        functools.partial(_sw_kernel, nbk=nbk),
        grid=(b, r, m // rows),
        in_specs=[cur(0), cur(1), cur(2), prev(1), prev(2)],
        out_specs=[out_spec, out_spec],
        out_shape=[jax.ShapeDtypeStruct((b, m, r * DIL_W), F32)] * 2,
        compiler_params=_cparams(("parallel", "parallel", "arbitrary")),
        name=f"sliding_window_g{g}",
    )(view, view, view, view, view)
    return o.reshape(b, s, DIL_W), l.reshape(b, s, DIL_W)


def _even_out_kernel(x_ref, g_ref, oa_ref, o0_ref, o1_ref, o2_ref, l0_ref, l1_ref, l2_ref, w_ref, y_ref):
    l0, l1, l2 = l0_ref[0], l1_ref[0], l2_ref[0]
    top = jnp.maximum(jnp.maximum(l0, l1), l2)
    w0, w1, w2 = jnp.exp2(l0 - top), jnp.exp2(l1 - top), jnp.exp2(l2 - top)
    ob = (w0 * o0_ref[0] + w1 * o1_ref[0] + w2 * o2_ref[0]) / (w0 + w1 + w2)
    na = oa_ref.shape[-1]
    y = (jnp.dot(oa_ref[0], w_ref[:na, :], preferred_element_type=F32)
         + jnp.dot(ob.astype(BF16), w_ref[na:, :], preferred_element_type=F32))
    y_ref[0] = x_ref[0] + g_ref[0] * y


def _even_out(x, g1, oa, o_dil, l_dil, w_out, tm):
    b, s, d = x.shape
    tok = lambda w: pl.BlockSpec((1, tm, w), lambda bi, i: (bi, i, 0))
    return pl.pallas_call(
        _even_out_kernel,
        grid=(b, s // tm),
        in_specs=[tok(d), pl.BlockSpec((1, 1, d), lambda bi, i: (bi, 0, 0)), tok(oa.shape[-1])]
                 + [tok(DIL_W)] * 6 + [pl.BlockSpec(w_out.shape, lambda bi, i: (0, 0))],
        out_specs=tok(d),
        out_shape=jax.ShapeDtypeStruct((b, s, d), F32),
        compiler_params=_cparams(("parallel", "arbitrary")),
        name="even_out_proj",
    )(x, g1, oa, *o_dil, *l_dil, w_out)


def _odd_out_kernel(x_ref, g_ref, oc_ref, od_ref, w_ref, y_ref):
    nc = oc_ref.shape[-1]
    y = (jnp.dot(oc_ref[0], w_ref[:nc, :], preferred_element_type=F32)
         + jnp.dot(od_ref[0], w_ref[nc:, :], preferred_element_type=F32))
    y_ref[0] = x_ref[0] + g_ref[0] * y


def _odd_out(x, g1, oc, od, w_out, tm):
    b, s, d = x.shape
    tok = lambda w: pl.BlockSpec((1, tm, w), lambda bi, i: (bi, i, 0))
    return pl.pallas_call(
        _odd_out_kernel,
        grid=(b, s // tm),
        in_specs=[tok(d), pl.BlockSpec((1, 1, d), lambda bi, i: (bi, 0, 0)), tok(oc.shape[-1]), tok(od.shape[-1]),
                  pl.BlockSpec(w_out.shape, lambda bi, i: (0, 0))],
        out_specs=tok(d),
        out_shape=jax.ShapeDtypeStruct((b, s, d), F32),
        compiler_params=_cparams(("parallel", "arbitrary")),
        name="odd_out_proj",
    )(x, g1, oc, od, w_out)


def _mlp_kernel(x_ref, sh_ref, sc_ref, g_ref, nw_ref, w1_ref, w2_ref, y_ref, *, tf):
    x = x_ref[0]
    h = (_rms(x, nw_ref[...]) * (1.0 + sc_ref[0]) + sh_ref[0]).astype(BF16)
    acc = jnp.zeros(x.shape, F32)
    for c in range(w1_ref.shape[1] // tf):
        a = jnp.maximum(jnp.dot(h, w1_ref[:, tf * c:tf * (c + 1)], preferred_element_type=F32), 0.0)
        acc = acc + jnp.dot((a * a).astype(BF16), w2_ref[tf * c:tf * (c + 1), :], preferred_element_type=F32)
    y_ref[0] = x + g_ref[0] * acc


def _mlp(x, sh, sc, g2, nw, w1, w2, tm, tf):
    b, s, d = x.shape
    tok = pl.BlockSpec((1, tm, d), lambda bi, i: (bi, i, 0))
    per_b = pl.BlockSpec((1, 1, d), lambda bi, i: (bi, 0, 0))
    const = lambda a: pl.BlockSpec(a.shape, lambda bi, i: (0, 0), pipeline_mode=pl.Buffered(1))
    return pl.pallas_call(
        functools.partial(_mlp_kernel, tf=tf),
        grid=(b, s // tm),
        in_specs=[tok, per_b, per_b, per_b, pl.BlockSpec(nw.shape, lambda bi, i: (0, 0)), const(w1), const(w2)],
        out_specs=tok,
        out_shape=jax.ShapeDtypeStruct((b, s, d), F32),
        compiler_params=_cparams(("parallel", "arbitrary")),
        name="mlp",
    )(x, sh, sc, g2, nw, w1, w2)


def _rope_tables(positions):
    pos = positions.astype(F32)[:, :, None]

    def cs(half):
        inv = ROPE_THETA ** (-jnp.arange(half, dtype=F32) / half)
        ang = pos * inv
        return jnp.cos(ang), jnp.sin(ang)

    c, s = cs(HEAD_DIM // 2)
    c64 = jnp.concatenate([c, c, c, c], axis=-1)
    s64 = jnp.concatenate([-s, s, -s, s], axis=-1)
    c, s = cs(MLA_ROPE // 2)
    ones = jnp.ones(pos.shape[:2] + (MLA_NOPE,), F32)
    pad = LANES - MLA_QK
    cm = jnp.concatenate([ones, c, c, jnp.ones(pos.shape[:2] + (pad,), F32)], axis=-1)
    sm = jnp.concatenate([0 * ones, -s, s, jnp.zeros(pos.shape[:2] + (pad,), F32)], axis=-1)
    return cm, sm, c64, s64


def _even_weights(w_in, w_uq, w_ukv, qn, kn):
    o2 = MLA_Q_RANK + MLA_KV_RANK
    o3 = o2 + MLA_ROPE
    d = w_in.shape[0]
    dil = w_in[:, o3:].reshape(d, 3, len(DIL_CONFIGS), DIL_W).transpose(0, 2, 1, 3).reshape(d, -1)
    win = jnp.concatenate([w_in[:, :o2], w_in[:, o2:o3], jnp.zeros((d, LANES - MLA_ROPE), w_in.dtype), dil], axis=1)
    padq = ((0, 0), (0, 0), (0, LANES - MLA_QK))
    wuq = jnp.pad(w_uq, padq).reshape(MLA_Q_RANK, MLA_HEADS * LANES)
    wk = jnp.pad(w_ukv[:, :, :MLA_NOPE], ((0, 0), (0, 0), (0, LANES - MLA_NOPE))).reshape(MLA_KV_RANK, -1)
    wv = w_ukv[:, :, MLA_NOPE:].reshape(MLA_KV_RANK, -1)
    wukv = jnp.concatenate([wk, wv], axis=1)
    pad1 = (0, LANES - MLA_QK)
    return win.astype(BF16), wuq.astype(BF16), wukv.astype(BF16), _row(jnp.pad(qn, pad1)), _row(jnp.pad(kn, pad1))


def _tile2(v):
    return _row(jnp.concatenate([v, v]))


def kernel(x, c, positions, ada_w, ada_b, norm_mix, norm_mlp, mlp_w1, mlp_w2, even_w_in, even_w_out, mla_q_lat_norm, mla_kv_lat_norm, mla_w_uq, mla_w_ukv, mla_q_norm, mla_k_norm, dil_q_norm, dil_k_norm, odd_w_in, odd_w_out, diff_q_norm, diff_k_norm, diff_lambda, diff_subln, moba_q_norm, moba_k_norm):
    b, s, d = x.shape
    depth = ada_w.shape[0]
    tm_in, tm_out, tm_mlp, tf = 256, 512, 512, 1024
    tq_dense = 512

    mod = _adaln(c, ada_w, ada_b)
    cm, sm, c64, s64 = _rope_tables(positions)

    for layer in range(depth):
        sh1, sc1, g1, sh2, sc2, g2 = [mod[layer, :, d * t:d * (t + 1)].reshape(b, 1, d) for t in range(6)]
        i = layer // 2
        if layer % 2 == 0:
            win, wuq, wukv, qn, kn = _even_weights(even_w_in[i], mla_w_uq[i], mla_w_ukv[i],
                                                   mla_q_norm[i], mla_k_norm[i])
            q, k, v, dil = _even_in(x, sh1, sc1, _row(norm_mix[layer]), win, wuq, wukv,
                                    _row(mla_q_lat_norm[i]), _row(mla_kv_lat_norm[i]), qn, kn,
                                    _tile2(dil_q_norm[i]), _tile2(dil_k_norm[i]), cm, sm, c64, s64, tm_in)
            o_a = _mla_attention(q, k, v, tq_dense)
            o_dil, l_dil = zip(*[_sliding_window(dil, g, 4) for g in range(len(DIL_CONFIGS))])
            x = _even_out(x, g1, o_a, o_dil, l_dil, even_w_out[i].astype(BF16), tm_out)
        else:
            lam_init = 0.8 - 0.6 * math.exp(-0.3 * layer)
            qc, kc, vc, qm, qmf, km, vm, kmean = _odd_in(
                x, sh1, sc1, _row(norm_mix[layer]), odd_w_in[i].astype(BF16),
                _tile2(diff_q_norm[i]), _tile2(diff_k_norm[i]), _tile2(moba_q_norm[i]), _tile2(moba_k_norm[i]),
                c64, s64, tm_in)
            o_c = _diff_attention(qc, kc, vc, diff_lambda[i].astype(F32), _row(diff_subln[i]), lam_init, tq_dense)
            o_d = _moba_attention(qm, qmf, km, vm, kmean.reshape(b, s // MOBA_BLOCK, kmean.shape[-1]))
            x = _odd_out(x, g1, o_c, o_d, odd_w_out[i].astype(BF16), tm_out)
        x = _mlp(x, sh2, sc2, g2, _row(norm_mlp[layer]), mlp_w1[layer].astype(BF16), mlp_w2[layer].astype(BF16),
                 tm_mlp, tf)
    return x
```

```python
import functools
import math

import jax
import jax.numpy as jnp
from jax import lax
from jax.experimental import pallas as pl
from jax.experimental.pallas import tpu as pltpu

F32 = jnp.float32
BF16 = jnp.bfloat16

LANES = 128
HEAD_DIM = 64
ROPE_THETA = 10000.0
NORM_EPS = 1e-6
NEG_INF = -1e30
LOG2E = math.log2(math.e)

MLA_HEADS = 8
MLA_Q_RANK = 384
MLA_KV_RANK = 256
MLA_NOPE = 64
MLA_ROPE = 32
MLA_QK = MLA_NOPE + MLA_ROPE
DIL_CONFIGS = ((128, 1), (512, 4), (2048, 16))
DIL_HEADS = 4
DIL_W = DIL_HEADS * HEAD_DIM
DIFF_HEADS = 4
MOBA_HEADS = 8
MOBA_BLOCK = 256
MOBA_TOPK = 3
SW_BLOCK = 128
ONES_ROWS = 16

VMEM_LIMIT = 56 * 1024 * 1024


def _cparams(sem):
    return pltpu.CompilerParams(dimension_semantics=sem, vmem_limit_bytes=VMEM_LIMIT)


def _nt_dot(a, b):
    return lax.dot_general(a, b, (((1,), (1,)), ((), ())), preferred_element_type=F32)


def _tn_dot(a, b):
    return lax.dot_general(a, b, (((0,), (0,)), ((), ())), preferred_element_type=F32)


def _rms(x, w):
    return x * lax.rsqrt(jnp.mean(x * x, axis=-1, keepdims=True) + NORM_EPS) * w


def _lane_iota(shape):
    return lax.broadcasted_iota(jnp.int32, shape, len(shape) - 1)


def _norm_cols64(x, gain):
    lane = _lane_iota((1, LANES))
    lo = lane < HEAD_DIM
    sq = x * x
    s_lo = jnp.sum(jnp.where(lo, sq, 0.0), axis=-1, keepdims=True)
    s_hi = jnp.sum(jnp.where(lo, 0.0, sq), axis=-1, keepdims=True)
    ms = jnp.where(lo, s_lo, s_hi) * (1.0 / HEAD_DIM)
    return x * lax.rsqrt(ms + NORM_EPS) * gain


def _rope_cols64(x, cos, sin):
    lane = _lane_iota((1, LANES))
    first = (lane % HEAD_DIM) < (HEAD_DIM // 2)
    fwd = pltpu.roll(x, LANES - HEAD_DIM // 2, 1)
    bwd = pltpu.roll(x, HEAD_DIM // 2, 1)
    return x * cos + jnp.where(first, fwd, bwd) * sin


def _rope_mla(x, cos, sin):
    lane = _lane_iota((1, LANES))
    first = lane < (MLA_NOPE + MLA_ROPE // 2)
    fwd = pltpu.roll(x, LANES - MLA_ROPE // 2, 1)
    bwd = pltpu.roll(x, MLA_ROPE // 2, 1)
    return x * cos + jnp.where(first, fwd, bwd) * sin


def _adaln_kernel(c_ref, w_ref, b_ref, o_ref):
    c = c_ref[...]
    cond = c * (1.0 / (1.0 + jnp.exp(-c)))
    o_ref[0] = jnp.dot(cond, w_ref[0], preferred_element_type=F32,
                       precision=lax.Precision.HIGHEST) + b_ref[0]


def _adaln(c, ada_w, ada_b):
    depth, d, n = ada_w.shape
    b = c.shape[0]
    tn = 1024
    return pl.pallas_call(
        _adaln_kernel,
        grid=(depth, n // tn),
        in_specs=[pl.BlockSpec((b, d), lambda l, j: (0, 0)),
                  pl.BlockSpec((1, d, tn), lambda l, j: (l, 0, j)),
                  pl.BlockSpec((1, 1, tn), lambda l, j: (l, 0, j))],
        out_specs=pl.BlockSpec((1, b, tn), lambda l, j: (l, 0, j)),
        out_shape=jax.ShapeDtypeStruct((depth, b, n), F32),
        compiler_params=_cparams(("parallel", "parallel")),
        name="adaln",
    )(c, ada_w, ada_b.reshape(depth, 1, n))


def _even_in_kernel(x_ref, sh_ref, sc_ref, nw_ref, win_ref, wuq_ref, wuk_ref, wvt_ref,
                    qlat_ref, kvlat_ref, qn_ref, kn_ref, dqn_ref, dkn_ref,
                    cm_ref, sm_ref, c64_ref, s64_ref,
                    q_out, k_out, vt_out, dil_out):
    x = x_ref[0]
    h = _rms(x, nw_ref[...]) * (1.0 + sc_ref[0]) + sh_ref[0]
    u = jnp.dot(h.astype(BF16), win_ref[...], preferred_element_type=F32)

    o1 = MLA_Q_RANK
    o2 = o1 + MLA_KV_RANK
    o3 = o2 + LANES
    cqn = _rms(u[:, :o1], qlat_ref[...]).astype(BF16)
    qp = jnp.dot(cqn, wuq_ref[...], preferred_element_type=F32)
    ckvn = _rms(u[:, o1:o2], kvlat_ref[...]).astype(BF16)
    kvp = jnp.dot(ckvn, wuk_ref[...], preferred_element_type=F32)
    vt_out[0, 0] = _nt_dot(wvt_ref[...], ckvn).astype(BF16)
    kr = pltpu.roll(u[:, o2:o3], MLA_NOPE, 1)

    cm, sm = cm_ref[0], sm_ref[0]
    q_scale = MLA_QK ** -0.5 * LOG2E
    for hd in range(MLA_HEADS):
        sl = slice(LANES * hd, LANES * (hd + 1))
        qh = qp[:, sl]
        qh = qh * lax.rsqrt(jnp.sum(qh * qh, -1, keepdims=True) * (1.0 / MLA_QK) + NORM_EPS) * qn_ref[...]
        q_out[0, :, sl] = (_rope_mla(qh, cm, sm) * q_scale).astype(BF16)
        kh = kvp[:, sl] + kr
        kh = kh * lax.rsqrt(jnp.sum(kh * kh, -1, keepdims=True) * (1.0 / MLA_QK) + NORM_EPS) * kn_ref[...]
        k_out[0, :, sl] = _rope_mla(kh, cm, sm).astype(BF16)

    c64, s64 = c64_ref[0], s64_ref[0]
    d_scale = HEAD_DIM ** -0.5 * LOG2E
    for g in range(len(DIL_CONFIGS)):
        base = o3 + 3 * DIL_W * g
        for col in range(DIL_W // LANES):
            a = base + LANES * col
            qc = _rope_cols64(_norm_cols64(u[:, a:a + LANES], dqn_ref[...]), c64, s64) * d_scale
            dil_out[0, :, a - o3:a - o3 + LANES] = qc.astype(BF16)
            a = base + DIL_W + LANES * col
            kc = _rope_cols64(_norm_cols64(u[:, a:a + LANES], dkn_ref[...]), c64, s64)
            dil_out[0, :, a - o3:a - o3 + LANES] = kc.astype(BF16)
        a = base + 2 * DIL_W
        dil_out[0, :, a - o3:a - o3 + DIL_W] = u[:, a:a + DIL_W].astype(BF16)


def _row(v):
    return v.reshape(1, -1).astype(F32)


def _even_in(x, sh, sc, nw, win, wuq, wuk, wvt, qlat, kvlat, qn, kn, dqn, dkn, cm, sm, c64, s64, tm):
    b, s, d = x.shape
    n_dil = 3 * DIL_W * len(DIL_CONFIGS)
    nv = wvt.shape[0]
    tok = lambda w: pl.BlockSpec((1, tm, w), lambda bi, i: (bi, i, 0))
    per_b = pl.BlockSpec((1, 1, d), lambda bi, i: (bi, 0, 0))
    full = lambda a: pl.BlockSpec(a.shape, lambda bi, i: (0,) * a.ndim)
    return pl.pallas_call(
        _even_in_kernel,
        grid=(b, s // tm),
        in_specs=[tok(d), per_b, per_b, full(nw), full(win), full(wuq), full(wuk), full(wvt),
                  full(qlat), full(kvlat), full(qn), full(kn), full(dqn), full(dkn),
                  tok(LANES), tok(LANES), tok(LANES), tok(LANES)],
        out_specs=[tok(MLA_HEADS * LANES), tok(MLA_HEADS * LANES),
                   pl.BlockSpec((1, 1, nv, tm), lambda bi, i: (bi, i, 0, 0)), tok(n_dil)],
        out_shape=[jax.ShapeDtypeStruct((b, s, MLA_HEADS * LANES), BF16),
                   jax.ShapeDtypeStruct((b, s, MLA_HEADS * LANES), BF16),
                   jax.ShapeDtypeStruct((b, s // tm, nv, tm), BF16),
                   jax.ShapeDtypeStruct((b, s, n_dil), BF16)],
        compiler_params=_cparams(("parallel", "arbitrary")),
        name="even_in_proj",
    )(x, sh, sc, nw, win, wuq, wuk, wvt, qlat, kvlat, qn, kn, dqn, dkn, cm, sm, c64, s64)


def _odd_in_kernel(x_ref, sh_ref, sc_ref, nw_ref, win_ref, wvt_ref, dqn_ref, dkn_ref, mqn_ref, mkn_ref,
                   c64_ref, s64_ref,
                   qc_out, kc_out, qm_out, qmf_out, km_out, vct_out, vmt_out, kmean_out):
    x = x_ref[0]
    tm = x.shape[0]
    h = (_rms(x, nw_ref[...]) * (1.0 + sc_ref[0]) + sh_ref[0]).astype(BF16)
    u = jnp.dot(h, win_ref[...], preferred_element_type=F32)
    vt = _nt_dot(wvt_ref[...], h).astype(BF16)
    nvc = vct_out.shape[2]
    vct_out[0, 0] = vt[:nvc]
    vmt_out[0, 0] = vt[nvc:]
    c64, s64 = c64_ref[0], s64_ref[0]
    scale = HEAD_DIM ** -0.5 * LOG2E
    nqk = DIFF_HEADS * LANES
    for col in range(nqk // LANES):
        sl = slice(LANES * col, LANES * (col + 1))
        qc = _rope_cols64(_norm_cols64(u[:, sl], dqn_ref[...]), c64, s64) * scale
        qc_out[0, :, sl] = qc.astype(BF16)
        kc = _rope_cols64(_norm_cols64(u[:, nqk + LANES * col:nqk + LANES * (col + 1)], dkn_ref[...]), c64, s64)
        kc_out[0, :, sl] = kc.astype(BF16)

    nm = MOBA_HEADS * HEAD_DIM
    base = 2 * nqk
    for col in range(nm // LANES):
        sl = slice(LANES * col, LANES * (col + 1))
        qm = _rope_cols64(_norm_cols64(u[:, base + LANES * col:base + LANES * (col + 1)], mqn_ref[...]), c64, s64)
        qmf_out[0, :, sl] = qm
        qm_out[0, :, sl] = (qm * scale).astype(BF16)
        a = base + nm + LANES * col
        km = _rope_cols64(_norm_cols64(u[:, a:a + LANES], mkn_ref[...]), c64, s64)
        km_out[0, :, sl] = km.astype(BF16)
        for blk in range(tm // MOBA_BLOCK):
            rows = slice(MOBA_BLOCK * blk, MOBA_BLOCK * (blk + 1))
            kmean_out[0, 0, blk:blk + 1, sl] = jnp.mean(km[rows], axis=0, keepdims=True)


def _odd_in(x, sh, sc, nw, win, wvt, dqn, dkn, mqn, mkn, c64, s64, tm):
    b, s, d = x.shape
    tok = lambda w: pl.BlockSpec((1, tm, w), lambda bi, i: (bi, i, 0))
    per_b = pl.BlockSpec((1, 1, d), lambda bi, i: (bi, 0, 0))
    full = lambda a: pl.BlockSpec(a.shape, lambda bi, i: (0,) * a.ndim)
    w = DIFF_HEADS * LANES
    nv = wvt.shape[0] // 2
    nblk = tm // MOBA_BLOCK
    shp = lambda dt: jax.ShapeDtypeStruct((b, s, w), dt)
    vt_spec = pl.BlockSpec((1, 1, nv, tm), lambda bi, i: (bi, i, 0, 0))
    vt_shape = jax.ShapeDtypeStruct((b, s // tm, nv, tm), BF16)
    return pl.pallas_call(
        _odd_in_kernel,
        grid=(b, s // tm),
        in_specs=[tok(d), per_b, per_b, full(nw), full(win), full(wvt), full(dqn), full(dkn), full(mqn), full(mkn),
                  tok(LANES), tok(LANES)],
        out_specs=[tok(w)] * 5 + [vt_spec, vt_spec, pl.BlockSpec((1, 1, nblk, w), lambda bi, i: (bi, i, 0, 0))],
        out_shape=[shp(BF16), shp(BF16), shp(BF16), shp(F32), shp(BF16), vt_shape, vt_shape,
                   jax.ShapeDtypeStruct((b, s // tm, nblk, w), F32)],
        compiler_params=_cparams(("parallel", "arbitrary")),
        name="odd_in_proj",
    )(x, sh, sc, nw, win, wvt, dqn, dkn, mqn, mkn, c64, s64)


def _flash_t(qs, k_ref, vt_ref, k_slices, v_rows, scr, i, t, past_mask=None):
    m_scr, acc_scr, s_scr, p_scr, a_scr = scr
    n = len(qs)
    ones = jnp.ones((ONES_ROWS, t), BF16)

    def qk(blk, s):
        b0 = pl.multiple_of(blk * t, t)
        return _nt_dot(k_ref[0, pl.ds(b0, t), k_slices[s]], qs[s])

    def pv(blk, s, p):
        vt = jnp.concatenate([vt_ref[0, blk, v_rows[s], :], ones], axis=0)
        return jnp.dot(vt, p, preferred_element_type=F32)

    krow = lax.broadcasted_iota(jnp.int32, (t, t), 0)
    qcol = lax.broadcasted_iota(jnp.int32, (t, t), 1)
    causal = krow <= qcol
    for s in range(n):
        st = jnp.where(causal, qk(i, s), NEG_INF)
        m = jnp.max(st, axis=0, keepdims=True)
        m_scr[s] = m
        p_scr[s] = jnp.exp2(st - m).astype(BF16)
        a_scr[s] = jnp.ones_like(m)
        acc_scr[s] = jnp.zeros(acc_scr.shape[1:], F32)
        s_scr[s] = qk(0, s)

    def body(jj, prev_blk):
        nxt = jnp.minimum(jj + 1, jnp.maximum(i - 1, 0))
        p_prev = [p_scr[s] for s in range(n)]
        a_prev = [a_scr[s] for s in range(n)]
        s_next = [qk(nxt, s) for s in range(n)]
        pvs = [pv(prev_blk, s, p_prev[s]) for s in range(n)]
        for s in range(n):
            st = s_scr[s]
            if past_mask is not None:
                st = jnp.where(past_mask(s, jj), st, NEG_INF)
            m_prev = m_scr[s]
            m_new = jnp.maximum(m_prev, jnp.max(st, axis=0, keepdims=True))
            a_scr[s] = jnp.exp2(m_prev - m_new)
            p_scr[s] = jnp.exp2(st - m_new).astype(BF16)
            m_scr[s] = m_new
        for s in range(n):
            s_scr[s] = s_next[s]
        for s in range(n):
            acc_scr[s] = a_prev[s] * acc_scr[s] + pvs[s]
        return jj

    last = lax.fori_loop(0, i, body, i)
    for s in range(n):
        acc_scr[s] = a_scr[s] * acc_scr[s] + pv(last, s, p_scr[s])


def _flash_scratch(n, nv, t):
    return [pltpu.VMEM((n, 1, t), F32), pltpu.VMEM((n, nv + ONES_ROWS, t), F32), pltpu.VMEM((n, t, t), F32),
            pltpu.VMEM((n, t, t), BF16), pltpu.VMEM((n, 1, t), F32)]


def _normalised(acc_scr, s, nv):
    return acc_scr[s, :nv, :] / acc_scr[s, nv:nv + 1, :]


def _half_mask(x, half):
    lane = _lane_iota((1, LANES))
    return jnp.where((lane >= HEAD_DIM) == bool(half), x, jnp.zeros_like(x))


def _mla_kernel(q_ref, k_ref, vt_ref, ot_ref, *scr, t, nh):
    i = pl.program_id(2)
    qs = [q_ref[0, :, LANES * s:LANES * (s + 1)] for s in range(nh)]
    ks = [slice(LANES * s, LANES * (s + 1)) for s in range(nh)]
    vr = [slice(HEAD_DIM * s, HEAD_DIM * (s + 1)) for s in range(nh)]
    _flash_t(qs, k_ref, vt_ref, ks, vr, scr, i, t)
    for s in range(nh):
        ot_ref[0, 0, vr[s], :] = _normalised(scr[1], s, HEAD_DIM).astype(ot_ref.dtype)


def _mla_attention(q, k, vt, nh):
    b, s, _ = q.shape
    _, nb, nv, t = vt.shape
    return pl.pallas_call(
        functools.partial(_mla_kernel, t=t, nh=nh),
        grid=(b, MLA_HEADS // nh, nb),
        in_specs=[pl.BlockSpec((1, t, nh * LANES), lambda bi, p, i: (bi, i, p)),
                  pl.BlockSpec((1, s, nh * LANES), lambda bi, p, i: (bi, 0, p)),
                  pl.BlockSpec((1, nb, nh * HEAD_DIM, t), lambda bi, p, i: (bi, 0, p, 0))],
        out_specs=pl.BlockSpec((1, 1, nh * HEAD_DIM, t), lambda bi, p, i: (bi, i, p, 0)),
        out_shape=jax.ShapeDtypeStruct((b, nb, nv, t), BF16),
        scratch_shapes=_flash_scratch(nh, HEAD_DIM, t),
        compiler_params=_cparams(("parallel", "parallel", "arbitrary")),
        name="mla_attention",
    )(q, k, vt)


def _diff_kernel(q_ref, k_ref, vt_ref, lam_ref, sub_ref, ot_ref, *scr, t, nh, lam_init):
    i = pl.program_id(2)
    qs, ks, vr = [], [], []
    for hd in range(nh):
        cols = slice(LANES * hd, LANES * (hd + 1))
        q = q_ref[0, :, cols]
        for half in range(2):
            qs.append(_half_mask(q, half))
            ks.append(cols)
            vr.append(cols)
    _flash_t(qs, k_ref, vt_ref, ks, vr, scr, i, t)
    lv = lam_ref[...]
    lam = (jnp.exp(jnp.sum(lv[0:1] * lv[1:2], keepdims=True))
           - jnp.exp(jnp.sum(lv[2:3] * lv[3:4], keepdims=True)) + lam_init)
    for hd in range(nh):
        o = _normalised(scr[1], 2 * hd, LANES) - lam * _normalised(scr[1], 2 * hd + 1, LANES)
        o = o * lax.rsqrt(jnp.mean(o * o, axis=0, keepdims=True) + NORM_EPS) * sub_ref[...]
        ot_ref[0, 0, LANES * hd:LANES * (hd + 1), :] = (o * (1.0 - lam_init)).astype(ot_ref.dtype)


def _diff_attention(q, k, vt, lam_rows, subln_col, lam_init, nh):
    b, s, w = q.shape
    _, nb, _, t = vt.shape
    return pl.pallas_call(
        functools.partial(_diff_kernel, t=t, nh=nh, lam_init=lam_init),
        grid=(b, DIFF_HEADS // nh, nb),
        in_specs=[pl.BlockSpec((1, t, nh * LANES), lambda bi, p, i: (bi, i, p)),
                  pl.BlockSpec((1, s, nh * LANES), lambda bi, p, i: (bi, 0, p)),
                  pl.BlockSpec((1, nb, nh * LANES, t), lambda bi, p, i: (bi, 0, p, 0)),
                  pl.BlockSpec(lam_rows.shape, lambda bi, p, i: (0, 0)),
                  pl.BlockSpec(subln_col.shape, lambda bi, p, i: (0, 0))],
        out_specs=pl.BlockSpec((1, 1, nh * LANES, t), lambda bi, p, i: (bi, i, p, 0)),
        out_shape=jax.ShapeDtypeStruct((b, nb, w, t), BF16),
        scratch_shapes=_flash_scratch(2 * nh, LANES, t),
        compiler_params=_cparams(("parallel", "parallel", "arbitrary")),
        name="diff_attention",
    )(q, k, vt, lam_rows, subln_col)


def _moba_kernel(q_ref, qf_ref, k_ref, vt_ref, kmean_ref, ot_ref, sel_scr, *scr, t, npair):
    i = pl.program_id(2)
    nb = kmean_ref.shape[1]
    brow = lax.broadcasted_iota(jnp.int32, (nb, t), 0).astype(F32)
    past = brow < i.astype(F32)
    qs, ks, vr = [], [], []
    for pr in range(npair):
        cols = slice(LANES * pr, LANES * (pr + 1))
        q, qf, kmean = q_ref[0, :, cols], qf_ref[0, :, cols], kmean_ref[0, :, cols]
        for half in range(2):
            s = 2 * pr + half
            gate = lax.dot_general(kmean, _half_mask(qf, half), (((1,), (1,)), ((), ())),
                                   preferred_element_type=F32, precision=lax.Precision.HIGHEST)
            gate = jnp.where(past, gate, NEG_INF)
            sel = jnp.zeros((nb, t), F32)
            for _ in range(MOBA_TOPK):
                top = jnp.max(gate, axis=0, keepdims=True)
                first = jnp.min(jnp.where(gate == top, brow, float(nb)), axis=0, keepdims=True)
                pick = brow == first
                sel = jnp.where(pick, 1.0, sel)
                gate = jnp.where(pick, 2 * NEG_INF, gate)
            sel_scr[s] = jnp.where(past, sel, 0.0)
            qs.append(_half_mask(q, half))
            ks.append(cols)
            vr.append(slice(HEAD_DIM * s, HEAD_DIM * (s + 1)))
    _flash_t(qs, k_ref, vt_ref, ks, vr, scr, i, t,
             past_mask=lambda s, jj: sel_scr[s, pl.ds(jj, 1), :] > 0.5)
    for s in range(2 * npair):
        ot_ref[0, 0, vr[s], :] = _normalised(scr[1], s, HEAD_DIM).astype(ot_ref.dtype)


def _moba_attention(q, qf, k, vt, kmean, npair):
    b, s, w = q.shape
    _, nb, _, t = vt.shape
    nkm = kmean.shape[1]
    lanes = npair * LANES
    return pl.pallas_call(
        functools.partial(_moba_kernel, t=t, npair=npair),
        grid=(b, w // lanes, nb),
        in_specs=[pl.BlockSpec((1, t, lanes), lambda bi, p, i: (bi, i, p)),
                  pl.BlockSpec((1, t, lanes), lambda bi, p, i: (bi, i, p)),
                  pl.BlockSpec((1, s, lanes), lambda bi, p, i: (bi, 0, p)),
                  pl.BlockSpec((1, nb, lanes, t), lambda bi, p, i: (bi, 0, p, 0)),
                  pl.BlockSpec((1, nkm, lanes), lambda bi, p, i: (bi, 0, p))],
        out_specs=pl.BlockSpec((1, 1, lanes, t), lambda bi, p, i: (bi, i, p, 0)),
        out_shape=jax.ShapeDtypeStruct((b, nb, w, t), BF16),
        scratch_shapes=[pltpu.VMEM((2 * npair, nkm, t), F32)] + _flash_scratch(2 * npair, HEAD_DIM, t),
        compiler_params=_cparams(("parallel", "parallel", "arbitrary")),
        name="moba_attention",
    )(q, qf, k, vt, kmean)


def _sw_kernel(q_ref, kc_ref, vc_ref, kp_ref, vp_ref, o_ref, l_ref, *, nbk):
    i = pl.program_id(2)
    blk = SW_BLOCK
    qi = lax.broadcasted_iota(jnp.int32, (blk, 2 * blk), 0)
    kj = lax.broadcasted_iota(jnp.int32, (blk, 2 * blk), 1)
    band = (kj >= qi) & (kj <= qi + blk)
    lane = _lane_iota((1, LANES))
    for n in range(nbk):
        rows = slice(blk * n, blk * (n + 1))
        q = q_ref[0, rows, :]
        if n == 0:
            kprev, vprev = kp_ref[0], vp_ref[0]
            valid = band & ((kj >= blk) | (i > 0))
        else:
            prev = slice(blk * (n - 1), blk * n)
            kprev, vprev = kc_ref[0, prev, :], vc_ref[0, prev, :]
            valid = band
        kk = jnp.concatenate([kprev, kc_ref[0, rows, :]], axis=0)
        vv = jnp.concatenate([vprev, vc_ref[0, rows, :]], axis=0)
        for pair in range(DIL_W // LANES):
            sl = slice(LANES * pair, LANES * (pair + 1))
            outs, lses = [], []
            for half in range(2):
                sc = jnp.where(valid, _nt_dot(_half_mask(q[:, sl], half), kk[:, sl]), NEG_INF)
                m = jnp.max(sc, axis=-1, keepdims=True)
                p = jnp.exp2(sc - m)
                l = jnp.sum(p, axis=-1, keepdims=True)
                outs.append(jnp.dot(p.astype(BF16), vv[:, sl], preferred_element_type=F32) / l)
                lses.append(m + jnp.log2(l))
            o_ref[0, rows, sl] = jnp.where(lane < HEAD_DIM, outs[0], outs[1])
            l_ref[0, rows, sl] = jnp.where(lane < HEAD_DIM, lses[0], lses[1])


def _sliding_window(dil, g, nbk):
    b, s, n_dil = dil.shape
    _, r = DIL_CONFIGS[g]
    m = s // r
    nbk = min(nbk, m // SW_BLOCK)
    rows = SW_BLOCK * nbk
    ncol = n_dil // DIL_W
    view = dil.reshape(b, m, r * n_dil)
    cq = lambda c: c * ncol + 3 * g
    cur = lambda off: pl.BlockSpec((1, rows, DIL_W), lambda bi, c, i: (bi, i, cq(c) + off))
    prev = lambda off: pl.BlockSpec((1, SW_BLOCK, DIL_W),
                                    lambda bi, c, i: (bi, jnp.maximum(i * nbk - 1, 0), cq(c) + off))
    out_spec = pl.BlockSpec((1, rows, DIL_W), lambda bi, c, i: (bi, i, c))
    o, l = pl.pallas_call(
        functools.partial(_sw_kernel, nbk=nbk),
        grid=(b, r, m // rows),
        in_specs=[cur(0), cur(1), cur(2), prev(1), prev(2)],
        out_specs=[out_spec, out_spec],
        out_shape=[jax.ShapeDtypeStruct((b, m, r * DIL_W), F32)] * 2,
        compiler_params=_cparams(("parallel", "parallel", "arbitrary")),
        name=f"sliding_window_g{g}",
    )(view, view, view, view, view)
    return o.reshape(b, s, DIL_W), l.reshape(b, s, DIL_W)


def _even_out_kernel(x_ref, g_ref, oat_ref, o0_ref, o1_ref, o2_ref, l0_ref, l1_ref, l2_ref, w_ref, y_ref):
    l0, l1, l2 = l0_ref[0], l1_ref[0], l2_ref[0]
    top = jnp.maximum(jnp.maximum(l0, l1), l2)
    w0, w1, w2 = jnp.exp2(l0 - top), jnp.exp2(l1 - top), jnp.exp2(l2 - top)
    ob = (w0 * o0_ref[0] + w1 * o1_ref[0] + w2 * o2_ref[0]) / (w0 + w1 + w2)
    na = oat_ref.shape[2]
    y = _tn_dot(oat_ref[0, 0], w_ref[:na, :]) + jnp.dot(ob.astype(BF16), w_ref[na:, :], preferred_element_type=F32)
    y_ref[0] = x_ref[0] + g_ref[0] * y


def _t_spec(a):
    return pl.BlockSpec((1, 1) + a.shape[2:], lambda bi, i: (bi, i, 0, 0))


def _even_out(x, g1, oat, o_dil, l_dil, w_out):
    b, s, d = x.shape
    tm = oat.shape[-1]
    tok = lambda w: pl.BlockSpec((1, tm, w), lambda bi, i: (bi, i, 0))
    return pl.pallas_call(
        _even_out_kernel,
        grid=(b, s // tm),
        in_specs=[tok(d), pl.BlockSpec((1, 1, d), lambda bi, i: (bi, 0, 0)), _t_spec(oat)]
                 + [tok(DIL_W)] * 6 + [pl.BlockSpec(w_out.shape, lambda bi, i: (0, 0))],
        out_specs=tok(d),
        out_shape=jax.ShapeDtypeStruct((b, s, d), F32),
        compiler_params=_cparams(("parallel", "arbitrary")),
        name="even_out_proj",
    )(x, g1, oat, *o_dil, *l_dil, w_out)


def _odd_out_kernel(x_ref, g_ref, oct_ref, odt_ref, w_ref, y_ref):
    nc = oct_ref.shape[2]
    y = _tn_dot(oct_ref[0, 0], w_ref[:nc, :]) + _tn_dot(odt_ref[0, 0], w_ref[nc:, :])
    y_ref[0] = x_ref[0] + g_ref[0] * y


def _odd_out(x, g1, oct, odt, w_out):
    b, s, d = x.shape
    tm = oct.shape[-1]
    tok = lambda w: pl.BlockSpec((1, tm, w), lambda bi, i: (bi, i, 0))
    return pl.pallas_call(
        _odd_out_kernel,
        grid=(b, s // tm),
        in_specs=[tok(d), pl.BlockSpec((1, 1, d), lambda bi, i: (bi, 0, 0)), _t_spec(oct), _t_spec(odt),
                  pl.BlockSpec(w_out.shape, lambda bi, i: (0, 0))],
        out_specs=tok(d),
        out_shape=jax.ShapeDtypeStruct((b, s, d), F32),
        compiler_params=_cparams(("parallel", "arbitrary")),
        name="odd_out_proj",
    )(x, g1, oct, odt, w_out)


def _mlp_kernel(x_ref, sh_ref, sc_ref, g_ref, nw_ref, w1_ref, w2_ref, y_ref, *, tf):
    x = x_ref[0]
    h = (_rms(x, nw_ref[...]) * (1.0 + sc_ref[0]) + sh_ref[0]).astype(BF16)
    acc = jnp.zeros(x.shape, F32)
    for c in range(w1_ref.shape[1] // tf):
        a = jnp.maximum(jnp.dot(h, w1_ref[:, tf * c:tf * (c + 1)], preferred_element_type=F32), 0.0)
        acc = acc + jnp.dot((a * a).astype(BF16), w2_ref[tf * c:tf * (c + 1), :], preferred_element_type=F32)
    y_ref[0] = x + g_ref[0] * acc


def _mlp(x, sh, sc, g2, nw, w1, w2, tm, tf):
    b, s, d = x.shape
    tok = pl.BlockSpec((1, tm, d), lambda bi, i: (bi, i, 0))
    per_b = pl.BlockSpec((1, 1, d), lambda bi, i: (bi, 0, 0))
    const = lambda a: pl.BlockSpec(a.shape, lambda bi, i: (0, 0), pipeline_mode=pl.Buffered(1))
    return pl.pallas_call(
        functools.partial(_mlp_kernel, tf=tf),
        grid=(b, s // tm),
        in_specs=[tok, per_b, per_b, per_b, pl.BlockSpec(nw.shape, lambda bi, i: (0, 0)), const(w1), const(w2)],
        out_specs=tok,
        out_shape=jax.ShapeDtypeStruct((b, s, d), F32),
        compiler_params=_cparams(("parallel", "arbitrary")),
        name="mlp",
    )(x, sh, sc, g2, nw, w1, w2)


def _rope_tables(positions):
    pos = positions.astype(F32)[:, :, None]

    def cs(half):
        inv = ROPE_THETA ** (-jnp.arange(half, dtype=F32) / half)
        ang = pos * inv
        return jnp.cos(ang), jnp.sin(ang)

    c, s = cs(HEAD_DIM // 2)
    c64 = jnp.concatenate([c, c, c, c], axis=-1)
    s64 = jnp.concatenate([-s, s, -s, s], axis=-1)
    c, s = cs(MLA_ROPE // 2)
    ones = jnp.ones(pos.shape[:2] + (MLA_NOPE,), F32)
    pad = LANES - MLA_QK
    cm = jnp.concatenate([ones, c, c, jnp.ones(pos.shape[:2] + (pad,), F32)], axis=-1)
    sm = jnp.concatenate([0 * ones, -s, s, jnp.zeros(pos.shape[:2] + (pad,), F32)], axis=-1)
    return cm, sm, c64, s64


def _even_weights(w_in, w_uq, w_ukv, qn, kn):
    o2 = MLA_Q_RANK + MLA_KV_RANK
    o3 = o2 + MLA_ROPE
    d = w_in.shape[0]
    dil = w_in[:, o3:].reshape(d, 3, len(DIL_CONFIGS), DIL_W).transpose(0, 2, 1, 3).reshape(d, -1)
    win = jnp.concatenate([w_in[:, :o2], w_in[:, o2:o3], jnp.zeros((d, LANES - MLA_ROPE), w_in.dtype), dil], axis=1)
    padq = ((0, 0), (0, 0), (0, LANES - MLA_QK))
    wuq = jnp.pad(w_uq, padq).reshape(MLA_Q_RANK, MLA_HEADS * LANES)
    wuk = jnp.pad(w_ukv[:, :, :MLA_NOPE], ((0, 0), (0, 0), (0, LANES - MLA_NOPE))).reshape(MLA_KV_RANK, -1)
    wvt = w_ukv[:, :, MLA_NOPE:].reshape(MLA_KV_RANK, -1).T
    pad1 = (0, LANES - MLA_QK)
    return (win.astype(BF16), wuq.astype(BF16), wuk.astype(BF16), wvt.astype(BF16),
            _row(jnp.pad(qn, pad1)), _row(jnp.pad(kn, pad1)))


def _odd_weights(w_in):
    nqk = DIFF_HEADS * LANES
    nm = MOBA_HEADS * HEAD_DIM
    win = jnp.concatenate([w_in[:, :2 * nqk], w_in[:, 3 * nqk:3 * nqk + 2 * nm]], axis=1)
    wvt = jnp.concatenate([w_in[:, 2 * nqk:3 * nqk], w_in[:, 3 * nqk + 2 * nm:]], axis=1).T
    return win.astype(BF16), wvt.astype(BF16)


def _tile2(v):
    return _row(jnp.concatenate([v, v]))


def kernel(x, c, positions, ada_w, ada_b, norm_mix, norm_mlp, mlp_w1, mlp_w2, even_w_in, even_w_out, mla_q_lat_norm, mla_kv_lat_norm, mla_w_uq, mla_w_ukv, mla_q_norm, mla_k_norm, dil_q_norm, dil_k_norm, odd_w_in, odd_w_out, diff_q_norm, diff_k_norm, diff_lambda, diff_subln, moba_q_norm, moba_k_norm):
    b, s, d = x.shape
    depth = ada_w.shape[0]
    t_attn = MOBA_BLOCK
    tm_mlp, tf = 512, 1024

    mod = _adaln(c, ada_w, ada_b)
    cm, sm, c64, s64 = _rope_tables(positions)

    for layer in range(depth):
        sh1, sc1, g1, sh2, sc2, g2 = [mod[layer, :, d * t:d * (t + 1)].reshape(b, 1, d) for t in range(6)]
        i = layer // 2
        if layer % 2 == 0:
            win, wuq, wuk, wvt, qn, kn = _even_weights(even_w_in[i], mla_w_uq[i], mla_w_ukv[i],
                                                       mla_q_norm[i], mla_k_norm[i])
            q, k, vt, dil = _even_in(x, sh1, sc1, _row(norm_mix[layer]), win, wuq, wuk, wvt,
                                     _row(mla_q_lat_norm[i]), _row(mla_kv_lat_norm[i]), qn, kn,
                                     _tile2(dil_q_norm[i]), _tile2(dil_k_norm[i]), cm, sm, c64, s64, t_attn)
            o_at = _mla_attention(q, k, vt, nh=4)
            o_dil, l_dil = zip(*[_sliding_window(dil, g, 4) for g in range(len(DIL_CONFIGS))])
            x = _even_out(x, g1, o_at, o_dil, l_dil, even_w_out[i].astype(BF16))
        else:
            lam_init = 0.8 - 0.6 * math.exp(-0.3 * layer)
            win, wvt = _odd_weights(odd_w_in[i])
            qc, kc, qm, qmf, km, vct, vmt, kmean = _odd_in(
                x, sh1, sc1, _row(norm_mix[layer]), win, wvt,
                _tile2(diff_q_norm[i]), _tile2(diff_k_norm[i]), _tile2(moba_q_norm[i]), _tile2(moba_k_norm[i]),
                c64, s64, t_attn)
            o_ct = _diff_attention(qc, kc, vct, diff_lambda[i].astype(F32),
                                   diff_subln[i].reshape(-1, 1).astype(F32), lam_init, nh=2)
            o_dt = _moba_attention(qm, qmf, km, vmt, kmean.reshape(b, s // MOBA_BLOCK, kmean.shape[-1]), npair=2)
            x = _odd_out(x, g1, o_ct, o_dt, odd_w_out[i].astype(BF16))
        x = _mlp(x, sh2, sc2, g2, _row(norm_mlp[layer]), mlp_w1[layer].astype(BF16), mlp_w2[layer].astype(BF16),
                 tm_mlp, tf)
    return x
```

```python
import functools
import math

import jax
import jax.numpy as jnp
import numpy as np
from jax import lax
from jax.experimental import pallas as pl
from jax.experimental.pallas import tpu as pltpu

F32 = jnp.float32
BF16 = jnp.bfloat16

LANES = 128
HEAD_DIM = 64
ROPE_THETA = 10000.0
NORM_EPS = 1e-6
NEG_INF = -1e30
LOG2E = math.log2(math.e)

MLA_HEADS = 8
MLA_Q_RANK = 384
MLA_KV_RANK = 256
MLA_NOPE = 64
MLA_ROPE = 32
MLA_QK = MLA_NOPE + MLA_ROPE
DIL_CONFIGS = ((128, 1), (512, 4), (2048, 16))
DIL_HEADS = 4
DIL_W = DIL_HEADS * HEAD_DIM
DIFF_HEADS = 4
MOBA_HEADS = 8
MOBA_BLOCK = 256
MOBA_TOPK = 3
SW_BLOCK = 128
ONES_ROWS = 16

VMEM_LIMIT = 56 * 1024 * 1024


def _cparams(sem):
    return pltpu.CompilerParams(dimension_semantics=sem, vmem_limit_bytes=VMEM_LIMIT)


def _nt_dot(a, b):
    return lax.dot_general(a, b, (((1,), (1,)), ((), ())), preferred_element_type=F32)


def _tn_dot(a, b):
    return lax.dot_general(a, b, (((0,), (0,)), ((), ())), preferred_element_type=F32)


def _rms(x, w):
    return x * lax.rsqrt(jnp.mean(x * x, axis=-1, keepdims=True) + NORM_EPS) * w


def _lane_iota(shape):
    return lax.broadcasted_iota(jnp.int32, shape, len(shape) - 1)


def _adaln_kernel(c_ref, w_ref, b_ref, o_ref):
    c = c_ref[...]
    cond = c * (1.0 / (1.0 + jnp.exp(-c)))
    o_ref[0] = jnp.dot(cond, w_ref[0], preferred_element_type=F32,
                       precision=lax.Precision.HIGHEST) + b_ref[0]


def _adaln(c, ada_w, ada_b):
    depth, d, n = ada_w.shape
    b = c.shape[0]
    tn = 1024
    return pl.pallas_call(
        _adaln_kernel,
        grid=(depth, n // tn),
        in_specs=[pl.BlockSpec((b, d), lambda l, j: (0, 0)),
                  pl.BlockSpec((1, d, tn), lambda l, j: (l, 0, j)),
                  pl.BlockSpec((1, 1, tn), lambda l, j: (l, 0, j))],
        out_specs=pl.BlockSpec((1, b, tn), lambda l, j: (l, 0, j)),
        out_shape=jax.ShapeDtypeStruct((depth, b, n), F32),
        compiler_params=_cparams(("parallel", "parallel")),
        name="adaln",
    )(c, ada_w, ada_b.reshape(depth, 1, n))


def _inv_rms64(x):
    lane = _lane_iota((1, LANES))
    lo = lane < HEAD_DIM
    sq = x * x
    s_lo = jnp.sum(jnp.where(lo, sq, 0.0), axis=-1, keepdims=True)
    s_hi = jnp.sum(jnp.where(lo, 0.0, sq), axis=-1, keepdims=True)
    return lax.rsqrt(jnp.where(lo, s_lo, s_hi) * (1.0 / HEAD_DIM) + NORM_EPS)


def _even_in_kernel(x_ref, sh_ref, sc_ref, nw_ref, win_ref, wuq_ref, wuk_ref, wvt_ref,
                    qlat_ref, kvlat_ref, qn_ref, kn_ref, dqn_ref, dkn_ref,
                    cm_ref, sm_ref, c64_ref, s64_ref,
                    q_out, k_out, vt_out, d0_out, d1_out, d2_out, dscr):
    x = x_ref[0]
    tm = x.shape[0]
    h = _rms(x, nw_ref[...]) * (1.0 + sc_ref[0]) + sh_ref[0]
    u = jnp.dot(h.astype(BF16), win_ref[...], preferred_element_type=F32)

    o1 = MLA_Q_RANK
    o2 = o1 + MLA_KV_RANK
    o_kr, o_krr, o_dil = o2, o2 + LANES, o2 + 2 * LANES
    n_dil = 3 * DIL_W * len(DIL_CONFIGS)
    o_rot = o_dil + n_dil
    nq = MLA_HEADS * LANES
    cqn = _rms(u[:, :o1], qlat_ref[...]).astype(BF16)
    qp = jnp.dot(cqn, wuq_ref[...], preferred_element_type=F32)
    ckvn = _rms(u[:, o1:o2], kvlat_ref[...]).astype(BF16)
    kvp = jnp.dot(ckvn, wuk_ref[...], preferred_element_type=F32)
    vt_out[0, 0] = _nt_dot(wvt_ref[...], ckvn).astype(BF16)
    kr, kr_rot = u[:, o_kr:o_kr + LANES], u[:, o_krr:o_krr + LANES]

    cm, sm = cm_ref[0], sm_ref[0]
    q_scale = MLA_QK ** -0.5 * LOG2E
    qgc, qgs = qn_ref[0:1] * cm * q_scale, qn_ref[1:2] * sm * q_scale
    kgc, kgs = kn_ref[0:1] * cm, kn_ref[1:2] * sm
    kr_term = kr_rot * kgs
    for hd in range(MLA_HEADS):
        sl = slice(LANES * hd, LANES * (hd + 1))
        qh = qp[:, sl]
        inv = lax.rsqrt(jnp.sum(qh * qh, -1, keepdims=True) * (1.0 / MLA_QK) + NORM_EPS)
        q_out[0, :, sl] = (inv * (qh * qgc + qp[:, nq + LANES * hd:nq + LANES * (hd + 1)] * qgs)).astype(BF16)
        kh = kvp[:, sl] + kr
        inv = lax.rsqrt(jnp.sum(kh * kh, -1, keepdims=True) * (1.0 / MLA_QK) + NORM_EPS)
        k_out[0, :, sl] = (inv * (kh * kgc + kr_term)).astype(BF16)

    c64, s64 = c64_ref[0], s64_ref[0]
    d_scale = HEAD_DIM ** -0.5 * LOG2E
    dqc, dqs = dqn_ref[0:1] * c64 * d_scale, dqn_ref[1:2] * s64 * d_scale
    dkc, dks = dkn_ref[0:1] * c64, dkn_ref[1:2] * s64
    ncol = DIL_W // LANES
    for g, d_out in enumerate((d0_out, d1_out, d2_out)):
        _, r = DIL_CONFIGS[g]
        base = o_dil + 3 * DIL_W * g
        rbase = o_rot + 2 * DIL_W * g
        for j in range(3 * ncol):
            xc = u[:, base + LANES * j:base + LANES * (j + 1)]
            if j < 2 * ncol:
                xr = u[:, rbase + LANES * j:rbase + LANES * (j + 1)]
                gc, gs = (dqc, dqs) if j < ncol else (dkc, dks)
                xc = _inv_rms64(xc) * (xc * gc + xr * gs)
            if r == 1:
                d_out[0, :, LANES * j:LANES * (j + 1)] = xc.astype(BF16)
            else:
                dscr[j] = xc
                for c in range(r):
                    d_out[0, c, :, LANES * j:LANES * (j + 1)] = dscr[j, pl.ds(c, tm // r, stride=r), :].astype(BF16)


def _row(v):
    return v.reshape(1, -1).astype(F32)


def _even_in(x, sh, sc, nw, win, wuq, wuk, wvt, qlat, kvlat, qn, kn, dqn, dkn, cm, sm, c64, s64, tm):
    b, s, d = x.shape
    nv = wvt.shape[0]
    tok = lambda w: pl.BlockSpec((1, tm, w), lambda bi, i: (bi, i, 0))
    per_b = pl.BlockSpec((1, 1, d), lambda bi, i: (bi, 0, 0))
    full = lambda a: pl.BlockSpec(a.shape, lambda bi, i: (0,) * a.ndim)
    dil_specs, dil_shapes = [], []
    for _, r in DIL_CONFIGS:
        if r == 1:
            dil_specs.append(tok(3 * DIL_W))
            dil_shapes.append(jax.ShapeDtypeStruct((b, s, 3 * DIL_W), BF16))
        else:
            dil_specs.append(pl.BlockSpec((1, r, tm // r, 3 * DIL_W), lambda bi, i: (bi, 0, i, 0)))
            dil_shapes.append(jax.ShapeDtypeStruct((b, r, s // r, 3 * DIL_W), BF16))
    return pl.pallas_call(
        _even_in_kernel,
        grid=(b, s // tm),
        in_specs=[tok(d), per_b, per_b, full(nw), full(win), full(wuq), full(wuk), full(wvt),
                  full(qlat), full(kvlat), full(qn), full(kn), full(dqn), full(dkn),
                  tok(LANES), tok(LANES), tok(LANES), tok(LANES)],
        out_specs=[tok(MLA_HEADS * LANES), tok(MLA_HEADS * LANES),
                   pl.BlockSpec((1, 1, nv, tm), lambda bi, i: (bi, i, 0, 0))] + dil_specs,
        out_shape=[jax.ShapeDtypeStruct((b, s, MLA_HEADS * LANES), BF16),
                   jax.ShapeDtypeStruct((b, s, MLA_HEADS * LANES), BF16),
                   jax.ShapeDtypeStruct((b, s // tm, nv, tm), BF16)] + dil_shapes,
        scratch_shapes=[pltpu.VMEM((3 * DIL_W // LANES, tm, LANES), F32)],
        compiler_params=_cparams(("parallel", "arbitrary")),
        name="even_in_proj",
    )(x, sh, sc, nw, win, wuq, wuk, wvt, qlat, kvlat, qn, kn, dqn, dkn, cm, sm, c64, s64)


def _odd_in_kernel(x_ref, sh_ref, sc_ref, nw_ref, win_ref, wvt_ref, dqn_ref, dkn_ref, mqn_ref, mkn_ref,
                   c64_ref, s64_ref,
                   qc_out, kc_out, qm_out, qmf_out, km_out, vct_out, vmt_out, kmean_out):
    x = x_ref[0]
    tm = x.shape[0]
    h = (_rms(x, nw_ref[...]) * (1.0 + sc_ref[0]) + sh_ref[0]).astype(BF16)
    u = jnp.dot(h, win_ref[...], preferred_element_type=F32)
    vt = _nt_dot(wvt_ref[...], h).astype(BF16)
    nvc = vct_out.shape[2]
    vct_out[0, 0] = vt[:nvc]
    vmt_out[0, 0] = vt[nvc:]
    c64, s64 = c64_ref[0], s64_ref[0]
    scale = HEAD_DIM ** -0.5 * LOG2E
    nqk = DIFF_HEADS * LANES
    nm = MOBA_HEADS * HEAD_DIM
    rot = 2 * nqk + 2 * nm

    def tables(gain_ref, mult):
        return gain_ref[0:1] * c64 * mult, gain_ref[1:2] * s64 * mult

    def roped(off, col, tab):
        a = off + LANES * col
        xc, xr = u[:, a:a + LANES], u[:, rot + a:rot + a + LANES]
        return _inv_rms64(xc) * (xc * tab[0] + xr * tab[1])

    t_qc, t_kc, t_qm, t_km = tables(dqn_ref, scale), tables(dkn_ref, 1.0), tables(mqn_ref, 1.0), tables(mkn_ref, 1.0)
    for col in range(nqk // LANES):
        sl = slice(LANES * col, LANES * (col + 1))
        qc_out[0, :, sl] = roped(0, col, t_qc).astype(BF16)
        kc_out[0, :, sl] = roped(nqk, col, t_kc).astype(BF16)

    base = 2 * nqk
    for col in range(nm // LANES):
        sl = slice(LANES * col, LANES * (col + 1))
        qm = roped(base, col, t_qm)
        qmf_out[0, :, sl] = qm
        qm_out[0, :, sl] = (qm * scale).astype(BF16)
        km = roped(base + nm, col, t_km)
        km_out[0, :, sl] = km.astype(BF16)
        for blk in range(tm // MOBA_BLOCK):
            rows = slice(MOBA_BLOCK * blk, MOBA_BLOCK * (blk + 1))
            kmean_out[0, 0, blk:blk + 1, sl] = jnp.mean(km[rows], axis=0, keepdims=True)


def _odd_in(x, sh, sc, nw, win, wvt, dqn, dkn, mqn, mkn, c64, s64, tm):
    b, s, d = x.shape
    tok = lambda w: pl.BlockSpec((1, tm, w), lambda bi, i: (bi, i, 0))
    per_b = pl.BlockSpec((1, 1, d), lambda bi, i: (bi, 0, 0))
    full = lambda a: pl.BlockSpec(a.shape, lambda bi, i: (0,) * a.ndim)
    w = DIFF_HEADS * LANES
    nv = wvt.shape[0] // 2
    nblk = tm // MOBA_BLOCK
    shp = lambda dt: jax.ShapeDtypeStruct((b, s, w), dt)
    vt_spec = pl.BlockSpec((1, 1, nv, tm), lambda bi, i: (bi, i, 0, 0))
    vt_shape = jax.ShapeDtypeStruct((b, s // tm, nv, tm), BF16)
    return pl.pallas_call(
        _odd_in_kernel,
        grid=(b, s // tm),
        in_specs=[tok(d), per_b, per_b, full(nw), full(win), full(wvt), full(dqn), full(dkn), full(mqn), full(mkn),
                  tok(LANES), tok(LANES)],
        out_specs=[tok(w)] * 5 + [vt_spec, vt_spec, pl.BlockSpec((1, 1, nblk, w), lambda bi, i: (bi, i, 0, 0))],
        out_shape=[shp(BF16), shp(BF16), shp(BF16), shp(F32), shp(BF16), vt_shape, vt_shape,
                   jax.ShapeDtypeStruct((b, s // tm, nblk, w), F32)],
        compiler_params=_cparams(("parallel", "arbitrary")),
        name="odd_in_proj",
    )(x, sh, sc, nw, win, wvt, dqn, dkn, mqn, mkn, c64, s64)


def _flash_t(qs, k_ref, vt_ref, k_slices, v_rows, scr, i, t, past_mask=None):
    m_scr, acc_scr, s_scr, p_scr, a_scr = scr
    n = len(qs)
    ones = jnp.ones((ONES_ROWS, t), BF16)

    def qk(blk, s):
        b0 = pl.multiple_of(blk * t, t)
        return _nt_dot(k_ref[0, pl.ds(b0, t), k_slices[s]], qs[s])

    def pv(blk, s, p):
        vt = jnp.concatenate([vt_ref[0, blk, v_rows[s], :], ones], axis=0)
        return jnp.dot(vt, p, preferred_element_type=F32)

    krow = lax.broadcasted_iota(jnp.int32, (t, t), 0)
    qcol = lax.broadcasted_iota(jnp.int32, (t, t), 1)
    causal = krow <= qcol
    for s in range(n):
        st = jnp.where(causal, qk(i, s), NEG_INF)
        m = jnp.max(st, axis=0, keepdims=True)
        m_scr[s] = m
        p_scr[s] = jnp.exp2(st - m).astype(BF16)
        a_scr[s] = jnp.ones_like(m)
        acc_scr[s] = jnp.zeros(acc_scr.shape[1:], F32)
        s_scr[s] = qk(0, s)

    def body(jj, prev_blk):
        nxt = jnp.minimum(jj + 1, jnp.maximum(i - 1, 0))
        p_prev = [p_scr[s] for s in range(n)]
        a_prev = [a_scr[s] for s in range(n)]
        s_next = [qk(nxt, s) for s in range(n)]
        pvs = [pv(prev_blk, s, p_prev[s]) for s in range(n)]
        for s in range(n):
            st = s_scr[s]
            if past_mask is not None:
                st = jnp.where(past_mask(s, jj), st, NEG_INF)
            m_prev = m_scr[s]
            m_new = jnp.maximum(m_prev, jnp.max(st, axis=0, keepdims=True))
            a_scr[s] = jnp.exp2(m_prev - m_new)
            p_scr[s] = jnp.exp2(st - m_new).astype(BF16)
            m_scr[s] = m_new
        for s in range(n):
            s_scr[s] = s_next[s]
        for s in range(n):
            acc_scr[s] = a_prev[s] * acc_scr[s] + pvs[s]
        return jj

    last = lax.fori_loop(0, i, body, i)
    for s in range(n):
        acc_scr[s] = a_scr[s] * acc_scr[s] + pv(last, s, p_scr[s])


def _flash_scratch(n, nv, t):
    return [pltpu.VMEM((n, 1, t), F32), pltpu.VMEM((n, nv + ONES_ROWS, t), F32), pltpu.VMEM((n, t, t), F32),
            pltpu.VMEM((n, t, t), BF16), pltpu.VMEM((n, 1, t), F32)]


def _normalised(acc_scr, s, nv):
    return acc_scr[s, :nv, :] / acc_scr[s, nv:nv + 1, :]


def _half_mask(x, half):
    lane = _lane_iota((1, LANES))
    return jnp.where((lane >= HEAD_DIM) == bool(half), x, jnp.zeros_like(x))


def _mla_kernel(q_ref, k_ref, vt_ref, ot_ref, *scr, t, nh):
    i = pl.program_id(2)
    qs = [q_ref[0, :, LANES * s:LANES * (s + 1)] for s in range(nh)]
    ks = [slice(LANES * s, LANES * (s + 1)) for s in range(nh)]
    vr = [slice(HEAD_DIM * s, HEAD_DIM * (s + 1)) for s in range(nh)]
    _flash_t(qs, k_ref, vt_ref, ks, vr, scr, i, t)
    for s in range(nh):
        ot_ref[0, 0, vr[s], :] = _normalised(scr[1], s, HEAD_DIM).astype(ot_ref.dtype)


def _mla_attention(q, k, vt, nh):
    b, s, _ = q.shape
    _, nb, nv, t = vt.shape
    return pl.pallas_call(
        functools.partial(_mla_kernel, t=t, nh=nh),
        grid=(b, MLA_HEADS // nh, nb),
        in_specs=[pl.BlockSpec((1, t, nh * LANES), lambda bi, p, i: (bi, i, p)),
                  pl.BlockSpec((1, s, nh * LANES), lambda bi, p, i: (bi, 0, p)),
                  pl.BlockSpec((1, nb, nh * HEAD_DIM, t), lambda bi, p, i: (bi, 0, p, 0))],
        out_specs=pl.BlockSpec((1, 1, nh * HEAD_DIM, t), lambda bi, p, i: (bi, i, p, 0)),
        out_shape=jax.ShapeDtypeStruct((b, nb, nv, t), BF16),
        scratch_shapes=_flash_scratch(nh, HEAD_DIM, t),
        compiler_params=_cparams(("parallel", "parallel", "arbitrary")),
        name="mla_attention",
    )(q, k, vt)


def _diff_kernel(q_ref, k_ref, vt_ref, lam_ref, sub_ref, ot_ref, *scr, t, nh, lam_init):
    i = pl.program_id(2)
    qs, ks, vr = [], [], []
    for hd in range(nh):
        cols = slice(LANES * hd, LANES * (hd + 1))
        q = q_ref[0, :, cols]
        for half in range(2):
            qs.append(_half_mask(q, half))
            ks.append(cols)
            vr.append(cols)
    _flash_t(qs, k_ref, vt_ref, ks, vr, scr, i, t)
    lv = lam_ref[...]
    lam = (jnp.exp(jnp.sum(lv[0:1] * lv[1:2], keepdims=True))
           - jnp.exp(jnp.sum(lv[2:3] * lv[3:4], keepdims=True)) + lam_init)
    for hd in range(nh):
        o = _normalised(scr[1], 2 * hd, LANES) - lam * _normalised(scr[1], 2 * hd + 1, LANES)
        o = o * lax.rsqrt(jnp.mean(o * o, axis=0, keepdims=True) + NORM_EPS) * sub_ref[...]
        ot_ref[0, 0, LANES * hd:LANES * (hd + 1), :] = (o * (1.0 - lam_init)).astype(ot_ref.dtype)


def _diff_attention(q, k, vt, lam_rows, subln_col, lam_init, nh):
    b, s, w = q.shape
    _, nb, _, t = vt.shape
    return pl.pallas_call(
        functools.partial(_diff_kernel, t=t, nh=nh, lam_init=lam_init),
        grid=(b, DIFF_HEADS // nh, nb),
        in_specs=[pl.BlockSpec((1, t, nh * LANES), lambda bi, p, i: (bi, i, p)),
                  pl.BlockSpec((1, s, nh * LANES), lambda bi, p, i: (bi, 0, p)),
                  pl.BlockSpec((1, nb, nh * LANES, t), lambda bi, p, i: (bi, 0, p, 0)),
                  pl.BlockSpec(lam_rows.shape, lambda bi, p, i: (0, 0)),
                  pl.BlockSpec(subln_col.shape, lambda bi, p, i: (0, 0))],
        out_specs=pl.BlockSpec((1, 1, nh * LANES, t), lambda bi, p, i: (bi, i, p, 0)),
        out_shape=jax.ShapeDtypeStruct((b, nb, w, t), BF16),
        scratch_shapes=_flash_scratch(2 * nh, LANES, t),
        compiler_params=_cparams(("parallel", "parallel", "arbitrary")),
        name="diff_attention",
    )(q, k, vt, lam_rows, subln_col)


def _moba_kernel(q_ref, qf_ref, k_ref, vt_ref, kmean_ref, ot_ref, sel_scr, *scr, t, npair):
    i = pl.program_id(2)
    nb = kmean_ref.shape[1]
    brow = lax.broadcasted_iota(jnp.int32, (nb, t), 0).astype(F32)
    past = brow < i.astype(F32)
    qs, ks, vr = [], [], []
    for pr in range(npair):
        cols = slice(LANES * pr, LANES * (pr + 1))
        q, qf, kmean = q_ref[0, :, cols], qf_ref[0, :, cols], kmean_ref[0, :, cols]
        for half in range(2):
            s = 2 * pr + half
            gate = lax.dot_general(kmean, _half_mask(qf, half), (((1,), (1,)), ((), ())),
                                   preferred_element_type=F32, precision=lax.Precision.HIGHEST)
            gate = jnp.where(past, gate, NEG_INF)
            sel = jnp.zeros((nb, t), F32)
            for _ in range(MOBA_TOPK):
                top = jnp.max(gate, axis=0, keepdims=True)
                first = jnp.min(jnp.where(gate == top, brow, float(nb)), axis=0, keepdims=True)
                pick = brow == first
                sel = jnp.where(pick, 1.0, sel)
                gate = jnp.where(pick, 2 * NEG_INF, gate)
            sel_scr[s] = jnp.where(past, sel, 0.0)
            qs.append(_half_mask(q, half))
            ks.append(cols)
            vr.append(slice(HEAD_DIM * s, HEAD_DIM * (s + 1)))
    _flash_t(qs, k_ref, vt_ref, ks, vr, scr, i, t,
             past_mask=lambda s, jj: sel_scr[s, pl.ds(jj, 1), :] > 0.5)
    for s in range(2 * npair):
        ot_ref[0, 0, vr[s], :] = _normalised(scr[1], s, HEAD_DIM).astype(ot_ref.dtype)


def _moba_attention(q, qf, k, vt, kmean, npair):
    b, s, w = q.shape
    _, nb, _, t = vt.shape
    nkm = kmean.shape[1]
    lanes = npair * LANES
    return pl.pallas_call(
        functools.partial(_moba_kernel, t=t, npair=npair),
        grid=(b, w // lanes, nb),
        in_specs=[pl.BlockSpec((1, t, lanes), lambda bi, p, i: (bi, i, p)),
                  pl.BlockSpec((1, t, lanes), lambda bi, p, i: (bi, i, p)),
                  pl.BlockSpec((1, s, lanes), lambda bi, p, i: (bi, 0, p)),
                  pl.BlockSpec((1, nb, lanes, t), lambda bi, p, i: (bi, 0, p, 0)),
                  pl.BlockSpec((1, nkm, lanes), lambda bi, p, i: (bi, 0, p))],
        out_specs=pl.BlockSpec((1, 1, lanes, t), lambda bi, p, i: (bi, i, p, 0)),
        out_shape=jax.ShapeDtypeStruct((b, nb, w, t), BF16),
        scratch_shapes=[pltpu.VMEM((2 * npair, nkm, t), F32)] + _flash_scratch(2 * npair, HEAD_DIM, t),
        compiler_params=_cparams(("parallel", "parallel", "arbitrary")),
        name="moba_attention",
    )(q, qf, k, vt, kmean)


def _sw_kernel(q_ref, kc_ref, vc_ref, kp_ref, vp_ref, o_ref, l_ref, *, nbk):
    i = pl.program_id(1)
    blk = SW_BLOCK
    qi = lax.broadcasted_iota(jnp.int32, (blk, 2 * blk), 0)
    kj = lax.broadcasted_iota(jnp.int32, (blk, 2 * blk), 1)
    band = (kj >= qi) & (kj <= qi + blk)
    lane = _lane_iota((1, LANES))
    for n in range(nbk):
        rows = slice(blk * n, blk * (n + 1))
        q = q_ref[0, rows, :]
        if n == 0:
            kprev, vprev = kp_ref[0], vp_ref[0]
            valid = band & ((kj >= blk) | (i > 0))
        else:
            prev = slice(blk * (n - 1), blk * n)
            kprev, vprev = kc_ref[0, prev, :], vc_ref[0, prev, :]
            valid = band
        kk = jnp.concatenate([kprev, kc_ref[0, rows, :]], axis=0)
        vv = jnp.concatenate([vprev, vc_ref[0, rows, :]], axis=0)
        for pair in range(DIL_W // LANES):
            sl = slice(LANES * pair, LANES * (pair + 1))
            outs, lses = [], []
            for half in range(2):
                sc = jnp.where(valid, _nt_dot(_half_mask(q[:, sl], half), kk[:, sl]), NEG_INF)
                m = jnp.max(sc, axis=-1, keepdims=True)
                p = jnp.exp2(sc - m)
                l = jnp.sum(p, axis=-1, keepdims=True)
                outs.append(jnp.dot(p.astype(BF16), vv[:, sl], preferred_element_type=F32) / l)
                lses.append(m + jnp.log2(l))
            o_ref[0, rows, sl] = jnp.where(lane < HEAD_DIM, outs[0], outs[1])
            l_ref[0, rows, sl] = jnp.where(lane < HEAD_DIM, lses[0], lses[1])


def _sliding_window(dil, nbk, name):
    lead, (m, w3) = dil.shape[:-2], dil.shape[-2:]
    seqs = dil.reshape((-1, m, w3))
    nbk = min(nbk, m // SW_BLOCK)
    rows = SW_BLOCK * nbk
    cur = lambda off: pl.BlockSpec((1, rows, DIL_W), lambda n, i: (n, i, off))
    prev = lambda off: pl.BlockSpec((1, SW_BLOCK, DIL_W), lambda n, i: (n, jnp.maximum(i * nbk - 1, 0), off))
    out_spec = pl.BlockSpec((1, rows, DIL_W), lambda n, i: (n, i, 0))
    o, l = pl.pallas_call(
        functools.partial(_sw_kernel, nbk=nbk),
        grid=(seqs.shape[0], m // rows),
        in_specs=[cur(0), cur(1), cur(2), prev(1), prev(2)],
        out_specs=[out_spec, out_spec],
        out_shape=[jax.ShapeDtypeStruct((seqs.shape[0], m, DIL_W), F32)] * 2,
        compiler_params=_cparams(("parallel", "arbitrary")),
        name=name,
    )(seqs, seqs, seqs, seqs, seqs)
    return o.reshape(lead + (m, DIL_W)), l.reshape(lead + (m, DIL_W))


def _even_out_kernel(x_ref, g_ref, oat_ref, o0_ref, o1_ref, o2_ref, l0_ref, l1_ref, l2_ref, w_ref, y_ref, tscr):
    tm = x_ref.shape[1]
    ncol = DIL_W // LANES

    def token_major(ref, slot):
        r = ref.shape[1]
        for c in range(r):
            for j in range(ncol):
                tscr[slot * ncol + j, pl.ds(c, tm // r, stride=r), :] = ref[0, c, :, LANES * j:LANES * (j + 1)]
        return jnp.concatenate([tscr[slot * ncol + j] for j in range(ncol)], axis=1)

    o0, l0 = o0_ref[0], l0_ref[0]
    o1, l1 = token_major(o1_ref, 0), token_major(l1_ref, 1)
    o2, l2 = token_major(o2_ref, 2), token_major(l2_ref, 3)
    top = jnp.maximum(jnp.maximum(l0, l1), l2)
    w0, w1, w2 = jnp.exp2(l0 - top), jnp.exp2(l1 - top), jnp.exp2(l2 - top)
    ob = (w0 * o0 + w1 * o1 + w2 * o2) / (w0 + w1 + w2)
    na = oat_ref.shape[2]
    y = _tn_dot(oat_ref[0, 0], w_ref[:na, :]) + jnp.dot(ob.astype(BF16), w_ref[na:, :], preferred_element_type=F32)
    y_ref[0] = x_ref[0] + g_ref[0] * y


def _t_spec(a):
    return pl.BlockSpec((1, 1) + a.shape[2:], lambda bi, i: (bi, i, 0, 0))


def _even_out(x, g1, oat, o_dil, l_dil, w_out):
    b, s, d = x.shape
    tm = oat.shape[-1]
    tok = lambda w: pl.BlockSpec((1, tm, w), lambda bi, i: (bi, i, 0))

    def dil_spec(a):
        if a.ndim == 3:
            return tok(DIL_W)
        r = a.shape[1]
        return pl.BlockSpec((1, r, tm // r, DIL_W), lambda bi, i: (bi, 0, i, 0))

    return pl.pallas_call(
        _even_out_kernel,
        grid=(b, s // tm),
        in_specs=[tok(d), pl.BlockSpec((1, 1, d), lambda bi, i: (bi, 0, 0)), _t_spec(oat)]
                 + [dil_spec(a) for a in (*o_dil, *l_dil)] + [pl.BlockSpec(w_out.shape, lambda bi, i: (0, 0))],
        out_specs=tok(d),
        out_shape=jax.ShapeDtypeStruct((b, s, d), F32),
        scratch_shapes=[pltpu.VMEM((4 * DIL_W // LANES, tm, LANES), F32)],
        compiler_params=_cparams(("parallel", "arbitrary")),
        name="even_out_proj",
    )(x, g1, oat, *o_dil, *l_dil, w_out)


def _odd_out_kernel(x_ref, g_ref, oct_ref, odt_ref, w_ref, y_ref):
    nc = oct_ref.shape[2]
    y = _tn_dot(oct_ref[0, 0], w_ref[:nc, :]) + _tn_dot(odt_ref[0, 0], w_ref[nc:, :])
    y_ref[0] = x_ref[0] + g_ref[0] * y


def _odd_out(x, g1, oct, odt, w_out):
    b, s, d = x.shape
    tm = oct.shape[-1]
    tok = lambda w: pl.BlockSpec((1, tm, w), lambda bi, i: (bi, i, 0))
    return pl.pallas_call(
        _odd_out_kernel,
        grid=(b, s // tm),
        in_specs=[tok(d), pl.BlockSpec((1, 1, d), lambda bi, i: (bi, 0, 0)), _t_spec(oct), _t_spec(odt),
                  pl.BlockSpec(w_out.shape, lambda bi, i: (0, 0))],
        out_specs=tok(d),
        out_shape=jax.ShapeDtypeStruct((b, s, d), F32),
        compiler_params=_cparams(("parallel", "arbitrary")),
        name="odd_out_proj",
    )(x, g1, oct, odt, w_out)


def _mlp_kernel(x_ref, sh_ref, sc_ref, g_ref, nw_ref, w1_ref, w2_ref, y_ref, *, tf):
    x = x_ref[0]
    h = (_rms(x, nw_ref[...]) * (1.0 + sc_ref[0]) + sh_ref[0]).astype(BF16)
    acc = jnp.zeros(x.shape, F32)
    for c in range(w1_ref.shape[1] // tf):
        a = jnp.maximum(jnp.dot(h, w1_ref[:, tf * c:tf * (c + 1)], preferred_element_type=F32), 0.0)
        acc = acc + jnp.dot((a * a).astype(BF16), w2_ref[tf * c:tf * (c + 1), :], preferred_element_type=F32)
    y_ref[0] = x + g_ref[0] * acc


def _mlp(x, sh, sc, g2, nw, w1, w2, tm, tf):
    b, s, d = x.shape
    tok = pl.BlockSpec((1, tm, d), lambda bi, i: (bi, i, 0))
    per_b = pl.BlockSpec((1, 1, d), lambda bi, i: (bi, 0, 0))
    const = lambda a: pl.BlockSpec(a.shape, lambda bi, i: (0, 0), pipeline_mode=pl.Buffered(1))
    return pl.pallas_call(
        functools.partial(_mlp_kernel, tf=tf),
        grid=(b, s // tm),
        in_specs=[tok, per_b, per_b, per_b, pl.BlockSpec(nw.shape, lambda bi, i: (0, 0)), const(w1), const(w2)],
        out_specs=tok,
        out_shape=jax.ShapeDtypeStruct((b, s, d), F32),
        compiler_params=_cparams(("parallel", "arbitrary")),
        name="mlp",
    )(x, sh, sc, g2, nw, w1, w2)


def _rope_tables(positions):
    pos = positions.astype(F32)[:, :, None]

    def cs(half):
        inv = ROPE_THETA ** (-jnp.arange(half, dtype=F32) / half)
        ang = pos * inv
        return jnp.cos(ang), jnp.sin(ang)

    c, s = cs(HEAD_DIM // 2)
    c64 = jnp.concatenate([c, c, c, c], axis=-1)
    s64 = jnp.concatenate([-s, s, -s, s], axis=-1)
    c, s = cs(MLA_ROPE // 2)
    ones = jnp.ones(pos.shape[:2] + (MLA_NOPE,), F32)
    pad = LANES - MLA_QK
    cm = jnp.concatenate([ones, c, c, jnp.ones(pos.shape[:2] + (pad,), F32)], axis=-1)
    sm = jnp.concatenate([0 * ones, -s, s, jnp.zeros(pos.shape[:2] + (pad,), F32)], axis=-1)
    return cm, sm, c64, s64


def _partner(n, width):
    idx = np.arange(n)
    return np.where(idx % width < width // 2, idx + width // 2, idx - width // 2)


def _even_weights(w_in, w_uq, w_ukv, qn, kn):
    o2 = MLA_Q_RANK + MLA_KV_RANK
    o3 = o2 + MLA_ROPE
    d = w_in.shape[0]
    p_rope = _partner(MLA_ROPE, MLA_ROPE)
    zeros = lambda n: jnp.zeros((d, n), w_in.dtype)
    kr = w_in[:, o2:o3]
    tail = LANES - MLA_QK
    krp = jnp.concatenate([zeros(MLA_NOPE), kr, zeros(tail)], axis=1)
    krp_rot = jnp.concatenate([zeros(MLA_NOPE), kr[:, p_rope], zeros(tail)], axis=1)
    dil = w_in[:, o3:].reshape(d, 3, len(DIL_CONFIGS), DIL_W)
    dil_main = dil.transpose(0, 2, 1, 3).reshape(d, -1)
    dil_rot = dil[:, :2][..., _partner(DIL_W, HEAD_DIM)].transpose(0, 2, 1, 3).reshape(d, -1)
    win = jnp.concatenate([w_in[:, :o2], krp, krp_rot, dil_main, dil_rot], axis=1)

    padq = ((0, 0), (0, 0), (0, tail))
    wuq_rot = jnp.concatenate([w_uq[:, :, :MLA_NOPE], w_uq[:, :, MLA_NOPE:][:, :, p_rope]], axis=-1)
    wuq = jnp.concatenate([jnp.pad(w_uq, padq).reshape(MLA_Q_RANK, -1),
                           jnp.pad(wuq_rot, padq).reshape(MLA_Q_RANK, -1)], axis=1)
    wuk = jnp.pad(w_ukv[:, :, :MLA_NOPE], ((0, 0), (0, 0), (0, LANES - MLA_NOPE))).reshape(MLA_KV_RANK, -1)
    wvt = w_ukv[:, :, MLA_NOPE:].reshape(MLA_KV_RANK, -1).T

    def gains(g):
        rot = jnp.concatenate([g[:MLA_NOPE], g[MLA_NOPE:][p_rope]])
        return jnp.stack([jnp.pad(g, (0, tail)), jnp.pad(rot, (0, tail))]).astype(F32)

    return win.astype(BF16), wuq.astype(BF16), wuk.astype(BF16), wvt.astype(BF16), gains(qn), gains(kn)


def _odd_weights(w_in):
    nqk = DIFF_HEADS * LANES
    nm = MOBA_HEADS * HEAD_DIM
    main = jnp.concatenate([w_in[:, :2 * nqk], w_in[:, 3 * nqk:3 * nqk + 2 * nm]], axis=1)
    win = jnp.concatenate([main, main[:, _partner(main.shape[1], HEAD_DIM)]], axis=1)
    wvt = jnp.concatenate([w_in[:, 2 * nqk:3 * nqk], w_in[:, 3 * nqk + 2 * nm:]], axis=1).T
    return win.astype(BF16), wvt.astype(BF16)


def _gains64(g):
    rot = g[_partner(HEAD_DIM, HEAD_DIM)]
    return jnp.stack([jnp.concatenate([g, g]), jnp.concatenate([rot, rot])]).astype(F32)


def kernel(x, c, positions, ada_w, ada_b, norm_mix, norm_mlp, mlp_w1, mlp_w2, even_w_in, even_w_out, mla_q_lat_norm, mla_kv_lat_norm, mla_w_uq, mla_w_ukv, mla_q_norm, mla_k_norm, dil_q_norm, dil_k_norm, odd_w_in, odd_w_out, diff_q_norm, diff_k_norm, diff_lambda, diff_subln, moba_q_norm, moba_k_norm):
    b, s, d = x.shape
    depth = ada_w.shape[0]
    t_attn = MOBA_BLOCK
    tm_mlp, tf = 512, 1024

    mod = _adaln(c, ada_w, ada_b)
    cm, sm, c64, s64 = _rope_tables(positions)

    for layer in range(depth):
        sh1, sc1, g1, sh2, sc2, g2 = [mod[layer, :, d * t:d * (t + 1)].reshape(b, 1, d) for t in range(6)]
        i = layer // 2
        if layer % 2 == 0:
            win, wuq, wuk, wvt, qn, kn = _even_weights(even_w_in[i], mla_w_uq[i], mla_w_ukv[i],
                                                       mla_q_norm[i], mla_k_norm[i])
            q, k, vt, *dils = _even_in(x, sh1, sc1, _row(norm_mix[layer]), win, wuq, wuk, wvt,
                                       _row(mla_q_lat_norm[i]), _row(mla_kv_lat_norm[i]), qn, kn,
                                       _gains64(dil_q_norm[i]), _gains64(dil_k_norm[i]), cm, sm, c64, s64, t_attn)
            o_at = _mla_attention(q, k, vt, nh=4)
            o_dil, l_dil = zip(*[_sliding_window(dg, 4, f"sliding_window_g{g}") for g, dg in enumerate(dils)])
            x = _even_out(x, g1, o_at, o_dil, l_dil, even_w_out[i].astype(BF16))
        else:
            lam_init = 0.8 - 0.6 * math.exp(-0.3 * layer)
            win, wvt = _odd_weights(odd_w_in[i])
            qc, kc, qm, qmf, km, vct, vmt, kmean = _odd_in(
                x, sh1, sc1, _row(norm_mix[layer]), win, wvt,
                _gains64(diff_q_norm[i]), _gains64(diff_k_norm[i]), _gains64(moba_q_norm[i]),
                _gains64(moba_k_norm[i]),
                c64, s64, t_attn)
            o_ct = _diff_attention(qc, kc, vct, diff_lambda[i].astype(F32),
                                   diff_subln[i].reshape(-1, 1).astype(F32), lam_init, nh=2)
            o_dt = _moba_attention(qm, qmf, km, vmt, kmean.reshape(b, s // MOBA_BLOCK, kmean.shape[-1]), npair=2)
            x = _odd_out(x, g1, o_ct, o_dt, odd_w_out[i].astype(BF16))
        x = _mlp(x, sh2, sc2, g2, _row(norm_mlp[layer]), mlp_w1[layer].astype(BF16), mlp_w2[layer].astype(BF16),
                 tm_mlp, tf)
    return x
```

```python
import functools
import math

import jax
import jax.numpy as jnp
import numpy as np
from jax import lax
from jax.experimental import pallas as pl
from jax.experimental.pallas import tpu as pltpu

F32 = jnp.float32
BF16 = jnp.bfloat16

LANES = 128
HEAD_DIM = 64
ROPE_THETA = 10000.0
NORM_EPS = 1e-6
NEG_INF = -1e30
LOG2E = math.log2(math.e)

MLA_HEADS = 8
MLA_Q_RANK = 384
MLA_KV_RANK = 256
MLA_NOPE = 64
MLA_ROPE = 32
MLA_QK = MLA_NOPE + MLA_ROPE
DIL_CONFIGS = ((128, 1), (512, 4), (2048, 16))
DIL_HEADS = 4
DIL_W = DIL_HEADS * HEAD_DIM
DIFF_HEADS = 4
MOBA_HEADS = 8
MOBA_BLOCK = 256
MOBA_TOPK = 3
SW_BLOCK = 128
ONES_ROWS = 16

VMEM_LIMIT = 56 * 1024 * 1024


def _cparams(sem):
    return pltpu.CompilerParams(dimension_semantics=sem, vmem_limit_bytes=VMEM_LIMIT)


def _nt_dot(a, b):
    return lax.dot_general(a, b, (((1,), (1,)), ((), ())), preferred_element_type=F32)


def _tn_dot(a, b):
    return lax.dot_general(a, b, (((0,), (0,)), ((), ())), preferred_element_type=F32)


def _rms(x, w):
    return x * lax.rsqrt(jnp.mean(x * x, axis=-1, keepdims=True) + NORM_EPS) * w


def _lane_iota(shape):
    return lax.broadcasted_iota(jnp.int32, shape, len(shape) - 1)


def _adaln_kernel(c_ref, w_ref, b_ref, o_ref):
    c = c_ref[...]
    cond = c * (1.0 / (1.0 + jnp.exp(-c)))
    o_ref[0] = jnp.dot(cond, w_ref[0], preferred_element_type=F32,
                       precision=lax.Precision.HIGHEST) + b_ref[0]


def _adaln(c, ada_w, ada_b):
    depth, d, n = ada_w.shape
    b = c.shape[0]
    tn = 1024
    return pl.pallas_call(
        _adaln_kernel,
        grid=(depth, n // tn),
        in_specs=[pl.BlockSpec((b, d), lambda l, j: (0, 0)),
                  pl.BlockSpec((1, d, tn), lambda l, j: (l, 0, j)),
                  pl.BlockSpec((1, 1, tn), lambda l, j: (l, 0, j))],
        out_specs=pl.BlockSpec((1, b, tn), lambda l, j: (l, 0, j)),
        out_shape=jax.ShapeDtypeStruct((depth, b, n), F32),
        compiler_params=_cparams(("parallel", "parallel")),
        name="adaln",
    )(c, ada_w, ada_b.reshape(depth, 1, n))


def _inv_rms64(x):
    lane = _lane_iota((1, LANES))
    lo = lane < HEAD_DIM
    sq = x * x
    s_lo = jnp.sum(jnp.where(lo, sq, 0.0), axis=-1, keepdims=True)
    s_hi = jnp.sum(jnp.where(lo, 0.0, sq), axis=-1, keepdims=True)
    return lax.rsqrt(jnp.where(lo, s_lo, s_hi) * (1.0 / HEAD_DIM) + NORM_EPS)


def _even_in_kernel(x_ref, sh_ref, sc_ref, nw_ref, win_ref, wuq_ref, wuk_ref, wvt_ref,
                    qlat_ref, kvlat_ref, qn_ref, kn_ref, dqn_ref, dkn_ref,
                    cm_ref, sm_ref, c64_ref, s64_ref,
                    q_out, k_out, vt_out, d0_out, d1_out, d2_out, dscr):
    x = x_ref[0]
    tm = x.shape[0]
    h = _rms(x, nw_ref[...]) * (1.0 + sc_ref[0]) + sh_ref[0]
    u = jnp.dot(h.astype(BF16), win_ref[...], preferred_element_type=F32)

    o1 = MLA_Q_RANK
    o2 = o1 + MLA_KV_RANK
    o_kr, o_krr, o_dil = o2, o2 + LANES, o2 + 2 * LANES
    n_dil = 3 * DIL_W * len(DIL_CONFIGS)
    o_rot = o_dil + n_dil
    nq = MLA_HEADS * LANES
    cqn = _rms(u[:, :o1], qlat_ref[...]).astype(BF16)
    qp = jnp.dot(cqn, wuq_ref[...], preferred_element_type=F32)
    ckvn = _rms(u[:, o1:o2], kvlat_ref[...]).astype(BF16)
    kvp = jnp.dot(ckvn, wuk_ref[...], preferred_element_type=F32)
    vt_out[0, 0] = _nt_dot(wvt_ref[...], ckvn).astype(BF16)
    kr, kr_rot = u[:, o_kr:o_kr + LANES], u[:, o_krr:o_krr + LANES]

    cm, sm = cm_ref[0], sm_ref[0]
    q_scale = MLA_QK ** -0.5 * LOG2E
    qgc, qgs = qn_ref[0:1] * cm * q_scale, qn_ref[1:2] * sm * q_scale
    kgc, kgs = kn_ref[0:1] * cm, kn_ref[1:2] * sm
    kr_term = kr_rot * kgs
    for hd in range(MLA_HEADS):
        sl = slice(LANES * hd, LANES * (hd + 1))
        qh = qp[:, sl]
        inv = lax.rsqrt(jnp.sum(qh * qh, -1, keepdims=True) * (1.0 / MLA_QK) + NORM_EPS)
        q_out[0, :, sl] = (inv * (qh * qgc + qp[:, nq + LANES * hd:nq + LANES * (hd + 1)] * qgs)).astype(BF16)
        kh = kvp[:, sl] + kr
        inv = lax.rsqrt(jnp.sum(kh * kh, -1, keepdims=True) * (1.0 / MLA_QK) + NORM_EPS)
        k_out[0, :, sl] = (inv * (kh * kgc + kr_term)).astype(BF16)

    c64, s64 = c64_ref[0], s64_ref[0]
    d_scale = HEAD_DIM ** -0.5 * LOG2E
    dqc, dqs = dqn_ref[0:1] * c64 * d_scale, dqn_ref[1:2] * s64 * d_scale
    dkc, dks = dkn_ref[0:1] * c64, dkn_ref[1:2] * s64
    ncol = DIL_W // LANES
    for g, d_out in enumerate((d0_out, d1_out, d2_out)):
        _, r = DIL_CONFIGS[g]
        base = o_dil + 3 * DIL_W * g
        rbase = o_rot + 2 * DIL_W * g
        for j in range(3 * ncol):
            xc = u[:, base + LANES * j:base + LANES * (j + 1)]
            if j < 2 * ncol:
                xr = u[:, rbase + LANES * j:rbase + LANES * (j + 1)]
                gc, gs = (dqc, dqs) if j < ncol else (dkc, dks)
                xc = _inv_rms64(xc) * (xc * gc + xr * gs)
            if r == 1:
                d_out[0, :, LANES * j:LANES * (j + 1)] = xc.astype(BF16)
            else:
                dscr[j] = xc
                for c in range(r):
                    d_out[0, c, :, LANES * j:LANES * (j + 1)] = dscr[j, pl.ds(c, tm // r, stride=r), :].astype(BF16)


def _row(v):
    return v.reshape(1, -1).astype(F32)


def _even_in(x, sh, sc, nw, win, wuq, wuk, wvt, qlat, kvlat, qn, kn, dqn, dkn, cm, sm, c64, s64, tm):
    b, s, d = x.shape
    nv = wvt.shape[0]
    tok = lambda w: pl.BlockSpec((1, tm, w), lambda bi, i: (bi, i, 0))
    per_b = pl.BlockSpec((1, 1, d), lambda bi, i: (bi, 0, 0))
    full = lambda a: pl.BlockSpec(a.shape, lambda bi, i: (0,) * a.ndim)
    dil_specs, dil_shapes = [], []
    for _, r in DIL_CONFIGS:
        if r == 1:
            dil_specs.append(tok(3 * DIL_W))
            dil_shapes.append(jax.ShapeDtypeStruct((b, s, 3 * DIL_W), BF16))
        else:
            dil_specs.append(pl.BlockSpec((1, r, tm // r, 3 * DIL_W), lambda bi, i: (bi, 0, i, 0)))
            dil_shapes.append(jax.ShapeDtypeStruct((b, r, s // r, 3 * DIL_W), BF16))
    return pl.pallas_call(
        _even_in_kernel,
        grid=(b, s // tm),
        in_specs=[tok(d), per_b, per_b, full(nw), full(win), full(wuq), full(wuk), full(wvt),
                  full(qlat), full(kvlat), full(qn), full(kn), full(dqn), full(dkn),
                  tok(LANES), tok(LANES), tok(LANES), tok(LANES)],
        out_specs=[tok(MLA_HEADS * LANES), tok(MLA_HEADS * LANES),
                   pl.BlockSpec((1, 1, nv, tm), lambda bi, i: (bi, i, 0, 0))] + dil_specs,
        out_shape=[jax.ShapeDtypeStruct((b, s, MLA_HEADS * LANES), BF16),
                   jax.ShapeDtypeStruct((b, s, MLA_HEADS * LANES), BF16),
                   jax.ShapeDtypeStruct((b, s // tm, nv, tm), BF16)] + dil_shapes,
        scratch_shapes=[pltpu.VMEM((3 * DIL_W // LANES, tm, LANES), F32)],
        compiler_params=_cparams(("parallel", "arbitrary")),
        name="even_in_proj",
    )(x, sh, sc, nw, win, wuq, wuk, wvt, qlat, kvlat, qn, kn, dqn, dkn, cm, sm, c64, s64)


def _odd_in_kernel(x_ref, sh_ref, sc_ref, nw_ref, win_ref, wvt_ref, dqn_ref, dkn_ref, mqn_ref, mkn_ref,
                   c64_ref, s64_ref,
                   qc_out, kc_out, qm_out, qmf_out, km_out, vct_out, vmt_out, kmean_out):
    x = x_ref[0]
    tm = x.shape[0]
    h = (_rms(x, nw_ref[...]) * (1.0 + sc_ref[0]) + sh_ref[0]).astype(BF16)
    u = jnp.dot(h, win_ref[...], preferred_element_type=F32)
    vt = _nt_dot(wvt_ref[...], h).astype(BF16)
    nvc = vct_out.shape[2]
    vct_out[0, 0] = vt[:nvc]
    vmt_out[0, 0] = vt[nvc:]
    c64, s64 = c64_ref[0], s64_ref[0]
    scale = HEAD_DIM ** -0.5 * LOG2E
    nqk = DIFF_HEADS * LANES
    nm = MOBA_HEADS * HEAD_DIM
    rot = 2 * nqk + 2 * nm

    def tables(gain_ref, mult):
        return gain_ref[0:1] * c64 * mult, gain_ref[1:2] * s64 * mult

    def roped(off, col, tab):
        a = off + LANES * col
        xc, xr = u[:, a:a + LANES], u[:, rot + a:rot + a + LANES]
        return _inv_rms64(xc) * (xc * tab[0] + xr * tab[1])

    t_qc, t_kc, t_qm, t_km = tables(dqn_ref, scale), tables(dkn_ref, 1.0), tables(mqn_ref, 1.0), tables(mkn_ref, 1.0)
    for col in range(nqk // LANES):
        sl = slice(LANES * col, LANES * (col + 1))
        qc_out[0, :, sl] = roped(0, col, t_qc).astype(BF16)
        kc_out[0, :, sl] = roped(nqk, col, t_kc).astype(BF16)

    base = 2 * nqk
    for col in range(nm // LANES):
        sl = slice(LANES * col, LANES * (col + 1))
        qm = roped(base, col, t_qm)
        qmf_out[0, :, sl] = qm
        qm_out[0, :, sl] = (qm * scale).astype(BF16)
        km = roped(base + nm, col, t_km)
        km_out[0, :, sl] = km.astype(BF16)
        for blk in range(tm // MOBA_BLOCK):
            rows = slice(MOBA_BLOCK * blk, MOBA_BLOCK * (blk + 1))
            kmean_out[0, 0, blk:blk + 1, sl] = jnp.mean(km[rows], axis=0, keepdims=True)


def _odd_in(x, sh, sc, nw, win, wvt, dqn, dkn, mqn, mkn, c64, s64, tm):
    b, s, d = x.shape
    tok = lambda w: pl.BlockSpec((1, tm, w), lambda bi, i: (bi, i, 0))
    per_b = pl.BlockSpec((1, 1, d), lambda bi, i: (bi, 0, 0))
    full = lambda a: pl.BlockSpec(a.shape, lambda bi, i: (0,) * a.ndim)
    w = DIFF_HEADS * LANES
    nv = wvt.shape[0] // 2
    nblk = tm // MOBA_BLOCK
    shp = lambda dt: jax.ShapeDtypeStruct((b, s, w), dt)
    vt_spec = pl.BlockSpec((1, 1, nv, tm), lambda bi, i: (bi, i, 0, 0))
    vt_shape = jax.ShapeDtypeStruct((b, s // tm, nv, tm), BF16)
    return pl.pallas_call(
        _odd_in_kernel,
        grid=(b, s // tm),
        in_specs=[tok(d), per_b, per_b, full(nw), full(win), full(wvt), full(dqn), full(dkn), full(mqn), full(mkn),
                  tok(LANES), tok(LANES)],
        out_specs=[tok(w)] * 5 + [vt_spec, vt_spec, pl.BlockSpec((1, 1, nblk, w), lambda bi, i: (bi, i, 0, 0))],
        out_shape=[shp(BF16), shp(BF16), shp(BF16), shp(F32), shp(BF16), vt_shape, vt_shape,
                   jax.ShapeDtypeStruct((b, s // tm, nblk, w), F32)],
        compiler_params=_cparams(("parallel", "arbitrary")),
        name="odd_in_proj",
    )(x, sh, sc, nw, win, wvt, dqn, dkn, mqn, mkn, c64, s64)


def _q_tiles_t(q, split_halves):
    qt = q.astype(F32).T
    if not split_halves:
        return [qt.astype(BF16)]
    row = lax.broadcasted_iota(jnp.int32, qt.shape, 0)
    return [jnp.where((row >= HEAD_DIM) == bool(half), qt, 0.0).astype(BF16) for half in range(2)]


def _flash_t(qts, k_ref, vt_ref, k_slices, v_rows, scr, i, t, past_mask=None):
    m_scr, acc_scr, s_scr, p_scr, a_scr = scr
    n = len(qts)
    ones = jnp.ones((ONES_ROWS, t), BF16)

    def qk(blk, s):
        b0 = pl.multiple_of(blk * t, t)
        return jnp.dot(k_ref[0, pl.ds(b0, t), k_slices[s]], qts[s], preferred_element_type=F32)

    def past_qk(blk, s):
        st = qk(blk, s)
        if past_mask is not None:
            st = jnp.where(past_mask(s, blk), st, NEG_INF)
        return st

    def pv(blk, s, p):
        vt = jnp.concatenate([vt_ref[0, blk, v_rows[s], :], ones], axis=0)
        return jnp.dot(vt, p, preferred_element_type=F32)

    krow = lax.broadcasted_iota(jnp.int32, (t, t), 0)
    qcol = lax.broadcasted_iota(jnp.int32, (t, t), 1)
    causal = krow <= qcol
    for s in range(n):
        st = jnp.where(causal, qk(i, s), NEG_INF)
        m = jnp.max(st, axis=0, keepdims=True)
        m_scr[s] = m
        p_scr[s] = jnp.exp2(st - m).astype(BF16)
        a_scr[s] = jnp.ones_like(m)
        acc_scr[s] = jnp.zeros(acc_scr.shape[1:], F32)
        s_scr[s] = past_qk(0, s)

    def body(jj, prev_blk):
        nxt = jnp.minimum(jj + 1, jnp.maximum(i - 1, 0))
        p_prev = [p_scr[s] for s in range(n)]
        a_prev = [a_scr[s] for s in range(n)]
        s_next = [past_qk(nxt, s) for s in range(n)]
        pvs = [pv(prev_blk, s, p_prev[s]) for s in range(n)]
        for s in range(n):
            st = s_scr[s]
            m_prev = m_scr[s]
            m_new = jnp.maximum(m_prev, jnp.max(st, axis=0, keepdims=True))
            a_scr[s] = jnp.exp2(m_prev - m_new)
            p_scr[s] = jnp.exp2(st - m_new).astype(BF16)
            m_scr[s] = m_new
        for s in range(n):
            s_scr[s] = s_next[s]
        for s in range(n):
            acc_scr[s] = a_prev[s] * acc_scr[s] + pvs[s]
        return jj

    last = lax.fori_loop(0, i, body, i)
    for s in range(n):
        acc_scr[s] = a_scr[s] * acc_scr[s] + pv(last, s, p_scr[s])


def _flash_scratch(n, nv, t):
    return [pltpu.VMEM((n, 1, t), F32), pltpu.VMEM((n, nv + ONES_ROWS, t), F32), pltpu.VMEM((n, t, t), F32),
            pltpu.VMEM((n, t, t), BF16), pltpu.VMEM((n, 1, t), F32)]


def _normalised(acc_scr, s, nv):
    return acc_scr[s, :nv, :] / acc_scr[s, nv:nv + 1, :]


def _half_mask(x, half):
    lane = _lane_iota((1, LANES))
    return jnp.where((lane >= HEAD_DIM) == bool(half), x, jnp.zeros_like(x))


def _mla_kernel(q_ref, k_ref, vt_ref, ot_ref, *scr, t, nh):
    i = pl.program_id(2)
    qs = [_q_tiles_t(q_ref[0, :, LANES * s:LANES * (s + 1)], False)[0] for s in range(nh)]
    ks = [slice(LANES * s, LANES * (s + 1)) for s in range(nh)]
    vr = [slice(HEAD_DIM * s, HEAD_DIM * (s + 1)) for s in range(nh)]
    _flash_t(qs, k_ref, vt_ref, ks, vr, scr, i, t)
    for s in range(nh):
        ot_ref[0, 0, vr[s], :] = _normalised(scr[1], s, HEAD_DIM).astype(ot_ref.dtype)


def _mla_attention(q, k, vt, nh):
    b, s, _ = q.shape
    _, nb, nv, t = vt.shape
    return pl.pallas_call(
        functools.partial(_mla_kernel, t=t, nh=nh),
        grid=(b, MLA_HEADS // nh, nb),
        in_specs=[pl.BlockSpec((1, t, nh * LANES), lambda bi, p, i: (bi, i, p)),
                  pl.BlockSpec((1, s, nh * LANES), lambda bi, p, i: (bi, 0, p)),
                  pl.BlockSpec((1, nb, nh * HEAD_DIM, t), lambda bi, p, i: (bi, 0, p, 0))],
        out_specs=pl.BlockSpec((1, 1, nh * HEAD_DIM, t), lambda bi, p, i: (bi, i, p, 0)),
        out_shape=jax.ShapeDtypeStruct((b, nb, nv, t), BF16),
        scratch_shapes=_flash_scratch(nh, HEAD_DIM, t),
        compiler_params=_cparams(("parallel", "parallel", "arbitrary")),
        name="mla_attention",
    )(q, k, vt)


def _diff_kernel(q_ref, k_ref, vt_ref, lam_ref, sub_ref, ot_ref, *scr, t, nh, lam_init):
    i = pl.program_id(2)
    qs, ks, vr = [], [], []
    for hd in range(nh):
        cols = slice(LANES * hd, LANES * (hd + 1))
        qs += _q_tiles_t(q_ref[0, :, cols], True)
        ks += [cols, cols]
        vr += [cols, cols]
    _flash_t(qs, k_ref, vt_ref, ks, vr, scr, i, t)
    lv = lam_ref[...]
    lam = (jnp.exp(jnp.sum(lv[0:1] * lv[1:2], keepdims=True))
           - jnp.exp(jnp.sum(lv[2:3] * lv[3:4], keepdims=True)) + lam_init)
    for hd in range(nh):
        o = _normalised(scr[1], 2 * hd, LANES) - lam * _normalised(scr[1], 2 * hd + 1, LANES)
        o = o * lax.rsqrt(jnp.mean(o * o, axis=0, keepdims=True) + NORM_EPS) * sub_ref[...]
        ot_ref[0, 0, LANES * hd:LANES * (hd + 1), :] = (o * (1.0 - lam_init)).astype(ot_ref.dtype)


def _diff_attention(q, k, vt, lam_rows, subln_col, lam_init, nh):
    b, s, w = q.shape
    _, nb, _, t = vt.shape
    return pl.pallas_call(
        functools.partial(_diff_kernel, t=t, nh=nh, lam_init=lam_init),
        grid=(b, DIFF_HEADS // nh, nb),
        in_specs=[pl.BlockSpec((1, t, nh * LANES), lambda bi, p, i: (bi, i, p)),
                  pl.BlockSpec((1, s, nh * LANES), lambda bi, p, i: (bi, 0, p)),
                  pl.BlockSpec((1, nb, nh * LANES, t), lambda bi, p, i: (bi, 0, p, 0)),
                  pl.BlockSpec(lam_rows.shape, lambda bi, p, i: (0, 0)),
                  pl.BlockSpec(subln_col.shape, lambda bi, p, i: (0, 0))],
        out_specs=pl.BlockSpec((1, 1, nh * LANES, t), lambda bi, p, i: (bi, i, p, 0)),
        out_shape=jax.ShapeDtypeStruct((b, nb, w, t), BF16),
        scratch_shapes=_flash_scratch(2 * nh, LANES, t),
        compiler_params=_cparams(("parallel", "parallel", "arbitrary")),
        name="diff_attention",
    )(q, k, vt, lam_rows, subln_col)


def _moba_kernel(q_ref, qf_ref, k_ref, vt_ref, kmean_ref, ot_ref, sel_scr, *scr, t, npair):
    i = pl.program_id(2)
    nb = kmean_ref.shape[1]
    brow = lax.broadcasted_iota(jnp.int32, (nb, t), 0).astype(F32)
    past = brow < i.astype(F32)
    qs, ks, vr = [], [], []
    for pr in range(npair):
        cols = slice(LANES * pr, LANES * (pr + 1))
        qf, kmean = qf_ref[0, :, cols], kmean_ref[0, :, cols]
        qs += _q_tiles_t(q_ref[0, :, cols], True)
        for half in range(2):
            s = 2 * pr + half
            gate = lax.dot_general(kmean, _half_mask(qf, half), (((1,), (1,)), ((), ())),
                                   preferred_element_type=F32, precision=lax.Precision.HIGHEST)
            gate = jnp.where(past, gate, NEG_INF)
            sel = jnp.zeros((nb, t), F32)
            for _ in range(MOBA_TOPK):
                top = jnp.max(gate, axis=0, keepdims=True)
                first = jnp.min(jnp.where(gate == top, brow, float(nb)), axis=0, keepdims=True)
                pick = brow == first
                sel = jnp.where(pick, 1.0, sel)
                gate = jnp.where(pick, 2 * NEG_INF, gate)
            sel_scr[s] = jnp.where(past, sel, 0.0)
            ks.append(cols)
            vr.append(slice(HEAD_DIM * s, HEAD_DIM * (s + 1)))
    _flash_t(qs, k_ref, vt_ref, ks, vr, scr, i, t,
             past_mask=lambda s, jj: sel_scr[s, pl.ds(jj, 1), :] > 0.5)
    for s in range(2 * npair):
        ot_ref[0, 0, vr[s], :] = _normalised(scr[1], s, HEAD_DIM).astype(ot_ref.dtype)


def _moba_attention(q, qf, k, vt, kmean, npair):
    b, s, w = q.shape
    _, nb, _, t = vt.shape
    nkm = kmean.shape[1]
    lanes = npair * LANES
    return pl.pallas_call(
        functools.partial(_moba_kernel, t=t, npair=npair),
        grid=(b, w // lanes, nb),
        in_specs=[pl.BlockSpec((1, t, lanes), lambda bi, p, i: (bi, i, p)),
                  pl.BlockSpec((1, t, lanes), lambda bi, p, i: (bi, i, p)),
                  pl.BlockSpec((1, s, lanes), lambda bi, p, i: (bi, 0, p)),
                  pl.BlockSpec((1, nb, lanes, t), lambda bi, p, i: (bi, 0, p, 0)),
                  pl.BlockSpec((1, nkm, lanes), lambda bi, p, i: (bi, 0, p))],
        out_specs=pl.BlockSpec((1, 1, lanes, t), lambda bi, p, i: (bi, i, p, 0)),
        out_shape=jax.ShapeDtypeStruct((b, nb, w, t), BF16),
        scratch_shapes=[pltpu.VMEM((2 * npair, nkm, t), F32)] + _flash_scratch(2 * npair, HEAD_DIM, t),
        compiler_params=_cparams(("parallel", "parallel", "arbitrary")),
        name="moba_attention",
    )(q, qf, k, vt, kmean)


def _sw_kernel(q_ref, kc_ref, vc_ref, kp_ref, vp_ref, o_ref, l_ref, *, nbk):
    i = pl.program_id(1)
    blk = SW_BLOCK
    qi = lax.broadcasted_iota(jnp.int32, (blk, 2 * blk), 0)
    kj = lax.broadcasted_iota(jnp.int32, (blk, 2 * blk), 1)
    band = (kj >= qi) & (kj <= qi + blk)
    lane = _lane_iota((1, LANES))
    for n in range(nbk):
        rows = slice(blk * n, blk * (n + 1))
        q = q_ref[0, rows, :]
        if n == 0:
            kprev, vprev = kp_ref[0], vp_ref[0]
            valid = band & ((kj >= blk) | (i > 0))
        else:
            prev = slice(blk * (n - 1), blk * n)
            kprev, vprev = kc_ref[0, prev, :], vc_ref[0, prev, :]
            valid = band
        kk = jnp.concatenate([kprev, kc_ref[0, rows, :]], axis=0)
        vv = jnp.concatenate([vprev, vc_ref[0, rows, :]], axis=0)
        for pair in range(DIL_W // LANES):
            sl = slice(LANES * pair, LANES * (pair + 1))
            outs, lses = [], []
            for half in range(2):
                sc = jnp.where(valid, _nt_dot(_half_mask(q[:, sl], half), kk[:, sl]), NEG_INF)
                m = jnp.max(sc, axis=-1, keepdims=True)
                p = jnp.exp2(sc - m)
                l = jnp.sum(p, axis=-1, keepdims=True)
                outs.append(jnp.dot(p.astype(BF16), vv[:, sl], preferred_element_type=F32) / l)
                lses.append(m + jnp.log2(l))
            o_ref[0, rows, sl] = jnp.where(lane < HEAD_DIM, outs[0], outs[1])
            l_ref[0, rows, sl] = jnp.where(lane < HEAD_DIM, lses[0], lses[1])


def _sliding_window(dil, nbk, name):
    lead, (m, w3) = dil.shape[:-2], dil.shape[-2:]
    seqs = dil.reshape((-1, m, w3))
    nbk = min(nbk, m // SW_BLOCK)
    rows = SW_BLOCK * nbk
    cur = lambda off: pl.BlockSpec((1, rows, DIL_W), lambda n, i: (n, i, off))
    prev = lambda off: pl.BlockSpec((1, SW_BLOCK, DIL_W), lambda n, i: (n, jnp.maximum(i * nbk - 1, 0), off))
    out_spec = pl.BlockSpec((1, rows, DIL_W), lambda n, i: (n, i, 0))
    o, l = pl.pallas_call(
        functools.partial(_sw_kernel, nbk=nbk),
        grid=(seqs.shape[0], m // rows),
        in_specs=[cur(0), cur(1), cur(2), prev(1), prev(2)],
        out_specs=[out_spec, out_spec],
        out_shape=[jax.ShapeDtypeStruct((seqs.shape[0], m, DIL_W), F32)] * 2,
        compiler_params=_cparams(("parallel", "arbitrary")),
        name=name,
    )(seqs, seqs, seqs, seqs, seqs)
    return o.reshape(lead + (m, DIL_W)), l.reshape(lead + (m, DIL_W))


def _even_out_kernel(x_ref, g_ref, oat_ref, o0_ref, o1_ref, o2_ref, l0_ref, l1_ref, l2_ref, w_ref, y_ref, tscr):
    tm = x_ref.shape[1]
    ncol = DIL_W // LANES

    def token_major(ref, slot):
        r = ref.shape[1]
        for c in range(r):
            for j in range(ncol):
                tscr[slot * ncol + j, pl.ds(c, tm // r, stride=r), :] = ref[0, c, :, LANES * j:LANES * (j + 1)]
        return jnp.concatenate([tscr[slot * ncol + j] for j in range(ncol)], axis=1)

    o0, l0 = o0_ref[0], l0_ref[0]
    o1, l1 = token_major(o1_ref, 0), token_major(l1_ref, 1)
    o2, l2 = token_major(o2_ref, 2), token_major(l2_ref, 3)
    top = jnp.maximum(jnp.maximum(l0, l1), l2)
    w0, w1, w2 = jnp.exp2(l0 - top), jnp.exp2(l1 - top), jnp.exp2(l2 - top)
    ob = (w0 * o0 + w1 * o1 + w2 * o2) / (w0 + w1 + w2)
    na = oat_ref.shape[2]
    y = _tn_dot(oat_ref[0, 0], w_ref[:na, :]) + jnp.dot(ob.astype(BF16), w_ref[na:, :], preferred_element_type=F32)
    y_ref[0] = x_ref[0] + g_ref[0] * y


def _t_spec(a):
    return pl.BlockSpec((1, 1) + a.shape[2:], lambda bi, i: (bi, i, 0, 0))


def _even_out(x, g1, oat, o_dil, l_dil, w_out):
    b, s, d = x.shape
    tm = oat.shape[-1]
    tok = lambda w: pl.BlockSpec((1, tm, w), lambda bi, i: (bi, i, 0))

    def dil_spec(a):
        if a.ndim == 3:
            return tok(DIL_W)
        r = a.shape[1]
        return pl.BlockSpec((1, r, tm // r, DIL_W), lambda bi, i: (bi, 0, i, 0))

    return pl.pallas_call(
        _even_out_kernel,
        grid=(b, s // tm),
        in_specs=[tok(d), pl.BlockSpec((1, 1, d), lambda bi, i: (bi, 0, 0)), _t_spec(oat)]
                 + [dil_spec(a) for a in (*o_dil, *l_dil)] + [pl.BlockSpec(w_out.shape, lambda bi, i: (0, 0))],
        out_specs=tok(d),
        out_shape=jax.ShapeDtypeStruct((b, s, d), F32),
        scratch_shapes=[pltpu.VMEM((4 * DIL_W // LANES, tm, LANES), F32)],
        compiler_params=_cparams(("parallel", "arbitrary")),
        name="even_out_proj",
    )(x, g1, oat, *o_dil, *l_dil, w_out)


def _odd_out_kernel(x_ref, g_ref, oct_ref, odt_ref, w_ref, y_ref):
    nc = oct_ref.shape[2]
    y = _tn_dot(oct_ref[0, 0], w_ref[:nc, :]) + _tn_dot(odt_ref[0, 0], w_ref[nc:, :])
    y_ref[0] = x_ref[0] + g_ref[0] * y


def _odd_out(x, g1, oct, odt, w_out):
    b, s, d = x.shape
    tm = oct.shape[-1]
    tok = lambda w: pl.BlockSpec((1, tm, w), lambda bi, i: (bi, i, 0))
    return pl.pallas_call(
        _odd_out_kernel,
        grid=(b, s // tm),
        in_specs=[tok(d), pl.BlockSpec((1, 1, d), lambda bi, i: (bi, 0, 0)), _t_spec(oct), _t_spec(odt),
                  pl.BlockSpec(w_out.shape, lambda bi, i: (0, 0))],
        out_specs=tok(d),
        out_shape=jax.ShapeDtypeStruct((b, s, d), F32),
        compiler_params=_cparams(("parallel", "arbitrary")),
        name="odd_out_proj",
    )(x, g1, oct, odt, w_out)


def _mlp_kernel(x_ref, sh_ref, sc_ref, g_ref, nw_ref, w1_ref, w2_ref, y_ref, *, tf):
    x = x_ref[0]
    h = (_rms(x, nw_ref[...]) * (1.0 + sc_ref[0]) + sh_ref[0]).astype(BF16)
    acc = jnp.zeros(x.shape, F32)
    for c in range(w1_ref.shape[1] // tf):
        a = jnp.maximum(jnp.dot(h, w1_ref[:, tf * c:tf * (c + 1)], preferred_element_type=F32), 0.0)
        acc = acc + jnp.dot((a * a).astype(BF16), w2_ref[tf * c:tf * (c + 1), :], preferred_element_type=F32)
    y_ref[0] = x + g_ref[0] * acc


def _mlp(x, sh, sc, g2, nw, w1, w2, tm, tf):
    b, s, d = x.shape
    tok = pl.BlockSpec((1, tm, d), lambda bi, i: (bi, i, 0))
    per_b = pl.BlockSpec((1, 1, d), lambda bi, i: (bi, 0, 0))
    const = lambda a: pl.BlockSpec(a.shape, lambda bi, i: (0, 0), pipeline_mode=pl.Buffered(1))
    return pl.pallas_call(
        functools.partial(_mlp_kernel, tf=tf),
        grid=(b, s // tm),
        in_specs=[tok, per_b, per_b, per_b, pl.BlockSpec(nw.shape, lambda bi, i: (0, 0)), const(w1), const(w2)],
        out_specs=tok,
        out_shape=jax.ShapeDtypeStruct((b, s, d), F32),
        compiler_params=_cparams(("parallel", "arbitrary")),
        name="mlp",
    )(x, sh, sc, g2, nw, w1, w2)


def _rope_tables(positions):
    pos = positions.astype(F32)[:, :, None]

    def cs(half):
        inv = ROPE_THETA ** (-jnp.arange(half, dtype=F32) / half)
        ang = pos * inv
        return jnp.cos(ang), jnp.sin(ang)

    c, s = cs(HEAD_DIM // 2)
    c64 = jnp.concatenate([c, c, c, c], axis=-1)
    s64 = jnp.concatenate([-s, s, -s, s], axis=-1)
    c, s = cs(MLA_ROPE // 2)
    ones = jnp.ones(pos.shape[:2] + (MLA_NOPE,), F32)
    pad = LANES - MLA_QK
    cm = jnp.concatenate([ones, c, c, jnp.ones(pos.shape[:2] + (pad,), F32)], axis=-1)
    sm = jnp.concatenate([0 * ones, -s, s, jnp.zeros(pos.shape[:2] + (pad,), F32)], axis=-1)
    return cm, sm, c64, s64


def _partner(n, width):
    idx = np.arange(n)
    return np.where(idx % width < width // 2, idx + width // 2, idx - width // 2)


def _even_weights(w_in, w_uq, w_ukv, qn, kn):
    o2 = MLA_Q_RANK + MLA_KV_RANK
    o3 = o2 + MLA_ROPE
    d = w_in.shape[0]
    p_rope = _partner(MLA_ROPE, MLA_ROPE)
    zeros = lambda n: jnp.zeros((d, n), w_in.dtype)
    kr = w_in[:, o2:o3]
    tail = LANES - MLA_QK
    krp = jnp.concatenate([zeros(MLA_NOPE), kr, zeros(tail)], axis=1)
    krp_rot = jnp.concatenate([zeros(MLA_NOPE), kr[:, p_rope], zeros(tail)], axis=1)
    dil = w_in[:, o3:].reshape(d, 3, len(DIL_CONFIGS), DIL_W)
    dil_main = dil.transpose(0, 2, 1, 3).reshape(d, -1)
    dil_rot = dil[:, :2][..., _partner(DIL_W, HEAD_DIM)].transpose(0, 2, 1, 3).reshape(d, -1)
    win = jnp.concatenate([w_in[:, :o2], krp, krp_rot, dil_main, dil_rot], axis=1)

    padq = ((0, 0), (0, 0), (0, tail))
    wuq_rot = jnp.concatenate([w_uq[:, :, :MLA_NOPE], w_uq[:, :, MLA_NOPE:][:, :, p_rope]], axis=-1)
    wuq = jnp.concatenate([jnp.pad(w_uq, padq).reshape(MLA_Q_RANK, -1),
                           jnp.pad(wuq_rot, padq).reshape(MLA_Q_RANK, -1)], axis=1)
    wuk = jnp.pad(w_ukv[:, :, :MLA_NOPE], ((0, 0), (0, 0), (0, LANES - MLA_NOPE))).reshape(MLA_KV_RANK, -1)
    wvt = w_ukv[:, :, MLA_NOPE:].reshape(MLA_KV_RANK, -1).T

    def gains(g):
        rot = jnp.concatenate([g[:MLA_NOPE], g[MLA_NOPE:][p_rope]])
        return jnp.stack([jnp.pad(g, (0, tail)), jnp.pad(rot, (0, tail))]).astype(F32)

    return win.astype(BF16), wuq.astype(BF16), wuk.astype(BF16), wvt.astype(BF16), gains(qn), gains(kn)


def _odd_weights(w_in):
    nqk = DIFF_HEADS * LANES
    nm = MOBA_HEADS * HEAD_DIM
    main = jnp.concatenate([w_in[:, :2 * nqk], w_in[:, 3 * nqk:3 * nqk + 2 * nm]], axis=1)
    win = jnp.concatenate([main, main[:, _partner(main.shape[1], HEAD_DIM)]], axis=1)
    wvt = jnp.concatenate([w_in[:, 2 * nqk:3 * nqk], w_in[:, 3 * nqk + 2 * nm:]], axis=1).T
    return win.astype(BF16), wvt.astype(BF16)


def _gains64(g):
    rot = g[_partner(HEAD_DIM, HEAD_DIM)]
    return jnp.stack([jnp.concatenate([g, g]), jnp.concatenate([rot, rot])]).astype(F32)


def kernel(x, c, positions, ada_w, ada_b, norm_mix, norm_mlp, mlp_w1, mlp_w2, even_w_in, even_w_out, mla_q_lat_norm, mla_kv_lat_norm, mla_w_uq, mla_w_ukv, mla_q_norm, mla_k_norm, dil_q_norm, dil_k_norm, odd_w_in, odd_w_out, diff_q_norm, diff_k_norm, diff_lambda, diff_subln, moba_q_norm, moba_k_norm):
    b, s, d = x.shape
    depth = ada_w.shape[0]
    t_attn = MOBA_BLOCK
    tm_mlp, tf = 512, 1024

    mod = _adaln(c, ada_w, ada_b)
    cm, sm, c64, s64 = _rope_tables(positions)

    for layer in range(depth):
        sh1, sc1, g1, sh2, sc2, g2 = [mod[layer, :, d * t:d * (t + 1)].reshape(b, 1, d) for t in range(6)]
        i = layer // 2
        if layer % 2 == 0:
            win, wuq, wuk, wvt, qn, kn = _even_weights(even_w_in[i], mla_w_uq[i], mla_w_ukv[i],
                                                       mla_q_norm[i], mla_k_norm[i])
            q, k, vt, *dils = _even_in(x, sh1, sc1, _row(norm_mix[layer]), win, wuq, wuk, wvt,
                                       _row(mla_q_lat_norm[i]), _row(mla_kv_lat_norm[i]), qn, kn,
                                       _gains64(dil_q_norm[i]), _gains64(dil_k_norm[i]), cm, sm, c64, s64, t_attn)
            o_at = _mla_attention(q, k, vt, nh=4)
            o_dil, l_dil = zip(*[_sliding_window(dg, 4, f"sliding_window_g{g}") for g, dg in enumerate(dils)])
            x = _even_out(x, g1, o_at, o_dil, l_dil, even_w_out[i].astype(BF16))
        else:
            lam_init = 0.8 - 0.6 * math.exp(-0.3 * layer)
            win, wvt = _odd_weights(odd_w_in[i])
            qc, kc, qm, qmf, km, vct, vmt, kmean = _odd_in(
                x, sh1, sc1, _row(norm_mix[layer]), win, wvt,
                _gains64(diff_q_norm[i]), _gains64(diff_k_norm[i]), _gains64(moba_q_norm[i]),
                _gains64(moba_k_norm[i]),
                c64, s64, t_attn)
            o_ct = _diff_attention(qc, kc, vct, diff_lambda[i].astype(F32),
                                   diff_subln[i].reshape(-1, 1).astype(F32), lam_init, nh=2)
            o_dt = _moba_attention(qm, qmf, km, vmt, kmean.reshape(b, s // MOBA_BLOCK, kmean.shape[-1]), npair=2)
            x = _odd_out(x, g1, o_ct, o_dt, odd_w_out[i].astype(BF16))
        x = _mlp(x, sh2, sc2, g2, _row(norm_mlp[layer]), mlp_w1[layer].astype(BF16), mlp_w2[layer].astype(BF16),
                 tm_mlp, tf)
    return x
```

```python
import functools
import math

import jax
import jax.numpy as jnp
import numpy as np
from jax import lax
from jax.experimental import pallas as pl
from jax.experimental.pallas import tpu as pltpu

F32 = jnp.float32
BF16 = jnp.bfloat16

LANES = 128
HEAD_DIM = 64
ROPE_THETA = 10000.0
NORM_EPS = 1e-6
NEG_INF = -1e30
LOG2E = math.log2(math.e)

MLA_HEADS = 8
MLA_Q_RANK = 384
MLA_KV_RANK = 256
MLA_NOPE = 64
MLA_ROPE = 32
MLA_QK = MLA_NOPE + MLA_ROPE
DIL_CONFIGS = ((128, 1), (512, 4), (2048, 16))
DIL_HEADS = 4
DIL_W = DIL_HEADS * HEAD_DIM
DIFF_HEADS = 4
MOBA_HEADS = 8
MOBA_BLOCK = 256
MOBA_TOPK = 3
SW_BLOCK = 128
ONES_ROWS = 16

VMEM_LIMIT = 56 * 1024 * 1024


def _cparams(sem):
    return pltpu.CompilerParams(dimension_semantics=sem, vmem_limit_bytes=VMEM_LIMIT)


def _nt_dot(a, b):
    return lax.dot_general(a, b, (((1,), (1,)), ((), ())), preferred_element_type=F32)


def _tn_dot(a, b):
    return lax.dot_general(a, b, (((0,), (0,)), ((), ())), preferred_element_type=F32)


def _rms(x, w):
    return x * lax.rsqrt(jnp.mean(x * x, axis=-1, keepdims=True) + NORM_EPS) * w


def _lane_iota(shape):
    return lax.broadcasted_iota(jnp.int32, shape, len(shape) - 1)


def _adaln_kernel(c_ref, w_ref, b_ref, o_ref):
    c = c_ref[...]
    cond = c * (1.0 / (1.0 + jnp.exp(-c)))
    o_ref[0] = jnp.dot(cond, w_ref[0], preferred_element_type=F32,
                       precision=lax.Precision.HIGHEST) + b_ref[0]


def _adaln(c, ada_w, ada_b):
    depth, d, n = ada_w.shape
    b = c.shape[0]
    tn = 1024
    return pl.pallas_call(
        _adaln_kernel,
        grid=(depth, n // tn),
        in_specs=[pl.BlockSpec((b, d), lambda l, j: (0, 0)),
                  pl.BlockSpec((1, d, tn), lambda l, j: (l, 0, j)),
                  pl.BlockSpec((1, 1, tn), lambda l, j: (l, 0, j))],
        out_specs=pl.BlockSpec((1, b, tn), lambda l, j: (l, 0, j)),
        out_shape=jax.ShapeDtypeStruct((depth, b, n), F32),
        compiler_params=_cparams(("parallel", "parallel")),
        name="adaln",
    )(c, ada_w, ada_b.reshape(depth, 1, n))


def _inv_rms64(x):
    lane = _lane_iota((1, LANES))
    lo = lane < HEAD_DIM
    sq = x * x
    s_lo = jnp.sum(jnp.where(lo, sq, 0.0), axis=-1, keepdims=True)
    s_hi = jnp.sum(jnp.where(lo, 0.0, sq), axis=-1, keepdims=True)
    return lax.rsqrt(jnp.where(lo, s_lo, s_hi) * (1.0 / HEAD_DIM) + NORM_EPS)


def _even_in_kernel(x_ref, sh_ref, sc_ref, nw_ref, win_ref, wuq_ref, wuk_ref, wvt_ref,
                    qlat_ref, kvlat_ref, qn_ref, kn_ref, dqn_ref, dkn_ref,
                    cm_ref, sm_ref, c64_ref, s64_ref,
                    q_out, k_out, vt_out, d0_out, d1_out, d2_out, dscr):
    x = x_ref[0]
    tm = x.shape[0]
    h = _rms(x, nw_ref[...]) * (1.0 + sc_ref[0]) + sh_ref[0]
    u = jnp.dot(h.astype(BF16), win_ref[...], preferred_element_type=F32)

    o1 = MLA_Q_RANK
    o2 = o1 + MLA_KV_RANK
    o_kr, o_krr, o_dil = o2, o2 + LANES, o2 + 2 * LANES
    n_dil = 3 * DIL_W * len(DIL_CONFIGS)
    o_rot = o_dil + n_dil
    nq = MLA_HEADS * LANES
    cqn = _rms(u[:, :o1], qlat_ref[...]).astype(BF16)
    qp = jnp.dot(cqn, wuq_ref[...], preferred_element_type=F32)
    ckvn = _rms(u[:, o1:o2], kvlat_ref[...]).astype(BF16)
    kvp = jnp.dot(ckvn, wuk_ref[...], preferred_element_type=F32)
    vt_out[0, 0] = _nt_dot(wvt_ref[...], ckvn).astype(BF16)
    kr, kr_rot = u[:, o_kr:o_kr + LANES], u[:, o_krr:o_krr + LANES]

    cm, sm = cm_ref[0], sm_ref[0]
    q_scale = MLA_QK ** -0.5 * LOG2E
    qgc, qgs = qn_ref[0:1] * cm * q_scale, qn_ref[1:2] * sm * q_scale
    kgc, kgs = kn_ref[0:1] * cm, kn_ref[1:2] * sm
    kr_term = kr_rot * kgs
    for hd in range(MLA_HEADS):
        sl = slice(LANES * hd, LANES * (hd + 1))
        qh = qp[:, sl]
        inv = lax.rsqrt(jnp.sum(qh * qh, -1, keepdims=True) * (1.0 / MLA_QK) + NORM_EPS)
        q_out[0, :, sl] = (inv * (qh * qgc + qp[:, nq + LANES * hd:nq + LANES * (hd + 1)] * qgs)).astype(BF16)
        kh = kvp[:, sl] + kr
        inv = lax.rsqrt(jnp.sum(kh * kh, -1, keepdims=True) * (1.0 / MLA_QK) + NORM_EPS)
        k_out[0, :, sl] = (inv * (kh * kgc + kr_term)).astype(BF16)

    c64, s64 = c64_ref[0], s64_ref[0]
    d_scale = HEAD_DIM ** -0.5 * LOG2E
    dqc, dqs = dqn_ref[0:1] * c64 * d_scale, dqn_ref[1:2] * s64 * d_scale
    dkc, dks = dkn_ref[0:1] * c64, dkn_ref[1:2] * s64
    ncol = DIL_W // LANES
    for g, d_out in enumerate((d0_out, d1_out, d2_out)):
        _, r = DIL_CONFIGS[g]
        base = o_dil + 3 * DIL_W * g
        rbase = o_rot + 2 * DIL_W * g
        for j in range(3 * ncol):
            xc = u[:, base + LANES * j:base + LANES * (j + 1)]
            if j < 2 * ncol:
                xr = u[:, rbase + LANES * j:rbase + LANES * (j + 1)]
                gc, gs = (dqc, dqs) if j < ncol else (dkc, dks)
                xc = _inv_rms64(xc) * (xc * gc + xr * gs)
            if r == 1:
                d_out[0, :, LANES * j:LANES * (j + 1)] = xc.astype(BF16)
            else:
                dscr[j] = xc
                for c in range(r):
                    d_out[0, c, :, LANES * j:LANES * (j + 1)] = dscr[j, pl.ds(c, tm // r, stride=r), :].astype(BF16)


def _row(v):
    return v.reshape(1, -1).astype(F32)


def _even_in(x, sh, sc, nw, win, wuq, wuk, wvt, qlat, kvlat, qn, kn, dqn, dkn, cm, sm, c64, s64, tm):
    b, s, d = x.shape
    nv = wvt.shape[0]
    tok = lambda w: pl.BlockSpec((1, tm, w), lambda bi, i: (bi, i, 0))
    per_b = pl.BlockSpec((1, 1, d), lambda bi, i: (bi, 0, 0))
    full = lambda a: pl.BlockSpec(a.shape, lambda bi, i: (0,) * a.ndim)
    dil_specs, dil_shapes = [], []
    for _, r in DIL_CONFIGS:
        if r == 1:
            dil_specs.append(tok(3 * DIL_W))
            dil_shapes.append(jax.ShapeDtypeStruct((b, s, 3 * DIL_W), BF16))
        else:
            dil_specs.append(pl.BlockSpec((1, r, tm // r, 3 * DIL_W), lambda bi, i: (bi, 0, i, 0)))
            dil_shapes.append(jax.ShapeDtypeStruct((b, r, s // r, 3 * DIL_W), BF16))
    return pl.pallas_call(
        _even_in_kernel,
        grid=(b, s // tm),
        in_specs=[tok(d), per_b, per_b, full(nw), full(win), full(wuq), full(wuk), full(wvt),
                  full(qlat), full(kvlat), full(qn), full(kn), full(dqn), full(dkn),
                  tok(LANES), tok(LANES), tok(LANES), tok(LANES)],
        out_specs=[tok(MLA_HEADS * LANES), tok(MLA_HEADS * LANES),
                   pl.BlockSpec((1, 1, nv, tm), lambda bi, i: (bi, i, 0, 0))] + dil_specs,
        out_shape=[jax.ShapeDtypeStruct((b, s, MLA_HEADS * LANES), BF16),
                   jax.ShapeDtypeStruct((b, s, MLA_HEADS * LANES), BF16),
                   jax.ShapeDtypeStruct((b, s // tm, nv, tm), BF16)] + dil_shapes,
        scratch_shapes=[pltpu.VMEM((3 * DIL_W // LANES, tm, LANES), F32)],
        compiler_params=_cparams(("parallel", "arbitrary")),
        name="even_in_proj",
    )(x, sh, sc, nw, win, wuq, wuk, wvt, qlat, kvlat, qn, kn, dqn, dkn, cm, sm, c64, s64)


def _odd_in_kernel(x_ref, sh_ref, sc_ref, nw_ref, win_ref, wvt_ref, dqn_ref, dkn_ref, mqn_ref, mkn_ref,
                   c64_ref, s64_ref,
                   qc_out, kc_out, qm_out, qmf_out, km_out, vct_out, vmt_out, kmean_out):
    x = x_ref[0]
    tm = x.shape[0]
    h = (_rms(x, nw_ref[...]) * (1.0 + sc_ref[0]) + sh_ref[0]).astype(BF16)
    u = jnp.dot(h, win_ref[...], preferred_element_type=F32)
    vt = _nt_dot(wvt_ref[...], h).astype(BF16)
    nvc = vct_out.shape[2]
    vct_out[0, 0] = vt[:nvc]
    vmt_out[0, 0] = vt[nvc:]
    c64, s64 = c64_ref[0], s64_ref[0]
    scale = HEAD_DIM ** -0.5 * LOG2E
    nqk = DIFF_HEADS * LANES
    nm = MOBA_HEADS * HEAD_DIM
    rot = 2 * nqk + 2 * nm

    def tables(gain_ref, mult):
        return gain_ref[0:1] * c64 * mult, gain_ref[1:2] * s64 * mult

    def roped(off, col, tab):
        a = off + LANES * col
        xc, xr = u[:, a:a + LANES], u[:, rot + a:rot + a + LANES]
        return _inv_rms64(xc) * (xc * tab[0] + xr * tab[1])

    t_qc, t_kc, t_qm, t_km = tables(dqn_ref, scale), tables(dkn_ref, 1.0), tables(mqn_ref, 1.0), tables(mkn_ref, 1.0)
    for col in range(nqk // LANES):
        sl = slice(LANES * col, LANES * (col + 1))
        qc_out[0, :, sl] = roped(0, col, t_qc).astype(BF16)
        kc_out[0, :, sl] = roped(nqk, col, t_kc).astype(BF16)

    base = 2 * nqk
    for col in range(nm // LANES):
        sl = slice(LANES * col, LANES * (col + 1))
        qm = roped(base, col, t_qm)
        qmf_out[0, :, sl] = qm
        qm_out[0, :, sl] = (qm * scale).astype(BF16)
        km = roped(base + nm, col, t_km)
        km_out[0, :, sl] = km.astype(BF16)
        for blk in range(tm // MOBA_BLOCK):
            rows = slice(MOBA_BLOCK * blk, MOBA_BLOCK * (blk + 1))
            kmean_out[0, 0, blk:blk + 1, sl] = jnp.mean(km[rows], axis=0, keepdims=True)


def _odd_in(x, sh, sc, nw, win, wvt, dqn, dkn, mqn, mkn, c64, s64, tm):
    b, s, d = x.shape
    tok = lambda w: pl.BlockSpec((1, tm, w), lambda bi, i: (bi, i, 0))
    per_b = pl.BlockSpec((1, 1, d), lambda bi, i: (bi, 0, 0))
    full = lambda a: pl.BlockSpec(a.shape, lambda bi, i: (0,) * a.ndim)
    w = DIFF_HEADS * LANES
    nv = wvt.shape[0] // 2
    nblk = tm // MOBA_BLOCK
    shp = lambda dt: jax.ShapeDtypeStruct((b, s, w), dt)
    vt_spec = pl.BlockSpec((1, 1, nv, tm), lambda bi, i: (bi, i, 0, 0))
    vt_shape = jax.ShapeDtypeStruct((b, s // tm, nv, tm), BF16)
    return pl.pallas_call(
        _odd_in_kernel,
        grid=(b, s // tm),
        in_specs=[tok(d), per_b, per_b, full(nw), full(win), full(wvt), full(dqn), full(dkn), full(mqn), full(mkn),
                  tok(LANES), tok(LANES)],
        out_specs=[tok(w)] * 5 + [vt_spec, vt_spec, pl.BlockSpec((1, 1, nblk, w), lambda bi, i: (bi, i, 0, 0))],
        out_shape=[shp(BF16), shp(BF16), shp(BF16), shp(F32), shp(BF16), vt_shape, vt_shape,
                   jax.ShapeDtypeStruct((b, s // tm, nblk, w), F32)],
        compiler_params=_cparams(("parallel", "arbitrary")),
        name="odd_in_proj",
    )(x, sh, sc, nw, win, wvt, dqn, dkn, mqn, mkn, c64, s64)


def _q_tiles_t(q, split_halves):
    qt = q.astype(F32).T
    if not split_halves:
        return [qt.astype(BF16)]
    row = lax.broadcasted_iota(jnp.int32, qt.shape, 0)
    return [jnp.where((row >= HEAD_DIM) == bool(half), qt, 0.0).astype(BF16) for half in range(2)]


def _flash_t(qts, k_ref, vt_ref, k_slices, v_rows, scr, i, t, past_mask=None):
    m_scr, acc_scr, s_scr, p_scr, a_scr = scr
    n = len(qts)
    ones = jnp.ones((ONES_ROWS, t), BF16)

    def qk(blk, s):
        b0 = pl.multiple_of(blk * t, t)
        return jnp.dot(k_ref[0, pl.ds(b0, t), k_slices[s]], qts[s], preferred_element_type=F32)

    def past_qk(blk, s):
        st = qk(blk, s)
        if past_mask is not None:
            st = jnp.where(past_mask(s, blk), st, NEG_INF)
        return st

    def pv(blk, s, p):
        vt = jnp.concatenate([vt_ref[0, blk, v_rows[s], :], ones], axis=0)
        return jnp.dot(vt, p, preferred_element_type=F32)

    krow = lax.broadcasted_iota(jnp.int32, (t, t), 0)
    qcol = lax.broadcasted_iota(jnp.int32, (t, t), 1)
    causal = krow <= qcol
    last_past = jnp.maximum(i - 1, 0)
    for s in range(n):
        st = jnp.where(causal, qk(i, s), NEG_INF)
        m = jnp.max(st, axis=0, keepdims=True)
        m_scr[s] = m
        p_scr[0, s] = jnp.exp2(st - m).astype(BF16)
        p_scr[1, s] = jnp.zeros((t, t), BF16)
        a_scr[0, s] = jnp.ones_like(m)
        a_scr[1, s] = jnp.ones_like(m)
        acc_scr[s] = jnp.zeros(acc_scr.shape[1:], F32)
        s_scr[0, s] = past_qk(0, s)
        s_scr[1, s] = past_qk(jnp.minimum(1, last_past), s)

    def softmax_stage(slot, valid, rs, ws):
        for s in range(n):
            st = s_scr[2 * rs + slot, s]
            cmax = jnp.max(st, axis=0, keepdims=True)
            if valid is not None:
                cmax = jnp.where(valid, cmax, NEG_INF)
            m_prev = m_scr[s]
            m_new = jnp.maximum(m_prev, cmax)
            a_scr[2 * ws + slot, s] = jnp.exp2(m_prev - m_new)
            p_scr[2 * ws + slot, s] = jnp.exp2(st - m_new).astype(BF16)
            m_scr[s] = m_new

    def body(kk, rs, ws):
        first = kk == 0
        pa_blk = jnp.where(first, i, 2 * kk - 2)
        pb_blk = jnp.where(first, i, 2 * kk - 1)
        p_prev = [[p_scr[2 * rs + sl, s] for s in range(n)] for sl in range(2)]
        a_prev = [[a_scr[2 * rs + sl, s] for s in range(n)] for sl in range(2)]
        s_next = [[past_qk(jnp.minimum(2 * kk + 2 + sl, last_past), s) for s in range(n)] for sl in range(2)]
        pvs = [[pv(blk, s, p_prev[sl][s]) for s in range(n)] for sl, blk in enumerate((pa_blk, pb_blk))]
        softmax_stage(0, None, rs, ws)
        softmax_stage(1, 2 * kk + 1 < i, rs, ws)
        for sl in range(2):
            for s in range(n):
                s_scr[2 * ws + sl, s] = s_next[sl][s]
        for s in range(n):
            acc_scr[s] = a_prev[1][s] * (a_prev[0][s] * acc_scr[s] + pvs[0][s]) + pvs[1][s]

    def two_bodies(kp, carry):
        body(2 * kp, 0, 1)
        body(2 * kp + 1, 1, 0)
        return carry

    nbody = (i + 1) // 2
    lax.fori_loop(0, nbody // 2, two_bodies, 0)

    @pl.when(nbody % 2 == 1)
    def _():
        body(nbody - 1, 0, 1)

    fs = nbody % 2
    kl = jnp.maximum(nbody - 1, 0)
    none = i == 0
    fa_blk = jnp.where(none, i, 2 * kl)
    fb_blk = jnp.where(none, i, jnp.minimum(2 * kl + 1, last_past))
    fb_scale = jnp.where(jnp.logical_or(none, i % 2 == 0), 1.0, 0.0)
    for s in range(n):
        acc_scr[s] = (a_scr[2 * fs + 1, s] * (a_scr[2 * fs, s] * acc_scr[s] + pv(fa_blk, s, p_scr[2 * fs, s]))
                      + fb_scale * pv(fb_blk, s, p_scr[2 * fs + 1, s]))


def _flash_scratch(n, nv, t):
    return [pltpu.VMEM((n, 1, t), F32), pltpu.VMEM((n, nv + ONES_ROWS, t), F32), pltpu.VMEM((4, n, t, t), F32),
            pltpu.VMEM((4, n, t, t), BF16), pltpu.VMEM((4, n, 1, t), F32)]


def _normalised(acc_scr, s, nv):
    return acc_scr[s, :nv, :] / acc_scr[s, nv:nv + 1, :]


def _half_mask(x, half):
    lane = _lane_iota((1, LANES))
    return jnp.where((lane >= HEAD_DIM) == bool(half), x, jnp.zeros_like(x))


def _mla_kernel(q_ref, k_ref, vt_ref, ot_ref, *scr, t, nh):
    i = pl.program_id(2)
    qs = [_q_tiles_t(q_ref[0, :, LANES * s:LANES * (s + 1)], False)[0] for s in range(nh)]
    ks = [slice(LANES * s, LANES * (s + 1)) for s in range(nh)]
    vr = [slice(HEAD_DIM * s, HEAD_DIM * (s + 1)) for s in range(nh)]
    _flash_t(qs, k_ref, vt_ref, ks, vr, scr, i, t)
    for s in range(nh):
        ot_ref[0, 0, vr[s], :] = _normalised(scr[1], s, HEAD_DIM).astype(ot_ref.dtype)


def _mla_attention(q, k, vt, nh):
    b, s, _ = q.shape
    _, nb, nv, t = vt.shape
    return pl.pallas_call(
        functools.partial(_mla_kernel, t=t, nh=nh),
        grid=(b, MLA_HEADS // nh, nb),
        in_specs=[pl.BlockSpec((1, t, nh * LANES), lambda bi, p, i: (bi, i, p)),
                  pl.BlockSpec((1, s, nh * LANES), lambda bi, p, i: (bi, 0, p)),
                  pl.BlockSpec((1, nb, nh * HEAD_DIM, t), lambda bi, p, i: (bi, 0, p, 0))],
        out_specs=pl.BlockSpec((1, 1, nh * HEAD_DIM, t), lambda bi, p, i: (bi, i, p, 0)),
        out_shape=jax.ShapeDtypeStruct((b, nb, nv, t), BF16),
        scratch_shapes=_flash_scratch(nh, HEAD_DIM, t),
        compiler_params=_cparams(("parallel", "parallel", "arbitrary")),
        name="mla_attention",
    )(q, k, vt)


def _diff_kernel(q_ref, k_ref, vt_ref, lam_ref, sub_ref, ot_ref, *scr, t, nh, lam_init):
    i = pl.program_id(2)
    qs, ks, vr = [], [], []
    for hd in range(nh):
        cols = slice(LANES * hd, LANES * (hd + 1))
        qs += _q_tiles_t(q_ref[0, :, cols], True)
        ks += [cols, cols]
        vr += [cols, cols]
    _flash_t(qs, k_ref, vt_ref, ks, vr, scr, i, t)
    lv = lam_ref[...]
    lam = (jnp.exp(jnp.sum(lv[0:1] * lv[1:2], keepdims=True))
           - jnp.exp(jnp.sum(lv[2:3] * lv[3:4], keepdims=True)) + lam_init)
    for hd in range(nh):
        o = _normalised(scr[1], 2 * hd, LANES) - lam * _normalised(scr[1], 2 * hd + 1, LANES)
        o = o * lax.rsqrt(jnp.mean(o * o, axis=0, keepdims=True) + NORM_EPS) * sub_ref[...]
        ot_ref[0, 0, LANES * hd:LANES * (hd + 1), :] = (o * (1.0 - lam_init)).astype(ot_ref.dtype)


def _diff_attention(q, k, vt, lam_rows, subln_col, lam_init, nh):
    b, s, w = q.shape
    _, nb, _, t = vt.shape
    return pl.pallas_call(
        functools.partial(_diff_kernel, t=t, nh=nh, lam_init=lam_init),
        grid=(b, DIFF_HEADS // nh, nb),
        in_specs=[pl.BlockSpec((1, t, nh * LANES), lambda bi, p, i: (bi, i, p)),
                  pl.BlockSpec((1, s, nh * LANES), lambda bi, p, i: (bi, 0, p)),
                  pl.BlockSpec((1, nb, nh * LANES, t), lambda bi, p, i: (bi, 0, p, 0)),
                  pl.BlockSpec(lam_rows.shape, lambda bi, p, i: (0, 0)),
                  pl.BlockSpec(subln_col.shape, lambda bi, p, i: (0, 0))],
        out_specs=pl.BlockSpec((1, 1, nh * LANES, t), lambda bi, p, i: (bi, i, p, 0)),
        out_shape=jax.ShapeDtypeStruct((b, nb, w, t), BF16),
        scratch_shapes=_flash_scratch(2 * nh, LANES, t),
        compiler_params=_cparams(("parallel", "parallel", "arbitrary")),
        name="diff_attention",
    )(q, k, vt, lam_rows, subln_col)


def _moba_kernel(q_ref, qf_ref, k_ref, vt_ref, kmean_ref, ot_ref, sel_scr, *scr, t, npair):
    i = pl.program_id(2)
    nb = kmean_ref.shape[1]
    brow = lax.broadcasted_iota(jnp.int32, (nb, t), 0).astype(F32)
    past = brow < i.astype(F32)
    qs, ks, vr = [], [], []
    for pr in range(npair):
        cols = slice(LANES * pr, LANES * (pr + 1))
        qf, kmean = qf_ref[0, :, cols], kmean_ref[0, :, cols]
        qs += _q_tiles_t(q_ref[0, :, cols], True)
        for half in range(2):
            s = 2 * pr + half
            gate = lax.dot_general(kmean, _half_mask(qf, half), (((1,), (1,)), ((), ())),
                                   preferred_element_type=F32, precision=lax.Precision.HIGHEST)
            gate = jnp.where(past, gate, NEG_INF)
            sel = jnp.zeros((nb, t), F32)
            for _ in range(MOBA_TOPK):
                top = jnp.max(gate, axis=0, keepdims=True)
                first = jnp.min(jnp.where(gate == top, brow, float(nb)), axis=0, keepdims=True)
                pick = brow == first
                sel = jnp.where(pick, 1.0, sel)
                gate = jnp.where(pick, 2 * NEG_INF, gate)
            sel_scr[s] = jnp.where(past, sel, 0.0)
            ks.append(cols)
            vr.append(slice(HEAD_DIM * s, HEAD_DIM * (s + 1)))
    _flash_t(qs, k_ref, vt_ref, ks, vr, scr, i, t,
             past_mask=lambda s, jj: sel_scr[s, pl.ds(jj, 1), :] > 0.5)
    for s in range(2 * npair):
        ot_ref[0, 0, vr[s], :] = _normalised(scr[1], s, HEAD_DIM).astype(ot_ref.dtype)


def _moba_attention(q, qf, k, vt, kmean, npair):
    b, s, w = q.shape
    _, nb, _, t = vt.shape
    nkm = kmean.shape[1]
    lanes = npair * LANES
    return pl.pallas_call(
        functools.partial(_moba_kernel, t=t, npair=npair),
        grid=(b, w // lanes, nb),
        in_specs=[pl.BlockSpec((1, t, lanes), lambda bi, p, i: (bi, i, p)),
                  pl.BlockSpec((1, t, lanes), lambda bi, p, i: (bi, i, p)),
                  pl.BlockSpec((1, s, lanes), lambda bi, p, i: (bi, 0, p)),
                  pl.BlockSpec((1, nb, lanes, t), lambda bi, p, i: (bi, 0, p, 0)),
                  pl.BlockSpec((1, nkm, lanes), lambda bi, p, i: (bi, 0, p))],
        out_specs=pl.BlockSpec((1, 1, lanes, t), lambda bi, p, i: (bi, i, p, 0)),
        out_shape=jax.ShapeDtypeStruct((b, nb, w, t), BF16),
        scratch_shapes=[pltpu.VMEM((2 * npair, nkm, t), F32)] + _flash_scratch(2 * npair, HEAD_DIM, t),
        compiler_params=_cparams(("parallel", "parallel", "arbitrary")),
        name="moba_attention",
    )(q, qf, k, vt, kmean)


def _sw_kernel(q_ref, kc_ref, vc_ref, kp_ref, vp_ref, o_ref, l_ref, *, nbk):
    i = pl.program_id(1)
    blk = SW_BLOCK
    qi = lax.broadcasted_iota(jnp.int32, (blk, 2 * blk), 0)
    kj = lax.broadcasted_iota(jnp.int32, (blk, 2 * blk), 1)
    band = (kj >= qi) & (kj <= qi + blk)
    lane = _lane_iota((1, LANES))
    for n in range(nbk):
        rows = slice(blk * n, blk * (n + 1))
        q = q_ref[0, rows, :]
        if n == 0:
            kprev, vprev = kp_ref[0], vp_ref[0]
            valid = band & ((kj >= blk) | (i > 0))
        else:
            prev = slice(blk * (n - 1), blk * n)
            kprev, vprev = kc_ref[0, prev, :], vc_ref[0, prev, :]
            valid = band
        kk = jnp.concatenate([kprev, kc_ref[0, rows, :]], axis=0)
        vv = jnp.concatenate([vprev, vc_ref[0, rows, :]], axis=0)
        for pair in range(DIL_W // LANES):
            sl = slice(LANES * pair, LANES * (pair + 1))
            outs, lses = [], []
            for half in range(2):
                sc = jnp.where(valid, _nt_dot(_half_mask(q[:, sl], half), kk[:, sl]), NEG_INF)
                m = jnp.max(sc, axis=-1, keepdims=True)
                p = jnp.exp2(sc - m)
                l = jnp.sum(p, axis=-1, keepdims=True)
                outs.append(jnp.dot(p.astype(BF16), vv[:, sl], preferred_element_type=F32) / l)
                lses.append(m + jnp.log2(l))
            o_ref[0, rows, sl] = jnp.where(lane < HEAD_DIM, outs[0], outs[1])
            l_ref[0, rows, sl] = jnp.where(lane < HEAD_DIM, lses[0], lses[1])


def _sliding_window(dil, nbk, name):
    lead, (m, w3) = dil.shape[:-2], dil.shape[-2:]
    seqs = dil.reshape((-1, m, w3))
    nbk = min(nbk, m // SW_BLOCK)
    rows = SW_BLOCK * nbk
    cur = lambda off: pl.BlockSpec((1, rows, DIL_W), lambda n, i: (n, i, off))
    prev = lambda off: pl.BlockSpec((1, SW_BLOCK, DIL_W), lambda n, i: (n, jnp.maximum(i * nbk - 1, 0), off))
    out_spec = pl.BlockSpec((1, rows, DIL_W), lambda n, i: (n, i, 0))
    o, l = pl.pallas_call(
        functools.partial(_sw_kernel, nbk=nbk),
        grid=(seqs.shape[0], m // rows),
        in_specs=[cur(0), cur(1), cur(2), prev(1), prev(2)],
        out_specs=[out_spec, out_spec],
        out_shape=[jax.ShapeDtypeStruct((seqs.shape[0], m, DIL_W), F32)] * 2,
        compiler_params=_cparams(("parallel", "arbitrary")),
        name=name,
    )(seqs, seqs, seqs, seqs, seqs)
    return o.reshape(lead + (m, DIL_W)), l.reshape(lead + (m, DIL_W))


def _even_out_kernel(x_ref, g_ref, oat_ref, o0_ref, o1_ref, o2_ref, l0_ref, l1_ref, l2_ref, w_ref, y_ref, tscr):
    tm = x_ref.shape[1]
    ncol = DIL_W // LANES

    def token_major(ref, slot):
        r = ref.shape[1]
        for c in range(r):
            for j in range(ncol):
                tscr[slot * ncol + j, pl.ds(c, tm // r, stride=r), :] = ref[0, c, :, LANES * j:LANES * (j + 1)]
        return jnp.concatenate([tscr[slot * ncol + j] for j in range(ncol)], axis=1)

    o0, l0 = o0_ref[0], l0_ref[0]
    o1, l1 = token_major(o1_ref, 0), token_major(l1_ref, 1)
    o2, l2 = token_major(o2_ref, 2), token_major(l2_ref, 3)
    top = jnp.maximum(jnp.maximum(l0, l1), l2)
    w0, w1, w2 = jnp.exp2(l0 - top), jnp.exp2(l1 - top), jnp.exp2(l2 - top)
    ob = (w0 * o0 + w1 * o1 + w2 * o2) / (w0 + w1 + w2)
    na = oat_ref.shape[2]
    y = _tn_dot(oat_ref[0, 0], w_ref[:na, :]) + jnp.dot(ob.astype(BF16), w_ref[na:, :], preferred_element_type=F32)
    y_ref[0] = x_ref[0] + g_ref[0] * y


def _t_spec(a):
    return pl.BlockSpec((1, 1) + a.shape[2:], lambda bi, i: (bi, i, 0, 0))


def _even_out(x, g1, oat, o_dil, l_dil, w_out):
    b, s, d = x.shape
    tm = oat.shape[-1]
    tok = lambda w: pl.BlockSpec((1, tm, w), lambda bi, i: (bi, i, 0))

    def dil_spec(a):
        if a.ndim == 3:
            return tok(DIL_W)
        r = a.shape[1]
        return pl.BlockSpec((1, r, tm // r, DIL_W), lambda bi, i: (bi, 0, i, 0))

    return pl.pallas_call(
        _even_out_kernel,
        grid=(b, s // tm),
        in_specs=[tok(d), pl.BlockSpec((1, 1, d), lambda bi, i: (bi, 0, 0)), _t_spec(oat)]
                 + [dil_spec(a) for a in (*o_dil, *l_dil)] + [pl.BlockSpec(w_out.shape, lambda bi, i: (0, 0))],
        out_specs=tok(d),
        out_shape=jax.ShapeDtypeStruct((b, s, d), F32),
        scratch_shapes=[pltpu.VMEM((4 * DIL_W // LANES, tm, LANES), F32)],
        compiler_params=_cparams(("parallel", "arbitrary")),
        name="even_out_proj",
    )(x, g1, oat, *o_dil, *l_dil, w_out)


def _odd_out_kernel(x_ref, g_ref, oct_ref, odt_ref, w_ref, y_ref):
    nc = oct_ref.shape[2]
    y = _tn_dot(oct_ref[0, 0], w_ref[:nc, :]) + _tn_dot(odt_ref[0, 0], w_ref[nc:, :])
    y_ref[0] = x_ref[0] + g_ref[0] * y


def _odd_out(x, g1, oct, odt, w_out):
    b, s, d = x.shape
    tm = oct.shape[-1]
    tok = lambda w: pl.BlockSpec((1, tm, w), lambda bi, i: (bi, i, 0))
    return pl.pallas_call(
        _odd_out_kernel,
        grid=(b, s // tm),
        in_specs=[tok(d), pl.BlockSpec((1, 1, d), lambda bi, i: (bi, 0, 0)), _t_spec(oct), _t_spec(odt),
                  pl.BlockSpec(w_out.shape, lambda bi, i: (0, 0))],
        out_specs=tok(d),
        out_shape=jax.ShapeDtypeStruct((b, s, d), F32),
        compiler_params=_cparams(("parallel", "arbitrary")),
        name="odd_out_proj",
    )(x, g1, oct, odt, w_out)


def _mlp_kernel(x_ref, sh_ref, sc_ref, g_ref, nw_ref, w1_ref, w2_ref, y_ref, *, tf):
    x = x_ref[0]
    h = (_rms(x, nw_ref[...]) * (1.0 + sc_ref[0]) + sh_ref[0]).astype(BF16)
    acc = jnp.zeros(x.shape, F32)
    for c in range(w1_ref.shape[1] // tf):
        a = jnp.maximum(jnp.dot(h, w1_ref[:, tf * c:tf * (c + 1)], preferred_element_type=F32), 0.0)
        acc = acc + jnp.dot((a * a).astype(BF16), w2_ref[tf * c:tf * (c + 1), :], preferred_element_type=F32)
    y_ref[0] = x + g_ref[0] * acc


def _mlp(x, sh, sc, g2, nw, w1, w2, tm, tf):
    b, s, d = x.shape
    tok = pl.BlockSpec((1, tm, d), lambda bi, i: (bi, i, 0))
    per_b = pl.BlockSpec((1, 1, d), lambda bi, i: (bi, 0, 0))
    const = lambda a: pl.BlockSpec(a.shape, lambda bi, i: (0, 0), pipeline_mode=pl.Buffered(1))
    return pl.pallas_call(
        functools.partial(_mlp_kernel, tf=tf),
        grid=(b, s // tm),
        in_specs=[tok, per_b, per_b, per_b, pl.BlockSpec(nw.shape, lambda bi, i: (0, 0)), const(w1), const(w2)],
        out_specs=tok,
        out_shape=jax.ShapeDtypeStruct((b, s, d), F32),
        compiler_params=_cparams(("parallel", "arbitrary")),
        name="mlp",
    )(x, sh, sc, g2, nw, w1, w2)


def _rope_tables(positions):
    pos = positions.astype(F32)[:, :, None]

    def cs(half):
        inv = ROPE_THETA ** (-jnp.arange(half, dtype=F32) / half)
        ang = pos * inv
        return jnp.cos(ang), jnp.sin(ang)

    c, s = cs(HEAD_DIM // 2)
    c64 = jnp.concatenate([c, c, c, c], axis=-1)
    s64 = jnp.concatenate([-s, s, -s, s], axis=-1)
    c, s = cs(MLA_ROPE // 2)
    ones = jnp.ones(pos.shape[:2] + (MLA_NOPE,), F32)
    pad = LANES - MLA_QK
    cm = jnp.concatenate([ones, c, c, jnp.ones(pos.shape[:2] + (pad,), F32)], axis=-1)
    sm = jnp.concatenate([0 * ones, -s, s, jnp.zeros(pos.shape[:2] + (pad,), F32)], axis=-1)
    return cm, sm, c64, s64


def _partner(n, width):
    idx = np.arange(n)
    return np.where(idx % width < width // 2, idx + width // 2, idx - width // 2)


def _even_weights(w_in, w_uq, w_ukv, qn, kn):
    o2 = MLA_Q_RANK + MLA_KV_RANK
    o3 = o2 + MLA_ROPE
    d = w_in.shape[0]
    p_rope = _partner(MLA_ROPE, MLA_ROPE)
    zeros = lambda n: jnp.zeros((d, n), w_in.dtype)
    kr = w_in[:, o2:o3]
    tail = LANES - MLA_QK
    krp = jnp.concatenate([zeros(MLA_NOPE), kr, zeros(tail)], axis=1)
    krp_rot = jnp.concatenate([zeros(MLA_NOPE), kr[:, p_rope], zeros(tail)], axis=1)
    dil = w_in[:, o3:].reshape(d, 3, len(DIL_CONFIGS), DIL_W)
    dil_main = dil.transpose(0, 2, 1, 3).reshape(d, -1)
    dil_rot = dil[:, :2][..., _partner(DIL_W, HEAD_DIM)].transpose(0, 2, 1, 3).reshape(d, -1)
    win = jnp.concatenate([w_in[:, :o2], krp, krp_rot, dil_main, dil_rot], axis=1)

    padq = ((0, 0), (0, 0), (0, tail))
    wuq_rot = jnp.concatenate([w_uq[:, :, :MLA_NOPE], w_uq[:, :, MLA_NOPE:][:, :, p_rope]], axis=-1)
    wuq = jnp.concatenate([jnp.pad(w_uq, padq).reshape(MLA_Q_RANK, -1),
                           jnp.pad(wuq_rot, padq).reshape(MLA_Q_RANK, -1)], axis=1)
    wuk = jnp.pad(w_ukv[:, :, :MLA_NOPE], ((0, 0), (0, 0), (0, LANES - MLA_NOPE))).reshape(MLA_KV_RANK, -1)
    wvt = w_ukv[:, :, MLA_NOPE:].reshape(MLA_KV_RANK, -1).T

    def gains(g):
        rot = jnp.concatenate([g[:MLA_NOPE], g[MLA_NOPE:][p_rope]])
        return jnp.stack([jnp.pad(g, (0, tail)), jnp.pad(rot, (0, tail))]).astype(F32)

    return win.astype(BF16), wuq.astype(BF16), wuk.astype(BF16), wvt.astype(BF16), gains(qn), gains(kn)


def _odd_weights(w_in):
    nqk = DIFF_HEADS * LANES
    nm = MOBA_HEADS * HEAD_DIM
    main = jnp.concatenate([w_in[:, :2 * nqk], w_in[:, 3 * nqk:3 * nqk + 2 * nm]], axis=1)
    win = jnp.concatenate([main, main[:, _partner(main.shape[1], HEAD_DIM)]], axis=1)
    wvt = jnp.concatenate([w_in[:, 2 * nqk:3 * nqk], w_in[:, 3 * nqk + 2 * nm:]], axis=1).T
    return win.astype(BF16), wvt.astype(BF16)


def _gains64(g):
    rot = g[_partner(HEAD_DIM, HEAD_DIM)]
    return jnp.stack([jnp.concatenate([g, g]), jnp.concatenate([rot, rot])]).astype(F32)


def kernel(x, c, positions, ada_w, ada_b, norm_mix, norm_mlp, mlp_w1, mlp_w2, even_w_in, even_w_out, mla_q_lat_norm, mla_kv_lat_norm, mla_w_uq, mla_w_ukv, mla_q_norm, mla_k_norm, dil_q_norm, dil_k_norm, odd_w_in, odd_w_out, diff_q_norm, diff_k_norm, diff_lambda, diff_subln, moba_q_norm, moba_k_norm):
    b, s, d = x.shape
    depth = ada_w.shape[0]
    t_attn = MOBA_BLOCK
    tm_mlp, tf = 512, 1024

    mod = _adaln(c, ada_w, ada_b)
    cm, sm, c64, s64 = _rope_tables(positions)

    for layer in range(depth):
        sh1, sc1, g1, sh2, sc2, g2 = [mod[layer, :, d * t:d * (t + 1)].reshape(b, 1, d) for t in range(6)]
        i = layer // 2
        if layer % 2 == 0:
            win, wuq, wuk, wvt, qn, kn = _even_weights(even_w_in[i], mla_w_uq[i], mla_w_ukv[i],
                                                       mla_q_norm[i], mla_k_norm[i])
            q, k, vt, *dils = _even_in(x, sh1, sc1, _row(norm_mix[layer]), win, wuq, wuk, wvt,
                                       _row(mla_q_lat_norm[i]), _row(mla_kv_lat_norm[i]), qn, kn,
                                       _gains64(dil_q_norm[i]), _gains64(dil_k_norm[i]), cm, sm, c64, s64, t_attn)
            o_at = _mla_attention(q, k, vt, nh=4)
            o_dil, l_dil = zip(*[_sliding_window(dg, 4, f"sliding_window_g{g}") for g, dg in enumerate(dils)])
            x = _even_out(x, g1, o_at, o_dil, l_dil, even_w_out[i].astype(BF16))
        else:
            lam_init = 0.8 - 0.6 * math.exp(-0.3 * layer)
            win, wvt = _odd_weights(odd_w_in[i])
            qc, kc, qm, qmf, km, vct, vmt, kmean = _odd_in(
                x, sh1, sc1, _row(norm_mix[layer]), win, wvt,
                _gains64(diff_q_norm[i]), _gains64(diff_k_norm[i]), _gains64(moba_q_norm[i]),
                _gains64(moba_k_norm[i]),
                c64, s64, t_attn)
            o_ct = _diff_attention(qc, kc, vct, diff_lambda[i].astype(F32),
                                   diff_subln[i].reshape(-1, 1).astype(F32), lam_init, nh=2)
            o_dt = _moba_attention(qm, qmf, km, vmt, kmean.reshape(b, s // MOBA_BLOCK, kmean.shape[-1]), npair=2)
            x = _odd_out(x, g1, o_ct, o_dt, odd_w_out[i].astype(BF16))
        x = _mlp(x, sh2, sc2, g2, _row(norm_mlp[layer]), mlp_w1[layer].astype(BF16), mlp_w2[layer].astype(BF16),
                 tm_mlp, tf)
    return x
```

```python
import functools
import math

import jax
import jax.numpy as jnp
import numpy as np
from jax import lax
from jax.experimental import pallas as pl
from jax.experimental.pallas import tpu as pltpu

F32 = jnp.float32
BF16 = jnp.bfloat16

LANES = 128
HEAD_DIM = 64
ROPE_THETA = 10000.0
NORM_EPS = 1e-6
NEG_INF = -1e30
LOG2E = math.log2(math.e)

MLA_HEADS = 8
MLA_Q_RANK = 384
MLA_KV_RANK = 256
MLA_NOPE = 64
MLA_ROPE = 32
MLA_QK = MLA_NOPE + MLA_ROPE
DIL_CONFIGS = ((128, 1), (512, 4), (2048, 16))
DIL_HEADS = 4
DIL_W = DIL_HEADS * HEAD_DIM
DIFF_HEADS = 4
MOBA_HEADS = 8
MOBA_BLOCK = 256
MOBA_TOPK = 3
SW_BLOCK = 128
ONES_ROWS = 16

VMEM_LIMIT = 56 * 1024 * 1024


def _cparams(sem):
    return pltpu.CompilerParams(dimension_semantics=sem, vmem_limit_bytes=VMEM_LIMIT)


def _nt_dot(a, b):
    return lax.dot_general(a, b, (((1,), (1,)), ((), ())), preferred_element_type=F32)


def _tn_dot(a, b):
    return lax.dot_general(a, b, (((0,), (0,)), ((), ())), preferred_element_type=F32)


def _rms(x, w):
    return x * lax.rsqrt(jnp.mean(x * x, axis=-1, keepdims=True) + NORM_EPS) * w


def _lane_iota(shape):
    return lax.broadcasted_iota(jnp.int32, shape, len(shape) - 1)


def _adaln_kernel(c_ref, w_ref, b_ref, o_ref):
    c = c_ref[...]
    cond = c * (1.0 / (1.0 + jnp.exp(-c)))
    o_ref[0] = jnp.dot(cond, w_ref[0], preferred_element_type=F32,
                       precision=lax.Precision.HIGHEST) + b_ref[0]


def _adaln(c, ada_w, ada_b):
    depth, d, n = ada_w.shape
    b = c.shape[0]
    tn = 1024
    return pl.pallas_call(
        _adaln_kernel,
        grid=(depth, n // tn),
        in_specs=[pl.BlockSpec((b, d), lambda l, j: (0, 0)),
                  pl.BlockSpec((1, d, tn), lambda l, j: (l, 0, j)),
                  pl.BlockSpec((1, 1, tn), lambda l, j: (l, 0, j))],
        out_specs=pl.BlockSpec((1, b, tn), lambda l, j: (l, 0, j)),
        out_shape=jax.ShapeDtypeStruct((depth, b, n), F32),
        compiler_params=_cparams(("parallel", "parallel")),
        name="adaln",
    )(c, ada_w, ada_b.reshape(depth, 1, n))


def _inv_rms64(x):
    lane = _lane_iota((1, LANES))
    lo = lane < HEAD_DIM
    sq = x * x
    s_lo = jnp.sum(jnp.where(lo, sq, 0.0), axis=-1, keepdims=True)
    s_hi = jnp.sum(jnp.where(lo, 0.0, sq), axis=-1, keepdims=True)
    return lax.rsqrt(jnp.where(lo, s_lo, s_hi) * (1.0 / HEAD_DIM) + NORM_EPS)


def _even_in_kernel(x_ref, sh_ref, sc_ref, nw_ref, win_ref, wuq_ref, wuk_ref, wvt_ref,
                    qlat_ref, kvlat_ref, qn_ref, kn_ref, dqn_ref, dkn_ref,
                    cm_ref, sm_ref, c64_ref, s64_ref,
                    q_out, k_out, vt_out, d0_out, d1_out, d2_out, dscr):
    x = x_ref[0]
    tm = x.shape[0]
    h = _rms(x, nw_ref[...]) * (1.0 + sc_ref[0]) + sh_ref[0]
    u = jnp.dot(h.astype(BF16), win_ref[...], preferred_element_type=F32)

    o1 = MLA_Q_RANK
    o2 = o1 + MLA_KV_RANK
    o_kr, o_krr, o_dil = o2, o2 + LANES, o2 + 2 * LANES
    n_dil = 3 * DIL_W * len(DIL_CONFIGS)
    o_rot = o_dil + n_dil
    nq = MLA_HEADS * LANES
    cqn = _rms(u[:, :o1], qlat_ref[...]).astype(BF16)
    qp = jnp.dot(cqn, wuq_ref[...], preferred_element_type=F32)
    ckvn = _rms(u[:, o1:o2], kvlat_ref[...]).astype(BF16)
    kvp = jnp.dot(ckvn, wuk_ref[...], preferred_element_type=F32)
    vt_out[0, 0] = _nt_dot(wvt_ref[...], ckvn).astype(BF16)
    kr, kr_rot = u[:, o_kr:o_kr + LANES], u[:, o_krr:o_krr + LANES]

    cm, sm = cm_ref[0], sm_ref[0]
    q_scale = MLA_QK ** -0.5 * LOG2E
    qgc, qgs = qn_ref[0:1] * cm * q_scale, qn_ref[1:2] * sm * q_scale
    kgc, kgs = kn_ref[0:1] * cm, kn_ref[1:2] * sm
    kr_term = kr_rot * kgs
    for hd in range(MLA_HEADS):
        sl = slice(LANES * hd, LANES * (hd + 1))
        qh = qp[:, sl]
        inv = lax.rsqrt(jnp.sum(qh * qh, -1, keepdims=True) * (1.0 / MLA_QK) + NORM_EPS)
        q_out[0, :, sl] = (inv * (qh * qgc + qp[:, nq + LANES * hd:nq + LANES * (hd + 1)] * qgs)).astype(BF16)
        kh = kvp[:, sl] + kr
        inv = lax.rsqrt(jnp.sum(kh * kh, -1, keepdims=True) * (1.0 / MLA_QK) + NORM_EPS)
        k_out[0, :, sl] = (inv * (kh * kgc + kr_term)).astype(BF16)

    c64, s64 = c64_ref[0], s64_ref[0]
    d_scale = HEAD_DIM ** -0.5 * LOG2E
    dqc, dqs = dqn_ref[0:1] * c64 * d_scale, dqn_ref[1:2] * s64 * d_scale
    dkc, dks = dkn_ref[0:1] * c64, dkn_ref[1:2] * s64
    ncol = DIL_W // LANES
    for g, d_out in enumerate((d0_out, d1_out, d2_out)):
        _, r = DIL_CONFIGS[g]
        base = o_dil + 3 * DIL_W * g
        rbase = o_rot + 2 * DIL_W * g
        for j in range(3 * ncol):
            xc = u[:, base + LANES * j:base + LANES * (j + 1)]
            if j < 2 * ncol:
                xr = u[:, rbase + LANES * j:rbase + LANES * (j + 1)]
                gc, gs = (dqc, dqs) if j < ncol else (dkc, dks)
                xc = _inv_rms64(xc) * (xc * gc + xr * gs)
            if r == 1:
                d_out[0, :, LANES * j:LANES * (j + 1)] = xc.astype(BF16)
            else:
                dscr[j] = xc
                for c in range(r):
                    d_out[0, c, :, LANES * j:LANES * (j + 1)] = dscr[j, pl.ds(c, tm // r, stride=r), :].astype(BF16)


def _row(v):
    return v.reshape(1, -1).astype(F32)


def _even_in(x, sh, sc, nw, win, wuq, wuk, wvt, qlat, kvlat, qn, kn, dqn, dkn, cm, sm, c64, s64, tm):
    b, s, d = x.shape
    nv = wvt.shape[0]
    tok = lambda w: pl.BlockSpec((1, tm, w), lambda bi, i: (bi, i, 0))
    per_b = pl.BlockSpec((1, 1, d), lambda bi, i: (bi, 0, 0))
    full = lambda a: pl.BlockSpec(a.shape, lambda bi, i: (0,) * a.ndim)
    dil_specs, dil_shapes = [], []
    for _, r in DIL_CONFIGS:
        if r == 1:
            dil_specs.append(tok(3 * DIL_W))
            dil_shapes.append(jax.ShapeDtypeStruct((b, s, 3 * DIL_W), BF16))
        else:
            dil_specs.append(pl.BlockSpec((1, r, tm // r, 3 * DIL_W), lambda bi, i: (bi, 0, i, 0)))
            dil_shapes.append(jax.ShapeDtypeStruct((b, r, s // r, 3 * DIL_W), BF16))
    return pl.pallas_call(
        _even_in_kernel,
        grid=(b, s // tm),
        in_specs=[tok(d), per_b, per_b, full(nw), full(win), full(wuq), full(wuk), full(wvt),
                  full(qlat), full(kvlat), full(qn), full(kn), full(dqn), full(dkn),
                  tok(LANES), tok(LANES), tok(LANES), tok(LANES)],
        out_specs=[tok(MLA_HEADS * LANES), tok(MLA_HEADS * LANES),
                   pl.BlockSpec((1, 1, nv, tm), lambda bi, i: (bi, i, 0, 0))] + dil_specs,
        out_shape=[jax.ShapeDtypeStruct((b, s, MLA_HEADS * LANES), BF16),
                   jax.ShapeDtypeStruct((b, s, MLA_HEADS * LANES), BF16),
                   jax.ShapeDtypeStruct((b, s // tm, nv, tm), BF16)] + dil_shapes,
        scratch_shapes=[pltpu.VMEM((3 * DIL_W // LANES, tm, LANES), F32)],
        compiler_params=_cparams(("parallel", "arbitrary")),
        name="even_in_proj",
    )(x, sh, sc, nw, win, wuq, wuk, wvt, qlat, kvlat, qn, kn, dqn, dkn, cm, sm, c64, s64)


def _odd_in_kernel(x_ref, sh_ref, sc_ref, nw_ref, win_ref, wvt_ref, dqn_ref, dkn_ref, mqn_ref, mkn_ref,
                   c64_ref, s64_ref,
                   qc_out, kc_out, qm_out, qmf_out, km_out, vct_out, vmt_out, kmean_out):
    x = x_ref[0]
    tm = x.shape[0]
    h = (_rms(x, nw_ref[...]) * (1.0 + sc_ref[0]) + sh_ref[0]).astype(BF16)
    u = jnp.dot(h, win_ref[...], preferred_element_type=F32)
    vt = _nt_dot(wvt_ref[...], h).astype(BF16)
    nvc = vct_out.shape[2]
    vct_out[0, 0] = vt[:nvc]
    vmt_out[0, 0] = vt[nvc:]
    c64, s64 = c64_ref[0], s64_ref[0]
    scale = HEAD_DIM ** -0.5 * LOG2E
    nqk = DIFF_HEADS * LANES
    nm = MOBA_HEADS * HEAD_DIM
    rot = 2 * nqk + 2 * nm

    def tables(gain_ref, mult):
        return gain_ref[0:1] * c64 * mult, gain_ref[1:2] * s64 * mult

    def roped(off, col, tab):
        a = off + LANES * col
        xc, xr = u[:, a:a + LANES], u[:, rot + a:rot + a + LANES]
        return _inv_rms64(xc) * (xc * tab[0] + xr * tab[1])

    t_qc, t_kc, t_qm, t_km = tables(dqn_ref, scale), tables(dkn_ref, 1.0), tables(mqn_ref, 1.0), tables(mkn_ref, 1.0)
    for col in range(nqk // LANES):
        sl = slice(LANES * col, LANES * (col + 1))
        qc_out[0, :, sl] = roped(0, col, t_qc).astype(BF16)
        kc_out[0, :, sl] = roped(nqk, col, t_kc).astype(BF16)

    base = 2 * nqk
    for col in range(nm // LANES):
        sl = slice(LANES * col, LANES * (col + 1))
        qm = roped(base, col, t_qm)
        qmf_out[0, :, sl] = qm
        qm_out[0, :, sl] = (qm * scale).astype(BF16)
        km = roped(base + nm, col, t_km)
        km_out[0, :, sl] = km.astype(BF16)
        for blk in range(tm // MOBA_BLOCK):
            rows = slice(MOBA_BLOCK * blk, MOBA_BLOCK * (blk + 1))
            kmean_out[0, 0, blk:blk + 1, sl] = jnp.mean(km[rows], axis=0, keepdims=True)


def _odd_in(x, sh, sc, nw, win, wvt, dqn, dkn, mqn, mkn, c64, s64, tm):
    b, s, d = x.shape
    tok = lambda w: pl.BlockSpec((1, tm, w), lambda bi, i: (bi, i, 0))
    per_b = pl.BlockSpec((1, 1, d), lambda bi, i: (bi, 0, 0))
    full = lambda a: pl.BlockSpec(a.shape, lambda bi, i: (0,) * a.ndim)
    w = DIFF_HEADS * LANES
    nv = wvt.shape[0] // 2
    nblk = tm // MOBA_BLOCK
    shp = lambda dt: jax.ShapeDtypeStruct((b, s, w), dt)
    vt_spec = pl.BlockSpec((1, 1, nv, tm), lambda bi, i: (bi, i, 0, 0))
    vt_shape = jax.ShapeDtypeStruct((b, s // tm, nv, tm), BF16)
    return pl.pallas_call(
        _odd_in_kernel,
        grid=(b, s // tm),
        in_specs=[tok(d), per_b, per_b, full(nw), full(win), full(wvt), full(dqn), full(dkn), full(mqn), full(mkn),
                  tok(LANES), tok(LANES)],
        out_specs=[tok(w)] * 5 + [vt_spec, vt_spec, pl.BlockSpec((1, 1, nblk, w), lambda bi, i: (bi, i, 0, 0))],
        out_shape=[shp(BF16), shp(BF16), shp(BF16), shp(F32), shp(BF16), vt_shape, vt_shape,
                   jax.ShapeDtypeStruct((b, s // tm, nblk, w), F32)],
        compiler_params=_cparams(("parallel", "arbitrary")),
        name="odd_in_proj",
    )(x, sh, sc, nw, win, wvt, dqn, dkn, mqn, mkn, c64, s64)


def _q_tiles_t(q, split_halves):
    qt = q.astype(F32).T
    if not split_halves:
        return [qt.astype(BF16)]
    row = lax.broadcasted_iota(jnp.int32, qt.shape, 0)
    return [jnp.where((row >= HEAD_DIM) == bool(half), qt, 0.0).astype(BF16) for half in range(2)]


def _flash_t(qts, k_ref, vt_ref, k_slices, v_rows, scr, i, t, past_mask=None):
    m_scr, acc_scr, s_scr, p_scr, a_scr = scr
    n = len(qts)
    ones = jnp.ones((ONES_ROWS, t), BF16)

    def qk(blk, s):
        b0 = pl.multiple_of(blk * t, t)
        return jnp.dot(k_ref[0, pl.ds(b0, t), k_slices[s]], qts[s], preferred_element_type=F32)

    def past_qk(blk, s):
        st = qk(blk, s)
        if past_mask is not None:
            st = jnp.where(past_mask(s, blk), st, NEG_INF)
        return st

    def pv(blk, s, p):
        vt = jnp.concatenate([vt_ref[0, blk, v_rows[s], :], ones], axis=0)
        return jnp.dot(vt, p, preferred_element_type=F32)

    krow = lax.broadcasted_iota(jnp.int32, (t, t), 0)
    qcol = lax.broadcasted_iota(jnp.int32, (t, t), 1)
    causal = krow <= qcol
    last_past = jnp.maximum(i - 1, 0)
    for s in range(n):
        st = jnp.where(causal, qk(i, s), NEG_INF)
        m = jnp.max(st, axis=0, keepdims=True)
        m_scr[s] = m
        p_scr[0, s] = jnp.exp2(st - m).astype(BF16)
        p_scr[1, s] = jnp.zeros((t, t), BF16)
        a_scr[0, s] = jnp.ones_like(m)
        a_scr[1, s] = jnp.ones_like(m)
        acc_scr[s] = jnp.zeros(acc_scr.shape[1:], F32)
        s_scr[0, s] = past_qk(0, s)
        s_scr[1, s] = past_qk(jnp.minimum(1, last_past), s)

    def softmax_stage(slot, valid, rs, ws):
        for s in range(n):
            st = s_scr[2 * rs + slot, s]
            cmax = jnp.max(st, axis=0, keepdims=True)
            if valid is not None:
                cmax = jnp.where(valid, cmax, NEG_INF)
            m_prev = m_scr[s]
            m_new = jnp.maximum(m_prev, cmax)
            a_scr[2 * ws + slot, s] = jnp.exp2(m_prev - m_new)
            p_scr[2 * ws + slot, s] = jnp.exp2(st - m_new).astype(BF16)
            m_scr[s] = m_new

    def body(kk, rs, ws):
        first = kk == 0
        pa_blk = jnp.where(first, i, 2 * kk - 2)
        pb_blk = jnp.where(first, i, 2 * kk - 1)
        p_prev = [[p_scr[2 * rs + sl, s] for s in range(n)] for sl in range(2)]
        a_prev = [[a_scr[2 * rs + sl, s] for s in range(n)] for sl in range(2)]
        s_next = [[past_qk(jnp.minimum(2 * kk + 2 + sl, last_past), s) for s in range(n)] for sl in range(2)]
        pvs = [[pv(blk, s, p_prev[sl][s]) for s in range(n)] for sl, blk in enumerate((pa_blk, pb_blk))]
        softmax_stage(0, None, rs, ws)
        softmax_stage(1, 2 * kk + 1 < i, rs, ws)
        for sl in range(2):
            for s in range(n):
                s_scr[2 * ws + sl, s] = s_next[sl][s]
        for s in range(n):
            acc_scr[s] = a_prev[1][s] * (a_prev[0][s] * acc_scr[s] + pvs[0][s]) + pvs[1][s]

    def two_bodies(kp, carry):
        body(2 * kp, 0, 1)
        body(2 * kp + 1, 1, 0)
        return carry

    nbody = (i + 1) // 2
    lax.fori_loop(0, nbody // 2, two_bodies, 0)

    @pl.when(nbody % 2 == 1)
    def _():
        body(nbody - 1, 0, 1)

    fs = nbody % 2
    kl = jnp.maximum(nbody - 1, 0)
    none = i == 0
    fa_blk = jnp.where(none, i, 2 * kl)
    fb_blk = jnp.where(none, i, jnp.minimum(2 * kl + 1, last_past))
    fb_scale = jnp.where(jnp.logical_or(none, i % 2 == 0), 1.0, 0.0)
    for s in range(n):
        acc_scr[s] = (a_scr[2 * fs + 1, s] * (a_scr[2 * fs, s] * acc_scr[s] + pv(fa_blk, s, p_scr[2 * fs, s]))
                      + fb_scale * pv(fb_blk, s, p_scr[2 * fs + 1, s]))


def _flash_scratch(n, nv, t):
    return [pltpu.VMEM((n, 1, t), F32), pltpu.VMEM((n, nv + ONES_ROWS, t), F32), pltpu.VMEM((4, n, t, t), F32),
            pltpu.VMEM((4, n, t, t), BF16), pltpu.VMEM((4, n, 1, t), F32)]


def _normalised(acc_scr, s, nv):
    return acc_scr[s, :nv, :] / acc_scr[s, nv:nv + 1, :]


def _half_mask(x, half):
    lane = _lane_iota((1, LANES))
    return jnp.where((lane >= HEAD_DIM) == bool(half), x, jnp.zeros_like(x))


def _mla_kernel(q_ref, k_ref, vt_ref, ot_ref, *scr, t, nh):
    i = pl.program_id(2)
    qs = [_q_tiles_t(q_ref[0, :, LANES * s:LANES * (s + 1)], False)[0] for s in range(nh)]
    ks = [slice(LANES * s, LANES * (s + 1)) for s in range(nh)]
    vr = [slice(HEAD_DIM * s, HEAD_DIM * (s + 1)) for s in range(nh)]
    _flash_t(qs, k_ref, vt_ref, ks, vr, scr, i, t)
    for s in range(nh):
        ot_ref[0, 0, vr[s], :] = _normalised(scr[1], s, HEAD_DIM).astype(ot_ref.dtype)


def _mla_attention(q, k, vt, nh):
    b, s, _ = q.shape
    _, nb, nv, t = vt.shape
    return pl.pallas_call(
        functools.partial(_mla_kernel, t=t, nh=nh),
        grid=(b, MLA_HEADS // nh, nb),
        in_specs=[pl.BlockSpec((1, t, nh * LANES), lambda bi, p, i: (bi, i, p)),
                  pl.BlockSpec((1, s, nh * LANES), lambda bi, p, i: (bi, 0, p)),
                  pl.BlockSpec((1, nb, nh * HEAD_DIM, t), lambda bi, p, i: (bi, 0, p, 0))],
        out_specs=pl.BlockSpec((1, 1, nh * HEAD_DIM, t), lambda bi, p, i: (bi, i, p, 0)),
        out_shape=jax.ShapeDtypeStruct((b, nb, nv, t), BF16),
        scratch_shapes=_flash_scratch(nh, HEAD_DIM, t),
        compiler_params=_cparams(("parallel", "parallel", "arbitrary")),
        name="mla_attention",
    )(q, k, vt)


def _diff_kernel(q_ref, k_ref, vt_ref, lam_ref, sub_ref, ot_ref, *scr, t, nh, lam_init):
    i = pl.program_id(2)
    qs, ks, vr = [], [], []
    for hd in range(nh):
        cols = slice(LANES * hd, LANES * (hd + 1))
        qs += _q_tiles_t(q_ref[0, :, cols], True)
        ks += [cols, cols]
        vr += [cols, cols]
    _flash_t(qs, k_ref, vt_ref, ks, vr, scr, i, t)
    lv = lam_ref[...]
    lam = (jnp.exp(jnp.sum(lv[0:1] * lv[1:2], keepdims=True))
           - jnp.exp(jnp.sum(lv[2:3] * lv[3:4], keepdims=True)) + lam_init)
    for hd in range(nh):
        o = _normalised(scr[1], 2 * hd, LANES) - lam * _normalised(scr[1], 2 * hd + 1, LANES)
        o = o * lax.rsqrt(jnp.mean(o * o, axis=0, keepdims=True) + NORM_EPS) * sub_ref[...]
        ot_ref[0, 0, LANES * hd:LANES * (hd + 1), :] = (o * (1.0 - lam_init)).astype(ot_ref.dtype)


def _diff_attention(q, k, vt, lam_rows, subln_col, lam_init, nh):
    b, s, w = q.shape
    _, nb, _, t = vt.shape
    return pl.pallas_call(
        functools.partial(_diff_kernel, t=t, nh=nh, lam_init=lam_init),
        grid=(b, DIFF_HEADS // nh, nb),
        in_specs=[pl.BlockSpec((1, t, nh * LANES), lambda bi, p, i: (bi, i, p)),
                  pl.BlockSpec((1, s, nh * LANES), lambda bi, p, i: (bi, 0, p)),
                  pl.BlockSpec((1, nb, nh * LANES, t), lambda bi, p, i: (bi, 0, p, 0)),
                  pl.BlockSpec(lam_rows.shape, lambda bi, p, i: (0, 0)),
                  pl.BlockSpec(subln_col.shape, lambda bi, p, i: (0, 0))],
        out_specs=pl.BlockSpec((1, 1, nh * LANES, t), lambda bi, p, i: (bi, i, p, 0)),
        out_shape=jax.ShapeDtypeStruct((b, nb, w, t), BF16),
        scratch_shapes=_flash_scratch(2 * nh, LANES, t),
        compiler_params=_cparams(("parallel", "parallel", "arbitrary")),
        name="diff_attention",
    )(q, k, vt, lam_rows, subln_col)


def _moba_kernel(q_ref, qf_ref, k_ref, vt_ref, kmean_ref, ot_ref, sel_scr, *scr, t, npair):
    i = pl.program_id(2)
    nb = kmean_ref.shape[1]
    brow = lax.broadcasted_iota(jnp.int32, (nb, t), 0).astype(F32)
    past = brow < i.astype(F32)
    qs, ks, vr = [], [], []
    for pr in range(npair):
        cols = slice(LANES * pr, LANES * (pr + 1))
        qf, kmean = qf_ref[0, :, cols], kmean_ref[0, :, cols]
        qs += _q_tiles_t(q_ref[0, :, cols], True)
        for half in range(2):
            s = 2 * pr + half
            gate = lax.dot_general(kmean, _half_mask(qf, half), (((1,), (1,)), ((), ())),
                                   preferred_element_type=F32, precision=lax.Precision.HIGHEST)
            gate = jnp.where(past, gate, NEG_INF)
            sel = jnp.zeros((nb, t), F32)
            for _ in range(MOBA_TOPK):
                top = jnp.max(gate, axis=0, keepdims=True)
                first = jnp.min(jnp.where(gate == top, brow, float(nb)), axis=0, keepdims=True)
                pick = brow == first
                sel = jnp.where(pick, 1.0, sel)
                gate = jnp.where(pick, 2 * NEG_INF, gate)
            sel_scr[s] = jnp.where(past, sel, 0.0)
            ks.append(cols)
            vr.append(slice(HEAD_DIM * s, HEAD_DIM * (s + 1)))
    _flash_t(qs, k_ref, vt_ref, ks, vr, scr, i, t,
             past_mask=lambda s, jj: sel_scr[s, pl.ds(jj, 1), :] > 0.5)
    for s in range(2 * npair):
        ot_ref[0, 0, vr[s], :] = _normalised(scr[1], s, HEAD_DIM).astype(ot_ref.dtype)


def _moba_attention(q, qf, k, vt, kmean, npair):
    b, s, w = q.shape
    _, nb, _, t = vt.shape
    nkm = kmean.shape[1]
    lanes = npair * LANES
    return pl.pallas_call(
        functools.partial(_moba_kernel, t=t, npair=npair),
        grid=(b, w // lanes, nb),
        in_specs=[pl.BlockSpec((1, t, lanes), lambda bi, p, i: (bi, i, p)),
                  pl.BlockSpec((1, t, lanes), lambda bi, p, i: (bi, i, p)),
                  pl.BlockSpec((1, s, lanes), lambda bi, p, i: (bi, 0, p)),
                  pl.BlockSpec((1, nb, lanes, t), lambda bi, p, i: (bi, 0, p, 0)),
                  pl.BlockSpec((1, nkm, lanes), lambda bi, p, i: (bi, 0, p))],
        out_specs=pl.BlockSpec((1, 1, lanes, t), lambda bi, p, i: (bi, i, p, 0)),
        out_shape=jax.ShapeDtypeStruct((b, nb, w, t), BF16),
        scratch_shapes=[pltpu.VMEM((2 * npair, nkm, t), F32)] + _flash_scratch(2 * npair, HEAD_DIM, t),
        compiler_params=_cparams(("parallel", "parallel", "arbitrary")),
        name="moba_attention",
    )(q, qf, k, vt, kmean)


def _sw_kernel(q_ref, kc_ref, vc_ref, kp_ref, vp_ref, o_ref, l_ref, *, nbk):
    i = pl.program_id(1)
    blk = SW_BLOCK
    qi = lax.broadcasted_iota(jnp.int32, (blk, 2 * blk), 0)
    kj = lax.broadcasted_iota(jnp.int32, (blk, 2 * blk), 1)
    band = (kj >= qi) & (kj <= qi + blk)
    lane = _lane_iota((1, LANES))
    heads = [(pair, half) for pair in range(DIL_W // LANES) for half in range(2)]

    def scores(n):
        rows = slice(blk * n, blk * (n + 1))
        q = q_ref[0, rows, :]
        if n == 0:
            kprev, valid = kp_ref[0], band & ((kj >= blk) | (i > 0))
        else:
            kprev, valid = kc_ref[0, blk * (n - 1):blk * n, :], band
        kk = jnp.concatenate([kprev, kc_ref[0, rows, :]], axis=0)
        out = []
        for pair, half in heads:
            sl = slice(LANES * pair, LANES * (pair + 1))
            out.append(jnp.where(valid, _nt_dot(_half_mask(q[:, sl], half), kk[:, sl]), NEG_INF))
        return out

    nxt = scores(0)
    for n in range(nbk):
        cur, rows = nxt, slice(blk * n, blk * (n + 1))
        if n + 1 < nbk:
            nxt = scores(n + 1)
        vprev = vp_ref[0] if n == 0 else vc_ref[0, blk * (n - 1):blk * n, :]
        vv = jnp.concatenate([vprev, vc_ref[0, rows, :]], axis=0)
        outs, lses = [], []
        for (pair, half), sc in zip(heads, cur):
            m = jnp.max(sc, axis=-1, keepdims=True)
            p = jnp.exp2(sc - m)
            l = jnp.sum(p, axis=-1, keepdims=True)
            sl = slice(LANES * pair, LANES * (pair + 1))
            outs.append(jnp.dot(p.astype(BF16), vv[:, sl], preferred_element_type=F32) / l)
            lses.append(m + jnp.log2(l))
        for pair in range(DIL_W // LANES):
            sl = slice(LANES * pair, LANES * (pair + 1))
            o_ref[0, rows, sl] = jnp.where(lane < HEAD_DIM, outs[2 * pair], outs[2 * pair + 1])
            l_ref[0, rows, sl] = jnp.where(lane < HEAD_DIM, lses[2 * pair], lses[2 * pair + 1])


def _sliding_window(dil, nbk, name):
    lead, (m, w3) = dil.shape[:-2], dil.shape[-2:]
    seqs = dil.reshape((-1, m, w3))
    nbk = min(nbk, m // SW_BLOCK)
    rows = SW_BLOCK * nbk
    cur = lambda off: pl.BlockSpec((1, rows, DIL_W), lambda n, i: (n, i, off))
    prev = lambda off: pl.BlockSpec((1, SW_BLOCK, DIL_W), lambda n, i: (n, jnp.maximum(i * nbk - 1, 0), off))
    out_spec = pl.BlockSpec((1, rows, DIL_W), lambda n, i: (n, i, 0))
    o, l = pl.pallas_call(
        functools.partial(_sw_kernel, nbk=nbk),
        grid=(seqs.shape[0], m // rows),
        in_specs=[cur(0), cur(1), cur(2), prev(1), prev(2)],
        out_specs=[out_spec, out_spec],
        out_shape=[jax.ShapeDtypeStruct((seqs.shape[0], m, DIL_W), F32)] * 2,
        compiler_params=_cparams(("parallel", "arbitrary")),
        name=name,
    )(seqs, seqs, seqs, seqs, seqs)
    return o.reshape(lead + (m, DIL_W)), l.reshape(lead + (m, DIL_W))


MLP_CHUNK = 1024


def _mlp_tail(y, mlp_refs):
    sh_ref, sc_ref, g_ref, nw_ref, w1_ref, w2_ref = mlp_refs
    h = (_rms(y, nw_ref[...]) * (1.0 + sc_ref[0]) + sh_ref[0]).astype(BF16)
    acc = jnp.zeros(y.shape, F32)
    for c in range(w1_ref.shape[1] // MLP_CHUNK):
        cols = slice(MLP_CHUNK * c, MLP_CHUNK * (c + 1))
        a = jnp.maximum(jnp.dot(h, w1_ref[:, cols], preferred_element_type=F32), 0.0)
        acc = acc + jnp.dot((a * a).astype(BF16), w2_ref[cols, :], preferred_element_type=F32)
    return y + g_ref[0] * acc


def _proj_t(ot_ref, w_rows):
    return jnp.concatenate([_tn_dot(ot_ref[0, j], w_rows) for j in range(ot_ref.shape[1])], axis=0)


def _mlp_specs(mlp_args, d):
    per_b = pl.BlockSpec((1, 1, d), lambda bi, i: (bi, 0, 0))
    const = lambda a: pl.BlockSpec(a.shape, lambda bi, i: (0, 0), pipeline_mode=pl.Buffered(1))
    sh, sc, g2, nw, w1, w2 = mlp_args
    return [per_b, per_b, per_b, pl.BlockSpec(nw.shape, lambda bi, i: (0, 0)), const(w1), const(w2)]


def _even_out_kernel(x_ref, g_ref, oat_ref, o0_ref, o1_ref, o2_ref, l0_ref, l1_ref, l2_ref, w_ref,
                     *rest):
    *mlp_refs, y_ref, tscr = rest
    tm = x_ref.shape[1]
    ncol = DIL_W // LANES

    def token_major(ref, slot):
        r = ref.shape[1]
        for c in range(r):
            for j in range(ncol):
                tscr[slot * ncol + j, pl.ds(c, tm // r, stride=r), :] = ref[0, c, :, LANES * j:LANES * (j + 1)]
        return jnp.concatenate([tscr[slot * ncol + j] for j in range(ncol)], axis=1)

    o0, l0 = o0_ref[0], l0_ref[0]
    o1, l1 = token_major(o1_ref, 0), token_major(l1_ref, 1)
    o2, l2 = token_major(o2_ref, 2), token_major(l2_ref, 3)
    top = jnp.maximum(jnp.maximum(l0, l1), l2)
    w0, w1, w2 = jnp.exp2(l0 - top), jnp.exp2(l1 - top), jnp.exp2(l2 - top)
    ob = (w0 * o0 + w1 * o1 + w2 * o2) / (w0 + w1 + w2)
    na = oat_ref.shape[2]
    y = _proj_t(oat_ref, w_ref[:na, :]) + jnp.dot(ob.astype(BF16), w_ref[na:, :], preferred_element_type=F32)
    y_ref[0] = _mlp_tail(x_ref[0] + g_ref[0] * y, mlp_refs)


def _t_spec(a, tm):
    return pl.BlockSpec((1, tm // a.shape[-1]) + a.shape[2:], lambda bi, i: (bi, i, 0, 0))


def _even_out(x, g1, oat, o_dil, l_dil, w_out, mlp_args, tm):
    b, s, d = x.shape
    tok = lambda w: pl.BlockSpec((1, tm, w), lambda bi, i: (bi, i, 0))

    def dil_spec(a):
        if a.ndim == 3:
            return tok(DIL_W)
        r = a.shape[1]
        return pl.BlockSpec((1, r, tm // r, DIL_W), lambda bi, i: (bi, 0, i, 0))

    return pl.pallas_call(
        _even_out_kernel,
        grid=(b, s // tm),
        in_specs=[tok(d), pl.BlockSpec((1, 1, d), lambda bi, i: (bi, 0, 0)), _t_spec(oat, tm)]
                 + [dil_spec(a) for a in (*o_dil, *l_dil)]
                 + [pl.BlockSpec(w_out.shape, lambda bi, i: (0, 0), pipeline_mode=pl.Buffered(1))]
                 + _mlp_specs(mlp_args, d),
        out_specs=tok(d),
        out_shape=jax.ShapeDtypeStruct((b, s, d), F32),
        scratch_shapes=[pltpu.VMEM((4 * DIL_W // LANES, tm, LANES), F32)],
        compiler_params=_cparams(("parallel", "arbitrary")),
        name="even_out_mlp",
    )(x, g1, oat, *o_dil, *l_dil, w_out, *mlp_args)


def _odd_out_kernel(x_ref, g_ref, oct_ref, odt_ref, w_ref, *rest):
    *mlp_refs, y_ref = rest
    nc = oct_ref.shape[2]
    y = _proj_t(oct_ref, w_ref[:nc, :]) + _proj_t(odt_ref, w_ref[nc:, :])
    y_ref[0] = _mlp_tail(x_ref[0] + g_ref[0] * y, mlp_refs)


def _odd_out(x, g1, oct, odt, w_out, mlp_args, tm):
    b, s, d = x.shape
    tok = lambda w: pl.BlockSpec((1, tm, w), lambda bi, i: (bi, i, 0))
    return pl.pallas_call(
        _odd_out_kernel,
        grid=(b, s // tm),
        in_specs=[tok(d), pl.BlockSpec((1, 1, d), lambda bi, i: (bi, 0, 0)), _t_spec(oct, tm), _t_spec(odt, tm),
                  pl.BlockSpec(w_out.shape, lambda bi, i: (0, 0), pipeline_mode=pl.Buffered(1))]
                 + _mlp_specs(mlp_args, d),
        out_specs=tok(d),
        out_shape=jax.ShapeDtypeStruct((b, s, d), F32),
        compiler_params=_cparams(("parallel", "arbitrary")),
        name="odd_out_mlp",
    )(x, g1, oct, odt, w_out, *mlp_args)


def _rope_tables(positions):
    pos = positions.astype(F32)[:, :, None]

    def cs(half):
        inv = ROPE_THETA ** (-jnp.arange(half, dtype=F32) / half)
        ang = pos * inv
        return lax.optimization_barrier((jnp.cos(ang), jnp.sin(ang)))

    c, s = cs(HEAD_DIM // 2)
    c64 = jnp.concatenate([c, c, c, c], axis=-1)
    s64 = jnp.concatenate([-s, s, -s, s], axis=-1)
    c, s = cs(MLA_ROPE // 2)
    ones = jnp.ones(pos.shape[:2] + (MLA_NOPE,), F32)
    pad = LANES - MLA_QK
    cm = jnp.concatenate([ones, c, c, jnp.ones(pos.shape[:2] + (pad,), F32)], axis=-1)
    sm = jnp.concatenate([0 * ones, -s, s, jnp.zeros(pos.shape[:2] + (pad,), F32)], axis=-1)
    return cm, sm, c64, s64


def _partner(n, width):
    idx = np.arange(n)
    return np.where(idx % width < width // 2, idx + width // 2, idx - width // 2)


def _even_weights(w_in, w_uq, w_ukv, qn, kn):
    o2 = MLA_Q_RANK + MLA_KV_RANK
    o3 = o2 + MLA_ROPE
    d = w_in.shape[0]
    p_rope = _partner(MLA_ROPE, MLA_ROPE)
    zeros = lambda n: jnp.zeros((d, n), w_in.dtype)
    kr = w_in[:, o2:o3]
    tail = LANES - MLA_QK
    krp = jnp.concatenate([zeros(MLA_NOPE), kr, zeros(tail)], axis=1)
    krp_rot = jnp.concatenate([zeros(MLA_NOPE), kr[:, p_rope], zeros(tail)], axis=1)
    dil = w_in[:, o3:].reshape(d, 3, len(DIL_CONFIGS), DIL_W)
    dil_main = dil.transpose(0, 2, 1, 3).reshape(d, -1)
    dil_rot = dil[:, :2][..., _partner(DIL_W, HEAD_DIM)].transpose(0, 2, 1, 3).reshape(d, -1)
    win = jnp.concatenate([w_in[:, :o2], krp, krp_rot, dil_main, dil_rot], axis=1)

    padq = ((0, 0), (0, 0), (0, tail))
    wuq_rot = jnp.concatenate([w_uq[:, :, :MLA_NOPE], w_uq[:, :, MLA_NOPE:][:, :, p_rope]], axis=-1)
    wuq = jnp.concatenate([jnp.pad(w_uq, padq).reshape(MLA_Q_RANK, -1),
                           jnp.pad(wuq_rot, padq).reshape(MLA_Q_RANK, -1)], axis=1)
    wuk = jnp.pad(w_ukv[:, :, :MLA_NOPE], ((0, 0), (0, 0), (0, LANES - MLA_NOPE))).reshape(MLA_KV_RANK, -1)
    wvt = w_ukv[:, :, MLA_NOPE:].reshape(MLA_KV_RANK, -1).T

    def gains(g):
        rot = jnp.concatenate([g[:MLA_NOPE], g[MLA_NOPE:][p_rope]])
        return jnp.stack([jnp.pad(g, (0, tail)), jnp.pad(rot, (0, tail))]).astype(F32)

    return win.astype(BF16), wuq.astype(BF16), wuk.astype(BF16), wvt.astype(BF16), gains(qn), gains(kn)


def _odd_weights(w_in):
    nqk = DIFF_HEADS * LANES
    nm = MOBA_HEADS * HEAD_DIM
    main = jnp.concatenate([w_in[:, :2 * nqk], w_in[:, 3 * nqk:3 * nqk + 2 * nm]], axis=1)
    win = jnp.concatenate([main, main[:, _partner(main.shape[1], HEAD_DIM)]], axis=1)
    wvt = jnp.concatenate([w_in[:, 2 * nqk:3 * nqk], w_in[:, 3 * nqk + 2 * nm:]], axis=1).T
    return win.astype(BF16), wvt.astype(BF16)


def _gains64(g):
    rot = g[_partner(HEAD_DIM, HEAD_DIM)]
    return jnp.stack([jnp.concatenate([g, g]), jnp.concatenate([rot, rot])]).astype(F32)


def kernel(x, c, positions, ada_w, ada_b, norm_mix, norm_mlp, mlp_w1, mlp_w2, even_w_in, even_w_out, mla_q_lat_norm, mla_kv_lat_norm, mla_w_uq, mla_w_ukv, mla_q_norm, mla_k_norm, dil_q_norm, dil_k_norm, odd_w_in, odd_w_out, diff_q_norm, diff_k_norm, diff_lambda, diff_subln, moba_q_norm, moba_k_norm):
    b, s, d = x.shape
    depth = ada_w.shape[0]
    t_attn = MOBA_BLOCK
    tm_out = 2 * t_attn

    mod = _adaln(c, ada_w, ada_b)
    cm, sm, c64, s64 = _rope_tables(positions)

    for layer in range(depth):
        sh1, sc1, g1, sh2, sc2, g2 = [mod[layer, :, d * t:d * (t + 1)].reshape(b, 1, d) for t in range(6)]
        mlp_args = (sh2, sc2, g2, _row(norm_mlp[layer]), mlp_w1[layer].astype(BF16), mlp_w2[layer].astype(BF16))
        i = layer // 2
        if layer % 2 == 0:
            win, wuq, wuk, wvt, qn, kn = _even_weights(even_w_in[i], mla_w_uq[i], mla_w_ukv[i],
                                                       mla_q_norm[i], mla_k_norm[i])
            q, k, vt, *dils = _even_in(x, sh1, sc1, _row(norm_mix[layer]), win, wuq, wuk, wvt,
                                       _row(mla_q_lat_norm[i]), _row(mla_kv_lat_norm[i]), qn, kn,
                                       _gains64(dil_q_norm[i]), _gains64(dil_k_norm[i]), cm, sm, c64, s64, t_attn)
            o_at = _mla_attention(q, k, vt, nh=4)
            o_dil, l_dil = zip(*[_sliding_window(dg, 4, f"sliding_window_g{g}") for g, dg in enumerate(dils)])
            x = _even_out(x, g1, o_at, o_dil, l_dil, even_w_out[i].astype(BF16), mlp_args, tm_out)
        else:
            lam_init = 0.8 - 0.6 * math.exp(-0.3 * layer)
            win, wvt = _odd_weights(odd_w_in[i])
            qc, kc, qm, qmf, km, vct, vmt, kmean = _odd_in(
                x, sh1, sc1, _row(norm_mix[layer]), win, wvt,
                _gains64(diff_q_norm[i]), _gains64(diff_k_norm[i]), _gains64(moba_q_norm[i]),
                _gains64(moba_k_norm[i]),
                c64, s64, t_attn)
            o_ct = _diff_attention(qc, kc, vct, diff_lambda[i].astype(F32),
                                   diff_subln[i].reshape(-1, 1).astype(F32), lam_init, nh=2)
            o_dt = _moba_attention(qm, qmf, km, vmt, kmean.reshape(b, s // MOBA_BLOCK, kmean.shape[-1]), npair=2)
            x = _odd_out(x, g1, o_ct, o_dt, odd_w_out[i].astype(BF16), mlp_args, tm_out)
    return x
```

```python
import functools
import math

import jax
import jax.numpy as jnp
import numpy as np
from jax import lax
from jax.experimental import pallas as pl
from jax.experimental.pallas import tpu as pltpu

F32 = jnp.float32
BF16 = jnp.bfloat16

LANES = 128
HEAD_DIM = 64
ROPE_THETA = 10000.0
NORM_EPS = 1e-6
NEG_INF = -1e30
LOG2E = math.log2(math.e)

MLA_HEADS = 8
MLA_Q_RANK = 384
MLA_KV_RANK = 256
MLA_NOPE = 64
MLA_ROPE = 32
MLA_QK = MLA_NOPE + MLA_ROPE
DIL_CONFIGS = ((128, 1), (512, 4), (2048, 16))
DIL_HEADS = 4
DIL_W = DIL_HEADS * HEAD_DIM
DIFF_HEADS = 4
MOBA_HEADS = 8
MOBA_BLOCK = 256
MOBA_TOPK = 3
SW_BLOCK = 128
ONES_ROWS = 16

VMEM_LIMIT = 56 * 1024 * 1024


def _cparams(sem):
    return pltpu.CompilerParams(dimension_semantics=sem, vmem_limit_bytes=VMEM_LIMIT)


def _nt_dot(a, b):
    return lax.dot_general(a, b, (((1,), (1,)), ((), ())), preferred_element_type=F32)


def _tn_dot(a, b):
    return lax.dot_general(a, b, (((0,), (0,)), ((), ())), preferred_element_type=F32)


def _rms(x, w):
    return x * lax.rsqrt(jnp.mean(x * x, axis=-1, keepdims=True) + NORM_EPS) * w


def _lane_iota(shape):
    return lax.broadcasted_iota(jnp.int32, shape, len(shape) - 1)


def _adaln_kernel(c_ref, w_ref, b_ref, o_ref):
    c = c_ref[...]
    cond = c * (1.0 / (1.0 + jnp.exp(-c)))
    o_ref[0] = jnp.dot(cond, w_ref[0], preferred_element_type=F32,
                       precision=lax.Precision.HIGHEST) + b_ref[0]


def _adaln(c, ada_w, ada_b):
    depth, d, n = ada_w.shape
    b = c.shape[0]
    tn = 1024
    return pl.pallas_call(
        _adaln_kernel,
        grid=(depth, n // tn),
        in_specs=[pl.BlockSpec((b, d), lambda l, j: (0, 0)),
                  pl.BlockSpec((1, d, tn), lambda l, j: (l, 0, j)),
                  pl.BlockSpec((1, 1, tn), lambda l, j: (l, 0, j))],
        out_specs=pl.BlockSpec((1, b, tn), lambda l, j: (l, 0, j)),
        out_shape=jax.ShapeDtypeStruct((depth, b, n), F32),
        compiler_params=_cparams(("parallel", "parallel")),
        name="adaln",
    )(c, ada_w, ada_b.reshape(depth, 1, n))


def _inv_rms64(x):
    lane = _lane_iota((1, LANES))
    lo = lane < HEAD_DIM
    sq = x * x
    s_lo = jnp.sum(jnp.where(lo, sq, 0.0), axis=-1, keepdims=True)
    s_hi = jnp.sum(jnp.where(lo, 0.0, sq), axis=-1, keepdims=True)
    return lax.rsqrt(jnp.where(lo, s_lo, s_hi) * (1.0 / HEAD_DIM) + NORM_EPS)


def _even_in_kernel(x_ref, sh_ref, sc_ref, nw_ref, win_ref, wuq_ref, wuk_ref, wvt_ref,
                    qlat_ref, kvlat_ref, qn_ref, kn_ref, dqn_ref, dkn_ref,
                    cm_ref, sm_ref, c64_ref, s64_ref,
                    q_out, k_out, vt_out, d0_out, d1_out, d2_out, dscr):
    x = x_ref[0]
    tm = x.shape[0]
    h = _rms(x, nw_ref[...]) * (1.0 + sc_ref[0]) + sh_ref[0]
    u = jnp.dot(h.astype(BF16), win_ref[...], preferred_element_type=F32)

    o1 = MLA_Q_RANK
    o2 = o1 + MLA_KV_RANK
    o_kr, o_krr, o_dil = o2, o2 + LANES, o2 + 2 * LANES
    n_dil = 3 * DIL_W * len(DIL_CONFIGS)
    o_rot = o_dil + n_dil
    nq = MLA_HEADS * LANES
    cqn = _rms(u[:, :o1], qlat_ref[...]).astype(BF16)
    qp = jnp.dot(cqn, wuq_ref[...], preferred_element_type=F32)
    ckvn = _rms(u[:, o1:o2], kvlat_ref[...]).astype(BF16)
    kvp = jnp.dot(ckvn, wuk_ref[...], preferred_element_type=F32)
    vt_out[0, 0] = _nt_dot(wvt_ref[...], ckvn).astype(BF16)
    kr, kr_rot = u[:, o_kr:o_kr + LANES], u[:, o_krr:o_krr + LANES]

    cm, sm = cm_ref[0], sm_ref[0]
    q_scale = MLA_QK ** -0.5 * LOG2E
    qgc, qgs = qn_ref[0:1] * cm * q_scale, qn_ref[1:2] * sm * q_scale
    kgc, kgs = kn_ref[0:1] * cm, kn_ref[1:2] * sm
    kr_term = kr_rot * kgs
    for hd in range(MLA_HEADS):
        sl = slice(LANES * hd, LANES * (hd + 1))
        qh = qp[:, sl]
        inv = lax.rsqrt(jnp.sum(qh * qh, -1, keepdims=True) * (1.0 / MLA_QK) + NORM_EPS)
        q_out[0, :, sl] = (inv * (qh * qgc + qp[:, nq + LANES * hd:nq + LANES * (hd + 1)] * qgs)).astype(BF16)
        kh = kvp[:, sl] + kr
        inv = lax.rsqrt(jnp.sum(kh * kh, -1, keepdims=True) * (1.0 / MLA_QK) + NORM_EPS)
        k_out[0, :, sl] = (inv * (kh * kgc + kr_term)).astype(BF16)

    c64, s64 = c64_ref[0], s64_ref[0]
    d_scale = HEAD_DIM ** -0.5 * LOG2E
    dqc, dqs = dqn_ref[0:1] * c64 * d_scale, dqn_ref[1:2] * s64 * d_scale
    dkc, dks = dkn_ref[0:1] * c64, dkn_ref[1:2] * s64
    ncol = DIL_W // LANES
    for g, d_out in enumerate((d0_out, d1_out, d2_out)):
        _, r = DIL_CONFIGS[g]
        base = o_dil + 3 * DIL_W * g
        rbase = o_rot + 2 * DIL_W * g
        for j in range(3 * ncol):
            xc = u[:, base + LANES * j:base + LANES * (j + 1)]
            if j < 2 * ncol:
                xr = u[:, rbase + LANES * j:rbase + LANES * (j + 1)]
                gc, gs = (dqc, dqs) if j < ncol else (dkc, dks)
                xc = _inv_rms64(xc) * (xc * gc + xr * gs)
            if r == 1:
                d_out[0, :, LANES * j:LANES * (j + 1)] = xc.astype(BF16)
            else:
                dscr[j] = xc
                for c in range(r):
                    d_out[0, c, :, LANES * j:LANES * (j + 1)] = dscr[j, pl.ds(c, tm // r, stride=r), :].astype(BF16)


def _row(v):
    return v.reshape(1, -1).astype(F32)


def _even_in(x, sh, sc, nw, win, wuq, wuk, wvt, qlat, kvlat, qn, kn, dqn, dkn, cm, sm, c64, s64, tm):
    b, s, d = x.shape
    nv = wvt.shape[0]
    tok = lambda w: pl.BlockSpec((1, tm, w), lambda bi, i: (bi, i, 0))
    per_b = pl.BlockSpec((1, 1, d), lambda bi, i: (bi, 0, 0))
    full = lambda a: pl.BlockSpec(a.shape, lambda bi, i: (0,) * a.ndim)
    dil_specs, dil_shapes = [], []
    for _, r in DIL_CONFIGS:
        if r == 1:
            dil_specs.append(tok(3 * DIL_W))
            dil_shapes.append(jax.ShapeDtypeStruct((b, s, 3 * DIL_W), BF16))
        else:
            dil_specs.append(pl.BlockSpec((1, r, tm // r, 3 * DIL_W), lambda bi, i: (bi, 0, i, 0)))
            dil_shapes.append(jax.ShapeDtypeStruct((b, r, s // r, 3 * DIL_W), BF16))
    return pl.pallas_call(
        _even_in_kernel,
        grid=(b, s // tm),
        in_specs=[tok(d), per_b, per_b, full(nw), full(win), full(wuq), full(wuk), full(wvt),
                  full(qlat), full(kvlat), full(qn), full(kn), full(dqn), full(dkn),
                  tok(LANES), tok(LANES), tok(LANES), tok(LANES)],
        out_specs=[tok(MLA_HEADS * LANES), tok(MLA_HEADS * LANES),
                   pl.BlockSpec((1, 1, nv, tm), lambda bi, i: (bi, i, 0, 0))] + dil_specs,
        out_shape=[jax.ShapeDtypeStruct((b, s, MLA_HEADS * LANES), BF16),
                   jax.ShapeDtypeStruct((b, s, MLA_HEADS * LANES), BF16),
                   jax.ShapeDtypeStruct((b, s // tm, nv, tm), BF16)] + dil_shapes,
        scratch_shapes=[pltpu.VMEM((3 * DIL_W // LANES, tm, LANES), F32)],
        compiler_params=_cparams(("parallel", "arbitrary")),
        name="even_in_proj",
    )(x, sh, sc, nw, win, wuq, wuk, wvt, qlat, kvlat, qn, kn, dqn, dkn, cm, sm, c64, s64)


def _odd_in_kernel(x_ref, sh_ref, sc_ref, nw_ref, win_ref, wvt_ref, dqn_ref, dkn_ref, mqn_ref, mkn_ref,
                   c64_ref, s64_ref,
                   qc_out, kc_out, qm_out, qmf_out, km_out, vct_out, vmt_out, kmean_out):
    x = x_ref[0]
    tm = x.shape[0]
    h = (_rms(x, nw_ref[...]) * (1.0 + sc_ref[0]) + sh_ref[0]).astype(BF16)
    u = jnp.dot(h, win_ref[...], preferred_element_type=F32)
    vt = _nt_dot(wvt_ref[...], h).astype(BF16)
    nvc = vct_out.shape[2]
    vct_out[0, 0] = vt[:nvc]
    vmt_out[0, 0] = vt[nvc:]
    c64, s64 = c64_ref[0], s64_ref[0]
    scale = HEAD_DIM ** -0.5 * LOG2E
    nqk = DIFF_HEADS * LANES
    nm = MOBA_HEADS * HEAD_DIM
    rot = 2 * nqk + 2 * nm

    def tables(gain_ref, mult):
        return gain_ref[0:1] * c64 * mult, gain_ref[1:2] * s64 * mult

    def roped(off, col, tab):
        a = off + LANES * col
        xc, xr = u[:, a:a + LANES], u[:, rot + a:rot + a + LANES]
        return _inv_rms64(xc) * (xc * tab[0] + xr * tab[1])

    t_qc, t_kc, t_qm, t_km = tables(dqn_ref, scale), tables(dkn_ref, 1.0), tables(mqn_ref, 1.0), tables(mkn_ref, 1.0)
    for col in range(nqk // LANES):
        sl = slice(LANES * col, LANES * (col + 1))
        qc_out[0, :, sl] = roped(0, col, t_qc).astype(BF16)
        kc_out[0, :, sl] = roped(nqk, col, t_kc).astype(BF16)

    base = 2 * nqk
    for col in range(nm // LANES):
        sl = slice(LANES * col, LANES * (col + 1))
        qm = roped(base, col, t_qm)
        qmf_out[0, :, sl] = qm
        qm_out[0, :, sl] = (qm * scale).astype(BF16)
        km = roped(base + nm, col, t_km)
        km_out[0, :, sl] = km.astype(BF16)
        for blk in range(tm // MOBA_BLOCK):
            rows = slice(MOBA_BLOCK * blk, MOBA_BLOCK * (blk + 1))
            kmean_out[0, 0, blk:blk + 1, sl] = jnp.mean(km[rows], axis=0, keepdims=True)


def _odd_in(x, sh, sc, nw, win, wvt, dqn, dkn, mqn, mkn, c64, s64, tm):
    b, s, d = x.shape
    tok = lambda w: pl.BlockSpec((1, tm, w), lambda bi, i: (bi, i, 0))
    per_b = pl.BlockSpec((1, 1, d), lambda bi, i: (bi, 0, 0))
    full = lambda a: pl.BlockSpec(a.shape, lambda bi, i: (0,) * a.ndim)
    w = DIFF_HEADS * LANES
    nv = wvt.shape[0] // 2
    nblk = tm // MOBA_BLOCK
    shp = lambda dt: jax.ShapeDtypeStruct((b, s, w), dt)
    vt_spec = pl.BlockSpec((1, 1, nv, tm), lambda bi, i: (bi, i, 0, 0))
    vt_shape = jax.ShapeDtypeStruct((b, s // tm, nv, tm), BF16)
    return pl.pallas_call(
        _odd_in_kernel,
        grid=(b, s // tm),
        in_specs=[tok(d), per_b, per_b, full(nw), full(win), full(wvt), full(dqn), full(dkn), full(mqn), full(mkn),
                  tok(LANES), tok(LANES)],
        out_specs=[tok(w)] * 5 + [vt_spec, vt_spec, pl.BlockSpec((1, 1, nblk, w), lambda bi, i: (bi, i, 0, 0))],
        out_shape=[shp(BF16), shp(BF16), shp(BF16), shp(F32), shp(BF16), vt_shape, vt_shape,
                   jax.ShapeDtypeStruct((b, s // tm, nblk, w), F32)],
        compiler_params=_cparams(("parallel", "arbitrary")),
        name="odd_in_proj",
    )(x, sh, sc, nw, win, wvt, dqn, dkn, mqn, mkn, c64, s64)


def _q_tiles_t(q, split_halves):
    qt = q.astype(F32).T
    if not split_halves:
        return [qt.astype(BF16)]
    row = lax.broadcasted_iota(jnp.int32, qt.shape, 0)
    return [jnp.where((row >= HEAD_DIM) == bool(half), qt, 0.0).astype(BF16) for half in range(2)]


def _flash_t(qts, k_ref, vt_ref, k_slices, v_rows, scr, i, t, past_mask=None):
    m_scr, acc_scr, s_scr, p_scr, a_scr = scr
    n = len(qts)
    ones = jnp.ones((ONES_ROWS, t), BF16)

    def qk(blk, s):
        b0 = pl.multiple_of(blk * t, t)
        return jnp.dot(k_ref[0, pl.ds(b0, t), k_slices[s]], qts[s], preferred_element_type=F32)

    def past_qk(blk, s):
        st = qk(blk, s)
        if past_mask is not None:
            st = jnp.where(past_mask(s, blk), st, NEG_INF)
        return st

    def pv(blk, s, p):
        vt = jnp.concatenate([vt_ref[0, blk, v_rows[s], :], ones], axis=0)
        return jnp.dot(vt, p, preferred_element_type=F32)

    krow = lax.broadcasted_iota(jnp.int32, (t, t), 0)
    qcol = lax.broadcasted_iota(jnp.int32, (t, t), 1)
    causal = krow <= qcol
    last_past = jnp.maximum(i - 1, 0)
    for s in range(n):
        st = jnp.where(causal, qk(i, s), NEG_INF)
        m = jnp.max(st, axis=0, keepdims=True)
        m_scr[s] = m
        p_scr[0, s] = jnp.exp2(st - m).astype(BF16)
        p_scr[1, s] = jnp.zeros((t, t), BF16)
        a_scr[0, s] = jnp.ones_like(m)
        a_scr[1, s] = jnp.ones_like(m)
        acc_scr[s] = jnp.zeros(acc_scr.shape[1:], F32)
        s_scr[0, s] = past_qk(0, s)
        s_scr[1, s] = past_qk(jnp.minimum(1, last_past), s)

    def softmax_stage(slot, valid, rs, ws):
        for s in range(n):
            st = s_scr[2 * rs + slot, s]
            cmax = jnp.max(st, axis=0, keepdims=True)
            if valid is not None:
                cmax = jnp.where(valid, cmax, NEG_INF)
            m_prev = m_scr[s]
            m_new = jnp.maximum(m_prev, cmax)
            a_scr[2 * ws + slot, s] = jnp.exp2(m_prev - m_new)
            p_scr[2 * ws + slot, s] = jnp.exp2(st - m_new).astype(BF16)
            m_scr[s] = m_new

    def body(kk, rs, ws):
        first = kk == 0
        pa_blk = jnp.where(first, i, 2 * kk - 2)
        pb_blk = jnp.where(first, i, 2 * kk - 1)
        p_prev = [[p_scr[2 * rs + sl, s] for s in range(n)] for sl in range(2)]
        a_prev = [[a_scr[2 * rs + sl, s] for s in range(n)] for sl in range(2)]
        s_next = [[past_qk(jnp.minimum(2 * kk + 2 + sl, last_past), s) for s in range(n)] for sl in range(2)]
        pvs = [[pv(blk, s, p_prev[sl][s]) for s in range(n)] for sl, blk in enumerate((pa_blk, pb_blk))]
        softmax_stage(0, None, rs, ws)
        softmax_stage(1, 2 * kk + 1 < i, rs, ws)
        for sl in range(2):
            for s in range(n):
                s_scr[2 * ws + sl, s] = s_next[sl][s]
        for s in range(n):
            acc_scr[s] = a_prev[1][s] * (a_prev[0][s] * acc_scr[s] + pvs[0][s]) + pvs[1][s]

    def two_bodies(kp, carry):
        body(2 * kp, 0, 1)
        body(2 * kp + 1, 1, 0)
        return carry

    nbody = (i + 1) // 2
    lax.fori_loop(0, nbody // 2, two_bodies, 0)

    @pl.when(nbody % 2 == 1)
    def _():
        body(nbody - 1, 0, 1)

    fs = nbody % 2
    kl = jnp.maximum(nbody - 1, 0)
    none = i == 0
    fa_blk = jnp.where(none, i, 2 * kl)
    fb_blk = jnp.where(none, i, jnp.minimum(2 * kl + 1, last_past))
    fb_scale = jnp.where(jnp.logical_or(none, i % 2 == 0), 1.0, 0.0)
    for s in range(n):
        acc_scr[s] = (a_scr[2 * fs + 1, s] * (a_scr[2 * fs, s] * acc_scr[s] + pv(fa_blk, s, p_scr[2 * fs, s]))
                      + fb_scale * pv(fb_blk, s, p_scr[2 * fs + 1, s]))


def _flash_scratch(n, nv, t):
    return [pltpu.VMEM((n, 1, t), F32), pltpu.VMEM((n, nv + ONES_ROWS, t), F32), pltpu.VMEM((4, n, t, t), F32),
            pltpu.VMEM((4, n, t, t), BF16), pltpu.VMEM((4, n, 1, t), F32)]


def _normalised(acc_scr, s, nv):
    return acc_scr[s, :nv, :] / acc_scr[s, nv:nv + 1, :]


def _half_mask(x, half):
    lane = _lane_iota((1, LANES))
    return jnp.where((lane >= HEAD_DIM) == bool(half), x, jnp.zeros_like(x))


def _mla_kernel(q_ref, k_ref, vt_ref, ot_ref, *scr, t, nh):
    i = pl.program_id(2)
    qs = [_q_tiles_t(q_ref[0, :, LANES * s:LANES * (s + 1)], False)[0] for s in range(nh)]
    ks = [slice(LANES * s, LANES * (s + 1)) for s in range(nh)]
    vr = [slice(HEAD_DIM * s, HEAD_DIM * (s + 1)) for s in range(nh)]
    _flash_t(qs, k_ref, vt_ref, ks, vr, scr, i, t)
    for s in range(nh):
        ot_ref[0, 0, vr[s], :] = _normalised(scr[1], s, HEAD_DIM).astype(ot_ref.dtype)


def _mla_attention(q, k, vt, nh):
    b, s, _ = q.shape
    _, nb, nv, t = vt.shape
    return pl.pallas_call(
        functools.partial(_mla_kernel, t=t, nh=nh),
        grid=(b, MLA_HEADS // nh, nb),
        in_specs=[pl.BlockSpec((1, t, nh * LANES), lambda bi, p, i: (bi, i, p)),
                  pl.BlockSpec((1, s, nh * LANES), lambda bi, p, i: (bi, 0, p)),
                  pl.BlockSpec((1, nb, nh * HEAD_DIM, t), lambda bi, p, i: (bi, 0, p, 0))],
        out_specs=pl.BlockSpec((1, 1, nh * HEAD_DIM, t), lambda bi, p, i: (bi, i, p, 0)),
        out_shape=jax.ShapeDtypeStruct((b, nb, nv, t), BF16),
        scratch_shapes=_flash_scratch(nh, HEAD_DIM, t),
        compiler_params=_cparams(("parallel", "parallel", "arbitrary")),
        name="mla_attention",
    )(q, k, vt)


def _diff_kernel(q_ref, k_ref, vt_ref, lam_ref, sub_ref, ot_ref, *scr, t, nh, lam_init):
    i = pl.program_id(2)
    qs, ks, vr = [], [], []
    for hd in range(nh):
        cols = slice(LANES * hd, LANES * (hd + 1))
        qs += _q_tiles_t(q_ref[0, :, cols], True)
        ks += [cols, cols]
        vr += [cols, cols]
    _flash_t(qs, k_ref, vt_ref, ks, vr, scr, i, t)
    lv = lam_ref[...]
    lam = (jnp.exp(jnp.sum(lv[0:1] * lv[1:2], keepdims=True))
           - jnp.exp(jnp.sum(lv[2:3] * lv[3:4], keepdims=True)) + lam_init)
    for hd in range(nh):
        o = _normalised(scr[1], 2 * hd, LANES) - lam * _normalised(scr[1], 2 * hd + 1, LANES)
        o = o * lax.rsqrt(jnp.mean(o * o, axis=0, keepdims=True) + NORM_EPS) * sub_ref[...]
        ot_ref[0, 0, LANES * hd:LANES * (hd + 1), :] = (o * (1.0 - lam_init)).astype(ot_ref.dtype)


def _diff_attention(q, k, vt, lam_rows, subln_col, lam_init, nh):
    b, s, w = q.shape
    _, nb, _, t = vt.shape
    return pl.pallas_call(
        functools.partial(_diff_kernel, t=t, nh=nh, lam_init=lam_init),
        grid=(b, DIFF_HEADS // nh, nb),
        in_specs=[pl.BlockSpec((1, t, nh * LANES), lambda bi, p, i: (bi, i, p)),
                  pl.BlockSpec((1, s, nh * LANES), lambda bi, p, i: (bi, 0, p)),
                  pl.BlockSpec((1, nb, nh * LANES, t), lambda bi, p, i: (bi, 0, p, 0)),
                  pl.BlockSpec(lam_rows.shape, lambda bi, p, i: (0, 0)),
                  pl.BlockSpec(subln_col.shape, lambda bi, p, i: (0, 0))],
        out_specs=pl.BlockSpec((1, 1, nh * LANES, t), lambda bi, p, i: (bi, i, p, 0)),
        out_shape=jax.ShapeDtypeStruct((b, nb, w, t), BF16),
        scratch_shapes=_flash_scratch(2 * nh, LANES, t),
        compiler_params=_cparams(("parallel", "parallel", "arbitrary")),
        name="diff_attention",
    )(q, k, vt, lam_rows, subln_col)


def _moba_kernel(q_ref, qf_ref, k_ref, vt_ref, kmean_ref, ot_ref, sel_scr, *scr, t, npair):
    i = pl.program_id(2)
    nb = kmean_ref.shape[1]
    brow = lax.broadcasted_iota(jnp.int32, (nb, t), 0).astype(F32)
    past = brow < i.astype(F32)
    qs, ks, vr = [], [], []
    for pr in range(npair):
        cols = slice(LANES * pr, LANES * (pr + 1))
        qf, kmean = qf_ref[0, :, cols], kmean_ref[0, :, cols]
        qs += _q_tiles_t(q_ref[0, :, cols], True)
        for half in range(2):
            s = 2 * pr + half
            gate = lax.dot_general(kmean, _half_mask(qf, half), (((1,), (1,)), ((), ())),
                                   preferred_element_type=F32, precision=lax.Precision.HIGHEST)
            gate = jnp.where(past, gate, NEG_INF)
            sel = jnp.zeros((nb, t), F32)
            for _ in range(MOBA_TOPK):
                top = jnp.max(gate, axis=0, keepdims=True)
                first = jnp.min(jnp.where(gate == top, brow, float(nb)), axis=0, keepdims=True)
                pick = brow == first
                sel = jnp.where(pick, 1.0, sel)
                gate = jnp.where(pick, 2 * NEG_INF, gate)
            sel_scr[s] = jnp.where(past, sel, 0.0)
            ks.append(cols)
            vr.append(slice(HEAD_DIM * s, HEAD_DIM * (s + 1)))
    _flash_t(qs, k_ref, vt_ref, ks, vr, scr, i, t,
             past_mask=lambda s, jj: sel_scr[s, pl.ds(jj, 1), :] > 0.5)
    for s in range(2 * npair):
        ot_ref[0, 0, vr[s], :] = _normalised(scr[1], s, HEAD_DIM).astype(ot_ref.dtype)


def _moba_attention(q, qf, k, vt, kmean, npair):
    b, s, w = q.shape
    _, nb, _, t = vt.shape
    nkm = kmean.shape[1]
    lanes = npair * LANES
    return pl.pallas_call(
        functools.partial(_moba_kernel, t=t, npair=npair),
        grid=(b, w // lanes, nb),
        in_specs=[pl.BlockSpec((1, t, lanes), lambda bi, p, i: (bi, i, p)),
                  pl.BlockSpec((1, t, lanes), lambda bi, p, i: (bi, i, p)),
                  pl.BlockSpec((1, s, lanes), lambda bi, p, i: (bi, 0, p)),
                  pl.BlockSpec((1, nb, lanes, t), lambda bi, p, i: (bi, 0, p, 0)),
                  pl.BlockSpec((1, nkm, lanes), lambda bi, p, i: (bi, 0, p))],
        out_specs=pl.BlockSpec((1, 1, lanes, t), lambda bi, p, i: (bi, i, p, 0)),
        out_shape=jax.ShapeDtypeStruct((b, nb, w, t), BF16),
        scratch_shapes=[pltpu.VMEM((2 * npair, nkm, t), F32)] + _flash_scratch(2 * npair, HEAD_DIM, t),
        compiler_params=_cparams(("parallel", "parallel", "arbitrary")),
        name="moba_attention",
    )(q, qf, k, vt, kmean)


def _sw_kernel(q_ref, kc_ref, vc_ref, kp_ref, vp_ref, o_ref, l_ref, *, nbk):
    i = pl.program_id(1)
    blk = SW_BLOCK
    qi = lax.broadcasted_iota(jnp.int32, (blk, 2 * blk), 0)
    kj = lax.broadcasted_iota(jnp.int32, (blk, 2 * blk), 1)
    band = (kj >= qi) & (kj <= qi + blk)
    lane = _lane_iota((1, LANES))
    heads = [(pair, half) for pair in range(DIL_W // LANES) for half in range(2)]

    def scores(n):
        rows = slice(blk * n, blk * (n + 1))
        q = q_ref[0, rows, :]
        if n == 0:
            kprev, valid = kp_ref[0], band & ((kj >= blk) | (i > 0))
        else:
            kprev, valid = kc_ref[0, blk * (n - 1):blk * n, :], band
        kk = jnp.concatenate([kprev, kc_ref[0, rows, :]], axis=0)
        out = []
        for pair, half in heads:
            sl = slice(LANES * pair, LANES * (pair + 1))
            out.append(jnp.where(valid, _nt_dot(_half_mask(q[:, sl], half), kk[:, sl]), NEG_INF))
        return out

    nxt = scores(0)
    for n in range(nbk):
        cur, rows = nxt, slice(blk * n, blk * (n + 1))
        if n + 1 < nbk:
            nxt = scores(n + 1)
        vprev = vp_ref[0] if n == 0 else vc_ref[0, blk * (n - 1):blk * n, :]
        vv = jnp.concatenate([vprev, vc_ref[0, rows, :]], axis=0)
        outs, lses = [], []
        for (pair, half), sc in zip(heads, cur):
            m = jnp.max(sc, axis=-1, keepdims=True)
            p = jnp.exp2(sc - m)
            l = jnp.sum(p, axis=-1, keepdims=True)
            sl = slice(LANES * pair, LANES * (pair + 1))
            outs.append(jnp.dot(p.astype(BF16), vv[:, sl], preferred_element_type=F32) / l)
            lses.append(m + jnp.log2(l))
        for pair in range(DIL_W // LANES):
            sl = slice(LANES * pair, LANES * (pair + 1))
            o_ref[0, rows, sl] = jnp.where(lane < HEAD_DIM, outs[2 * pair], outs[2 * pair + 1])
            l_ref[0, rows, sl] = jnp.where(lane < HEAD_DIM, lses[2 * pair], lses[2 * pair + 1])


def _sliding_window(dil, nbk, name):
    lead, (m, w3) = dil.shape[:-2], dil.shape[-2:]
    seqs = dil.reshape((-1, m, w3))
    nbk = min(nbk, m // SW_BLOCK)
    rows = SW_BLOCK * nbk
    cur = lambda off: pl.BlockSpec((1, rows, DIL_W), lambda n, i: (n, i, off))
    prev = lambda off: pl.BlockSpec((1, SW_BLOCK, DIL_W), lambda n, i: (n, jnp.maximum(i * nbk - 1, 0), off))
    out_spec = pl.BlockSpec((1, rows, DIL_W), lambda n, i: (n, i, 0))
    o, l = pl.pallas_call(
        functools.partial(_sw_kernel, nbk=nbk),
        grid=(seqs.shape[0], m // rows),
        in_specs=[cur(0), cur(1), cur(2), prev(1), prev(2)],
        out_specs=[out_spec, out_spec],
        out_shape=[jax.ShapeDtypeStruct((seqs.shape[0], m, DIL_W), F32)] * 2,
        compiler_params=_cparams(("parallel", "arbitrary")),
        name=name,
    )(seqs, seqs, seqs, seqs, seqs)
    return o.reshape(lead + (m, DIL_W)), l.reshape(lead + (m, DIL_W))


MLP_CHUNK = 1024


def _mlp_tail(y, mlp_refs):
    sh_ref, sc_ref, g_ref, nw_ref, w1_ref, w2_ref = mlp_refs
    h = (_rms(y, nw_ref[...]) * (1.0 + sc_ref[0]) + sh_ref[0]).astype(BF16)
    acc = jnp.zeros(y.shape, F32)
    for c in range(w1_ref.shape[1] // MLP_CHUNK):
        cols = slice(MLP_CHUNK * c, MLP_CHUNK * (c + 1))
        a = jnp.maximum(jnp.dot(h, w1_ref[:, cols], preferred_element_type=F32), 0.0)
        acc = acc + jnp.dot((a * a).astype(BF16), w2_ref[cols, :], preferred_element_type=F32)
    return y + g_ref[0] * acc


def _proj_t(ot_ref, w_rows):
    return jnp.concatenate([_tn_dot(ot_ref[0, j], w_rows) for j in range(ot_ref.shape[1])], axis=0)


def _mlp_specs(mlp_args, d):
    per_b = pl.BlockSpec((1, 1, d), lambda bi, i: (bi, 0, 0))
    const = lambda a: pl.BlockSpec(a.shape, lambda bi, i: (0, 0), pipeline_mode=pl.Buffered(1))
    sh, sc, g2, nw, w1, w2 = mlp_args
    return [per_b, per_b, per_b, pl.BlockSpec(nw.shape, lambda bi, i: (0, 0)), const(w1), const(w2)]


def _even_out_kernel(x_ref, g_ref, oat_ref, o0_ref, o1_ref, o2_ref, l0_ref, l1_ref, l2_ref, w_ref,
                     *rest):
    *mlp_refs, y_ref, tscr = rest
    tm = x_ref.shape[1]
    ncol = DIL_W // LANES

    def token_major(ref, slot):
        r = ref.shape[1]
        for c in range(r):
            for j in range(ncol):
                tscr[slot * ncol + j, pl.ds(c, tm // r, stride=r), :] = ref[0, c, :, LANES * j:LANES * (j + 1)]
        return jnp.concatenate([tscr[slot * ncol + j] for j in range(ncol)], axis=1)

    o0, l0 = o0_ref[0], l0_ref[0]
    o1, l1 = token_major(o1_ref, 0), token_major(l1_ref, 1)
    o2, l2 = token_major(o2_ref, 2), token_major(l2_ref, 3)
    top = jnp.maximum(jnp.maximum(l0, l1), l2)
    w0, w1, w2 = jnp.exp2(l0 - top), jnp.exp2(l1 - top), jnp.exp2(l2 - top)
    ob = (w0 * o0 + w1 * o1 + w2 * o2) / (w0 + w1 + w2)
    na = oat_ref.shape[2]
    y = _proj_t(oat_ref, w_ref[:na, :]) + jnp.dot(ob.astype(BF16), w_ref[na:, :], preferred_element_type=F32)
    y_ref[0] = _mlp_tail(x_ref[0] + g_ref[0] * y, mlp_refs)


def _t_spec(a, tm):
    return pl.BlockSpec((1, tm // a.shape[-1]) + a.shape[2:], lambda bi, i: (bi, i, 0, 0))


def _even_out(x, g1, oat, o_dil, l_dil, w_out, mlp_args, tm):
    b, s, d = x.shape
    tok = lambda w: pl.BlockSpec((1, tm, w), lambda bi, i: (bi, i, 0))

    def dil_spec(a):
        if a.ndim == 3:
            return tok(DIL_W)
        r = a.shape[1]
        return pl.BlockSpec((1, r, tm // r, DIL_W), lambda bi, i: (bi, 0, i, 0))

    return pl.pallas_call(
        _even_out_kernel,
        grid=(b, s // tm),
        in_specs=[tok(d), pl.BlockSpec((1, 1, d), lambda bi, i: (bi, 0, 0)), _t_spec(oat, tm)]
                 + [dil_spec(a) for a in (*o_dil, *l_dil)]
                 + [pl.BlockSpec(w_out.shape, lambda bi, i: (0, 0), pipeline_mode=pl.Buffered(1))]
                 + _mlp_specs(mlp_args, d),
        out_specs=tok(d),
        out_shape=jax.ShapeDtypeStruct((b, s, d), F32),
        scratch_shapes=[pltpu.VMEM((4 * DIL_W // LANES, tm, LANES), F32)],
        compiler_params=_cparams(("parallel", "arbitrary")),
        name="even_out_mlp",
    )(x, g1, oat, *o_dil, *l_dil, w_out, *mlp_args)


def _odd_out_kernel(x_ref, g_ref, oct_ref, odt_ref, w_ref, *rest):
    *mlp_refs, y_ref = rest
    nc = oct_ref.shape[2]
    y = _proj_t(oct_ref, w_ref[:nc, :]) + _proj_t(odt_ref, w_ref[nc:, :])
    y_ref[0] = _mlp_tail(x_ref[0] + g_ref[0] * y, mlp_refs)


def _odd_out(x, g1, oct, odt, w_out, mlp_args, tm):
    b, s, d = x.shape
    tok = lambda w: pl.BlockSpec((1, tm, w), lambda bi, i: (bi, i, 0))
    return pl.pallas_call(
        _odd_out_kernel,
        grid=(b, s // tm),
        in_specs=[tok(d), pl.BlockSpec((1, 1, d), lambda bi, i: (bi, 0, 0)), _t_spec(oct, tm), _t_spec(odt, tm),
                  pl.BlockSpec(w_out.shape, lambda bi, i: (0, 0), pipeline_mode=pl.Buffered(1))]
                 + _mlp_specs(mlp_args, d),
        out_specs=tok(d),
        out_shape=jax.ShapeDtypeStruct((b, s, d), F32),
        compiler_params=_cparams(("parallel", "arbitrary")),
        name="odd_out_mlp",
    )(x, g1, oct, odt, w_out, *mlp_args)


def _rope_tables(positions):
    pos = positions.astype(F32)[:, :, None]
    half = HEAD_DIM // 2
    inv = ROPE_THETA ** (-jnp.arange(half, dtype=F32) / half)
    ang = pos * inv
    c, s = lax.optimization_barrier((jnp.cos(ang), jnp.sin(ang)))
    c64 = jnp.concatenate([c, c, c, c], axis=-1)
    s64 = jnp.concatenate([-s, s, -s, s], axis=-1)
    step = HEAD_DIM // MLA_ROPE
    c, s = c[..., ::step], s[..., ::step]
    ones = jnp.ones(pos.shape[:2] + (MLA_NOPE,), F32)
    pad = LANES - MLA_QK
    cm = jnp.concatenate([ones, c, c, jnp.ones(pos.shape[:2] + (pad,), F32)], axis=-1)
    sm = jnp.concatenate([0 * ones, -s, s, jnp.zeros(pos.shape[:2] + (pad,), F32)], axis=-1)
    return cm, sm, c64, s64


def _partner(n, width):
    idx = np.arange(n)
    return np.where(idx % width < width // 2, idx + width // 2, idx - width // 2)


def _even_weights(w_in, w_uq, w_ukv, qn, kn):
    o2 = MLA_Q_RANK + MLA_KV_RANK
    o3 = o2 + MLA_ROPE
    d = w_in.shape[0]
    p_rope = _partner(MLA_ROPE, MLA_ROPE)
    zeros = lambda n: jnp.zeros((d, n), w_in.dtype)
    kr = w_in[:, o2:o3]
    tail = LANES - MLA_QK
    krp = jnp.concatenate([zeros(MLA_NOPE), kr, zeros(tail)], axis=1)
    krp_rot = jnp.concatenate([zeros(MLA_NOPE), kr[:, p_rope], zeros(tail)], axis=1)
    dil = w_in[:, o3:].reshape(d, 3, len(DIL_CONFIGS), DIL_W)
    dil_main = dil.transpose(0, 2, 1, 3).reshape(d, -1)
    dil_rot = dil[:, :2][..., _partner(DIL_W, HEAD_DIM)].transpose(0, 2, 1, 3).reshape(d, -1)
    win = jnp.concatenate([w_in[:, :o2], krp, krp_rot, dil_main, dil_rot], axis=1)

    padq = ((0, 0), (0, 0), (0, tail))
    wuq_rot = jnp.concatenate([w_uq[:, :, :MLA_NOPE], w_uq[:, :, MLA_NOPE:][:, :, p_rope]], axis=-1)
    wuq = jnp.concatenate([jnp.pad(w_uq, padq).reshape(MLA_Q_RANK, -1),
                           jnp.pad(wuq_rot, padq).reshape(MLA_Q_RANK, -1)], axis=1)
    wuk = jnp.pad(w_ukv[:, :, :MLA_NOPE], ((0, 0), (0, 0), (0, LANES - MLA_NOPE))).reshape(MLA_KV_RANK, -1)
    wvt = w_ukv[:, :, MLA_NOPE:].reshape(MLA_KV_RANK, -1).T

    def gains(g):
        rot = jnp.concatenate([g[:MLA_NOPE], g[MLA_NOPE:][p_rope]])
        return jnp.stack([jnp.pad(g, (0, tail)), jnp.pad(rot, (0, tail))]).astype(F32)

    return win.astype(BF16), wuq.astype(BF16), wuk.astype(BF16), wvt.astype(BF16), gains(qn), gains(kn)


def _odd_weights(w_in):
    nqk = DIFF_HEADS * LANES
    nm = MOBA_HEADS * HEAD_DIM
    main = jnp.concatenate([w_in[:, :2 * nqk], w_in[:, 3 * nqk:3 * nqk + 2 * nm]], axis=1)
    win = jnp.concatenate([main, main[:, _partner(main.shape[1], HEAD_DIM)]], axis=1)
    wvt = jnp.concatenate([w_in[:, 2 * nqk:3 * nqk], w_in[:, 3 * nqk + 2 * nm:]], axis=1).T
    return win.astype(BF16), wvt.astype(BF16)


def _gains64(g):
    rot = g[_partner(HEAD_DIM, HEAD_DIM)]
    return jnp.stack([jnp.concatenate([g, g]), jnp.concatenate([rot, rot])]).astype(F32)


def kernel(x, c, positions, ada_w, ada_b, norm_mix, norm_mlp, mlp_w1, mlp_w2, even_w_in, even_w_out, mla_q_lat_norm, mla_kv_lat_norm, mla_w_uq, mla_w_ukv, mla_q_norm, mla_k_norm, dil_q_norm, dil_k_norm, odd_w_in, odd_w_out, diff_q_norm, diff_k_norm, diff_lambda, diff_subln, moba_q_norm, moba_k_norm):
    b, s, d = x.shape
    depth = ada_w.shape[0]
    t_attn = MOBA_BLOCK
    tm_out = 2 * t_attn

    mod = _adaln(c, ada_w, ada_b)
    cm, sm, c64, s64 = _rope_tables(positions)

    for layer in range(depth):
        sh1, sc1, g1, sh2, sc2, g2 = [mod[layer, :, d * t:d * (t + 1)].reshape(b, 1, d) for t in range(6)]
        mlp_args = (sh2, sc2, g2, _row(norm_mlp[layer]), mlp_w1[layer].astype(BF16), mlp_w2[layer].astype(BF16))
        i = layer // 2
        if layer % 2 == 0:
            win, wuq, wuk, wvt, qn, kn = _even_weights(even_w_in[i], mla_w_uq[i], mla_w_ukv[i],
                                                       mla_q_norm[i], mla_k_norm[i])
            q, k, vt, *dils = _even_in(x, sh1, sc1, _row(norm_mix[layer]), win, wuq, wuk, wvt,
                                       _row(mla_q_lat_norm[i]), _row(mla_kv_lat_norm[i]), qn, kn,
                                       _gains64(dil_q_norm[i]), _gains64(dil_k_norm[i]), cm, sm, c64, s64, t_attn)
            o_at = _mla_attention(q, k, vt, nh=4)
            o_dil, l_dil = zip(*[_sliding_window(dg, 4, f"sliding_window_g{g}") for g, dg in enumerate(dils)])
            x = _even_out(x, g1, o_at, o_dil, l_dil, even_w_out[i].astype(BF16), mlp_args, tm_out)
        else:
            lam_init = 0.8 - 0.6 * math.exp(-0.3 * layer)
            win, wvt = _odd_weights(odd_w_in[i])
            qc, kc, qm, qmf, km, vct, vmt, kmean = _odd_in(
                x, sh1, sc1, _row(norm_mix[layer]), win, wvt,
                _gains64(diff_q_norm[i]), _gains64(diff_k_norm[i]), _gains64(moba_q_norm[i]),
                _gains64(moba_k_norm[i]),
                c64, s64, t_attn)
            o_ct = _diff_attention(qc, kc, vct, diff_lambda[i].astype(F32),
                                   diff_subln[i].reshape(-1, 1).astype(F32), lam_init, nh=2)
            o_dt = _moba_attention(qm, qmf, km, vmt, kmean.reshape(b, s // MOBA_BLOCK, kmean.shape[-1]), npair=2)
            x = _odd_out(x, g1, o_ct, o_dt, odd_w_out[i].astype(BF16), mlp_args, tm_out)
    return x
```

```python
import functools
import math

import jax
import jax.numpy as jnp
import numpy as np
from jax import lax
from jax.experimental import pallas as pl
from jax.experimental.pallas import tpu as pltpu

F32 = jnp.float32
BF16 = jnp.bfloat16

LANES = 128
HEAD_DIM = 64
ROPE_THETA = 10000.0
NORM_EPS = 1e-6
NEG_INF = -1e30
LOG2E = math.log2(math.e)

MLA_HEADS = 8
MLA_Q_RANK = 384
MLA_KV_RANK = 256
MLA_NOPE = 64
MLA_ROPE = 32
MLA_QK = MLA_NOPE + MLA_ROPE
DIL_CONFIGS = ((128, 1), (512, 4), (2048, 16))
DIL_HEADS = 4
DIL_W = DIL_HEADS * HEAD_DIM
DIFF_HEADS = 4
MOBA_HEADS = 8
MOBA_BLOCK = 256
MOBA_TOPK = 3
SW_BLOCK = 128
ONES_ROWS = 16

VMEM_LIMIT = 56 * 1024 * 1024


def _cparams(sem):
    return pltpu.CompilerParams(dimension_semantics=sem, vmem_limit_bytes=VMEM_LIMIT)


def _nt_dot(a, b):
    return lax.dot_general(a, b, (((1,), (1,)), ((), ())), preferred_element_type=F32)


def _tn_dot(a, b):
    return lax.dot_general(a, b, (((0,), (0,)), ((), ())), preferred_element_type=F32)


def _rms(x, w):
    return x * lax.rsqrt(jnp.mean(x * x, axis=-1, keepdims=True) + NORM_EPS) * w


def _lane_iota(shape):
    return lax.broadcasted_iota(jnp.int32, shape, len(shape) - 1)


def _adaln_kernel(c_ref, w_ref, b_ref, o_ref):
    c = c_ref[...]
    cond = c * (1.0 / (1.0 + jnp.exp(-c)))
    o_ref[0] = jnp.dot(cond, w_ref[0], preferred_element_type=F32,
                       precision=lax.Precision.HIGHEST) + b_ref[0]


def _adaln(c, ada_w, ada_b):
    depth, d, n = ada_w.shape
    b = c.shape[0]
    tn = 1024
    return pl.pallas_call(
        _adaln_kernel,
        grid=(depth, n // tn),
        in_specs=[pl.BlockSpec((b, d), lambda l, j: (0, 0)),
                  pl.BlockSpec((1, d, tn), lambda l, j: (l, 0, j)),
                  pl.BlockSpec((1, 1, tn), lambda l, j: (l, 0, j))],
        out_specs=pl.BlockSpec((1, b, tn), lambda l, j: (l, 0, j)),
        out_shape=jax.ShapeDtypeStruct((depth, b, n), F32),
        compiler_params=_cparams(("parallel", "parallel")),
        name="adaln",
    )(c, ada_w, ada_b.reshape(depth, 1, n))


def _inv_rms64(x):
    lane = _lane_iota((1, LANES))
    lo = lane < HEAD_DIM
    sq = x * x
    s_lo = jnp.sum(jnp.where(lo, sq, 0.0), axis=-1, keepdims=True)
    s_hi = jnp.sum(jnp.where(lo, 0.0, sq), axis=-1, keepdims=True)
    return lax.rsqrt(jnp.where(lo, s_lo, s_hi) * (1.0 / HEAD_DIM) + NORM_EPS)


def _store_t_blocks(out_ref, xt):
    t = out_ref.shape[-1]
    for j in range(out_ref.shape[1]):
        out_ref[0, j] = xt[:, t * j:t * (j + 1)]


def _t_blocks(b, s, rows, tm, t):
    return (pl.BlockSpec((1, tm // t, rows, t), lambda bi, i: (bi, i, 0, 0)),
            jax.ShapeDtypeStruct((b, s // t, rows, t), BF16))


def _even_in_kernel(x_ref, sh_ref, sc_ref, nw_ref, win_ref, wuq_ref, wuk_ref, wvt_ref,
                    qlat_ref, kvlat_ref, qn_ref, kn_ref, dqn_ref, dkn_ref,
                    cm_ref, sm_ref, c64_ref, s64_ref,
                    q_out, k_out, vt_out, d0_out, d1_out, d2_out, dscr):
    x = x_ref[0]
    tm = x.shape[0]
    h = _rms(x, nw_ref[...]) * (1.0 + sc_ref[0]) + sh_ref[0]
    u = jnp.dot(h.astype(BF16), win_ref[...], preferred_element_type=F32)

    o1 = MLA_Q_RANK
    o2 = o1 + MLA_KV_RANK
    o_kr, o_krr, o_dil = o2, o2 + LANES, o2 + 2 * LANES
    n_dil = 3 * DIL_W * len(DIL_CONFIGS)
    o_rot = o_dil + n_dil
    nq = MLA_HEADS * LANES
    cqn = _rms(u[:, :o1], qlat_ref[...]).astype(BF16)
    qp = jnp.dot(cqn, wuq_ref[...], preferred_element_type=F32)
    ckvn = _rms(u[:, o1:o2], kvlat_ref[...]).astype(BF16)
    kvp = jnp.dot(ckvn, wuk_ref[...], preferred_element_type=F32)
    _store_t_blocks(vt_out, _nt_dot(wvt_ref[...], ckvn).astype(BF16))
    kr, kr_rot = u[:, o_kr:o_kr + LANES], u[:, o_krr:o_krr + LANES]

    cm, sm = cm_ref[0], sm_ref[0]
    q_scale = MLA_QK ** -0.5 * LOG2E
    qgc, qgs = qn_ref[0:1] * cm * q_scale, qn_ref[1:2] * sm * q_scale
    kgc, kgs = kn_ref[0:1] * cm, kn_ref[1:2] * sm
    kr_term = kr_rot * kgs
    for hd in range(MLA_HEADS):
        sl = slice(LANES * hd, LANES * (hd + 1))
        qh = qp[:, sl]
        inv = lax.rsqrt(jnp.sum(qh * qh, -1, keepdims=True) * (1.0 / MLA_QK) + NORM_EPS)
        q_out[0, :, sl] = (inv * (qh * qgc + qp[:, nq + LANES * hd:nq + LANES * (hd + 1)] * qgs)).astype(BF16)
        kh = kvp[:, sl] + kr
        inv = lax.rsqrt(jnp.sum(kh * kh, -1, keepdims=True) * (1.0 / MLA_QK) + NORM_EPS)
        k_out[0, :, sl] = (inv * (kh * kgc + kr_term)).astype(BF16)

    c64, s64 = c64_ref[0], s64_ref[0]
    d_scale = HEAD_DIM ** -0.5 * LOG2E
    dqc, dqs = dqn_ref[0:1] * c64 * d_scale, dqn_ref[1:2] * s64 * d_scale
    dkc, dks = dkn_ref[0:1] * c64, dkn_ref[1:2] * s64
    ncol = DIL_W // LANES
    for g, d_out in enumerate((d0_out, d1_out, d2_out)):
        _, r = DIL_CONFIGS[g]
        base = o_dil + 3 * DIL_W * g
        rbase = o_rot + 2 * DIL_W * g
        for j in range(3 * ncol):
            xc = u[:, base + LANES * j:base + LANES * (j + 1)]
            if j < 2 * ncol:
                xr = u[:, rbase + LANES * j:rbase + LANES * (j + 1)]
                gc, gs = (dqc, dqs) if j < ncol else (dkc, dks)
                xc = _inv_rms64(xc) * (xc * gc + xr * gs)
            if r == 1:
                d_out[0, :, LANES * j:LANES * (j + 1)] = xc.astype(BF16)
            else:
                dscr[j] = xc
                for c in range(r):
                    d_out[0, c, :, LANES * j:LANES * (j + 1)] = dscr[j, pl.ds(c, tm // r, stride=r), :].astype(BF16)


def _row(v):
    return v.reshape(1, -1).astype(F32)


def _even_in(x, sh, sc, nw, win, wuq, wuk, wvt, qlat, kvlat, qn, kn, dqn, dkn, cm, sm, c64, s64, tm, t):
    b, s, d = x.shape
    tok = lambda w: pl.BlockSpec((1, tm, w), lambda bi, i: (bi, i, 0))
    per_b = pl.BlockSpec((1, 1, d), lambda bi, i: (bi, 0, 0))
    full = lambda a: pl.BlockSpec(a.shape, lambda bi, i: (0,) * a.ndim, pipeline_mode=pl.Buffered(1))
    vt_spec, vt_shape = _t_blocks(b, s, wvt.shape[0], tm, t)
    dil_specs, dil_shapes = [], []
    for _, r in DIL_CONFIGS:
        if r == 1:
            dil_specs.append(tok(3 * DIL_W))
            dil_shapes.append(jax.ShapeDtypeStruct((b, s, 3 * DIL_W), BF16))
        else:
            dil_specs.append(pl.BlockSpec((1, r, tm // r, 3 * DIL_W), lambda bi, i: (bi, 0, i, 0)))
            dil_shapes.append(jax.ShapeDtypeStruct((b, r, s // r, 3 * DIL_W), BF16))
    return pl.pallas_call(
        _even_in_kernel,
        grid=(b, s // tm),
        in_specs=[tok(d), per_b, per_b, full(nw), full(win), full(wuq), full(wuk), full(wvt),
                  full(qlat), full(kvlat), full(qn), full(kn), full(dqn), full(dkn),
                  tok(LANES), tok(LANES), tok(LANES), tok(LANES)],
        out_specs=[tok(MLA_HEADS * LANES), tok(MLA_HEADS * LANES), vt_spec] + dil_specs,
        out_shape=[jax.ShapeDtypeStruct((b, s, MLA_HEADS * LANES), BF16),
                   jax.ShapeDtypeStruct((b, s, MLA_HEADS * LANES), BF16), vt_shape] + dil_shapes,
        scratch_shapes=[pltpu.VMEM((3 * DIL_W // LANES, tm, LANES), F32)],
        compiler_params=_cparams(("parallel", "arbitrary")),
        name="even_in_proj",
    )(x, sh, sc, nw, win, wuq, wuk, wvt, qlat, kvlat, qn, kn, dqn, dkn, cm, sm, c64, s64)


def _odd_in_kernel(x_ref, sh_ref, sc_ref, nw_ref, win_ref, wvt_ref, dqn_ref, dkn_ref, mqn_ref, mkn_ref,
                   c64_ref, s64_ref,
                   qc_out, kc_out, qm_out, qmf_out, km_out, vct_out, vmt_out, kmean_out):
    x = x_ref[0]
    tm = x.shape[0]
    h = (_rms(x, nw_ref[...]) * (1.0 + sc_ref[0]) + sh_ref[0]).astype(BF16)
    u = jnp.dot(h, win_ref[...], preferred_element_type=F32)
    vt = _nt_dot(wvt_ref[...], h).astype(BF16)
    nvc = vct_out.shape[2]
    _store_t_blocks(vct_out, vt[:nvc])
    _store_t_blocks(vmt_out, vt[nvc:])
    c64, s64 = c64_ref[0], s64_ref[0]
    scale = HEAD_DIM ** -0.5 * LOG2E
    nqk = DIFF_HEADS * LANES
    nm = MOBA_HEADS * HEAD_DIM
    rot = 2 * nqk + 2 * nm

    def tables(gain_ref, mult):
        return gain_ref[0:1] * c64 * mult, gain_ref[1:2] * s64 * mult

    def roped(off, col, tab):
        a = off + LANES * col
        xc, xr = u[:, a:a + LANES], u[:, rot + a:rot + a + LANES]
        return _inv_rms64(xc) * (xc * tab[0] + xr * tab[1])

    t_qc, t_kc, t_qm, t_km = tables(dqn_ref, scale), tables(dkn_ref, 1.0), tables(mqn_ref, 1.0), tables(mkn_ref, 1.0)
    for col in range(nqk // LANES):
        sl = slice(LANES * col, LANES * (col + 1))
        qc_out[0, :, sl] = roped(0, col, t_qc).astype(BF16)
        kc_out[0, :, sl] = roped(nqk, col, t_kc).astype(BF16)

    base = 2 * nqk
    for col in range(nm // LANES):
        sl = slice(LANES * col, LANES * (col + 1))
        qm = roped(base, col, t_qm)
        qmf_out[0, :, sl] = qm
        qm_out[0, :, sl] = (qm * scale).astype(BF16)
        km = roped(base + nm, col, t_km)
        km_out[0, :, sl] = km.astype(BF16)
        for blk in range(tm // MOBA_BLOCK):
            rows = slice(MOBA_BLOCK * blk, MOBA_BLOCK * (blk + 1))
            kmean_out[0, 0, blk:blk + 1, sl] = jnp.mean(km[rows], axis=0, keepdims=True)


def _odd_in(x, sh, sc, nw, win, wvt, dqn, dkn, mqn, mkn, c64, s64, tm, t):
    b, s, d = x.shape
    tok = lambda w: pl.BlockSpec((1, tm, w), lambda bi, i: (bi, i, 0))
    per_b = pl.BlockSpec((1, 1, d), lambda bi, i: (bi, 0, 0))
    full = lambda a: pl.BlockSpec(a.shape, lambda bi, i: (0,) * a.ndim, pipeline_mode=pl.Buffered(1))
    w = DIFF_HEADS * LANES
    nblk = tm // MOBA_BLOCK
    shp = lambda dt: jax.ShapeDtypeStruct((b, s, w), dt)
    vt_spec, vt_shape = _t_blocks(b, s, wvt.shape[0] // 2, tm, t)
    return pl.pallas_call(
        _odd_in_kernel,
        grid=(b, s // tm),
        in_specs=[tok(d), per_b, per_b, full(nw), full(win), full(wvt), full(dqn), full(dkn), full(mqn), full(mkn),
                  tok(LANES), tok(LANES)],
        out_specs=[tok(w)] * 5 + [vt_spec, vt_spec, pl.BlockSpec((1, 1, nblk, w), lambda bi, i: (bi, i, 0, 0))],
        out_shape=[shp(BF16), shp(BF16), shp(BF16), shp(F32), shp(BF16), vt_shape, vt_shape,
                   jax.ShapeDtypeStruct((b, s // tm, nblk, w), F32)],
        compiler_params=_cparams(("parallel", "arbitrary")),
        name="odd_in_proj",
    )(x, sh, sc, nw, win, wvt, dqn, dkn, mqn, mkn, c64, s64)


def _q_tiles_t(q, split_halves):
    qt = q.astype(F32).T
    if not split_halves:
        return [qt.astype(BF16)]
    row = lax.broadcasted_iota(jnp.int32, qt.shape, 0)
    return [jnp.where((row >= HEAD_DIM) == bool(half), qt, 0.0).astype(BF16) for half in range(2)]


def _flash_t(qts, k_ref, vt_ref, k_slices, v_rows, scr, i, t, past_mask=None):
    m_scr, acc_scr, s_scr, p_scr, a_scr = scr
    n = len(qts)
    ones = jnp.ones((ONES_ROWS, t), BF16)

    def qk(blk, s):
        b0 = pl.multiple_of(blk * t, t)
        return jnp.dot(k_ref[0, pl.ds(b0, t), k_slices[s]], qts[s], preferred_element_type=F32)

    def past_qk(blk, s):
        st = qk(blk, s)
        if past_mask is not None:
            st = jnp.where(past_mask(s, blk), st, NEG_INF)
        return st

    def pv(blk, s, p):
        vt = jnp.concatenate([vt_ref[0, blk, v_rows[s], :], ones], axis=0)
        return jnp.dot(vt, p, preferred_element_type=F32)

    krow = lax.broadcasted_iota(jnp.int32, (t, t), 0)
    qcol = lax.broadcasted_iota(jnp.int32, (t, t), 1)
    causal = krow <= qcol
    last_past = jnp.maximum(i - 1, 0)
    for s in range(n):
        st = jnp.where(causal, qk(i, s), NEG_INF)
        m = jnp.max(st, axis=0, keepdims=True)
        m_scr[s] = m
        p_scr[0, s] = jnp.exp2(st - m).astype(BF16)
        p_scr[1, s] = jnp.zeros((t, t), BF16)
        a_scr[0, s] = jnp.ones_like(m)
        a_scr[1, s] = jnp.ones_like(m)
        acc_scr[s] = jnp.zeros(acc_scr.shape[1:], F32)
        s_scr[0, s] = past_qk(0, s)
        s_scr[1, s] = past_qk(jnp.minimum(1, last_past), s)

    def softmax_stage(slot, valid, rs, ws):
        for s in range(n):
            st = s_scr[2 * rs + slot, s]
            cmax = jnp.max(st, axis=0, keepdims=True)
            if valid is not None:
                cmax = jnp.where(valid, cmax, NEG_INF)
            m_prev = m_scr[s]
            m_new = jnp.maximum(m_prev, cmax)
            a_scr[2 * ws + slot, s] = jnp.exp2(m_prev - m_new)
            p_scr[2 * ws + slot, s] = jnp.exp2(st - m_new).astype(BF16)
            m_scr[s] = m_new

    def body(kk, rs, ws):
        first = kk == 0
        pa_blk = jnp.where(first, i, 2 * kk - 2)
        pb_blk = jnp.where(first, i, 2 * kk - 1)
        p_prev = [[p_scr[2 * rs + sl, s] for s in range(n)] for sl in range(2)]
        a_prev = [[a_scr[2 * rs + sl, s] for s in range(n)] for sl in range(2)]
        s_next = [[past_qk(jnp.minimum(2 * kk + 2 + sl, last_past), s) for s in range(n)] for sl in range(2)]
        pvs = [[pv(blk, s, p_prev[sl][s]) for s in range(n)] for sl, blk in enumerate((pa_blk, pb_blk))]
        softmax_stage(0, None, rs, ws)
        softmax_stage(1, 2 * kk + 1 < i, rs, ws)
        for sl in range(2):
            for s in range(n):
                s_scr[2 * ws + sl, s] = s_next[sl][s]
        for s in range(n):
            acc_scr[s] = a_prev[1][s] * (a_prev[0][s] * acc_scr[s] + pvs[0][s]) + pvs[1][s]

    def two_bodies(kp, carry):
        body(2 * kp, 0, 1)
        body(2 * kp + 1, 1, 0)
        return carry

    nbody = (i + 1) // 2
    lax.fori_loop(0, nbody // 2, two_bodies, 0)

    @pl.when(nbody % 2 == 1)
    def _():
        body(nbody - 1, 0, 1)

    fs = nbody % 2
    kl = jnp.maximum(nbody - 1, 0)
    none = i == 0
    fa_blk = jnp.where(none, i, 2 * kl)
    fb_blk = jnp.where(none, i, jnp.minimum(2 * kl + 1, last_past))
    fb_scale = jnp.where(jnp.logical_or(none, i % 2 == 0), 1.0, 0.0)
    for s in range(n):
        acc_scr[s] = (a_scr[2 * fs + 1, s] * (a_scr[2 * fs, s] * acc_scr[s] + pv(fa_blk, s, p_scr[2 * fs, s]))
                      + fb_scale * pv(fb_blk, s, p_scr[2 * fs + 1, s]))


def _flash_scratch(n, nv, t):
    return [pltpu.VMEM((n, 1, t), F32), pltpu.VMEM((n, nv + ONES_ROWS, t), F32), pltpu.VMEM((4, n, t, t), F32),
            pltpu.VMEM((4, n, t, t), BF16), pltpu.VMEM((4, n, 1, t), F32)]


def _normalised(acc_scr, s, nv):
    return acc_scr[s, :nv, :] / acc_scr[s, nv:nv + 1, :]


def _half_mask(x, half):
    lane = _lane_iota((1, LANES))
    return jnp.where((lane >= HEAD_DIM) == bool(half), x, jnp.zeros_like(x))


def _mla_kernel(q_ref, k_ref, vt_ref, ot_ref, *scr, t, nh):
    i = pl.program_id(2)
    qs = [_q_tiles_t(q_ref[0, :, LANES * s:LANES * (s + 1)], False)[0] for s in range(nh)]
    ks = [slice(LANES * s, LANES * (s + 1)) for s in range(nh)]
    vr = [slice(HEAD_DIM * s, HEAD_DIM * (s + 1)) for s in range(nh)]
    _flash_t(qs, k_ref, vt_ref, ks, vr, scr, i, t)
    for s in range(nh):
        ot_ref[0, 0, vr[s], :] = _normalised(scr[1], s, HEAD_DIM).astype(ot_ref.dtype)


def _mla_attention(q, k, vt, nh):
    b, s, _ = q.shape
    _, nb, nv, t = vt.shape
    return pl.pallas_call(
        functools.partial(_mla_kernel, t=t, nh=nh),
        grid=(b, MLA_HEADS // nh, nb),
        in_specs=[pl.BlockSpec((1, t, nh * LANES), lambda bi, p, i: (bi, i, p)),
                  pl.BlockSpec((1, s, nh * LANES), lambda bi, p, i: (bi, 0, p)),
                  pl.BlockSpec((1, nb, nh * HEAD_DIM, t), lambda bi, p, i: (bi, 0, p, 0))],
        out_specs=pl.BlockSpec((1, 1, nh * HEAD_DIM, t), lambda bi, p, i: (bi, i, p, 0)),
        out_shape=jax.ShapeDtypeStruct((b, nb, nv, t), BF16),
        scratch_shapes=_flash_scratch(nh, HEAD_DIM, t),
        compiler_params=_cparams(("parallel", "parallel", "arbitrary")),
        name="mla_attention",
    )(q, k, vt)


def _diff_kernel(q_ref, k_ref, vt_ref, lam_ref, sub_ref, ot_ref, *scr, t, nh, lam_init):
    i = pl.program_id(2)
    qs, ks, vr = [], [], []
    for hd in range(nh):
        cols = slice(LANES * hd, LANES * (hd + 1))
        qs += _q_tiles_t(q_ref[0, :, cols], True)
        ks += [cols, cols]
        vr += [cols, cols]
    _flash_t(qs, k_ref, vt_ref, ks, vr, scr, i, t)
    lv = lam_ref[...]
    lam = (jnp.exp(jnp.sum(lv[0:1] * lv[1:2], keepdims=True))
           - jnp.exp(jnp.sum(lv[2:3] * lv[3:4], keepdims=True)) + lam_init)
    for hd in range(nh):
        o = _normalised(scr[1], 2 * hd, LANES) - lam * _normalised(scr[1], 2 * hd + 1, LANES)
        o = o * lax.rsqrt(jnp.mean(o * o, axis=0, keepdims=True) + NORM_EPS) * sub_ref[...]
        ot_ref[0, 0, LANES * hd:LANES * (hd + 1), :] = (o * (1.0 - lam_init)).astype(ot_ref.dtype)


def _diff_attention(q, k, vt, lam_rows, subln_col, lam_init, nh):
    b, s, w = q.shape
    _, nb, _, t = vt.shape
    return pl.pallas_call(
        functools.partial(_diff_kernel, t=t, nh=nh, lam_init=lam_init),
        grid=(b, DIFF_HEADS // nh, nb),
        in_specs=[pl.BlockSpec((1, t, nh * LANES), lambda bi, p, i: (bi, i, p)),
                  pl.BlockSpec((1, s, nh * LANES), lambda bi, p, i: (bi, 0, p)),
                  pl.BlockSpec((1, nb, nh * LANES, t), lambda bi, p, i: (bi, 0, p, 0)),
                  pl.BlockSpec(lam_rows.shape, lambda bi, p, i: (0, 0)),
                  pl.BlockSpec(subln_col.shape, lambda bi, p, i: (0, 0))],
        out_specs=pl.BlockSpec((1, 1, nh * LANES, t), lambda bi, p, i: (bi, i, p, 0)),
        out_shape=jax.ShapeDtypeStruct((b, nb, w, t), BF16),
        scratch_shapes=_flash_scratch(2 * nh, LANES, t),
        compiler_params=_cparams(("parallel", "parallel", "arbitrary")),
        name="diff_attention",
    )(q, k, vt, lam_rows, subln_col)


def _moba_kernel(q_ref, qf_ref, k_ref, vt_ref, kmean_ref, ot_ref, sel_scr, *scr, t, npair):
    i = pl.program_id(2)
    nb = kmean_ref.shape[1]
    brow = lax.broadcasted_iota(jnp.int32, (nb, t), 0).astype(F32)
    past = brow < i.astype(F32)
    qs, ks, vr = [], [], []
    for pr in range(npair):
        cols = slice(LANES * pr, LANES * (pr + 1))
        qf, kmean = qf_ref[0, :, cols], kmean_ref[0, :, cols]
        qs += _q_tiles_t(q_ref[0, :, cols], True)
        for half in range(2):
            s = 2 * pr + half
            gate = lax.dot_general(kmean, _half_mask(qf, half), (((1,), (1,)), ((), ())),
                                   preferred_element_type=F32, precision=lax.Precision.HIGHEST)
            gate = jnp.where(past, gate, NEG_INF)
            sel = jnp.zeros((nb, t), F32)
            for _ in range(MOBA_TOPK):
                top = jnp.max(gate, axis=0, keepdims=True)
                first = jnp.min(jnp.where(gate == top, brow, float(nb)), axis=0, keepdims=True)
                pick = brow == first
                sel = jnp.where(pick, 1.0, sel)
                gate = jnp.where(pick, 2 * NEG_INF, gate)
            sel_scr[s] = jnp.where(past, sel, 0.0)
            ks.append(cols)
            vr.append(slice(HEAD_DIM * s, HEAD_DIM * (s + 1)))
    _flash_t(qs, k_ref, vt_ref, ks, vr, scr, i, t,
             past_mask=lambda s, jj: sel_scr[s, pl.ds(jj, 1), :] > 0.5)
    for s in range(2 * npair):
        ot_ref[0, 0, vr[s], :] = _normalised(scr[1], s, HEAD_DIM).astype(ot_ref.dtype)


def _moba_attention(q, qf, k, vt, kmean, npair):
    b, s, w = q.shape
    _, nb, _, t = vt.shape
    nkm = kmean.shape[1]
    lanes = npair * LANES
    return pl.pallas_call(
        functools.partial(_moba_kernel, t=t, npair=npair),
        grid=(b, w // lanes, nb),
        in_specs=[pl.BlockSpec((1, t, lanes), lambda bi, p, i: (bi, i, p)),
                  pl.BlockSpec((1, t, lanes), lambda bi, p, i: (bi, i, p)),
                  pl.BlockSpec((1, s, lanes), lambda bi, p, i: (bi, 0, p)),
                  pl.BlockSpec((1, nb, lanes, t), lambda bi, p, i: (bi, 0, p, 0)),
                  pl.BlockSpec((1, nkm, lanes), lambda bi, p, i: (bi, 0, p))],
        out_specs=pl.BlockSpec((1, 1, lanes, t), lambda bi, p, i: (bi, i, p, 0)),
        out_shape=jax.ShapeDtypeStruct((b, nb, w, t), BF16),
        scratch_shapes=[pltpu.VMEM((2 * npair, nkm, t), F32)] + _flash_scratch(2 * npair, HEAD_DIM, t),
        compiler_params=_cparams(("parallel", "parallel", "arbitrary")),
        name="moba_attention",
    )(q, qf, k, vt, kmean)


def _sw_kernel(q_ref, kc_ref, vc_ref, kp_ref, vp_ref, o_ref, l_ref, *, nbk):
    i = pl.program_id(1)
    blk = SW_BLOCK
    qi = lax.broadcasted_iota(jnp.int32, (blk, 2 * blk), 0)
    kj = lax.broadcasted_iota(jnp.int32, (blk, 2 * blk), 1)
    band = (kj >= qi) & (kj <= qi + blk)
    lane = _lane_iota((1, LANES))
    heads = [(pair, half) for pair in range(DIL_W // LANES) for half in range(2)]

    def scores(n):
        rows = slice(blk * n, blk * (n + 1))
        q = q_ref[0, rows, :]
        if n == 0:
            kprev, valid = kp_ref[0], band & ((kj >= blk) | (i > 0))
        else:
            kprev, valid = kc_ref[0, blk * (n - 1):blk * n, :], band
        kk = jnp.concatenate([kprev, kc_ref[0, rows, :]], axis=0)
        out = []
        for pair, half in heads:
            sl = slice(LANES * pair, LANES * (pair + 1))
            out.append(jnp.where(valid, _nt_dot(_half_mask(q[:, sl], half), kk[:, sl]), NEG_INF))
        return out

    nxt = scores(0)
    for n in range(nbk):
        cur, rows = nxt, slice(blk * n, blk * (n + 1))
        if n + 1 < nbk:
            nxt = scores(n + 1)
        vprev = vp_ref[0] if n == 0 else vc_ref[0, blk * (n - 1):blk * n, :]
        vv = jnp.concatenate([vprev, vc_ref[0, rows, :]], axis=0)
        outs, lses = [], []
        for (pair, half), sc in zip(heads, cur):
            m = jnp.max(sc, axis=-1, keepdims=True)
            p = jnp.exp2(sc - m)
            l = jnp.sum(p, axis=-1, keepdims=True)
            sl = slice(LANES * pair, LANES * (pair + 1))
            outs.append(jnp.dot(p.astype(BF16), vv[:, sl], preferred_element_type=F32) / l)
            lses.append(m + jnp.log2(l))
        for pair in range(DIL_W // LANES):
            sl = slice(LANES * pair, LANES * (pair + 1))
            o_ref[0, rows, sl] = jnp.where(lane < HEAD_DIM, outs[2 * pair], outs[2 * pair + 1])
            l_ref[0, rows, sl] = jnp.where(lane < HEAD_DIM, lses[2 * pair], lses[2 * pair + 1])


def _sliding_window(dil, nbk, name):
    lead, (m, w3) = dil.shape[:-2], dil.shape[-2:]
    seqs = dil.reshape((-1, m, w3))
    nbk = min(nbk, m // SW_BLOCK)
    rows = SW_BLOCK * nbk
    cur = lambda off: pl.BlockSpec((1, rows, DIL_W), lambda n, i: (n, i, off))
    prev = lambda off: pl.BlockSpec((1, SW_BLOCK, DIL_W), lambda n, i: (n, jnp.maximum(i * nbk - 1, 0), off))
    out_spec = pl.BlockSpec((1, rows, DIL_W), lambda n, i: (n, i, 0))
    o, l = pl.pallas_call(
        functools.partial(_sw_kernel, nbk=nbk),
        grid=(seqs.shape[0], m // rows),
        in_specs=[cur(0), cur(1), cur(2), prev(1), prev(2)],
        out_specs=[out_spec, out_spec],
        out_shape=[jax.ShapeDtypeStruct((seqs.shape[0], m, DIL_W), F32)] * 2,
        compiler_params=_cparams(("parallel", "arbitrary")),
        name=name,
    )(seqs, seqs, seqs, seqs, seqs)
    return o.reshape(lead + (m, DIL_W)), l.reshape(lead + (m, DIL_W))


MLP_CHUNK = 1024


def _mlp_tail(y, mlp_refs):
    sh_ref, sc_ref, g_ref, nw_ref, w1_ref, w2_ref = mlp_refs
    h = (_rms(y, nw_ref[...]) * (1.0 + sc_ref[0]) + sh_ref[0]).astype(BF16)
    acc = jnp.zeros(y.shape, F32)
    for c in range(w1_ref.shape[1] // MLP_CHUNK):
        cols = slice(MLP_CHUNK * c, MLP_CHUNK * (c + 1))
        a = jnp.maximum(jnp.dot(h, w1_ref[:, cols], preferred_element_type=F32), 0.0)
        acc = acc + jnp.dot((a * a).astype(BF16), w2_ref[cols, :], preferred_element_type=F32)
    return y + g_ref[0] * acc


def _proj_t(ot_ref, w_rows):
    return jnp.concatenate([_tn_dot(ot_ref[0, j], w_rows) for j in range(ot_ref.shape[1])], axis=0)


def _mlp_specs(mlp_args, d):
    per_b = pl.BlockSpec((1, 1, d), lambda bi, i: (bi, 0, 0))
    const = lambda a: pl.BlockSpec(a.shape, lambda bi, i: (0, 0), pipeline_mode=pl.Buffered(1))
    sh, sc, g2, nw, w1, w2 = mlp_args
    return [per_b, per_b, per_b, pl.BlockSpec(nw.shape, lambda bi, i: (0, 0)), const(w1), const(w2)]


def _even_out_kernel(x_ref, g_ref, oat_ref, o0_ref, o1_ref, o2_ref, l0_ref, l1_ref, l2_ref, w_ref,
                     *rest):
    *mlp_refs, y_ref, tscr = rest
    tm = x_ref.shape[1]
    ncol = DIL_W // LANES

    def token_major(ref, slot):
        r = ref.shape[1]
        for c in range(r):
            for j in range(ncol):
                tscr[slot * ncol + j, pl.ds(c, tm // r, stride=r), :] = ref[0, c, :, LANES * j:LANES * (j + 1)]
        return jnp.concatenate([tscr[slot * ncol + j] for j in range(ncol)], axis=1)

    o0, l0 = o0_ref[0], l0_ref[0]
    o1, l1 = token_major(o1_ref, 0), token_major(l1_ref, 1)
    o2, l2 = token_major(o2_ref, 2), token_major(l2_ref, 3)
    top = jnp.maximum(jnp.maximum(l0, l1), l2)
    w0, w1, w2 = jnp.exp2(l0 - top), jnp.exp2(l1 - top), jnp.exp2(l2 - top)
    ob = (w0 * o0 + w1 * o1 + w2 * o2) / (w0 + w1 + w2)
    na = oat_ref.shape[2]
    y = _proj_t(oat_ref, w_ref[:na, :]) + jnp.dot(ob.astype(BF16), w_ref[na:, :], preferred_element_type=F32)
    y_ref[0] = _mlp_tail(x_ref[0] + g_ref[0] * y, mlp_refs)


def _t_spec(a, tm):
    return pl.BlockSpec((1, tm // a.shape[-1]) + a.shape[2:], lambda bi, i: (bi, i, 0, 0))


def _even_out(x, g1, oat, o_dil, l_dil, w_out, mlp_args, tm):
    b, s, d = x.shape
    tok = lambda w: pl.BlockSpec((1, tm, w), lambda bi, i: (bi, i, 0))

    def dil_spec(a):
        if a.ndim == 3:
            return tok(DIL_W)
        r = a.shape[1]
        return pl.BlockSpec((1, r, tm // r, DIL_W), lambda bi, i: (bi, 0, i, 0))

    return pl.pallas_call(
        _even_out_kernel,
        grid=(b, s // tm),
        in_specs=[tok(d), pl.BlockSpec((1, 1, d), lambda bi, i: (bi, 0, 0)), _t_spec(oat, tm)]
                 + [dil_spec(a) for a in (*o_dil, *l_dil)]
                 + [pl.BlockSpec(w_out.shape, lambda bi, i: (0, 0), pipeline_mode=pl.Buffered(1))]
                 + _mlp_specs(mlp_args, d),
        out_specs=tok(d),
        out_shape=jax.ShapeDtypeStruct((b, s, d), F32),
        scratch_shapes=[pltpu.VMEM((4 * DIL_W // LANES, tm, LANES), F32)],
        compiler_params=_cparams(("parallel", "arbitrary")),
        name="even_out_mlp",
    )(x, g1, oat, *o_dil, *l_dil, w_out, *mlp_args)


def _odd_out_kernel(x_ref, g_ref, oct_ref, odt_ref, w_ref, *rest):
    *mlp_refs, y_ref = rest
    nc = oct_ref.shape[2]
    y = _proj_t(oct_ref, w_ref[:nc, :]) + _proj_t(odt_ref, w_ref[nc:, :])
    y_ref[0] = _mlp_tail(x_ref[0] + g_ref[0] * y, mlp_refs)


def _odd_out(x, g1, oct, odt, w_out, mlp_args, tm):
    b, s, d = x.shape
    tok = lambda w: pl.BlockSpec((1, tm, w), lambda bi, i: (bi, i, 0))
    return pl.pallas_call(
        _odd_out_kernel,
        grid=(b, s // tm),
        in_specs=[tok(d), pl.BlockSpec((1, 1, d), lambda bi, i: (bi, 0, 0)), _t_spec(oct, tm), _t_spec(odt, tm),
                  pl.BlockSpec(w_out.shape, lambda bi, i: (0, 0), pipeline_mode=pl.Buffered(1))]
                 + _mlp_specs(mlp_args, d),
        out_specs=tok(d),
        out_shape=jax.ShapeDtypeStruct((b, s, d), F32),
        compiler_params=_cparams(("parallel", "arbitrary")),
        name="odd_out_mlp",
    )(x, g1, oct, odt, w_out, *mlp_args)


def _rope_tables(positions):
    pos = positions.astype(F32)[:, :, None]

    def cs(half):
        inv = ROPE_THETA ** (-jnp.arange(half, dtype=F32) / half)
        ang = pos * inv
        return jnp.cos(ang), jnp.sin(ang)

    c, s = cs(HEAD_DIM // 2)
    c64 = jnp.concatenate([c, c, c, c], axis=-1)
    s64 = jnp.concatenate([-s, s, -s, s], axis=-1)
    c, s = cs(MLA_ROPE // 2)
    ones = jnp.ones(pos.shape[:2] + (MLA_NOPE,), F32)
    pad = LANES - MLA_QK
    cm = jnp.concatenate([ones, c, c, jnp.ones(pos.shape[:2] + (pad,), F32)], axis=-1)
    sm = jnp.concatenate([0 * ones, -s, s, jnp.zeros(pos.shape[:2] + (pad,), F32)], axis=-1)
    return cm, sm, c64, s64


def _partner(n, width):
    idx = np.arange(n)
    return np.where(idx % width < width // 2, idx + width // 2, idx - width // 2)


def _even_weights(w_in, w_uq, w_ukv, qn, kn):
    o2 = MLA_Q_RANK + MLA_KV_RANK
    o3 = o2 + MLA_ROPE
    d = w_in.shape[0]
    p_rope = _partner(MLA_ROPE, MLA_ROPE)
    zeros = lambda n: jnp.zeros((d, n), w_in.dtype)
    kr = w_in[:, o2:o3]
    tail = LANES - MLA_QK
    krp = jnp.concatenate([zeros(MLA_NOPE), kr, zeros(tail)], axis=1)
    krp_rot = jnp.concatenate([zeros(MLA_NOPE), kr[:, p_rope], zeros(tail)], axis=1)
    dil = w_in[:, o3:].reshape(d, 3, len(DIL_CONFIGS), DIL_W)
    dil_main = dil.transpose(0, 2, 1, 3).reshape(d, -1)
    dil_rot = dil[:, :2][..., _partner(DIL_W, HEAD_DIM)].transpose(0, 2, 1, 3).reshape(d, -1)
    win = jnp.concatenate([w_in[:, :o2], krp, krp_rot, dil_main, dil_rot], axis=1)

    padq = ((0, 0), (0, 0), (0, tail))
    wuq_rot = jnp.concatenate([w_uq[:, :, :MLA_NOPE], w_uq[:, :, MLA_NOPE:][:, :, p_rope]], axis=-1)
    wuq = jnp.concatenate([jnp.pad(w_uq, padq).reshape(MLA_Q_RANK, -1),
                           jnp.pad(wuq_rot, padq).reshape(MLA_Q_RANK, -1)], axis=1)
    wuk = jnp.pad(w_ukv[:, :, :MLA_NOPE], ((0, 0), (0, 0), (0, LANES - MLA_NOPE))).reshape(MLA_KV_RANK, -1)
    wvt = w_ukv[:, :, MLA_NOPE:].reshape(MLA_KV_RANK, -1).T

    def gains(g):
        rot = jnp.concatenate([g[:MLA_NOPE], g[MLA_NOPE:][p_rope]])
        return jnp.stack([jnp.pad(g, (0, tail)), jnp.pad(rot, (0, tail))]).astype(F32)

    return win.astype(BF16), wuq.astype(BF16), wuk.astype(BF16), wvt.astype(BF16), gains(qn), gains(kn)


def _odd_weights(w_in):
    nqk = DIFF_HEADS * LANES
    nm = MOBA_HEADS * HEAD_DIM
    main = jnp.concatenate([w_in[:, :2 * nqk], w_in[:, 3 * nqk:3 * nqk + 2 * nm]], axis=1)
    win = jnp.concatenate([main, main[:, _partner(main.shape[1], HEAD_DIM)]], axis=1)
    wvt = jnp.concatenate([w_in[:, 2 * nqk:3 * nqk], w_in[:, 3 * nqk + 2 * nm:]], axis=1).T
    return win.astype(BF16), wvt.astype(BF16)


def _gains64(g):
    rot = g[_partner(HEAD_DIM, HEAD_DIM)]
    return jnp.stack([jnp.concatenate([g, g]), jnp.concatenate([rot, rot])]).astype(F32)


def kernel(x, c, positions, ada_w, ada_b, norm_mix, norm_mlp, mlp_w1, mlp_w2, even_w_in, even_w_out, mla_q_lat_norm, mla_kv_lat_norm, mla_w_uq, mla_w_ukv, mla_q_norm, mla_k_norm, dil_q_norm, dil_k_norm, odd_w_in, odd_w_out, diff_q_norm, diff_k_norm, diff_lambda, diff_subln, moba_q_norm, moba_k_norm):
    b, s, d = x.shape
    depth = ada_w.shape[0]
    t_attn = MOBA_BLOCK
    tm_in = tm_out = 2 * t_attn

    mod = _adaln(c, ada_w, ada_b)
    cm, sm, c64, s64 = _rope_tables(positions)

    for layer in range(depth):
        sh1, sc1, g1, sh2, sc2, g2 = [mod[layer, :, d * t:d * (t + 1)].reshape(b, 1, d) for t in range(6)]
        mlp_args = (sh2, sc2, g2, _row(norm_mlp[layer]), mlp_w1[layer].astype(BF16), mlp_w2[layer].astype(BF16))
        i = layer // 2
        if layer % 2 == 0:
            win, wuq, wuk, wvt, qn, kn = _even_weights(even_w_in[i], mla_w_uq[i], mla_w_ukv[i],
                                                       mla_q_norm[i], mla_k_norm[i])
            q, k, vt, *dils = _even_in(x, sh1, sc1, _row(norm_mix[layer]), win, wuq, wuk, wvt,
                                       _row(mla_q_lat_norm[i]), _row(mla_kv_lat_norm[i]), qn, kn,
                                       _gains64(dil_q_norm[i]), _gains64(dil_k_norm[i]), cm, sm, c64, s64, tm_in, t_attn)
            o_at = _mla_attention(q, k, vt, nh=4)
            o_dil, l_dil = zip(*[_sliding_window(dg, 4, f"sliding_window_g{g}") for g, dg in enumerate(dils)])
            x = _even_out(x, g1, o_at, o_dil, l_dil, even_w_out[i].astype(BF16), mlp_args, tm_out)
        else:
            lam_init = 0.8 - 0.6 * math.exp(-0.3 * layer)
            win, wvt = _odd_weights(odd_w_in[i])
            qc, kc, qm, qmf, km, vct, vmt, kmean = _odd_in(
                x, sh1, sc1, _row(norm_mix[layer]), win, wvt,
                _gains64(diff_q_norm[i]), _gains64(diff_k_norm[i]), _gains64(moba_q_norm[i]),
                _gains64(moba_k_norm[i]),
                c64, s64, tm_in, t_attn)
            o_ct = _diff_attention(qc, kc, vct, diff_lambda[i].astype(F32),
                                   diff_subln[i].reshape(-1, 1).astype(F32), lam_init, nh=2)
            o_dt = _moba_attention(qm, qmf, km, vmt, kmean.reshape(b, s // MOBA_BLOCK, kmean.shape[-1]), npair=2)
            x = _odd_out(x, g1, o_ct, o_dt, odd_w_out[i].astype(BF16), mlp_args, tm_out)
    return x
```

```python
import functools
import math

import jax
import jax.numpy as jnp
import numpy as np
from jax import lax
from jax.experimental import pallas as pl
from jax.experimental.pallas import tpu as pltpu

F32 = jnp.float32
BF16 = jnp.bfloat16

LANES = 128
HEAD_DIM = 64
ROPE_THETA = 10000.0
NORM_EPS = 1e-6
NEG_INF = -1e30
LOG2E = math.log2(math.e)

MLA_HEADS = 8
MLA_Q_RANK = 384
MLA_KV_RANK = 256
MLA_NOPE = 64
MLA_ROPE = 32
MLA_QK = MLA_NOPE + MLA_ROPE
DIL_CONFIGS = ((128, 1), (512, 4), (2048, 16))
DIL_HEADS = 4
DIL_W = DIL_HEADS * HEAD_DIM
DIFF_HEADS = 4
MOBA_HEADS = 8
MOBA_BLOCK = 256
MOBA_TOPK = 3
SW_BLOCK = 128
ONES_ROWS = 16

VMEM_LIMIT = 56 * 1024 * 1024


def _cparams(sem):
    return pltpu.CompilerParams(dimension_semantics=sem, vmem_limit_bytes=VMEM_LIMIT)


def _nt_dot(a, b):
    return lax.dot_general(a, b, (((1,), (1,)), ((), ())), preferred_element_type=F32)


def _tn_dot(a, b):
    return lax.dot_general(a, b, (((0,), (0,)), ((), ())), preferred_element_type=F32)


def _rms(x, w):
    return x * lax.rsqrt(jnp.mean(x * x, axis=-1, keepdims=True) + NORM_EPS) * w


def _lane_iota(shape):
    return lax.broadcasted_iota(jnp.int32, shape, len(shape) - 1)


def _adaln_kernel(c_ref, w_ref, b_ref, o_ref):
    c = c_ref[...]
    cond = c * (1.0 / (1.0 + jnp.exp(-c)))
    o_ref[0] = jnp.dot(cond, w_ref[0], preferred_element_type=F32,
                       precision=lax.Precision.HIGHEST) + b_ref[0]


def _adaln(c, ada_w, ada_b):
    depth, d, n = ada_w.shape
    b = c.shape[0]
    tn = 1024
    return pl.pallas_call(
        _adaln_kernel,
        grid=(depth, n // tn),
        in_specs=[pl.BlockSpec((b, d), lambda l, j: (0, 0)),
                  pl.BlockSpec((1, d, tn), lambda l, j: (l, 0, j)),
                  pl.BlockSpec((1, 1, tn), lambda l, j: (l, 0, j))],
        out_specs=pl.BlockSpec((1, b, tn), lambda l, j: (l, 0, j)),
        out_shape=jax.ShapeDtypeStruct((depth, b, n), F32),
        compiler_params=_cparams(("parallel", "parallel")),
        name="adaln",
    )(c, ada_w, ada_b.reshape(depth, 1, n))


def _inv_rms64(x):
    lane = _lane_iota((1, LANES))
    lo = lane < HEAD_DIM
    sq = x * x
    s_lo = jnp.sum(jnp.where(lo, sq, 0.0), axis=-1, keepdims=True)
    s_hi = jnp.sum(jnp.where(lo, 0.0, sq), axis=-1, keepdims=True)
    return lax.rsqrt(jnp.where(lo, s_lo, s_hi) * (1.0 / HEAD_DIM) + NORM_EPS)


def _store_t_blocks(out_ref, xt):
    t = out_ref.shape[-1]
    for j in range(out_ref.shape[1]):
        out_ref[0, j] = xt[:, t * j:t * (j + 1)]


def _t_blocks(b, s, rows, tm, t):
    return (pl.BlockSpec((1, tm // t, rows, t), lambda bi, i: (bi, i, 0, 0)),
            jax.ShapeDtypeStruct((b, s // t, rows, t), BF16))


def _even_in_kernel(x_ref, sh_ref, sc_ref, nw_ref, win_ref, wuq_ref, wuk_ref, wvt_ref,
                    qlat_ref, kvlat_ref, qn_ref, kn_ref, dqn_ref, dkn_ref,
                    cm_ref, sm_ref, c64_ref, s64_ref,
                    q_out, k_out, vt_out, d0_out, d1_out, d2_out, dscr):
    x = x_ref[0]
    tm = x.shape[0]
    h = _rms(x, nw_ref[...]) * (1.0 + sc_ref[0]) + sh_ref[0]
    u = jnp.dot(h.astype(BF16), win_ref[...], preferred_element_type=F32)

    o1 = MLA_Q_RANK
    o2 = o1 + MLA_KV_RANK
    o_kr, o_krr, o_dil = o2, o2 + LANES, o2 + 2 * LANES
    n_dil = 3 * DIL_W * len(DIL_CONFIGS)
    o_rot = o_dil + n_dil
    nq = MLA_HEADS * LANES
    cqn = _rms(u[:, :o1], qlat_ref[...]).astype(BF16)
    qp = jnp.dot(cqn, wuq_ref[...], preferred_element_type=F32)
    ckvn = _rms(u[:, o1:o2], kvlat_ref[...]).astype(BF16)
    kvp = jnp.dot(ckvn, wuk_ref[...], preferred_element_type=F32)
    _store_t_blocks(vt_out, _nt_dot(wvt_ref[...], ckvn).astype(BF16))
    kr, kr_rot = u[:, o_kr:o_kr + LANES], u[:, o_krr:o_krr + LANES]

    cm, sm = cm_ref[0], sm_ref[0]
    q_scale = MLA_QK ** -0.5 * LOG2E
    qgc, qgs = qn_ref[0:1] * cm * q_scale, qn_ref[1:2] * sm * q_scale
    kgc, kgs = kn_ref[0:1] * cm, kn_ref[1:2] * sm
    kr_term = kr_rot * kgs
    for hd in range(MLA_HEADS):
        sl = slice(LANES * hd, LANES * (hd + 1))
        qh = qp[:, sl]
        inv = lax.rsqrt(jnp.sum(qh * qh, -1, keepdims=True) * (1.0 / MLA_QK) + NORM_EPS)
        q_out[0, :, sl] = (inv * (qh * qgc + qp[:, nq + LANES * hd:nq + LANES * (hd + 1)] * qgs)).astype(BF16)
        kh = kvp[:, sl] + kr
        inv = lax.rsqrt(jnp.sum(kh * kh, -1, keepdims=True) * (1.0 / MLA_QK) + NORM_EPS)
        k_out[0, :, sl] = (inv * (kh * kgc + kr_term)).astype(BF16)

    c64, s64 = c64_ref[0], s64_ref[0]
    d_scale = HEAD_DIM ** -0.5 * LOG2E
    dqc, dqs = dqn_ref[0:1] * c64 * d_scale, dqn_ref[1:2] * s64 * d_scale
    dkc, dks = dkn_ref[0:1] * c64, dkn_ref[1:2] * s64
    ncol = DIL_W // LANES
    for g, d_out in enumerate((d0_out, d1_out, d2_out)):
        _, r = DIL_CONFIGS[g]
        base = o_dil + 3 * DIL_W * g
        rbase = o_rot + 2 * DIL_W * g
        for j in range(3 * ncol):
            xc = u[:, base + LANES * j:base + LANES * (j + 1)]
            if j < 2 * ncol:
                xr = u[:, rbase + LANES * j:rbase + LANES * (j + 1)]
                gc, gs = (dqc, dqs) if j < ncol else (dkc, dks)
                xc = _inv_rms64(xc) * (xc * gc + xr * gs)
            if r == 1:
                d_out[0, :, LANES * j:LANES * (j + 1)] = xc.astype(BF16)
            else:
                dscr[j] = xc
                for c in range(r):
                    d_out[0, c, :, LANES * j:LANES * (j + 1)] = dscr[j, pl.ds(c, tm // r, stride=r), :].astype(BF16)


def _row(v):
    return v.reshape(1, -1).astype(F32)


def _even_in(x, sh, sc, nw, win, wuq, wuk, wvt, qlat, kvlat, qn, kn, dqn, dkn, cm, sm, c64, s64, tm, t):
    b, s, d = x.shape
    tok = lambda w: pl.BlockSpec((1, tm, w), lambda bi, i: (bi, i, 0))
    per_b = pl.BlockSpec((1, 1, d), lambda bi, i: (bi, 0, 0))
    full = lambda a: pl.BlockSpec(a.shape, lambda bi, i: (0,) * a.ndim, pipeline_mode=pl.Buffered(1))
    vt_spec, vt_shape = _t_blocks(b, s, wvt.shape[0], tm, t)
    dil_specs, dil_shapes = [], []
    for _, r in DIL_CONFIGS:
        if r == 1:
            dil_specs.append(tok(3 * DIL_W))
            dil_shapes.append(jax.ShapeDtypeStruct((b, s, 3 * DIL_W), BF16))
        else:
            dil_specs.append(pl.BlockSpec((1, r, tm // r, 3 * DIL_W), lambda bi, i: (bi, 0, i, 0)))
            dil_shapes.append(jax.ShapeDtypeStruct((b, r, s // r, 3 * DIL_W), BF16))
    return pl.pallas_call(
        _even_in_kernel,
        grid=(b, s // tm),
        in_specs=[tok(d), per_b, per_b, full(nw), full(win), full(wuq), full(wuk), full(wvt),
                  full(qlat), full(kvlat), full(qn), full(kn), full(dqn), full(dkn),
                  tok(LANES), tok(LANES), tok(LANES), tok(LANES)],
        out_specs=[tok(MLA_HEADS * LANES), tok(MLA_HEADS * LANES), vt_spec] + dil_specs,
        out_shape=[jax.ShapeDtypeStruct((b, s, MLA_HEADS * LANES), BF16),
                   jax.ShapeDtypeStruct((b, s, MLA_HEADS * LANES), BF16), vt_shape] + dil_shapes,
        scratch_shapes=[pltpu.VMEM((3 * DIL_W // LANES, tm, LANES), F32)],
        compiler_params=_cparams(("parallel", "arbitrary")),
        name="even_in_proj",
    )(x, sh, sc, nw, win, wuq, wuk, wvt, qlat, kvlat, qn, kn, dqn, dkn, cm, sm, c64, s64)


def _odd_in_kernel(x_ref, sh_ref, sc_ref, nw_ref, win_ref, wvt_ref, dqn_ref, dkn_ref, mqn_ref, mkn_ref,
                   c64_ref, s64_ref,
                   qc_out, kc_out, qm_out, qmf_out, km_out, vct_out, vmt_out, kmean_out):
    x = x_ref[0]
    tm = x.shape[0]
    h = (_rms(x, nw_ref[...]) * (1.0 + sc_ref[0]) + sh_ref[0]).astype(BF16)
    u = jnp.dot(h, win_ref[...], preferred_element_type=F32)
    vt = _nt_dot(wvt_ref[...], h).astype(BF16)
    nvc = vct_out.shape[2]
    _store_t_blocks(vct_out, vt[:nvc])
    _store_t_blocks(vmt_out, vt[nvc:])
    c64, s64 = c64_ref[0], s64_ref[0]
    scale = HEAD_DIM ** -0.5 * LOG2E
    nqk = DIFF_HEADS * LANES
    nm = MOBA_HEADS * HEAD_DIM
    rot = 2 * nqk + 2 * nm

    def tables(gain_ref, mult):
        return gain_ref[0:1] * c64 * mult, gain_ref[1:2] * s64 * mult

    def roped(off, col, tab):
        a = off + LANES * col
        xc, xr = u[:, a:a + LANES], u[:, rot + a:rot + a + LANES]
        return _inv_rms64(xc) * (xc * tab[0] + xr * tab[1])

    t_qc, t_kc, t_qm, t_km = tables(dqn_ref, scale), tables(dkn_ref, 1.0), tables(mqn_ref, 1.0), tables(mkn_ref, 1.0)
    for col in range(nqk // LANES):
        sl = slice(LANES * col, LANES * (col + 1))
        qc_out[0, :, sl] = roped(0, col, t_qc).astype(BF16)
        kc_out[0, :, sl] = roped(nqk, col, t_kc).astype(BF16)

    base = 2 * nqk
    for col in range(nm // LANES):
        sl = slice(LANES * col, LANES * (col + 1))
        qm = roped(base, col, t_qm)
        qmf_out[0, :, sl] = qm
        qm_out[0, :, sl] = (qm * scale).astype(BF16)
        km = roped(base + nm, col, t_km)
        km_out[0, :, sl] = km.astype(BF16)
        for blk in range(tm // MOBA_BLOCK):
            rows = slice(MOBA_BLOCK * blk, MOBA_BLOCK * (blk + 1))
            kmean_out[0, 0, blk:blk + 1, sl] = jnp.mean(km[rows], axis=0, keepdims=True)


def _odd_in(x, sh, sc, nw, win, wvt, dqn, dkn, mqn, mkn, c64, s64, tm, t):
    b, s, d = x.shape
    tok = lambda w: pl.BlockSpec((1, tm, w), lambda bi, i: (bi, i, 0))
    per_b = pl.BlockSpec((1, 1, d), lambda bi, i: (bi, 0, 0))
    full = lambda a: pl.BlockSpec(a.shape, lambda bi, i: (0,) * a.ndim, pipeline_mode=pl.Buffered(1))
    w = DIFF_HEADS * LANES
    nblk = tm // MOBA_BLOCK
    shp = lambda dt: jax.ShapeDtypeStruct((b, s, w), dt)
    vt_spec, vt_shape = _t_blocks(b, s, wvt.shape[0] // 2, tm, t)
    return pl.pallas_call(
        _odd_in_kernel,
        grid=(b, s // tm),
        in_specs=[tok(d), per_b, per_b, full(nw), full(win), full(wvt), full(dqn), full(dkn), full(mqn), full(mkn),
                  tok(LANES), tok(LANES)],
        out_specs=[tok(w)] * 5 + [vt_spec, vt_spec, pl.BlockSpec((1, 1, nblk, w), lambda bi, i: (bi, i, 0, 0))],
        out_shape=[shp(BF16), shp(BF16), shp(BF16), shp(F32), shp(BF16), vt_shape, vt_shape,
                   jax.ShapeDtypeStruct((b, s // tm, nblk, w), F32)],
        compiler_params=_cparams(("parallel", "arbitrary")),
        name="odd_in_proj",
    )(x, sh, sc, nw, win, wvt, dqn, dkn, mqn, mkn, c64, s64)


def _q_tiles_t(q, split_halves):
    qt = q.astype(F32).T
    if not split_halves:
        return [qt.astype(BF16)]
    row = lax.broadcasted_iota(jnp.int32, qt.shape, 0)
    return [jnp.where((row >= HEAD_DIM) == bool(half), qt, 0.0).astype(BF16) for half in range(2)]


def _flash_t(qts, k_ref, vt_ref, k_slices, v_rows, scr, i, t, past_mask=None):
    m_scr, acc_scr, s_scr, p_scr, a_scr = scr
    n = len(qts)
    ones = jnp.ones((ONES_ROWS, t), BF16)

    def qk(blk, s):
        b0 = pl.multiple_of(blk * t, t)
        return jnp.dot(k_ref[0, pl.ds(b0, t), k_slices[s]], qts[s], preferred_element_type=F32)

    def past_qk(blk, s):
        st = qk(blk, s)
        if past_mask is not None:
            st = jnp.where(past_mask(s, blk), st, NEG_INF)
        return st

    def pv(blk, s, p):
        vt = jnp.concatenate([vt_ref[0, blk, v_rows[s], :], ones], axis=0)
        return jnp.dot(vt, p, preferred_element_type=F32)

    krow = lax.broadcasted_iota(jnp.int32, (t, t), 0)
    qcol = lax.broadcasted_iota(jnp.int32, (t, t), 1)
    causal = krow <= qcol
    last_past = jnp.maximum(i - 1, 0)
    for s in range(n):
        st = jnp.where(causal, qk(i, s), NEG_INF)
        m = jnp.max(st, axis=0, keepdims=True)
        m_scr[s] = m
        p_scr[0, s] = jnp.exp2(st - m).astype(BF16)
        p_scr[1, s] = jnp.zeros((t, t), BF16)
        a_scr[0, s] = jnp.ones_like(m)
        a_scr[1, s] = jnp.ones_like(m)
        acc_scr[s] = jnp.zeros(acc_scr.shape[1:], F32)
        s_scr[0, s] = past_qk(0, s)
        s_scr[1, s] = past_qk(jnp.minimum(1, last_past), s)

    def softmax_stage(slot, valid, rs, ws):
        for s in range(n):
            st = s_scr[2 * rs + slot, s]
            cmax = jnp.max(st, axis=0, keepdims=True)
            if valid is not None:
                cmax = jnp.where(valid, cmax, NEG_INF)
            m_prev = m_scr[s]
            m_new = jnp.maximum(m_prev, cmax)
            a_scr[2 * ws + slot, s] = jnp.exp2(m_prev - m_new)
            p_scr[2 * ws + slot, s] = jnp.exp2(st - m_new).astype(BF16)
            m_scr[s] = m_new

    def body(kk, rs, ws):
        first = kk == 0
        pa_blk = jnp.where(first, i, 2 * kk - 2)
        pb_blk = jnp.where(first, i, 2 * kk - 1)
        p_prev = [[p_scr[2 * rs + sl, s] for s in range(n)] for sl in range(2)]
        a_prev = [[a_scr[2 * rs + sl, s] for s in range(n)] for sl in range(2)]
        s_next = [[past_qk(jnp.minimum(2 * kk + 2 + sl, last_past), s) for s in range(n)] for sl in range(2)]
        pvs = [[pv(blk, s, p_prev[sl][s]) for s in range(n)] for sl, blk in enumerate((pa_blk, pb_blk))]
        softmax_stage(0, None, rs, ws)
        softmax_stage(1, 2 * kk + 1 < i, rs, ws)
        for sl in range(2):
            for s in range(n):
                s_scr[2 * ws + sl, s] = s_next[sl][s]
        for s in range(n):
            acc_scr[s] = a_prev[1][s] * (a_prev[0][s] * acc_scr[s] + pvs[0][s]) + pvs[1][s]

    def two_bodies(kp, carry):
        body(2 * kp, 0, 1)
        body(2 * kp + 1, 1, 0)
        return carry

    nbody = (i + 1) // 2
    lax.fori_loop(0, nbody // 2, two_bodies, 0)

    @pl.when(nbody % 2 == 1)
    def _():
        body(nbody - 1, 0, 1)

    fs = nbody % 2
    kl = jnp.maximum(nbody - 1, 0)
    none = i == 0
    fa_blk = jnp.where(none, i, 2 * kl)
    fb_blk = jnp.where(none, i, jnp.minimum(2 * kl + 1, last_past))
    fb_scale = jnp.where(jnp.logical_or(none, i % 2 == 0), 1.0, 0.0)
    for s in range(n):
        acc_scr[s] = (a_scr[2 * fs + 1, s] * (a_scr[2 * fs, s] * acc_scr[s] + pv(fa_blk, s, p_scr[2 * fs, s]))
                      + fb_scale * pv(fb_blk, s, p_scr[2 * fs + 1, s]))


def _flash_scratch(n, nv, t):
    return [pltpu.VMEM((n, 1, t), F32), pltpu.VMEM((n, nv + ONES_ROWS, t), F32), pltpu.VMEM((4, n, t, t), F32),
            pltpu.VMEM((4, n, t, t), BF16), pltpu.VMEM((4, n, 1, t), F32)]


def _normalised(acc_scr, s, nv):
    return acc_scr[s, :nv, :] / acc_scr[s, nv:nv + 1, :]


def _half_mask(x, half):
    lane = _lane_iota((1, LANES))
    return jnp.where((lane >= HEAD_DIM) == bool(half), x, jnp.zeros_like(x))


def _mla_kernel(q_ref, k_ref, vt_ref, ot_ref, *scr, t, nh):
    ks = [slice(LANES * s, LANES * (s + 1)) for s in range(nh)]
    vr = [slice(HEAD_DIM * s, HEAD_DIM * (s + 1)) for s in range(nh)]

    def tile(i, carry):
        rows = pl.ds(pl.multiple_of(i * t, t), t)
        qs = [_q_tiles_t(q_ref[0, rows, ks[s]], False)[0] for s in range(nh)]
        _flash_t(qs, k_ref, vt_ref, ks, vr, scr, i, t)
        for s in range(nh):
            ot_ref[0, i, vr[s], :] = _normalised(scr[1], s, HEAD_DIM).astype(ot_ref.dtype)
        return carry

    lax.fori_loop(0, ot_ref.shape[1], tile, 0)


def _resident(shape, index_map):
    return pl.BlockSpec(shape, index_map, pipeline_mode=pl.Buffered(1))


def _mla_attention(q, k, vt, nh):
    b, s, _ = q.shape
    _, nb, nv, t = vt.shape
    return pl.pallas_call(
        functools.partial(_mla_kernel, t=t, nh=nh),
        grid=(b, MLA_HEADS // nh),
        in_specs=[_resident((1, s, nh * LANES), lambda bi, p: (bi, 0, p)),
                  _resident((1, s, nh * LANES), lambda bi, p: (bi, 0, p)),
                  _resident((1, nb, nh * HEAD_DIM, t), lambda bi, p: (bi, 0, p, 0))],
        out_specs=pl.BlockSpec((1, nb, nh * HEAD_DIM, t), lambda bi, p: (bi, 0, p, 0)),
        out_shape=jax.ShapeDtypeStruct((b, nb, nv, t), BF16),
        scratch_shapes=_flash_scratch(nh, HEAD_DIM, t),
        compiler_params=_cparams(("parallel", "arbitrary")),
        name="mla_attention",
    )(q, k, vt)


def _diff_kernel(q_ref, k_ref, vt_ref, lam_ref, sub_ref, ot_ref, *scr, t, nh, lam_init):
    i = pl.program_id(2)
    qs, ks, vr = [], [], []
    for hd in range(nh):
        cols = slice(LANES * hd, LANES * (hd + 1))
        qs += _q_tiles_t(q_ref[0, :, cols], True)
        ks += [cols, cols]
        vr += [cols, cols]
    _flash_t(qs, k_ref, vt_ref, ks, vr, scr, i, t)
    lv = lam_ref[...]
    lam = (jnp.exp(jnp.sum(lv[0:1] * lv[1:2], keepdims=True))
           - jnp.exp(jnp.sum(lv[2:3] * lv[3:4], keepdims=True)) + lam_init)
    for hd in range(nh):
        o = _normalised(scr[1], 2 * hd, LANES) - lam * _normalised(scr[1], 2 * hd + 1, LANES)
        o = o * lax.rsqrt(jnp.mean(o * o, axis=0, keepdims=True) + NORM_EPS) * sub_ref[...]
        ot_ref[0, 0, LANES * hd:LANES * (hd + 1), :] = (o * (1.0 - lam_init)).astype(ot_ref.dtype)


def _diff_attention(q, k, vt, lam_rows, subln_col, lam_init, nh):
    b, s, w = q.shape
    _, nb, _, t = vt.shape
    return pl.pallas_call(
        functools.partial(_diff_kernel, t=t, nh=nh, lam_init=lam_init),
        grid=(b, DIFF_HEADS // nh, nb),
        in_specs=[pl.BlockSpec((1, t, nh * LANES), lambda bi, p, i: (bi, i, p)),
                  pl.BlockSpec((1, s, nh * LANES), lambda bi, p, i: (bi, 0, p)),
                  pl.BlockSpec((1, nb, nh * LANES, t), lambda bi, p, i: (bi, 0, p, 0)),
                  pl.BlockSpec(lam_rows.shape, lambda bi, p, i: (0, 0)),
                  pl.BlockSpec(subln_col.shape, lambda bi, p, i: (0, 0))],
        out_specs=pl.BlockSpec((1, 1, nh * LANES, t), lambda bi, p, i: (bi, i, p, 0)),
        out_shape=jax.ShapeDtypeStruct((b, nb, w, t), BF16),
        scratch_shapes=_flash_scratch(2 * nh, LANES, t),
        compiler_params=_cparams(("parallel", "parallel", "arbitrary")),
        name="diff_attention",
    )(q, k, vt, lam_rows, subln_col)


def _moba_kernel(q_ref, qf_ref, k_ref, vt_ref, kmean_ref, ot_ref, sel_scr, *scr, t, npair):
    i = pl.program_id(2)
    nb = kmean_ref.shape[1]
    brow = lax.broadcasted_iota(jnp.int32, (nb, t), 0).astype(F32)
    past = brow < i.astype(F32)
    qs, ks, vr = [], [], []
    for pr in range(npair):
        cols = slice(LANES * pr, LANES * (pr + 1))
        qf, kmean = qf_ref[0, :, cols], kmean_ref[0, :, cols]
        qs += _q_tiles_t(q_ref[0, :, cols], True)
        for half in range(2):
            s = 2 * pr + half
            gate = lax.dot_general(kmean, _half_mask(qf, half), (((1,), (1,)), ((), ())),
                                   preferred_element_type=F32, precision=lax.Precision.HIGHEST)
            gate = jnp.where(past, gate, NEG_INF)
            sel = jnp.zeros((nb, t), F32)
            for _ in range(MOBA_TOPK):
                top = jnp.max(gate, axis=0, keepdims=True)
                first = jnp.min(jnp.where(gate == top, brow, float(nb)), axis=0, keepdims=True)
                pick = brow == first
                sel = jnp.where(pick, 1.0, sel)
                gate = jnp.where(pick, 2 * NEG_INF, gate)
            sel_scr[s] = jnp.where(past, sel, 0.0)
            ks.append(cols)
            vr.append(slice(HEAD_DIM * s, HEAD_DIM * (s + 1)))
    _flash_t(qs, k_ref, vt_ref, ks, vr, scr, i, t,
             past_mask=lambda s, jj: sel_scr[s, pl.ds(jj, 1), :] > 0.5)
    for s in range(2 * npair):
        ot_ref[0, 0, vr[s], :] = _normalised(scr[1], s, HEAD_DIM).astype(ot_ref.dtype)


def _moba_attention(q, qf, k, vt, kmean, npair):
    b, s, w = q.shape
    _, nb, _, t = vt.shape
    nkm = kmean.shape[1]
    lanes = npair * LANES
    return pl.pallas_call(
        functools.partial(_moba_kernel, t=t, npair=npair),
        grid=(b, w // lanes, nb),
        in_specs=[pl.BlockSpec((1, t, lanes), lambda bi, p, i: (bi, i, p)),
                  pl.BlockSpec((1, t, lanes), lambda bi, p, i: (bi, i, p)),
                  pl.BlockSpec((1, s, lanes), lambda bi, p, i: (bi, 0, p)),
                  pl.BlockSpec((1, nb, lanes, t), lambda bi, p, i: (bi, 0, p, 0)),
                  pl.BlockSpec((1, nkm, lanes), lambda bi, p, i: (bi, 0, p))],
        out_specs=pl.BlockSpec((1, 1, lanes, t), lambda bi, p, i: (bi, i, p, 0)),
        out_shape=jax.ShapeDtypeStruct((b, nb, w, t), BF16),
        scratch_shapes=[pltpu.VMEM((2 * npair, nkm, t), F32)] + _flash_scratch(2 * npair, HEAD_DIM, t),
        compiler_params=_cparams(("parallel", "parallel", "arbitrary")),
        name="moba_attention",
    )(q, qf, k, vt, kmean)


def _sw_kernel(q_ref, kc_ref, vc_ref, kp_ref, vp_ref, o_ref, l_ref, *, nbk):
    i = pl.program_id(1)
    blk = SW_BLOCK
    qi = lax.broadcasted_iota(jnp.int32, (blk, 2 * blk), 0)
    kj = lax.broadcasted_iota(jnp.int32, (blk, 2 * blk), 1)
    band = (kj >= qi) & (kj <= qi + blk)
    lane = _lane_iota((1, LANES))
    heads = [(pair, half) for pair in range(DIL_W // LANES) for half in range(2)]

    def scores(n):
        rows = slice(blk * n, blk * (n + 1))
        q = q_ref[0, rows, :]
        if n == 0:
            kprev, valid = kp_ref[0], band & ((kj >= blk) | (i > 0))
        else:
            kprev, valid = kc_ref[0, blk * (n - 1):blk * n, :], band
        kk = jnp.concatenate([kprev, kc_ref[0, rows, :]], axis=0)
        out = []
        for pair, half in heads:
            sl = slice(LANES * pair, LANES * (pair + 1))
            out.append(jnp.where(valid, _nt_dot(_half_mask(q[:, sl], half), kk[:, sl]), NEG_INF))
        return out

    nxt = scores(0)
    for n in range(nbk):
        cur, rows = nxt, slice(blk * n, blk * (n + 1))
        if n + 1 < nbk:
            nxt = scores(n + 1)
        vprev = vp_ref[0] if n == 0 else vc_ref[0, blk * (n - 1):blk * n, :]
        vv = jnp.concatenate([vprev, vc_ref[0, rows, :]], axis=0)
        outs, lses = [], []
        for (pair, half), sc in zip(heads, cur):
            m = jnp.max(sc, axis=-1, keepdims=True)
            p = jnp.exp2(sc - m)
            l = jnp.sum(p, axis=-1, keepdims=True)
            sl = slice(LANES * pair, LANES * (pair + 1))
            outs.append(jnp.dot(p.astype(BF16), vv[:, sl], preferred_element_type=F32) / l)
            lses.append(m + jnp.log2(l))
        for pair in range(DIL_W // LANES):
            sl = slice(LANES * pair, LANES * (pair + 1))
            o_ref[0, rows, sl] = jnp.where(lane < HEAD_DIM, outs[2 * pair], outs[2 * pair + 1])
            l_ref[0, rows, sl] = jnp.where(lane < HEAD_DIM, lses[2 * pair], lses[2 * pair + 1])


def _sliding_window(dil, nbk, name):
    lead, (m, w3) = dil.shape[:-2], dil.shape[-2:]
    seqs = dil.reshape((-1, m, w3))
    nbk = min(nbk, m // SW_BLOCK)
    rows = SW_BLOCK * nbk
    cur = lambda off: pl.BlockSpec((1, rows, DIL_W), lambda n, i: (n, i, off))
    prev = lambda off: pl.BlockSpec((1, SW_BLOCK, DIL_W), lambda n, i: (n, jnp.maximum(i * nbk - 1, 0), off))
    out_spec = pl.BlockSpec((1, rows, DIL_W), lambda n, i: (n, i, 0))
    o, l = pl.pallas_call(
        functools.partial(_sw_kernel, nbk=nbk),
        grid=(seqs.shape[0], m // rows),
        in_specs=[cur(0), cur(1), cur(2), prev(1), prev(2)],
        out_specs=[out_spec, out_spec],
        out_shape=[jax.ShapeDtypeStruct((seqs.shape[0], m, DIL_W), F32)] * 2,
        compiler_params=_cparams(("parallel", "arbitrary")),
        name=name,
    )(seqs, seqs, seqs, seqs, seqs)
    return o.reshape(lead + (m, DIL_W)), l.reshape(lead + (m, DIL_W))


MLP_CHUNK = 1024


def _mlp_tail(y, mlp_refs):
    sh_ref, sc_ref, g_ref, nw_ref, w1_ref, w2_ref = mlp_refs
    h = (_rms(y, nw_ref[...]) * (1.0 + sc_ref[0]) + sh_ref[0]).astype(BF16)
    acc = jnp.zeros(y.shape, F32)
    for c in range(w1_ref.shape[1] // MLP_CHUNK):
        cols = slice(MLP_CHUNK * c, MLP_CHUNK * (c + 1))
        a = jnp.maximum(jnp.dot(h, w1_ref[:, cols], preferred_element_type=F32), 0.0)
        acc = acc + jnp.dot((a * a).astype(BF16), w2_ref[cols, :], preferred_element_type=F32)
    return y + g_ref[0] * acc


def _proj_t(ot_ref, w_rows):
    return jnp.concatenate([_tn_dot(ot_ref[0, j], w_rows) for j in range(ot_ref.shape[1])], axis=0)


def _mlp_specs(mlp_args, d):
    per_b = pl.BlockSpec((1, 1, d), lambda bi, i: (bi, 0, 0))
    const = lambda a: pl.BlockSpec(a.shape, lambda bi, i: (0, 0), pipeline_mode=pl.Buffered(1))
    sh, sc, g2, nw, w1, w2 = mlp_args
    return [per_b, per_b, per_b, pl.BlockSpec(nw.shape, lambda bi, i: (0, 0)), const(w1), const(w2)]


def _even_out_kernel(x_ref, g_ref, oat_ref, o0_ref, o1_ref, o2_ref, l0_ref, l1_ref, l2_ref, w_ref,
                     *rest):
    *mlp_refs, y_ref, tscr = rest
    tm = x_ref.shape[1]
    ncol = DIL_W // LANES

    def token_major(ref, slot):
        r = ref.shape[1]
        for c in range(r):
            for j in range(ncol):
                tscr[slot * ncol + j, pl.ds(c, tm // r, stride=r), :] = ref[0, c, :, LANES * j:LANES * (j + 1)]
        return jnp.concatenate([tscr[slot * ncol + j] for j in range(ncol)], axis=1)

    o0, l0 = o0_ref[0], l0_ref[0]
    o1, l1 = token_major(o1_ref, 0), token_major(l1_ref, 1)
    o2, l2 = token_major(o2_ref, 2), token_major(l2_ref, 3)
    top = jnp.maximum(jnp.maximum(l0, l1), l2)
    w0, w1, w2 = jnp.exp2(l0 - top), jnp.exp2(l1 - top), jnp.exp2(l2 - top)
    ob = (w0 * o0 + w1 * o1 + w2 * o2) / (w0 + w1 + w2)
    na = oat_ref.shape[2]
    y = _proj_t(oat_ref, w_ref[:na, :]) + jnp.dot(ob.astype(BF16), w_ref[na:, :], preferred_element_type=F32)
    y_ref[0] = _mlp_tail(x_ref[0] + g_ref[0] * y, mlp_refs)


def _t_spec(a, tm):
    return pl.BlockSpec((1, tm // a.shape[-1]) + a.shape[2:], lambda bi, i: (bi, i, 0, 0))


def _even_out(x, g1, oat, o_dil, l_dil, w_out, mlp_args, tm):
    b, s, d = x.shape
    tok = lambda w: pl.BlockSpec((1, tm, w), lambda bi, i: (bi, i, 0))

    def dil_spec(a):
        if a.ndim == 3:
            return tok(DIL_W)
        r = a.shape[1]
        return pl.BlockSpec((1, r, tm // r, DIL_W), lambda bi, i: (bi, 0, i, 0))

    return pl.pallas_call(
        _even_out_kernel,
        grid=(b, s // tm),
        in_specs=[tok(d), pl.BlockSpec((1, 1, d), lambda bi, i: (bi, 0, 0)), _t_spec(oat, tm)]
                 + [dil_spec(a) for a in (*o_dil, *l_dil)]
                 + [pl.BlockSpec(w_out.shape, lambda bi, i: (0, 0), pipeline_mode=pl.Buffered(1))]
                 + _mlp_specs(mlp_args, d),
        out_specs=tok(d),
        out_shape=jax.ShapeDtypeStruct((b, s, d), F32),
        scratch_shapes=[pltpu.VMEM((4 * DIL_W // LANES, tm, LANES), F32)],
        compiler_params=_cparams(("parallel", "arbitrary")),
        name="even_out_mlp",
    )(x, g1, oat, *o_dil, *l_dil, w_out, *mlp_args)


def _odd_out_kernel(x_ref, g_ref, oct_ref, odt_ref, w_ref, *rest):
    *mlp_refs, y_ref = rest
    nc = oct_ref.shape[2]
    y = _proj_t(oct_ref, w_ref[:nc, :]) + _proj_t(odt_ref, w_ref[nc:, :])
    y_ref[0] = _mlp_tail(x_ref[0] + g_ref[0] * y, mlp_refs)


def _odd_out(x, g1, oct, odt, w_out, mlp_args, tm):
    b, s, d = x.shape
    tok = lambda w: pl.BlockSpec((1, tm, w), lambda bi, i: (bi, i, 0))
    return pl.pallas_call(
        _odd_out_kernel,
        grid=(b, s // tm),
        in_specs=[tok(d), pl.BlockSpec((1, 1, d), lambda bi, i: (bi, 0, 0)), _t_spec(oct, tm), _t_spec(odt, tm),
                  pl.BlockSpec(w_out.shape, lambda bi, i: (0, 0), pipeline_mode=pl.Buffered(1))]
                 + _mlp_specs(mlp_args, d),
        out_specs=tok(d),
        out_shape=jax.ShapeDtypeStruct((b, s, d), F32),
        compiler_params=_cparams(("parallel", "arbitrary")),
        name="odd_out_mlp",
    )(x, g1, oct, odt, w_out, *mlp_args)


def _rope_tables(positions):
    pos = positions.astype(F32)[:, :, None]

    def cs(half):
        inv = ROPE_THETA ** (-jnp.arange(half, dtype=F32) / half)
        ang = pos * inv
        return jnp.cos(ang), jnp.sin(ang)

    c, s = cs(HEAD_DIM // 2)
    c64 = jnp.concatenate([c, c, c, c], axis=-1)
    s64 = jnp.concatenate([-s, s, -s, s], axis=-1)
    c, s = cs(MLA_ROPE // 2)
    ones = jnp.ones(pos.shape[:2] + (MLA_NOPE,), F32)
    pad = LANES - MLA_QK
    cm = jnp.concatenate([ones, c, c, jnp.ones(pos.shape[:2] + (pad,), F32)], axis=-1)
    sm = jnp.concatenate([0 * ones, -s, s, jnp.zeros(pos.shape[:2] + (pad,), F32)], axis=-1)
    return cm, sm, c64, s64


def _partner(n, width):
    idx = np.arange(n)
    return np.where(idx % width < width // 2, idx + width // 2, idx - width // 2)


def _even_weights(w_in, w_uq, w_ukv, qn, kn):
    o2 = MLA_Q_RANK + MLA_KV_RANK
    o3 = o2 + MLA_ROPE
    d = w_in.shape[0]
    p_rope = _partner(MLA_ROPE, MLA_ROPE)
    zeros = lambda n: jnp.zeros((d, n), w_in.dtype)
    kr = w_in[:, o2:o3]
    tail = LANES - MLA_QK
    krp = jnp.concatenate([zeros(MLA_NOPE), kr, zeros(tail)], axis=1)
    krp_rot = jnp.concatenate([zeros(MLA_NOPE), kr[:, p_rope], zeros(tail)], axis=1)
    dil = w_in[:, o3:].reshape(d, 3, len(DIL_CONFIGS), DIL_W)
    dil_main = dil.transpose(0, 2, 1, 3).reshape(d, -1)
    dil_rot = dil[:, :2][..., _partner(DIL_W, HEAD_DIM)].transpose(0, 2, 1, 3).reshape(d, -1)
    win = jnp.concatenate([w_in[:, :o2], krp, krp_rot, dil_main, dil_rot], axis=1)

    padq = ((0, 0), (0, 0), (0, tail))
    wuq_rot = jnp.concatenate([w_uq[:, :, :MLA_NOPE], w_uq[:, :, MLA_NOPE:][:, :, p_rope]], axis=-1)
    wuq = jnp.concatenate([jnp.pad(w_uq, padq).reshape(MLA_Q_RANK, -1),
                           jnp.pad(wuq_rot, padq).reshape(MLA_Q_RANK, -1)], axis=1)
    wuk = jnp.pad(w_ukv[:, :, :MLA_NOPE], ((0, 0), (0, 0), (0, LANES - MLA_NOPE))).reshape(MLA_KV_RANK, -1)
    wvt = w_ukv[:, :, MLA_NOPE:].reshape(MLA_KV_RANK, -1).T

    def gains(g):
        rot = jnp.concatenate([g[:MLA_NOPE], g[MLA_NOPE:][p_rope]])
        return jnp.stack([jnp.pad(g, (0, tail)), jnp.pad(rot, (0, tail))]).astype(F32)

    return win.astype(BF16), wuq.astype(BF16), wuk.astype(BF16), wvt.astype(BF16), gains(qn), gains(kn)


def _odd_weights(w_in):
    nqk = DIFF_HEADS * LANES
    nm = MOBA_HEADS * HEAD_DIM
    main = jnp.concatenate([w_in[:, :2 * nqk], w_in[:, 3 * nqk:3 * nqk + 2 * nm]], axis=1)
    win = jnp.concatenate([main, main[:, _partner(main.shape[1], HEAD_DIM)]], axis=1)
    wvt = jnp.concatenate([w_in[:, 2 * nqk:3 * nqk], w_in[:, 3 * nqk + 2 * nm:]], axis=1).T
    return win.astype(BF16), wvt.astype(BF16)


def _gains64(g):
    rot = g[_partner(HEAD_DIM, HEAD_DIM)]
    return jnp.stack([jnp.concatenate([g, g]), jnp.concatenate([rot, rot])]).astype(F32)


def kernel(x, c, positions, ada_w, ada_b, norm_mix, norm_mlp, mlp_w1, mlp_w2, even_w_in, even_w_out, mla_q_lat_norm, mla_kv_lat_norm, mla_w_uq, mla_w_ukv, mla_q_norm, mla_k_norm, dil_q_norm, dil_k_norm, odd_w_in, odd_w_out, diff_q_norm, diff_k_norm, diff_lambda, diff_subln, moba_q_norm, moba_k_norm):
    b, s, d = x.shape
    depth = ada_w.shape[0]
    t_attn = MOBA_BLOCK
    tm_in = tm_out = 2 * t_attn

    mod = _adaln(c, ada_w, ada_b)
    cm, sm, c64, s64 = _rope_tables(positions)

    for layer in range(depth):
        sh1, sc1, g1, sh2, sc2, g2 = [mod[layer, :, d * t:d * (t + 1)].reshape(b, 1, d) for t in range(6)]
        mlp_args = (sh2, sc2, g2, _row(norm_mlp[layer]), mlp_w1[layer].astype(BF16), mlp_w2[layer].astype(BF16))
        i = layer // 2
        if layer % 2 == 0:
            win, wuq, wuk, wvt, qn, kn = _even_weights(even_w_in[i], mla_w_uq[i], mla_w_ukv[i],
                                                       mla_q_norm[i], mla_k_norm[i])
            q, k, vt, *dils = _even_in(x, sh1, sc1, _row(norm_mix[layer]), win, wuq, wuk, wvt,
                                       _row(mla_q_lat_norm[i]), _row(mla_kv_lat_norm[i]), qn, kn,
                                       _gains64(dil_q_norm[i]), _gains64(dil_k_norm[i]), cm, sm, c64, s64, tm_in, t_attn)
            o_at = _mla_attention(q, k, vt, nh=4)
            o_dil, l_dil = zip(*[_sliding_window(dg, 4, f"sliding_window_g{g}") for g, dg in enumerate(dils)])
            x = _even_out(x, g1, o_at, o_dil, l_dil, even_w_out[i].astype(BF16), mlp_args, tm_out)
        else:
            lam_init = 0.8 - 0.6 * math.exp(-0.3 * layer)
            win, wvt = _odd_weights(odd_w_in[i])
            qc, kc, qm, qmf, km, vct, vmt, kmean = _odd_in(
                x, sh1, sc1, _row(norm_mix[layer]), win, wvt,
                _gains64(diff_q_norm[i]), _gains64(diff_k_norm[i]), _gains64(moba_q_norm[i]),
                _gains64(moba_k_norm[i]),
                c64, s64, tm_in, t_attn)
            o_ct = _diff_attention(qc, kc, vct, diff_lambda[i].astype(F32),
                                   diff_subln[i].reshape(-1, 1).astype(F32), lam_init, nh=2)
            o_dt = _moba_attention(qm, qmf, km, vmt, kmean.reshape(b, s // MOBA_BLOCK, kmean.shape[-1]), npair=2)
            x = _odd_out(x, g1, o_ct, o_dt, odd_w_out[i].astype(BF16), mlp_args, tm_out)
    return x
```

```python
import functools
import math

import jax
import jax.numpy as jnp
import numpy as np
from jax import lax
from jax.experimental import pallas as pl
from jax.experimental.pallas import tpu as pltpu

F32 = jnp.float32
BF16 = jnp.bfloat16

LANES = 128
HEAD_DIM = 64
ROPE_THETA = 10000.0
NORM_EPS = 1e-6
NEG_INF = -1e30
LOG2E = math.log2(math.e)

MLA_HEADS = 8
MLA_Q_RANK = 384
MLA_KV_RANK = 256
MLA_NOPE = 64
MLA_ROPE = 32
MLA_QK = MLA_NOPE + MLA_ROPE
DIL_CONFIGS = ((128, 1), (512, 4), (2048, 16))
DIL_HEADS = 4
DIL_W = DIL_HEADS * HEAD_DIM
DIFF_HEADS = 4
MOBA_HEADS = 8
MOBA_BLOCK = 256
MOBA_TOPK = 3
SW_BLOCK = 128
ONES_ROWS = 16

VMEM_LIMIT = 56 * 1024 * 1024

ATTN_BLOCK = MOBA_BLOCK
TOKEN_TILE = 2 * ATTN_BLOCK
MLP_CHUNK = 1024
ADALN_TILE = 1024
SW_BLOCKS_PER_STEP = 4
MLA_HEADS_PER_STEP = 4
DIFF_HEADS_PER_STEP = 2
MOBA_PAIRS_PER_STEP = 2


def _cparams(sem):
    return pltpu.CompilerParams(dimension_semantics=sem, vmem_limit_bytes=VMEM_LIMIT)


def _nt_dot(a, b):
    return lax.dot_general(a, b, (((1,), (1,)), ((), ())), preferred_element_type=F32)


def _tn_dot(a, b):
    return lax.dot_general(a, b, (((0,), (0,)), ((), ())), preferred_element_type=F32)


def _rms(x, w):
    return x * lax.rsqrt(jnp.mean(x * x, axis=-1, keepdims=True) + NORM_EPS) * w


def _lane_iota(shape):
    return lax.broadcasted_iota(jnp.int32, shape, len(shape) - 1)


def _adaln_kernel(c_ref, w_ref, b_ref, o_ref):
    c = c_ref[...]
    cond = c * (1.0 / (1.0 + jnp.exp(-c)))
    o_ref[0] = jnp.dot(cond, w_ref[0], preferred_element_type=F32,
                       precision=lax.Precision.HIGHEST) + b_ref[0]


def _adaln(c, ada_w, ada_b):
    depth, d, n = ada_w.shape
    b = c.shape[0]
    tn = ADALN_TILE
    return pl.pallas_call(
        _adaln_kernel,
        grid=(depth, n // tn),
        in_specs=[pl.BlockSpec((b, d), lambda l, j: (0, 0)),
                  pl.BlockSpec((1, d, tn), lambda l, j: (l, 0, j)),
                  pl.BlockSpec((1, 1, tn), lambda l, j: (l, 0, j))],
        out_specs=pl.BlockSpec((1, b, tn), lambda l, j: (l, 0, j)),
        out_shape=jax.ShapeDtypeStruct((depth, b, n), F32),
        compiler_params=_cparams(("parallel", "parallel")),
        name="adaln",
    )(c, ada_w, ada_b.reshape(depth, 1, n))


def _inv_rms64(x):
    lane = _lane_iota((1, LANES))
    lo = lane < HEAD_DIM
    sq = x * x
    s_lo = jnp.sum(jnp.where(lo, sq, 0.0), axis=-1, keepdims=True)
    s_hi = jnp.sum(jnp.where(lo, 0.0, sq), axis=-1, keepdims=True)
    return lax.rsqrt(jnp.where(lo, s_lo, s_hi) * (1.0 / HEAD_DIM) + NORM_EPS)


def _store_t_blocks(out_ref, xt):
    t = out_ref.shape[-1]
    for j in range(out_ref.shape[1]):
        out_ref[0, j] = xt[:, t * j:t * (j + 1)]


def _t_blocks(b, s, rows, tm, t):
    return (pl.BlockSpec((1, tm // t, rows, t), lambda bi, i: (bi, i, 0, 0)),
            jax.ShapeDtypeStruct((b, s // t, rows, t), BF16))


def _even_in_kernel(x_ref, sh_ref, sc_ref, nw_ref, win_ref, wuq_ref, wuk_ref, wvt_ref,
                    qlat_ref, kvlat_ref, qn_ref, kn_ref, dqn_ref, dkn_ref,
                    cm_ref, sm_ref, c64_ref, s64_ref,
                    q_out, k_out, vt_out, d0_out, d1_out, d2_out, dscr):
    x = x_ref[0]
    tm = x.shape[0]
    h = _rms(x, nw_ref[...]) * (1.0 + sc_ref[0]) + sh_ref[0]
    u = jnp.dot(h.astype(BF16), win_ref[...], preferred_element_type=F32)

    o1 = MLA_Q_RANK
    o2 = o1 + MLA_KV_RANK
    o_kr, o_krr, o_dil = o2, o2 + LANES, o2 + 2 * LANES
    n_dil = 3 * DIL_W * len(DIL_CONFIGS)
    o_rot = o_dil + n_dil
    nq = MLA_HEADS * LANES
    cqn = _rms(u[:, :o1], qlat_ref[...]).astype(BF16)
    qp = jnp.dot(cqn, wuq_ref[...], preferred_element_type=F32)
    ckvn = _rms(u[:, o1:o2], kvlat_ref[...]).astype(BF16)
    kvp = jnp.dot(ckvn, wuk_ref[...], preferred_element_type=F32)
    _store_t_blocks(vt_out, _nt_dot(wvt_ref[...], ckvn).astype(BF16))
    kr, kr_rot = u[:, o_kr:o_kr + LANES], u[:, o_krr:o_krr + LANES]

    cm, sm = cm_ref[0], sm_ref[0]
    q_scale = MLA_QK ** -0.5 * LOG2E
    qgc, qgs = qn_ref[0:1] * cm * q_scale, qn_ref[1:2] * sm * q_scale
    kgc, kgs = kn_ref[0:1] * cm, kn_ref[1:2] * sm
    kr_term = kr_rot * kgs
    for hd in range(MLA_HEADS):
        sl = slice(LANES * hd, LANES * (hd + 1))
        qh = qp[:, sl]
        inv = lax.rsqrt(jnp.sum(qh * qh, -1, keepdims=True) * (1.0 / MLA_QK) + NORM_EPS)
        q_out[0, :, sl] = (inv * (qh * qgc + qp[:, nq + LANES * hd:nq + LANES * (hd + 1)] * qgs)).astype(BF16)
        kh = kvp[:, sl] + kr
        inv = lax.rsqrt(jnp.sum(kh * kh, -1, keepdims=True) * (1.0 / MLA_QK) + NORM_EPS)
        k_out[0, :, sl] = (inv * (kh * kgc + kr_term)).astype(BF16)

    c64, s64 = c64_ref[0], s64_ref[0]
    d_scale = HEAD_DIM ** -0.5 * LOG2E
    dqc, dqs = dqn_ref[0:1] * c64 * d_scale, dqn_ref[1:2] * s64 * d_scale
    dkc, dks = dkn_ref[0:1] * c64, dkn_ref[1:2] * s64
    ncol = DIL_W // LANES
    for g, d_out in enumerate((d0_out, d1_out, d2_out)):
        _, r = DIL_CONFIGS[g]
        base = o_dil + 3 * DIL_W * g
        rbase = o_rot + 2 * DIL_W * g
        for j in range(3 * ncol):
            xc = u[:, base + LANES * j:base + LANES * (j + 1)]
            if j < 2 * ncol:
                xr = u[:, rbase + LANES * j:rbase + LANES * (j + 1)]
                gc, gs = (dqc, dqs) if j < ncol else (dkc, dks)
                xc = _inv_rms64(xc) * (xc * gc + xr * gs)
            if r == 1:
                d_out[0, :, LANES * j:LANES * (j + 1)] = xc.astype(BF16)
            else:
                dscr[j] = xc
                for c in range(r):
                    d_out[0, c, :, LANES * j:LANES * (j + 1)] = dscr[j, pl.ds(c, tm // r, stride=r), :].astype(BF16)


def _row(v):
    return v.reshape(1, -1).astype(F32)


def _even_in(x, sh, sc, nw, win, wuq, wuk, wvt, qlat, kvlat, qn, kn, dqn, dkn, cm, sm, c64, s64, tm, t):
    b, s, d = x.shape
    tok = lambda w: pl.BlockSpec((1, tm, w), lambda bi, i: (bi, i, 0))
    per_b = pl.BlockSpec((1, 1, d), lambda bi, i: (bi, 0, 0))
    full = lambda a: pl.BlockSpec(a.shape, lambda bi, i: (0,) * a.ndim, pipeline_mode=pl.Buffered(1))
    vt_spec, vt_shape = _t_blocks(b, s, wvt.shape[0], tm, t)
    dil_specs, dil_shapes = [], []
    for _, r in DIL_CONFIGS:
        if r == 1:
            dil_specs.append(tok(3 * DIL_W))
            dil_shapes.append(jax.ShapeDtypeStruct((b, s, 3 * DIL_W), BF16))
        else:
            dil_specs.append(pl.BlockSpec((1, r, tm // r, 3 * DIL_W), lambda bi, i: (bi, 0, i, 0)))
            dil_shapes.append(jax.ShapeDtypeStruct((b, r, s // r, 3 * DIL_W), BF16))
    return pl.pallas_call(
        _even_in_kernel,
        grid=(b, s // tm),
        in_specs=[tok(d), per_b, per_b, full(nw), full(win), full(wuq), full(wuk), full(wvt),
                  full(qlat), full(kvlat), full(qn), full(kn), full(dqn), full(dkn),
                  tok(LANES), tok(LANES), tok(LANES), tok(LANES)],
        out_specs=[tok(MLA_HEADS * LANES), tok(MLA_HEADS * LANES), vt_spec] + dil_specs,
        out_shape=[jax.ShapeDtypeStruct((b, s, MLA_HEADS * LANES), BF16),
                   jax.ShapeDtypeStruct((b, s, MLA_HEADS * LANES), BF16), vt_shape] + dil_shapes,
        scratch_shapes=[pltpu.VMEM((3 * DIL_W // LANES, tm, LANES), F32)],
        compiler_params=_cparams(("parallel", "arbitrary")),
        name="even_in_proj",
    )(x, sh, sc, nw, win, wuq, wuk, wvt, qlat, kvlat, qn, kn, dqn, dkn, cm, sm, c64, s64)


def _odd_in_kernel(x_ref, sh_ref, sc_ref, nw_ref, win_ref, wvt_ref, dqn_ref, dkn_ref, mqn_ref, mkn_ref,
                   c64_ref, s64_ref,
                   qc_out, kc_out, qm_out, qmf_out, km_out, vct_out, vmt_out, kmean_out):
    x = x_ref[0]
    tm = x.shape[0]
    h = (_rms(x, nw_ref[...]) * (1.0 + sc_ref[0]) + sh_ref[0]).astype(BF16)
    u = jnp.dot(h, win_ref[...], preferred_element_type=F32)
    vt = _nt_dot(wvt_ref[...], h).astype(BF16)
    nvc = vct_out.shape[2]
    _store_t_blocks(vct_out, vt[:nvc])
    _store_t_blocks(vmt_out, vt[nvc:])
    c64, s64 = c64_ref[0], s64_ref[0]
    scale = HEAD_DIM ** -0.5 * LOG2E
    nqk = DIFF_HEADS * LANES
    nm = MOBA_HEADS * HEAD_DIM
    rot = 2 * nqk + 2 * nm

    def tables(gain_ref, mult):
        return gain_ref[0:1] * c64 * mult, gain_ref[1:2] * s64 * mult

    def roped(off, col, tab):
        a = off + LANES * col
        xc, xr = u[:, a:a + LANES], u[:, rot + a:rot + a + LANES]
        return _inv_rms64(xc) * (xc * tab[0] + xr * tab[1])

    t_qc, t_kc, t_qm, t_km = tables(dqn_ref, scale), tables(dkn_ref, 1.0), tables(mqn_ref, 1.0), tables(mkn_ref, 1.0)
    for col in range(nqk // LANES):
        sl = slice(LANES * col, LANES * (col + 1))
        qc_out[0, :, sl] = roped(0, col, t_qc).astype(BF16)
        kc_out[0, :, sl] = roped(nqk, col, t_kc).astype(BF16)

    base = 2 * nqk
    for col in range(nm // LANES):
        sl = slice(LANES * col, LANES * (col + 1))
        qm = roped(base, col, t_qm)
        qmf_out[0, :, sl] = qm
        qm_out[0, :, sl] = (qm * scale).astype(BF16)
        km = roped(base + nm, col, t_km)
        km_out[0, :, sl] = km.astype(BF16)
        for blk in range(tm // MOBA_BLOCK):
            rows = slice(MOBA_BLOCK * blk, MOBA_BLOCK * (blk + 1))
            kmean_out[0, 0, blk:blk + 1, sl] = jnp.mean(km[rows], axis=0, keepdims=True)


def _odd_in(x, sh, sc, nw, win, wvt, dqn, dkn, mqn, mkn, c64, s64, tm, t):
    b, s, d = x.shape
    tok = lambda w: pl.BlockSpec((1, tm, w), lambda bi, i: (bi, i, 0))
    per_b = pl.BlockSpec((1, 1, d), lambda bi, i: (bi, 0, 0))
    full = lambda a: pl.BlockSpec(a.shape, lambda bi, i: (0,) * a.ndim, pipeline_mode=pl.Buffered(1))
    w = DIFF_HEADS * LANES
    nblk = tm // MOBA_BLOCK
    shp = lambda dt: jax.ShapeDtypeStruct((b, s, w), dt)
    vt_spec, vt_shape = _t_blocks(b, s, wvt.shape[0] // 2, tm, t)
    return pl.pallas_call(
        _odd_in_kernel,
        grid=(b, s // tm),
        in_specs=[tok(d), per_b, per_b, full(nw), full(win), full(wvt), full(dqn), full(dkn), full(mqn), full(mkn),
                  tok(LANES), tok(LANES)],
        out_specs=[tok(w)] * 5 + [vt_spec, vt_spec, pl.BlockSpec((1, 1, nblk, w), lambda bi, i: (bi, i, 0, 0))],
        out_shape=[shp(BF16), shp(BF16), shp(BF16), shp(F32), shp(BF16), vt_shape, vt_shape,
                   jax.ShapeDtypeStruct((b, s // tm, nblk, w), F32)],
        compiler_params=_cparams(("parallel", "arbitrary")),
        name="odd_in_proj",
    )(x, sh, sc, nw, win, wvt, dqn, dkn, mqn, mkn, c64, s64)


def _q_tiles_t(q, split_halves):
    qt = q.astype(F32).T
    if not split_halves:
        return [qt.astype(BF16)]
    row = lax.broadcasted_iota(jnp.int32, qt.shape, 0)
    return [jnp.where((row >= HEAD_DIM) == bool(half), qt, 0.0).astype(BF16) for half in range(2)]


def _flash_t(qts, k_ref, vt_ref, k_slices, v_rows, scr, i, t, past_mask=None):
    m_scr, acc_scr, s_scr, p_scr, a_scr = scr
    n = len(qts)
    ones = jnp.ones((ONES_ROWS, t), BF16)

    def qk(blk, s):
        b0 = pl.multiple_of(blk * t, t)
        return jnp.dot(k_ref[0, pl.ds(b0, t), k_slices[s]], qts[s], preferred_element_type=F32)

    def pv(blk, s, p):
        vt = jnp.concatenate([vt_ref[0, blk, v_rows[s], :], ones], axis=0)
        return jnp.dot(vt, p, preferred_element_type=F32)

    krow = lax.broadcasted_iota(jnp.int32, (t, t), 0)
    qcol = lax.broadcasted_iota(jnp.int32, (t, t), 1)
    causal = krow <= qcol
    last_past = jnp.maximum(i - 1, 0)
    for s in range(n):
        st = jnp.where(causal, qk(i, s), NEG_INF)
        m = jnp.max(st, axis=0, keepdims=True)
        m_scr[s] = m
        p_scr[0, s] = jnp.exp2(st - m).astype(BF16)
        p_scr[1, s] = jnp.zeros((t, t), BF16)
        a_scr[0, s] = jnp.ones_like(m)
        a_scr[1, s] = jnp.ones_like(m)
        acc_scr[s] = jnp.zeros(acc_scr.shape[1:], F32)
        s_scr[0, s] = qk(0, s)
        s_scr[1, s] = qk(jnp.minimum(1, last_past), s)

    def softmax_stage(slot, valid, rs, ws, blk):
        for s in range(n):
            st = s_scr[2 * rs + slot, s]
            cmax = jnp.max(st, axis=0, keepdims=True)
            keep = None if past_mask is None else past_mask(s, blk)
            if keep is not None:
                cmax = jnp.where(keep, cmax, NEG_INF)
            if valid is not None:
                cmax = jnp.where(valid, cmax, NEG_INF)
            m_prev = m_scr[s]
            m_new = jnp.maximum(m_prev, cmax)
            a_scr[2 * ws + slot, s] = jnp.exp2(m_prev - m_new)
            p = jnp.exp2(st - m_new).astype(BF16)
            if keep is not None:
                p = jnp.where(keep, p, jnp.zeros_like(p))
            p_scr[2 * ws + slot, s] = p
            m_scr[s] = m_new

    def body(kk, rs, ws):
        first = kk == 0
        pa_blk = jnp.where(first, i, 2 * kk - 2)
        pb_blk = jnp.where(first, i, 2 * kk - 1)
        p_prev = [[p_scr[2 * rs + sl, s] for s in range(n)] for sl in range(2)]
        a_prev = [[a_scr[2 * rs + sl, s] for s in range(n)] for sl in range(2)]
        s_next = [[qk(jnp.minimum(2 * kk + 2 + sl, last_past), s) for s in range(n)] for sl in range(2)]
        pvs = [[pv(blk, s, p_prev[sl][s]) for s in range(n)] for sl, blk in enumerate((pa_blk, pb_blk))]
        softmax_stage(0, None, rs, ws, 2 * kk)
        softmax_stage(1, 2 * kk + 1 < i, rs, ws, jnp.minimum(2 * kk + 1, last_past))
        for sl in range(2):
            for s in range(n):
                s_scr[2 * ws + sl, s] = s_next[sl][s]
        for s in range(n):
            acc_scr[s] = a_prev[1][s] * (a_prev[0][s] * acc_scr[s] + pvs[0][s]) + pvs[1][s]

    def two_bodies(kp, carry):
        body(2 * kp, 0, 1)
        body(2 * kp + 1, 1, 0)
        return carry

    nbody = (i + 1) // 2
    lax.fori_loop(0, nbody // 2, two_bodies, 0)

    @pl.when(nbody % 2 == 1)
    def _():
        body(nbody - 1, 0, 1)

    fs = nbody % 2
    kl = jnp.maximum(nbody - 1, 0)
    none = i == 0
    fa_blk = jnp.where(none, i, 2 * kl)
    fb_blk = jnp.where(none, i, jnp.minimum(2 * kl + 1, last_past))
    fb_scale = jnp.where(jnp.logical_or(none, i % 2 == 0), 1.0, 0.0)
    for s in range(n):
        acc_scr[s] = (a_scr[2 * fs + 1, s] * (a_scr[2 * fs, s] * acc_scr[s] + pv(fa_blk, s, p_scr[2 * fs, s]))
                      + fb_scale * pv(fb_blk, s, p_scr[2 * fs + 1, s]))


def _flash_scratch(n, nv, t):
    return [pltpu.VMEM((n, 1, t), F32), pltpu.VMEM((n, nv + ONES_ROWS, t), F32), pltpu.VMEM((4, n, t, t), F32),
            pltpu.VMEM((4, n, t, t), BF16), pltpu.VMEM((4, n, 1, t), F32)]


def _normalised(acc_scr, s, nv):
    return acc_scr[s, :nv, :] / acc_scr[s, nv:nv + 1, :]


def _half_mask(x, half):
    lane = _lane_iota((1, LANES))
    return jnp.where((lane >= HEAD_DIM) == bool(half), x, jnp.zeros_like(x))


def _mla_kernel(q_ref, k_ref, vt_ref, ot_ref, *scr, t, nh):
    i = pl.program_id(2)
    qs = [_q_tiles_t(q_ref[0, :, LANES * s:LANES * (s + 1)], False)[0] for s in range(nh)]
    ks = [slice(LANES * s, LANES * (s + 1)) for s in range(nh)]
    vr = [slice(HEAD_DIM * s, HEAD_DIM * (s + 1)) for s in range(nh)]
    _flash_t(qs, k_ref, vt_ref, ks, vr, scr, i, t)
    for s in range(nh):
        ot_ref[0, 0, vr[s], :] = _normalised(scr[1], s, HEAD_DIM).astype(ot_ref.dtype)


def _mla_attention(q, k, vt, nh):
    b, s, _ = q.shape
    _, nb, nv, t = vt.shape
    return pl.pallas_call(
        functools.partial(_mla_kernel, t=t, nh=nh),
        grid=(b, MLA_HEADS // nh, nb),
        in_specs=[pl.BlockSpec((1, t, nh * LANES), lambda bi, p, i: (bi, i, p)),
                  pl.BlockSpec((1, s, nh * LANES), lambda bi, p, i: (bi, 0, p)),
                  pl.BlockSpec((1, nb, nh * HEAD_DIM, t), lambda bi, p, i: (bi, 0, p, 0))],
        out_specs=pl.BlockSpec((1, 1, nh * HEAD_DIM, t), lambda bi, p, i: (bi, i, p, 0)),
        out_shape=jax.ShapeDtypeStruct((b, nb, nv, t), BF16),
        scratch_shapes=_flash_scratch(nh, HEAD_DIM, t),
        compiler_params=_cparams(("parallel", "parallel", "arbitrary")),
        name="mla_attention",
    )(q, k, vt)


def _diff_kernel(q_ref, k_ref, vt_ref, lam_ref, sub_ref, ot_ref, *scr, t, nh, lam_init):
    i = pl.program_id(2)
    qs, ks, vr = [], [], []
    for hd in range(nh):
        cols = slice(LANES * hd, LANES * (hd + 1))
        qs += _q_tiles_t(q_ref[0, :, cols], True)
        ks += [cols, cols]
        vr += [cols, cols]
    _flash_t(qs, k_ref, vt_ref, ks, vr, scr, i, t)
    lv = lam_ref[...]
    lam = (jnp.exp(jnp.sum(lv[0:1] * lv[1:2], keepdims=True))
           - jnp.exp(jnp.sum(lv[2:3] * lv[3:4], keepdims=True)) + lam_init)
    for hd in range(nh):
        o = _normalised(scr[1], 2 * hd, LANES) - lam * _normalised(scr[1], 2 * hd + 1, LANES)
        o = o * lax.rsqrt(jnp.mean(o * o, axis=0, keepdims=True) + NORM_EPS) * sub_ref[...]
        ot_ref[0, 0, LANES * hd:LANES * (hd + 1), :] = (o * (1.0 - lam_init)).astype(ot_ref.dtype)


def _diff_attention(q, k, vt, lam_rows, subln_col, lam_init, nh):
    b, s, w = q.shape
    _, nb, _, t = vt.shape
    return pl.pallas_call(
        functools.partial(_diff_kernel, t=t, nh=nh, lam_init=lam_init),
        grid=(b, DIFF_HEADS // nh, nb),
        in_specs=[pl.BlockSpec((1, t, nh * LANES), lambda bi, p, i: (bi, i, p)),
                  pl.BlockSpec((1, s, nh * LANES), lambda bi, p, i: (bi, 0, p)),
                  pl.BlockSpec((1, nb, nh * LANES, t), lambda bi, p, i: (bi, 0, p, 0)),
                  pl.BlockSpec(lam_rows.shape, lambda bi, p, i: (0, 0)),
                  pl.BlockSpec(subln_col.shape, lambda bi, p, i: (0, 0))],
        out_specs=pl.BlockSpec((1, 1, nh * LANES, t), lambda bi, p, i: (bi, i, p, 0)),
        out_shape=jax.ShapeDtypeStruct((b, nb, w, t), BF16),
        scratch_shapes=_flash_scratch(2 * nh, LANES, t),
        compiler_params=_cparams(("parallel", "parallel", "arbitrary")),
        name="diff_attention",
    )(q, k, vt, lam_rows, subln_col)


def _moba_kernel(q_ref, qf_ref, k_ref, vt_ref, kmean_ref, ot_ref, sel_scr, *scr, t, npair):
    i = pl.program_id(2)
    nb = kmean_ref.shape[1]
    brow = lax.broadcasted_iota(jnp.int32, (nb, t), 0).astype(F32)
    past = brow < i.astype(F32)
    qs, ks, vr = [], [], []
    for pr in range(npair):
        cols = slice(LANES * pr, LANES * (pr + 1))
        qf, kmean = qf_ref[0, :, cols], kmean_ref[0, :, cols]
        qs += _q_tiles_t(q_ref[0, :, cols], True)
        for half in range(2):
            s = 2 * pr + half
            gate = lax.dot_general(kmean, _half_mask(qf, half), (((1,), (1,)), ((), ())),
                                   preferred_element_type=F32, precision=lax.Precision.HIGHEST)
            gate = jnp.where(past, gate, NEG_INF)
            sel = jnp.zeros((nb, t), F32)
            for _ in range(MOBA_TOPK):
                top = jnp.max(gate, axis=0, keepdims=True)
                first = jnp.min(jnp.where(gate == top, brow, float(nb)), axis=0, keepdims=True)
                pick = brow == first
                sel = jnp.where(pick, 1.0, sel)
                gate = jnp.where(pick, 2 * NEG_INF, gate)
            sel_scr[s] = jnp.where(past, sel, 0.0)
            ks.append(cols)
            vr.append(slice(HEAD_DIM * s, HEAD_DIM * (s + 1)))
    _flash_t(qs, k_ref, vt_ref, ks, vr, scr, i, t,
             past_mask=lambda s, jj: sel_scr[s, pl.ds(jj, 1), :] > 0.5)
    for s in range(2 * npair):
        ot_ref[0, 0, vr[s], :] = _normalised(scr[1], s, HEAD_DIM).astype(ot_ref.dtype)


def _moba_attention(q, qf, k, vt, kmean, npair):
    b, s, w = q.shape
    _, nb, _, t = vt.shape
    nkm = kmean.shape[1]
    lanes = npair * LANES
    return pl.pallas_call(
        functools.partial(_moba_kernel, t=t, npair=npair),
        grid=(b, w // lanes, nb),
        in_specs=[pl.BlockSpec((1, t, lanes), lambda bi, p, i: (bi, i, p)),
                  pl.BlockSpec((1, t, lanes), lambda bi, p, i: (bi, i, p)),
                  pl.BlockSpec((1, s, lanes), lambda bi, p, i: (bi, 0, p)),
                  pl.BlockSpec((1, nb, lanes, t), lambda bi, p, i: (bi, 0, p, 0)),
                  pl.BlockSpec((1, nkm, lanes), lambda bi, p, i: (bi, 0, p))],
        out_specs=pl.BlockSpec((1, 1, lanes, t), lambda bi, p, i: (bi, i, p, 0)),
        out_shape=jax.ShapeDtypeStruct((b, nb, w, t), BF16),
        scratch_shapes=[pltpu.VMEM((2 * npair, nkm, t), F32)] + _flash_scratch(2 * npair, HEAD_DIM, t),
        compiler_params=_cparams(("parallel", "parallel", "arbitrary")),
        name="moba_attention",
    )(q, qf, k, vt, kmean)


def _sw_kernel(q_ref, kc_ref, vc_ref, kp_ref, vp_ref, o_ref, l_ref, *, nbk):
    i = pl.program_id(1)
    blk = SW_BLOCK
    qi = lax.broadcasted_iota(jnp.int32, (blk, 2 * blk), 0)
    kj = lax.broadcasted_iota(jnp.int32, (blk, 2 * blk), 1)
    band = (kj >= qi) & (kj <= qi + blk)
    lane = _lane_iota((1, LANES))
    heads = [(pair, half) for pair in range(DIL_W // LANES) for half in range(2)]

    def scores(n):
        rows = slice(blk * n, blk * (n + 1))
        q = q_ref[0, rows, :]
        if n == 0:
            kprev, valid = kp_ref[0], band & ((kj >= blk) | (i > 0))
        else:
            kprev, valid = kc_ref[0, blk * (n - 1):blk * n, :], band
        kk = jnp.concatenate([kprev, kc_ref[0, rows, :]], axis=0)
        out = []
        for pair, half in heads:
            sl = slice(LANES * pair, LANES * (pair + 1))
            out.append(jnp.where(valid, _nt_dot(_half_mask(q[:, sl], half), kk[:, sl]), NEG_INF))
        return out

    nxt = scores(0)
    for n in range(nbk):
        cur, rows = nxt, slice(blk * n, blk * (n + 1))
        if n + 1 < nbk:
            nxt = scores(n + 1)
        vprev = vp_ref[0] if n == 0 else vc_ref[0, blk * (n - 1):blk * n, :]
        vv = jnp.concatenate([vprev, vc_ref[0, rows, :]], axis=0)
        outs, lses = [], []
        for (pair, half), sc in zip(heads, cur):
            m = jnp.max(sc, axis=-1, keepdims=True)
            p = jnp.exp2(sc - m)
            l = jnp.sum(p, axis=-1, keepdims=True)
            sl = slice(LANES * pair, LANES * (pair + 1))
            outs.append(jnp.dot(p.astype(BF16), vv[:, sl], preferred_element_type=F32) / l)
            lses.append(m + jnp.log2(l))
        for pair in range(DIL_W // LANES):
            sl = slice(LANES * pair, LANES * (pair + 1))
            o_ref[0, rows, sl] = jnp.where(lane < HEAD_DIM, outs[2 * pair], outs[2 * pair + 1])
            l_ref[0, rows, sl] = jnp.where(lane < HEAD_DIM, lses[2 * pair], lses[2 * pair + 1])


def _sliding_window(dil, nbk, name):
    lead, (m, w3) = dil.shape[:-2], dil.shape[-2:]
    seqs = dil.reshape((-1, m, w3))
    nbk = min(nbk, m // SW_BLOCK)
    rows = SW_BLOCK * nbk
    cur = lambda off: pl.BlockSpec((1, rows, DIL_W), lambda n, i: (n, i, off))
    prev = lambda off: pl.BlockSpec((1, SW_BLOCK, DIL_W), lambda n, i: (n, jnp.maximum(i * nbk - 1, 0), off))
    out_spec = pl.BlockSpec((1, rows, DIL_W), lambda n, i: (n, i, 0))
    o, l = pl.pallas_call(
        functools.partial(_sw_kernel, nbk=nbk),
        grid=(seqs.shape[0], m // rows),
        in_specs=[cur(0), cur(1), cur(2), prev(1), prev(2)],
        out_specs=[out_spec, out_spec],
        out_shape=[jax.ShapeDtypeStruct((seqs.shape[0], m, DIL_W), F32)] * 2,
        compiler_params=_cparams(("parallel", "arbitrary")),
        name=name,
    )(seqs, seqs, seqs, seqs, seqs)
    return o.reshape(lead + (m, DIL_W)), l.reshape(lead + (m, DIL_W))


def _mlp_tail(y, mlp_refs):
    sh_ref, sc_ref, g_ref, nw_ref, w1_ref, w2_ref = mlp_refs
    h = (_rms(y, nw_ref[...]) * (1.0 + sc_ref[0]) + sh_ref[0]).astype(BF16)
    acc = jnp.zeros(y.shape, F32)
    for c in range(w1_ref.shape[1] // MLP_CHUNK):
        cols = slice(MLP_CHUNK * c, MLP_CHUNK * (c + 1))
        a = jnp.maximum(jnp.dot(h, w1_ref[:, cols], preferred_element_type=F32), 0.0)
        acc = acc + jnp.dot((a * a).astype(BF16), w2_ref[cols, :], preferred_element_type=F32)
    return y + g_ref[0] * acc


def _proj_t(ot_ref, w_rows):
    return jnp.concatenate([_tn_dot(ot_ref[0, j], w_rows) for j in range(ot_ref.shape[1])], axis=0)


def _mlp_specs(mlp_args, d):
    per_b = pl.BlockSpec((1, 1, d), lambda bi, i: (bi, 0, 0))
    const = lambda a: pl.BlockSpec(a.shape, lambda bi, i: (0, 0), pipeline_mode=pl.Buffered(1))
    sh, sc, g2, nw, w1, w2 = mlp_args
    return [per_b, per_b, per_b, pl.BlockSpec(nw.shape, lambda bi, i: (0, 0)), const(w1), const(w2)]


def _even_out_kernel(x_ref, g_ref, oat_ref, o0_ref, o1_ref, o2_ref, l0_ref, l1_ref, l2_ref, w_ref,
                     *rest):
    *mlp_refs, y_ref, tscr = rest
    tm = x_ref.shape[1]
    ncol = DIL_W // LANES

    def token_major(ref, slot):
        r = ref.shape[1]
        for c in range(r):
            for j in range(ncol):
                tscr[slot * ncol + j, pl.ds(c, tm // r, stride=r), :] = ref[0, c, :, LANES * j:LANES * (j + 1)]
        return jnp.concatenate([tscr[slot * ncol + j] for j in range(ncol)], axis=1)

    o0, l0 = o0_ref[0], l0_ref[0]
    o1, l1 = token_major(o1_ref, 0), token_major(l1_ref, 1)
    o2, l2 = token_major(o2_ref, 2), token_major(l2_ref, 3)
    top = jnp.maximum(jnp.maximum(l0, l1), l2)
    w0, w1, w2 = jnp.exp2(l0 - top), jnp.exp2(l1 - top), jnp.exp2(l2 - top)
    ob = (w0 * o0 + w1 * o1 + w2 * o2) / (w0 + w1 + w2)
    na = oat_ref.shape[2]
    y = _proj_t(oat_ref, w_ref[:na, :]) + jnp.dot(ob.astype(BF16), w_ref[na:, :], preferred_element_type=F32)
    y_ref[0] = _mlp_tail(x_ref[0] + g_ref[0] * y, mlp_refs)


def _t_spec(a, tm):
    return pl.BlockSpec((1, tm // a.shape[-1]) + a.shape[2:], lambda bi, i: (bi, i, 0, 0))


def _even_out(x, g1, oat, o_dil, l_dil, w_out, mlp_args, tm):
    b, s, d = x.shape
    tok = lambda w: pl.BlockSpec((1, tm, w), lambda bi, i: (bi, i, 0))

    def dil_spec(a):
        if a.ndim == 3:
            return tok(DIL_W)
        r = a.shape[1]
        return pl.BlockSpec((1, r, tm // r, DIL_W), lambda bi, i: (bi, 0, i, 0))

    return pl.pallas_call(
        _even_out_kernel,
        grid=(b, s // tm),
        in_specs=[tok(d), pl.BlockSpec((1, 1, d), lambda bi, i: (bi, 0, 0)), _t_spec(oat, tm)]
                 + [dil_spec(a) for a in (*o_dil, *l_dil)]
                 + [pl.BlockSpec(w_out.shape, lambda bi, i: (0, 0), pipeline_mode=pl.Buffered(1))]
                 + _mlp_specs(mlp_args, d),
        out_specs=tok(d),
        out_shape=jax.ShapeDtypeStruct((b, s, d), F32),
        scratch_shapes=[pltpu.VMEM((4 * DIL_W // LANES, tm, LANES), F32)],
        compiler_params=_cparams(("parallel", "arbitrary")),
        name="even_out_mlp",
    )(x, g1, oat, *o_dil, *l_dil, w_out, *mlp_args)


def _odd_out_kernel(x_ref, g_ref, oct_ref, odt_ref, w_ref, *rest):
    *mlp_refs, y_ref = rest
    nc = oct_ref.shape[2]
    y = _proj_t(oct_ref, w_ref[:nc, :]) + _proj_t(odt_ref, w_ref[nc:, :])
    y_ref[0] = _mlp_tail(x_ref[0] + g_ref[0] * y, mlp_refs)


def _odd_out(x, g1, oct, odt, w_out, mlp_args, tm):
    b, s, d = x.shape
    tok = lambda w: pl.BlockSpec((1, tm, w), lambda bi, i: (bi, i, 0))
    return pl.pallas_call(
        _odd_out_kernel,
        grid=(b, s // tm),
        in_specs=[tok(d), pl.BlockSpec((1, 1, d), lambda bi, i: (bi, 0, 0)), _t_spec(oct, tm), _t_spec(odt, tm),
                  pl.BlockSpec(w_out.shape, lambda bi, i: (0, 0), pipeline_mode=pl.Buffered(1))]
                 + _mlp_specs(mlp_args, d),
        out_specs=tok(d),
        out_shape=jax.ShapeDtypeStruct((b, s, d), F32),
        compiler_params=_cparams(("parallel", "arbitrary")),
        name="odd_out_mlp",
    )(x, g1, oct, odt, w_out, *mlp_args)


def _rope_tables(positions):
    pos = positions.astype(F32)[:, :, None]

    def cs(half):
        inv = ROPE_THETA ** (-jnp.arange(half, dtype=F32) / half)
        ang = pos * inv
        return jnp.cos(ang), jnp.sin(ang)

    c, s = cs(HEAD_DIM // 2)
    c64 = jnp.concatenate([c, c, c, c], axis=-1)
    s64 = jnp.concatenate([-s, s, -s, s], axis=-1)
    c, s = cs(MLA_ROPE // 2)
    ones = jnp.ones(pos.shape[:2] + (MLA_NOPE,), F32)
    pad = LANES - MLA_QK
    cm = jnp.concatenate([ones, c, c, jnp.ones(pos.shape[:2] + (pad,), F32)], axis=-1)
    sm = jnp.concatenate([0 * ones, -s, s, jnp.zeros(pos.shape[:2] + (pad,), F32)], axis=-1)
    return cm, sm, c64, s64


def _partner(n, width):
    idx = np.arange(n)
    return np.where(idx % width < width // 2, idx + width // 2, idx - width // 2)


def _even_weights(w_in, w_uq, w_ukv, qn, kn):
    o2 = MLA_Q_RANK + MLA_KV_RANK
    o3 = o2 + MLA_ROPE
    d = w_in.shape[0]
    p_rope = _partner(MLA_ROPE, MLA_ROPE)
    zeros = lambda n: jnp.zeros((d, n), w_in.dtype)
    kr = w_in[:, o2:o3]
    tail = LANES - MLA_QK
    krp = jnp.concatenate([zeros(MLA_NOPE), kr, zeros(tail)], axis=1)
    krp_rot = jnp.concatenate([zeros(MLA_NOPE), kr[:, p_rope], zeros(tail)], axis=1)
    dil = w_in[:, o3:].reshape(d, 3, len(DIL_CONFIGS), DIL_W)
    dil_main = dil.transpose(0, 2, 1, 3).reshape(d, -1)
    dil_rot = dil[:, :2][..., _partner(DIL_W, HEAD_DIM)].transpose(0, 2, 1, 3).reshape(d, -1)
    win = jnp.concatenate([w_in[:, :o2], krp, krp_rot, dil_main, dil_rot], axis=1)

    padq = ((0, 0), (0, 0), (0, tail))
    wuq_rot = jnp.concatenate([w_uq[:, :, :MLA_NOPE], w_uq[:, :, MLA_NOPE:][:, :, p_rope]], axis=-1)
    wuq = jnp.concatenate([jnp.pad(w_uq, padq).reshape(MLA_Q_RANK, -1),
                           jnp.pad(wuq_rot, padq).reshape(MLA_Q_RANK, -1)], axis=1)
    wuk = jnp.pad(w_ukv[:, :, :MLA_NOPE], ((0, 0), (0, 0), (0, LANES - MLA_NOPE))).reshape(MLA_KV_RANK, -1)
    wvt = w_ukv[:, :, MLA_NOPE:].reshape(MLA_KV_RANK, -1).T

    def gains(g):
        rot = jnp.concatenate([g[:MLA_NOPE], g[MLA_NOPE:][p_rope]])
        return jnp.stack([jnp.pad(g, (0, tail)), jnp.pad(rot, (0, tail))]).astype(F32)

    return win.astype(BF16), wuq.astype(BF16), wuk.astype(BF16), wvt.astype(BF16), gains(qn), gains(kn)


def _odd_weights(w_in):
    nqk = DIFF_HEADS * LANES
    nm = MOBA_HEADS * HEAD_DIM
    main = jnp.concatenate([w_in[:, :2 * nqk], w_in[:, 3 * nqk:3 * nqk + 2 * nm]], axis=1)
    win = jnp.concatenate([main, main[:, _partner(main.shape[1], HEAD_DIM)]], axis=1)
    wvt = jnp.concatenate([w_in[:, 2 * nqk:3 * nqk], w_in[:, 3 * nqk + 2 * nm:]], axis=1).T
    return win.astype(BF16), wvt.astype(BF16)


def _gains64(g):
    rot = g[_partner(HEAD_DIM, HEAD_DIM)]
    return jnp.stack([jnp.concatenate([g, g]), jnp.concatenate([rot, rot])]).astype(F32)


def kernel(x, c, positions, ada_w, ada_b, norm_mix, norm_mlp, mlp_w1, mlp_w2, even_w_in, even_w_out, mla_q_lat_norm, mla_kv_lat_norm, mla_w_uq, mla_w_ukv, mla_q_norm, mla_k_norm, dil_q_norm, dil_k_norm, odd_w_in, odd_w_out, diff_q_norm, diff_k_norm, diff_lambda, diff_subln, moba_q_norm, moba_k_norm):
    b, s, d = x.shape
    depth = ada_w.shape[0]
    t_attn, tm_in, tm_out = ATTN_BLOCK, TOKEN_TILE, TOKEN_TILE

    mod = _adaln(c, ada_w, ada_b)
    cm, sm, c64, s64 = _rope_tables(positions)

    for layer in range(depth):
        sh1, sc1, g1, sh2, sc2, g2 = [mod[layer, :, d * t:d * (t + 1)].reshape(b, 1, d) for t in range(6)]
        mlp_args = (sh2, sc2, g2, _row(norm_mlp[layer]), mlp_w1[layer].astype(BF16), mlp_w2[layer].astype(BF16))
        i = layer // 2
        if layer % 2 == 0:
            win, wuq, wuk, wvt, qn, kn = _even_weights(even_w_in[i], mla_w_uq[i], mla_w_ukv[i],
                                                       mla_q_norm[i], mla_k_norm[i])
            q, k, vt, *dils = _even_in(x, sh1, sc1, _row(norm_mix[layer]), win, wuq, wuk, wvt,
                                       _row(mla_q_lat_norm[i]), _row(mla_kv_lat_norm[i]), qn, kn,
                                       _gains64(dil_q_norm[i]), _gains64(dil_k_norm[i]), cm, sm, c64, s64, tm_in, t_attn)
            o_at = _mla_attention(q, k, vt, MLA_HEADS_PER_STEP)
            o_dil, l_dil = zip(*[_sliding_window(dg, SW_BLOCKS_PER_STEP, f"sliding_window_g{g}")
                                 for g, dg in enumerate(dils)])
            x = _even_out(x, g1, o_at, o_dil, l_dil, even_w_out[i].astype(BF16), mlp_args, tm_out)
        else:
            lam_init = 0.8 - 0.6 * math.exp(-0.3 * layer)
            win, wvt = _odd_weights(odd_w_in[i])
            qc, kc, qm, qmf, km, vct, vmt, kmean = _odd_in(
                x, sh1, sc1, _row(norm_mix[layer]), win, wvt,
                _gains64(diff_q_norm[i]), _gains64(diff_k_norm[i]), _gains64(moba_q_norm[i]),
                _gains64(moba_k_norm[i]),
                c64, s64, tm_in, t_attn)
            o_ct = _diff_attention(qc, kc, vct, diff_lambda[i].astype(F32),
                                   diff_subln[i].reshape(-1, 1).astype(F32), lam_init, DIFF_HEADS_PER_STEP)
            o_dt = _moba_attention(qm, qmf, km, vmt, kmean.reshape(b, s // MOBA_BLOCK, kmean.shape[-1]),
                                   MOBA_PAIRS_PER_STEP)
            x = _odd_out(x, g1, o_ct, o_dt, odd_w_out[i].astype(BF16), mlp_args, tm_out)
    return x
```

```python
import functools
import math

import jax
import jax.numpy as jnp
import numpy as np
from jax import lax
from jax.experimental import pallas as pl
from jax.experimental.pallas import tpu as pltpu

F32 = jnp.float32
BF16 = jnp.bfloat16

LANES = 128
HEAD_DIM = 64
ROPE_THETA = 10000.0
NORM_EPS = 1e-6
NEG_INF = -1e30
LOG2E = math.log2(math.e)

MLA_HEADS = 8
MLA_Q_RANK = 384
MLA_KV_RANK = 256
MLA_NOPE = 64
MLA_ROPE = 32
MLA_QK = MLA_NOPE + MLA_ROPE
DIL_CONFIGS = ((128, 1), (512, 4), (2048, 16))
DIL_HEADS = 4
DIL_W = DIL_HEADS * HEAD_DIM
DIFF_HEADS = 4
MOBA_HEADS = 8
MOBA_BLOCK = 256
MOBA_TOPK = 3
SW_BLOCK = 128
ONES_ROWS = 16

VMEM_LIMIT = 56 * 1024 * 1024

ATTN_BLOCK = MOBA_BLOCK
TOKEN_TILE = 2 * ATTN_BLOCK
MLP_CHUNK = 1024
ADALN_TILE = 1024
SW_BLOCKS_PER_STEP = 4
MLA_HEADS_PER_STEP = 4
DIFF_HEADS_PER_STEP = 2
MOBA_PAIRS_PER_STEP = 2


def _cparams(sem):
    return pltpu.CompilerParams(dimension_semantics=sem, vmem_limit_bytes=VMEM_LIMIT)


def _nt_dot(a, b):
    return lax.dot_general(a, b, (((1,), (1,)), ((), ())), preferred_element_type=F32)


def _tn_dot(a, b):
    return lax.dot_general(a, b, (((0,), (0,)), ((), ())), preferred_element_type=F32)


def _rms(x, w):
    return x * lax.rsqrt(jnp.mean(x * x, axis=-1, keepdims=True) + NORM_EPS) * w


def _lane_iota(shape):
    return lax.broadcasted_iota(jnp.int32, shape, len(shape) - 1)


def _adaln_kernel(c_ref, w_ref, b_ref, o_ref):
    c = c_ref[...]
    cond = c * (1.0 / (1.0 + jnp.exp(-c)))
    o_ref[0] = jnp.dot(cond, w_ref[0], preferred_element_type=F32,
                       precision=lax.Precision.HIGHEST) + b_ref[0]


def _adaln(c, ada_w, ada_b):
    depth, d, n = ada_w.shape
    b = c.shape[0]
    tn = ADALN_TILE
    return pl.pallas_call(
        _adaln_kernel,
        grid=(depth, n // tn),
        in_specs=[pl.BlockSpec((b, d), lambda l, j: (0, 0)),
                  pl.BlockSpec((1, d, tn), lambda l, j: (l, 0, j)),
                  pl.BlockSpec((1, 1, tn), lambda l, j: (l, 0, j))],
        out_specs=pl.BlockSpec((1, b, tn), lambda l, j: (l, 0, j)),
        out_shape=jax.ShapeDtypeStruct((depth, b, n), F32),
        compiler_params=_cparams(("parallel", "parallel")),
        name="adaln",
    )(c, ada_w, ada_b.reshape(depth, 1, n))


def _inv_rms64(x):
    lane = _lane_iota((1, LANES))
    lo = lane < HEAD_DIM
    sq = x * x
    s_lo = jnp.sum(jnp.where(lo, sq, 0.0), axis=-1, keepdims=True)
    s_hi = jnp.sum(jnp.where(lo, 0.0, sq), axis=-1, keepdims=True)
    return lax.rsqrt(jnp.where(lo, s_lo, s_hi) * (1.0 / HEAD_DIM) + NORM_EPS)


def _store_t_blocks(out_ref, xt):
    t = out_ref.shape[-1]
    for j in range(out_ref.shape[1]):
        out_ref[0, j] = xt[:, t * j:t * (j + 1)]


def _t_blocks(b, s, rows, tm, t):
    return (pl.BlockSpec((1, tm // t, rows, t), lambda bi, i: (bi, i, 0, 0)),
            jax.ShapeDtypeStruct((b, s // t, rows, t), BF16))


def _even_in_kernel(x_ref, sh_ref, sc_ref, nw_ref, win_ref, wuq_ref, wuk_ref, wvt_ref,
                    qlat_ref, kvlat_ref, qn_ref, kn_ref, dqn_ref, dkn_ref,
                    cm_ref, sm_ref, c64_ref, s64_ref,
                    qt_out, k_out, vt_out, d0_out, d1_out, d2_out, dscr):
    x = x_ref[0]
    tm = x.shape[0]
    h = _rms(x, nw_ref[...]) * (1.0 + sc_ref[0]) + sh_ref[0]
    u = jnp.dot(h.astype(BF16), win_ref[...], preferred_element_type=F32)

    o1 = MLA_Q_RANK
    o2 = o1 + MLA_KV_RANK
    o_kr, o_krr, o_dil = o2, o2 + LANES, o2 + 2 * LANES
    n_dil = 3 * DIL_W * len(DIL_CONFIGS)
    o_rot = o_dil + n_dil
    nq = MLA_HEADS * LANES
    cqn = _rms(u[:, :o1], qlat_ref[...]).astype(BF16)
    qp = jnp.dot(cqn, wuq_ref[...], preferred_element_type=F32)
    ckvn = _rms(u[:, o1:o2], kvlat_ref[...]).astype(BF16)
    kvp = jnp.dot(ckvn, wuk_ref[...], preferred_element_type=F32)
    _store_t_blocks(vt_out, _nt_dot(wvt_ref[...], ckvn).astype(BF16))
    kr, kr_rot = u[:, o_kr:o_kr + LANES], u[:, o_krr:o_krr + LANES]

    cm, sm = cm_ref[0], sm_ref[0]
    q_scale = MLA_QK ** -0.5 * LOG2E
    qgc, qgs = qn_ref[0:1] * cm * q_scale, qn_ref[1:2] * sm * q_scale
    kgc, kgs = kn_ref[0:1] * cm, kn_ref[1:2] * sm
    kr_term = kr_rot * kgs
    for hd in range(MLA_HEADS):
        sl = slice(LANES * hd, LANES * (hd + 1))
        qh = qp[:, sl]
        inv = lax.rsqrt(jnp.sum(qh * qh, -1, keepdims=True) * (1.0 / MLA_QK) + NORM_EPS)
        qh = inv * (qh * qgc + qp[:, nq + LANES * hd:nq + LANES * (hd + 1)] * qgs)
        tq = qt_out.shape[-1]
        for j in range(qt_out.shape[1]):
            qt_out[0, j, sl, :] = qh[tq * j:tq * (j + 1)].T.astype(BF16)
        kh = kvp[:, sl] + kr
        inv = lax.rsqrt(jnp.sum(kh * kh, -1, keepdims=True) * (1.0 / MLA_QK) + NORM_EPS)
        k_out[0, :, sl] = (inv * (kh * kgc + kr_term)).astype(BF16)

    c64, s64 = c64_ref[0], s64_ref[0]
    d_scale = HEAD_DIM ** -0.5 * LOG2E
    dqc, dqs = dqn_ref[0:1] * c64 * d_scale, dqn_ref[1:2] * s64 * d_scale
    dkc, dks = dkn_ref[0:1] * c64, dkn_ref[1:2] * s64
    ncol = DIL_W // LANES
    for g, d_out in enumerate((d0_out, d1_out, d2_out)):
        _, r = DIL_CONFIGS[g]
        base = o_dil + 3 * DIL_W * g
        rbase = o_rot + 2 * DIL_W * g
        for j in range(3 * ncol):
            xc = u[:, base + LANES * j:base + LANES * (j + 1)]
            if j < 2 * ncol:
                xr = u[:, rbase + LANES * j:rbase + LANES * (j + 1)]
                gc, gs = (dqc, dqs) if j < ncol else (dkc, dks)
                xc = _inv_rms64(xc) * (xc * gc + xr * gs)
            if r == 1:
                d_out[0, :, LANES * j:LANES * (j + 1)] = xc.astype(BF16)
            else:
                dscr[j] = xc
                for c in range(r):
                    d_out[0, c, :, LANES * j:LANES * (j + 1)] = dscr[j, pl.ds(c, tm // r, stride=r), :].astype(BF16)


def _row(v):
    return v.reshape(1, -1).astype(F32)


def _even_in(x, sh, sc, nw, win, wuq, wuk, wvt, qlat, kvlat, qn, kn, dqn, dkn, cm, sm, c64, s64, tm, t):
    b, s, d = x.shape
    tok = lambda w: pl.BlockSpec((1, tm, w), lambda bi, i: (bi, i, 0))
    per_b = pl.BlockSpec((1, 1, d), lambda bi, i: (bi, 0, 0))
    full = lambda a: pl.BlockSpec(a.shape, lambda bi, i: (0,) * a.ndim, pipeline_mode=pl.Buffered(1))
    vt_spec, vt_shape = _t_blocks(b, s, wvt.shape[0], tm, t)
    qt_spec, qt_shape = _t_blocks(b, s, MLA_HEADS * LANES, tm, t)
    dil_specs, dil_shapes = [], []
    for _, r in DIL_CONFIGS:
        if r == 1:
            dil_specs.append(tok(3 * DIL_W))
            dil_shapes.append(jax.ShapeDtypeStruct((b, s, 3 * DIL_W), BF16))
        else:
            dil_specs.append(pl.BlockSpec((1, r, tm // r, 3 * DIL_W), lambda bi, i: (bi, 0, i, 0)))
            dil_shapes.append(jax.ShapeDtypeStruct((b, r, s // r, 3 * DIL_W), BF16))
    return pl.pallas_call(
        _even_in_kernel,
        grid=(b, s // tm),
        in_specs=[tok(d), per_b, per_b, full(nw), full(win), full(wuq), full(wuk), full(wvt),
                  full(qlat), full(kvlat), full(qn), full(kn), full(dqn), full(dkn),
                  tok(LANES), tok(LANES), tok(LANES), tok(LANES)],
        out_specs=[qt_spec, tok(MLA_HEADS * LANES), vt_spec] + dil_specs,
        out_shape=[qt_shape, jax.ShapeDtypeStruct((b, s, MLA_HEADS * LANES), BF16), vt_shape] + dil_shapes,
        scratch_shapes=[pltpu.VMEM((3 * DIL_W // LANES, tm, LANES), F32)],
        compiler_params=_cparams(("parallel", "arbitrary")),
        name="even_in_proj",
    )(x, sh, sc, nw, win, wuq, wuk, wvt, qlat, kvlat, qn, kn, dqn, dkn, cm, sm, c64, s64)


def _odd_in_kernel(x_ref, sh_ref, sc_ref, nw_ref, win_ref, wvt_ref, dqn_ref, dkn_ref, mqn_ref, mkn_ref,
                   c64_ref, s64_ref,
                   qc_out, kc_out, qm_out, qmf_out, km_out, vct_out, vmt_out, kmean_out):
    x = x_ref[0]
    tm = x.shape[0]
    h = (_rms(x, nw_ref[...]) * (1.0 + sc_ref[0]) + sh_ref[0]).astype(BF16)
    u = jnp.dot(h, win_ref[...], preferred_element_type=F32)
    vt = _nt_dot(wvt_ref[...], h).astype(BF16)
    nvc = vct_out.shape[2]
    _store_t_blocks(vct_out, vt[:nvc])
    _store_t_blocks(vmt_out, vt[nvc:])
    c64, s64 = c64_ref[0], s64_ref[0]
    scale = HEAD_DIM ** -0.5 * LOG2E
    nqk = DIFF_HEADS * LANES
    nm = MOBA_HEADS * HEAD_DIM
    rot = 2 * nqk + 2 * nm

    def tables(gain_ref, mult):
        return gain_ref[0:1] * c64 * mult, gain_ref[1:2] * s64 * mult

    def roped(off, col, tab):
        a = off + LANES * col
        xc, xr = u[:, a:a + LANES], u[:, rot + a:rot + a + LANES]
        return _inv_rms64(xc) * (xc * tab[0] + xr * tab[1])

    t_qc, t_kc, t_qm, t_km = tables(dqn_ref, scale), tables(dkn_ref, 1.0), tables(mqn_ref, 1.0), tables(mkn_ref, 1.0)
    for col in range(nqk // LANES):
        sl = slice(LANES * col, LANES * (col + 1))
        qc_out[0, :, sl] = roped(0, col, t_qc).astype(BF16)
        kc_out[0, :, sl] = roped(nqk, col, t_kc).astype(BF16)

    base = 2 * nqk
    for col in range(nm // LANES):
        sl = slice(LANES * col, LANES * (col + 1))
        qm = roped(base, col, t_qm)
        qmf_out[0, :, sl] = qm
        qm_out[0, :, sl] = (qm * scale).astype(BF16)
        km = roped(base + nm, col, t_km)
        km_out[0, :, sl] = km.astype(BF16)
        for blk in range(tm // MOBA_BLOCK):
            rows = slice(MOBA_BLOCK * blk, MOBA_BLOCK * (blk + 1))
            kmean_out[0, 0, blk:blk + 1, sl] = jnp.mean(km[rows], axis=0, keepdims=True)


def _odd_in(x, sh, sc, nw, win, wvt, dqn, dkn, mqn, mkn, c64, s64, tm, t):
    b, s, d = x.shape
    tok = lambda w: pl.BlockSpec((1, tm, w), lambda bi, i: (bi, i, 0))
    per_b = pl.BlockSpec((1, 1, d), lambda bi, i: (bi, 0, 0))
    full = lambda a: pl.BlockSpec(a.shape, lambda bi, i: (0,) * a.ndim, pipeline_mode=pl.Buffered(1))
    w = DIFF_HEADS * LANES
    nblk = tm // MOBA_BLOCK
    shp = lambda dt: jax.ShapeDtypeStruct((b, s, w), dt)
    vt_spec, vt_shape = _t_blocks(b, s, wvt.shape[0] // 2, tm, t)
    return pl.pallas_call(
        _odd_in_kernel,
        grid=(b, s // tm),
        in_specs=[tok(d), per_b, per_b, full(nw), full(win), full(wvt), full(dqn), full(dkn), full(mqn), full(mkn),
                  tok(LANES), tok(LANES)],
        out_specs=[tok(w)] * 5 + [vt_spec, vt_spec, pl.BlockSpec((1, 1, nblk, w), lambda bi, i: (bi, i, 0, 0))],
        out_shape=[shp(BF16), shp(BF16), shp(BF16), shp(F32), shp(BF16), vt_shape, vt_shape,
                   jax.ShapeDtypeStruct((b, s // tm, nblk, w), F32)],
        compiler_params=_cparams(("parallel", "arbitrary")),
        name="odd_in_proj",
    )(x, sh, sc, nw, win, wvt, dqn, dkn, mqn, mkn, c64, s64)


def _q_tiles_t(q, split_halves):
    qt = q.astype(F32).T
    if not split_halves:
        return [qt.astype(BF16)]
    row = lax.broadcasted_iota(jnp.int32, qt.shape, 0)
    return [jnp.where((row >= HEAD_DIM) == bool(half), qt, 0.0).astype(BF16) for half in range(2)]


def _flash_t(qts, k_ref, vt_ref, k_slices, v_rows, scr, i, t, past_mask=None):
    m_scr, acc_scr, s_scr, p_scr, a_scr = scr
    n = len(qts)
    ones = jnp.ones((ONES_ROWS, t), BF16)

    def qk(blk, s):
        b0 = pl.multiple_of(blk * t, t)
        return jnp.dot(k_ref[0, pl.ds(b0, t), k_slices[s]], qts[s], preferred_element_type=F32)

    def pv(blk, s, p):
        vt = jnp.concatenate([vt_ref[0, blk, v_rows[s], :], ones], axis=0)
        return jnp.dot(vt, p, preferred_element_type=F32)

    krow = lax.broadcasted_iota(jnp.int32, (t, t), 0)
    qcol = lax.broadcasted_iota(jnp.int32, (t, t), 1)
    causal = krow <= qcol
    last_past = jnp.maximum(i - 1, 0)
    for s in range(n):
        st = jnp.where(causal, qk(i, s), NEG_INF)
        m = jnp.max(st, axis=0, keepdims=True)
        m_scr[s] = m
        p_scr[0, s] = jnp.exp2(st - m).astype(BF16)
        p_scr[1, s] = jnp.zeros((t, t), BF16)
        a_scr[0, s] = jnp.ones_like(m)
        a_scr[1, s] = jnp.ones_like(m)
        acc_scr[s] = jnp.zeros(acc_scr.shape[1:], F32)
        s_scr[0, s] = qk(0, s)
        s_scr[1, s] = qk(jnp.minimum(1, last_past), s)

    def softmax_stage(slot, valid, rs, ws, blk):
        for s in range(n):
            st = s_scr[2 * rs + slot, s]
            cmax = jnp.max(st, axis=0, keepdims=True)
            keep = None if past_mask is None else past_mask(s, blk)
            if keep is not None:
                cmax = jnp.where(keep, cmax, NEG_INF)
            if valid is not None:
                cmax = jnp.where(valid, cmax, NEG_INF)
            m_prev = m_scr[s]
            m_new = jnp.maximum(m_prev, cmax)
            a_scr[2 * ws + slot, s] = jnp.exp2(m_prev - m_new)
            p = jnp.exp2(st - m_new).astype(BF16)
            if keep is not None:
                p = jnp.where(keep, p, jnp.zeros_like(p))
            p_scr[2 * ws + slot, s] = p
            m_scr[s] = m_new

    def body(kk, rs, ws):
        first = kk == 0
        pa_blk = jnp.where(first, i, 2 * kk - 2)
        pb_blk = jnp.where(first, i, 2 * kk - 1)
        p_prev = [[p_scr[2 * rs + sl, s] for s in range(n)] for sl in range(2)]
        a_prev = [[a_scr[2 * rs + sl, s] for s in range(n)] for sl in range(2)]
        s_next = [[qk(jnp.minimum(2 * kk + 2 + sl, last_past), s) for s in range(n)] for sl in range(2)]
        pvs = [[pv(blk, s, p_prev[sl][s]) for s in range(n)] for sl, blk in enumerate((pa_blk, pb_blk))]
        softmax_stage(0, None, rs, ws, 2 * kk)
        softmax_stage(1, 2 * kk + 1 < i, rs, ws, jnp.minimum(2 * kk + 1, last_past))
        for sl in range(2):
            for s in range(n):
                s_scr[2 * ws + sl, s] = s_next[sl][s]
        for s in range(n):
            acc_scr[s] = a_prev[1][s] * (a_prev[0][s] * acc_scr[s] + pvs[0][s]) + pvs[1][s]

    def two_bodies(kp, carry):
        body(2 * kp, 0, 1)
        body(2 * kp + 1, 1, 0)
        return carry

    nbody = (i + 1) // 2
    lax.fori_loop(0, nbody // 2, two_bodies, 0)

    @pl.when(nbody % 2 == 1)
    def _():
        body(nbody - 1, 0, 1)

    fs = nbody % 2
    kl = jnp.maximum(nbody - 1, 0)
    none = i == 0
    fa_blk = jnp.where(none, i, 2 * kl)
    fb_blk = jnp.where(none, i, jnp.minimum(2 * kl + 1, last_past))
    fb_scale = jnp.where(jnp.logical_or(none, i % 2 == 0), 1.0, 0.0)
    for s in range(n):
        acc_scr[s] = (a_scr[2 * fs + 1, s] * (a_scr[2 * fs, s] * acc_scr[s] + pv(fa_blk, s, p_scr[2 * fs, s]))
                      + fb_scale * pv(fb_blk, s, p_scr[2 * fs + 1, s]))


def _flash_scratch(n, nv, t):
    return [pltpu.VMEM((n, 1, t), F32), pltpu.VMEM((n, nv + ONES_ROWS, t), F32), pltpu.VMEM((4, n, t, t), F32),
            pltpu.VMEM((4, n, t, t), BF16), pltpu.VMEM((4, n, 1, t), F32)]


def _normalised(acc_scr, s, nv):
    return acc_scr[s, :nv, :] / acc_scr[s, nv:nv + 1, :]


def _half_mask(x, half):
    lane = _lane_iota((1, LANES))
    return jnp.where((lane >= HEAD_DIM) == bool(half), x, jnp.zeros_like(x))


def _causal_schedule(nb):
    items = [(i, blk, int(n == 0)) for i in range(nb) for n, blk in enumerate([i] + list(range(i)))]
    assert len(items) % 4 == 0
    tail = (items[-1][0], items[-1][1], 0)
    return np.array(items + [tail, tail], np.int32).T.copy()


def _flash_persistent(tab_ref, qt_ref, k_ref, vt_ref, k_slices, q_rows, v_rows, scr, t, n_items, emit):
    m_scr, acc_scr, s_scr, p_scr, a_scr, cap_scr = scr
    n = len(k_slices)
    ones = jnp.ones((ONES_ROWS, t), BF16)
    krow = lax.broadcasted_iota(jnp.int32, (t, t), 0)
    qcol = lax.broadcasted_iota(jnp.int32, (t, t), 1)
    big = float(jnp.finfo(F32).max)
    cap_scr[0] = jnp.full((t, t), big, F32)
    cap_scr[1] = jnp.where(krow <= qcol, big, NEG_INF).astype(F32)

    def scores(w, s):
        tile, blk, first = tab_ref[0, w], tab_ref[1, w], tab_ref[2, w]
        b0 = pl.multiple_of(blk * t, t)
        st = jnp.dot(k_ref[0, pl.ds(b0, t), k_slices[s]], qt_ref[0, tile, q_rows[s], :],
                     preferred_element_type=F32)
        return jnp.minimum(st, cap_scr[first])

    def pv(w, s, p):
        vt = jnp.concatenate([vt_ref[0, tab_ref[1, w], v_rows[s], :], ones], axis=0)
        return jnp.dot(vt, p, preferred_element_type=F32)

    def softmax_stage(w, slot, rs, ws):
        first = tab_ref[2, w] == 1
        for s in range(n):
            st = s_scr[2 * rs + slot, s]
            m_prev = jnp.where(first, NEG_INF, m_scr[s])
            m_new = jnp.maximum(m_prev, jnp.max(st, axis=0, keepdims=True))
            a_scr[2 * ws + slot, s] = jnp.exp2(m_prev - m_new)
            p_scr[2 * ws + slot, s] = jnp.exp2(st - m_new).astype(BF16)
            m_scr[s] = m_new

    def accumulate(w, slot, rs, pvs):
        for s in range(n):
            acc_scr[s] = a_scr[2 * rs + slot, s] * acc_scr[s] + pvs[s]
        emit(tab_ref[0, w], acc_scr)

    def body(kk, rs, ws):
        w = 2 * kk
        wp = jnp.maximum(w - 2, 0)
        p_prev = [[p_scr[2 * rs + sl, s] for s in range(n)] for sl in range(2)]
        s_next = [[scores(w + 2 + sl, s) for s in range(n)] for sl in range(2)]
        pvs = [[pv(wp + sl, s, p_prev[sl][s]) for s in range(n)] for sl in range(2)]
        softmax_stage(w, 0, rs, ws)
        softmax_stage(w + 1, 1, rs, ws)
        for sl in range(2):
            for s in range(n):
                s_scr[2 * ws + sl, s] = s_next[sl][s]
        accumulate(wp, 0, rs, pvs[0])
        accumulate(wp + 1, 1, rs, pvs[1])

    for sl in range(2):
        for s in range(n):
            s_scr[sl, s] = scores(sl, s)
            p_scr[sl, s] = jnp.zeros((t, t), BF16)
            a_scr[sl, s] = jnp.ones((1, t), F32)
    for s in range(n):
        acc_scr[s] = jnp.zeros(acc_scr.shape[1:], F32)
        m_scr[s] = jnp.full((1, t), NEG_INF, F32)

    def two_bodies(kp, carry):
        body(2 * kp, 0, 1)
        body(2 * kp + 1, 1, 0)
        return carry

    lax.fori_loop(0, n_items // 4, two_bodies, 0)
    for sl in range(2):
        w = n_items - 2 + sl
        accumulate(w, sl, 0, [pv(w, s, p_scr[sl, s]) for s in range(n)])


def _mla_kernel(tab_ref, qt_ref, k_ref, vt_ref, ot_ref, *scr, t, nh, n_items):
    ks = [slice(LANES * s, LANES * (s + 1)) for s in range(nh)]
    vr = [slice(HEAD_DIM * s, HEAD_DIM * (s + 1)) for s in range(nh)]

    def emit(tile, acc_scr):
        for s in range(nh):
            l = acc_scr[s, HEAD_DIM:HEAD_DIM + 1, :]
            l = jnp.where(l > 0.0, l, 1.0)
            ot_ref[0, tile, vr[s], :] = (acc_scr[s, :HEAD_DIM, :] / l).astype(ot_ref.dtype)

    _flash_persistent(tab_ref, qt_ref, k_ref, vt_ref, ks, ks, vr, scr, t, n_items, emit)


def _mla_attention(qt, k, vt, nh):
    b, nb, _, t = qt.shape
    s = nb * t
    nv = vt.shape[2]
    tab = jnp.asarray(_causal_schedule(nb))
    once = pl.Buffered(1)
    return pl.pallas_call(
        functools.partial(_mla_kernel, t=t, nh=nh, n_items=tab.shape[1] - 2),
        grid=(b, MLA_HEADS // nh),
        in_specs=[pl.BlockSpec(memory_space=pltpu.SMEM),
                  pl.BlockSpec((1, nb, nh * LANES, t), lambda bi, p: (bi, 0, p, 0), pipeline_mode=once),
                  pl.BlockSpec((1, s, nh * LANES), lambda bi, p: (bi, 0, p), pipeline_mode=once),
                  pl.BlockSpec((1, nb, nh * HEAD_DIM, t), lambda bi, p: (bi, 0, p, 0), pipeline_mode=once)],
        out_specs=pl.BlockSpec((1, nb, nh * HEAD_DIM, t), lambda bi, p: (bi, 0, p, 0)),
        out_shape=jax.ShapeDtypeStruct((b, nb, nv, t), BF16),
        scratch_shapes=_flash_scratch(nh, HEAD_DIM, t) + [pltpu.VMEM((2, t, t), F32)],
        compiler_params=_cparams(("parallel", "arbitrary")),
        name="mla_attention",
    )(tab, qt, k, vt)


def _diff_kernel(q_ref, k_ref, vt_ref, lam_ref, sub_ref, ot_ref, *scr, t, nh, lam_init):
    i = pl.program_id(2)
    qs, ks, vr = [], [], []
    for hd in range(nh):
        cols = slice(LANES * hd, LANES * (hd + 1))
        qs += _q_tiles_t(q_ref[0, :, cols], True)
        ks += [cols, cols]
        vr += [cols, cols]
    _flash_t(qs, k_ref, vt_ref, ks, vr, scr, i, t)
    lv = lam_ref[...]
    lam = (jnp.exp(jnp.sum(lv[0:1] * lv[1:2], keepdims=True))
           - jnp.exp(jnp.sum(lv[2:3] * lv[3:4], keepdims=True)) + lam_init)
    for hd in range(nh):
        o = _normalised(scr[1], 2 * hd, LANES) - lam * _normalised(scr[1], 2 * hd + 1, LANES)
        o = o * lax.rsqrt(jnp.mean(o * o, axis=0, keepdims=True) + NORM_EPS) * sub_ref[...]
        ot_ref[0, 0, LANES * hd:LANES * (hd + 1), :] = (o * (1.0 - lam_init)).astype(ot_ref.dtype)


def _diff_attention(q, k, vt, lam_rows, subln_col, lam_init, nh):
    b, s, w = q.shape
    _, nb, _, t = vt.shape
    return pl.pallas_call(
        functools.partial(_diff_kernel, t=t, nh=nh, lam_init=lam_init),
        grid=(b, DIFF_HEADS // nh, nb),
        in_specs=[pl.BlockSpec((1, t, nh * LANES), lambda bi, p, i: (bi, i, p)),
                  pl.BlockSpec((1, s, nh * LANES), lambda bi, p, i: (bi, 0, p)),
                  pl.BlockSpec((1, nb, nh * LANES, t), lambda bi, p, i: (bi, 0, p, 0)),
                  pl.BlockSpec(lam_rows.shape, lambda bi, p, i: (0, 0)),
                  pl.BlockSpec(subln_col.shape, lambda bi, p, i: (0, 0))],
        out_specs=pl.BlockSpec((1, 1, nh * LANES, t), lambda bi, p, i: (bi, i, p, 0)),
        out_shape=jax.ShapeDtypeStruct((b, nb, w, t), BF16),
        scratch_shapes=_flash_scratch(2 * nh, LANES, t),
        compiler_params=_cparams(("parallel", "parallel", "arbitrary")),
        name="diff_attention",
    )(q, k, vt, lam_rows, subln_col)


def _moba_kernel(q_ref, qf_ref, k_ref, vt_ref, kmean_ref, ot_ref, sel_scr, *scr, t, npair):
    i = pl.program_id(2)
    nb = kmean_ref.shape[1]
    brow = lax.broadcasted_iota(jnp.int32, (nb, t), 0).astype(F32)
    past = brow < i.astype(F32)
    qs, ks, vr = [], [], []
    for pr in range(npair):
        cols = slice(LANES * pr, LANES * (pr + 1))
        qf, kmean = qf_ref[0, :, cols], kmean_ref[0, :, cols]
        qs += _q_tiles_t(q_ref[0, :, cols], True)
        for half in range(2):
            s = 2 * pr + half
            gate = lax.dot_general(kmean, _half_mask(qf, half), (((1,), (1,)), ((), ())),
                                   preferred_element_type=F32, precision=lax.Precision.HIGHEST)
            gate = jnp.where(past, gate, NEG_INF)
            sel = jnp.zeros((nb, t), F32)
            for _ in range(MOBA_TOPK):
                top = jnp.max(gate, axis=0, keepdims=True)
                first = jnp.min(jnp.where(gate == top, brow, float(nb)), axis=0, keepdims=True)
                pick = brow == first
                sel = jnp.where(pick, 1.0, sel)
                gate = jnp.where(pick, 2 * NEG_INF, gate)
            sel_scr[s] = jnp.where(past, sel, 0.0)
            ks.append(cols)
            vr.append(slice(HEAD_DIM * s, HEAD_DIM * (s + 1)))
    _flash_t(qs, k_ref, vt_ref, ks, vr, scr, i, t,
             past_mask=lambda s, jj: sel_scr[s, pl.ds(jj, 1), :] > 0.5)
    for s in range(2 * npair):
        ot_ref[0, 0, vr[s], :] = _normalised(scr[1], s, HEAD_DIM).astype(ot_ref.dtype)


def _moba_attention(q, qf, k, vt, kmean, npair):
    b, s, w = q.shape
    _, nb, _, t = vt.shape
    nkm = kmean.shape[1]
    lanes = npair * LANES
    return pl.pallas_call(
        functools.partial(_moba_kernel, t=t, npair=npair),
        grid=(b, w // lanes, nb),
        in_specs=[pl.BlockSpec((1, t, lanes), lambda bi, p, i: (bi, i, p)),
                  pl.BlockSpec((1, t, lanes), lambda bi, p, i: (bi, i, p)),
                  pl.BlockSpec((1, s, lanes), lambda bi, p, i: (bi, 0, p)),
                  pl.BlockSpec((1, nb, lanes, t), lambda bi, p, i: (bi, 0, p, 0)),
                  pl.BlockSpec((1, nkm, lanes), lambda bi, p, i: (bi, 0, p))],
        out_specs=pl.BlockSpec((1, 1, lanes, t), lambda bi, p, i: (bi, i, p, 0)),
        out_shape=jax.ShapeDtypeStruct((b, nb, w, t), BF16),
        scratch_shapes=[pltpu.VMEM((2 * npair, nkm, t), F32)] + _flash_scratch(2 * npair, HEAD_DIM, t),
        compiler_params=_cparams(("parallel", "parallel", "arbitrary")),
        name="moba_attention",
    )(q, qf, k, vt, kmean)


def _sw_kernel(q_ref, kc_ref, vc_ref, kp_ref, vp_ref, o_ref, l_ref, *, nbk):
    i = pl.program_id(1)
    blk = SW_BLOCK
    qi = lax.broadcasted_iota(jnp.int32, (blk, 2 * blk), 0)
    kj = lax.broadcasted_iota(jnp.int32, (blk, 2 * blk), 1)
    band = (kj >= qi) & (kj <= qi + blk)
    lane = _lane_iota((1, LANES))
    heads = [(pair, half) for pair in range(DIL_W // LANES) for half in range(2)]

    def scores(n):
        rows = slice(blk * n, blk * (n + 1))
        q = q_ref[0, rows, :]
        if n == 0:
            kprev, valid = kp_ref[0], band & ((kj >= blk) | (i > 0))
        else:
            kprev, valid = kc_ref[0, blk * (n - 1):blk * n, :], band
        kk = jnp.concatenate([kprev, kc_ref[0, rows, :]], axis=0)
        out = []
        for pair, half in heads:
            sl = slice(LANES * pair, LANES * (pair + 1))
            out.append(jnp.where(valid, _nt_dot(_half_mask(q[:, sl], half), kk[:, sl]), NEG_INF))
        return out

    nxt = scores(0)
    for n in range(nbk):
        cur, rows = nxt, slice(blk * n, blk * (n + 1))
        if n + 1 < nbk:
            nxt = scores(n + 1)
        vprev = vp_ref[0] if n == 0 else vc_ref[0, blk * (n - 1):blk * n, :]
        vv = jnp.concatenate([vprev, vc_ref[0, rows, :]], axis=0)
        outs, lses = [], []
        for (pair, half), sc in zip(heads, cur):
            m = jnp.max(sc, axis=-1, keepdims=True)
            p = jnp.exp2(sc - m)
            l = jnp.sum(p, axis=-1, keepdims=True)
            sl = slice(LANES * pair, LANES * (pair + 1))
            outs.append(jnp.dot(p.astype(BF16), vv[:, sl], preferred_element_type=F32) / l)
            lses.append(m + jnp.log2(l))
        for pair in range(DIL_W // LANES):
            sl = slice(LANES * pair, LANES * (pair + 1))
            o_ref[0, rows, sl] = jnp.where(lane < HEAD_DIM, outs[2 * pair], outs[2 * pair + 1])
            l_ref[0, rows, sl] = jnp.where(lane < HEAD_DIM, lses[2 * pair], lses[2 * pair + 1])


def _sliding_window(dil, nbk, name):
    lead, (m, w3) = dil.shape[:-2], dil.shape[-2:]
    seqs = dil.reshape((-1, m, w3))
    nbk = min(nbk, m // SW_BLOCK)
    rows = SW_BLOCK * nbk
    cur = lambda off: pl.BlockSpec((1, rows, DIL_W), lambda n, i: (n, i, off))
    prev = lambda off: pl.BlockSpec((1, SW_BLOCK, DIL_W), lambda n, i: (n, jnp.maximum(i * nbk - 1, 0), off))
    out_spec = pl.BlockSpec((1, rows, DIL_W), lambda n, i: (n, i, 0))
    o, l = pl.pallas_call(
        functools.partial(_sw_kernel, nbk=nbk),
        grid=(seqs.shape[0], m // rows),
        in_specs=[cur(0), cur(1), cur(2), prev(1), prev(2)],
        out_specs=[out_spec, out_spec],
        out_shape=[jax.ShapeDtypeStruct((seqs.shape[0], m, DIL_W), F32)] * 2,
        compiler_params=_cparams(("parallel", "arbitrary")),
        name=name,
    )(seqs, seqs, seqs, seqs, seqs)
    return o.reshape(lead + (m, DIL_W)), l.reshape(lead + (m, DIL_W))


def _mlp_tail(y, mlp_refs):
    sh_ref, sc_ref, g_ref, nw_ref, w1_ref, w2_ref = mlp_refs
    h = (_rms(y, nw_ref[...]) * (1.0 + sc_ref[0]) + sh_ref[0]).astype(BF16)
    acc = jnp.zeros(y.shape, F32)
    for c in range(w1_ref.shape[1] // MLP_CHUNK):
        cols = slice(MLP_CHUNK * c, MLP_CHUNK * (c + 1))
        a = jnp.maximum(jnp.dot(h, w1_ref[:, cols], preferred_element_type=F32), 0.0)
        acc = acc + jnp.dot((a * a).astype(BF16), w2_ref[cols, :], preferred_element_type=F32)
    return y + g_ref[0] * acc


def _proj_t(ot_ref, w_rows):
    return jnp.concatenate([_tn_dot(ot_ref[0, j], w_rows) for j in range(ot_ref.shape[1])], axis=0)


def _mlp_specs(mlp_args, d):
    per_b = pl.BlockSpec((1, 1, d), lambda bi, i: (bi, 0, 0))
    const = lambda a: pl.BlockSpec(a.shape, lambda bi, i: (0, 0), pipeline_mode=pl.Buffered(1))
    sh, sc, g2, nw, w1, w2 = mlp_args
    return [per_b, per_b, per_b, pl.BlockSpec(nw.shape, lambda bi, i: (0, 0)), const(w1), const(w2)]


def _even_out_kernel(x_ref, g_ref, oat_ref, o0_ref, o1_ref, o2_ref, l0_ref, l1_ref, l2_ref, w_ref,
                     *rest):
    *mlp_refs, y_ref, tscr = rest
    tm = x_ref.shape[1]
    ncol = DIL_W // LANES

    def token_major(ref, slot):
        r = ref.shape[1]
        for c in range(r):
            for j in range(ncol):
                tscr[slot * ncol + j, pl.ds(c, tm // r, stride=r), :] = ref[0, c, :, LANES * j:LANES * (j + 1)]
        return jnp.concatenate([tscr[slot * ncol + j] for j in range(ncol)], axis=1)

    o0, l0 = o0_ref[0], l0_ref[0]
    o1, l1 = token_major(o1_ref, 0), token_major(l1_ref, 1)
    o2, l2 = token_major(o2_ref, 2), token_major(l2_ref, 3)
    top = jnp.maximum(jnp.maximum(l0, l1), l2)
    w0, w1, w2 = jnp.exp2(l0 - top), jnp.exp2(l1 - top), jnp.exp2(l2 - top)
    ob = (w0 * o0 + w1 * o1 + w2 * o2) / (w0 + w1 + w2)
    na = oat_ref.shape[2]
    y = _proj_t(oat_ref, w_ref[:na, :]) + jnp.dot(ob.astype(BF16), w_ref[na:, :], preferred_element_type=F32)
    y_ref[0] = _mlp_tail(x_ref[0] + g_ref[0] * y, mlp_refs)


def _t_spec(a, tm):
    return pl.BlockSpec((1, tm // a.shape[-1]) + a.shape[2:], lambda bi, i: (bi, i, 0, 0))


def _even_out(x, g1, oat, o_dil, l_dil, w_out, mlp_args, tm):
    b, s, d = x.shape
    tok = lambda w: pl.BlockSpec((1, tm, w), lambda bi, i: (bi, i, 0))

    def dil_spec(a):
        if a.ndim == 3:
            return tok(DIL_W)
        r = a.shape[1]
        return pl.BlockSpec((1, r, tm // r, DIL_W), lambda bi, i: (bi, 0, i, 0))

    return pl.pallas_call(
        _even_out_kernel,
        grid=(b, s // tm),
        in_specs=[tok(d), pl.BlockSpec((1, 1, d), lambda bi, i: (bi, 0, 0)), _t_spec(oat, tm)]
                 + [dil_spec(a) for a in (*o_dil, *l_dil)]
                 + [pl.BlockSpec(w_out.shape, lambda bi, i: (0, 0), pipeline_mode=pl.Buffered(1))]
                 + _mlp_specs(mlp_args, d),
        out_specs=tok(d),
        out_shape=jax.ShapeDtypeStruct((b, s, d), F32),
        scratch_shapes=[pltpu.VMEM((4 * DIL_W // LANES, tm, LANES), F32)],
        compiler_params=_cparams(("parallel", "arbitrary")),
        name="even_out_mlp",
    )(x, g1, oat, *o_dil, *l_dil, w_out, *mlp_args)


def _odd_out_kernel(x_ref, g_ref, oct_ref, odt_ref, w_ref, *rest):
    *mlp_refs, y_ref = rest
    nc = oct_ref.shape[2]
    y = _proj_t(oct_ref, w_ref[:nc, :]) + _proj_t(odt_ref, w_ref[nc:, :])
    y_ref[0] = _mlp_tail(x_ref[0] + g_ref[0] * y, mlp_refs)


def _odd_out(x, g1, oct, odt, w_out, mlp_args, tm):
    b, s, d = x.shape
    tok = lambda w: pl.BlockSpec((1, tm, w), lambda bi, i: (bi, i, 0))
    return pl.pallas_call(
        _odd_out_kernel,
        grid=(b, s // tm),
        in_specs=[tok(d), pl.BlockSpec((1, 1, d), lambda bi, i: (bi, 0, 0)), _t_spec(oct, tm), _t_spec(odt, tm),
                  pl.BlockSpec(w_out.shape, lambda bi, i: (0, 0), pipeline_mode=pl.Buffered(1))]
                 + _mlp_specs(mlp_args, d),
        out_specs=tok(d),
        out_shape=jax.ShapeDtypeStruct((b, s, d), F32),
        compiler_params=_cparams(("parallel", "arbitrary")),
        name="odd_out_mlp",
    )(x, g1, oct, odt, w_out, *mlp_args)


def _rope_tables(positions):
    pos = positions.astype(F32)[:, :, None]

    def cs(half):
        inv = ROPE_THETA ** (-jnp.arange(half, dtype=F32) / half)
        ang = pos * inv
        return jnp.cos(ang), jnp.sin(ang)

    c, s = cs(HEAD_DIM // 2)
    c64 = jnp.concatenate([c, c, c, c], axis=-1)
    s64 = jnp.concatenate([-s, s, -s, s], axis=-1)
    c, s = cs(MLA_ROPE // 2)
    ones = jnp.ones(pos.shape[:2] + (MLA_NOPE,), F32)
    pad = LANES - MLA_QK
    cm = jnp.concatenate([ones, c, c, jnp.ones(pos.shape[:2] + (pad,), F32)], axis=-1)
    sm = jnp.concatenate([0 * ones, -s, s, jnp.zeros(pos.shape[:2] + (pad,), F32)], axis=-1)
    return cm, sm, c64, s64


def _partner(n, width):
    idx = np.arange(n)
    return np.where(idx % width < width // 2, idx + width // 2, idx - width // 2)


def _even_weights(w_in, w_uq, w_ukv, qn, kn):
    o2 = MLA_Q_RANK + MLA_KV_RANK
    o3 = o2 + MLA_ROPE
    d = w_in.shape[0]
    p_rope = _partner(MLA_ROPE, MLA_ROPE)
    zeros = lambda n: jnp.zeros((d, n), w_in.dtype)
    kr = w_in[:, o2:o3]
    tail = LANES - MLA_QK
    krp = jnp.concatenate([zeros(MLA_NOPE), kr, zeros(tail)], axis=1)
    krp_rot = jnp.concatenate([zeros(MLA_NOPE), kr[:, p_rope], zeros(tail)], axis=1)
    dil = w_in[:, o3:].reshape(d, 3, len(DIL_CONFIGS), DIL_W)
    dil_main = dil.transpose(0, 2, 1, 3).reshape(d, -1)
    dil_rot = dil[:, :2][..., _partner(DIL_W, HEAD_DIM)].transpose(0, 2, 1, 3).reshape(d, -1)
    win = jnp.concatenate([w_in[:, :o2], krp, krp_rot, dil_main, dil_rot], axis=1)

    padq = ((0, 0), (0, 0), (0, tail))
    wuq_rot = jnp.concatenate([w_uq[:, :, :MLA_NOPE], w_uq[:, :, MLA_NOPE:][:, :, p_rope]], axis=-1)
    wuq = jnp.concatenate([jnp.pad(w_uq, padq).reshape(MLA_Q_RANK, -1),
                           jnp.pad(wuq_rot, padq).reshape(MLA_Q_RANK, -1)], axis=1)
    wuk = jnp.pad(w_ukv[:, :, :MLA_NOPE], ((0, 0), (0, 0), (0, LANES - MLA_NOPE))).reshape(MLA_KV_RANK, -1)
    wvt = w_ukv[:, :, MLA_NOPE:].reshape(MLA_KV_RANK, -1).T

    def gains(g):
        rot = jnp.concatenate([g[:MLA_NOPE], g[MLA_NOPE:][p_rope]])
        return jnp.stack([jnp.pad(g, (0, tail)), jnp.pad(rot, (0, tail))]).astype(F32)

    return win.astype(BF16), wuq.astype(BF16), wuk.astype(BF16), wvt.astype(BF16), gains(qn), gains(kn)


def _odd_weights(w_in):
    nqk = DIFF_HEADS * LANES
    nm = MOBA_HEADS * HEAD_DIM
    main = jnp.concatenate([w_in[:, :2 * nqk], w_in[:, 3 * nqk:3 * nqk + 2 * nm]], axis=1)
    win = jnp.concatenate([main, main[:, _partner(main.shape[1], HEAD_DIM)]], axis=1)
    wvt = jnp.concatenate([w_in[:, 2 * nqk:3 * nqk], w_in[:, 3 * nqk + 2 * nm:]], axis=1).T
    return win.astype(BF16), wvt.astype(BF16)


def _gains64(g):
    rot = g[_partner(HEAD_DIM, HEAD_DIM)]
    return jnp.stack([jnp.concatenate([g, g]), jnp.concatenate([rot, rot])]).astype(F32)


def kernel(x, c, positions, ada_w, ada_b, norm_mix, norm_mlp, mlp_w1, mlp_w2, even_w_in, even_w_out, mla_q_lat_norm, mla_kv_lat_norm, mla_w_uq, mla_w_ukv, mla_q_norm, mla_k_norm, dil_q_norm, dil_k_norm, odd_w_in, odd_w_out, diff_q_norm, diff_k_norm, diff_lambda, diff_subln, moba_q_norm, moba_k_norm):
    b, s, d = x.shape
    depth = ada_w.shape[0]
    t_attn, tm_in, tm_out = ATTN_BLOCK, TOKEN_TILE, TOKEN_TILE

    mod = _adaln(c, ada_w, ada_b)
    cm, sm, c64, s64 = _rope_tables(positions)

    for layer in range(depth):
        sh1, sc1, g1, sh2, sc2, g2 = [mod[layer, :, d * t:d * (t + 1)].reshape(b, 1, d) for t in range(6)]
        mlp_args = (sh2, sc2, g2, _row(norm_mlp[layer]), mlp_w1[layer].astype(BF16), mlp_w2[layer].astype(BF16))
        i = layer // 2
        if layer % 2 == 0:
            win, wuq, wuk, wvt, qn, kn = _even_weights(even_w_in[i], mla_w_uq[i], mla_w_ukv[i],
                                                       mla_q_norm[i], mla_k_norm[i])
            q, k, vt, *dils = _even_in(x, sh1, sc1, _row(norm_mix[layer]), win, wuq, wuk, wvt,
                                       _row(mla_q_lat_norm[i]), _row(mla_kv_lat_norm[i]), qn, kn,
                                       _gains64(dil_q_norm[i]), _gains64(dil_k_norm[i]), cm, sm, c64, s64, tm_in, t_attn)
            o_at = _mla_attention(q, k, vt, MLA_HEADS_PER_STEP)
            o_dil, l_dil = zip(*[_sliding_window(dg, SW_BLOCKS_PER_STEP, f"sliding_window_g{g}")
                                 for g, dg in enumerate(dils)])
            x = _even_out(x, g1, o_at, o_dil, l_dil, even_w_out[i].astype(BF16), mlp_args, tm_out)
        else:
            lam_init = 0.8 - 0.6 * math.exp(-0.3 * layer)
            win, wvt = _odd_weights(odd_w_in[i])
            qc, kc, qm, qmf, km, vct, vmt, kmean = _odd_in(
                x, sh1, sc1, _row(norm_mix[layer]), win, wvt,
                _gains64(diff_q_norm[i]), _gains64(diff_k_norm[i]), _gains64(moba_q_norm[i]),
                _gains64(moba_k_norm[i]),
                c64, s64, tm_in, t_attn)
            o_ct = _diff_attention(qc, kc, vct, diff_lambda[i].astype(F32),
                                   diff_subln[i].reshape(-1, 1).astype(F32), lam_init, DIFF_HEADS_PER_STEP)
            o_dt = _moba_attention(qm, qmf, km, vmt, kmean.reshape(b, s // MOBA_BLOCK, kmean.shape[-1]),
                                   MOBA_PAIRS_PER_STEP)
            x = _odd_out(x, g1, o_ct, o_dt, odd_w_out[i].astype(BF16), mlp_args, tm_out)
    return x
```

```python
import functools
import math

import jax
import jax.numpy as jnp
import numpy as np
from jax import lax
from jax.experimental import pallas as pl
from jax.experimental.pallas import tpu as pltpu

F32 = jnp.float32
BF16 = jnp.bfloat16

LANES = 128
HEAD_DIM = 64
ROPE_THETA = 10000.0
NORM_EPS = 1e-6
NEG_INF = -1e30
LOG2E = math.log2(math.e)

MLA_HEADS = 8
MLA_Q_RANK = 384
MLA_KV_RANK = 256
MLA_NOPE = 64
MLA_ROPE = 32
MLA_QK = MLA_NOPE + MLA_ROPE
DIL_CONFIGS = ((128, 1), (512, 4), (2048, 16))
DIL_HEADS = 4
DIL_W = DIL_HEADS * HEAD_DIM
DIFF_HEADS = 4
MOBA_HEADS = 8
MOBA_BLOCK = 256
MOBA_TOPK = 3
SW_BLOCK = 128
ONES_ROWS = 16

VMEM_LIMIT = 56 * 1024 * 1024

ATTN_BLOCK = MOBA_BLOCK
TOKEN_TILE = 2 * ATTN_BLOCK
MLP_CHUNK = 1024
ADALN_TILE = 1024
SW_BLOCKS_PER_STEP = 4
MLA_HEADS_PER_STEP = 4
DIFF_HEADS_PER_STEP = 2
MOBA_PAIRS_PER_STEP = 2


def _cparams(sem):
    return pltpu.CompilerParams(dimension_semantics=sem, vmem_limit_bytes=VMEM_LIMIT)


def _nt_dot(a, b):
    return lax.dot_general(a, b, (((1,), (1,)), ((), ())), preferred_element_type=F32)


def _tn_dot(a, b):
    return lax.dot_general(a, b, (((0,), (0,)), ((), ())), preferred_element_type=F32)


def _rms(x, w):
    return x * lax.rsqrt(jnp.mean(x * x, axis=-1, keepdims=True) + NORM_EPS) * w


def _lane_iota(shape):
    return lax.broadcasted_iota(jnp.int32, shape, len(shape) - 1)


def _adaln_kernel(c_ref, w_ref, b_ref, o_ref):
    c = c_ref[...]
    cond = c * (1.0 / (1.0 + jnp.exp(-c)))
    o_ref[0] = jnp.dot(cond, w_ref[0], preferred_element_type=F32,
                       precision=lax.Precision.HIGHEST) + b_ref[0]


def _adaln(c, ada_w, ada_b):
    depth, d, n = ada_w.shape
    b = c.shape[0]
    tn = ADALN_TILE
    return pl.pallas_call(
        _adaln_kernel,
        grid=(depth, n // tn),
        in_specs=[pl.BlockSpec((b, d), lambda l, j: (0, 0)),
                  pl.BlockSpec((1, d, tn), lambda l, j: (l, 0, j)),
                  pl.BlockSpec((1, 1, tn), lambda l, j: (l, 0, j))],
        out_specs=pl.BlockSpec((1, b, tn), lambda l, j: (l, 0, j)),
        out_shape=jax.ShapeDtypeStruct((depth, b, n), F32),
        compiler_params=_cparams(("parallel", "parallel")),
        name="adaln",
    )(c, ada_w, ada_b.reshape(depth, 1, n))


def _inv_rms64(x):
    lane = _lane_iota((1, LANES))
    lo = lane < HEAD_DIM
    sq = x * x
    s_lo = jnp.sum(jnp.where(lo, sq, 0.0), axis=-1, keepdims=True)
    s_hi = jnp.sum(jnp.where(lo, 0.0, sq), axis=-1, keepdims=True)
    return lax.rsqrt(jnp.where(lo, s_lo, s_hi) * (1.0 / HEAD_DIM) + NORM_EPS)


def _store_t_blocks(out_ref, xt):
    t = out_ref.shape[-1]
    for j in range(out_ref.shape[1]):
        out_ref[0, j] = xt[:, t * j:t * (j + 1)]


def _t_blocks(b, s, rows, tm, t):
    return (pl.BlockSpec((1, tm // t, rows, t), lambda bi, i: (bi, i, 0, 0)),
            jax.ShapeDtypeStruct((b, s // t, rows, t), BF16))


def _even_in_kernel(x_ref, sh_ref, sc_ref, nw_ref, win_ref, wuq_ref, wuk_ref, wvt_ref,
                    qlat_ref, kvlat_ref, qn_ref, kn_ref, dqn_ref, dkn_ref,
                    cm_ref, sm_ref, c64_ref, s64_ref,
                    q_out, k_out, vt_out, d0_out, d1_out, d2_out, dscr):
    x = x_ref[0]
    tm = x.shape[0]
    h = _rms(x, nw_ref[...]) * (1.0 + sc_ref[0]) + sh_ref[0]
    u = jnp.dot(h.astype(BF16), win_ref[...], preferred_element_type=F32)

    o1 = MLA_Q_RANK
    o2 = o1 + MLA_KV_RANK
    o_kr, o_krr, o_dil = o2, o2 + LANES, o2 + 2 * LANES
    n_dil = 3 * DIL_W * len(DIL_CONFIGS)
    o_rot = o_dil + n_dil
    nq = MLA_HEADS * LANES
    cqn = _rms(u[:, :o1], qlat_ref[...]).astype(BF16)
    qp = jnp.dot(cqn, wuq_ref[...], preferred_element_type=F32)
    ckvn = _rms(u[:, o1:o2], kvlat_ref[...]).astype(BF16)
    kvp = jnp.dot(ckvn, wuk_ref[...], preferred_element_type=F32)
    _store_t_blocks(vt_out, _nt_dot(wvt_ref[...], ckvn).astype(BF16))
    kr, kr_rot = u[:, o_kr:o_kr + LANES], u[:, o_krr:o_krr + LANES]

    cm, sm = cm_ref[0], sm_ref[0]
    q_scale = MLA_QK ** -0.5 * LOG2E
    qgc, qgs = qn_ref[0:1] * cm * q_scale, qn_ref[1:2] * sm * q_scale
    kgc, kgs = kn_ref[0:1] * cm, kn_ref[1:2] * sm
    kr_term = kr_rot * kgs
    for hd in range(MLA_HEADS):
        sl = slice(LANES * hd, LANES * (hd + 1))
        qh = qp[:, sl]
        inv = lax.rsqrt(jnp.sum(qh * qh, -1, keepdims=True) * (1.0 / MLA_QK) + NORM_EPS)
        q_out[0, :, sl] = (inv * (qh * qgc + qp[:, nq + LANES * hd:nq + LANES * (hd + 1)] * qgs)).astype(BF16)
        kh = kvp[:, sl] + kr
        inv = lax.rsqrt(jnp.sum(kh * kh, -1, keepdims=True) * (1.0 / MLA_QK) + NORM_EPS)
        k_out[0, :, sl] = (inv * (kh * kgc + kr_term)).astype(BF16)

    c64, s64 = c64_ref[0], s64_ref[0]
    d_scale = HEAD_DIM ** -0.5 * LOG2E
    dqc, dqs = dqn_ref[0:1] * c64 * d_scale, dqn_ref[1:2] * s64 * d_scale
    dkc, dks = dkn_ref[0:1] * c64, dkn_ref[1:2] * s64
    ncol = DIL_W // LANES
    for g, d_out in enumerate((d0_out, d1_out, d2_out)):
        _, r = DIL_CONFIGS[g]
        base = o_dil + 3 * DIL_W * g
        rbase = o_rot + 2 * DIL_W * g
        for j in range(3 * ncol):
            xc = u[:, base + LANES * j:base + LANES * (j + 1)]
            if j < 2 * ncol:
                xr = u[:, rbase + LANES * j:rbase + LANES * (j + 1)]
                gc, gs = (dqc, dqs) if j < ncol else (dkc, dks)
                xc = _inv_rms64(xc) * (xc * gc + xr * gs)
            if r == 1:
                d_out[0, :, LANES * j:LANES * (j + 1)] = xc.astype(BF16)
            else:
                dscr[j] = xc
                for c in range(r):
                    d_out[0, c, :, LANES * j:LANES * (j + 1)] = dscr[j, pl.ds(c, tm // r, stride=r), :].astype(BF16)


def _row(v):
    return v.reshape(1, -1).astype(F32)


def _even_in(x, sh, sc, nw, win, wuq, wuk, wvt, qlat, kvlat, qn, kn, dqn, dkn, cm, sm, c64, s64, tm, t):
    b, s, d = x.shape
    tok = lambda w: pl.BlockSpec((1, tm, w), lambda bi, i: (bi, i, 0))
    per_b = pl.BlockSpec((1, 1, d), lambda bi, i: (bi, 0, 0))
    full = lambda a: pl.BlockSpec(a.shape, lambda bi, i: (0,) * a.ndim, pipeline_mode=pl.Buffered(1))
    vt_spec, vt_shape = _t_blocks(b, s, wvt.shape[0], tm, t)
    dil_specs, dil_shapes = [], []
    for _, r in DIL_CONFIGS:
        if r == 1:
            dil_specs.append(tok(3 * DIL_W))
            dil_shapes.append(jax.ShapeDtypeStruct((b, s, 3 * DIL_W), BF16))
        else:
            dil_specs.append(pl.BlockSpec((1, r, tm // r, 3 * DIL_W), lambda bi, i: (bi, 0, i, 0)))
            dil_shapes.append(jax.ShapeDtypeStruct((b, r, s // r, 3 * DIL_W), BF16))
    return pl.pallas_call(
        _even_in_kernel,
        grid=(b, s // tm),
        in_specs=[tok(d), per_b, per_b, full(nw), full(win), full(wuq), full(wuk), full(wvt),
                  full(qlat), full(kvlat), full(qn), full(kn), full(dqn), full(dkn),
                  tok(LANES), tok(LANES), tok(LANES), tok(LANES)],
        out_specs=[tok(MLA_HEADS * LANES), tok(MLA_HEADS * LANES), vt_spec] + dil_specs,
        out_shape=[jax.ShapeDtypeStruct((b, s, MLA_HEADS * LANES), BF16),
                   jax.ShapeDtypeStruct((b, s, MLA_HEADS * LANES), BF16), vt_shape] + dil_shapes,
        scratch_shapes=[pltpu.VMEM((3 * DIL_W // LANES, tm, LANES), F32)],
        compiler_params=_cparams(("parallel", "arbitrary")),
        name="even_in_proj",
    )(x, sh, sc, nw, win, wuq, wuk, wvt, qlat, kvlat, qn, kn, dqn, dkn, cm, sm, c64, s64)


def _odd_in_kernel(x_ref, sh_ref, sc_ref, nw_ref, win_ref, wvt_ref, dqn_ref, dkn_ref, mqn_ref, mkn_ref,
                   c64_ref, s64_ref,
                   qc_out, kc_out, qm_out, qmf_out, km_out, vct_out, vmt_out, kmean_out):
    x = x_ref[0]
    tm = x.shape[0]
    h = (_rms(x, nw_ref[...]) * (1.0 + sc_ref[0]) + sh_ref[0]).astype(BF16)
    u = jnp.dot(h, win_ref[...], preferred_element_type=F32)
    vt = _nt_dot(wvt_ref[...], h).astype(BF16)
    nvc = vct_out.shape[2]
    _store_t_blocks(vct_out, vt[:nvc])
    _store_t_blocks(vmt_out, vt[nvc:])
    c64, s64 = c64_ref[0], s64_ref[0]
    scale = HEAD_DIM ** -0.5 * LOG2E
    nqk = DIFF_HEADS * LANES
    nm = MOBA_HEADS * HEAD_DIM
    rot = 2 * nqk + 2 * nm

    def tables(gain_ref, mult):
        return gain_ref[0:1] * c64 * mult, gain_ref[1:2] * s64 * mult

    def roped(off, col, tab):
        a = off + LANES * col
        xc, xr = u[:, a:a + LANES], u[:, rot + a:rot + a + LANES]
        return _inv_rms64(xc) * (xc * tab[0] + xr * tab[1])

    t_qc, t_kc, t_qm, t_km = tables(dqn_ref, scale), tables(dkn_ref, 1.0), tables(mqn_ref, 1.0), tables(mkn_ref, 1.0)
    for col in range(nqk // LANES):
        sl = slice(LANES * col, LANES * (col + 1))
        qc_out[0, :, sl] = roped(0, col, t_qc).astype(BF16)
        kc_out[0, :, sl] = roped(nqk, col, t_kc).astype(BF16)

    base = 2 * nqk
    for col in range(nm // LANES):
        sl = slice(LANES * col, LANES * (col + 1))
        qm = roped(base, col, t_qm)
        qmf_out[0, :, sl] = qm
        qm_out[0, :, sl] = (qm * scale).astype(BF16)
        km = roped(base + nm, col, t_km)
        km_out[0, :, sl] = km.astype(BF16)
        for blk in range(tm // MOBA_BLOCK):
            rows = slice(MOBA_BLOCK * blk, MOBA_BLOCK * (blk + 1))
            kmean_out[0, 0, blk:blk + 1, sl] = jnp.mean(km[rows], axis=0, keepdims=True)


def _odd_in(x, sh, sc, nw, win, wvt, dqn, dkn, mqn, mkn, c64, s64, tm, t):
    b, s, d = x.shape
    tok = lambda w: pl.BlockSpec((1, tm, w), lambda bi, i: (bi, i, 0))
    per_b = pl.BlockSpec((1, 1, d), lambda bi, i: (bi, 0, 0))
    full = lambda a: pl.BlockSpec(a.shape, lambda bi, i: (0,) * a.ndim, pipeline_mode=pl.Buffered(1))
    w = DIFF_HEADS * LANES
    nblk = tm // MOBA_BLOCK
    shp = lambda dt: jax.ShapeDtypeStruct((b, s, w), dt)
    vt_spec, vt_shape = _t_blocks(b, s, wvt.shape[0] // 2, tm, t)
    return pl.pallas_call(
        _odd_in_kernel,
        grid=(b, s // tm),
        in_specs=[tok(d), per_b, per_b, full(nw), full(win), full(wvt), full(dqn), full(dkn), full(mqn), full(mkn),
                  tok(LANES), tok(LANES)],
        out_specs=[tok(w)] * 5 + [vt_spec, vt_spec, pl.BlockSpec((1, 1, nblk, w), lambda bi, i: (bi, i, 0, 0))],
        out_shape=[shp(BF16), shp(BF16), shp(BF16), shp(F32), shp(BF16), vt_shape, vt_shape,
                   jax.ShapeDtypeStruct((b, s // tm, nblk, w), F32)],
        compiler_params=_cparams(("parallel", "arbitrary")),
        name="odd_in_proj",
    )(x, sh, sc, nw, win, wvt, dqn, dkn, mqn, mkn, c64, s64)


def _q_tiles_t(q, split_halves):
    qt = q.astype(F32).T
    if not split_halves:
        return [qt.astype(BF16)]
    row = lax.broadcasted_iota(jnp.int32, qt.shape, 0)
    return [jnp.where((row >= HEAD_DIM) == bool(half), qt, 0.0).astype(BF16) for half in range(2)]


def _flash_t(qts, k_ref, vt_ref, k_slices, v_rows, scr, i, t, past_mask=None):
    m_scr, acc_scr, s_scr, p_scr, a_scr, b_scr = scr
    n = len(qts)
    ones = jnp.ones((ONES_ROWS, t), BF16)

    def qk(blk, s):
        b0 = pl.multiple_of(blk * t, t)
        return jnp.dot(k_ref[0, pl.ds(b0, t), k_slices[s]], qts[s], preferred_element_type=F32)

    def pv(blk, s, p):
        vt = jnp.concatenate([vt_ref[0, blk, v_rows[s], :], ones], axis=0)
        return jnp.dot(vt, p, preferred_element_type=F32)

    def store_scores(slot, s, st):
        s_scr[slot, s] = st
        b_scr[slot, s] = jnp.max(st, axis=0, keepdims=True)

    krow = lax.broadcasted_iota(jnp.int32, (t, t), 0)
    qcol = lax.broadcasted_iota(jnp.int32, (t, t), 1)
    causal = krow <= qcol
    last_past = jnp.maximum(i - 1, 0)
    for s in range(n):
        st = jnp.where(causal, qk(i, s), NEG_INF)
        m = jnp.max(st, axis=0, keepdims=True)
        m_scr[s] = m
        p_scr[0, s] = jnp.exp2(st - m).astype(BF16)
        p_scr[1, s] = jnp.zeros((t, t), BF16)
        a_scr[0, s] = jnp.ones_like(m)
        a_scr[1, s] = jnp.ones_like(m)
        acc_scr[s] = jnp.zeros(acc_scr.shape[1:], F32)
        store_scores(0, s, qk(0, s))
        store_scores(1, s, qk(jnp.minimum(1, last_past), s))

    def softmax_stage(slot, valid, rs, ws, blk):
        for s in range(n):
            cmax = b_scr[2 * rs + slot, s]
            keep = None if past_mask is None else past_mask(s, blk)
            if keep is not None:
                cmax = jnp.where(keep, cmax, NEG_INF)
            if valid is not None:
                cmax = jnp.where(valid, cmax, NEG_INF)
            m_prev = m_scr[s]
            m_new = jnp.maximum(m_prev, cmax)
            a_scr[2 * ws + slot, s] = jnp.exp2(m_prev - m_new)
            p = jnp.exp2(s_scr[2 * rs + slot, s] - m_new).astype(BF16)
            if keep is not None:
                p = jnp.where(keep, p, jnp.zeros_like(p))
            p_scr[2 * ws + slot, s] = p
            m_scr[s] = m_new

    def body(kk, rs, ws):
        first = kk == 0
        pa_blk = jnp.where(first, i, 2 * kk - 2)
        pb_blk = jnp.where(first, i, 2 * kk - 1)
        p_prev = [[p_scr[2 * rs + sl, s] for s in range(n)] for sl in range(2)]
        a_prev = [[a_scr[2 * rs + sl, s] for s in range(n)] for sl in range(2)]
        s_next = [[qk(jnp.minimum(2 * kk + 2 + sl, last_past), s) for s in range(n)] for sl in range(2)]
        pvs = [[pv(blk, s, p_prev[sl][s]) for s in range(n)] for sl, blk in enumerate((pa_blk, pb_blk))]
        softmax_stage(0, None, rs, ws, 2 * kk)
        softmax_stage(1, 2 * kk + 1 < i, rs, ws, jnp.minimum(2 * kk + 1, last_past))
        for sl in range(2):
            for s in range(n):
                store_scores(2 * ws + sl, s, s_next[sl][s])
        for s in range(n):
            acc_scr[s] = a_prev[1][s] * (a_prev[0][s] * acc_scr[s] + pvs[0][s]) + pvs[1][s]

    def two_bodies(kp, carry):
        body(2 * kp, 0, 1)
        body(2 * kp + 1, 1, 0)
        return carry

    nbody = (i + 1) // 2
    lax.fori_loop(0, nbody // 2, two_bodies, 0)

    @pl.when(nbody % 2 == 1)
    def _():
        body(nbody - 1, 0, 1)

    fs = nbody % 2
    kl = jnp.maximum(nbody - 1, 0)
    none = i == 0
    fa_blk = jnp.where(none, i, 2 * kl)
    fb_blk = jnp.where(none, i, jnp.minimum(2 * kl + 1, last_past))
    fb_scale = jnp.where(jnp.logical_or(none, i % 2 == 0), 1.0, 0.0)
    for s in range(n):
        acc_scr[s] = (a_scr[2 * fs + 1, s] * (a_scr[2 * fs, s] * acc_scr[s] + pv(fa_blk, s, p_scr[2 * fs, s]))
                      + fb_scale * pv(fb_blk, s, p_scr[2 * fs + 1, s]))


def _flash_scratch(n, nv, t):
    return [pltpu.VMEM((n, 1, t), F32), pltpu.VMEM((n, nv + ONES_ROWS, t), F32), pltpu.VMEM((4, n, t, t), F32),
            pltpu.VMEM((4, n, t, t), BF16), pltpu.VMEM((4, n, 1, t), F32), pltpu.VMEM((4, n, 1, t), F32)]


def _normalised(acc_scr, s, nv):
    return acc_scr[s, :nv, :] / acc_scr[s, nv:nv + 1, :]


def _half_mask(x, half):
    lane = _lane_iota((1, LANES))
    return jnp.where((lane >= HEAD_DIM) == bool(half), x, jnp.zeros_like(x))


def _mla_kernel(q_ref, k_ref, vt_ref, ot_ref, *scr, t, nh):
    i = pl.program_id(2)
    qs = [_q_tiles_t(q_ref[0, :, LANES * s:LANES * (s + 1)], False)[0] for s in range(nh)]
    ks = [slice(LANES * s, LANES * (s + 1)) for s in range(nh)]
    vr = [slice(HEAD_DIM * s, HEAD_DIM * (s + 1)) for s in range(nh)]
    _flash_t(qs, k_ref, vt_ref, ks, vr, scr, i, t)
    for s in range(nh):
        ot_ref[0, 0, vr[s], :] = _normalised(scr[1], s, HEAD_DIM).astype(ot_ref.dtype)


def _mla_attention(q, k, vt, nh):
    b, s, _ = q.shape
    _, nb, nv, t = vt.shape
    return pl.pallas_call(
        functools.partial(_mla_kernel, t=t, nh=nh),
        grid=(b, MLA_HEADS // nh, nb),
        in_specs=[pl.BlockSpec((1, t, nh * LANES), lambda bi, p, i: (bi, i, p)),
                  pl.BlockSpec((1, s, nh * LANES), lambda bi, p, i: (bi, 0, p)),
                  pl.BlockSpec((1, nb, nh * HEAD_DIM, t), lambda bi, p, i: (bi, 0, p, 0))],
        out_specs=pl.BlockSpec((1, 1, nh * HEAD_DIM, t), lambda bi, p, i: (bi, i, p, 0)),
        out_shape=jax.ShapeDtypeStruct((b, nb, nv, t), BF16),
        scratch_shapes=_flash_scratch(nh, HEAD_DIM, t),
        compiler_params=_cparams(("parallel", "parallel", "arbitrary")),
        name="mla_attention",
    )(q, k, vt)


def _diff_kernel(q_ref, k_ref, vt_ref, lam_ref, sub_ref, ot_ref, *scr, t, nh, lam_init):
    i = pl.program_id(2)
    qs, ks, vr = [], [], []
    for hd in range(nh):
        cols = slice(LANES * hd, LANES * (hd + 1))
        qs += _q_tiles_t(q_ref[0, :, cols], True)
        ks += [cols, cols]
        vr += [cols, cols]
    _flash_t(qs, k_ref, vt_ref, ks, vr, scr, i, t)
    lv = lam_ref[...]
    lam = (jnp.exp(jnp.sum(lv[0:1] * lv[1:2], keepdims=True))
           - jnp.exp(jnp.sum(lv[2:3] * lv[3:4], keepdims=True)) + lam_init)
    for hd in range(nh):
        o = _normalised(scr[1], 2 * hd, LANES) - lam * _normalised(scr[1], 2 * hd + 1, LANES)
        o = o * lax.rsqrt(jnp.mean(o * o, axis=0, keepdims=True) + NORM_EPS) * sub_ref[...]
        ot_ref[0, 0, LANES * hd:LANES * (hd + 1), :] = (o * (1.0 - lam_init)).astype(ot_ref.dtype)


def _diff_attention(q, k, vt, lam_rows, subln_col, lam_init, nh):
    b, s, w = q.shape
    _, nb, _, t = vt.shape
    return pl.pallas_call(
        functools.partial(_diff_kernel, t=t, nh=nh, lam_init=lam_init),
        grid=(b, DIFF_HEADS // nh, nb),
        in_specs=[pl.BlockSpec((1, t, nh * LANES), lambda bi, p, i: (bi, i, p)),
                  pl.BlockSpec((1, s, nh * LANES), lambda bi, p, i: (bi, 0, p)),
                  pl.BlockSpec((1, nb, nh * LANES, t), lambda bi, p, i: (bi, 0, p, 0)),
                  pl.BlockSpec(lam_rows.shape, lambda bi, p, i: (0, 0)),
                  pl.BlockSpec(subln_col.shape, lambda bi, p, i: (0, 0))],
        out_specs=pl.BlockSpec((1, 1, nh * LANES, t), lambda bi, p, i: (bi, i, p, 0)),
        out_shape=jax.ShapeDtypeStruct((b, nb, w, t), BF16),
        scratch_shapes=_flash_scratch(2 * nh, LANES, t),
        compiler_params=_cparams(("parallel", "parallel", "arbitrary")),
        name="diff_attention",
    )(q, k, vt, lam_rows, subln_col)


def _moba_kernel(q_ref, qf_ref, k_ref, vt_ref, kmean_ref, ot_ref, sel_scr, *scr, t, npair):
    i = pl.program_id(2)
    nb = kmean_ref.shape[1]
    brow = lax.broadcasted_iota(jnp.int32, (nb, t), 0).astype(F32)
    past = brow < i.astype(F32)
    qs, ks, vr = [], [], []
    for pr in range(npair):
        cols = slice(LANES * pr, LANES * (pr + 1))
        qf, kmean = qf_ref[0, :, cols], kmean_ref[0, :, cols]
        qs += _q_tiles_t(q_ref[0, :, cols], True)
        for half in range(2):
            s = 2 * pr + half
            gate = lax.dot_general(kmean, _half_mask(qf, half), (((1,), (1,)), ((), ())),
                                   preferred_element_type=F32, precision=lax.Precision.HIGHEST)
            gate = jnp.where(past, gate, NEG_INF)
            sel = jnp.zeros((nb, t), F32)
            for _ in range(MOBA_TOPK):
                top = jnp.max(gate, axis=0, keepdims=True)
                first = jnp.min(jnp.where(gate == top, brow, float(nb)), axis=0, keepdims=True)
                pick = brow == first
                sel = jnp.where(pick, 1.0, sel)
                gate = jnp.where(pick, 2 * NEG_INF, gate)
            sel_scr[s] = jnp.where(past, sel, 0.0)
            ks.append(cols)
            vr.append(slice(HEAD_DIM * s, HEAD_DIM * (s + 1)))
    _flash_t(qs, k_ref, vt_ref, ks, vr, scr, i, t,
             past_mask=lambda s, jj: sel_scr[s, pl.ds(jj, 1), :] > 0.5)
    for s in range(2 * npair):
        ot_ref[0, 0, vr[s], :] = _normalised(scr[1], s, HEAD_DIM).astype(ot_ref.dtype)


def _moba_attention(q, qf, k, vt, kmean, npair):
    b, s, w = q.shape
    _, nb, _, t = vt.shape
    nkm = kmean.shape[1]
    lanes = npair * LANES
    return pl.pallas_call(
        functools.partial(_moba_kernel, t=t, npair=npair),
        grid=(b, w // lanes, nb),
        in_specs=[pl.BlockSpec((1, t, lanes), lambda bi, p, i: (bi, i, p)),
                  pl.BlockSpec((1, t, lanes), lambda bi, p, i: (bi, i, p)),
                  pl.BlockSpec((1, s, lanes), lambda bi, p, i: (bi, 0, p)),
                  pl.BlockSpec((1, nb, lanes, t), lambda bi, p, i: (bi, 0, p, 0)),
                  pl.BlockSpec((1, nkm, lanes), lambda bi, p, i: (bi, 0, p))],
        out_specs=pl.BlockSpec((1, 1, lanes, t), lambda bi, p, i: (bi, i, p, 0)),
        out_shape=jax.ShapeDtypeStruct((b, nb, w, t), BF16),
        scratch_shapes=[pltpu.VMEM((2 * npair, nkm, t), F32)] + _flash_scratch(2 * npair, HEAD_DIM, t),
        compiler_params=_cparams(("parallel", "parallel", "arbitrary")),
        name="moba_attention",
    )(q, qf, k, vt, kmean)


def _sw_kernel(q_ref, kc_ref, vc_ref, kp_ref, vp_ref, o_ref, l_ref, *, nbk):
    i = pl.program_id(1)
    blk = SW_BLOCK
    qi = lax.broadcasted_iota(jnp.int32, (blk, 2 * blk), 0)
    kj = lax.broadcasted_iota(jnp.int32, (blk, 2 * blk), 1)
    band = (kj >= qi) & (kj <= qi + blk)
    lane = _lane_iota((1, LANES))
    heads = [(pair, half) for pair in range(DIL_W // LANES) for half in range(2)]

    def scores(n):
        rows = slice(blk * n, blk * (n + 1))
        q = q_ref[0, rows, :]
        if n == 0:
            kprev, valid = kp_ref[0], band & ((kj >= blk) | (i > 0))
        else:
            kprev, valid = kc_ref[0, blk * (n - 1):blk * n, :], band
        kk = jnp.concatenate([kprev, kc_ref[0, rows, :]], axis=0)
        out = []
        for pair, half in heads:
            sl = slice(LANES * pair, LANES * (pair + 1))
            out.append(jnp.where(valid, _nt_dot(_half_mask(q[:, sl], half), kk[:, sl]), NEG_INF))
        return out

    nxt = scores(0)
    for n in range(nbk):
        cur, rows = nxt, slice(blk * n, blk * (n + 1))
        if n + 1 < nbk:
            nxt = scores(n + 1)
        vprev = vp_ref[0] if n == 0 else vc_ref[0, blk * (n - 1):blk * n, :]
        vv = jnp.concatenate([vprev, vc_ref[0, rows, :]], axis=0)
        outs, lses = [], []
        for (pair, half), sc in zip(heads, cur):
            m = jnp.max(sc, axis=-1, keepdims=True)
            p = jnp.exp2(sc - m)
            l = jnp.sum(p, axis=-1, keepdims=True)
            sl = slice(LANES * pair, LANES * (pair + 1))
            outs.append(jnp.dot(p.astype(BF16), vv[:, sl], preferred_element_type=F32) / l)
            lses.append(m + jnp.log2(l))
        for pair in range(DIL_W // LANES):
            sl = slice(LANES * pair, LANES * (pair + 1))
            o_ref[0, rows, sl] = jnp.where(lane < HEAD_DIM, outs[2 * pair], outs[2 * pair + 1])
            l_ref[0, rows, sl] = jnp.where(lane < HEAD_DIM, lses[2 * pair], lses[2 * pair + 1])


def _sliding_window(dil, nbk, name):
    lead, (m, w3) = dil.shape[:-2], dil.shape[-2:]
    seqs = dil.reshape((-1, m, w3))
    nbk = min(nbk, m // SW_BLOCK)
    rows = SW_BLOCK * nbk
    cur = lambda off: pl.BlockSpec((1, rows, DIL_W), lambda n, i: (n, i, off))
    prev = lambda off: pl.BlockSpec((1, SW_BLOCK, DIL_W), lambda n, i: (n, jnp.maximum(i * nbk - 1, 0), off))
    out_spec = pl.BlockSpec((1, rows, DIL_W), lambda n, i: (n, i, 0))
    o, l = pl.pallas_call(
        functools.partial(_sw_kernel, nbk=nbk),
        grid=(seqs.shape[0], m // rows),
        in_specs=[cur(0), cur(1), cur(2), prev(1), prev(2)],
        out_specs=[out_spec, out_spec],
        out_shape=[jax.ShapeDtypeStruct((seqs.shape[0], m, DIL_W), F32)] * 2,
        compiler_params=_cparams(("parallel", "arbitrary")),
        name=name,
    )(seqs, seqs, seqs, seqs, seqs)
    return o.reshape(lead + (m, DIL_W)), l.reshape(lead + (m, DIL_W))


def _mlp_tail(y, mlp_refs):
    sh_ref, sc_ref, g_ref, nw_ref, w1_ref, w2_ref = mlp_refs
    h = (_rms(y, nw_ref[...]) * (1.0 + sc_ref[0]) + sh_ref[0]).astype(BF16)
    acc = jnp.zeros(y.shape, F32)
    for c in range(w1_ref.shape[1] // MLP_CHUNK):
        cols = slice(MLP_CHUNK * c, MLP_CHUNK * (c + 1))
        a = jnp.maximum(jnp.dot(h, w1_ref[:, cols], preferred_element_type=F32), 0.0)
        acc = acc + jnp.dot((a * a).astype(BF16), w2_ref[cols, :], preferred_element_type=F32)
    return y + g_ref[0] * acc


def _proj_t(ot_ref, w_rows):
    return jnp.concatenate([_tn_dot(ot_ref[0, j], w_rows) for j in range(ot_ref.shape[1])], axis=0)


def _mlp_specs(mlp_args, d):
    per_b = pl.BlockSpec((1, 1, d), lambda bi, i: (bi, 0, 0))
    const = lambda a: pl.BlockSpec(a.shape, lambda bi, i: (0, 0), pipeline_mode=pl.Buffered(1))
    sh, sc, g2, nw, w1, w2 = mlp_args
    return [per_b, per_b, per_b, pl.BlockSpec(nw.shape, lambda bi, i: (0, 0)), const(w1), const(w2)]


def _even_out_kernel(x_ref, g_ref, oat_ref, o0_ref, o1_ref, o2_ref, l0_ref, l1_ref, l2_ref, w_ref,
                     *rest):
    *mlp_refs, y_ref, tscr = rest
    tm = x_ref.shape[1]
    ncol = DIL_W // LANES

    def token_major(ref, slot):
        r = ref.shape[1]
        for c in range(r):
            for j in range(ncol):
                tscr[slot * ncol + j, pl.ds(c, tm // r, stride=r), :] = ref[0, c, :, LANES * j:LANES * (j + 1)]
        return jnp.concatenate([tscr[slot * ncol + j] for j in range(ncol)], axis=1)

    o0, l0 = o0_ref[0], l0_ref[0]
    o1, l1 = token_major(o1_ref, 0), token_major(l1_ref, 1)
    o2, l2 = token_major(o2_ref, 2), token_major(l2_ref, 3)
    top = jnp.maximum(jnp.maximum(l0, l1), l2)
    w0, w1, w2 = jnp.exp2(l0 - top), jnp.exp2(l1 - top), jnp.exp2(l2 - top)
    ob = (w0 * o0 + w1 * o1 + w2 * o2) / (w0 + w1 + w2)
    na = oat_ref.shape[2]
    y = _proj_t(oat_ref, w_ref[:na, :]) + jnp.dot(ob.astype(BF16), w_ref[na:, :], preferred_element_type=F32)
    y_ref[0] = _mlp_tail(x_ref[0] + g_ref[0] * y, mlp_refs)


def _t_spec(a, tm):
    return pl.BlockSpec((1, tm // a.shape[-1]) + a.shape[2:], lambda bi, i: (bi, i, 0, 0))


def _even_out(x, g1, oat, o_dil, l_dil, w_out, mlp_args, tm):
    b, s, d = x.shape
    tok = lambda w: pl.BlockSpec((1, tm, w), lambda bi, i: (bi, i, 0))

    def dil_spec(a):
        if a.ndim == 3:
            return tok(DIL_W)
        r = a.shape[1]
        return pl.BlockSpec((1, r, tm // r, DIL_W), lambda bi, i: (bi, 0, i, 0))

    return pl.pallas_call(
        _even_out_kernel,
        grid=(b, s // tm),
        in_specs=[tok(d), pl.BlockSpec((1, 1, d), lambda bi, i: (bi, 0, 0)), _t_spec(oat, tm)]
                 + [dil_spec(a) for a in (*o_dil, *l_dil)]
                 + [pl.BlockSpec(w_out.shape, lambda bi, i: (0, 0), pipeline_mode=pl.Buffered(1))]
                 + _mlp_specs(mlp_args, d),
        out_specs=tok(d),
        out_shape=jax.ShapeDtypeStruct((b, s, d), F32),
        scratch_shapes=[pltpu.VMEM((4 * DIL_W // LANES, tm, LANES), F32)],
        compiler_params=_cparams(("parallel", "arbitrary")),
        name="even_out_mlp",
    )(x, g1, oat, *o_dil, *l_dil, w_out, *mlp_args)


def _odd_out_kernel(x_ref, g_ref, oct_ref, odt_ref, w_ref, *rest):
    *mlp_refs, y_ref = rest
    nc = oct_ref.shape[2]
    y = _proj_t(oct_ref, w_ref[:nc, :]) + _proj_t(odt_ref, w_ref[nc:, :])
    y_ref[0] = _mlp_tail(x_ref[0] + g_ref[0] * y, mlp_refs)


def _odd_out(x, g1, oct, odt, w_out, mlp_args, tm):
    b, s, d = x.shape
    tok = lambda w: pl.BlockSpec((1, tm, w), lambda bi, i: (bi, i, 0))
    return pl.pallas_call(
        _odd_out_kernel,
        grid=(b, s // tm),
        in_specs=[tok(d), pl.BlockSpec((1, 1, d), lambda bi, i: (bi, 0, 0)), _t_spec(oct, tm), _t_spec(odt, tm),
                  pl.BlockSpec(w_out.shape, lambda bi, i: (0, 0), pipeline_mode=pl.Buffered(1))]
                 + _mlp_specs(mlp_args, d),
        out_specs=tok(d),
        out_shape=jax.ShapeDtypeStruct((b, s, d), F32),
        compiler_params=_cparams(("parallel", "arbitrary")),
        name="odd_out_mlp",
    )(x, g1, oct, odt, w_out, *mlp_args)


def _rope_tables(positions):
    pos = positions.astype(F32)[:, :, None]

    def cs(half):
        inv = ROPE_THETA ** (-jnp.arange(half, dtype=F32) / half)
        ang = pos * inv
        return jnp.cos(ang), jnp.sin(ang)

    c, s = cs(HEAD_DIM // 2)
    c64 = jnp.concatenate([c, c, c, c], axis=-1)
    s64 = jnp.concatenate([-s, s, -s, s], axis=-1)
    c, s = cs(MLA_ROPE // 2)
    ones = jnp.ones(pos.shape[:2] + (MLA_NOPE,), F32)
    pad = LANES - MLA_QK
    cm = jnp.concatenate([ones, c, c, jnp.ones(pos.shape[:2] + (pad,), F32)], axis=-1)
    sm = jnp.concatenate([0 * ones, -s, s, jnp.zeros(pos.shape[:2] + (pad,), F32)], axis=-1)
    return cm, sm, c64, s64


def _partner(n, width):
    idx = np.arange(n)
    return np.where(idx % width < width // 2, idx + width // 2, idx - width // 2)


def _even_weights(w_in, w_uq, w_ukv, qn, kn):
    o2 = MLA_Q_RANK + MLA_KV_RANK
    o3 = o2 + MLA_ROPE
    d = w_in.shape[0]
    p_rope = _partner(MLA_ROPE, MLA_ROPE)
    zeros = lambda n: jnp.zeros((d, n), w_in.dtype)
    kr = w_in[:, o2:o3]
    tail = LANES - MLA_QK
    krp = jnp.concatenate([zeros(MLA_NOPE), kr, zeros(tail)], axis=1)
    krp_rot = jnp.concatenate([zeros(MLA_NOPE), kr[:, p_rope], zeros(tail)], axis=1)
    dil = w_in[:, o3:].reshape(d, 3, len(DIL_CONFIGS), DIL_W)
    dil_main = dil.transpose(0, 2, 1, 3).reshape(d, -1)
    dil_rot = dil[:, :2][..., _partner(DIL_W, HEAD_DIM)].transpose(0, 2, 1, 3).reshape(d, -1)
    win = jnp.concatenate([w_in[:, :o2], krp, krp_rot, dil_main, dil_rot], axis=1)

    padq = ((0, 0), (0, 0), (0, tail))
    wuq_rot = jnp.concatenate([w_uq[:, :, :MLA_NOPE], w_uq[:, :, MLA_NOPE:][:, :, p_rope]], axis=-1)
    wuq = jnp.concatenate([jnp.pad(w_uq, padq).reshape(MLA_Q_RANK, -1),
                           jnp.pad(wuq_rot, padq).reshape(MLA_Q_RANK, -1)], axis=1)
    wuk = jnp.pad(w_ukv[:, :, :MLA_NOPE], ((0, 0), (0, 0), (0, LANES - MLA_NOPE))).reshape(MLA_KV_RANK, -1)
    wvt = w_ukv[:, :, MLA_NOPE:].reshape(MLA_KV_RANK, -1).T

    def gains(g):
        rot = jnp.concatenate([g[:MLA_NOPE], g[MLA_NOPE:][p_rope]])
        return jnp.stack([jnp.pad(g, (0, tail)), jnp.pad(rot, (0, tail))]).astype(F32)

    return win.astype(BF16), wuq.astype(BF16), wuk.astype(BF16), wvt.astype(BF16), gains(qn), gains(kn)


def _odd_weights(w_in):
    nqk = DIFF_HEADS * LANES
    nm = MOBA_HEADS * HEAD_DIM
    main = jnp.concatenate([w_in[:, :2 * nqk], w_in[:, 3 * nqk:3 * nqk + 2 * nm]], axis=1)
    win = jnp.concatenate([main, main[:, _partner(main.shape[1], HEAD_DIM)]], axis=1)
    wvt = jnp.concatenate([w_in[:, 2 * nqk:3 * nqk], w_in[:, 3 * nqk + 2 * nm:]], axis=1).T
    return win.astype(BF16), wvt.astype(BF16)


def _gains64(g):
    rot = g[_partner(HEAD_DIM, HEAD_DIM)]
    return jnp.stack([jnp.concatenate([g, g]), jnp.concatenate([rot, rot])]).astype(F32)


def kernel(x, c, positions, ada_w, ada_b, norm_mix, norm_mlp, mlp_w1, mlp_w2, even_w_in, even_w_out, mla_q_lat_norm, mla_kv_lat_norm, mla_w_uq, mla_w_ukv, mla_q_norm, mla_k_norm, dil_q_norm, dil_k_norm, odd_w_in, odd_w_out, diff_q_norm, diff_k_norm, diff_lambda, diff_subln, moba_q_norm, moba_k_norm):
    b, s, d = x.shape
    depth = ada_w.shape[0]
    t_attn, tm_in, tm_out = ATTN_BLOCK, TOKEN_TILE, TOKEN_TILE

    mod = _adaln(c, ada_w, ada_b)
    cm, sm, c64, s64 = _rope_tables(positions)

    for layer in range(depth):
        sh1, sc1, g1, sh2, sc2, g2 = [mod[layer, :, d * t:d * (t + 1)].reshape(b, 1, d) for t in range(6)]
        mlp_args = (sh2, sc2, g2, _row(norm_mlp[layer]), mlp_w1[layer].astype(BF16), mlp_w2[layer].astype(BF16))
        i = layer // 2
        if layer % 2 == 0:
            win, wuq, wuk, wvt, qn, kn = _even_weights(even_w_in[i], mla_w_uq[i], mla_w_ukv[i],
                                                       mla_q_norm[i], mla_k_norm[i])
            q, k, vt, *dils = _even_in(x, sh1, sc1, _row(norm_mix[layer]), win, wuq, wuk, wvt,
                                       _row(mla_q_lat_norm[i]), _row(mla_kv_lat_norm[i]), qn, kn,
                                       _gains64(dil_q_norm[i]), _gains64(dil_k_norm[i]), cm, sm, c64, s64, tm_in, t_attn)
            o_at = _mla_attention(q, k, vt, MLA_HEADS_PER_STEP)
            o_dil, l_dil = zip(*[_sliding_window(dg, SW_BLOCKS_PER_STEP, f"sliding_window_g{g}")
                                 for g, dg in enumerate(dils)])
            x = _even_out(x, g1, o_at, o_dil, l_dil, even_w_out[i].astype(BF16), mlp_args, tm_out)
        else:
            lam_init = 0.8 - 0.6 * math.exp(-0.3 * layer)
            win, wvt = _odd_weights(odd_w_in[i])
            qc, kc, qm, qmf, km, vct, vmt, kmean = _odd_in(
                x, sh1, sc1, _row(norm_mix[layer]), win, wvt,
                _gains64(diff_q_norm[i]), _gains64(diff_k_norm[i]), _gains64(moba_q_norm[i]),
                _gains64(moba_k_norm[i]),
                c64, s64, tm_in, t_attn)
            o_ct = _diff_attention(qc, kc, vct, diff_lambda[i].astype(F32),
                                   diff_subln[i].reshape(-1, 1).astype(F32), lam_init, DIFF_HEADS_PER_STEP)
            o_dt = _moba_attention(qm, qmf, km, vmt, kmean.reshape(b, s // MOBA_BLOCK, kmean.shape[-1]),
                                   MOBA_PAIRS_PER_STEP)
            x = _odd_out(x, g1, o_ct, o_dt, odd_w_out[i].astype(BF16), mlp_args, tm_out)
    return x
```

```python
import functools
import math

import jax
import jax.numpy as jnp
import numpy as np
from jax import lax
from jax.experimental import pallas as pl
from jax.experimental.pallas import tpu as pltpu

F32 = jnp.float32
BF16 = jnp.bfloat16

LANES = 128
HEAD_DIM = 64
ROPE_THETA = 10000.0
NORM_EPS = 1e-6
NEG_INF = -1e30
LOG2E = math.log2(math.e)

MLA_HEADS = 8
MLA_Q_RANK = 384
MLA_KV_RANK = 256
MLA_NOPE = 64
MLA_ROPE = 32
MLA_QK = MLA_NOPE + MLA_ROPE
DIL_CONFIGS = ((128, 1), (512, 4), (2048, 16))
DIL_HEADS = 4
DIL_W = DIL_HEADS * HEAD_DIM
DIFF_HEADS = 4
MOBA_HEADS = 8
MOBA_BLOCK = 256
MOBA_TOPK = 3
SW_BLOCK = 128
ONES_ROWS = 16

VMEM_LIMIT = 56 * 1024 * 1024

ATTN_BLOCK = MOBA_BLOCK
TOKEN_TILE = 2 * ATTN_BLOCK
MLP_CHUNK = 1024
ADALN_TILE = 1024
SW_BLOCKS_PER_STEP = 4
MLA_HEADS_PER_STEP = 4
DIFF_HEADS_PER_STEP = 2
MOBA_PAIRS_PER_STEP = 2


def _cparams(sem):
    return pltpu.CompilerParams(dimension_semantics=sem, vmem_limit_bytes=VMEM_LIMIT)


def _nt_dot(a, b):
    return lax.dot_general(a, b, (((1,), (1,)), ((), ())), preferred_element_type=F32)


def _tn_dot(a, b):
    return lax.dot_general(a, b, (((0,), (0,)), ((), ())), preferred_element_type=F32)


def _rms(x, w):
    return x * lax.rsqrt(jnp.mean(x * x, axis=-1, keepdims=True) + NORM_EPS) * w


def _lane_iota(shape):
    return lax.broadcasted_iota(jnp.int32, shape, len(shape) - 1)


def _adaln_kernel(c_ref, w_ref, b_ref, o_ref):
    c = c_ref[...]
    cond = c * (1.0 / (1.0 + jnp.exp(-c)))
    o_ref[0] = jnp.dot(cond, w_ref[0], preferred_element_type=F32,
                       precision=lax.Precision.HIGHEST) + b_ref[0]


def _adaln(c, ada_w, ada_b):
    depth, d, n = ada_w.shape
    b = c.shape[0]
    tn = ADALN_TILE
    return pl.pallas_call(
        _adaln_kernel,
        grid=(depth, n // tn),
        in_specs=[pl.BlockSpec((b, d), lambda l, j: (0, 0)),
                  pl.BlockSpec((1, d, tn), lambda l, j: (l, 0, j)),
                  pl.BlockSpec((1, 1, tn), lambda l, j: (l, 0, j))],
        out_specs=pl.BlockSpec((1, b, tn), lambda l, j: (l, 0, j)),
        out_shape=jax.ShapeDtypeStruct((depth, b, n), F32),
        compiler_params=_cparams(("parallel", "parallel")),
        name="adaln",
    )(c, ada_w, ada_b.reshape(depth, 1, n))


def _inv_rms64(x):
    lane = _lane_iota((1, LANES))
    lo = lane < HEAD_DIM
    sq = x * x
    s_lo = jnp.sum(jnp.where(lo, sq, 0.0), axis=-1, keepdims=True)
    s_hi = jnp.sum(jnp.where(lo, 0.0, sq), axis=-1, keepdims=True)
    return lax.rsqrt(jnp.where(lo, s_lo, s_hi) * (1.0 / HEAD_DIM) + NORM_EPS)


def _store_t_blocks(out_ref, xt):
    t = out_ref.shape[-1]
    for j in range(out_ref.shape[1]):
        out_ref[0, j] = xt[:, t * j:t * (j + 1)]


def _t_blocks(b, s, rows, tm, t):
    return (pl.BlockSpec((1, tm // t, rows, t), lambda bi, i: (bi, i, 0, 0)),
            jax.ShapeDtypeStruct((b, s // t, rows, t), BF16))


def _even_in_kernel(x_ref, sh_ref, sc_ref, nw_ref, win_ref, wuq_ref, wuk_ref, wvt_ref,
                    qlat_ref, kvlat_ref, qn_ref, kn_ref, dqn_ref, dkn_ref,
                    cm_ref, sm_ref, c64_ref, s64_ref,
                    q_out, k_out, vt_out, d0_out, d1_out, d2_out, dscr):
    x = x_ref[0]
    tm = x.shape[0]
    h = _rms(x, nw_ref[...]) * (1.0 + sc_ref[0]) + sh_ref[0]
    u = jnp.dot(h.astype(BF16), win_ref[...], preferred_element_type=F32)

    o1 = MLA_Q_RANK
    o2 = o1 + MLA_KV_RANK
    o_kr, o_krr, o_dil = o2, o2 + LANES, o2 + 2 * LANES
    n_dil = 3 * DIL_W * len(DIL_CONFIGS)
    o_rot = o_dil + n_dil
    nq = MLA_HEADS * LANES
    cqn = _rms(u[:, :o1], qlat_ref[...]).astype(BF16)
    qp = jnp.dot(cqn, wuq_ref[...], preferred_element_type=F32)
    ckvn = _rms(u[:, o1:o2], kvlat_ref[...]).astype(BF16)
    kvp = jnp.dot(ckvn, wuk_ref[...], preferred_element_type=F32)
    _store_t_blocks(vt_out, _nt_dot(wvt_ref[...], ckvn).astype(BF16))
    kr, kr_rot = u[:, o_kr:o_kr + LANES], u[:, o_krr:o_krr + LANES]

    cm, sm = cm_ref[0], sm_ref[0]
    q_scale = MLA_QK ** -0.5 * LOG2E
    qgc, qgs = qn_ref[0:1] * cm * q_scale, qn_ref[1:2] * sm * q_scale
    kgc, kgs = kn_ref[0:1] * cm, kn_ref[1:2] * sm
    kr_term = kr_rot * kgs
    for hd in range(MLA_HEADS):
        sl = slice(LANES * hd, LANES * (hd + 1))
        qh = qp[:, sl]
        inv = lax.rsqrt(jnp.sum(qh * qh, -1, keepdims=True) * (1.0 / MLA_QK) + NORM_EPS)
        q_out[0, :, sl] = (inv * (qh * qgc + qp[:, nq + LANES * hd:nq + LANES * (hd + 1)] * qgs)).astype(BF16)
        kh = kvp[:, sl] + kr
        inv = lax.rsqrt(jnp.sum(kh * kh, -1, keepdims=True) * (1.0 / MLA_QK) + NORM_EPS)
        k_out[0, :, sl] = (inv * (kh * kgc + kr_term)).astype(BF16)

    c64, s64 = c64_ref[0], s64_ref[0]
    d_scale = HEAD_DIM ** -0.5 * LOG2E
    dqc, dqs = dqn_ref[0:1] * c64 * d_scale, dqn_ref[1:2] * s64 * d_scale
    dkc, dks = dkn_ref[0:1] * c64, dkn_ref[1:2] * s64
    ncol = DIL_W // LANES
    for g, d_out in enumerate((d0_out, d1_out, d2_out)):
        _, r = DIL_CONFIGS[g]
        base = o_dil + 3 * DIL_W * g
        rbase = o_rot + 2 * DIL_W * g
        for j in range(3 * ncol):
            xc = u[:, base + LANES * j:base + LANES * (j + 1)]
            if j < 2 * ncol:
                xr = u[:, rbase + LANES * j:rbase + LANES * (j + 1)]
                gc, gs = (dqc, dqs) if j < ncol else (dkc, dks)
                xc = _inv_rms64(xc) * (xc * gc + xr * gs)
            if r == 1:
                d_out[0, :, LANES * j:LANES * (j + 1)] = xc.astype(BF16)
            else:
                dscr[j] = xc
                for c in range(r):
                    d_out[0, c, :, LANES * j:LANES * (j + 1)] = dscr[j, pl.ds(c, tm // r, stride=r), :].astype(BF16)


def _row(v):
    return v.reshape(1, -1).astype(F32)


def _even_in(x, sh, sc, nw, win, wuq, wuk, wvt, qlat, kvlat, qn, kn, dqn, dkn, cm, sm, c64, s64, tm, t):
    b, s, d = x.shape
    tok = lambda w: pl.BlockSpec((1, tm, w), lambda bi, i: (bi, i, 0))
    per_b = pl.BlockSpec((1, 1, d), lambda bi, i: (bi, 0, 0))
    full = lambda a: pl.BlockSpec(a.shape, lambda bi, i: (0,) * a.ndim, pipeline_mode=pl.Buffered(1))
    vt_spec, vt_shape = _t_blocks(b, s, wvt.shape[0], tm, t)
    dil_specs, dil_shapes = [], []
    for _, r in DIL_CONFIGS:
        if r == 1:
            dil_specs.append(tok(3 * DIL_W))
            dil_shapes.append(jax.ShapeDtypeStruct((b, s, 3 * DIL_W), BF16))
        else:
            dil_specs.append(pl.BlockSpec((1, r, tm // r, 3 * DIL_W), lambda bi, i: (bi, 0, i, 0)))
            dil_shapes.append(jax.ShapeDtypeStruct((b, r, s // r, 3 * DIL_W), BF16))
    return pl.pallas_call(
        _even_in_kernel,
        grid=(b, s // tm),
        in_specs=[tok(d), per_b, per_b, full(nw), full(win), full(wuq), full(wuk), full(wvt),
                  full(qlat), full(kvlat), full(qn), full(kn), full(dqn), full(dkn),
                  tok(LANES), tok(LANES), tok(LANES), tok(LANES)],
        out_specs=[tok(MLA_HEADS * LANES), tok(MLA_HEADS * LANES), vt_spec] + dil_specs,
        out_shape=[jax.ShapeDtypeStruct((b, s, MLA_HEADS * LANES), BF16),
                   jax.ShapeDtypeStruct((b, s, MLA_HEADS * LANES), BF16), vt_shape] + dil_shapes,
        scratch_shapes=[pltpu.VMEM((3 * DIL_W // LANES, tm, LANES), F32)],
        compiler_params=_cparams(("parallel", "arbitrary")),
        name="even_in_proj",
    )(x, sh, sc, nw, win, wuq, wuk, wvt, qlat, kvlat, qn, kn, dqn, dkn, cm, sm, c64, s64)


def _odd_in_kernel(x_ref, sh_ref, sc_ref, nw_ref, win_ref, wvt_ref, dqn_ref, dkn_ref, mqn_ref, mkn_ref,
                   c64_ref, s64_ref,
                   qc_out, kc_out, qm_out, qmf_out, km_out, vct_out, vmt_out, kmean_out):
    x = x_ref[0]
    tm = x.shape[0]
    h = (_rms(x, nw_ref[...]) * (1.0 + sc_ref[0]) + sh_ref[0]).astype(BF16)
    u = jnp.dot(h, win_ref[...], preferred_element_type=F32)
    vt = _nt_dot(wvt_ref[...], h).astype(BF16)
    nvc = vct_out.shape[2]
    _store_t_blocks(vct_out, vt[:nvc])
    _store_t_blocks(vmt_out, vt[nvc:])
    c64, s64 = c64_ref[0], s64_ref[0]
    scale = HEAD_DIM ** -0.5 * LOG2E
    nqk = DIFF_HEADS * LANES
    nm = MOBA_HEADS * HEAD_DIM
    rot = 2 * nqk + 2 * nm

    def tables(gain_ref, mult):
        return gain_ref[0:1] * c64 * mult, gain_ref[1:2] * s64 * mult

    def roped(off, col, tab):
        a = off + LANES * col
        xc, xr = u[:, a:a + LANES], u[:, rot + a:rot + a + LANES]
        return _inv_rms64(xc) * (xc * tab[0] + xr * tab[1])

    t_qc, t_kc, t_qm, t_km = tables(dqn_ref, scale), tables(dkn_ref, 1.0), tables(mqn_ref, 1.0), tables(mkn_ref, 1.0)
    for col in range(nqk // LANES):
        sl = slice(LANES * col, LANES * (col + 1))
        qc_out[0, :, sl] = roped(0, col, t_qc).astype(BF16)
        kc_out[0, :, sl] = roped(nqk, col, t_kc).astype(BF16)

    base = 2 * nqk
    for col in range(nm // LANES):
        sl = slice(LANES * col, LANES * (col + 1))
        qm = roped(base, col, t_qm)
        qmf_out[0, :, sl] = qm
        qm_out[0, :, sl] = (qm * scale).astype(BF16)
        km = roped(base + nm, col, t_km)
        km_out[0, :, sl] = km.astype(BF16)
        for blk in range(tm // MOBA_BLOCK):
            rows = slice(MOBA_BLOCK * blk, MOBA_BLOCK * (blk + 1))
            kmean_out[0, 0, blk:blk + 1, sl] = jnp.mean(km[rows], axis=0, keepdims=True)


def _odd_in(x, sh, sc, nw, win, wvt, dqn, dkn, mqn, mkn, c64, s64, tm, t):
    b, s, d = x.shape
    tok = lambda w: pl.BlockSpec((1, tm, w), lambda bi, i: (bi, i, 0))
    per_b = pl.BlockSpec((1, 1, d), lambda bi, i: (bi, 0, 0))
    full = lambda a: pl.BlockSpec(a.shape, lambda bi, i: (0,) * a.ndim, pipeline_mode=pl.Buffered(1))
    w = DIFF_HEADS * LANES
    nblk = tm // MOBA_BLOCK
    shp = lambda dt: jax.ShapeDtypeStruct((b, s, w), dt)
    vt_spec, vt_shape = _t_blocks(b, s, wvt.shape[0] // 2, tm, t)
    return pl.pallas_call(
        _odd_in_kernel,
        grid=(b, s // tm),
        in_specs=[tok(d), per_b, per_b, full(nw), full(win), full(wvt), full(dqn), full(dkn), full(mqn), full(mkn),
                  tok(LANES), tok(LANES)],
        out_specs=[tok(w)] * 5 + [vt_spec, vt_spec, pl.BlockSpec((1, 1, nblk, w), lambda bi, i: (bi, i, 0, 0))],
        out_shape=[shp(BF16), shp(BF16), shp(BF16), shp(F32), shp(BF16), vt_shape, vt_shape,
                   jax.ShapeDtypeStruct((b, s // tm, nblk, w), F32)],
        compiler_params=_cparams(("parallel", "arbitrary")),
        name="odd_in_proj",
    )(x, sh, sc, nw, win, wvt, dqn, dkn, mqn, mkn, c64, s64)


def _q_tiles_t(q, split_halves):
    qt = q.astype(F32).T
    if not split_halves:
        return [qt.astype(BF16)]
    row = lax.broadcasted_iota(jnp.int32, qt.shape, 0)
    return [jnp.where((row >= HEAD_DIM) == bool(half), qt, 0.0).astype(BF16) for half in range(2)]


def _flash_t(qts, k_ref, vt_ref, k_slices, v_rows, scr, i, t, past_mask=None):
    m_scr, acc_scr, s_scr, p_scr, a_scr, b_scr = scr
    n = len(qts)
    ones = jnp.ones((ONES_ROWS, t), BF16)

    def qk(blk, s):
        b0 = pl.multiple_of(blk * t, t)
        return jnp.dot(k_ref[0, pl.ds(b0, t), k_slices[s]], qts[s], preferred_element_type=F32)

    def pv(blk, s, p):
        vt = jnp.concatenate([vt_ref[0, blk, v_rows[s], :], ones], axis=0)
        return jnp.dot(vt, p, preferred_element_type=F32)

    def store_scores(slot, s, st):
        s_scr[slot, s] = st
        b_scr[slot, s] = jnp.max(st, axis=0, keepdims=True)

    krow = lax.broadcasted_iota(jnp.int32, (t, t), 0)
    qcol = lax.broadcasted_iota(jnp.int32, (t, t), 1)
    causal = krow <= qcol
    last_past = jnp.maximum(i - 1, 0)
    for s in range(n):
        st = jnp.where(causal, qk(i, s), NEG_INF)
        m = jnp.max(st, axis=0, keepdims=True)
        m_scr[s] = m
        p_scr[0, s] = jnp.exp2(st - m).astype(BF16)
        p_scr[1, s] = jnp.zeros((t, t), BF16)
        a_scr[0, s] = jnp.ones_like(m)
        a_scr[1, s] = jnp.ones_like(m)
        acc_scr[s] = jnp.zeros(acc_scr.shape[1:], F32)
        store_scores(0, s, qk(0, s))
        store_scores(1, s, qk(jnp.minimum(1, last_past), s))

    def softmax_stage(slot, valid, rs, ws, blk):
        for s in range(n):
            cmax = b_scr[2 * rs + slot, s]
            keep = None if past_mask is None else past_mask(s, blk)
            if keep is not None:
                cmax = jnp.where(keep, cmax, NEG_INF)
            if valid is not None:
                cmax = jnp.where(valid, cmax, NEG_INF)
            m_prev = m_scr[s]
            m_new = jnp.maximum(m_prev, cmax)
            a_scr[2 * ws + slot, s] = jnp.exp2(m_prev - m_new)
            p = jnp.exp2(s_scr[2 * rs + slot, s] - m_new).astype(BF16)
            if keep is not None:
                p = jnp.where(keep, p, jnp.zeros_like(p))
            p_scr[2 * ws + slot, s] = p
            m_scr[s] = m_new

    def body(kk, rs, ws):
        first = kk == 0
        pa_blk = jnp.where(first, i, 2 * kk - 2)
        pb_blk = jnp.where(first, i, 2 * kk - 1)
        p_prev = [[p_scr[2 * rs + sl, s] for s in range(n)] for sl in range(2)]
        a_prev = [[a_scr[2 * rs + sl, s] for s in range(n)] for sl in range(2)]
        s_next = [[qk(jnp.minimum(2 * kk + 2 + sl, last_past), s) for s in range(n)] for sl in range(2)]
        pvs = [[pv(blk, s, p_prev[sl][s]) for s in range(n)] for sl, blk in enumerate((pa_blk, pb_blk))]
        softmax_stage(0, None, rs, ws, 2 * kk)
        softmax_stage(1, 2 * kk + 1 < i, rs, ws, jnp.minimum(2 * kk + 1, last_past))
        for sl in range(2):
            for s in range(n):
                store_scores(2 * ws + sl, s, s_next[sl][s])
        for s in range(n):
            acc_scr[s] = a_prev[1][s] * (a_prev[0][s] * acc_scr[s] + pvs[0][s]) + pvs[1][s]

    def two_bodies(kp, carry):
        body(2 * kp, 0, 1)
        body(2 * kp + 1, 1, 0)
        return carry

    nbody = (i + 1) // 2
    lax.fori_loop(0, nbody // 2, two_bodies, 0)

    @pl.when(nbody % 2 == 1)
    def _():
        body(nbody - 1, 0, 1)

    fs = nbody % 2
    kl = jnp.maximum(nbody - 1, 0)
    none = i == 0
    fa_blk = jnp.where(none, i, 2 * kl)
    fb_blk = jnp.where(none, i, jnp.minimum(2 * kl + 1, last_past))
    fb_scale = jnp.where(jnp.logical_or(none, i % 2 == 0), 1.0, 0.0)
    for s in range(n):
        acc_scr[s] = (a_scr[2 * fs + 1, s] * (a_scr[2 * fs, s] * acc_scr[s] + pv(fa_blk, s, p_scr[2 * fs, s]))
                      + fb_scale * pv(fb_blk, s, p_scr[2 * fs + 1, s]))


def _flash_scratch(n, nv, t):
    return [pltpu.VMEM((n, 1, t), F32), pltpu.VMEM((n, nv + ONES_ROWS, t), F32), pltpu.VMEM((4, n, t, t), F32),
            pltpu.VMEM((4, n, t, t), BF16), pltpu.VMEM((4, n, 1, t), F32), pltpu.VMEM((4, n, 1, t), F32)]


def _normalised(acc_scr, s, nv):
    return acc_scr[s, :nv, :] / acc_scr[s, nv:nv + 1, :]


def _half_mask(x, half):
    lane = _lane_iota((1, LANES))
    return jnp.where((lane >= HEAD_DIM) == bool(half), x, jnp.zeros_like(x))


def _mla_kernel(q_ref, k_ref, vt_ref, ot_ref, *scr, t, nh):
    i = pl.program_id(2)
    qs = [_q_tiles_t(q_ref[0, :, LANES * s:LANES * (s + 1)], False)[0] for s in range(nh)]
    ks = [slice(LANES * s, LANES * (s + 1)) for s in range(nh)]
    vr = [slice(HEAD_DIM * s, HEAD_DIM * (s + 1)) for s in range(nh)]
    _flash_t(qs, k_ref, vt_ref, ks, vr, scr, i, t)
    for s in range(nh):
        ot_ref[0, 0, vr[s], :] = _normalised(scr[1], s, HEAD_DIM).astype(ot_ref.dtype)


def _mla_attention(q, k, vt, nh):
    b, s, _ = q.shape
    _, nb, nv, t = vt.shape
    return pl.pallas_call(
        functools.partial(_mla_kernel, t=t, nh=nh),
        grid=(b, MLA_HEADS // nh, nb),
        in_specs=[pl.BlockSpec((1, t, nh * LANES), lambda bi, p, i: (bi, i, p)),
                  pl.BlockSpec((1, s, nh * LANES), lambda bi, p, i: (bi, 0, p)),
                  pl.BlockSpec((1, nb, nh * HEAD_DIM, t), lambda bi, p, i: (bi, 0, p, 0))],
        out_specs=pl.BlockSpec((1, 1, nh * HEAD_DIM, t), lambda bi, p, i: (bi, i, p, 0)),
        out_shape=jax.ShapeDtypeStruct((b, nb, nv, t), BF16),
        scratch_shapes=_flash_scratch(nh, HEAD_DIM, t),
        compiler_params=_cparams(("parallel", "parallel", "arbitrary")),
        name="mla_attention",
    )(q, k, vt)


def _diff_kernel(q_ref, k_ref, vt_ref, lam_ref, sub_ref, ot_ref, *scr, t, nh, lam_init):
    i = pl.program_id(2)
    qs, ks, vr = [], [], []
    for hd in range(nh):
        cols = slice(LANES * hd, LANES * (hd + 1))
        qs += _q_tiles_t(q_ref[0, :, cols], True)
        ks += [cols, cols]
        vr += [cols, cols]
    _flash_t(qs, k_ref, vt_ref, ks, vr, scr, i, t)
    lv = lam_ref[...]
    lam = (jnp.exp(jnp.sum(lv[0:1] * lv[1:2], keepdims=True))
           - jnp.exp(jnp.sum(lv[2:3] * lv[3:4], keepdims=True)) + lam_init)
    for hd in range(nh):
        o = _normalised(scr[1], 2 * hd, LANES) - lam * _normalised(scr[1], 2 * hd + 1, LANES)
        o = o * lax.rsqrt(jnp.mean(o * o, axis=0, keepdims=True) + NORM_EPS) * sub_ref[...]
        ot_ref[0, 0, LANES * hd:LANES * (hd + 1), :] = (o * (1.0 - lam_init)).astype(ot_ref.dtype)


def _diff_attention(q, k, vt, lam_rows, subln_col, lam_init, nh):
    b, s, w = q.shape
    _, nb, _, t = vt.shape
    return pl.pallas_call(
        functools.partial(_diff_kernel, t=t, nh=nh, lam_init=lam_init),
        grid=(b, DIFF_HEADS // nh, nb),
        in_specs=[pl.BlockSpec((1, t, nh * LANES), lambda bi, p, i: (bi, i, p)),
                  pl.BlockSpec((1, s, nh * LANES), lambda bi, p, i: (bi, 0, p)),
                  pl.BlockSpec((1, nb, nh * LANES, t), lambda bi, p, i: (bi, 0, p, 0)),
                  pl.BlockSpec(lam_rows.shape, lambda bi, p, i: (0, 0)),
                  pl.BlockSpec(subln_col.shape, lambda bi, p, i: (0, 0))],
        out_specs=pl.BlockSpec((1, 1, nh * LANES, t), lambda bi, p, i: (bi, i, p, 0)),
        out_shape=jax.ShapeDtypeStruct((b, nb, w, t), BF16),
        scratch_shapes=_flash_scratch(2 * nh, LANES, t),
        compiler_params=_cparams(("parallel", "parallel", "arbitrary")),
        name="diff_attention",
    )(q, k, vt, lam_rows, subln_col)


def _moba_kernel(q_ref, qf_ref, k_ref, vt_ref, kmean_ref, ot_ref, sel_scr, *scr, t, npair):
    i = pl.program_id(2)
    nb = kmean_ref.shape[1]
    brow = lax.broadcasted_iota(jnp.int32, (nb, t), 0).astype(F32)
    past = brow < i.astype(F32)
    qs, ks, vr = [], [], []
    for pr in range(npair):
        cols = slice(LANES * pr, LANES * (pr + 1))
        qf, kmean = qf_ref[0, :, cols], kmean_ref[0, :, cols]
        qs += _q_tiles_t(q_ref[0, :, cols], True)
        for half in range(2):
            s = 2 * pr + half
            gate = lax.dot_general(kmean, _half_mask(qf, half), (((1,), (1,)), ((), ())),
                                   preferred_element_type=F32, precision=lax.Precision.HIGHEST)
            gate = jnp.where(past, gate, NEG_INF)
            sel = jnp.zeros((nb, t), F32)
            for _ in range(MOBA_TOPK):
                top = jnp.max(gate, axis=0, keepdims=True)
                first = jnp.min(jnp.where(gate == top, brow, float(nb)), axis=0, keepdims=True)
                pick = brow == first
                sel = jnp.where(pick, 1.0, sel)
                gate = jnp.where(pick, 2 * NEG_INF, gate)
            sel_scr[s] = jnp.where(past, sel, 0.0)
            ks.append(cols)
            vr.append(slice(HEAD_DIM * s, HEAD_DIM * (s + 1)))
    _flash_t(qs, k_ref, vt_ref, ks, vr, scr, i, t,
             past_mask=lambda s, jj: sel_scr[s, pl.ds(jj, 1), :] > 0.5)
    for s in range(2 * npair):
        ot_ref[0, 0, vr[s], :] = _normalised(scr[1], s, HEAD_DIM).astype(ot_ref.dtype)


def _moba_attention(q, qf, k, vt, kmean, npair):
    b, s, w = q.shape
    _, nb, _, t = vt.shape
    nkm = kmean.shape[1]
    lanes = npair * LANES
    return pl.pallas_call(
        functools.partial(_moba_kernel, t=t, npair=npair),
        grid=(b, w // lanes, nb),
        in_specs=[pl.BlockSpec((1, t, lanes), lambda bi, p, i: (bi, i, p)),
                  pl.BlockSpec((1, t, lanes), lambda bi, p, i: (bi, i, p)),
                  pl.BlockSpec((1, s, lanes), lambda bi, p, i: (bi, 0, p)),
                  pl.BlockSpec((1, nb, lanes, t), lambda bi, p, i: (bi, 0, p, 0)),
                  pl.BlockSpec((1, nkm, lanes), lambda bi, p, i: (bi, 0, p))],
        out_specs=pl.BlockSpec((1, 1, lanes, t), lambda bi, p, i: (bi, i, p, 0)),
        out_shape=jax.ShapeDtypeStruct((b, nb, w, t), BF16),
        scratch_shapes=[pltpu.VMEM((2 * npair, nkm, t), F32)] + _flash_scratch(2 * npair, HEAD_DIM, t),
        compiler_params=_cparams(("parallel", "parallel", "arbitrary")),
        name="moba_attention",
    )(q, qf, k, vt, kmean)


def _sw_kernel(q_ref, kc_ref, vc_ref, kp_ref, vp_ref, o_ref, l_ref, *, nbk):
    i = pl.program_id(1)
    blk = SW_BLOCK
    qi = lax.broadcasted_iota(jnp.int32, (blk, 2 * blk), 0)
    kj = lax.broadcasted_iota(jnp.int32, (blk, 2 * blk), 1)
    band = (kj >= qi) & (kj <= qi + blk)
    lane = _lane_iota((1, LANES))
    heads = [(pair, half) for pair in range(DIL_W // LANES) for half in range(2)]

    def scores(n):
        rows = slice(blk * n, blk * (n + 1))
        q = q_ref[0, rows, :]
        if n == 0:
            kprev, valid = kp_ref[0], band & ((kj >= blk) | (i > 0))
        else:
            kprev, valid = kc_ref[0, blk * (n - 1):blk * n, :], band
        kk = jnp.concatenate([kprev, kc_ref[0, rows, :]], axis=0)
        out = []
        for pair, half in heads:
            sl = slice(LANES * pair, LANES * (pair + 1))
            out.append(jnp.where(valid, _nt_dot(_half_mask(q[:, sl], half), kk[:, sl]), NEG_INF))
        return out

    nxt = scores(0)
    for n in range(nbk):
        cur, rows = nxt, slice(blk * n, blk * (n + 1))
        if n + 1 < nbk:
            nxt = scores(n + 1)
        vprev = vp_ref[0] if n == 0 else vc_ref[0, blk * (n - 1):blk * n, :]
        vv = jnp.concatenate([vprev, vc_ref[0, rows, :]], axis=0)
        outs, lses = [], []
        for (pair, half), sc in zip(heads, cur):
            m = jnp.max(sc, axis=-1, keepdims=True)
            p = jnp.exp2(sc - m)
            l = jnp.sum(p, axis=-1, keepdims=True)
            sl = slice(LANES * pair, LANES * (pair + 1))
            outs.append(jnp.dot(p.astype(BF16), vv[:, sl], preferred_element_type=F32) / l)
            lses.append(m + jnp.log2(l))
        for pair in range(DIL_W // LANES):
            sl = slice(LANES * pair, LANES * (pair + 1))
            o_ref[0, rows, sl] = jnp.where(lane < HEAD_DIM, outs[2 * pair], outs[2 * pair + 1])
            l_ref[0, rows, sl] = jnp.where(lane < HEAD_DIM, lses[2 * pair], lses[2 * pair + 1])


def _sliding_window(dil, nbk, name):
    lead, (m, w3) = dil.shape[:-2], dil.shape[-2:]
    seqs = dil.reshape((-1, m, w3))
    nbk = min(nbk, m // SW_BLOCK)
    rows = SW_BLOCK * nbk
    cur = lambda off: pl.BlockSpec((1, rows, DIL_W), lambda n, i: (n, i, off))
    prev = lambda off: pl.BlockSpec((1, SW_BLOCK, DIL_W), lambda n, i: (n, jnp.maximum(i * nbk - 1, 0), off))
    out_spec = pl.BlockSpec((1, rows, DIL_W), lambda n, i: (n, i, 0))
    o, l = pl.pallas_call(
        functools.partial(_sw_kernel, nbk=nbk),
        grid=(seqs.shape[0], m // rows),
        in_specs=[cur(0), cur(1), cur(2), prev(1), prev(2)],
        out_specs=[out_spec, out_spec],
        out_shape=[jax.ShapeDtypeStruct((seqs.shape[0], m, DIL_W), F32)] * 2,
        compiler_params=_cparams(("parallel", "arbitrary")),
        name=name,
    )(seqs, seqs, seqs, seqs, seqs)
    return o.reshape(lead + (m, DIL_W)), l.reshape(lead + (m, DIL_W))


def _mlp_tail(y, mlp_refs):
    sh_ref, sc_ref, g_ref, nw_ref, w1_ref, w2_ref = mlp_refs
    h = (_rms(y, nw_ref[...]) * (1.0 + sc_ref[0]) + sh_ref[0]).astype(BF16)
    acc = jnp.zeros(y.shape, F32)
    for c in range(w1_ref.shape[1] // MLP_CHUNK):
        cols = slice(MLP_CHUNK * c, MLP_CHUNK * (c + 1))
        a = jnp.maximum(jnp.dot(h, w1_ref[:, cols], preferred_element_type=F32), 0.0)
        acc = acc + jnp.dot((a * a).astype(BF16), w2_ref[cols, :], preferred_element_type=F32)
    return y + g_ref[0] * acc


def _proj_t(ot_ref, w_rows):
    return jnp.concatenate([_tn_dot(ot_ref[0, j], w_rows) for j in range(ot_ref.shape[1])], axis=0)


def _mlp_specs(mlp_args, d):
    per_b = pl.BlockSpec((1, 1, d), lambda bi, i: (bi, 0, 0))
    const = lambda a: pl.BlockSpec(a.shape, lambda bi, i: (0, 0), pipeline_mode=pl.Buffered(1))
    sh, sc, g2, nw, w1, w2 = mlp_args
    return [per_b, per_b, per_b, pl.BlockSpec(nw.shape, lambda bi, i: (0, 0)), const(w1), const(w2)]


def _even_out_kernel(x_ref, g_ref, oat_ref, o0_ref, o1_ref, o2_ref, l0_ref, l1_ref, l2_ref, w_ref,
                     *rest):
    *mlp_refs, y_ref, tscr = rest
    tm = x_ref.shape[1]
    ncol = DIL_W // LANES

    def token_major(ref, slot):
        r = ref.shape[1]
        for c in range(r):
            for j in range(ncol):
                tscr[slot * ncol + j, pl.ds(c, tm // r, stride=r), :] = ref[0, c, :, LANES * j:LANES * (j + 1)]
        return jnp.concatenate([tscr[slot * ncol + j] for j in range(ncol)], axis=1)

    o0, l0 = o0_ref[0], l0_ref[0]
    o1, l1 = token_major(o1_ref, 0), token_major(l1_ref, 1)
    o2, l2 = token_major(o2_ref, 2), token_major(l2_ref, 3)
    top = jnp.maximum(jnp.maximum(l0, l1), l2)
    w0, w1, w2 = jnp.exp2(l0 - top), jnp.exp2(l1 - top), jnp.exp2(l2 - top)
    ob = (w0 * o0 + w1 * o1 + w2 * o2) / (w0 + w1 + w2)
    na = oat_ref.shape[2]
    y = _proj_t(oat_ref, w_ref[:na, :]) + jnp.dot(ob.astype(BF16), w_ref[na:, :], preferred_element_type=F32)
    y_ref[0] = _mlp_tail(x_ref[0] + g_ref[0] * y, mlp_refs)


def _t_spec(a, tm):
    return pl.BlockSpec((1, tm // a.shape[-1]) + a.shape[2:], lambda bi, i: (bi, i, 0, 0))


def _even_out(x, g1, oat, o_dil, l_dil, w_out, mlp_args, tm):
    b, s, d = x.shape
    tok = lambda w: pl.BlockSpec((1, tm, w), lambda bi, i: (bi, i, 0))

    def dil_spec(a):
        if a.ndim == 3:
            return tok(DIL_W)
        r = a.shape[1]
        return pl.BlockSpec((1, r, tm // r, DIL_W), lambda bi, i: (bi, 0, i, 0))

    return pl.pallas_call(
        _even_out_kernel,
        grid=(b, s // tm),
        in_specs=[tok(d), pl.BlockSpec((1, 1, d), lambda bi, i: (bi, 0, 0)), _t_spec(oat, tm)]
                 + [dil_spec(a) for a in (*o_dil, *l_dil)]
                 + [pl.BlockSpec(w_out.shape, lambda bi, i: (0, 0), pipeline_mode=pl.Buffered(1))]
                 + _mlp_specs(mlp_args, d),
        out_specs=tok(d),
        out_shape=jax.ShapeDtypeStruct((b, s, d), F32),
        scratch_shapes=[pltpu.VMEM((4 * DIL_W // LANES, tm, LANES), F32)],
        compiler_params=_cparams(("parallel", "arbitrary")),
        name="even_out_mlp",
    )(x, g1, oat, *o_dil, *l_dil, w_out, *mlp_args)


def _odd_out_kernel(x_ref, g_ref, oct_ref, odt_ref, w_ref, *rest):
    *mlp_refs, y_ref = rest
    nc = oct_ref.shape[2]
    y = _proj_t(oct_ref, w_ref[:nc, :]) + _proj_t(odt_ref, w_ref[nc:, :])
    y_ref[0] = _mlp_tail(x_ref[0] + g_ref[0] * y, mlp_refs)


def _odd_out(x, g1, oct, odt, w_out, mlp_args, tm):
    b, s, d = x.shape
    tok = lambda w: pl.BlockSpec((1, tm, w), lambda bi, i: (bi, i, 0))
    return pl.pallas_call(
        _odd_out_kernel,
        grid=(b, s // tm),
        in_specs=[tok(d), pl.BlockSpec((1, 1, d), lambda bi, i: (bi, 0, 0)), _t_spec(oct, tm), _t_spec(odt, tm),
                  pl.BlockSpec(w_out.shape, lambda bi, i: (0, 0), pipeline_mode=pl.Buffered(1))]
                 + _mlp_specs(mlp_args, d),
        out_specs=tok(d),
        out_shape=jax.ShapeDtypeStruct((b, s, d), F32),
        compiler_params=_cparams(("parallel", "arbitrary")),
        name="odd_out_mlp",
    )(x, g1, oct, odt, w_out, *mlp_args)


def _rope_tables(positions):
    pos = positions.astype(F32)[:, :, None]
    half = HEAD_DIM // 2
    inv = ROPE_THETA ** (-jnp.arange(half, dtype=F32) / half)
    ang = pos * inv[_freq_order()]
    c, s = jnp.cos(ang), jnp.sin(ang)
    c64 = jnp.concatenate([c, c, c, c], axis=-1)
    s64 = jnp.concatenate([-s, s, -s, s], axis=-1)
    c, s = c[..., :MLA_ROPE // 2], s[..., :MLA_ROPE // 2]
    ones = jnp.ones(pos.shape[:2] + (MLA_NOPE,), F32)
    pad = LANES - MLA_QK
    cm = jnp.concatenate([ones, c, c, jnp.ones(pos.shape[:2] + (pad,), F32)], axis=-1)
    sm = jnp.concatenate([0 * ones, -s, s, jnp.zeros(pos.shape[:2] + (pad,), F32)], axis=-1)
    return cm, sm, c64, s64


def _freq_order():
    half = HEAD_DIM // 2
    return np.concatenate([np.arange(0, half, 2), np.arange(1, half, 2)])


def _head_perm(n):
    within = np.concatenate([_freq_order(), HEAD_DIM // 2 + _freq_order()])
    idx = np.arange(n)
    return idx // HEAD_DIM * HEAD_DIM + within[idx % HEAD_DIM]


def _partner(n, width):
    idx = np.arange(n)
    return np.where(idx % width < width // 2, idx + width // 2, idx - width // 2)


def _even_weights(w_in, w_uq, w_ukv, qn, kn):
    o2 = MLA_Q_RANK + MLA_KV_RANK
    o3 = o2 + MLA_ROPE
    d = w_in.shape[0]
    p_rope = _partner(MLA_ROPE, MLA_ROPE)
    zeros = lambda n: jnp.zeros((d, n), w_in.dtype)
    kr = w_in[:, o2:o3]
    tail = LANES - MLA_QK
    krp = jnp.concatenate([zeros(MLA_NOPE), kr, zeros(tail)], axis=1)
    krp_rot = jnp.concatenate([zeros(MLA_NOPE), kr[:, p_rope], zeros(tail)], axis=1)
    dil = w_in[:, o3:].reshape(d, 3, len(DIL_CONFIGS), DIL_W)
    dil = jnp.concatenate([dil[:, :2][..., _head_perm(DIL_W)], dil[:, 2:]], axis=1)
    dil_main = dil.transpose(0, 2, 1, 3).reshape(d, -1)
    dil_rot = dil[:, :2][..., _partner(DIL_W, HEAD_DIM)].transpose(0, 2, 1, 3).reshape(d, -1)
    win = jnp.concatenate([w_in[:, :o2], krp, krp_rot, dil_main, dil_rot], axis=1)

    padq = ((0, 0), (0, 0), (0, tail))
    wuq_rot = jnp.concatenate([w_uq[:, :, :MLA_NOPE], w_uq[:, :, MLA_NOPE:][:, :, p_rope]], axis=-1)
    wuq = jnp.concatenate([jnp.pad(w_uq, padq).reshape(MLA_Q_RANK, -1),
                           jnp.pad(wuq_rot, padq).reshape(MLA_Q_RANK, -1)], axis=1)
    wuk = jnp.pad(w_ukv[:, :, :MLA_NOPE], ((0, 0), (0, 0), (0, LANES - MLA_NOPE))).reshape(MLA_KV_RANK, -1)
    wvt = w_ukv[:, :, MLA_NOPE:].reshape(MLA_KV_RANK, -1).T

    def gains(g):
        rot = jnp.concatenate([g[:MLA_NOPE], g[MLA_NOPE:][p_rope]])
        return jnp.stack([jnp.pad(g, (0, tail)), jnp.pad(rot, (0, tail))]).astype(F32)

    return win.astype(BF16), wuq.astype(BF16), wuk.astype(BF16), wvt.astype(BF16), gains(qn), gains(kn)


def _odd_weights(w_in):
    nqk = DIFF_HEADS * LANES
    nm = MOBA_HEADS * HEAD_DIM
    main = jnp.concatenate([w_in[:, :2 * nqk], w_in[:, 3 * nqk:3 * nqk + 2 * nm]], axis=1)
    main = main[:, _head_perm(main.shape[1])]
    win =jnp.concatenate([main, main[:, _partner(main.shape[1], HEAD_DIM)]], axis=1)
    wvt = jnp.concatenate([w_in[:, 2 * nqk:3 * nqk], w_in[:, 3 * nqk + 2 * nm:]], axis=1).T
    return win.astype(BF16), wvt.astype(BF16)


def _gains64(g):
    g = g[_head_perm(HEAD_DIM)]
    rot = g[_partner(HEAD_DIM, HEAD_DIM)]
    return jnp.stack([jnp.concatenate([g, g]), jnp.concatenate([rot, rot])]).astype(F32)


def kernel(x, c, positions, ada_w, ada_b, norm_mix, norm_mlp, mlp_w1, mlp_w2, even_w_in, even_w_out, mla_q_lat_norm, mla_kv_lat_norm, mla_w_uq, mla_w_ukv, mla_q_norm, mla_k_norm, dil_q_norm, dil_k_norm, odd_w_in, odd_w_out, diff_q_norm, diff_k_norm, diff_lambda, diff_subln, moba_q_norm, moba_k_norm):
    b, s, d = x.shape
    depth = ada_w.shape[0]
    t_attn, tm_in, tm_out = ATTN_BLOCK, TOKEN_TILE, TOKEN_TILE

    mod = _adaln(c, ada_w, ada_b)
    cm, sm, c64, s64 = _rope_tables(positions)

    for layer in range(depth):
        sh1, sc1, g1, sh2, sc2, g2 = [mod[layer, :, d * t:d * (t + 1)].reshape(b, 1, d) for t in range(6)]
        mlp_args = (sh2, sc2, g2, _row(norm_mlp[layer]), mlp_w1[layer].astype(BF16), mlp_w2[layer].astype(BF16))
        i = layer // 2
        if layer % 2 == 0:
            win, wuq, wuk, wvt, qn, kn = _even_weights(even_w_in[i], mla_w_uq[i], mla_w_ukv[i],
                                                       mla_q_norm[i], mla_k_norm[i])
            q, k, vt, *dils = _even_in(x, sh1, sc1, _row(norm_mix[layer]), win, wuq, wuk, wvt,
                                       _row(mla_q_lat_norm[i]), _row(mla_kv_lat_norm[i]), qn, kn,
                                       _gains64(dil_q_norm[i]), _gains64(dil_k_norm[i]), cm, sm, c64, s64, tm_in, t_attn)
            o_at = _mla_attention(q, k, vt, MLA_HEADS_PER_STEP)
            o_dil, l_dil = zip(*[_sliding_window(dg, SW_BLOCKS_PER_STEP, f"sliding_window_g{g}")
                                 for g, dg in enumerate(dils)])
            x = _even_out(x, g1, o_at, o_dil, l_dil, even_w_out[i].astype(BF16), mlp_args, tm_out)
        else:
            lam_init = 0.8 - 0.6 * math.exp(-0.3 * layer)
            win, wvt = _odd_weights(odd_w_in[i])
            qc, kc, qm, qmf, km, vct, vmt, kmean = _odd_in(
                x, sh1, sc1, _row(norm_mix[layer]), win, wvt,
                _gains64(diff_q_norm[i]), _gains64(diff_k_norm[i]), _gains64(moba_q_norm[i]),
                _gains64(moba_k_norm[i]),
                c64, s64, tm_in, t_attn)
            o_ct = _diff_attention(qc, kc, vct, diff_lambda[i].astype(F32),
                                   diff_subln[i].reshape(-1, 1).astype(F32), lam_init, DIFF_HEADS_PER_STEP)
            o_dt = _moba_attention(qm, qmf, km, vmt, kmean.reshape(b, s // MOBA_BLOCK, kmean.shape[-1]),
                                   MOBA_PAIRS_PER_STEP)
            x = _odd_out(x, g1, o_ct, o_dt, odd_w_out[i].astype(BF16), mlp_args, tm_out)
    return x
```

```python
import functools
import math

import jax
import jax.numpy as jnp
import numpy as np
from jax import lax
from jax.experimental import pallas as pl
from jax.experimental.pallas import tpu as pltpu

F32 = jnp.float32
BF16 = jnp.bfloat16

LANES = 128
HEAD_DIM = 64
ROPE_THETA = 10000.0
NORM_EPS = 1e-6
NEG_INF = -1e30
LOG2E = math.log2(math.e)

MLA_HEADS = 8
MLA_Q_RANK = 384
MLA_KV_RANK = 256
MLA_NOPE = 64
MLA_ROPE = 32
MLA_QK = MLA_NOPE + MLA_ROPE
DIL_CONFIGS = ((128, 1), (512, 4), (2048, 16))
DIL_HEADS = 4
DIL_W = DIL_HEADS * HEAD_DIM
DIFF_HEADS = 4
MOBA_HEADS = 8
MOBA_BLOCK = 256
MOBA_TOPK = 3
SW_BLOCK = 128
ONES_ROWS = 16

VMEM_LIMIT = 56 * 1024 * 1024

ATTN_BLOCK = MOBA_BLOCK
TOKEN_TILE = 2 * ATTN_BLOCK
MLP_CHUNK = 1024
ADALN_TILE = 1024
SW_BLOCKS_PER_STEP = 4
MLA_HEADS_PER_STEP = 4
DIFF_HEADS_PER_STEP = 2
MOBA_PAIRS_PER_STEP = 2


def _cparams(sem):
    return pltpu.CompilerParams(dimension_semantics=sem, vmem_limit_bytes=VMEM_LIMIT)


def _nt_dot(a, b):
    return lax.dot_general(a, b, (((1,), (1,)), ((), ())), preferred_element_type=F32)


def _tn_dot(a, b):
    return lax.dot_general(a, b, (((0,), (0,)), ((), ())), preferred_element_type=F32)


def _rms(x, w):
    return x * lax.rsqrt(jnp.mean(x * x, axis=-1, keepdims=True) + NORM_EPS) * w


def _lane_iota(shape):
    return lax.broadcasted_iota(jnp.int32, shape, len(shape) - 1)


def _adaln_kernel(c_ref, w_ref, b_ref, o_ref):
    c = c_ref[...]
    cond = c * (1.0 / (1.0 + jnp.exp(-c)))
    o_ref[0] = jnp.dot(cond, w_ref[0], preferred_element_type=F32,
                       precision=lax.Precision.HIGHEST) + b_ref[0]


def _adaln(c, ada_w, ada_b):
    depth, d, n = ada_w.shape
    b = c.shape[0]
    tn = ADALN_TILE
    return pl.pallas_call(
        _adaln_kernel,
        grid=(depth, n // tn),
        in_specs=[pl.BlockSpec((b, d), lambda l, j: (0, 0)),
                  pl.BlockSpec((1, d, tn), lambda l, j: (l, 0, j)),
                  pl.BlockSpec((1, 1, tn), lambda l, j: (l, 0, j))],
        out_specs=pl.BlockSpec((1, b, tn), lambda l, j: (l, 0, j)),
        out_shape=jax.ShapeDtypeStruct((depth, b, n), F32),
        compiler_params=_cparams(("parallel", "parallel")),
        name="adaln",
    )(c, ada_w, ada_b.reshape(depth, 1, n))


def _inv_rms64(x):
    lane = _lane_iota((1, LANES))
    lo = lane < HEAD_DIM
    sq = x * x
    s_lo = jnp.sum(jnp.where(lo, sq, 0.0), axis=-1, keepdims=True)
    s_hi = jnp.sum(jnp.where(lo, 0.0, sq), axis=-1, keepdims=True)
    return lax.rsqrt(jnp.where(lo, s_lo, s_hi) * (1.0 / HEAD_DIM) + NORM_EPS)


def _store_t_blocks(out_ref, xt):
    t = out_ref.shape[-1]
    for j in range(out_ref.shape[1]):
        out_ref[0, j] = xt[:, t * j:t * (j + 1)]


def _t_blocks(b, s, rows, tm, t):
    return (pl.BlockSpec((1, tm // t, rows, t), lambda bi, i: (bi, i, 0, 0)),
            jax.ShapeDtypeStruct((b, s // t, rows, t), BF16))


def _even_in_kernel(x_ref, sh_ref, sc_ref, nw_ref, win_ref, wuq_ref, wuk_ref, wvt_ref,
                    qlat_ref, kvlat_ref, qn_ref, kn_ref, dqn_ref, dkn_ref,
                    cm_ref, sm_ref, c64_ref, s64_ref,
                    q_out, k_out, vt_out, d0_out, d1_out, d2_out, dscr):
    x = x_ref[0]
    tm = x.shape[0]
    h = _rms(x, nw_ref[...]) * (1.0 + sc_ref[0]) + sh_ref[0]
    u = jnp.dot(h.astype(BF16), win_ref[...], preferred_element_type=F32)

    o1 = MLA_Q_RANK
    o2 = o1 + MLA_KV_RANK
    o_kr, o_krr, o_dil = o2, o2 + LANES, o2 + 2 * LANES
    n_dil = 3 * DIL_W * len(DIL_CONFIGS)
    o_rot = o_dil + n_dil
    nq = MLA_HEADS * LANES
    cqn = _rms(u[:, :o1], qlat_ref[...]).astype(BF16)
    qp = jnp.dot(cqn, wuq_ref[...], preferred_element_type=F32)
    ckvn = _rms(u[:, o1:o2], kvlat_ref[...]).astype(BF16)
    kvp = jnp.dot(ckvn, wuk_ref[...], preferred_element_type=F32)
    _store_t_blocks(vt_out, _nt_dot(wvt_ref[...], ckvn).astype(BF16))
    kr, kr_rot = u[:, o_kr:o_kr + LANES], u[:, o_krr:o_krr + LANES]

    cm, sm = cm_ref[0], sm_ref[0]
    q_scale = MLA_QK ** -0.5 * LOG2E
    qgc, qgs = qn_ref[0:1] * cm * q_scale, qn_ref[1:2] * sm * q_scale
    kgc, kgs = kn_ref[0:1] * cm, kn_ref[1:2] * sm
    kr_term = kr_rot * kgs
    for hd in range(MLA_HEADS):
        sl = slice(LANES * hd, LANES * (hd + 1))
        qh = qp[:, sl]
        inv = lax.rsqrt(jnp.sum(qh * qh, -1, keepdims=True) * (1.0 / MLA_QK) + NORM_EPS)
        q_out[0, :, sl] = (inv * (qh * qgc + qp[:, nq + LANES * hd:nq + LANES * (hd + 1)] * qgs)).astype(BF16)
        kh = kvp[:, sl] + kr
        inv = lax.rsqrt(jnp.sum(kh * kh, -1, keepdims=True) * (1.0 / MLA_QK) + NORM_EPS)
        k_out[0, :, sl] = (inv * (kh * kgc + kr_term)).astype(BF16)

    c64, s64 = c64_ref[0], s64_ref[0]
    d_scale = HEAD_DIM ** -0.5 * LOG2E
    dqc, dqs = dqn_ref[0:1] * c64 * d_scale, dqn_ref[1:2] * s64 * d_scale
    dkc, dks = dkn_ref[0:1] * c64, dkn_ref[1:2] * s64
    ncol = DIL_W // LANES
    for g, d_out in enumerate((d0_out, d1_out, d2_out)):
        _, r = DIL_CONFIGS[g]
        base = o_dil + 3 * DIL_W * g
        rbase = o_rot + 2 * DIL_W * g
        for j in range(3 * ncol):
            xc = u[:, base + LANES * j:base + LANES * (j + 1)]
            if j < 2 * ncol:
                xr = u[:, rbase + LANES * j:rbase + LANES * (j + 1)]
                gc, gs = (dqc, dqs) if j < ncol else (dkc, dks)
                xc = _inv_rms64(xc) * (xc * gc + xr * gs)
            if r == 1:
                d_out[0, :, LANES * j:LANES * (j + 1)] = xc.astype(BF16)
            else:
                dscr[j] = xc
                for c in range(r):
                    d_out[0, c, :, LANES * j:LANES * (j + 1)] = dscr[j, pl.ds(c, tm // r, stride=r), :].astype(BF16)


def _row(v):
    return v.reshape(1, -1).astype(F32)


def _even_in(x, sh, sc, nw, win, wuq, wuk, wvt, qlat, kvlat, qn, kn, dqn, dkn, cm, sm, c64, s64, tm, t):
    b, s, d = x.shape
    tok = lambda w: pl.BlockSpec((1, tm, w), lambda bi, i: (bi, i, 0))
    per_b = pl.BlockSpec((1, 1, d), lambda bi, i: (bi, 0, 0))
    full = lambda a: pl.BlockSpec(a.shape, lambda bi, i: (0,) * a.ndim, pipeline_mode=pl.Buffered(1))
    vt_spec, vt_shape = _t_blocks(b, s, wvt.shape[0], tm, t)
    dil_specs, dil_shapes = [], []
    for _, r in DIL_CONFIGS:
        if r == 1:
            dil_specs.append(tok(3 * DIL_W))
            dil_shapes.append(jax.ShapeDtypeStruct((b, s, 3 * DIL_W), BF16))
        else:
            dil_specs.append(pl.BlockSpec((1, r, tm // r, 3 * DIL_W), lambda bi, i: (bi, 0, i, 0)))
            dil_shapes.append(jax.ShapeDtypeStruct((b, r, s // r, 3 * DIL_W), BF16))
    return pl.pallas_call(
        _even_in_kernel,
        grid=(b, s // tm),
        in_specs=[tok(d), per_b, per_b, full(nw), full(win), full(wuq), full(wuk), full(wvt),
                  full(qlat), full(kvlat), full(qn), full(kn), full(dqn), full(dkn),
                  tok(LANES), tok(LANES), tok(LANES), tok(LANES)],
        out_specs=[tok(MLA_HEADS * LANES), tok(MLA_HEADS * LANES), vt_spec] + dil_specs,
        out_shape=[jax.ShapeDtypeStruct((b, s, MLA_HEADS * LANES), BF16),
                   jax.ShapeDtypeStruct((b, s, MLA_HEADS * LANES), BF16), vt_shape] + dil_shapes,
        scratch_shapes=[pltpu.VMEM((3 * DIL_W // LANES, tm, LANES), F32)],
        compiler_params=_cparams(("parallel", "arbitrary")),
        name="even_in_proj",
    )(x, sh, sc, nw, win, wuq, wuk, wvt, qlat, kvlat, qn, kn, dqn, dkn, cm, sm, c64, s64)


def _odd_in_kernel(x_ref, sh_ref, sc_ref, nw_ref, win_ref, wvt_ref, dqn_ref, dkn_ref, mqn_ref, mkn_ref,
                   c64_ref, s64_ref,
                   qc_out, kc_out, qm_out, qmf_out, km_out, vct_out, vmt_out, kmean_out):
    x = x_ref[0]
    tm = x.shape[0]
    h = (_rms(x, nw_ref[...]) * (1.0 + sc_ref[0]) + sh_ref[0]).astype(BF16)
    u = jnp.dot(h, win_ref[...], preferred_element_type=F32)
    vt = _nt_dot(wvt_ref[...], h).astype(BF16)
    nvc = vct_out.shape[2]
    _store_t_blocks(vct_out, vt[:nvc])
    _store_t_blocks(vmt_out, vt[nvc:])
    c64, s64 = c64_ref[0], s64_ref[0]
    scale = HEAD_DIM ** -0.5 * LOG2E
    nqk = DIFF_HEADS * LANES
    nm = MOBA_HEADS * HEAD_DIM
    rot = 2 * nqk + 2 * nm

    def tables(gain_ref, mult):
        return gain_ref[0:1] * c64 * mult, gain_ref[1:2] * s64 * mult

    def roped(off, col, tab):
        a = off + LANES * col
        xc, xr = u[:, a:a + LANES], u[:, rot + a:rot + a + LANES]
        return _inv_rms64(xc) * (xc * tab[0] + xr * tab[1])

    t_qc, t_kc, t_qm, t_km = tables(dqn_ref, scale), tables(dkn_ref, 1.0), tables(mqn_ref, 1.0), tables(mkn_ref, 1.0)
    for col in range(nqk // LANES):
        sl = slice(LANES * col, LANES * (col + 1))
        qc_out[0, :, sl] = roped(0, col, t_qc).astype(BF16)
        kc_out[0, :, sl] = roped(nqk, col, t_kc).astype(BF16)

    base = 2 * nqk
    for col in range(nm // LANES):
        sl = slice(LANES * col, LANES * (col + 1))
        qm = roped(base, col, t_qm)
        qmf_out[0, :, sl] = qm
        qm_out[0, :, sl] = (qm * scale).astype(BF16)
        km = roped(base + nm, col, t_km)
        km_out[0, :, sl] = km.astype(BF16)
        for blk in range(tm // MOBA_BLOCK):
            rows = slice(MOBA_BLOCK * blk, MOBA_BLOCK * (blk + 1))
            kmean_out[0, 0, blk:blk + 1, sl] = jnp.mean(km[rows], axis=0, keepdims=True)


def _odd_in(x, sh, sc, nw, win, wvt, dqn, dkn, mqn, mkn, c64, s64, tm, t):
    b, s, d = x.shape
    tok = lambda w: pl.BlockSpec((1, tm, w), lambda bi, i: (bi, i, 0))
    per_b = pl.BlockSpec((1, 1, d), lambda bi, i: (bi, 0, 0))
    full = lambda a: pl.BlockSpec(a.shape, lambda bi, i: (0,) * a.ndim, pipeline_mode=pl.Buffered(1))
    w = DIFF_HEADS * LANES
    nblk = tm // MOBA_BLOCK
    shp = lambda dt: jax.ShapeDtypeStruct((b, s, w), dt)
    vt_spec, vt_shape = _t_blocks(b, s, wvt.shape[0] // 2, tm, t)
    return pl.pallas_call(
        _odd_in_kernel,
        grid=(b, s // tm),
        in_specs=[tok(d), per_b, per_b, full(nw), full(win), full(wvt), full(dqn), full(dkn), full(mqn), full(mkn),
                  tok(LANES), tok(LANES)],
        out_specs=[tok(w)] * 5 + [vt_spec, vt_spec, pl.BlockSpec((1, 1, nblk, w), lambda bi, i: (bi, i, 0, 0))],
        out_shape=[shp(BF16), shp(BF16), shp(BF16), shp(F32), shp(BF16), vt_shape, vt_shape,
                   jax.ShapeDtypeStruct((b, s // tm, nblk, w), F32)],
        compiler_params=_cparams(("parallel", "arbitrary")),
        name="odd_in_proj",
    )(x, sh, sc, nw, win, wvt, dqn, dkn, mqn, mkn, c64, s64)


def _q_tiles_t(q, split_halves):
    qt = q.astype(F32).T
    if not split_halves:
        return [qt.astype(BF16)]
    row = lax.broadcasted_iota(jnp.int32, qt.shape, 0)
    return [jnp.where((row >= HEAD_DIM) == bool(half), qt, 0.0).astype(BF16) for half in range(2)]


def _flash_t(qts, k_ref, vt_ref, k_slices, v_rows, scr, i, t, past_mask=None):
    m_scr, acc_scr, s_scr, p_scr, a_scr, b_scr = scr
    n = len(qts)
    ones = jnp.ones((ONES_ROWS, t), BF16)

    def qk(blk, s):
        b0 = pl.multiple_of(blk * t, t)
        return jnp.dot(k_ref[0, pl.ds(b0, t), k_slices[s]], qts[s], preferred_element_type=F32)

    def pv(blk, s, p):
        vt = jnp.concatenate([vt_ref[0, blk, v_rows[s], :], ones], axis=0)
        return jnp.dot(vt, p, preferred_element_type=F32)

    def store_scores(slot, s, st):
        s_scr[slot, s] = st
        b_scr[slot, s] = jnp.max(st, axis=0, keepdims=True)

    krow = lax.broadcasted_iota(jnp.int32, (t, t), 0)
    qcol = lax.broadcasted_iota(jnp.int32, (t, t), 1)
    causal = krow <= qcol
    last_past = jnp.maximum(i - 1, 0)
    for s in range(n):
        st = jnp.where(causal, qk(i, s), NEG_INF)
        m = jnp.max(st, axis=0, keepdims=True)
        m_scr[s] = m
        p_scr[0, s] = jnp.exp2(st - m).astype(BF16)
        p_scr[1, s] = jnp.zeros((t, t), BF16)
        a_scr[0, s] = jnp.ones_like(m)
        a_scr[1, s] = jnp.ones_like(m)
        acc_scr[s] = jnp.zeros(acc_scr.shape[1:], F32)
        store_scores(0, s, qk(0, s))
        store_scores(1, s, qk(jnp.minimum(1, last_past), s))

    def softmax_stage(slot, valid, rs, ws, blk):
        for s in range(n):
            cmax = b_scr[2 * rs + slot, s]
            keep = None if past_mask is None else past_mask(s, blk)
            if keep is not None:
                cmax = jnp.where(keep, cmax, NEG_INF)
            if valid is not None:
                cmax = jnp.where(valid, cmax, NEG_INF)
            m_prev = m_scr[s]
            m_new = jnp.maximum(m_prev, cmax)
            a_scr[2 * ws + slot, s] = jnp.exp2(m_prev - m_new)
            p = jnp.exp2(s_scr[2 * rs + slot, s] - m_new).astype(BF16)
            if keep is not None:
                p = jnp.where(keep, p, jnp.zeros_like(p))
            p_scr[2 * ws + slot, s] = p
            m_scr[s] = m_new

    def body(kk, rs, ws):
        first = kk == 0
        pa_blk = jnp.where(first, i, 2 * kk - 2)
        pb_blk = jnp.where(first, i, 2 * kk - 1)
        p_prev = [[p_scr[2 * rs + sl, s] for s in range(n)] for sl in range(2)]
        a_prev = [[a_scr[2 * rs + sl, s] for s in range(n)] for sl in range(2)]
        s_next = [[qk(jnp.minimum(2 * kk + 2 + sl, last_past), s) for s in range(n)] for sl in range(2)]
        pvs = [[pv(blk, s, p_prev[sl][s]) for s in range(n)] for sl, blk in enumerate((pa_blk, pb_blk))]
        softmax_stage(0, None, rs, ws, 2 * kk)
        softmax_stage(1, 2 * kk + 1 < i, rs, ws, jnp.minimum(2 * kk + 1, last_past))
        for sl in range(2):
            for s in range(n):
                store_scores(2 * ws + sl, s, s_next[sl][s])
        for s in range(n):
            acc_scr[s] = a_prev[1][s] * (a_prev[0][s] * acc_scr[s] + pvs[0][s]) + pvs[1][s]

    def two_bodies(kp, carry):
        body(2 * kp, 0, 1)
        body(2 * kp + 1, 1, 0)
        return carry

    nbody = (i + 1) // 2
    lax.fori_loop(0, nbody // 2, two_bodies, 0)

    @pl.when(nbody % 2 == 1)
    def _():
        body(nbody - 1, 0, 1)

    fs = nbody % 2
    kl = jnp.maximum(nbody - 1, 0)
    none = i == 0
    fa_blk = jnp.where(none, i, 2 * kl)
    fb_blk = jnp.where(none, i, jnp.minimum(2 * kl + 1, last_past))
    fb_scale = jnp.where(jnp.logical_or(none, i % 2 == 0), 1.0, 0.0)
    for s in range(n):
        acc_scr[s] = (a_scr[2 * fs + 1, s] * (a_scr[2 * fs, s] * acc_scr[s] + pv(fa_blk, s, p_scr[2 * fs, s]))
                      + fb_scale * pv(fb_blk, s, p_scr[2 * fs + 1, s]))


def _flash_scratch(n, nv, t):
    return [pltpu.VMEM((n, 1, t), F32), pltpu.VMEM((n, nv + ONES_ROWS, t), F32), pltpu.VMEM((4, n, t, t), F32),
            pltpu.VMEM((4, n, t, t), BF16), pltpu.VMEM((4, n, 1, t), F32), pltpu.VMEM((4, n, 1, t), F32)]


def _normalised(acc_scr, s, nv):
    return acc_scr[s, :nv, :] / acc_scr[s, nv:nv + 1, :]


def _half_mask(x, half):
    lane = _lane_iota((1, LANES))
    return jnp.where((lane >= HEAD_DIM) == bool(half), x, jnp.zeros_like(x))


def _mla_kernel(q_ref, k_ref, vt_ref, ot_ref, *scr, t, nh):
    i = pl.program_id(2)
    qs = [_q_tiles_t(q_ref[0, :, LANES * s:LANES * (s + 1)], False)[0] for s in range(nh)]
    ks = [slice(LANES * s, LANES * (s + 1)) for s in range(nh)]
    vr = [slice(HEAD_DIM * s, HEAD_DIM * (s + 1)) for s in range(nh)]
    _flash_t(qs, k_ref, vt_ref, ks, vr, scr, i, t)
    for s in range(nh):
        ot_ref[0, 0, vr[s], :] = _normalised(scr[1], s, HEAD_DIM).astype(ot_ref.dtype)


def _mla_attention(q, k, vt, nh):
    b, s, _ = q.shape
    _, nb, nv, t = vt.shape
    return pl.pallas_call(
        functools.partial(_mla_kernel, t=t, nh=nh),
        grid=(b, MLA_HEADS // nh, nb),
        in_specs=[pl.BlockSpec((1, t, nh * LANES), lambda bi, p, i: (bi, i, p)),
                  pl.BlockSpec((1, s, nh * LANES), lambda bi, p, i: (bi, 0, p)),
                  pl.BlockSpec((1, nb, nh * HEAD_DIM, t), lambda bi, p, i: (bi, 0, p, 0))],
        out_specs=pl.BlockSpec((1, 1, nh * HEAD_DIM, t), lambda bi, p, i: (bi, i, p, 0)),
        out_shape=jax.ShapeDtypeStruct((b, nb, nv, t), BF16),
        scratch_shapes=_flash_scratch(nh, HEAD_DIM, t),
        compiler_params=_cparams(("parallel", "parallel", "arbitrary")),
        name="mla_attention",
    )(q, k, vt)


def _diff_kernel(q_ref, k_ref, vt_ref, lam_ref, sub_ref, ot_ref, *scr, t, nh, lam_init):
    i = pl.program_id(2)
    qs, ks, vr = [], [], []
    for hd in range(nh):
        cols = slice(LANES * hd, LANES * (hd + 1))
        qs += _q_tiles_t(q_ref[0, :, cols], True)
        ks += [cols, cols]
        vr += [cols, cols]
    _flash_t(qs, k_ref, vt_ref, ks, vr, scr, i, t)
    lv = lam_ref[...]
    lam = (jnp.exp(jnp.sum(lv[0:1] * lv[1:2], keepdims=True))
           - jnp.exp(jnp.sum(lv[2:3] * lv[3:4], keepdims=True)) + lam_init)
    for hd in range(nh):
        o = _normalised(scr[1], 2 * hd, LANES) - lam * _normalised(scr[1], 2 * hd + 1, LANES)
        o = o * lax.rsqrt(jnp.mean(o * o, axis=0, keepdims=True) + NORM_EPS) * sub_ref[...]
        ot_ref[0, 0, LANES * hd:LANES * (hd + 1), :] = (o * (1.0 - lam_init)).astype(ot_ref.dtype)


def _diff_attention(q, k, vt, lam_rows, subln_col, lam_init, nh):
    b, s, w = q.shape
    _, nb, _, t = vt.shape
    return pl.pallas_call(
        functools.partial(_diff_kernel, t=t, nh=nh, lam_init=lam_init),
        grid=(b, DIFF_HEADS // nh, nb),
        in_specs=[pl.BlockSpec((1, t, nh * LANES), lambda bi, p, i: (bi, i, p)),
                  pl.BlockSpec((1, s, nh * LANES), lambda bi, p, i: (bi, 0, p)),
                  pl.BlockSpec((1, nb, nh * LANES, t), lambda bi, p, i: (bi, 0, p, 0)),
                  pl.BlockSpec(lam_rows.shape, lambda bi, p, i: (0, 0)),
                  pl.BlockSpec(subln_col.shape, lambda bi, p, i: (0, 0))],
        out_specs=pl.BlockSpec((1, 1, nh * LANES, t), lambda bi, p, i: (bi, i, p, 0)),
        out_shape=jax.ShapeDtypeStruct((b, nb, w, t), BF16),
        scratch_shapes=_flash_scratch(2 * nh, LANES, t),
        compiler_params=_cparams(("parallel", "parallel", "arbitrary")),
        name="diff_attention",
    )(q, k, vt, lam_rows, subln_col)


def _moba_kernel(q_ref, qf_ref, k_ref, vt_ref, kmean_ref, ot_ref, sel_scr, *scr, t, npair):
    i = pl.program_id(2)
    nb = kmean_ref.shape[1]
    brow = lax.broadcasted_iota(jnp.int32, (nb, t), 0).astype(F32)
    past = brow < i.astype(F32)
    qs, ks, vr = [], [], []
    for pr in range(npair):
        cols = slice(LANES * pr, LANES * (pr + 1))
        qf, kmean = qf_ref[0, :, cols], kmean_ref[0, :, cols]
        qs += _q_tiles_t(q_ref[0, :, cols], True)
        for half in range(2):
            s = 2 * pr + half
            gate = lax.dot_general(kmean, _half_mask(qf, half), (((1,), (1,)), ((), ())),
                                   preferred_element_type=F32, precision=lax.Precision.HIGHEST)
            gate = jnp.where(past, gate, NEG_INF)
            sel = jnp.zeros((nb, t), F32)
            for _ in range(MOBA_TOPK):
                top = jnp.max(gate, axis=0, keepdims=True)
                first = jnp.min(jnp.where(gate == top, brow, float(nb)), axis=0, keepdims=True)
                pick = brow == first
                sel = jnp.where(pick, 1.0, sel)
                gate = jnp.where(pick, 2 * NEG_INF, gate)
            sel_scr[s] = jnp.where(past, sel, 0.0)
            ks.append(cols)
            vr.append(slice(HEAD_DIM * s, HEAD_DIM * (s + 1)))
    _flash_t(qs, k_ref, vt_ref, ks, vr, scr, i, t,
             past_mask=lambda s, jj: sel_scr[s, pl.ds(jj, 1), :] > 0.5)
    for s in range(2 * npair):
        ot_ref[0, 0, vr[s], :] = _normalised(scr[1], s, HEAD_DIM).astype(ot_ref.dtype)


def _moba_attention(q, qf, k, vt, kmean, npair):
    b, s, w = q.shape
    _, nb, _, t = vt.shape
    nkm = kmean.shape[1]
    lanes = npair * LANES
    return pl.pallas_call(
        functools.partial(_moba_kernel, t=t, npair=npair),
        grid=(b, w // lanes, nb),
        in_specs=[pl.BlockSpec((1, t, lanes), lambda bi, p, i: (bi, i, p)),
                  pl.BlockSpec((1, t, lanes), lambda bi, p, i: (bi, i, p)),
                  pl.BlockSpec((1, s, lanes), lambda bi, p, i: (bi, 0, p)),
                  pl.BlockSpec((1, nb, lanes, t), lambda bi, p, i: (bi, 0, p, 0)),
                  pl.BlockSpec((1, nkm, lanes), lambda bi, p, i: (bi, 0, p))],
        out_specs=pl.BlockSpec((1, 1, lanes, t), lambda bi, p, i: (bi, i, p, 0)),
        out_shape=jax.ShapeDtypeStruct((b, nb, w, t), BF16),
        scratch_shapes=[pltpu.VMEM((2 * npair, nkm, t), F32)] + _flash_scratch(2 * npair, HEAD_DIM, t),
        compiler_params=_cparams(("parallel", "parallel", "arbitrary")),
        name="moba_attention",
    )(q, qf, k, vt, kmean)


def _sw_kernel(q_ref, kc_ref, vc_ref, kp_ref, vp_ref, o_ref, l_ref, *, nbk):
    i = pl.program_id(1)
    blk = SW_BLOCK
    qi = lax.broadcasted_iota(jnp.int32, (blk, 2 * blk), 0)
    kj = lax.broadcasted_iota(jnp.int32, (blk, 2 * blk), 1)
    band = (kj >= qi) & (kj <= qi + blk)
    lane = _lane_iota((1, LANES))
    heads = [(pair, half) for pair in range(DIL_W // LANES) for half in range(2)]

    def scores(n):
        rows = slice(blk * n, blk * (n + 1))
        q = q_ref[0, rows, :]
        if n == 0:
            kprev, valid = kp_ref[0], band & ((kj >= blk) | (i > 0))
        else:
            kprev, valid = kc_ref[0, blk * (n - 1):blk * n, :], band
        kk = jnp.concatenate([kprev, kc_ref[0, rows, :]], axis=0)
        out = []
        for pair, half in heads:
            sl = slice(LANES * pair, LANES * (pair + 1))
            out.append(jnp.where(valid, _nt_dot(_half_mask(q[:, sl], half), kk[:, sl]), NEG_INF))
        return out

    nxt = scores(0)
    for n in range(nbk):
        cur, rows = nxt, slice(blk * n, blk * (n + 1))
        if n + 1 < nbk:
            nxt = scores(n + 1)
        vprev = vp_ref[0] if n == 0 else vc_ref[0, blk * (n - 1):blk * n, :]
        vv = jnp.concatenate([vprev, vc_ref[0, rows, :]], axis=0)
        outs, lses = [], []
        for (pair, half), sc in zip(heads, cur):
            m = jnp.max(sc, axis=-1, keepdims=True)
            p = jnp.exp2(sc - m)
            l = jnp.sum(p, axis=-1, keepdims=True)
            sl = slice(LANES * pair, LANES * (pair + 1))
            outs.append(jnp.dot(p.astype(BF16), vv[:, sl], preferred_element_type=F32) / l)
            lses.append(m + jnp.log2(l))
        for pair in range(DIL_W // LANES):
            sl = slice(LANES * pair, LANES * (pair + 1))
            o_ref[0, rows, sl] = jnp.where(lane < HEAD_DIM, outs[2 * pair], outs[2 * pair + 1])
            l_ref[0, rows, sl] = jnp.where(lane < HEAD_DIM, lses[2 * pair], lses[2 * pair + 1])


def _sliding_window(dil, nbk, name):
    lead, (m, w3) = dil.shape[:-2], dil.shape[-2:]
    seqs = dil.reshape((-1, m, w3))
    nbk = min(nbk, m // SW_BLOCK)
    rows = SW_BLOCK * nbk
    cur = lambda off: pl.BlockSpec((1, rows, DIL_W), lambda n, i: (n, i, off))
    prev = lambda off: pl.BlockSpec((1, SW_BLOCK, DIL_W), lambda n, i: (n, jnp.maximum(i * nbk - 1, 0), off))
    out_spec = pl.BlockSpec((1, rows, DIL_W), lambda n, i: (n, i, 0))
    o, l = pl.pallas_call(
        functools.partial(_sw_kernel, nbk=nbk),
        grid=(seqs.shape[0], m // rows),
        in_specs=[cur(0), cur(1), cur(2), prev(1), prev(2)],
        out_specs=[out_spec, out_spec],
        out_shape=[jax.ShapeDtypeStruct((seqs.shape[0], m, DIL_W), F32)] * 2,
        compiler_params=_cparams(("parallel", "arbitrary")),
        name=name,
    )(seqs, seqs, seqs, seqs, seqs)
    return o.reshape(lead + (m, DIL_W)), l.reshape(lead + (m, DIL_W))


def _mlp_tail(y, mlp_refs):
    sh_ref, sc_ref, g_ref, nw_ref, w1_ref, w2_ref = mlp_refs
    h = (_rms(y, nw_ref[...]) * (1.0 + sc_ref[0]) + sh_ref[0]).astype(BF16)
    acc = jnp.zeros(y.shape, F32)
    for c in range(w1_ref.shape[1] // MLP_CHUNK):
        cols = slice(MLP_CHUNK * c, MLP_CHUNK * (c + 1))
        a = jnp.maximum(jnp.dot(h, w1_ref[:, cols], preferred_element_type=F32), 0.0)
        acc = acc + jnp.dot((a * a).astype(BF16), w2_ref[cols, :], preferred_element_type=F32)
    return y + g_ref[0] * acc


def _proj_t(ot_ref, w_rows):
    return jnp.concatenate([_tn_dot(ot_ref[0, j], w_rows) for j in range(ot_ref.shape[1])], axis=0)


def _mlp_specs(mlp_args, d):
    per_b = pl.BlockSpec((1, 1, d), lambda bi, i: (bi, 0, 0))
    const = lambda a: pl.BlockSpec(a.shape, lambda bi, i: (0, 0), pipeline_mode=pl.Buffered(1))
    sh, sc, g2, nw, w1, w2 = mlp_args
    return [per_b, per_b, per_b, pl.BlockSpec(nw.shape, lambda bi, i: (0, 0)), const(w1), const(w2)]


def _even_out_kernel(x_ref, g_ref, oat_ref, o0_ref, o1_ref, o2_ref, l0_ref, l1_ref, l2_ref, w_ref,
                     *rest):
    *mlp_refs, y_ref, tscr = rest
    tm = x_ref.shape[1]
    ncol = DIL_W // LANES

    def token_major(ref, slot):
        r = ref.shape[1]
        for c in range(r):
            for j in range(ncol):
                tscr[slot * ncol + j, pl.ds(c, tm // r, stride=r), :] = ref[0, c, :, LANES * j:LANES * (j + 1)]
        return jnp.concatenate([tscr[slot * ncol + j] for j in range(ncol)], axis=1)

    o0, l0 = o0_ref[0], l0_ref[0]
    o1, l1 = token_major(o1_ref, 0), token_major(l1_ref, 1)
    o2, l2 = token_major(o2_ref, 2), token_major(l2_ref, 3)
    top = jnp.maximum(jnp.maximum(l0, l1), l2)
    w0, w1, w2 = jnp.exp2(l0 - top), jnp.exp2(l1 - top), jnp.exp2(l2 - top)
    ob = (w0 * o0 + w1 * o1 + w2 * o2) / (w0 + w1 + w2)
    na = oat_ref.shape[2]
    y = _proj_t(oat_ref, w_ref[:na, :]) + jnp.dot(ob.astype(BF16), w_ref[na:, :], preferred_element_type=F32)
    y_ref[0] = _mlp_tail(x_ref[0] + g_ref[0] * y, mlp_refs)


def _t_spec(a, tm):
    return pl.BlockSpec((1, tm // a.shape[-1]) + a.shape[2:], lambda bi, i: (bi, i, 0, 0))


def _even_out(x, g1, oat, o_dil, l_dil, w_out, mlp_args, tm):
    b, s, d = x.shape
    tok = lambda w: pl.BlockSpec((1, tm, w), lambda bi, i: (bi, i, 0))

    def dil_spec(a):
        if a.ndim == 3:
            return tok(DIL_W)
        r = a.shape[1]
        return pl.BlockSpec((1, r, tm // r, DIL_W), lambda bi, i: (bi, 0, i, 0))

    return pl.pallas_call(
        _even_out_kernel,
        grid=(b, s // tm),
        in_specs=[tok(d), pl.BlockSpec((1, 1, d), lambda bi, i: (bi, 0, 0)), _t_spec(oat, tm)]
                 + [dil_spec(a) for a in (*o_dil, *l_dil)]
                 + [pl.BlockSpec(w_out.shape, lambda bi, i: (0, 0), pipeline_mode=pl.Buffered(1))]
                 + _mlp_specs(mlp_args, d),
        out_specs=tok(d),
        out_shape=jax.ShapeDtypeStruct((b, s, d), F32),
        scratch_shapes=[pltpu.VMEM((4 * DIL_W // LANES, tm, LANES), F32)],
        compiler_params=_cparams(("parallel", "arbitrary")),
        name="even_out_mlp",
    )(x, g1, oat, *o_dil, *l_dil, w_out, *mlp_args)


def _odd_out_kernel(x_ref, g_ref, oct_ref, odt_ref, w_ref, *rest):
    *mlp_refs, y_ref = rest
    nc = oct_ref.shape[2]
    y = _proj_t(oct_ref, w_ref[:nc, :]) + _proj_t(odt_ref, w_ref[nc:, :])
    y_ref[0] = _mlp_tail(x_ref[0] + g_ref[0] * y, mlp_refs)


def _odd_out(x, g1, oct, odt, w_out, mlp_args, tm):
    b, s, d = x.shape
    tok = lambda w: pl.BlockSpec((1, tm, w), lambda bi, i: (bi, i, 0))
    return pl.pallas_call(
        _odd_out_kernel,
        grid=(b, s // tm),
        in_specs=[tok(d), pl.BlockSpec((1, 1, d), lambda bi, i: (bi, 0, 0)), _t_spec(oct, tm), _t_spec(odt, tm),
                  pl.BlockSpec(w_out.shape, lambda bi, i: (0, 0), pipeline_mode=pl.Buffered(1))]
                 + _mlp_specs(mlp_args, d),
        out_specs=tok(d),
        out_shape=jax.ShapeDtypeStruct((b, s, d), F32),
        compiler_params=_cparams(("parallel", "arbitrary")),
        name="odd_out_mlp",
    )(x, g1, oct, odt, w_out, *mlp_args)


def _rope_tables(positions):
    pos = positions.astype(F32)[:, :, None]
    half = HEAD_DIM // 2
    inv = ROPE_THETA ** (-jnp.arange(half, dtype=F32) / half)
    ang = pos * inv[_freq_order()]
    c, s = jnp.cos(ang), jnp.sin(ang)

    def tiled(x, width):
        reps = LANES // width
        return jnp.broadcast_to(x[:, :, None, :], x.shape[:2] + (reps, width)).reshape(x.shape[:2] + (LANES,))

    lane = np.arange(LANES)
    sign = np.where(lane % HEAD_DIM < half, -1.0, 1.0).astype(np.float32)
    c64, s64 = tiled(c, half), tiled(s, half) * sign
    hm = MLA_ROPE // 2
    rope = (lane >= MLA_NOPE) & (lane < MLA_QK)
    sign_m = np.where(lane < MLA_NOPE + hm, -1.0, 1.0).astype(np.float32)
    cm = jnp.where(rope, tiled(c[..., :hm], hm), 1.0)
    sm = jnp.where(rope, tiled(s[..., :hm], hm) * sign_m, 0.0)
    return cm, sm, c64, s64


def _freq_order():
    half = HEAD_DIM // 2
    return np.concatenate([np.arange(0, half, 2), np.arange(1, half, 2)])


def _head_perm(n):
    within = np.concatenate([_freq_order(), HEAD_DIM // 2 + _freq_order()])
    idx = np.arange(n)
    return idx // HEAD_DIM * HEAD_DIM + within[idx % HEAD_DIM]


def _partner(n, width):
    idx = np.arange(n)
    return np.where(idx % width < width // 2, idx + width // 2, idx - width // 2)


def _even_weights(w_in, w_uq, w_ukv, qn, kn):
    o2 = MLA_Q_RANK + MLA_KV_RANK
    o3 = o2 + MLA_ROPE
    d = w_in.shape[0]
    p_rope = _partner(MLA_ROPE, MLA_ROPE)
    zeros = lambda n: jnp.zeros((d, n), w_in.dtype)
    kr = w_in[:, o2:o3]
    tail = LANES - MLA_QK
    krp = jnp.concatenate([zeros(MLA_NOPE), kr, zeros(tail)], axis=1)
    krp_rot = jnp.concatenate([zeros(MLA_NOPE), kr[:, p_rope], zeros(tail)], axis=1)
    dil = w_in[:, o3:].reshape(d, 3, len(DIL_CONFIGS), DIL_W)
    dil = jnp.concatenate([dil[:, :2][..., _head_perm(DIL_W)], dil[:, 2:]], axis=1)
    dil_main = dil.transpose(0, 2, 1, 3).reshape(d, -1)
    dil_rot = dil[:, :2][..., _partner(DIL_W, HEAD_DIM)].transpose(0, 2, 1, 3).reshape(d, -1)
    win = jnp.concatenate([w_in[:, :o2], krp, krp_rot, dil_main, dil_rot], axis=1)

    padq = ((0, 0), (0, 0), (0, tail))
    wuq_rot = jnp.concatenate([w_uq[:, :, :MLA_NOPE], w_uq[:, :, MLA_NOPE:][:, :, p_rope]], axis=-1)
    wuq = jnp.concatenate([jnp.pad(w_uq, padq).reshape(MLA_Q_RANK, -1),
                           jnp.pad(wuq_rot, padq).reshape(MLA_Q_RANK, -1)], axis=1)
    wuk = jnp.pad(w_ukv[:, :, :MLA_NOPE], ((0, 0), (0, 0), (0, LANES - MLA_NOPE))).reshape(MLA_KV_RANK, -1)
    wvt = w_ukv[:, :, MLA_NOPE:].reshape(MLA_KV_RANK, -1).T

    def gains(g):
        rot = jnp.concatenate([g[:MLA_NOPE], g[MLA_NOPE:][p_rope]])
        return jnp.stack([jnp.pad(g, (0, tail)), jnp.pad(rot, (0, tail))]).astype(F32)

    return win.astype(BF16), wuq.astype(BF16), wuk.astype(BF16), wvt.astype(BF16), gains(qn), gains(kn)


def _odd_weights(w_in):
    nqk = DIFF_HEADS * LANES
    nm = MOBA_HEADS * HEAD_DIM
    main = jnp.concatenate([w_in[:, :2 * nqk], w_in[:, 3 * nqk:3 * nqk + 2 * nm]], axis=1)
    main = main[:, _head_perm(main.shape[1])]
    win =jnp.concatenate([main, main[:, _partner(main.shape[1], HEAD_DIM)]], axis=1)
    wvt = jnp.concatenate([w_in[:, 2 * nqk:3 * nqk], w_in[:, 3 * nqk + 2 * nm:]], axis=1).T
    return win.astype(BF16), wvt.astype(BF16)


def _gains64(g):
    g = g[_head_perm(HEAD_DIM)]
    rot = g[_partner(HEAD_DIM, HEAD_DIM)]
    return jnp.stack([jnp.concatenate([g, g]), jnp.concatenate([rot, rot])]).astype(F32)


def kernel(x, c, positions, ada_w, ada_b, norm_mix, norm_mlp, mlp_w1, mlp_w2, even_w_in, even_w_out, mla_q_lat_norm, mla_kv_lat_norm, mla_w_uq, mla_w_ukv, mla_q_norm, mla_k_norm, dil_q_norm, dil_k_norm, odd_w_in, odd_w_out, diff_q_norm, diff_k_norm, diff_lambda, diff_subln, moba_q_norm, moba_k_norm):
    b, s, d = x.shape
    depth = ada_w.shape[0]
    t_attn, tm_in, tm_out = ATTN_BLOCK, TOKEN_TILE, TOKEN_TILE

    mod = _adaln(c, ada_w, ada_b)
    cm, sm, c64, s64 = _rope_tables(positions)

    for layer in range(depth):
        sh1, sc1, g1, sh2, sc2, g2 = [mod[layer, :, d * t:d * (t + 1)].reshape(b, 1, d) for t in range(6)]
        mlp_args = (sh2, sc2, g2, _row(norm_mlp[layer]), mlp_w1[layer].astype(BF16), mlp_w2[layer].astype(BF16))
        i = layer // 2
        if layer % 2 == 0:
            win, wuq, wuk, wvt, qn, kn = _even_weights(even_w_in[i], mla_w_uq[i], mla_w_ukv[i],
                                                       mla_q_norm[i], mla_k_norm[i])
            q, k, vt, *dils = _even_in(x, sh1, sc1, _row(norm_mix[layer]), win, wuq, wuk, wvt,
                                       _row(mla_q_lat_norm[i]), _row(mla_kv_lat_norm[i]), qn, kn,
                                       _gains64(dil_q_norm[i]), _gains64(dil_k_norm[i]), cm, sm, c64, s64, tm_in, t_attn)
            o_at = _mla_attention(q, k, vt, MLA_HEADS_PER_STEP)
            o_dil, l_dil = zip(*[_sliding_window(dg, SW_BLOCKS_PER_STEP, f"sliding_window_g{g}")
                                 for g, dg in enumerate(dils)])
            x = _even_out(x, g1, o_at, o_dil, l_dil, even_w_out[i].astype(BF16), mlp_args, tm_out)
        else:
            lam_init = 0.8 - 0.6 * math.exp(-0.3 * layer)
            win, wvt = _odd_weights(odd_w_in[i])
            qc, kc, qm, qmf, km, vct, vmt, kmean = _odd_in(
                x, sh1, sc1, _row(norm_mix[layer]), win, wvt,
                _gains64(diff_q_norm[i]), _gains64(diff_k_norm[i]), _gains64(moba_q_norm[i]),
                _gains64(moba_k_norm[i]),
                c64, s64, tm_in, t_attn)
            o_ct = _diff_attention(qc, kc, vct, diff_lambda[i].astype(F32),
                                   diff_subln[i].reshape(-1, 1).astype(F32), lam_init, DIFF_HEADS_PER_STEP)
            o_dt = _moba_attention(qm, qmf, km, vmt, kmean.reshape(b, s // MOBA_BLOCK, kmean.shape[-1]),
                                   MOBA_PAIRS_PER_STEP)
            x = _odd_out(x, g1, o_ct, o_dt, odd_w_out[i].astype(BF16), mlp_args, tm_out)
    return x
```

```python
import functools
import math

import jax
import jax.numpy as jnp
import numpy as np
from jax import lax
from jax.experimental import pallas as pl
from jax.experimental.pallas import tpu as pltpu

F32 = jnp.float32
BF16 = jnp.bfloat16

LANES = 128
HEAD_DIM = 64
ROPE_THETA = 10000.0
NORM_EPS = 1e-6
NEG_INF = -1e30
LOG2E = math.log2(math.e)

MLA_HEADS = 8
MLA_Q_RANK = 384
MLA_KV_RANK = 256
MLA_NOPE = 64
MLA_ROPE = 32
MLA_QK = MLA_NOPE + MLA_ROPE
DIL_CONFIGS = ((128, 1), (512, 4), (2048, 16))
DIL_HEADS = 4
DIL_W = DIL_HEADS * HEAD_DIM
DIFF_HEADS = 4
MOBA_HEADS = 8
MOBA_BLOCK = 256
MOBA_TOPK = 3
SW_BLOCK = 128
ONES_ROWS = 16

VMEM_LIMIT = 56 * 1024 * 1024

ATTN_BLOCK = MOBA_BLOCK
TOKEN_TILE = 2 * ATTN_BLOCK
MLP_CHUNK = 1024
ADALN_TILE = 1024
SW_BLOCKS_PER_STEP = 8
MLA_HEADS_PER_STEP = 4
DIFF_HEADS_PER_STEP = 2
MOBA_PAIRS_PER_STEP = 2


def _cparams(sem):
    return pltpu.CompilerParams(dimension_semantics=sem, vmem_limit_bytes=VMEM_LIMIT)


def _nt_dot(a, b):
    return lax.dot_general(a, b, (((1,), (1,)), ((), ())), preferred_element_type=F32)


def _tn_dot(a, b):
    return lax.dot_general(a, b, (((0,), (0,)), ((), ())), preferred_element_type=F32)


def _rms(x, w):
    return x * lax.rsqrt(jnp.mean(x * x, axis=-1, keepdims=True) + NORM_EPS) * w


def _lane_iota(shape):
    return lax.broadcasted_iota(jnp.int32, shape, len(shape) - 1)


def _adaln_kernel(c_ref, w_ref, b_ref, o_ref):
    c = c_ref[...]
    cond = c * (1.0 / (1.0 + jnp.exp(-c)))
    o_ref[0] = jnp.dot(cond, w_ref[0], preferred_element_type=F32,
                       precision=lax.Precision.HIGHEST) + b_ref[0]


def _adaln(c, ada_w, ada_b):
    depth, d, n = ada_w.shape
    b = c.shape[0]
    tn = ADALN_TILE
    return pl.pallas_call(
        _adaln_kernel,
        grid=(depth, n // tn),
        in_specs=[pl.BlockSpec((b, d), lambda l, j: (0, 0)),
                  pl.BlockSpec((1, d, tn), lambda l, j: (l, 0, j)),
                  pl.BlockSpec((1, 1, tn), lambda l, j: (l, 0, j))],
        out_specs=pl.BlockSpec((1, b, tn), lambda l, j: (l, 0, j)),
        out_shape=jax.ShapeDtypeStruct((depth, b, n), F32),
        compiler_params=_cparams(("parallel", "parallel")),
        name="adaln",
    )(c, ada_w, ada_b.reshape(depth, 1, n))


def _inv_rms64(x):
    lane = _lane_iota((1, LANES))
    lo = lane < HEAD_DIM
    sq = x * x
    s_lo = jnp.sum(jnp.where(lo, sq, 0.0), axis=-1, keepdims=True)
    s_hi = jnp.sum(jnp.where(lo, 0.0, sq), axis=-1, keepdims=True)
    return lax.rsqrt(jnp.where(lo, s_lo, s_hi) * (1.0 / HEAD_DIM) + NORM_EPS)


def _store_t_blocks(out_ref, xt):
    t = out_ref.shape[-1]
    for j in range(out_ref.shape[1]):
        out_ref[0, j] = xt[:, t * j:t * (j + 1)]


def _t_blocks(b, s, rows, tm, t):
    return (pl.BlockSpec((1, tm // t, rows, t), lambda bi, i: (bi, i, 0, 0)),
            jax.ShapeDtypeStruct((b, s // t, rows, t), BF16))


def _even_in_kernel(x_ref, sh_ref, sc_ref, nw_ref, win_ref, wuq_ref, wuk_ref, wvt_ref,
                    qlat_ref, kvlat_ref, qn_ref, kn_ref, dqn_ref, dkn_ref,
                    cm_ref, sm_ref, c64_ref, s64_ref,
                    q_out, k_out, vt_out, d0_out, d1_out, d2_out, dscr):
    x = x_ref[0]
    tm = x.shape[0]
    h = _rms(x, nw_ref[...]) * (1.0 + sc_ref[0]) + sh_ref[0]
    u = jnp.dot(h.astype(BF16), win_ref[...], preferred_element_type=F32)

    o1 = MLA_Q_RANK
    o2 = o1 + MLA_KV_RANK
    o_kr, o_krr, o_dil = o2, o2 + LANES, o2 + 2 * LANES
    n_dil = 3 * DIL_W * len(DIL_CONFIGS)
    o_rot = o_dil + n_dil
    nq = MLA_HEADS * LANES
    cqn = _rms(u[:, :o1], qlat_ref[...]).astype(BF16)
    qp = jnp.dot(cqn, wuq_ref[...], preferred_element_type=F32)
    ckvn = _rms(u[:, o1:o2], kvlat_ref[...]).astype(BF16)
    kvp = jnp.dot(ckvn, wuk_ref[...], preferred_element_type=F32)
    _store_t_blocks(vt_out, _nt_dot(wvt_ref[...], ckvn).astype(BF16))
    kr, kr_rot = u[:, o_kr:o_kr + LANES], u[:, o_krr:o_krr + LANES]

    cm, sm = cm_ref[0], sm_ref[0]
    q_scale = MLA_QK ** -0.5 * LOG2E
    qgc, qgs = qn_ref[0:1] * cm * q_scale, qn_ref[1:2] * sm * q_scale
    kgc, kgs = kn_ref[0:1] * cm, kn_ref[1:2] * sm
    kr_term = kr_rot * kgs
    for hd in range(MLA_HEADS):
        sl = slice(LANES * hd, LANES * (hd + 1))
        qh = qp[:, sl]
        inv = lax.rsqrt(jnp.sum(qh * qh, -1, keepdims=True) * (1.0 / MLA_QK) + NORM_EPS)
        q_out[0, :, sl] = (inv * (qh * qgc + qp[:, nq + LANES * hd:nq + LANES * (hd + 1)] * qgs)).astype(BF16)
        kh = kvp[:, sl] + kr
        inv = lax.rsqrt(jnp.sum(kh * kh, -1, keepdims=True) * (1.0 / MLA_QK) + NORM_EPS)
        k_out[0, :, sl] = (inv * (kh * kgc + kr_term)).astype(BF16)

    c64, s64 = c64_ref[0], s64_ref[0]
    d_scale = HEAD_DIM ** -0.5 * LOG2E
    dqc, dqs = dqn_ref[0:1] * c64 * d_scale, dqn_ref[1:2] * s64 * d_scale
    dkc, dks = dkn_ref[0:1] * c64, dkn_ref[1:2] * s64
    ncol = DIL_W // LANES
    for g, d_out in enumerate((d0_out, d1_out, d2_out)):
        _, r = DIL_CONFIGS[g]
        base = o_dil + 3 * DIL_W * g
        rbase = o_rot + 2 * DIL_W * g
        for j in range(3 * ncol):
            xc = u[:, base + LANES * j:base + LANES * (j + 1)]
            if j < 2 * ncol:
                xr = u[:, rbase + LANES * j:rbase + LANES * (j + 1)]
                gc, gs = (dqc, dqs) if j < ncol else (dkc, dks)
                xc = _inv_rms64(xc) * (xc * gc + xr * gs)
            if r == 1:
                d_out[0, :, LANES * j:LANES * (j + 1)] = xc.astype(BF16)
            else:
                dscr[j] = xc
                for c in range(r):
                    d_out[0, c, :, LANES * j:LANES * (j + 1)] = dscr[j, pl.ds(c, tm // r, stride=r), :].astype(BF16)


def _row(v):
    return v.reshape(1, -1).astype(F32)


def _even_in(x, sh, sc, nw, win, wuq, wuk, wvt, qlat, kvlat, qn, kn, dqn, dkn, cm, sm, c64, s64, tm, t):
    b, s, d = x.shape
    tok = lambda w: pl.BlockSpec((1, tm, w), lambda bi, i: (bi, i, 0))
    per_b = pl.BlockSpec((1, 1, d), lambda bi, i: (bi, 0, 0))
    full = lambda a: pl.BlockSpec(a.shape, lambda bi, i: (0,) * a.ndim, pipeline_mode=pl.Buffered(1))
    vt_spec, vt_shape = _t_blocks(b, s, wvt.shape[0], tm, t)
    dil_specs, dil_shapes = [], []
    for _, r in DIL_CONFIGS:
        if r == 1:
            dil_specs.append(tok(3 * DIL_W))
            dil_shapes.append(jax.ShapeDtypeStruct((b, s, 3 * DIL_W), BF16))
        else:
            dil_specs.append(pl.BlockSpec((1, r, tm // r, 3 * DIL_W), lambda bi, i: (bi, 0, i, 0)))
            dil_shapes.append(jax.ShapeDtypeStruct((b, r, s // r, 3 * DIL_W), BF16))
    return pl.pallas_call(
        _even_in_kernel,
        grid=(b, s // tm),
        in_specs=[tok(d), per_b, per_b, full(nw), full(win), full(wuq), full(wuk), full(wvt),
                  full(qlat), full(kvlat), full(qn), full(kn), full(dqn), full(dkn),
                  tok(LANES), tok(LANES), tok(LANES), tok(LANES)],
        out_specs=[tok(MLA_HEADS * LANES), tok(MLA_HEADS * LANES), vt_spec] + dil_specs,
        out_shape=[jax.ShapeDtypeStruct((b, s, MLA_HEADS * LANES), BF16),
                   jax.ShapeDtypeStruct((b, s, MLA_HEADS * LANES), BF16), vt_shape] + dil_shapes,
        scratch_shapes=[pltpu.VMEM((3 * DIL_W // LANES, tm, LANES), F32)],
        compiler_params=_cparams(("parallel", "arbitrary")),
        name="even_in_proj",
    )(x, sh, sc, nw, win, wuq, wuk, wvt, qlat, kvlat, qn, kn, dqn, dkn, cm, sm, c64, s64)


def _odd_in_kernel(x_ref, sh_ref, sc_ref, nw_ref, win_ref, wvt_ref, dqn_ref, dkn_ref, mqn_ref, mkn_ref,
                   c64_ref, s64_ref,
                   qc_out, kc_out, qm_out, qmf_out, km_out, vct_out, vmt_out, kmean_out):
    x = x_ref[0]
    tm = x.shape[0]
    h = (_rms(x, nw_ref[...]) * (1.0 + sc_ref[0]) + sh_ref[0]).astype(BF16)
    u = jnp.dot(h, win_ref[...], preferred_element_type=F32)
    vt = _nt_dot(wvt_ref[...], h).astype(BF16)
    nvc = vct_out.shape[2]
    _store_t_blocks(vct_out, vt[:nvc])
    _store_t_blocks(vmt_out, vt[nvc:])
    c64, s64 = c64_ref[0], s64_ref[0]
    scale = HEAD_DIM ** -0.5 * LOG2E
    nqk = DIFF_HEADS * LANES
    nm = MOBA_HEADS * HEAD_DIM
    rot = 2 * nqk + 2 * nm

    def tables(gain_ref, mult):
        return gain_ref[0:1] * c64 * mult, gain_ref[1:2] * s64 * mult

    def roped(off, col, tab):
        a = off + LANES * col
        xc, xr = u[:, a:a + LANES], u[:, rot + a:rot + a + LANES]
        return _inv_rms64(xc) * (xc * tab[0] + xr * tab[1])

    t_qc, t_kc, t_qm, t_km = tables(dqn_ref, scale), tables(dkn_ref, 1.0), tables(mqn_ref, 1.0), tables(mkn_ref, 1.0)
    for col in range(nqk // LANES):
        sl = slice(LANES * col, LANES * (col + 1))
        qc_out[0, :, sl] = roped(0, col, t_qc).astype(BF16)
        kc_out[0, :, sl] = roped(nqk, col, t_kc).astype(BF16)

    base = 2 * nqk
    for col in range(nm // LANES):
        sl = slice(LANES * col, LANES * (col + 1))
        qm = roped(base, col, t_qm)
        qmf_out[0, :, sl] = qm
        qm_out[0, :, sl] = (qm * scale).astype(BF16)
        km = roped(base + nm, col, t_km)
        km_out[0, :, sl] = km.astype(BF16)
        for blk in range(tm // MOBA_BLOCK):
            rows = slice(MOBA_BLOCK * blk, MOBA_BLOCK * (blk + 1))
            kmean_out[0, 0, blk:blk + 1, sl] = jnp.mean(km[rows], axis=0, keepdims=True)


def _odd_in(x, sh, sc, nw, win, wvt, dqn, dkn, mqn, mkn, c64, s64, tm, t):
    b, s, d = x.shape
    tok = lambda w: pl.BlockSpec((1, tm, w), lambda bi, i: (bi, i, 0))
    per_b = pl.BlockSpec((1, 1, d), lambda bi, i: (bi, 0, 0))
    full = lambda a: pl.BlockSpec(a.shape, lambda bi, i: (0,) * a.ndim, pipeline_mode=pl.Buffered(1))
    w = DIFF_HEADS * LANES
    nblk = tm // MOBA_BLOCK
    shp = lambda dt: jax.ShapeDtypeStruct((b, s, w), dt)
    vt_spec, vt_shape = _t_blocks(b, s, wvt.shape[0] // 2, tm, t)
    return pl.pallas_call(
        _odd_in_kernel,
        grid=(b, s // tm),
        in_specs=[tok(d), per_b, per_b, full(nw), full(win), full(wvt), full(dqn), full(dkn), full(mqn), full(mkn),
                  tok(LANES), tok(LANES)],
        out_specs=[tok(w)] * 5 + [vt_spec, vt_spec, pl.BlockSpec((1, 1, nblk, w), lambda bi, i: (bi, i, 0, 0))],
        out_shape=[shp(BF16), shp(BF16), shp(BF16), shp(F32), shp(BF16), vt_shape, vt_shape,
                   jax.ShapeDtypeStruct((b, s // tm, nblk, w), F32)],
        compiler_params=_cparams(("parallel", "arbitrary")),
        name="odd_in_proj",
    )(x, sh, sc, nw, win, wvt, dqn, dkn, mqn, mkn, c64, s64)


def _q_tiles_t(q, split_halves):
    qt = q.astype(F32).T
    if not split_halves:
        return [qt.astype(BF16)]
    row = lax.broadcasted_iota(jnp.int32, qt.shape, 0)
    return [jnp.where((row >= HEAD_DIM) == bool(half), qt, 0.0).astype(BF16) for half in range(2)]


def _flash_t(qts, k_ref, vt_ref, k_slices, v_rows, scr, i, t, past_mask=None):
    m_scr, acc_scr, s_scr, p_scr, a_scr, b_scr = scr
    n = len(qts)
    ones = jnp.ones((ONES_ROWS, t), BF16)

    def qk(blk, s):
        b0 = pl.multiple_of(blk * t, t)
        return jnp.dot(k_ref[0, pl.ds(b0, t), k_slices[s]], qts[s], preferred_element_type=F32)

    def pv(blk, s, p):
        vt = jnp.concatenate([vt_ref[0, blk, v_rows[s], :], ones], axis=0)
        return jnp.dot(vt, p, preferred_element_type=F32)

    def store_scores(slot, s, st):
        s_scr[slot, s] = st
        b_scr[slot, s] = jnp.max(st, axis=0, keepdims=True)

    krow = lax.broadcasted_iota(jnp.int32, (t, t), 0)
    qcol = lax.broadcasted_iota(jnp.int32, (t, t), 1)
    causal = krow <= qcol
    last_past = jnp.maximum(i - 1, 0)
    for s in range(n):
        st = jnp.where(causal, qk(i, s), NEG_INF)
        m = jnp.max(st, axis=0, keepdims=True)
        m_scr[s] = m
        p_scr[0, s] = jnp.exp2(st - m).astype(BF16)
        p_scr[1, s] = jnp.zeros((t, t), BF16)
        a_scr[0, s] = jnp.ones_like(m)
        a_scr[1, s] = jnp.ones_like(m)
        acc_scr[s] = jnp.zeros(acc_scr.shape[1:], F32)
        store_scores(0, s, qk(0, s))
        store_scores(1, s, qk(jnp.minimum(1, last_past), s))

    def softmax_stage(slot, valid, rs, ws, blk):
        for s in range(n):
            cmax = b_scr[2 * rs + slot, s]
            keep = None if past_mask is None else past_mask(s, blk)
            if keep is not None:
                cmax = jnp.where(keep, cmax, NEG_INF)
            if valid is not None:
                cmax = jnp.where(valid, cmax, NEG_INF)
            m_prev = m_scr[s]
            m_new = jnp.maximum(m_prev, cmax)
            a_scr[2 * ws + slot, s] = jnp.exp2(m_prev - m_new)
            p = jnp.exp2(s_scr[2 * rs + slot, s] - m_new).astype(BF16)
            if keep is not None:
                p = jnp.where(keep, p, jnp.zeros_like(p))
            p_scr[2 * ws + slot, s] = p
            m_scr[s] = m_new

    def body(kk, rs, ws):
        first = kk == 0
        pa_blk = jnp.where(first, i, 2 * kk - 2)
        pb_blk = jnp.where(first, i, 2 * kk - 1)
        p_prev = [[p_scr[2 * rs + sl, s] for s in range(n)] for sl in range(2)]
        a_prev = [[a_scr[2 * rs + sl, s] for s in range(n)] for sl in range(2)]
        s_next = [[qk(jnp.minimum(2 * kk + 2 + sl, last_past), s) for s in range(n)] for sl in range(2)]
        pvs = [[pv(blk, s, p_prev[sl][s]) for s in range(n)] for sl, blk in enumerate((pa_blk, pb_blk))]
        softmax_stage(0, None, rs, ws, 2 * kk)
        softmax_stage(1, 2 * kk + 1 < i, rs, ws, jnp.minimum(2 * kk + 1, last_past))
        for sl in range(2):
            for s in range(n):
                store_scores(2 * ws + sl, s, s_next[sl][s])
        for s in range(n):
            acc_scr[s] = a_prev[1][s] * (a_prev[0][s] * acc_scr[s] + pvs[0][s]) + pvs[1][s]

    def two_bodies(kp, carry):
        body(2 * kp, 0, 1)
        body(2 * kp + 1, 1, 0)
        return carry

    nbody = (i + 1) // 2
    lax.fori_loop(0, nbody // 2, two_bodies, 0)

    @pl.when(nbody % 2 == 1)
    def _():
        body(nbody - 1, 0, 1)

    fs = nbody % 2
    kl = jnp.maximum(nbody - 1, 0)
    none = i == 0
    fa_blk = jnp.where(none, i, 2 * kl)
    fb_blk = jnp.where(none, i, jnp.minimum(2 * kl + 1, last_past))
    fb_scale = jnp.where(jnp.logical_or(none, i % 2 == 0), 1.0, 0.0)
    for s in range(n):
        acc_scr[s] = (a_scr[2 * fs + 1, s] * (a_scr[2 * fs, s] * acc_scr[s] + pv(fa_blk, s, p_scr[2 * fs, s]))
                      + fb_scale * pv(fb_blk, s, p_scr[2 * fs + 1, s]))


def _flash_scratch(n, nv, t):
    return [pltpu.VMEM((n, 1, t), F32), pltpu.VMEM((n, nv + ONES_ROWS, t), F32), pltpu.VMEM((4, n, t, t), F32),
            pltpu.VMEM((4, n, t, t), BF16), pltpu.VMEM((4, n, 1, t), F32), pltpu.VMEM((4, n, 1, t), F32)]


def _normalised(acc_scr, s, nv):
    return acc_scr[s, :nv, :] / acc_scr[s, nv:nv + 1, :]


def _half_mask(x, half):
    lane = _lane_iota((1, LANES))
    return jnp.where((lane >= HEAD_DIM) == bool(half), x, jnp.zeros_like(x))


def _mla_kernel(q_ref, k_ref, vt_ref, ot_ref, *scr, t, nh):
    i = pl.program_id(2)
    qs = [_q_tiles_t(q_ref[0, :, LANES * s:LANES * (s + 1)], False)[0] for s in range(nh)]
    ks = [slice(LANES * s, LANES * (s + 1)) for s in range(nh)]
    vr = [slice(HEAD_DIM * s, HEAD_DIM * (s + 1)) for s in range(nh)]
    _flash_t(qs, k_ref, vt_ref, ks, vr, scr, i, t)
    for s in range(nh):
        ot_ref[0, 0, vr[s], :] = _normalised(scr[1], s, HEAD_DIM).astype(ot_ref.dtype)


def _mla_attention(q, k, vt, nh):
    b, s, _ = q.shape
    _, nb, nv, t = vt.shape
    return pl.pallas_call(
        functools.partial(_mla_kernel, t=t, nh=nh),
        grid=(b, MLA_HEADS // nh, nb),
        in_specs=[pl.BlockSpec((1, t, nh * LANES), lambda bi, p, i: (bi, i, p)),
                  pl.BlockSpec((1, s, nh * LANES), lambda bi, p, i: (bi, 0, p)),
                  pl.BlockSpec((1, nb, nh * HEAD_DIM, t), lambda bi, p, i: (bi, 0, p, 0))],
        out_specs=pl.BlockSpec((1, 1, nh * HEAD_DIM, t), lambda bi, p, i: (bi, i, p, 0)),
        out_shape=jax.ShapeDtypeStruct((b, nb, nv, t), BF16),
        scratch_shapes=_flash_scratch(nh, HEAD_DIM, t),
        compiler_params=_cparams(("parallel", "parallel", "arbitrary")),
        name="mla_attention",
    )(q, k, vt)


def _diff_kernel(q_ref, k_ref, vt_ref, lam_ref, sub_ref, ot_ref, *scr, t, nh, lam_init):
    i = pl.program_id(2)
    qs, ks, vr = [], [], []
    for hd in range(nh):
        cols = slice(LANES * hd, LANES * (hd + 1))
        qs += _q_tiles_t(q_ref[0, :, cols], True)
        ks += [cols, cols]
        vr += [cols, cols]
    _flash_t(qs, k_ref, vt_ref, ks, vr, scr, i, t)
    lv = lam_ref[...]
    lam = (jnp.exp(jnp.sum(lv[0:1] * lv[1:2], keepdims=True))
           - jnp.exp(jnp.sum(lv[2:3] * lv[3:4], keepdims=True)) + lam_init)
    for hd in range(nh):
        o = _normalised(scr[1], 2 * hd, LANES) - lam * _normalised(scr[1], 2 * hd + 1, LANES)
        o = o * lax.rsqrt(jnp.mean(o * o, axis=0, keepdims=True) + NORM_EPS) * sub_ref[...]
        ot_ref[0, 0, LANES * hd:LANES * (hd + 1), :] = (o * (1.0 - lam_init)).astype(ot_ref.dtype)


def _diff_attention(q, k, vt, lam_rows, subln_col, lam_init, nh):
    b, s, w = q.shape
    _, nb, _, t = vt.shape
    return pl.pallas_call(
        functools.partial(_diff_kernel, t=t, nh=nh, lam_init=lam_init),
        grid=(b, DIFF_HEADS // nh, nb),
        in_specs=[pl.BlockSpec((1, t, nh * LANES), lambda bi, p, i: (bi, i, p)),
                  pl.BlockSpec((1, s, nh * LANES), lambda bi, p, i: (bi, 0, p)),
                  pl.BlockSpec((1, nb, nh * LANES, t), lambda bi, p, i: (bi, 0, p, 0)),
                  pl.BlockSpec(lam_rows.shape, lambda bi, p, i: (0, 0)),
                  pl.BlockSpec(subln_col.shape, lambda bi, p, i: (0, 0))],
        out_specs=pl.BlockSpec((1, 1, nh * LANES, t), lambda bi, p, i: (bi, i, p, 0)),
        out_shape=jax.ShapeDtypeStruct((b, nb, w, t), BF16),
        scratch_shapes=_flash_scratch(2 * nh, LANES, t),
        compiler_params=_cparams(("parallel", "parallel", "arbitrary")),
        name="diff_attention",
    )(q, k, vt, lam_rows, subln_col)


def _moba_kernel(q_ref, qf_ref, k_ref, vt_ref, kmean_ref, ot_ref, sel_scr, *scr, t, npair):
    i = pl.program_id(2)
    nb = kmean_ref.shape[1]
    brow = lax.broadcasted_iota(jnp.int32, (nb, t), 0).astype(F32)
    past = brow < i.astype(F32)
    qs, ks, vr = [], [], []
    for pr in range(npair):
        cols = slice(LANES * pr, LANES * (pr + 1))
        qf, kmean = qf_ref[0, :, cols], kmean_ref[0, :, cols]
        qs += _q_tiles_t(q_ref[0, :, cols], True)
        for half in range(2):
            s = 2 * pr + half
            gate = lax.dot_general(kmean, _half_mask(qf, half), (((1,), (1,)), ((), ())),
                                   preferred_element_type=F32, precision=lax.Precision.HIGHEST)
            gate = jnp.where(past, gate, NEG_INF)
            sel = jnp.zeros((nb, t), F32)
            for _ in range(MOBA_TOPK):
                top = jnp.max(gate, axis=0, keepdims=True)
                first = jnp.min(jnp.where(gate == top, brow, float(nb)), axis=0, keepdims=True)
                pick = brow == first
                sel = jnp.where(pick, 1.0, sel)
                gate = jnp.where(pick, 2 * NEG_INF, gate)
            sel_scr[s] = jnp.where(past, sel, 0.0)
            ks.append(cols)
            vr.append(slice(HEAD_DIM * s, HEAD_DIM * (s + 1)))
    _flash_t(qs, k_ref, vt_ref, ks, vr, scr, i, t,
             past_mask=lambda s, jj: sel_scr[s, pl.ds(jj, 1), :] > 0.5)
    for s in range(2 * npair):
        ot_ref[0, 0, vr[s], :] = _normalised(scr[1], s, HEAD_DIM).astype(ot_ref.dtype)


def _moba_attention(q, qf, k, vt, kmean, npair):
    b, s, w = q.shape
    _, nb, _, t = vt.shape
    nkm = kmean.shape[1]
    lanes = npair * LANES
    return pl.pallas_call(
        functools.partial(_moba_kernel, t=t, npair=npair),
        grid=(b, w // lanes, nb),
        in_specs=[pl.BlockSpec((1, t, lanes), lambda bi, p, i: (bi, i, p)),
                  pl.BlockSpec((1, t, lanes), lambda bi, p, i: (bi, i, p)),
                  pl.BlockSpec((1, s, lanes), lambda bi, p, i: (bi, 0, p)),
                  pl.BlockSpec((1, nb, lanes, t), lambda bi, p, i: (bi, 0, p, 0)),
                  pl.BlockSpec((1, nkm, lanes), lambda bi, p, i: (bi, 0, p))],
        out_specs=pl.BlockSpec((1, 1, lanes, t), lambda bi, p, i: (bi, i, p, 0)),
        out_shape=jax.ShapeDtypeStruct((b, nb, w, t), BF16),
        scratch_shapes=[pltpu.VMEM((2 * npair, nkm, t), F32)] + _flash_scratch(2 * npair, HEAD_DIM, t),
        compiler_params=_cparams(("parallel", "parallel", "arbitrary")),
        name="moba_attention",
    )(q, qf, k, vt, kmean)


def _sw_kernel(q_ref, kc_ref, vc_ref, kp_ref, vp_ref, o_ref, l_ref, *, nbk):
    i = pl.program_id(1)
    blk = SW_BLOCK
    qi = lax.broadcasted_iota(jnp.int32, (blk, 2 * blk), 0)
    kj = lax.broadcasted_iota(jnp.int32, (blk, 2 * blk), 1)
    band = (kj >= qi) & (kj <= qi + blk)
    lane = _lane_iota((1, LANES))
    heads = [(pair, half) for pair in range(DIL_W // LANES) for half in range(2)]

    def scores(n):
        rows = slice(blk * n, blk * (n + 1))
        q = q_ref[0, rows, :]
        if n == 0:
            kprev, valid = kp_ref[0], band & ((kj >= blk) | (i > 0))
        else:
            kprev, valid = kc_ref[0, blk * (n - 1):blk * n, :], band
        kk = jnp.concatenate([kprev, kc_ref[0, rows, :]], axis=0)
        out = []
        for pair, half in heads:
            sl = slice(LANES * pair, LANES * (pair + 1))
            out.append(jnp.where(valid, _nt_dot(_half_mask(q[:, sl], half), kk[:, sl]), NEG_INF))
        return out

    nxt = scores(0)
    for n in range(nbk):
        cur, rows = nxt, slice(blk * n, blk * (n + 1))
        if n + 1 < nbk:
            nxt = scores(n + 1)
        vprev = vp_ref[0] if n == 0 else vc_ref[0, blk * (n - 1):blk * n, :]
        vv = jnp.concatenate([vprev, vc_ref[0, rows, :]], axis=0)
        outs, lses = [], []
        for (pair, half), sc in zip(heads, cur):
            m = jnp.max(sc, axis=-1, keepdims=True)
            p = jnp.exp2(sc - m)
            l = jnp.sum(p, axis=-1, keepdims=True)
            sl = slice(LANES * pair, LANES * (pair + 1))
            outs.append(jnp.dot(p.astype(BF16), vv[:, sl], preferred_element_type=F32) / l)
            lses.append(m + jnp.log2(l))
        for pair in range(DIL_W // LANES):
            sl = slice(LANES * pair, LANES * (pair + 1))
            o_ref[0, rows, sl] = jnp.where(lane < HEAD_DIM, outs[2 * pair], outs[2 * pair + 1])
            l_ref[0, rows, sl] = jnp.where(lane < HEAD_DIM, lses[2 * pair], lses[2 * pair + 1])


def _sliding_window(dil, nbk, name):
    lead, (m, w3) = dil.shape[:-2], dil.shape[-2:]
    seqs = dil.reshape((-1, m, w3))
    nbk = min(nbk, m // SW_BLOCK)
    rows = SW_BLOCK * nbk
    cur = lambda off: pl.BlockSpec((1, rows, DIL_W), lambda n, i: (n, i, off))
    prev = lambda off: pl.BlockSpec((1, SW_BLOCK, DIL_W), lambda n, i: (n, jnp.maximum(i * nbk - 1, 0), off))
    out_spec = pl.BlockSpec((1, rows, DIL_W), lambda n, i: (n, i, 0))
    o, l = pl.pallas_call(
        functools.partial(_sw_kernel, nbk=nbk),
        grid=(seqs.shape[0], m // rows),
        in_specs=[cur(0), cur(1), cur(2), prev(1), prev(2)],
        out_specs=[out_spec, out_spec],
        out_shape=[jax.ShapeDtypeStruct((seqs.shape[0], m, DIL_W), F32)] * 2,
        compiler_params=_cparams(("parallel", "arbitrary")),
        name=name,
    )(seqs, seqs, seqs, seqs, seqs)
    return o.reshape(lead + (m, DIL_W)), l.reshape(lead + (m, DIL_W))


def _mlp_tail(y, mlp_refs):
    sh_ref, sc_ref, g_ref, nw_ref, w1_ref, w2_ref = mlp_refs
    h = (_rms(y, nw_ref[...]) * (1.0 + sc_ref[0]) + sh_ref[0]).astype(BF16)
    acc = jnp.zeros(y.shape, F32)
    for c in range(w1_ref.shape[1] // MLP_CHUNK):
        cols = slice(MLP_CHUNK * c, MLP_CHUNK * (c + 1))
        a = jnp.maximum(jnp.dot(h, w1_ref[:, cols], preferred_element_type=F32), 0.0)
        acc = acc + jnp.dot((a * a).astype(BF16), w2_ref[cols, :], preferred_element_type=F32)
    return y + g_ref[0] * acc


def _proj_t(ot_ref, w_rows):
    return jnp.concatenate([_tn_dot(ot_ref[0, j], w_rows) for j in range(ot_ref.shape[1])], axis=0)


def _mlp_specs(mlp_args, d):
    per_b = pl.BlockSpec((1, 1, d), lambda bi, i: (bi, 0, 0))
    const = lambda a: pl.BlockSpec(a.shape, lambda bi, i: (0, 0), pipeline_mode=pl.Buffered(1))
    sh, sc, g2, nw, w1, w2 = mlp_args
    return [per_b, per_b, per_b, pl.BlockSpec(nw.shape, lambda bi, i: (0, 0)), const(w1), const(w2)]


def _even_out_kernel(x_ref, g_ref, oat_ref, o0_ref, o1_ref, o2_ref, l0_ref, l1_ref, l2_ref, w_ref,
                     *rest):
    *mlp_refs, y_ref, tscr = rest
    tm = x_ref.shape[1]
    ncol = DIL_W // LANES

    def token_major(ref, slot):
        r = ref.shape[1]
        for c in range(r):
            for j in range(ncol):
                tscr[slot * ncol + j, pl.ds(c, tm // r, stride=r), :] = ref[0, c, :, LANES * j:LANES * (j + 1)]
        return jnp.concatenate([tscr[slot * ncol + j] for j in range(ncol)], axis=1)

    o0, l0 = o0_ref[0], l0_ref[0]
    o1, l1 = token_major(o1_ref, 0), token_major(l1_ref, 1)
    o2, l2 = token_major(o2_ref, 2), token_major(l2_ref, 3)
    top = jnp.maximum(jnp.maximum(l0, l1), l2)
    w0, w1, w2 = jnp.exp2(l0 - top), jnp.exp2(l1 - top), jnp.exp2(l2 - top)
    ob = (w0 * o0 + w1 * o1 + w2 * o2) / (w0 + w1 + w2)
    na = oat_ref.shape[2]
    y = _proj_t(oat_ref, w_ref[:na, :]) + jnp.dot(ob.astype(BF16), w_ref[na:, :], preferred_element_type=F32)
    y_ref[0] = _mlp_tail(x_ref[0] + g_ref[0] * y, mlp_refs)


def _t_spec(a, tm):
    return pl.BlockSpec((1, tm // a.shape[-1]) + a.shape[2:], lambda bi, i: (bi, i, 0, 0))


def _even_out(x, g1, oat, o_dil, l_dil, w_out, mlp_args, tm):
    b, s, d = x.shape
    tok = lambda w: pl.BlockSpec((1, tm, w), lambda bi, i: (bi, i, 0))

    def dil_spec(a):
        if a.ndim == 3:
            return tok(DIL_W)
        r = a.shape[1]
        return pl.BlockSpec((1, r, tm // r, DIL_W), lambda bi, i: (bi, 0, i, 0))

    return pl.pallas_call(
        _even_out_kernel,
        grid=(b, s // tm),
        in_specs=[tok(d), pl.BlockSpec((1, 1, d), lambda bi, i: (bi, 0, 0)), _t_spec(oat, tm)]
                 + [dil_spec(a) for a in (*o_dil, *l_dil)]
                 + [pl.BlockSpec(w_out.shape, lambda bi, i: (0, 0), pipeline_mode=pl.Buffered(1))]
                 + _mlp_specs(mlp_args, d),
        out_specs=tok(d),
        out_shape=jax.ShapeDtypeStruct((b, s, d), F32),
        scratch_shapes=[pltpu.VMEM((4 * DIL_W // LANES, tm, LANES), F32)],
        compiler_params=_cparams(("parallel", "arbitrary")),
        name="even_out_mlp",
    )(x, g1, oat, *o_dil, *l_dil, w_out, *mlp_args)


def _odd_out_kernel(x_ref, g_ref, oct_ref, odt_ref, w_ref, *rest):
    *mlp_refs, y_ref = rest
    nc = oct_ref.shape[2]
    y = _proj_t(oct_ref, w_ref[:nc, :]) + _proj_t(odt_ref, w_ref[nc:, :])
    y_ref[0] = _mlp_tail(x_ref[0] + g_ref[0] * y, mlp_refs)


def _odd_out(x, g1, oct, odt, w_out, mlp_args, tm):
    b, s, d = x.shape
    tok = lambda w: pl.BlockSpec((1, tm, w), lambda bi, i: (bi, i, 0))
    return pl.pallas_call(
        _odd_out_kernel,
        grid=(b, s // tm),
        in_specs=[tok(d), pl.BlockSpec((1, 1, d), lambda bi, i: (bi, 0, 0)), _t_spec(oct, tm), _t_spec(odt, tm),
                  pl.BlockSpec(w_out.shape, lambda bi, i: (0, 0), pipeline_mode=pl.Buffered(1))]
                 + _mlp_specs(mlp_args, d),
        out_specs=tok(d),
        out_shape=jax.ShapeDtypeStruct((b, s, d), F32),
        compiler_params=_cparams(("parallel", "arbitrary")),
        name="odd_out_mlp",
    )(x, g1, oct, odt, w_out, *mlp_args)


def _rope_tables(positions):
    pos = positions.astype(F32)[:, :, None]
    half = HEAD_DIM // 2
    inv = ROPE_THETA ** (-jnp.arange(half, dtype=F32) / half)
    ang = pos * inv[_freq_order()]
    c, s = jnp.cos(ang), jnp.sin(ang)

    def tiled(x, width):
        reps = LANES // width
        return jnp.broadcast_to(x[:, :, None, :], x.shape[:2] + (reps, width)).reshape(x.shape[:2] + (LANES,))

    lane = np.arange(LANES)
    sign = np.where(lane % HEAD_DIM < half, -1.0, 1.0).astype(np.float32)
    c64, s64 = tiled(c, half), tiled(s, half) * sign
    hm = MLA_ROPE // 2
    rope = (lane >= MLA_NOPE) & (lane < MLA_QK)
    sign_m = np.where(lane < MLA_NOPE + hm, -1.0, 1.0).astype(np.float32)
    cm = jnp.where(rope, tiled(c[..., :hm], hm), 1.0)
    sm = jnp.where(rope, tiled(s[..., :hm], hm) * sign_m, 0.0)
    return cm, sm, c64, s64


def _freq_order():
    half = HEAD_DIM // 2
    return np.concatenate([np.arange(0, half, 2), np.arange(1, half, 2)])


def _head_perm(n):
    within = np.concatenate([_freq_order(), HEAD_DIM // 2 + _freq_order()])
    idx = np.arange(n)
    return idx // HEAD_DIM * HEAD_DIM + within[idx % HEAD_DIM]


def _partner(n, width):
    idx = np.arange(n)
    return np.where(idx % width < width // 2, idx + width // 2, idx - width // 2)


def _even_weights(w_in, w_uq, w_ukv, qn, kn):
    w_in, w_uq, w_ukv = w_in.astype(BF16), w_uq.astype(BF16), w_ukv.astype(BF16)
    o2 = MLA_Q_RANK + MLA_KV_RANK
    o3 = o2 + MLA_ROPE
    d = w_in.shape[0]
    p_rope = _partner(MLA_ROPE, MLA_ROPE)
    zeros = lambda n: jnp.zeros((d, n), w_in.dtype)
    kr = w_in[:, o2:o3]
    tail = LANES - MLA_QK
    krp = jnp.concatenate([zeros(MLA_NOPE), kr, zeros(tail)], axis=1)
    krp_rot = jnp.concatenate([zeros(MLA_NOPE), kr[:, p_rope], zeros(tail)], axis=1)
    dil = w_in[:, o3:].reshape(d, 3, len(DIL_CONFIGS), DIL_W)
    dil = jnp.concatenate([dil[:, :2][..., _head_perm(DIL_W)], dil[:, 2:]], axis=1)
    dil_main = dil.transpose(0, 2, 1, 3).reshape(d, -1)
    dil_rot = dil[:, :2][..., _partner(DIL_W, HEAD_DIM)].transpose(0, 2, 1, 3).reshape(d, -1)
    win = jnp.concatenate([w_in[:, :o2], krp, krp_rot, dil_main, dil_rot], axis=1)

    padq = ((0, 0), (0, 0), (0, tail))
    wuq_rot = jnp.concatenate([w_uq[:, :, :MLA_NOPE], w_uq[:, :, MLA_NOPE:][:, :, p_rope]], axis=-1)
    wuq = jnp.concatenate([jnp.pad(w_uq, padq).reshape(MLA_Q_RANK, -1),
                           jnp.pad(wuq_rot, padq).reshape(MLA_Q_RANK, -1)], axis=1)
    wuk = jnp.pad(w_ukv[:, :, :MLA_NOPE], ((0, 0), (0, 0), (0, LANES - MLA_NOPE))).reshape(MLA_KV_RANK, -1)
    wvt = w_ukv[:, :, MLA_NOPE:].reshape(MLA_KV_RANK, -1).T

    def gains(g):
        rot = jnp.concatenate([g[:MLA_NOPE], g[MLA_NOPE:][p_rope]])
        return jnp.stack([jnp.pad(g, (0, tail)), jnp.pad(rot, (0, tail))]).astype(F32)

    return win.astype(BF16), wuq.astype(BF16), wuk.astype(BF16), wvt.astype(BF16), gains(qn), gains(kn)


def _odd_weights(w_in):
    w_in = w_in.astype(BF16)
    nqk = DIFF_HEADS * LANES
    nm = MOBA_HEADS * HEAD_DIM
    main = jnp.concatenate([w_in[:, :2 * nqk], w_in[:, 3 * nqk:3 * nqk + 2 * nm]], axis=1)
    main = main[:, _head_perm(main.shape[1])]
    win =jnp.concatenate([main, main[:, _partner(main.shape[1], HEAD_DIM)]], axis=1)
    wvt = jnp.concatenate([w_in[:, 2 * nqk:3 * nqk], w_in[:, 3 * nqk + 2 * nm:]], axis=1).T
    return win.astype(BF16), wvt.astype(BF16)


def _gains64(g):
    g = g[_head_perm(HEAD_DIM)]
    rot = g[_partner(HEAD_DIM, HEAD_DIM)]
    return jnp.stack([jnp.concatenate([g, g]), jnp.concatenate([rot, rot])]).astype(F32)


def kernel(x, c, positions, ada_w, ada_b, norm_mix, norm_mlp, mlp_w1, mlp_w2, even_w_in, even_w_out, mla_q_lat_norm, mla_kv_lat_norm, mla_w_uq, mla_w_ukv, mla_q_norm, mla_k_norm, dil_q_norm, dil_k_norm, odd_w_in, odd_w_out, diff_q_norm, diff_k_norm, diff_lambda, diff_subln, moba_q_norm, moba_k_norm):
    b, s, d = x.shape
    depth = ada_w.shape[0]
    t_attn, tm_in, tm_out = ATTN_BLOCK, TOKEN_TILE, TOKEN_TILE

    mod = _adaln(c, ada_w, ada_b)
    cm, sm, c64, s64 = _rope_tables(positions)

    for layer in range(depth):
        sh1, sc1, g1, sh2, sc2, g2 = [mod[layer, :, d * t:d * (t + 1)].reshape(b, 1, d) for t in range(6)]
        mlp_args = (sh2, sc2, g2, _row(norm_mlp[layer]), mlp_w1[layer].astype(BF16), mlp_w2[layer].astype(BF16))
        i = layer // 2
        if layer % 2 == 0:
            win, wuq, wuk, wvt, qn, kn = _even_weights(even_w_in[i], mla_w_uq[i], mla_w_ukv[i],
                                                       mla_q_norm[i], mla_k_norm[i])
            q, k, vt, *dils = _even_in(x, sh1, sc1, _row(norm_mix[layer]), win, wuq, wuk, wvt,
                                       _row(mla_q_lat_norm[i]), _row(mla_kv_lat_norm[i]), qn, kn,
                                       _gains64(dil_q_norm[i]), _gains64(dil_k_norm[i]), cm, sm, c64, s64, tm_in, t_attn)
            o_at = _mla_attention(q, k, vt, MLA_HEADS_PER_STEP)
            o_dil, l_dil = zip(*[_sliding_window(dg, SW_BLOCKS_PER_STEP, f"sliding_window_g{g}")
                                 for g, dg in enumerate(dils)])
            x = _even_out(x, g1, o_at, o_dil, l_dil, even_w_out[i].astype(BF16), mlp_args, tm_out)
        else:
            lam_init = 0.8 - 0.6 * math.exp(-0.3 * layer)
            win, wvt = _odd_weights(odd_w_in[i])
            qc, kc, qm, qmf, km, vct, vmt, kmean = _odd_in(
                x, sh1, sc1, _row(norm_mix[layer]), win, wvt,
                _gains64(diff_q_norm[i]), _gains64(diff_k_norm[i]), _gains64(moba_q_norm[i]),
                _gains64(moba_k_norm[i]),
                c64, s64, tm_in, t_attn)
            o_ct = _diff_attention(qc, kc, vct, diff_lambda[i].astype(F32),
                                   diff_subln[i].reshape(-1, 1).astype(F32), lam_init, DIFF_HEADS_PER_STEP)
            o_dt = _moba_attention(qm, qmf, km, vmt, kmean.reshape(b, s // MOBA_BLOCK, kmean.shape[-1]),
                                   MOBA_PAIRS_PER_STEP)
            x = _odd_out(x, g1, o_ct, o_dt, odd_w_out[i].astype(BF16), mlp_args, tm_out)
    return x
```

```python
import functools
import math

import jax
import jax.numpy as jnp
import numpy as np
from jax import lax
from jax.experimental import pallas as pl
from jax.experimental.pallas import tpu as pltpu

F32 = jnp.float32
BF16 = jnp.bfloat16

LANES = 128
HEAD_DIM = 64
ROPE_THETA = 10000.0
NORM_EPS = 1e-6
NEG_INF = -1e30
LOG2E = math.log2(math.e)

MLA_HEADS = 8
MLA_Q_RANK = 384
MLA_KV_RANK = 256
MLA_NOPE = 64
MLA_ROPE = 32
MLA_QK = MLA_NOPE + MLA_ROPE
DIL_CONFIGS = ((128, 1), (512, 4), (2048, 16))
DIL_HEADS = 4
DIL_W = DIL_HEADS * HEAD_DIM
DIFF_HEADS = 4
MOBA_HEADS = 8
MOBA_BLOCK = 256
MOBA_TOPK = 3
SW_BLOCK = 128
ONES_ROWS = 16

VMEM_LIMIT = 56 * 1024 * 1024

ATTN_BLOCK = MOBA_BLOCK
TOKEN_TILE = 2 * ATTN_BLOCK
MLP_CHUNK = 1024
ADALN_TILE = 2048
SW_BLOCKS_PER_STEP = 8
MLA_HEADS_PER_STEP = 4
DIFF_HEADS_PER_STEP = 2
MOBA_PAIRS_PER_STEP = 2


def _cparams(sem):
    return pltpu.CompilerParams(dimension_semantics=sem, vmem_limit_bytes=VMEM_LIMIT)


def _nt_dot(a, b):
    return lax.dot_general(a, b, (((1,), (1,)), ((), ())), preferred_element_type=F32)


def _tn_dot(a, b):
    return lax.dot_general(a, b, (((0,), (0,)), ((), ())), preferred_element_type=F32)


def _rms(x, w):
    return x * lax.rsqrt(jnp.mean(x * x, axis=-1, keepdims=True) + NORM_EPS) * w


def _lane_iota(shape):
    return lax.broadcasted_iota(jnp.int32, shape, len(shape) - 1)


def _adaln_kernel(c_ref, w_ref, b_ref, o_ref):
    c = c_ref[...]
    cond = c * (1.0 / (1.0 + jnp.exp(-c)))
    o_ref[0] = jnp.dot(cond, w_ref[0], preferred_element_type=F32,
                       precision=lax.Precision.HIGHEST) + b_ref[0]


def _adaln(c, ada_w, ada_b):
    depth, d, n = ada_w.shape
    b = c.shape[0]
    tn = ADALN_TILE
    return pl.pallas_call(
        _adaln_kernel,
        grid=(depth, n // tn),
        in_specs=[pl.BlockSpec((b, d), lambda l, j: (0, 0)),
                  pl.BlockSpec((1, d, tn), lambda l, j: (l, 0, j)),
                  pl.BlockSpec((1, 1, tn), lambda l, j: (l, 0, j))],
        out_specs=pl.BlockSpec((1, b, tn), lambda l, j: (l, 0, j)),
        out_shape=jax.ShapeDtypeStruct((depth, b, n), F32),
        compiler_params=_cparams(("parallel", "parallel")),
        name="adaln",
    )(c, ada_w, ada_b.reshape(depth, 1, n))


def _inv_rms64(x):
    lane = _lane_iota((1, LANES))
    lo = lane < HEAD_DIM
    sq = x * x
    s_lo = jnp.sum(jnp.where(lo, sq, 0.0), axis=-1, keepdims=True)
    s_hi = jnp.sum(jnp.where(lo, 0.0, sq), axis=-1, keepdims=True)
    return lax.rsqrt(jnp.where(lo, s_lo, s_hi) * (1.0 / HEAD_DIM) + NORM_EPS)


def _store_t_blocks(out_ref, xt):
    t = out_ref.shape[-1]
    for j in range(out_ref.shape[1]):
        out_ref[0, j] = xt[:, t * j:t * (j + 1)]


def _store_col_t(out_ref, col, x):
    t = out_ref.shape[-1]
    for j in range(out_ref.shape[1]):
        out_ref[0, j, LANES * col:LANES * (col + 1), :] = x[t * j:t * (j + 1)].T.astype(out_ref.dtype)


def _t_blocks(b, s, rows, tm, t):
    return (pl.BlockSpec((1, tm // t, rows, t), lambda bi, i: (bi, i, 0, 0)),
            jax.ShapeDtypeStruct((b, s // t, rows, t), BF16))


def _even_in_kernel(x_ref, sh_ref, sc_ref, nw_ref, win_ref, wuq_ref, wuk_ref, wvt_ref,
                    qlat_ref, kvlat_ref, qn_ref, kn_ref, dqn_ref, dkn_ref,
                    cm_ref, sm_ref, c64_ref, s64_ref,
                    qt_out, k_out, vt_out, d0_out, d1_out, d2_out, dscr):
    x = x_ref[0]
    tm = x.shape[0]
    h = _rms(x, nw_ref[...]) * (1.0 + sc_ref[0]) + sh_ref[0]
    u = jnp.dot(h.astype(BF16), win_ref[...], preferred_element_type=F32)

    o1 = MLA_Q_RANK
    o2 = o1 + MLA_KV_RANK
    o_kr, o_krr, o_dil = o2, o2 + LANES, o2 + 2 * LANES
    n_dil = 3 * DIL_W * len(DIL_CONFIGS)
    o_rot = o_dil + n_dil
    nq = MLA_HEADS * LANES
    cqn = _rms(u[:, :o1], qlat_ref[...]).astype(BF16)
    qp = jnp.dot(cqn, wuq_ref[...], preferred_element_type=F32)
    ckvn = _rms(u[:, o1:o2], kvlat_ref[...]).astype(BF16)
    kvp = jnp.dot(ckvn, wuk_ref[...], preferred_element_type=F32)
    _store_t_blocks(vt_out, _nt_dot(wvt_ref[...], ckvn).astype(BF16))
    kr, kr_rot = u[:, o_kr:o_kr + LANES], u[:, o_krr:o_krr + LANES]

    cm, sm = cm_ref[0], sm_ref[0]
    q_scale = MLA_QK ** -0.5 * LOG2E
    qgc, qgs = qn_ref[0:1] * cm * q_scale, qn_ref[1:2] * sm * q_scale
    kgc, kgs = kn_ref[0:1] * cm, kn_ref[1:2] * sm
    kr_term = kr_rot * kgs
    for hd in range(MLA_HEADS):
        sl = slice(LANES * hd, LANES * (hd + 1))
        qh = qp[:, sl]
        inv = lax.rsqrt(jnp.sum(qh * qh, -1, keepdims=True) * (1.0 / MLA_QK) + NORM_EPS)
        _store_col_t(qt_out, hd, inv * (qh * qgc + qp[:, nq + LANES * hd:nq + LANES * (hd + 1)] * qgs))
        kh = kvp[:, sl] + kr
        inv = lax.rsqrt(jnp.sum(kh * kh, -1, keepdims=True) * (1.0 / MLA_QK) + NORM_EPS)
        k_out[0, :, sl] = (inv * (kh * kgc + kr_term)).astype(BF16)

    c64, s64 = c64_ref[0], s64_ref[0]
    d_scale = HEAD_DIM ** -0.5 * LOG2E
    dqc, dqs = dqn_ref[0:1] * c64 * d_scale, dqn_ref[1:2] * s64 * d_scale
    dkc, dks = dkn_ref[0:1] * c64, dkn_ref[1:2] * s64
    ncol = DIL_W // LANES
    for g, d_out in enumerate((d0_out, d1_out, d2_out)):
        _, r = DIL_CONFIGS[g]
        base = o_dil + 3 * DIL_W * g
        rbase = o_rot + 2 * DIL_W * g
        for j in range(3 * ncol):
            xc = u[:, base + LANES * j:base + LANES * (j + 1)]
            if j < 2 * ncol:
                xr = u[:, rbase + LANES * j:rbase + LANES * (j + 1)]
                gc, gs = (dqc, dqs) if j < ncol else (dkc, dks)
                xc = _inv_rms64(xc) * (xc * gc + xr * gs)
            if r == 1:
                d_out[0, :, LANES * j:LANES * (j + 1)] = xc.astype(BF16)
            else:
                dscr[j] = xc
                for c in range(r):
                    d_out[0, c, :, LANES * j:LANES * (j + 1)] = dscr[j, pl.ds(c, tm // r, stride=r), :].astype(BF16)


def _row(v):
    return v.reshape(1, -1).astype(F32)


def _even_in(x, sh, sc, nw, win, wuq, wuk, wvt, qlat, kvlat, qn, kn, dqn, dkn, cm, sm, c64, s64, tm, t):
    b, s, d = x.shape
    tok = lambda w: pl.BlockSpec((1, tm, w), lambda bi, i: (bi, i, 0))
    per_b = pl.BlockSpec((1, 1, d), lambda bi, i: (bi, 0, 0))
    full = lambda a: pl.BlockSpec(a.shape, lambda bi, i: (0,) * a.ndim, pipeline_mode=pl.Buffered(1))
    vt_spec, vt_shape = _t_blocks(b, s, wvt.shape[0], tm, t)
    qt_spec, qt_shape = _t_blocks(b, s, MLA_HEADS * LANES, tm, t)
    dil_specs, dil_shapes = [], []
    for _, r in DIL_CONFIGS:
        if r == 1:
            dil_specs.append(tok(3 * DIL_W))
            dil_shapes.append(jax.ShapeDtypeStruct((b, s, 3 * DIL_W), BF16))
        else:
            dil_specs.append(pl.BlockSpec((1, r, tm // r, 3 * DIL_W), lambda bi, i: (bi, 0, i, 0)))
            dil_shapes.append(jax.ShapeDtypeStruct((b, r, s // r, 3 * DIL_W), BF16))
    return pl.pallas_call(
        _even_in_kernel,
        grid=(b, s // tm),
        in_specs=[tok(d), per_b, per_b, full(nw), full(win), full(wuq), full(wuk), full(wvt),
                  full(qlat), full(kvlat), full(qn), full(kn), full(dqn), full(dkn),
                  tok(LANES), tok(LANES), tok(LANES), tok(LANES)],
        out_specs=[qt_spec, tok(MLA_HEADS * LANES), vt_spec] + dil_specs,
        out_shape=[qt_shape, jax.ShapeDtypeStruct((b, s, MLA_HEADS * LANES), BF16), vt_shape] + dil_shapes,
        scratch_shapes=[pltpu.VMEM((3 * DIL_W // LANES, tm, LANES), F32)],
        compiler_params=_cparams(("parallel", "arbitrary")),
        name="even_in_proj",
    )(x, sh, sc, nw, win, wuq, wuk, wvt, qlat, kvlat, qn, kn, dqn, dkn, cm, sm, c64, s64)


def _odd_in_kernel(x_ref, sh_ref, sc_ref, nw_ref, win_ref, wvt_ref, dqn_ref, dkn_ref, mqn_ref, mkn_ref,
                   c64_ref, s64_ref,
                   qct_out, kc_out, qmt_out, qmf_out, km_out, vct_out, vmt_out, kmean_out):
    x = x_ref[0]
    tm = x.shape[0]
    h = (_rms(x, nw_ref[...]) * (1.0 + sc_ref[0]) + sh_ref[0]).astype(BF16)
    u = jnp.dot(h, win_ref[...], preferred_element_type=F32)
    vt = _nt_dot(wvt_ref[...], h).astype(BF16)
    nvc = vct_out.shape[2]
    _store_t_blocks(vct_out, vt[:nvc])
    _store_t_blocks(vmt_out, vt[nvc:])
    c64, s64 = c64_ref[0], s64_ref[0]
    scale = HEAD_DIM ** -0.5 * LOG2E
    nqk = DIFF_HEADS * LANES
    nm = MOBA_HEADS * HEAD_DIM
    rot = 2 * nqk + 2 * nm

    def tables(gain_ref, mult):
        return gain_ref[0:1] * c64 * mult, gain_ref[1:2] * s64 * mult

    def roped(off, col, tab):
        a = off + LANES * col
        xc, xr = u[:, a:a + LANES], u[:, rot + a:rot + a + LANES]
        return _inv_rms64(xc) * (xc * tab[0] + xr * tab[1])

    t_qc, t_kc, t_qm, t_km = tables(dqn_ref, scale), tables(dkn_ref, 1.0), tables(mqn_ref, 1.0), tables(mkn_ref, 1.0)
    for col in range(nqk // LANES):
        sl = slice(LANES * col, LANES * (col + 1))
        _store_col_t(qct_out, col, roped(0, col, t_qc))
        kc_out[0, :, sl] = roped(nqk, col, t_kc).astype(BF16)

    base = 2 * nqk
    for col in range(nm // LANES):
        sl = slice(LANES * col, LANES * (col + 1))
        qm = roped(base, col, t_qm)
        qmf_out[0, :, sl] = qm
        _store_col_t(qmt_out, col, qm * scale)
        km = roped(base + nm, col, t_km)
        km_out[0, :, sl] = km.astype(BF16)
        for blk in range(tm // MOBA_BLOCK):
            rows = slice(MOBA_BLOCK * blk, MOBA_BLOCK * (blk + 1))
            kmean_out[0, 0, blk:blk + 1, sl] = jnp.mean(km[rows], axis=0, keepdims=True)


def _odd_in(x, sh, sc, nw, win, wvt, dqn, dkn, mqn, mkn, c64, s64, tm, t):
    b, s, d = x.shape
    tok = lambda w: pl.BlockSpec((1, tm, w), lambda bi, i: (bi, i, 0))
    per_b = pl.BlockSpec((1, 1, d), lambda bi, i: (bi, 0, 0))
    full = lambda a: pl.BlockSpec(a.shape, lambda bi, i: (0,) * a.ndim, pipeline_mode=pl.Buffered(1))
    w = DIFF_HEADS * LANES
    nblk = tm // MOBA_BLOCK
    shp = lambda dt: jax.ShapeDtypeStruct((b, s, w), dt)
    vt_spec, vt_shape = _t_blocks(b, s, wvt.shape[0] // 2, tm, t)
    return pl.pallas_call(
        _odd_in_kernel,
        grid=(b, s // tm),
        in_specs=[tok(d), per_b, per_b, full(nw), full(win), full(wvt), full(dqn), full(dkn), full(mqn), full(mkn),
                  tok(LANES), tok(LANES)],
        out_specs=[vt_spec, tok(w), vt_spec, tok(w), tok(w), vt_spec, vt_spec,
                   pl.BlockSpec((1, 1, nblk, w), lambda bi, i: (bi, i, 0, 0))],
        out_shape=[vt_shape, shp(BF16), vt_shape, shp(F32), shp(BF16), vt_shape, vt_shape,
                   jax.ShapeDtypeStruct((b, s // tm, nblk, w), F32)],
        compiler_params=_cparams(("parallel", "arbitrary")),
        name="odd_in_proj",
    )(x, sh, sc, nw, win, wvt, dqn, dkn, mqn, mkn, c64, s64)


def _qt_halves(qt):
    row = lax.broadcasted_iota(jnp.int32, qt.shape, 0)
    return [jnp.where((row >= HEAD_DIM) == bool(half), qt, jnp.zeros_like(qt)) for half in range(2)]


def _flash_t(qts, k_ref, vt_ref, k_slices, v_rows, scr, i, t, past_mask=None):
    m_scr, acc_scr, s_scr, p_scr, a_scr, b_scr = scr
    n = len(qts)
    ones = jnp.ones((ONES_ROWS, t), BF16)

    def qk(blk, s):
        b0 = pl.multiple_of(blk * t, t)
        return jnp.dot(k_ref[0, pl.ds(b0, t), k_slices[s]], qts[s], preferred_element_type=F32)

    def pv(blk, s, p):
        vt = jnp.concatenate([vt_ref[0, blk, v_rows[s], :], ones], axis=0)
        return jnp.dot(vt, p, preferred_element_type=F32)

    def store_scores(slot, s, st):
        s_scr[slot, s] = st
        b_scr[slot, s] = jnp.max(st, axis=0, keepdims=True)

    krow = lax.broadcasted_iota(jnp.int32, (t, t), 0)
    qcol = lax.broadcasted_iota(jnp.int32, (t, t), 1)
    causal = krow <= qcol
    last_past = jnp.maximum(i - 1, 0)
    for s in range(n):
        st = jnp.where(causal, qk(i, s), NEG_INF)
        m = jnp.max(st, axis=0, keepdims=True)
        m_scr[s] = m
        p_scr[0, s] = jnp.exp2(st - m).astype(BF16)
        p_scr[1, s] = jnp.zeros((t, t), BF16)
        a_scr[0, s] = jnp.ones_like(m)
        a_scr[1, s] = jnp.ones_like(m)
        acc_scr[s] = jnp.zeros(acc_scr.shape[1:], F32)
        store_scores(0, s, qk(0, s))
        store_scores(1, s, qk(jnp.minimum(1, last_past), s))

    def softmax_stage(slot, valid, rs, ws, blk):
        for s in range(n):
            cmax = b_scr[2 * rs + slot, s]
            keep = None if past_mask is None else past_mask(s, blk)
            if keep is not None:
                cmax = jnp.where(keep, cmax, NEG_INF)
            if valid is not None:
                cmax = jnp.where(valid, cmax, NEG_INF)
            m_prev = m_scr[s]
            m_new = jnp.maximum(m_prev, cmax)
            a_scr[2 * ws + slot, s] = jnp.exp2(m_prev - m_new)
            p = jnp.exp2(s_scr[2 * rs + slot, s] - m_new).astype(BF16)
            if keep is not None:
                p = jnp.where(keep, p, jnp.zeros_like(p))
            p_scr[2 * ws + slot, s] = p
            m_scr[s] = m_new

    def body(kk, rs, ws):
        first = kk == 0
        pa_blk = jnp.where(first, i, 2 * kk - 2)
        pb_blk = jnp.where(first, i, 2 * kk - 1)
        p_prev = [[p_scr[2 * rs + sl, s] for s in range(n)] for sl in range(2)]
        a_prev = [[a_scr[2 * rs + sl, s] for s in range(n)] for sl in range(2)]
        s_next = [[qk(jnp.minimum(2 * kk + 2 + sl, last_past), s) for s in range(n)] for sl in range(2)]
        pvs = [[pv(blk, s, p_prev[sl][s]) for s in range(n)] for sl, blk in enumerate((pa_blk, pb_blk))]
        softmax_stage(0, None, rs, ws, 2 * kk)
        softmax_stage(1, 2 * kk + 1 < i, rs, ws, jnp.minimum(2 * kk + 1, last_past))
        for sl in range(2):
            for s in range(n):
                store_scores(2 * ws + sl, s, s_next[sl][s])
        for s in range(n):
            acc_scr[s] = a_prev[1][s] * (a_prev[0][s] * acc_scr[s] + pvs[0][s]) + pvs[1][s]

    def two_bodies(kp, carry):
        body(2 * kp, 0, 1)
        body(2 * kp + 1, 1, 0)
        return carry

    nbody = (i + 1) // 2
    lax.fori_loop(0, nbody // 2, two_bodies, 0)

    @pl.when(nbody % 2 == 1)
    def _():
        body(nbody - 1, 0, 1)

    fs = nbody % 2
    kl = jnp.maximum(nbody - 1, 0)
    none = i == 0
    fa_blk = jnp.where(none, i, 2 * kl)
    fb_blk = jnp.where(none, i, jnp.minimum(2 * kl + 1, last_past))
    fb_scale = jnp.where(jnp.logical_or(none, i % 2 == 0), 1.0, 0.0)
    for s in range(n):
        acc_scr[s] = (a_scr[2 * fs + 1, s] * (a_scr[2 * fs, s] * acc_scr[s] + pv(fa_blk, s, p_scr[2 * fs, s]))
                      + fb_scale * pv(fb_blk, s, p_scr[2 * fs + 1, s]))


def _flash_scratch(n, nv, t):
    return [pltpu.VMEM((n, 1, t), F32), pltpu.VMEM((n, nv + ONES_ROWS, t), F32), pltpu.VMEM((4, n, t, t), F32),
            pltpu.VMEM((4, n, t, t), BF16), pltpu.VMEM((4, n, 1, t), F32), pltpu.VMEM((4, n, 1, t), F32)]


def _normalised(acc_scr, s, nv):
    return acc_scr[s, :nv, :] / acc_scr[s, nv:nv + 1, :]


def _half_mask(x, half):
    lane = _lane_iota((1, LANES))
    return jnp.where((lane >= HEAD_DIM) == bool(half), x, jnp.zeros_like(x))


def _mla_kernel(qt_ref, k_ref, vt_ref, ot_ref, *scr, t, nh):
    i = pl.program_id(2)
    ks = [slice(LANES * s, LANES * (s + 1)) for s in range(nh)]
    qs = [qt_ref[0, 0, ks[s], :] for s in range(nh)]
    vr = [slice(HEAD_DIM * s, HEAD_DIM * (s + 1)) for s in range(nh)]
    _flash_t(qs, k_ref, vt_ref, ks, vr, scr, i, t)
    for s in range(nh):
        ot_ref[0, 0, vr[s], :] = _normalised(scr[1], s, HEAD_DIM).astype(ot_ref.dtype)


def _mla_attention(qt, k, vt, nh):
    b, s, _ = k.shape
    _, nb, nv, t = vt.shape
    return pl.pallas_call(
        functools.partial(_mla_kernel, t=t, nh=nh),
        grid=(b, MLA_HEADS // nh, nb),
        in_specs=[pl.BlockSpec((1, 1, nh * LANES, t), lambda bi, p, i: (bi, i, p, 0)),
                  pl.BlockSpec((1, s, nh * LANES), lambda bi, p, i: (bi, 0, p)),
                  pl.BlockSpec((1, nb, nh * HEAD_DIM, t), lambda bi, p, i: (bi, 0, p, 0))],
        out_specs=pl.BlockSpec((1, 1, nh * HEAD_DIM, t), lambda bi, p, i: (bi, i, p, 0)),
        out_shape=jax.ShapeDtypeStruct((b, nb, nv, t), BF16),
        scratch_shapes=_flash_scratch(nh, HEAD_DIM, t),
        compiler_params=_cparams(("parallel", "parallel", "arbitrary")),
        name="mla_attention",
    )(qt, k, vt)


def _diff_kernel(qt_ref, k_ref, vt_ref, lam_ref, sub_ref, ot_ref, *scr, t, nh, lam_init):
    i = pl.program_id(2)
    qs, ks, vr = [], [], []
    for hd in range(nh):
        cols = slice(LANES * hd, LANES * (hd + 1))
        qs += _qt_halves(qt_ref[0, 0, cols, :])
        ks += [cols, cols]
        vr += [cols, cols]
    _flash_t(qs, k_ref, vt_ref, ks, vr, scr, i, t)
    lv = lam_ref[...]
    lam = (jnp.exp(jnp.sum(lv[0:1] * lv[1:2], keepdims=True))
           - jnp.exp(jnp.sum(lv[2:3] * lv[3:4], keepdims=True)) + lam_init)
    for hd in range(nh):
        o = _normalised(scr[1], 2 * hd, LANES) - lam * _normalised(scr[1], 2 * hd + 1, LANES)
        o = o * lax.rsqrt(jnp.mean(o * o, axis=0, keepdims=True) + NORM_EPS) * sub_ref[...]
        ot_ref[0, 0, LANES * hd:LANES * (hd + 1), :] = (o * (1.0 - lam_init)).astype(ot_ref.dtype)


def _diff_attention(qt, k, vt, lam_rows, subln_col, lam_init, nh):
    b, s, w = k.shape
    _, nb, _, t = vt.shape
    return pl.pallas_call(
        functools.partial(_diff_kernel, t=t, nh=nh, lam_init=lam_init),
        grid=(b, DIFF_HEADS // nh, nb),
        in_specs=[pl.BlockSpec((1, 1, nh * LANES, t), lambda bi, p, i: (bi, i, p, 0)),
                  pl.BlockSpec((1, s, nh * LANES), lambda bi, p, i: (bi, 0, p)),
                  pl.BlockSpec((1, nb, nh * LANES, t), lambda bi, p, i: (bi, 0, p, 0)),
                  pl.BlockSpec(lam_rows.shape, lambda bi, p, i: (0, 0)),
                  pl.BlockSpec(subln_col.shape, lambda bi, p, i: (0, 0))],
        out_specs=pl.BlockSpec((1, 1, nh * LANES, t), lambda bi, p, i: (bi, i, p, 0)),
        out_shape=jax.ShapeDtypeStruct((b, nb, w, t), BF16),
        scratch_shapes=_flash_scratch(2 * nh, LANES, t),
        compiler_params=_cparams(("parallel", "parallel", "arbitrary")),
        name="diff_attention",
    )(qt, k, vt, lam_rows, subln_col)


def _moba_kernel(qt_ref, qf_ref, k_ref, vt_ref, kmean_ref, ot_ref, sel_scr, *scr, t, npair):
    i = pl.program_id(2)
    nb = kmean_ref.shape[1]
    brow = lax.broadcasted_iota(jnp.int32, (nb, t), 0).astype(F32)
    past = brow < i.astype(F32)
    qs, ks, vr, gates = [], [], [], []
    for pr in range(npair):
        cols = slice(LANES * pr, LANES * (pr + 1))
        qf, kmean = qf_ref[0, :, cols], kmean_ref[0, :, cols]
        qs += _qt_halves(qt_ref[0, 0, cols, :])
        for half in range(2):
            gate =lax.dot_general(kmean, _half_mask(qf, half), (((1,), (1,)), ((), ())),
                                   preferred_element_type=F32, precision=lax.Precision.HIGHEST)
            gates.append(jnp.where(past, gate, NEG_INF))
            ks.append(cols)
            vr.append(slice(HEAD_DIM * len(vr), HEAD_DIM * (len(vr) + 1)))
    sels = [jnp.zeros((nb, t), F32) for _ in gates]
    for _ in range(MOBA_TOPK):
        for s in range(len(gates)):
            top = jnp.max(gates[s], axis=0, keepdims=True)
            first = jnp.min(jnp.where(gates[s] == top, brow, float(nb)), axis=0, keepdims=True)
            pick = brow == first
            sels[s] = jnp.where(pick, 1.0, sels[s])
            gates[s] = jnp.where(pick, 2 * NEG_INF, gates[s])
    for s in range(len(gates)):
        sel_scr[s] = jnp.where(past, sels[s], 0.0)
    _flash_t(qs, k_ref, vt_ref, ks, vr, scr, i, t,
             past_mask=lambda s, jj: sel_scr[s, pl.ds(jj, 1), :] > 0.5)
    for s in range(2 * npair):
        ot_ref[0, 0, vr[s], :] = _normalised(scr[1], s, HEAD_DIM).astype(ot_ref.dtype)


def _moba_attention(qt, qf, k, vt, kmean, npair):
    b, s, w = k.shape
    _, nb, _, t = vt.shape
    nkm = kmean.shape[1]
    lanes = npair * LANES
    return pl.pallas_call(
        functools.partial(_moba_kernel, t=t, npair=npair),
        grid=(b, w // lanes, nb),
        in_specs=[pl.BlockSpec((1, 1, lanes, t), lambda bi, p, i: (bi, i, p, 0)),
                  pl.BlockSpec((1, t, lanes), lambda bi, p, i: (bi, i, p)),
                  pl.BlockSpec((1, s, lanes), lambda bi, p, i: (bi, 0, p)),
                  pl.BlockSpec((1, nb, lanes, t), lambda bi, p, i: (bi, 0, p, 0)),
                  pl.BlockSpec((1, nkm, lanes), lambda bi, p, i: (bi, 0, p))],
        out_specs=pl.BlockSpec((1, 1, lanes, t), lambda bi, p, i: (bi, i, p, 0)),
        out_shape=jax.ShapeDtypeStruct((b, nb, w, t), BF16),
        scratch_shapes=[pltpu.VMEM((2 * npair, nkm, t), F32)] + _flash_scratch(2 * npair, HEAD_DIM, t),
        compiler_params=_cparams(("parallel", "parallel", "arbitrary")),
        name="moba_attention",
    )(qt, qf, k, vt, kmean)


def _sw_kernel(q_ref, kc_ref, vc_ref, kp_ref, vp_ref, o_ref, l_ref, *, nbk):
    i = pl.program_id(1)
    blk = SW_BLOCK
    qi = lax.broadcasted_iota(jnp.int32, (blk, 2 * blk), 0)
    kj = lax.broadcasted_iota(jnp.int32, (blk, 2 * blk), 1)
    band = (kj >= qi) & (kj <= qi + blk)
    lane = _lane_iota((1, LANES))
    heads = [(pair, half) for pair in range(DIL_W // LANES) for half in range(2)]

    def scores(n):
        rows = slice(blk * n, blk * (n + 1))
        q = q_ref[0, rows, :]
        if n == 0:
            kprev, valid = kp_ref[0], band & ((kj >= blk) | (i > 0))
        else:
            kprev, valid = kc_ref[0, blk * (n - 1):blk * n, :], band
        kk = jnp.concatenate([kprev, kc_ref[0, rows, :]], axis=0)
        out = []
        for pair, half in heads:
            sl = slice(LANES * pair, LANES * (pair + 1))
            out.append(jnp.where(valid, _nt_dot(_half_mask(q[:, sl], half), kk[:, sl]), NEG_INF))
        return out

    nxt = scores(0)
    for n in range(nbk):
        cur, rows = nxt, slice(blk * n, blk * (n + 1))
        if n + 1 < nbk:
            nxt = scores(n + 1)
        vprev = vp_ref[0] if n == 0 else vc_ref[0, blk * (n - 1):blk * n, :]
        vv = jnp.concatenate([vprev, vc_ref[0, rows, :]], axis=0)
        outs, lses = [], []
        for (pair, half), sc in zip(heads, cur):
            m = jnp.max(sc, axis=-1, keepdims=True)
            p = jnp.exp2(sc - m)
            l = jnp.sum(p, axis=-1, keepdims=True)
            sl = slice(LANES * pair, LANES * (pair + 1))
            outs.append(jnp.dot(p.astype(BF16), vv[:, sl], preferred_element_type=F32) / l)
            lses.append(m + jnp.log2(l))
        for pair in range(DIL_W // LANES):
            sl = slice(LANES * pair, LANES * (pair + 1))
            o_ref[0, rows, sl] = jnp.where(lane < HEAD_DIM, outs[2 * pair], outs[2 * pair + 1])
            l_ref[0, rows, sl] = jnp.where(lane < HEAD_DIM, lses[2 * pair], lses[2 * pair + 1])


def _sliding_window(dil, nbk, name):
    lead, (m, w3) = dil.shape[:-2], dil.shape[-2:]
    seqs = dil.reshape((-1, m, w3))
    nbk = min(nbk, m // SW_BLOCK)
    rows = SW_BLOCK * nbk
    cur = lambda off: pl.BlockSpec((1, rows, DIL_W), lambda n, i: (n, i, off))
    prev = lambda off: pl.BlockSpec((1, SW_BLOCK, DIL_W), lambda n, i: (n, jnp.maximum(i * nbk - 1, 0), off))
    out_spec = pl.BlockSpec((1, rows, DIL_W), lambda n, i: (n, i, 0))
    o, l = pl.pallas_call(
        functools.partial(_sw_kernel, nbk=nbk),
        grid=(seqs.shape[0], m // rows),
        in_specs=[cur(0), cur(1), cur(2), prev(1), prev(2)],
        out_specs=[out_spec, out_spec],
        out_shape=[jax.ShapeDtypeStruct((seqs.shape[0], m, DIL_W), F32)] * 2,
        compiler_params=_cparams(("parallel", "arbitrary")),
        name=name,
    )(seqs, seqs, seqs, seqs, seqs)
    return o.reshape(lead + (m, DIL_W)), l.reshape(lead + (m, DIL_W))


def _mlp_tail(y, mlp_refs):
    sh_ref, sc_ref, g_ref, nw_ref, w1_ref, w2_ref = mlp_refs
    h = (_rms(y, nw_ref[...]) * (1.0 + sc_ref[0]) + sh_ref[0]).astype(BF16)
    acc = jnp.zeros(y.shape, F32)
    for c in range(w1_ref.shape[1] // MLP_CHUNK):
        cols = slice(MLP_CHUNK * c, MLP_CHUNK * (c + 1))
        a = jnp.maximum(jnp.dot(h, w1_ref[:, cols], preferred_element_type=F32), 0.0)
        acc = acc + jnp.dot((a * a).astype(BF16), w2_ref[cols, :], preferred_element_type=F32)
    return y + g_ref[0] * acc


def _proj_t(ot_ref, w_rows):
    return jnp.concatenate([_tn_dot(ot_ref[0, j], w_rows) for j in range(ot_ref.shape[1])], axis=0)


def _mlp_specs(mlp_args, d):
    per_b = pl.BlockSpec((1, 1, d), lambda bi, i: (bi, 0, 0))
    const = lambda a: pl.BlockSpec(a.shape, lambda bi, i: (0, 0), pipeline_mode=pl.Buffered(1))
    sh, sc, g2, nw, w1, w2 = mlp_args
    return [per_b, per_b, per_b, pl.BlockSpec(nw.shape, lambda bi, i: (0, 0)), const(w1), const(w2)]


def _even_out_kernel(x_ref, g_ref, oat_ref, o0_ref, o1_ref, o2_ref, l0_ref, l1_ref, l2_ref, w_ref,
                     *rest):
    *mlp_refs, y_ref, tscr = rest
    tm = x_ref.shape[1]
    ncol = DIL_W // LANES

    def token_major(ref, slot):
        r = ref.shape[1]
        for c in range(r):
            for j in range(ncol):
                tscr[slot * ncol + j, pl.ds(c, tm // r, stride=r), :] = ref[0, c, :, LANES * j:LANES * (j + 1)]
        return jnp.concatenate([tscr[slot * ncol + j] for j in range(ncol)], axis=1)

    o0, l0 = o0_ref[0], l0_ref[0]
    o1, l1 = token_major(o1_ref, 0), token_major(l1_ref, 1)
    o2, l2 = token_major(o2_ref, 2), token_major(l2_ref, 3)
    top = jnp.maximum(jnp.maximum(l0, l1), l2)
    w0, w1, w2 = jnp.exp2(l0 - top), jnp.exp2(l1 - top), jnp.exp2(l2 - top)
    ob = (w0 * o0 + w1 * o1 + w2 * o2) / (w0 + w1 + w2)
    na = oat_ref.shape[2]
    y = _proj_t(oat_ref, w_ref[:na, :]) + jnp.dot(ob.astype(BF16), w_ref[na:, :], preferred_element_type=F32)
    y_ref[0] = _mlp_tail(x_ref[0] + g_ref[0] * y, mlp_refs)


def _t_spec(a, tm):
    return pl.BlockSpec((1, tm // a.shape[-1]) + a.shape[2:], lambda bi, i: (bi, i, 0, 0))


def _even_out(x, g1, oat, o_dil, l_dil, w_out, mlp_args, tm):
    b, s, d = x.shape
    tok = lambda w: pl.BlockSpec((1, tm, w), lambda bi, i: (bi, i, 0))

    def dil_spec(a):
        if a.ndim == 3:
            return tok(DIL_W)
        r = a.shape[1]
        return pl.BlockSpec((1, r, tm // r, DIL_W), lambda bi, i: (bi, 0, i, 0))

    return pl.pallas_call(
        _even_out_kernel,
        grid=(b, s // tm),
        in_specs=[tok(d), pl.BlockSpec((1, 1, d), lambda bi, i: (bi, 0, 0)), _t_spec(oat, tm)]
                 + [dil_spec(a) for a in (*o_dil, *l_dil)]
                 + [pl.BlockSpec(w_out.shape, lambda bi, i: (0, 0), pipeline_mode=pl.Buffered(1))]
                 + _mlp_specs(mlp_args, d),
        out_specs=tok(d),
        out_shape=jax.ShapeDtypeStruct((b, s, d), F32),
        scratch_shapes=[pltpu.VMEM((4 * DIL_W // LANES, tm, LANES), F32)],
        compiler_params=_cparams(("parallel", "arbitrary")),
        name="even_out_mlp",
    )(x, g1, oat, *o_dil, *l_dil, w_out, *mlp_args)


def _odd_out_kernel(x_ref, g_ref, oct_ref, odt_ref, w_ref, *rest):
    *mlp_refs, y_ref = rest
    nc = oct_ref.shape[2]
    y = _proj_t(oct_ref, w_ref[:nc, :]) + _proj_t(odt_ref, w_ref[nc:, :])
    y_ref[0] = _mlp_tail(x_ref[0] + g_ref[0] * y, mlp_refs)


def _odd_out(x, g1, oct, odt, w_out, mlp_args, tm):
    b, s, d = x.shape
    tok = lambda w: pl.BlockSpec((1, tm, w), lambda bi, i: (bi, i, 0))
    return pl.pallas_call(
        _odd_out_kernel,
        grid=(b, s // tm),
        in_specs=[tok(d), pl.BlockSpec((1, 1, d), lambda bi, i: (bi, 0, 0)), _t_spec(oct, tm), _t_spec(odt, tm),
                  pl.BlockSpec(w_out.shape, lambda bi, i: (0, 0), pipeline_mode=pl.Buffered(1))]
                 + _mlp_specs(mlp_args, d),
        out_specs=tok(d),
        out_shape=jax.ShapeDtypeStruct((b, s, d), F32),
        compiler_params=_cparams(("parallel", "arbitrary")),
        name="odd_out_mlp",
    )(x, g1, oct, odt, w_out, *mlp_args)


def _rope_tables(positions):
    pos = positions.astype(F32)[:, :, None]
    half = HEAD_DIM // 2
    inv = ROPE_THETA ** (-jnp.arange(half, dtype=F32) / half)
    ang = pos * inv[_freq_order()]
    c, s = jnp.cos(ang), jnp.sin(ang)

    def tiled(x, width):
        reps = LANES // width
        return jnp.broadcast_to(x[:, :, None, :], x.shape[:2] + (reps, width)).reshape(x.shape[:2] + (LANES,))

    lane = np.arange(LANES)
    sign = np.where(lane % HEAD_DIM < half, -1.0, 1.0).astype(np.float32)
    c64, s64 = tiled(c, half), tiled(s, half) * sign
    hm = MLA_ROPE // 2
    rope = (lane >= MLA_NOPE) & (lane < MLA_QK)
    sign_m = np.where(lane < MLA_NOPE + hm, -1.0, 1.0).astype(np.float32)
    cm = jnp.where(rope, tiled(c[..., :hm], hm), 1.0)
    sm = jnp.where(rope, tiled(s[..., :hm], hm) * sign_m, 0.0)
    return cm, sm, c64, s64


def _freq_order():
    half = HEAD_DIM // 2
    return np.concatenate([np.arange(0, half, 2), np.arange(1, half, 2)])


def _head_perm(n):
    within = np.concatenate([_freq_order(), HEAD_DIM // 2 + _freq_order()])
    idx = np.arange(n)
    return idx // HEAD_DIM * HEAD_DIM + within[idx % HEAD_DIM]


def _partner(n, width):
    idx = np.arange(n)
    return np.where(idx % width < width // 2, idx + width // 2, idx - width // 2)


def _even_weights(w_in, w_uq, w_ukv, qn, kn):
    w_in, w_uq, w_ukv = w_in.astype(BF16), w_uq.astype(BF16), w_ukv.astype(BF16)
    o2 = MLA_Q_RANK + MLA_KV_RANK
    o3 = o2 + MLA_ROPE
    d = w_in.shape[0]
    p_rope = _partner(MLA_ROPE, MLA_ROPE)
    zeros = lambda n: jnp.zeros((d, n), w_in.dtype)
    kr = w_in[:, o2:o3]
    tail = LANES - MLA_QK
    krp = jnp.concatenate([zeros(MLA_NOPE), kr, zeros(tail)], axis=1)
    krp_rot = jnp.concatenate([zeros(MLA_NOPE), kr[:, p_rope], zeros(tail)], axis=1)
    dil = w_in[:, o3:].reshape(d, 3, len(DIL_CONFIGS), DIL_W)
    dil = jnp.concatenate([dil[:, :2][..., _head_perm(DIL_W)], dil[:, 2:]], axis=1)
    dil_main = dil.transpose(0, 2, 1, 3).reshape(d, -1)
    dil_rot = dil[:, :2][..., _partner(DIL_W, HEAD_DIM)].transpose(0, 2, 1, 3).reshape(d, -1)
    win = jnp.concatenate([w_in[:, :o2], krp, krp_rot, dil_main, dil_rot], axis=1)

    padq = ((0, 0), (0, 0), (0, tail))
    wuq_rot = jnp.concatenate([w_uq[:, :, :MLA_NOPE], w_uq[:, :, MLA_NOPE:][:, :, p_rope]], axis=-1)
    wuq = jnp.concatenate([jnp.pad(w_uq, padq).reshape(MLA_Q_RANK, -1),
                           jnp.pad(wuq_rot, padq).reshape(MLA_Q_RANK, -1)], axis=1)
    wuk = jnp.pad(w_ukv[:, :, :MLA_NOPE], ((0, 0), (0, 0), (0, LANES - MLA_NOPE))).reshape(MLA_KV_RANK, -1)
    wvt = w_ukv[:, :, MLA_NOPE:].reshape(MLA_KV_RANK, -1).T

    def gains(g):
        rot = jnp.concatenate([g[:MLA_NOPE], g[MLA_NOPE:][p_rope]])
        return jnp.stack([jnp.pad(g, (0, tail)), jnp.pad(rot, (0, tail))]).astype(F32)

    return win.astype(BF16), wuq.astype(BF16), wuk.astype(BF16), wvt.astype(BF16), gains(qn), gains(kn)


def _odd_weights(w_in):
    w_in = w_in.astype(BF16)
    nqk = DIFF_HEADS * LANES
    nm = MOBA_HEADS * HEAD_DIM
    main = jnp.concatenate([w_in[:, :2 * nqk], w_in[:, 3 * nqk:3 * nqk + 2 * nm]], axis=1)
    main = main[:, _head_perm(main.shape[1])]
    win =jnp.concatenate([main, main[:, _partner(main.shape[1], HEAD_DIM)]], axis=1)
    wvt = jnp.concatenate([w_in[:, 2 * nqk:3 * nqk], w_in[:, 3 * nqk + 2 * nm:]], axis=1).T
    return win.astype(BF16), wvt.astype(BF16)


def _gains64(g):
    g = g[_head_perm(HEAD_DIM)]
    rot = g[_partner(HEAD_DIM, HEAD_DIM)]
    return jnp.stack([jnp.concatenate([g, g]), jnp.concatenate([rot, rot])]).astype(F32)


def kernel(x, c, positions, ada_w, ada_b, norm_mix, norm_mlp, mlp_w1, mlp_w2, even_w_in, even_w_out, mla_q_lat_norm, mla_kv_lat_norm, mla_w_uq, mla_w_ukv, mla_q_norm, mla_k_norm, dil_q_norm, dil_k_norm, odd_w_in, odd_w_out, diff_q_norm, diff_k_norm, diff_lambda, diff_subln, moba_q_norm, moba_k_norm):
    b, s, d = x.shape
    depth = ada_w.shape[0]
    t_attn, tm_in, tm_out = ATTN_BLOCK, TOKEN_TILE, TOKEN_TILE

    mod = _adaln(c, ada_w, ada_b)
    cm, sm, c64, s64 = _rope_tables(positions)

    for layer in range(depth):
        sh1, sc1, g1, sh2, sc2, g2 = [mod[layer, :, d * t:d * (t + 1)].reshape(b, 1, d) for t in range(6)]
        mlp_args = (sh2, sc2, g2, _row(norm_mlp[layer]), mlp_w1[layer].astype(BF16), mlp_w2[layer].astype(BF16))
        i = layer // 2
        if layer % 2 == 0:
            win, wuq, wuk, wvt, qn, kn = _even_weights(even_w_in[i], mla_w_uq[i], mla_w_ukv[i],
                                                       mla_q_norm[i], mla_k_norm[i])
            q, k, vt, *dils = _even_in(x, sh1, sc1, _row(norm_mix[layer]), win, wuq, wuk, wvt,
                                       _row(mla_q_lat_norm[i]), _row(mla_kv_lat_norm[i]), qn, kn,
                                       _gains64(dil_q_norm[i]), _gains64(dil_k_norm[i]), cm, sm, c64, s64, tm_in, t_attn)
            o_at = _mla_attention(q, k, vt, MLA_HEADS_PER_STEP)
            o_dil, l_dil = zip(*[_sliding_window(dg, SW_BLOCKS_PER_STEP, f"sliding_window_g{g}")
                                 for g, dg in enumerate(dils)])
            x = _even_out(x, g1, o_at, o_dil, l_dil, even_w_out[i].astype(BF16), mlp_args, tm_out)
        else:
            lam_init = 0.8 - 0.6 * math.exp(-0.3 * layer)
            win, wvt = _odd_weights(odd_w_in[i])
            qc, kc, qm, qmf, km, vct, vmt, kmean = _odd_in(
                x, sh1, sc1, _row(norm_mix[layer]), win, wvt,
                _gains64(diff_q_norm[i]), _gains64(diff_k_norm[i]), _gains64(moba_q_norm[i]),
                _gains64(moba_k_norm[i]),
                c64, s64, tm_in, t_attn)
            o_ct = _diff_attention(qc, kc, vct, diff_lambda[i].astype(F32),
                                   diff_subln[i].reshape(-1, 1).astype(F32), lam_init, DIFF_HEADS_PER_STEP)
            o_dt = _moba_attention(qm, qmf, km, vmt, kmean.reshape(b, s // MOBA_BLOCK, kmean.shape[-1]),
                                   MOBA_PAIRS_PER_STEP)
            x = _odd_out(x, g1, o_ct, o_dt, odd_w_out[i].astype(BF16), mlp_args, tm_out)
    return x
```

```python
import functools
import math

import jax
import jax.numpy as jnp
import numpy as np
from jax import lax
from jax.experimental import pallas as pl
from jax.experimental.pallas import tpu as pltpu

F32 = jnp.float32
BF16 = jnp.bfloat16

LANES = 128
HEAD_DIM = 64
ROPE_THETA = 10000.0
NORM_EPS = 1e-6
NEG_INF = -1e30
LOG2E = math.log2(math.e)

MLA_HEADS = 8
MLA_Q_RANK = 384
MLA_KV_RANK = 256
MLA_NOPE = 64
MLA_ROPE = 32
MLA_QK = MLA_NOPE + MLA_ROPE
DIL_CONFIGS = ((128, 1), (512, 4), (2048, 16))
DIL_HEADS = 4
DIL_W = DIL_HEADS * HEAD_DIM
DIFF_HEADS = 4
MOBA_HEADS = 8
MOBA_BLOCK = 256
MOBA_TOPK = 3
SW_BLOCK = 128
ONES_ROWS = 16

VMEM_LIMIT = 56 * 1024 * 1024

ATTN_BLOCK = MOBA_BLOCK
TOKEN_TILE = 2 * ATTN_BLOCK
MLP_CHUNK = 1024
ADALN_TILE = 2048
SW_BLOCKS_PER_STEP = 8
MLA_HEADS_PER_STEP = 4
DIFF_HEADS_PER_STEP = 2
MOBA_PAIRS_PER_STEP = 2


def _cparams(sem):
    return pltpu.CompilerParams(dimension_semantics=sem, vmem_limit_bytes=VMEM_LIMIT)


def _nt_dot(a, b):
    return lax.dot_general(a, b, (((1,), (1,)), ((), ())), preferred_element_type=F32)


def _tn_dot(a, b):
    return lax.dot_general(a, b, (((0,), (0,)), ((), ())), preferred_element_type=F32)


def _rms(x, w):
    return x * lax.rsqrt(jnp.mean(x * x, axis=-1, keepdims=True) + NORM_EPS) * w


def _lane_iota(shape):
    return lax.broadcasted_iota(jnp.int32, shape, len(shape) - 1)


def _adaln_kernel(c_ref, w_ref, b_ref, o_ref):
    c = c_ref[...]
    cond = c * (1.0 / (1.0 + jnp.exp(-c)))
    o_ref[0] = jnp.dot(cond, w_ref[0], preferred_element_type=F32,
                       precision=lax.Precision.HIGHEST) + b_ref[0]


def _adaln(c, ada_w, ada_b):
    depth, d, n = ada_w.shape
    b = c.shape[0]
    tn = ADALN_TILE
    return pl.pallas_call(
        _adaln_kernel,
        grid=(depth, n // tn),
        in_specs=[pl.BlockSpec((b, d), lambda l, j: (0, 0)),
                  pl.BlockSpec((1, d, tn), lambda l, j: (l, 0, j)),
                  pl.BlockSpec((1, 1, tn), lambda l, j: (l, 0, j))],
        out_specs=pl.BlockSpec((1, b, tn), lambda l, j: (l, 0, j)),
        out_shape=jax.ShapeDtypeStruct((depth, b, n), F32),
        compiler_params=_cparams(("parallel", "parallel")),
        name="adaln",
    )(c, ada_w, ada_b.reshape(depth, 1, n))


def _inv_rms64(x):
    lane = _lane_iota((1, LANES))
    lo = lane < HEAD_DIM
    sq = x * x
    s_lo = jnp.sum(jnp.where(lo, sq, 0.0), axis=-1, keepdims=True)
    s_hi = jnp.sum(jnp.where(lo, 0.0, sq), axis=-1, keepdims=True)
    return lax.rsqrt(jnp.where(lo, s_lo, s_hi) * (1.0 / HEAD_DIM) + NORM_EPS)


def _store_t_blocks(out_ref, xt):
    t = out_ref.shape[-1]
    for j in range(out_ref.shape[1]):
        out_ref[0, j] = xt[:, t * j:t * (j + 1)]


def _store_col_t(out_ref, col, x):
    t = out_ref.shape[-1]
    for j in range(out_ref.shape[1]):
        out_ref[0, j, LANES * col:LANES * (col + 1), :] = x[t * j:t * (j + 1)].T.astype(out_ref.dtype)


def _t_blocks(b, s, rows, tm, t):
    return (pl.BlockSpec((1, tm // t, rows, t), lambda bi, i: (bi, i, 0, 0)),
            jax.ShapeDtypeStruct((b, s // t, rows, t), BF16))


def _even_in_kernel(x_ref, sh_ref, sc_ref, nw_ref, win_ref, wuq_ref, wuk_ref, wvt_ref,
                    qlat_ref, kvlat_ref, qn_ref, kn_ref, dqn_ref, dkn_ref,
                    cm_ref, sm_ref, c64_ref, s64_ref,
                    qt_out, k_out, vt_out, d0_out, d1_out, d2_out, dscr):
    x = x_ref[0]
    tm = x.shape[0]
    h = _rms(x, nw_ref[...]) * (1.0 + sc_ref[0]) + sh_ref[0]
    u = jnp.dot(h.astype(BF16), win_ref[...], preferred_element_type=F32)

    o1 = MLA_Q_RANK
    o2 = o1 + MLA_KV_RANK
    o_kr, o_krr, o_dil = o2, o2 + LANES, o2 + 2 * LANES
    n_dil = 3 * DIL_W * len(DIL_CONFIGS)
    o_rot = o_dil + n_dil
    nq = MLA_HEADS * LANES
    cqn = _rms(u[:, :o1], qlat_ref[...]).astype(BF16)
    qp = jnp.dot(cqn, wuq_ref[...], preferred_element_type=F32)
    ckvn = _rms(u[:, o1:o2], kvlat_ref[...]).astype(BF16)
    kvp = jnp.dot(ckvn, wuk_ref[...], preferred_element_type=F32)
    _store_t_blocks(vt_out, _nt_dot(wvt_ref[...], ckvn).astype(BF16))
    kr, kr_rot = u[:, o_kr:o_kr + LANES], u[:, o_krr:o_krr + LANES]

    cm, sm = cm_ref[0], sm_ref[0]
    q_scale = MLA_QK ** -0.5 * LOG2E
    qgc, qgs = qn_ref[0:1] * cm * q_scale, qn_ref[1:2] * sm * q_scale
    kgc, kgs = kn_ref[0:1] * cm, kn_ref[1:2] * sm
    kr_term = kr_rot * kgs
    for hd in range(MLA_HEADS):
        sl = slice(LANES * hd, LANES * (hd + 1))
        qh = qp[:, sl]
        inv = lax.rsqrt(jnp.sum(qh * qh, -1, keepdims=True) * (1.0 / MLA_QK) + NORM_EPS)
        _store_col_t(qt_out, hd, inv * (qh * qgc + qp[:, nq + LANES * hd:nq + LANES * (hd + 1)] * qgs))
        kh = kvp[:, sl] + kr
        inv = lax.rsqrt(jnp.sum(kh * kh, -1, keepdims=True) * (1.0 / MLA_QK) + NORM_EPS)
        k_out[0, :, sl] = (inv * (kh * kgc + kr_term)).astype(BF16)

    c64, s64 = c64_ref[0], s64_ref[0]
    d_scale = HEAD_DIM ** -0.5 * LOG2E
    dqc, dqs = dqn_ref[0:1] * c64 * d_scale, dqn_ref[1:2] * s64 * d_scale
    dkc, dks = dkn_ref[0:1] * c64, dkn_ref[1:2] * s64
    ncol = DIL_W // LANES
    for g, d_out in enumerate((d0_out, d1_out, d2_out)):
        _, r = DIL_CONFIGS[g]
        base = o_dil + 3 * DIL_W * g
        rbase = o_rot + 2 * DIL_W * g
        for j in range(3 * ncol):
            xc = u[:, base + LANES * j:base + LANES * (j + 1)]
            if j < 2 * ncol:
                xr = u[:, rbase + LANES * j:rbase + LANES * (j + 1)]
                gc, gs = (dqc, dqs) if j < ncol else (dkc, dks)
                xc = _inv_rms64(xc) * (xc * gc + xr * gs)
            if r == 1:
                d_out[0, :, LANES * j:LANES * (j + 1)] = xc.astype(BF16)
            else:
                dscr[j] = xc
                for c in range(r):
                    d_out[0, c, :, LANES * j:LANES * (j + 1)] = dscr[j, pl.ds(c, tm // r, stride=r), :].astype(BF16)


def _row(v):
    return v.reshape(1, -1).astype(F32)


def _even_in(x, sh, sc, nw, win, wuq, wuk, wvt, qlat, kvlat, qn, kn, dqn, dkn, cm, sm, c64, s64, tm, t):
    b, s, d = x.shape
    tok = lambda w: pl.BlockSpec((1, tm, w), lambda bi, i: (bi, i, 0))
    per_b = pl.BlockSpec((1, 1, d), lambda bi, i: (bi, 0, 0))
    full = lambda a: pl.BlockSpec(a.shape, lambda bi, i: (0,) * a.ndim, pipeline_mode=pl.Buffered(1))
    vt_spec, vt_shape = _t_blocks(b, s, wvt.shape[0], tm, t)
    qt_spec, qt_shape = _t_blocks(b, s, MLA_HEADS * LANES, tm, t)
    dil_specs, dil_shapes = [], []
    for _, r in DIL_CONFIGS:
        if r == 1:
            dil_specs.append(tok(3 * DIL_W))
            dil_shapes.append(jax.ShapeDtypeStruct((b, s, 3 * DIL_W), BF16))
        else:
            dil_specs.append(pl.BlockSpec((1, r, tm // r, 3 * DIL_W), lambda bi, i: (bi, 0, i, 0)))
            dil_shapes.append(jax.ShapeDtypeStruct((b, r, s // r, 3 * DIL_W), BF16))
    return pl.pallas_call(
        _even_in_kernel,
        grid=(b, s // tm),
        in_specs=[tok(d), per_b, per_b, full(nw), full(win), full(wuq), full(wuk), full(wvt),
                  full(qlat), full(kvlat), full(qn), full(kn), full(dqn), full(dkn),
                  tok(LANES), tok(LANES), tok(LANES), tok(LANES)],
        out_specs=[qt_spec, tok(MLA_HEADS * LANES), vt_spec] + dil_specs,
        out_shape=[qt_shape, jax.ShapeDtypeStruct((b, s, MLA_HEADS * LANES), BF16), vt_shape] + dil_shapes,
        scratch_shapes=[pltpu.VMEM((3 * DIL_W // LANES, tm, LANES), F32)],
        compiler_params=_cparams(("parallel", "arbitrary")),
        name="even_in_proj",
    )(x, sh, sc, nw, win, wuq, wuk, wvt, qlat, kvlat, qn, kn, dqn, dkn, cm, sm, c64, s64)


def _odd_in_kernel(x_ref, sh_ref, sc_ref, nw_ref, win_ref, wvt_ref, dqn_ref, dkn_ref, mqn_ref, mkn_ref,
                   c64_ref, s64_ref,
                   qc_out, kc_out, qm_out, qmf_out, km_out, vct_out, vmt_out, kmean_out):
    x = x_ref[0]
    tm = x.shape[0]
    h = (_rms(x, nw_ref[...]) * (1.0 + sc_ref[0]) + sh_ref[0]).astype(BF16)
    u = jnp.dot(h, win_ref[...], preferred_element_type=F32)
    vt = _nt_dot(wvt_ref[...], h).astype(BF16)
    nvc = vct_out.shape[2]
    _store_t_blocks(vct_out, vt[:nvc])
    _store_t_blocks(vmt_out, vt[nvc:])
    c64, s64 = c64_ref[0], s64_ref[0]
    scale = HEAD_DIM ** -0.5 * LOG2E
    nqk = DIFF_HEADS * LANES
    nm = MOBA_HEADS * HEAD_DIM
    rot = 2 * nqk + 2 * nm

    def tables(gain_ref, mult):
        return gain_ref[0:1] * c64 * mult, gain_ref[1:2] * s64 * mult

    def roped(off, col, tab):
        a = off + LANES * col
        xc, xr = u[:, a:a + LANES], u[:, rot + a:rot + a + LANES]
        return _inv_rms64(xc) * (xc * tab[0] + xr * tab[1])

    t_qc, t_kc, t_qm, t_km = tables(dqn_ref, scale), tables(dkn_ref, 1.0), tables(mqn_ref, 1.0), tables(mkn_ref, 1.0)
    for col in range(nqk // LANES):
        sl = slice(LANES * col, LANES * (col + 1))
        qc_out[0, :, sl] = roped(0, col, t_qc).astype(BF16)
        kc_out[0, :, sl] = roped(nqk, col, t_kc).astype(BF16)

    base = 2 * nqk
    for col in range(nm // LANES):
        sl = slice(LANES * col, LANES * (col + 1))
        qm = roped(base, col, t_qm)
        qmf_out[0, :, sl] = qm
        qm_out[0, :, sl] = (qm * scale).astype(BF16)
        km = roped(base + nm, col, t_km)
        km_out[0, :, sl] = km.astype(BF16)
        for blk in range(tm // MOBA_BLOCK):
            rows = slice(MOBA_BLOCK * blk, MOBA_BLOCK * (blk + 1))
            kmean_out[0, 0, blk:blk + 1, sl] = jnp.mean(km[rows], axis=0, keepdims=True)


def _odd_in(x, sh, sc, nw, win, wvt, dqn, dkn, mqn, mkn, c64, s64, tm, t):
    b, s, d = x.shape
    tok = lambda w: pl.BlockSpec((1, tm, w), lambda bi, i: (bi, i, 0))
    per_b = pl.BlockSpec((1, 1, d), lambda bi, i: (bi, 0, 0))
    full = lambda a: pl.BlockSpec(a.shape, lambda bi, i: (0,) * a.ndim, pipeline_mode=pl.Buffered(1))
    w = DIFF_HEADS * LANES
    nblk = tm // MOBA_BLOCK
    shp = lambda dt: jax.ShapeDtypeStruct((b, s, w), dt)
    vt_spec, vt_shape = _t_blocks(b, s, wvt.shape[0] // 2, tm, t)
    return pl.pallas_call(
        _odd_in_kernel,
        grid=(b, s // tm),
        in_specs=[tok(d), per_b, per_b, full(nw), full(win), full(wvt), full(dqn), full(dkn), full(mqn), full(mkn),
                  tok(LANES), tok(LANES)],
        out_specs=[tok(w)] * 5 + [vt_spec, vt_spec, pl.BlockSpec((1, 1, nblk, w), lambda bi, i: (bi, i, 0, 0))],
        out_shape=[shp(BF16), shp(BF16), shp(BF16), shp(F32), shp(BF16), vt_shape, vt_shape,
                   jax.ShapeDtypeStruct((b, s // tm, nblk, w), F32)],
        compiler_params=_cparams(("parallel", "arbitrary")),
        name="odd_in_proj",
    )(x, sh, sc, nw, win, wvt, dqn, dkn, mqn, mkn, c64, s64)


def _qt_halves(q):
    qt = q.astype(F32).T
    row = lax.broadcasted_iota(jnp.int32, qt.shape, 0)
    return [jnp.where((row >= HEAD_DIM) == bool(half), qt, 0.0).astype(BF16) for half in range(2)]


def _flash_t(qts, k_ref, vt_ref, k_slices, v_rows, scr, i, t, past_mask=None):
    m_scr, acc_scr, s_scr, p_scr, a_scr, b_scr = scr
    n = len(qts)
    ones = jnp.ones((ONES_ROWS, t), BF16)

    def qk(blk, s):
        b0 = pl.multiple_of(blk * t, t)
        return jnp.dot(k_ref[0, pl.ds(b0, t), k_slices[s]], qts[s], preferred_element_type=F32)

    def pv(blk, s, p):
        vt = jnp.concatenate([vt_ref[0, blk, v_rows[s], :], ones], axis=0)
        return jnp.dot(vt, p, preferred_element_type=F32)

    def store_scores(slot, s, st):
        s_scr[slot, s] = st
        b_scr[slot, s] = jnp.max(st, axis=0, keepdims=True)

    krow = lax.broadcasted_iota(jnp.int32, (t, t), 0)
    qcol = lax.broadcasted_iota(jnp.int32, (t, t), 1)
    causal = krow <= qcol
    last_past = jnp.maximum(i - 1, 0)
    for s in range(n):
        st = jnp.where(causal, qk(i, s), NEG_INF)
        m = jnp.max(st, axis=0, keepdims=True)
        m_scr[s] = m
        p_scr[0, s] = jnp.exp2(st - m).astype(BF16)
        p_scr[1, s] = jnp.zeros((t, t), BF16)
        a_scr[0, s] = jnp.ones_like(m)
        a_scr[1, s] = jnp.ones_like(m)
        acc_scr[s] = jnp.zeros(acc_scr.shape[1:], F32)
        store_scores(0, s, qk(0, s))
        store_scores(1, s, qk(jnp.minimum(1, last_past), s))

    def softmax_stage(slot, valid, rs, ws, blk):
        for s in range(n):
            cmax = b_scr[2 * rs + slot, s]
            keep = None if past_mask is None else past_mask(s, blk)
            if keep is not None:
                cmax = jnp.where(keep, cmax, NEG_INF)
            if valid is not None:
                cmax = jnp.where(valid, cmax, NEG_INF)
            m_prev = m_scr[s]
            m_new = jnp.maximum(m_prev, cmax)
            a_scr[2 * ws + slot, s] = jnp.exp2(m_prev - m_new)
            p = jnp.exp2(s_scr[2 * rs + slot, s] - m_new).astype(BF16)
            if keep is not None:
                p = jnp.where(keep, p, jnp.zeros_like(p))
            p_scr[2 * ws + slot, s] = p
            m_scr[s] = m_new

    def body(kk, rs, ws):
        first = kk == 0
        pa_blk = jnp.where(first, i, 2 * kk - 2)
        pb_blk = jnp.where(first, i, 2 * kk - 1)
        p_prev = [[p_scr[2 * rs + sl, s] for s in range(n)] for sl in range(2)]
        a_prev = [[a_scr[2 * rs + sl, s] for s in range(n)] for sl in range(2)]
        s_next = [[qk(jnp.minimum(2 * kk + 2 + sl, last_past), s) for s in range(n)] for sl in range(2)]
        pvs = [[pv(blk, s, p_prev[sl][s]) for s in range(n)] for sl, blk in enumerate((pa_blk, pb_blk))]
        softmax_stage(0, None, rs, ws, 2 * kk)
        softmax_stage(1, 2 * kk + 1 < i, rs, ws, jnp.minimum(2 * kk + 1, last_past))
        for sl in range(2):
            for s in range(n):
                store_scores(2 * ws + sl, s, s_next[sl][s])
        for s in range(n):
            acc_scr[s] = a_prev[1][s] * (a_prev[0][s] * acc_scr[s] + pvs[0][s]) + pvs[1][s]

    def two_bodies(kp, carry):
        body(2 * kp, 0, 1)
        body(2 * kp + 1, 1, 0)
        return carry

    nbody = (i + 1) // 2
    lax.fori_loop(0, nbody // 2, two_bodies, 0)

    @pl.when(nbody % 2 == 1)
    def _():
        body(nbody - 1, 0, 1)

    fs = nbody % 2
    kl = jnp.maximum(nbody - 1, 0)
    none = i == 0
    fa_blk = jnp.where(none, i, 2 * kl)
    fb_blk = jnp.where(none, i, jnp.minimum(2 * kl + 1, last_past))
    fb_scale = jnp.where(jnp.logical_or(none, i % 2 == 0), 1.0, 0.0)
    for s in range(n):
        acc_scr[s] = (a_scr[2 * fs + 1, s] * (a_scr[2 * fs, s] * acc_scr[s] + pv(fa_blk, s, p_scr[2 * fs, s]))
                      + fb_scale * pv(fb_blk, s, p_scr[2 * fs + 1, s]))


def _flash_scratch(n, nv, t):
    return [pltpu.VMEM((n, 1, t), F32), pltpu.VMEM((n, nv + ONES_ROWS, t), F32), pltpu.VMEM((4, n, t, t), F32),
            pltpu.VMEM((4, n, t, t), BF16), pltpu.VMEM((4, n, 1, t), F32), pltpu.VMEM((4, n, 1, t), F32)]


def _normalised(acc_scr, s, nv):
    return acc_scr[s, :nv, :] / acc_scr[s, nv:nv + 1, :]


def _half_mask(x, half):
    lane = _lane_iota((1, LANES))
    return jnp.where((lane >= HEAD_DIM) == bool(half), x, jnp.zeros_like(x))


def _mla_kernel(qt_ref, k_ref, vt_ref, ot_ref, *scr, t, nh):
    i = pl.program_id(2)
    ks = [slice(LANES * s, LANES * (s + 1)) for s in range(nh)]
    qs = [qt_ref[0, 0, ks[s], :] for s in range(nh)]
    vr = [slice(HEAD_DIM * s, HEAD_DIM * (s + 1)) for s in range(nh)]
    _flash_t(qs, k_ref, vt_ref, ks, vr, scr, i, t)
    for s in range(nh):
        ot_ref[0, 0, vr[s], :] = _normalised(scr[1], s, HEAD_DIM).astype(ot_ref.dtype)


def _mla_attention(qt, k, vt, nh):
    b, s, _ = k.shape
    _, nb, nv, t = vt.shape
    return pl.pallas_call(
        functools.partial(_mla_kernel, t=t, nh=nh),
        grid=(b, MLA_HEADS // nh, nb),
        in_specs=[pl.BlockSpec((1, 1, nh * LANES, t), lambda bi, p, i: (bi, i, p, 0)),
                  pl.BlockSpec((1, s, nh * LANES), lambda bi, p, i: (bi, 0, p)),
                  pl.BlockSpec((1, nb, nh * HEAD_DIM, t), lambda bi, p, i: (bi, 0, p, 0))],
        out_specs=pl.BlockSpec((1, 1, nh * HEAD_DIM, t), lambda bi, p, i: (bi, i, p, 0)),
        out_shape=jax.ShapeDtypeStruct((b, nb, nv, t), BF16),
        scratch_shapes=_flash_scratch(nh, HEAD_DIM, t),
        compiler_params=_cparams(("parallel", "parallel", "arbitrary")),
        name="mla_attention",
    )(qt, k, vt)


def _diff_kernel(q_ref, k_ref, vt_ref, lam_ref, sub_ref, ot_ref, *scr, t, nh, lam_init):
    i = pl.program_id(2)
    qs, ks, vr = [], [], []
    for hd in range(nh):
        cols = slice(LANES * hd, LANES * (hd + 1))
        qs += _qt_halves(q_ref[0, :, cols])
        ks += [cols, cols]
        vr += [cols, cols]
    _flash_t(qs, k_ref, vt_ref, ks, vr, scr, i, t)
    lv = lam_ref[...]
    lam = (jnp.exp(jnp.sum(lv[0:1] * lv[1:2], keepdims=True))
           - jnp.exp(jnp.sum(lv[2:3] * lv[3:4], keepdims=True)) + lam_init)
    for hd in range(nh):
        o = _normalised(scr[1], 2 * hd, LANES) - lam * _normalised(scr[1], 2 * hd + 1, LANES)
        o = o * lax.rsqrt(jnp.mean(o * o, axis=0, keepdims=True) + NORM_EPS) * sub_ref[...]
        ot_ref[0, 0, LANES * hd:LANES * (hd + 1), :] = (o * (1.0 - lam_init)).astype(ot_ref.dtype)


def _diff_attention(q, k, vt, lam_rows, subln_col, lam_init, nh):
    b, s, w = q.shape
    _, nb, _, t = vt.shape
    return pl.pallas_call(
        functools.partial(_diff_kernel, t=t, nh=nh, lam_init=lam_init),
        grid=(b, DIFF_HEADS // nh, nb),
        in_specs=[pl.BlockSpec((1, t, nh * LANES), lambda bi, p, i: (bi, i, p)),
                  pl.BlockSpec((1, s, nh * LANES), lambda bi, p, i: (bi, 0, p)),
                  pl.BlockSpec((1, nb, nh * LANES, t), lambda bi, p, i: (bi, 0, p, 0)),
                  pl.BlockSpec(lam_rows.shape, lambda bi, p, i: (0, 0)),
                  pl.BlockSpec(subln_col.shape, lambda bi, p, i: (0, 0))],
        out_specs=pl.BlockSpec((1, 1, nh * LANES, t), lambda bi, p, i: (bi, i, p, 0)),
        out_shape=jax.ShapeDtypeStruct((b, nb, w, t), BF16),
        scratch_shapes=_flash_scratch(2 * nh, LANES, t),
        compiler_params=_cparams(("parallel", "parallel", "arbitrary")),
        name="diff_attention",
    )(q, k, vt, lam_rows, subln_col)


def _moba_kernel(q_ref, qf_ref, k_ref, vt_ref, kmean_ref, ot_ref, sel_scr, *scr, t, npair):
    i = pl.program_id(2)
    nb = kmean_ref.shape[1]
    brow = lax.broadcasted_iota(jnp.int32, (nb, t), 0).astype(F32)
    past = brow < i.astype(F32)
    qs, ks, vr = [], [], []
    for pr in range(npair):
        cols = slice(LANES * pr, LANES * (pr + 1))
        qf, kmean = qf_ref[0, :, cols], kmean_ref[0, :, cols]
        qs += _qt_halves(q_ref[0, :, cols])
        for half in range(2):
            s = 2 * pr + half
            gate = lax.dot_general(kmean, _half_mask(qf, half), (((1,), (1,)), ((), ())),
                                   preferred_element_type=F32, precision=lax.Precision.HIGHEST)
            gate = jnp.where(past, gate, NEG_INF)
            sel = jnp.zeros((nb, t), F32)
            for _ in range(MOBA_TOPK):
                top = jnp.max(gate, axis=0, keepdims=True)
                first = jnp.min(jnp.where(gate == top, brow, float(nb)), axis=0, keepdims=True)
                pick = brow == first
                sel = jnp.where(pick, 1.0, sel)
                gate = jnp.where(pick, 2 * NEG_INF, gate)
            sel_scr[s] = jnp.where(past, sel, 0.0)
            ks.append(cols)
            vr.append(slice(HEAD_DIM * s, HEAD_DIM * (s + 1)))
    _flash_t(qs, k_ref, vt_ref, ks, vr, scr, i, t,
             past_mask=lambda s, jj: sel_scr[s, pl.ds(jj, 1), :] > 0.5)
    for s in range(2 * npair):
        ot_ref[0, 0, vr[s], :] = _normalised(scr[1], s, HEAD_DIM).astype(ot_ref.dtype)


def _moba_attention(q, qf, k, vt, kmean, npair):
    b, s, w = q.shape
    _, nb, _, t = vt.shape
    nkm = kmean.shape[1]
    lanes = npair * LANES
    return pl.pallas_call(
        functools.partial(_moba_kernel, t=t, npair=npair),
        grid=(b, w // lanes, nb),
        in_specs=[pl.BlockSpec((1, t, lanes), lambda bi, p, i: (bi, i, p)),
                  pl.BlockSpec((1, t, lanes), lambda bi, p, i: (bi, i, p)),
                  pl.BlockSpec((1, s, lanes), lambda bi, p, i: (bi, 0, p)),
                  pl.BlockSpec((1, nb, lanes, t), lambda bi, p, i: (bi, 0, p, 0)),
                  pl.BlockSpec((1, nkm, lanes), lambda bi, p, i: (bi, 0, p))],
        out_specs=pl.BlockSpec((1, 1, lanes, t), lambda bi, p, i: (bi, i, p, 0)),
        out_shape=jax.ShapeDtypeStruct((b, nb, w, t), BF16),
        scratch_shapes=[pltpu.VMEM((2 * npair, nkm, t), F32)] + _flash_scratch(2 * npair, HEAD_DIM, t),
        compiler_params=_cparams(("parallel", "parallel", "arbitrary")),
        name="moba_attention",
    )(q, qf, k, vt, kmean)


def _sw_kernel(q_ref, kc_ref, vc_ref, kp_ref, vp_ref, o_ref, l_ref, *, nbk):
    i = pl.program_id(1)
    blk = SW_BLOCK
    qi = lax.broadcasted_iota(jnp.int32, (blk, 2 * blk), 0)
    kj = lax.broadcasted_iota(jnp.int32, (blk, 2 * blk), 1)
    band = (kj >= qi) & (kj <= qi + blk)
    lane = _lane_iota((1, LANES))
    heads = [(pair, half) for pair in range(DIL_W // LANES) for half in range(2)]

    def scores(n):
        rows = slice(blk * n, blk * (n + 1))
        q = q_ref[0, rows, :]
        if n == 0:
            kprev, valid = kp_ref[0], band & ((kj >= blk) | (i > 0))
        else:
            kprev, valid = kc_ref[0, blk * (n - 1):blk * n, :], band
        kk = jnp.concatenate([kprev, kc_ref[0, rows, :]], axis=0)
        out = []
        for pair, half in heads:
            sl = slice(LANES * pair, LANES * (pair + 1))
            out.append(jnp.where(valid, _nt_dot(_half_mask(q[:, sl], half), kk[:, sl]), NEG_INF))
        return out

    nxt = scores(0)
    for n in range(nbk):
        cur, rows = nxt, slice(blk * n, blk * (n + 1))
        if n + 1 < nbk:
            nxt = scores(n + 1)
        vprev = vp_ref[0] if n == 0 else vc_ref[0, blk * (n - 1):blk * n, :]
        vv = jnp.concatenate([vprev, vc_ref[0, rows, :]], axis=0)
        outs, lses = [], []
        for (pair, half), sc in zip(heads, cur):
            m = jnp.max(sc, axis=-1, keepdims=True)
            p = jnp.exp2(sc - m)
            l = jnp.sum(p, axis=-1, keepdims=True)
            sl = slice(LANES * pair, LANES * (pair + 1))
            outs.append(jnp.dot(p.astype(BF16), vv[:, sl], preferred_element_type=F32) / l)
            lses.append(m + jnp.log2(l))
        for pair in range(DIL_W // LANES):
            sl = slice(LANES * pair, LANES * (pair + 1))
            o_ref[0, rows, sl] = jnp.where(lane < HEAD_DIM, outs[2 * pair], outs[2 * pair + 1])
            l_ref[0, rows, sl] = jnp.where(lane < HEAD_DIM, lses[2 * pair], lses[2 * pair + 1])


def _sliding_window(dil, nbk, name):
    lead, (m, w3) = dil.shape[:-2], dil.shape[-2:]
    seqs = dil.reshape((-1, m, w3))
    nbk = min(nbk, m // SW_BLOCK)
    rows = SW_BLOCK * nbk
    cur = lambda off: pl.BlockSpec((1, rows, DIL_W), lambda n, i: (n, i, off))
    prev = lambda off: pl.BlockSpec((1, SW_BLOCK, DIL_W), lambda n, i: (n, jnp.maximum(i * nbk - 1, 0), off))
    out_spec = pl.BlockSpec((1, rows, DIL_W), lambda n, i: (n, i, 0))
    o, l = pl.pallas_call(
        functools.partial(_sw_kernel, nbk=nbk),
        grid=(seqs.shape[0], m // rows),
        in_specs=[cur(0), cur(1), cur(2), prev(1), prev(2)],
        out_specs=[out_spec, out_spec],
        out_shape=[jax.ShapeDtypeStruct((seqs.shape[0], m, DIL_W), F32)] * 2,
        compiler_params=_cparams(("parallel", "arbitrary")),
        name=name,
    )(seqs, seqs, seqs, seqs, seqs)
    return o.reshape(lead + (m, DIL_W)), l.reshape(lead + (m, DIL_W))


def _mlp_tail(y, mlp_refs):
    sh_ref, sc_ref, g_ref, nw_ref, w1_ref, w2_ref = mlp_refs
    h = (_rms(y, nw_ref[...]) * (1.0 + sc_ref[0]) + sh_ref[0]).astype(BF16)
    acc = jnp.zeros(y.shape, F32)
    for c in range(w1_ref.shape[1] // MLP_CHUNK):
        cols = slice(MLP_CHUNK * c, MLP_CHUNK * (c + 1))
        a = jnp.maximum(jnp.dot(h, w1_ref[:, cols], preferred_element_type=F32), 0.0)
        acc = acc + jnp.dot((a * a).astype(BF16), w2_ref[cols, :], preferred_element_type=F32)
    return y + g_ref[0] * acc


def _proj_t(ot_ref, w_rows):
    return jnp.concatenate([_tn_dot(ot_ref[0, j], w_rows) for j in range(ot_ref.shape[1])], axis=0)


def _mlp_specs(mlp_args, d):
    per_b = pl.BlockSpec((1, 1, d), lambda bi, i: (bi, 0, 0))
    const = lambda a: pl.BlockSpec(a.shape, lambda bi, i: (0, 0), pipeline_mode=pl.Buffered(1))
    sh, sc, g2, nw, w1, w2 = mlp_args
    return [per_b, per_b, per_b, pl.BlockSpec(nw.shape, lambda bi, i: (0, 0)), const(w1), const(w2)]


def _even_out_kernel(x_ref, g_ref, oat_ref, o0_ref, o1_ref, o2_ref, l0_ref, l1_ref, l2_ref, w_ref,
                     *rest):
    *mlp_refs, y_ref, tscr = rest
    tm = x_ref.shape[1]
    ncol = DIL_W // LANES

    def token_major(ref, slot):
        r = ref.shape[1]
        for c in range(r):
            for j in range(ncol):
                tscr[slot * ncol + j, pl.ds(c, tm // r, stride=r), :] = ref[0, c, :, LANES * j:LANES * (j + 1)]
        return jnp.concatenate([tscr[slot * ncol + j] for j in range(ncol)], axis=1)

    o0, l0 = o0_ref[0], l0_ref[0]
    o1, l1 = token_major(o1_ref, 0), token_major(l1_ref, 1)
    o2, l2 = token_major(o2_ref, 2), token_major(l2_ref, 3)
    top = jnp.maximum(jnp.maximum(l0, l1), l2)
    w0, w1, w2 = jnp.exp2(l0 - top), jnp.exp2(l1 - top), jnp.exp2(l2 - top)
    ob = (w0 * o0 + w1 * o1 + w2 * o2) / (w0 + w1 + w2)
    na = oat_ref.shape[2]
    y = _proj_t(oat_ref, w_ref[:na, :]) + jnp.dot(ob.astype(BF16), w_ref[na:, :], preferred_element_type=F32)
    y_ref[0] = _mlp_tail(x_ref[0] + g_ref[0] * y, mlp_refs)


def _t_spec(a, tm):
    return pl.BlockSpec((1, tm // a.shape[-1]) + a.shape[2:], lambda bi, i: (bi, i, 0, 0))


def _even_out(x, g1, oat, o_dil, l_dil, w_out, mlp_args, tm):
    b, s, d = x.shape
    tok = lambda w: pl.BlockSpec((1, tm, w), lambda bi, i: (bi, i, 0))

    def dil_spec(a):
        if a.ndim == 3:
            return tok(DIL_W)
        r = a.shape[1]
        return pl.BlockSpec((1, r, tm // r, DIL_W), lambda bi, i: (bi, 0, i, 0))

    return pl.pallas_call(
        _even_out_kernel,
        grid=(b, s // tm),
        in_specs=[tok(d), pl.BlockSpec((1, 1, d), lambda bi, i: (bi, 0, 0)), _t_spec(oat, tm)]
                 + [dil_spec(a) for a in (*o_dil, *l_dil)]
                 + [pl.BlockSpec(w_out.shape, lambda bi, i: (0, 0), pipeline_mode=pl.Buffered(1))]
                 + _mlp_specs(mlp_args, d),
        out_specs=tok(d),
        out_shape=jax.ShapeDtypeStruct((b, s, d), F32),
        scratch_shapes=[pltpu.VMEM((4 * DIL_W // LANES, tm, LANES), F32)],
        compiler_params=_cparams(("parallel", "arbitrary")),
        name="even_out_mlp",
    )(x, g1, oat, *o_dil, *l_dil, w_out, *mlp_args)


def _odd_out_kernel(x_ref, g_ref, oct_ref, odt_ref, w_ref, *rest):
    *mlp_refs, y_ref = rest
    nc = oct_ref.shape[2]
    y = _proj_t(oct_ref, w_ref[:nc, :]) + _proj_t(odt_ref, w_ref[nc:, :])
    y_ref[0] = _mlp_tail(x_ref[0] + g_ref[0] * y, mlp_refs)


def _odd_out(x, g1, oct, odt, w_out, mlp_args, tm):
    b, s, d = x.shape
    tok = lambda w: pl.BlockSpec((1, tm, w), lambda bi, i: (bi, i, 0))
    return pl.pallas_call(
        _odd_out_kernel,
        grid=(b, s // tm),
        in_specs=[tok(d), pl.BlockSpec((1, 1, d), lambda bi, i: (bi, 0, 0)), _t_spec(oct, tm), _t_spec(odt, tm),
                  pl.BlockSpec(w_out.shape, lambda bi, i: (0, 0), pipeline_mode=pl.Buffered(1))]
                 + _mlp_specs(mlp_args, d),
        out_specs=tok(d),
        out_shape=jax.ShapeDtypeStruct((b, s, d), F32),
        compiler_params=_cparams(("parallel", "arbitrary")),
        name="odd_out_mlp",
    )(x, g1, oct, odt, w_out, *mlp_args)


def _rope_tables(positions):
    pos = positions.astype(F32)[:, :, None]
    half = HEAD_DIM // 2
    inv = ROPE_THETA ** (-jnp.arange(half, dtype=F32) / half)
    ang = pos * inv[_freq_order()]
    c, s = jnp.cos(ang), jnp.sin(ang)

    def tiled(x, width):
        reps = LANES // width
        return jnp.broadcast_to(x[:, :, None, :], x.shape[:2] + (reps, width)).reshape(x.shape[:2] + (LANES,))

    lane = np.arange(LANES)
    sign = np.where(lane % HEAD_DIM < half, -1.0, 1.0).astype(np.float32)
    c64, s64 = tiled(c, half), tiled(s, half) * sign
    hm = MLA_ROPE // 2
    rope = (lane >= MLA_NOPE) & (lane < MLA_QK)
    sign_m = np.where(lane < MLA_NOPE + hm, -1.0, 1.0).astype(np.float32)
    cm = jnp.where(rope, tiled(c[..., :hm], hm), 1.0)
    sm = jnp.where(rope, tiled(s[..., :hm], hm) * sign_m, 0.0)
    return cm, sm, c64, s64


def _freq_order():
    half = HEAD_DIM // 2
    return np.concatenate([np.arange(0, half, 2), np.arange(1, half, 2)])


def _head_perm(n):
    within = np.concatenate([_freq_order(), HEAD_DIM // 2 + _freq_order()])
    idx = np.arange(n)
    return idx // HEAD_DIM * HEAD_DIM + within[idx % HEAD_DIM]


def _partner(n, width):
    idx = np.arange(n)
    return np.where(idx % width < width // 2, idx + width // 2, idx - width // 2)


def _even_weights(w_in, w_uq, w_ukv, qn, kn):
    w_in, w_uq, w_ukv = w_in.astype(BF16), w_uq.astype(BF16), w_ukv.astype(BF16)
    o2 = MLA_Q_RANK + MLA_KV_RANK
    o3 = o2 + MLA_ROPE
    d = w_in.shape[0]
    p_rope = _partner(MLA_ROPE, MLA_ROPE)
    zeros = lambda n: jnp.zeros((d, n), w_in.dtype)
    kr = w_in[:, o2:o3]
    tail = LANES - MLA_QK
    krp = jnp.concatenate([zeros(MLA_NOPE), kr, zeros(tail)], axis=1)
    krp_rot = jnp.concatenate([zeros(MLA_NOPE), kr[:, p_rope], zeros(tail)], axis=1)
    dil = w_in[:, o3:].reshape(d, 3, len(DIL_CONFIGS), DIL_W)
    dil = jnp.concatenate([dil[:, :2][..., _head_perm(DIL_W)], dil[:, 2:]], axis=1)
    dil_main = dil.transpose(0, 2, 1, 3).reshape(d, -1)
    dil_rot = dil[:, :2][..., _partner(DIL_W, HEAD_DIM)].transpose(0, 2, 1, 3).reshape(d, -1)
    win = jnp.concatenate([w_in[:, :o2], krp, krp_rot, dil_main, dil_rot], axis=1)

    padq = ((0, 0), (0, 0), (0, tail))
    wuq_rot = jnp.concatenate([w_uq[:, :, :MLA_NOPE], w_uq[:, :, MLA_NOPE:][:, :, p_rope]], axis=-1)
    wuq = jnp.concatenate([jnp.pad(w_uq, padq).reshape(MLA_Q_RANK, -1),
                           jnp.pad(wuq_rot, padq).reshape(MLA_Q_RANK, -1)], axis=1)
    wuk = jnp.pad(w_ukv[:, :, :MLA_NOPE], ((0, 0), (0, 0), (0, LANES - MLA_NOPE))).reshape(MLA_KV_RANK, -1)
    wvt = w_ukv[:, :, MLA_NOPE:].reshape(MLA_KV_RANK, -1).T

    def gains(g):
        rot = jnp.concatenate([g[:MLA_NOPE], g[MLA_NOPE:][p_rope]])
        return jnp.stack([jnp.pad(g, (0, tail)), jnp.pad(rot, (0, tail))]).astype(F32)

    return win.astype(BF16), wuq.astype(BF16), wuk.astype(BF16), wvt.astype(BF16), gains(qn), gains(kn)


def _odd_weights(w_in):
    w_in = w_in.astype(BF16)
    nqk = DIFF_HEADS * LANES
    nm = MOBA_HEADS * HEAD_DIM
    main = jnp.concatenate([w_in[:, :2 * nqk], w_in[:, 3 * nqk:3 * nqk + 2 * nm]], axis=1)
    main = main[:, _head_perm(main.shape[1])]
    win =jnp.concatenate([main, main[:, _partner(main.shape[1], HEAD_DIM)]], axis=1)
    wvt = jnp.concatenate([w_in[:, 2 * nqk:3 * nqk], w_in[:, 3 * nqk + 2 * nm:]], axis=1).T
    return win.astype(BF16), wvt.astype(BF16)


def _gains64(g):
    g = g[_head_perm(HEAD_DIM)]
    rot = g[_partner(HEAD_DIM, HEAD_DIM)]
    return jnp.stack([jnp.concatenate([g, g]), jnp.concatenate([rot, rot])]).astype(F32)


def kernel(x, c, positions, ada_w, ada_b, norm_mix, norm_mlp, mlp_w1, mlp_w2, even_w_in, even_w_out, mla_q_lat_norm, mla_kv_lat_norm, mla_w_uq, mla_w_ukv, mla_q_norm, mla_k_norm, dil_q_norm, dil_k_norm, odd_w_in, odd_w_out, diff_q_norm, diff_k_norm, diff_lambda, diff_subln, moba_q_norm, moba_k_norm):
    b, s, d = x.shape
    depth = ada_w.shape[0]
    t_attn, tm_in, tm_out = ATTN_BLOCK, TOKEN_TILE, TOKEN_TILE

    mod = _adaln(c, ada_w, ada_b)
    cm, sm, c64, s64 = _rope_tables(positions)

    for layer in range(depth):
        sh1, sc1, g1, sh2, sc2, g2 = [mod[layer, :, d * t:d * (t + 1)].reshape(b, 1, d) for t in range(6)]
        mlp_args = (sh2, sc2, g2, _row(norm_mlp[layer]), mlp_w1[layer].astype(BF16), mlp_w2[layer].astype(BF16))
        i = layer // 2
        if layer % 2 == 0:
            win, wuq, wuk, wvt, qn, kn = _even_weights(even_w_in[i], mla_w_uq[i], mla_w_ukv[i],
                                                       mla_q_norm[i], mla_k_norm[i])
            q, k, vt, *dils = _even_in(x, sh1, sc1, _row(norm_mix[layer]), win, wuq, wuk, wvt,
                                       _row(mla_q_lat_norm[i]), _row(mla_kv_lat_norm[i]), qn, kn,
                                       _gains64(dil_q_norm[i]), _gains64(dil_k_norm[i]), cm, sm, c64, s64, tm_in, t_attn)
            o_at = _mla_attention(q, k, vt, MLA_HEADS_PER_STEP)
            o_dil, l_dil = zip(*[_sliding_window(dg, SW_BLOCKS_PER_STEP, f"sliding_window_g{g}")
                                 for g, dg in enumerate(dils)])
            x = _even_out(x, g1, o_at, o_dil, l_dil, even_w_out[i].astype(BF16), mlp_args, tm_out)
        else:
            lam_init = 0.8 - 0.6 * math.exp(-0.3 * layer)
            win, wvt = _odd_weights(odd_w_in[i])
            qc, kc, qm, qmf, km, vct, vmt, kmean = _odd_in(
                x, sh1, sc1, _row(norm_mix[layer]), win, wvt,
                _gains64(diff_q_norm[i]), _gains64(diff_k_norm[i]), _gains64(moba_q_norm[i]),
                _gains64(moba_k_norm[i]),
                c64, s64, tm_in, t_attn)
            o_ct = _diff_attention(qc, kc, vct, diff_lambda[i].astype(F32),
                                   diff_subln[i].reshape(-1, 1).astype(F32), lam_init, DIFF_HEADS_PER_STEP)
            o_dt = _moba_attention(qm, qmf, km, vmt, kmean.reshape(b, s // MOBA_BLOCK, kmean.shape[-1]),
                                   MOBA_PAIRS_PER_STEP)
            x = _odd_out(x, g1, o_ct, o_dt, odd_w_out[i].astype(BF16), mlp_args, tm_out)
    return x
```

```python
import functools
import math

import jax
import jax.numpy as jnp
import numpy as np
from jax import lax
from jax.experimental import pallas as pl
from jax.experimental.pallas import tpu as pltpu

F32 = jnp.float32
BF16 = jnp.bfloat16

LANES = 128
HEAD_DIM = 64
ROPE_THETA = 10000.0
NORM_EPS = 1e-6
NEG_INF = -1e30
LOG2E = math.log2(math.e)

MLA_HEADS = 8
MLA_Q_RANK = 384
MLA_KV_RANK = 256
MLA_NOPE = 64
MLA_ROPE = 32
MLA_QK = MLA_NOPE + MLA_ROPE
DIL_CONFIGS = ((128, 1), (512, 4), (2048, 16))
DIL_HEADS = 4
DIL_W = DIL_HEADS * HEAD_DIM
DIFF_HEADS = 4
MOBA_HEADS = 8
MOBA_BLOCK = 256
MOBA_TOPK = 3
SW_BLOCK = 128
ONES_ROWS = 16

VMEM_LIMIT = 56 * 1024 * 1024

ATTN_BLOCK = MOBA_BLOCK
TOKEN_TILE = 2 * ATTN_BLOCK
MLP_CHUNK = 1024
ADALN_TILE = 2048
SW_BLOCKS_PER_STEP = 8
MLA_HEADS_PER_STEP = 4
DIFF_HEADS_PER_STEP = 2
MOBA_PAIRS_PER_STEP = 2


def _cparams(sem):
    return pltpu.CompilerParams(dimension_semantics=sem, vmem_limit_bytes=VMEM_LIMIT)


def _nt_dot(a, b):
    return lax.dot_general(a, b, (((1,), (1,)), ((), ())), preferred_element_type=F32)


def _tn_dot(a, b):
    return lax.dot_general(a, b, (((0,), (0,)), ((), ())), preferred_element_type=F32)


def _rms(x, w):
    return x * lax.rsqrt(jnp.mean(x * x, axis=-1, keepdims=True) + NORM_EPS) * w


def _lane_iota(shape):
    return lax.broadcasted_iota(jnp.int32, shape, len(shape) - 1)


def _adaln_kernel(c_ref, w_ref, b_ref, o_ref):
    c = c_ref[...]
    cond = c * (1.0 / (1.0 + jnp.exp(-c)))
    o_ref[0] = jnp.dot(cond, w_ref[0], preferred_element_type=F32,
                       precision=lax.Precision.HIGHEST) + b_ref[0]


def _adaln(c, ada_w, ada_b):
    depth, d, n = ada_w.shape
    b = c.shape[0]
    tn = ADALN_TILE
    return pl.pallas_call(
        _adaln_kernel,
        grid=(depth, n // tn),
        in_specs=[pl.BlockSpec((b, d), lambda l, j: (0, 0)),
                  pl.BlockSpec((1, d, tn), lambda l, j: (l, 0, j)),
                  pl.BlockSpec((1, 1, tn), lambda l, j: (l, 0, j))],
        out_specs=pl.BlockSpec((1, b, tn), lambda l, j: (l, 0, j)),
        out_shape=jax.ShapeDtypeStruct((depth, b, n), F32),
        compiler_params=_cparams(("parallel", "parallel")),
        name="adaln",
    )(c, ada_w, ada_b.reshape(depth, 1, n))


def _inv_rms64(x):
    lane = _lane_iota((1, LANES))
    lo = lane < HEAD_DIM
    sq = x * x
    s_lo = jnp.sum(jnp.where(lo, sq, 0.0), axis=-1, keepdims=True)
    s_hi = jnp.sum(jnp.where(lo, 0.0, sq), axis=-1, keepdims=True)
    return lax.rsqrt(jnp.where(lo, s_lo, s_hi) * (1.0 / HEAD_DIM) + NORM_EPS)


def _store_t_blocks(out_ref, xt):
    t = out_ref.shape[-1]
    for j in range(out_ref.shape[1]):
        out_ref[0, j] = xt[:, t * j:t * (j + 1)]


def _store_col_t(out_ref, col, x):
    t = out_ref.shape[-1]
    for j in range(out_ref.shape[1]):
        out_ref[0, j, LANES * col:LANES * (col + 1), :] = x[t * j:t * (j + 1)].T.astype(out_ref.dtype)


def _t_blocks(b, s, rows, tm, t):
    return (pl.BlockSpec((1, tm // t, rows, t), lambda bi, i: (bi, i, 0, 0)),
            jax.ShapeDtypeStruct((b, s // t, rows, t), BF16))


def _even_in_kernel(x_ref, sh_ref, sc_ref, nw_ref, win_ref, wuq_ref, wuk_ref, wvt_ref,
                    qlat_ref, kvlat_ref, qn_ref, kn_ref, dqn_ref, dkn_ref,
                    cm_ref, sm_ref, c64_ref, s64_ref,
                    qt_out, k_out, vt_out, d0_out, d1_out, d2_out, dscr):
    x = x_ref[0]
    tm = x.shape[0]
    h = _rms(x, nw_ref[...]) * (1.0 + sc_ref[0]) + sh_ref[0]
    u = jnp.dot(h.astype(BF16), win_ref[...], preferred_element_type=F32)

    o1 = MLA_Q_RANK
    o2 = o1 + MLA_KV_RANK
    o_kr, o_krr, o_dil = o2, o2 + LANES, o2 + 2 * LANES
    n_dil = 3 * DIL_W * len(DIL_CONFIGS)
    o_rot = o_dil + n_dil
    nq = MLA_HEADS * LANES
    cqn = _rms(u[:, :o1], qlat_ref[...]).astype(BF16)
    qp = jnp.dot(cqn, wuq_ref[...], preferred_element_type=F32)
    ckvn = _rms(u[:, o1:o2], kvlat_ref[...]).astype(BF16)
    kvp = jnp.dot(ckvn, wuk_ref[...], preferred_element_type=F32)
    _store_t_blocks(vt_out, _nt_dot(wvt_ref[...], ckvn).astype(BF16))
    kr, kr_rot = u[:, o_kr:o_kr + LANES], u[:, o_krr:o_krr + LANES]

    cm, sm = cm_ref[0], sm_ref[0]
    q_scale = MLA_QK ** -0.5 * LOG2E
    qgc, qgs = qn_ref[0:1] * cm * q_scale, qn_ref[1:2] * sm * q_scale
    kgc, kgs = kn_ref[0:1] * cm, kn_ref[1:2] * sm
    kr_term = kr_rot * kgs
    for hd in range(MLA_HEADS):
        sl = slice(LANES * hd, LANES * (hd + 1))
        qh = qp[:, sl]
        inv = lax.rsqrt(jnp.sum(qh * qh, -1, keepdims=True) * (1.0 / MLA_QK) + NORM_EPS)
        _store_col_t(qt_out, hd, inv * (qh * qgc + qp[:, nq + LANES * hd:nq + LANES * (hd + 1)] * qgs))
        kh = kvp[:, sl] + kr
        inv = lax.rsqrt(jnp.sum(kh * kh, -1, keepdims=True) * (1.0 / MLA_QK) + NORM_EPS)
        k_out[0, :, sl] = (inv * (kh * kgc + kr_term)).astype(BF16)

    c64, s64 = c64_ref[0], s64_ref[0]
    d_scale = HEAD_DIM ** -0.5 * LOG2E
    dqc, dqs = dqn_ref[0:1] * c64 * d_scale, dqn_ref[1:2] * s64 * d_scale
    dkc, dks = dkn_ref[0:1] * c64, dkn_ref[1:2] * s64
    ncol = DIL_W // LANES
    for g, d_out in enumerate((d0_out, d1_out, d2_out)):
        _, r = DIL_CONFIGS[g]
        base = o_dil + 3 * DIL_W * g
        rbase = o_rot + 2 * DIL_W * g
        for j in range(3 * ncol):
            xc = u[:, base + LANES * j:base + LANES * (j + 1)]
            if j < 2 * ncol:
                xr = u[:, rbase + LANES * j:rbase + LANES * (j + 1)]
                gc, gs = (dqc, dqs) if j < ncol else (dkc, dks)
                xc = _inv_rms64(xc) * (xc * gc + xr * gs)
            if r == 1:
                d_out[0, :, LANES * j:LANES * (j + 1)] = xc.astype(BF16)
            else:
                dscr[j] = xc
                for c in range(r):
                    d_out[0, c, :, LANES * j:LANES * (j + 1)] = dscr[j, pl.ds(c, tm // r, stride=r), :].astype(BF16)


def _row(v):
    return v.reshape(1, -1).astype(F32)


def _even_in(x, sh, sc, nw, win, wuq, wuk, wvt, qlat, kvlat, qn, kn, dqn, dkn, cm, sm, c64, s64, tm, t):
    b, s, d = x.shape
    tok = lambda w: pl.BlockSpec((1, tm, w), lambda bi, i: (bi, i, 0))
    per_b = pl.BlockSpec((1, 1, d), lambda bi, i: (bi, 0, 0))
    full = lambda a: pl.BlockSpec(a.shape, lambda bi, i: (0,) * a.ndim, pipeline_mode=pl.Buffered(1))
    vt_spec, vt_shape = _t_blocks(b, s, wvt.shape[0], tm, t)
    qt_spec, qt_shape = _t_blocks(b, s, MLA_HEADS * LANES, tm, t)
    dil_specs, dil_shapes = [], []
    for _, r in DIL_CONFIGS:
        if r == 1:
            dil_specs.append(tok(3 * DIL_W))
            dil_shapes.append(jax.ShapeDtypeStruct((b, s, 3 * DIL_W), BF16))
        else:
            dil_specs.append(pl.BlockSpec((1, r, tm // r, 3 * DIL_W), lambda bi, i: (bi, 0, i, 0)))
            dil_shapes.append(jax.ShapeDtypeStruct((b, r, s // r, 3 * DIL_W), BF16))
    return pl.pallas_call(
        _even_in_kernel,
        grid=(b, s // tm),
        in_specs=[tok(d), per_b, per_b, full(nw), full(win), full(wuq), full(wuk), full(wvt),
                  full(qlat), full(kvlat), full(qn), full(kn), full(dqn), full(dkn),
                  tok(LANES), tok(LANES), tok(LANES), tok(LANES)],
        out_specs=[qt_spec, tok(MLA_HEADS * LANES), vt_spec] + dil_specs,
        out_shape=[qt_shape, jax.ShapeDtypeStruct((b, s, MLA_HEADS * LANES), BF16), vt_shape] + dil_shapes,
        scratch_shapes=[pltpu.VMEM((3 * DIL_W // LANES, tm, LANES), F32)],
        compiler_params=_cparams(("parallel", "arbitrary")),
        name="even_in_proj",
    )(x, sh, sc, nw, win, wuq, wuk, wvt, qlat, kvlat, qn, kn, dqn, dkn, cm, sm, c64, s64)


def _odd_in_kernel(x_ref, sh_ref, sc_ref, nw_ref, win_ref, wvt_ref, dqn_ref, dkn_ref, mqn_ref, mkn_ref,
                   c64_ref, s64_ref,
                   qc_out, kc_out, qm_out, qmf_out, km_out, vct_out, vmt_out, kmean_out):
    x = x_ref[0]
    tm = x.shape[0]
    h = (_rms(x, nw_ref[...]) * (1.0 + sc_ref[0]) + sh_ref[0]).astype(BF16)
    u = jnp.dot(h, win_ref[...], preferred_element_type=F32)
    vt = _nt_dot(wvt_ref[...], h).astype(BF16)
    nvc = vct_out.shape[2]
    _store_t_blocks(vct_out, vt[:nvc])
    _store_t_blocks(vmt_out, vt[nvc:])
    c64, s64 = c64_ref[0], s64_ref[0]
    scale = HEAD_DIM ** -0.5 * LOG2E
    nqk = DIFF_HEADS * LANES
    nm = MOBA_HEADS * HEAD_DIM
    rot = 2 * nqk + 2 * nm

    def tables(gain_ref, mult):
        return gain_ref[0:1] * c64 * mult, gain_ref[1:2] * s64 * mult

    def roped(off, col, tab):
        a = off + LANES * col
        xc, xr = u[:, a:a + LANES], u[:, rot + a:rot + a + LANES]
        return _inv_rms64(xc) * (xc * tab[0] + xr * tab[1])

    t_qc, t_kc, t_qm, t_km = tables(dqn_ref, scale), tables(dkn_ref, 1.0), tables(mqn_ref, 1.0), tables(mkn_ref, 1.0)
    for col in range(nqk // LANES):
        sl = slice(LANES * col, LANES * (col + 1))
        qc_out[0, :, sl] = roped(0, col, t_qc).astype(BF16)
        kc_out[0, :, sl] = roped(nqk, col, t_kc).astype(BF16)

    base = 2 * nqk
    for col in range(nm // LANES):
        sl = slice(LANES * col, LANES * (col + 1))
        qm = roped(base, col, t_qm)
        qmf_out[0, :, sl] = qm
        qm_out[0, :, sl] = (qm * scale).astype(BF16)
        km = roped(base + nm, col, t_km)
        km_out[0, :, sl] = km.astype(BF16)
        for blk in range(tm // MOBA_BLOCK):
            rows = slice(MOBA_BLOCK * blk, MOBA_BLOCK * (blk + 1))
            kmean_out[0, 0, blk:blk + 1, sl] = jnp.mean(km[rows], axis=0, keepdims=True)


def _odd_in(x, sh, sc, nw, win, wvt, dqn, dkn, mqn, mkn, c64, s64, tm, t):
    b, s, d = x.shape
    tok = lambda w: pl.BlockSpec((1, tm, w), lambda bi, i: (bi, i, 0))
    per_b = pl.BlockSpec((1, 1, d), lambda bi, i: (bi, 0, 0))
    full = lambda a: pl.BlockSpec(a.shape, lambda bi, i: (0,) * a.ndim, pipeline_mode=pl.Buffered(1))
    w = DIFF_HEADS * LANES
    nblk = tm // MOBA_BLOCK
    shp = lambda dt: jax.ShapeDtypeStruct((b, s, w), dt)
    vt_spec, vt_shape = _t_blocks(b, s, wvt.shape[0] // 2, tm, t)
    return pl.pallas_call(
        _odd_in_kernel,
        grid=(b, s // tm),
        in_specs=[tok(d), per_b, per_b, full(nw), full(win), full(wvt), full(dqn), full(dkn), full(mqn), full(mkn),
                  tok(LANES), tok(LANES)],
        out_specs=[tok(w)] * 5 + [vt_spec, vt_spec, pl.BlockSpec((1, 1, nblk, w), lambda bi, i: (bi, i, 0, 0))],
        out_shape=[shp(BF16), shp(BF16), shp(BF16), shp(F32), shp(BF16), vt_shape, vt_shape,
                   jax.ShapeDtypeStruct((b, s // tm, nblk, w), F32)],
        compiler_params=_cparams(("parallel", "arbitrary")),
        name="odd_in_proj",
    )(x, sh, sc, nw, win, wvt, dqn, dkn, mqn, mkn, c64, s64)


def _qt_halves(q):
    qt = q.astype(F32).T
    row = lax.broadcasted_iota(jnp.int32, qt.shape, 0)
    return [jnp.where((row >= HEAD_DIM) == bool(half), qt, 0.0).astype(BF16) for half in range(2)]


def _flash_t(qts, k_ref, vt_ref, k_slices, v_rows, scr, i, t, past_mask=None):
    m_scr, acc_scr, s_scr, p_scr, a_scr, b_scr = scr
    n = len(qts)
    ones = jnp.ones((ONES_ROWS, t), BF16)

    def qk(blk, s):
        b0 = pl.multiple_of(blk * t, t)
        return jnp.dot(k_ref[0, pl.ds(b0, t), k_slices[s]], qts[s], preferred_element_type=F32)

    def pv(blk, s, p):
        vt = jnp.concatenate([vt_ref[0, blk, v_rows[s], :], ones], axis=0)
        return jnp.dot(vt, p, preferred_element_type=F32)

    def store_scores(slot, s, st):
        s_scr[slot, s] = st
        b_scr[slot, s] = jnp.max(st, axis=0, keepdims=True)

    krow = lax.broadcasted_iota(jnp.int32, (t, t), 0)
    qcol = lax.broadcasted_iota(jnp.int32, (t, t), 1)
    causal = krow <= qcol
    last_past = jnp.maximum(i - 1, 0)
    for s in range(n):
        st = jnp.where(causal, qk(i, s), NEG_INF)
        m = jnp.max(st, axis=0, keepdims=True)
        m_scr[s] = m
        p_scr[0, s] = jnp.exp2(st - m).astype(BF16)
        p_scr[1, s] = jnp.zeros((t, t), BF16)
        a_scr[0, s] = jnp.ones_like(m)
        a_scr[1, s] = jnp.ones_like(m)
        acc_scr[s] = jnp.zeros(acc_scr.shape[1:], F32)
        store_scores(0, s, qk(0, s))
        store_scores(1, s, qk(jnp.minimum(1, last_past), s))

    def softmax_stage(slot, valid, rs, ws, blk):
        for s in range(n):
            cmax = b_scr[2 * rs + slot, s]
            keep = None if past_mask is None else past_mask(s, blk)
            if keep is not None:
                cmax = jnp.where(keep, cmax, NEG_INF)
            if valid is not None:
                cmax = jnp.where(valid, cmax, NEG_INF)
            m_prev = m_scr[s]
            m_new = jnp.maximum(m_prev, cmax)
            a_scr[2 * ws + slot, s] = jnp.exp2(m_prev - m_new)
            p = jnp.exp2(s_scr[2 * rs + slot, s] - m_new).astype(BF16)
            if keep is not None:
                p = jnp.where(keep, p, jnp.zeros_like(p))
            p_scr[2 * ws + slot, s] = p
            m_scr[s] = m_new

    def body(kk, rs, ws):
        first = kk == 0
        pa_blk = jnp.where(first, i, 2 * kk - 2)
        pb_blk = jnp.where(first, i, 2 * kk - 1)
        p_prev = [[p_scr[2 * rs + sl, s] for s in range(n)] for sl in range(2)]
        a_prev = [[a_scr[2 * rs + sl, s] for s in range(n)] for sl in range(2)]
        s_next = [[qk(jnp.minimum(2 * kk + 2 + sl, last_past), s) for s in range(n)] for sl in range(2)]
        pvs = [[pv(blk, s, p_prev[sl][s]) for s in range(n)] for sl, blk in enumerate((pa_blk, pb_blk))]
        softmax_stage(0, None, rs, ws, 2 * kk)
        softmax_stage(1, 2 * kk + 1 < i, rs, ws, jnp.minimum(2 * kk + 1, last_past))
        for sl in range(2):
            for s in range(n):
                store_scores(2 * ws + sl, s, s_next[sl][s])
        for s in range(n):
            acc_scr[s] = a_prev[1][s] * (a_prev[0][s] * acc_scr[s] + pvs[0][s]) + pvs[1][s]

    def two_bodies(kp, carry):
        body(2 * kp, 0, 1)
        body(2 * kp + 1, 1, 0)
        return carry

    nbody = (i + 1) // 2
    lax.fori_loop(0, nbody // 2, two_bodies, 0)

    @pl.when(nbody % 2 == 1)
    def _():
        body(nbody - 1, 0, 1)

    fs = nbody % 2
    kl = jnp.maximum(nbody - 1, 0)
    none = i == 0
    fa_blk = jnp.where(none, i, 2 * kl)
    fb_blk = jnp.where(none, i, jnp.minimum(2 * kl + 1, last_past))
    fb_scale = jnp.where(jnp.logical_or(none, i % 2 == 0), 1.0, 0.0)
    for s in range(n):
        acc_scr[s] = (a_scr[2 * fs + 1, s] * (a_scr[2 * fs, s] * acc_scr[s] + pv(fa_blk, s, p_scr[2 * fs, s]))
                      + fb_scale * pv(fb_blk, s, p_scr[2 * fs + 1, s]))


def _flash_scratch(n, nv, t):
    return [pltpu.VMEM((n, 1, t), F32), pltpu.VMEM((n, nv + ONES_ROWS, t), F32), pltpu.VMEM((4, n, t, t), F32),
            pltpu.VMEM((4, n, t, t), BF16), pltpu.VMEM((4, n, 1, t), F32), pltpu.VMEM((4, n, 1, t), F32)]


def _normalised(acc_scr, s, nv):
    return acc_scr[s, :nv, :] / acc_scr[s, nv:nv + 1, :]


def _half_mask(x, half):
    lane = _lane_iota((1, LANES))
    return jnp.where((lane >= HEAD_DIM) == bool(half), x, jnp.zeros_like(x))


def _mla_kernel(qt_ref, k_ref, vt_ref, ot_ref, *scr, t, nh):
    i = pl.program_id(2)
    ks = [slice(LANES * s, LANES * (s + 1)) for s in range(nh)]
    qs = [qt_ref[0, 0, ks[s], :] for s in range(nh)]
    vr = [slice(HEAD_DIM * s, HEAD_DIM * (s + 1)) for s in range(nh)]
    _flash_t(qs, k_ref, vt_ref, ks, vr, scr, i, t)
    for s in range(nh):
        ot_ref[0, 0, vr[s], :] = _normalised(scr[1], s, HEAD_DIM).astype(ot_ref.dtype)


def _mla_attention(qt, k, vt, nh):
    b, s, _ = k.shape
    _, nb, nv, t = vt.shape
    return pl.pallas_call(
        functools.partial(_mla_kernel, t=t, nh=nh),
        grid=(b, MLA_HEADS // nh, nb),
        in_specs=[pl.BlockSpec((1, 1, nh * LANES, t), lambda bi, p, i: (bi, i, p, 0)),
                  pl.BlockSpec((1, s, nh * LANES), lambda bi, p, i: (bi, 0, p)),
                  pl.BlockSpec((1, nb, nh * HEAD_DIM, t), lambda bi, p, i: (bi, 0, p, 0))],
        out_specs=pl.BlockSpec((1, 1, nh * HEAD_DIM, t), lambda bi, p, i: (bi, i, p, 0)),
        out_shape=jax.ShapeDtypeStruct((b, nb, nv, t), BF16),
        scratch_shapes=_flash_scratch(nh, HEAD_DIM, t),
        compiler_params=_cparams(("parallel", "parallel", "arbitrary")),
        name="mla_attention",
    )(qt, k, vt)


def _diff_kernel(q_ref, k_ref, vt_ref, lam_ref, sub_ref, ot_ref, *scr, t, nh, lam_init):
    i = pl.program_id(2)
    qs, ks, vr = [], [], []
    for hd in range(nh):
        cols = slice(LANES * hd, LANES * (hd + 1))
        qs += _qt_halves(q_ref[0, :, cols])
        ks += [cols, cols]
        vr += [cols, cols]
    _flash_t(qs, k_ref, vt_ref, ks, vr, scr, i, t)
    lv = lam_ref[...]
    lam = (jnp.exp(jnp.sum(lv[0:1] * lv[1:2], keepdims=True))
           - jnp.exp(jnp.sum(lv[2:3] * lv[3:4], keepdims=True)) + lam_init)
    for hd in range(nh):
        o = _normalised(scr[1], 2 * hd, LANES) - lam * _normalised(scr[1], 2 * hd + 1, LANES)
        o = o * lax.rsqrt(jnp.mean(o * o, axis=0, keepdims=True) + NORM_EPS) * sub_ref[...]
        ot_ref[0, 0, LANES * hd:LANES * (hd + 1), :] = (o * (1.0 - lam_init)).astype(ot_ref.dtype)


def _diff_attention(q, k, vt, lam_rows, subln_col, lam_init, nh):
    b, s, w = q.shape
    _, nb, _, t = vt.shape
    return pl.pallas_call(
        functools.partial(_diff_kernel, t=t, nh=nh, lam_init=lam_init),
        grid=(b, DIFF_HEADS // nh, nb),
        in_specs=[pl.BlockSpec((1, t, nh * LANES), lambda bi, p, i: (bi, i, p)),
                  pl.BlockSpec((1, s, nh * LANES), lambda bi, p, i: (bi, 0, p)),
                  pl.BlockSpec((1, nb, nh * LANES, t), lambda bi, p, i: (bi, 0, p, 0)),
                  pl.BlockSpec(lam_rows.shape, lambda bi, p, i: (0, 0)),
                  pl.BlockSpec(subln_col.shape, lambda bi, p, i: (0, 0))],
        out_specs=pl.BlockSpec((1, 1, nh * LANES, t), lambda bi, p, i: (bi, i, p, 0)),
        out_shape=jax.ShapeDtypeStruct((b, nb, w, t), BF16),
        scratch_shapes=_flash_scratch(2 * nh, LANES, t),
        compiler_params=_cparams(("parallel", "parallel", "arbitrary")),
        name="diff_attention",
    )(q, k, vt, lam_rows, subln_col)


def _moba_kernel(q_ref, qf_ref, k_ref, vt_ref, kmean_ref, ot_ref, sel_scr, *scr, t, npair):
    i = pl.program_id(2)
    nb = kmean_ref.shape[1]
    brow = lax.broadcasted_iota(jnp.int32, (nb, t), 0).astype(F32)
    past = brow < i.astype(F32)
    qs, ks, vr = [], [], []
    for pr in range(npair):
        cols = slice(LANES * pr, LANES * (pr + 1))
        qf, kmean = qf_ref[0, :, cols], kmean_ref[0, :, cols]
        qs += _qt_halves(q_ref[0, :, cols])
        for half in range(2):
            s = 2 * pr + half
            gate = lax.dot_general(kmean, _half_mask(qf, half), (((1,), (1,)), ((), ())),
                                   preferred_element_type=F32, precision=lax.Precision.HIGHEST)
            gate = jnp.where(past, gate, NEG_INF)
            sel = jnp.zeros((nb, t), F32)
            for _ in range(MOBA_TOPK):
                top = jnp.max(gate, axis=0, keepdims=True)
                first = jnp.min(jnp.where(gate == top, brow, float(nb)), axis=0, keepdims=True)
                pick = brow == first
                sel = jnp.where(pick, 1.0, sel)
                gate = jnp.where(pick, 2 * NEG_INF, gate)
            sel_scr[s] = jnp.where(past, sel, 0.0)
            ks.append(cols)
            vr.append(slice(HEAD_DIM * s, HEAD_DIM * (s + 1)))
    _flash_t(qs, k_ref, vt_ref, ks, vr, scr, i, t,
             past_mask=lambda s, jj: sel_scr[s, pl.ds(jj, 1), :] > 0.5)
    for s in range(2 * npair):
        ot_ref[0, 0, vr[s], :] = _normalised(scr[1], s, HEAD_DIM).astype(ot_ref.dtype)


def _moba_attention(q, qf, k, vt, kmean, npair):
    b, s, w = q.shape
    _, nb, _, t = vt.shape
    nkm = kmean.shape[1]
    lanes = npair * LANES
    return pl.pallas_call(
        functools.partial(_moba_kernel, t=t, npair=npair),
        grid=(b, w // lanes, nb),
        in_specs=[pl.BlockSpec((1, t, lanes), lambda bi, p, i: (bi, i, p)),
                  pl.BlockSpec((1, t, lanes), lambda bi, p, i: (bi, i, p)),
                  pl.BlockSpec((1, s, lanes), lambda bi, p, i: (bi, 0, p)),
                  pl.BlockSpec((1, nb, lanes, t), lambda bi, p, i: (bi, 0, p, 0)),
                  pl.BlockSpec((1, nkm, lanes), lambda bi, p, i: (bi, 0, p))],
        out_specs=pl.BlockSpec((1, 1, lanes, t), lambda bi, p, i: (bi, i, p, 0)),
        out_shape=jax.ShapeDtypeStruct((b, nb, w, t), BF16),
        scratch_shapes=[pltpu.VMEM((2 * npair, nkm, t), F32)] + _flash_scratch(2 * npair, HEAD_DIM, t),
        compiler_params=_cparams(("parallel", "parallel", "arbitrary")),
        name="moba_attention",
    )(q, qf, k, vt, kmean)


def _sw_kernel(q_ref, kc_ref, vc_ref, kp_ref, vp_ref, o_ref, l_ref, *, nbk):
    i = pl.program_id(1)
    blk = SW_BLOCK
    qi = lax.broadcasted_iota(jnp.int32, (blk, 2 * blk), 0)
    kj = lax.broadcasted_iota(jnp.int32, (blk, 2 * blk), 1)
    band = (kj >= qi) & (kj <= qi + blk)
    lane = _lane_iota((1, LANES))
    heads = [(pair, half) for pair in range(DIL_W // LANES) for half in range(2)]

    def scores(sq, n):
        rows = slice(blk * n, blk * (n + 1))
        q = q_ref[sq, rows, :]
        if n == 0:
            kprev, valid = kp_ref[sq], band & ((kj >= blk) | (i > 0))
        else:
            kprev, valid = kc_ref[sq, blk * (n - 1):blk * n, :], band
        kk = jnp.concatenate([kprev, kc_ref[sq, rows, :]], axis=0)
        out = []
        for pair, half in heads:
            sl = slice(LANES * pair, LANES * (pair + 1))
            out.append(jnp.where(valid, _nt_dot(_half_mask(q[:, sl], half), kk[:, sl]), NEG_INF))
        return out

    work = [(sq, n) for sq in range(q_ref.shape[0]) for n in range(nbk)]
    nxt = scores(*work[0])
    for w, (sq, n) in enumerate(work):
        cur, rows = nxt, slice(blk * n, blk * (n + 1))
        if w + 1 < len(work):
            nxt = scores(*work[w + 1])
        vprev = vp_ref[sq] if n == 0 else vc_ref[sq, blk * (n - 1):blk * n, :]
        vv = jnp.concatenate([vprev, vc_ref[sq, rows, :]], axis=0)
        outs, lses = [], []
        for (pair, half), sc in zip(heads, cur):
            m = jnp.max(sc, axis=-1, keepdims=True)
            p = jnp.exp2(sc - m)
            l = jnp.sum(p, axis=-1, keepdims=True)
            sl = slice(LANES * pair, LANES * (pair + 1))
            outs.append(jnp.dot(p.astype(BF16), vv[:, sl], preferred_element_type=F32) / l)
            lses.append(m + jnp.log2(l))
        for pair in range(DIL_W // LANES):
            sl = slice(LANES * pair, LANES * (pair + 1))
            o_ref[sq, rows, sl] = jnp.where(lane < HEAD_DIM, outs[2 * pair], outs[2 * pair + 1])
            l_ref[sq, rows, sl] = jnp.where(lane < HEAD_DIM, lses[2 * pair], lses[2 * pair + 1])


def _sliding_window(dil, nbk, name):
    lead, (m, w3) = dil.shape[:-2], dil.shape[-2:]
    seqs = dil.reshape((-1, m, w3))
    per_seq = min(nbk, m // SW_BLOCK)
    nsq = max(1, nbk // per_seq)
    nbk = per_seq
    rows = SW_BLOCK * nbk
    cur = lambda off: pl.BlockSpec((nsq, rows, DIL_W), lambda n, i: (n, i, off))
    prev = lambda off: pl.BlockSpec((nsq, SW_BLOCK, DIL_W), lambda n, i: (n, jnp.maximum(i * nbk - 1, 0), off))
    out_spec = pl.BlockSpec((nsq, rows, DIL_W), lambda n, i: (n, i, 0))
    o, l = pl.pallas_call(
        functools.partial(_sw_kernel, nbk=nbk),
        grid=(seqs.shape[0] // nsq, m // rows),
        in_specs=[cur(0), cur(1), cur(2), prev(1), prev(2)],
        out_specs=[out_spec, out_spec],
        out_shape=[jax.ShapeDtypeStruct((seqs.shape[0], m, DIL_W), F32)] * 2,
        compiler_params=_cparams(("parallel", "arbitrary")),
        name=name,
    )(seqs, seqs, seqs, seqs, seqs)
    return o.reshape(lead + (m, DIL_W)), l.reshape(lead + (m, DIL_W))


def _mlp_tail(y, mlp_refs):
    sh_ref, sc_ref, g_ref, nw_ref, w1_ref, w2_ref = mlp_refs
    h = (_rms(y, nw_ref[...]) * (1.0 + sc_ref[0]) + sh_ref[0]).astype(BF16)
    acc = jnp.zeros(y.shape, F32)
    for c in range(w1_ref.shape[1] // MLP_CHUNK):
        cols = slice(MLP_CHUNK * c, MLP_CHUNK * (c + 1))
        a = jnp.maximum(jnp.dot(h, w1_ref[:, cols], preferred_element_type=F32), 0.0)
        acc = acc + jnp.dot((a * a).astype(BF16), w2_ref[cols, :], preferred_element_type=F32)
    return y + g_ref[0] * acc


def _proj_t(ot_ref, w_rows):
    return jnp.concatenate([_tn_dot(ot_ref[0, j], w_rows) for j in range(ot_ref.shape[1])], axis=0)


def _mlp_specs(mlp_args, d):
    per_b = pl.BlockSpec((1, 1, d), lambda bi, i: (bi, 0, 0))
    const = lambda a: pl.BlockSpec(a.shape, lambda bi, i: (0, 0), pipeline_mode=pl.Buffered(1))
    sh, sc, g2, nw, w1, w2 = mlp_args
    return [per_b, per_b, per_b, pl.BlockSpec(nw.shape, lambda bi, i: (0, 0)), const(w1), const(w2)]


def _even_out_kernel(x_ref, g_ref, oat_ref, o0_ref, o1_ref, o2_ref, l0_ref, l1_ref, l2_ref, w_ref,
                     *rest):
    *mlp_refs, y_ref, tscr = rest
    tm = x_ref.shape[1]
    ncol = DIL_W // LANES

    def token_major(ref, slot):
        r = ref.shape[1]
        for c in range(r):
            for j in range(ncol):
                tscr[slot * ncol + j, pl.ds(c, tm // r, stride=r), :] = ref[0, c, :, LANES * j:LANES * (j + 1)]
        return jnp.concatenate([tscr[slot * ncol + j] for j in range(ncol)], axis=1)

    o0, l0 = o0_ref[0], l0_ref[0]
    o1, l1 = token_major(o1_ref, 0), token_major(l1_ref, 1)
    o2, l2 = token_major(o2_ref, 2), token_major(l2_ref, 3)
    top = jnp.maximum(jnp.maximum(l0, l1), l2)
    w0, w1, w2 = jnp.exp2(l0 - top), jnp.exp2(l1 - top), jnp.exp2(l2 - top)
    ob = (w0 * o0 + w1 * o1 + w2 * o2) / (w0 + w1 + w2)
    na = oat_ref.shape[2]
    y = _proj_t(oat_ref, w_ref[:na, :]) + jnp.dot(ob.astype(BF16), w_ref[na:, :], preferred_element_type=F32)
    y_ref[0] = _mlp_tail(x_ref[0] + g_ref[0] * y, mlp_refs)


def _t_spec(a, tm):
    return pl.BlockSpec((1, tm // a.shape[-1]) + a.shape[2:], lambda bi, i: (bi, i, 0, 0))


def _even_out(x, g1, oat, o_dil, l_dil, w_out, mlp_args, tm):
    b, s, d = x.shape
    tok = lambda w: pl.BlockSpec((1, tm, w), lambda bi, i: (bi, i, 0))

    def dil_spec(a):
        if a.ndim == 3:
            return tok(DIL_W)
        r = a.shape[1]
        return pl.BlockSpec((1, r, tm // r, DIL_W), lambda bi, i: (bi, 0, i, 0))

    return pl.pallas_call(
        _even_out_kernel,
        grid=(b, s // tm),
        in_specs=[tok(d), pl.BlockSpec((1, 1, d), lambda bi, i: (bi, 0, 0)), _t_spec(oat, tm)]
                 + [dil_spec(a) for a in (*o_dil, *l_dil)]
                 + [pl.BlockSpec(w_out.shape, lambda bi, i: (0, 0), pipeline_mode=pl.Buffered(1))]
                 + _mlp_specs(mlp_args, d),
        out_specs=tok(d),
        out_shape=jax.ShapeDtypeStruct((b, s, d), F32),
        scratch_shapes=[pltpu.VMEM((4 * DIL_W // LANES, tm, LANES), F32)],
        compiler_params=_cparams(("parallel", "arbitrary")),
        name="even_out_mlp",
    )(x, g1, oat, *o_dil, *l_dil, w_out, *mlp_args)


def _odd_out_kernel(x_ref, g_ref, oct_ref, odt_ref, w_ref, *rest):
    *mlp_refs, y_ref = rest
    nc = oct_ref.shape[2]
    y = _proj_t(oct_ref, w_ref[:nc, :]) + _proj_t(odt_ref, w_ref[nc:, :])
    y_ref[0] = _mlp_tail(x_ref[0] + g_ref[0] * y, mlp_refs)


def _odd_out(x, g1, oct, odt, w_out, mlp_args, tm):
    b, s, d = x.shape
    tok = lambda w: pl.BlockSpec((1, tm, w), lambda bi, i: (bi, i, 0))
    return pl.pallas_call(
        _odd_out_kernel,
        grid=(b, s // tm),
        in_specs=[tok(d), pl.BlockSpec((1, 1, d), lambda bi, i: (bi, 0, 0)), _t_spec(oct, tm), _t_spec(odt, tm),
                  pl.BlockSpec(w_out.shape, lambda bi, i: (0, 0), pipeline_mode=pl.Buffered(1))]
                 + _mlp_specs(mlp_args, d),
        out_specs=tok(d),
        out_shape=jax.ShapeDtypeStruct((b, s, d), F32),
        compiler_params=_cparams(("parallel", "arbitrary")),
        name="odd_out_mlp",
    )(x, g1, oct, odt, w_out, *mlp_args)


def _rope_tables(positions):
    pos = positions.astype(F32)[:, :, None]
    half = HEAD_DIM // 2
    inv = ROPE_THETA ** (-jnp.arange(half, dtype=F32) / half)
    ang = pos * inv[_freq_order()]
    c, s = jnp.cos(ang), jnp.sin(ang)

    def tiled(x, width):
        reps = LANES // width
        return jnp.broadcast_to(x[:, :, None, :], x.shape[:2] + (reps, width)).reshape(x.shape[:2] + (LANES,))

    lane = np.arange(LANES)
    sign = np.where(lane % HEAD_DIM < half, -1.0, 1.0).astype(np.float32)
    c64, s64 = tiled(c, half), tiled(s, half) * sign
    hm = MLA_ROPE // 2
    rope = (lane >= MLA_NOPE) & (lane < MLA_QK)
    sign_m = np.where(lane < MLA_NOPE + hm, -1.0, 1.0).astype(np.float32)
    cm = jnp.where(rope, tiled(c[..., :hm], hm), 1.0)
    sm = jnp.where(rope, tiled(s[..., :hm], hm) * sign_m, 0.0)
    return cm, sm, c64, s64


def _freq_order():
    half = HEAD_DIM // 2
    return np.concatenate([np.arange(0, half, 2), np.arange(1, half, 2)])


def _head_perm(n):
    within = np.concatenate([_freq_order(), HEAD_DIM // 2 + _freq_order()])
    idx = np.arange(n)
    return idx // HEAD_DIM * HEAD_DIM + within[idx % HEAD_DIM]


def _partner(n, width):
    idx = np.arange(n)
    return np.where(idx % width < width // 2, idx + width // 2, idx - width // 2)


def _even_weights(w_in, w_uq, w_ukv, qn, kn):
    w_in, w_uq, w_ukv = w_in.astype(BF16), w_uq.astype(BF16), w_ukv.astype(BF16)
    o2 = MLA_Q_RANK + MLA_KV_RANK
    o3 = o2 + MLA_ROPE
    d = w_in.shape[0]
    p_rope = _partner(MLA_ROPE, MLA_ROPE)
    zeros = lambda n: jnp.zeros((d, n), w_in.dtype)
    kr = w_in[:, o2:o3]
    tail = LANES - MLA_QK
    krp = jnp.concatenate([zeros(MLA_NOPE), kr, zeros(tail)], axis=1)
    krp_rot = jnp.concatenate([zeros(MLA_NOPE), kr[:, p_rope], zeros(tail)], axis=1)
    dil = w_in[:, o3:].reshape(d, 3, len(DIL_CONFIGS), DIL_W)
    dil = jnp.concatenate([dil[:, :2][..., _head_perm(DIL_W)], dil[:, 2:]], axis=1)
    dil_main = dil.transpose(0, 2, 1, 3).reshape(d, -1)
    dil_rot = dil[:, :2][..., _partner(DIL_W, HEAD_DIM)].transpose(0, 2, 1, 3).reshape(d, -1)
    win = jnp.concatenate([w_in[:, :o2], krp, krp_rot, dil_main, dil_rot], axis=1)

    padq = ((0, 0), (0, 0), (0, tail))
    wuq_rot = jnp.concatenate([w_uq[:, :, :MLA_NOPE], w_uq[:, :, MLA_NOPE:][:, :, p_rope]], axis=-1)
    wuq = jnp.concatenate([jnp.pad(w_uq, padq).reshape(MLA_Q_RANK, -1),
                           jnp.pad(wuq_rot, padq).reshape(MLA_Q_RANK, -1)], axis=1)
    wuk = jnp.pad(w_ukv[:, :, :MLA_NOPE], ((0, 0), (0, 0), (0, LANES - MLA_NOPE))).reshape(MLA_KV_RANK, -1)
    wvt = w_ukv[:, :, MLA_NOPE:].reshape(MLA_KV_RANK, -1).T

    def gains(g):
        rot = jnp.concatenate([g[:MLA_NOPE], g[MLA_NOPE:][p_rope]])
        return jnp.stack([jnp.pad(g, (0, tail)), jnp.pad(rot, (0, tail))]).astype(F32)

    return win.astype(BF16), wuq.astype(BF16), wuk.astype(BF16), wvt.astype(BF16), gains(qn), gains(kn)


def _odd_weights(w_in):
    w_in = w_in.astype(BF16)
    nqk = DIFF_HEADS * LANES
    nm = MOBA_HEADS * HEAD_DIM
    main = jnp.concatenate([w_in[:, :2 * nqk], w_in[:, 3 * nqk:3 * nqk + 2 * nm]], axis=1)
    main = main[:, _head_perm(main.shape[1])]
    win =jnp.concatenate([main, main[:, _partner(main.shape[1], HEAD_DIM)]], axis=1)
    wvt = jnp.concatenate([w_in[:, 2 * nqk:3 * nqk], w_in[:, 3 * nqk + 2 * nm:]], axis=1).T
    return win.astype(BF16), wvt.astype(BF16)


def _gains64(g):
    g = g[_head_perm(HEAD_DIM)]
    rot = g[_partner(HEAD_DIM, HEAD_DIM)]
    return jnp.stack([jnp.concatenate([g, g]), jnp.concatenate([rot, rot])]).astype(F32)


def kernel(x, c, positions, ada_w, ada_b, norm_mix, norm_mlp, mlp_w1, mlp_w2, even_w_in, even_w_out, mla_q_lat_norm, mla_kv_lat_norm, mla_w_uq, mla_w_ukv, mla_q_norm, mla_k_norm, dil_q_norm, dil_k_norm, odd_w_in, odd_w_out, diff_q_norm, diff_k_norm, diff_lambda, diff_subln, moba_q_norm, moba_k_norm):
    b, s, d = x.shape
    depth = ada_w.shape[0]
    t_attn, tm_in, tm_out = ATTN_BLOCK, TOKEN_TILE, TOKEN_TILE

    mod = _adaln(c, ada_w, ada_b)
    cm, sm, c64, s64 = _rope_tables(positions)

    for layer in range(depth):
        sh1, sc1, g1, sh2, sc2, g2 = [mod[layer, :, d * t:d * (t + 1)].reshape(b, 1, d) for t in range(6)]
        mlp_args = (sh2, sc2, g2, _row(norm_mlp[layer]), mlp_w1[layer].astype(BF16), mlp_w2[layer].astype(BF16))
        i = layer // 2
        if layer % 2 == 0:
            win, wuq, wuk, wvt, qn, kn = _even_weights(even_w_in[i], mla_w_uq[i], mla_w_ukv[i],
                                                       mla_q_norm[i], mla_k_norm[i])
            q, k, vt, *dils = _even_in(x, sh1, sc1, _row(norm_mix[layer]), win, wuq, wuk, wvt,
                                       _row(mla_q_lat_norm[i]), _row(mla_kv_lat_norm[i]), qn, kn,
                                       _gains64(dil_q_norm[i]), _gains64(dil_k_norm[i]), cm, sm, c64, s64, tm_in, t_attn)
            o_at = _mla_attention(q, k, vt, MLA_HEADS_PER_STEP)
            o_dil, l_dil = zip(*[_sliding_window(dg, SW_BLOCKS_PER_STEP, f"sliding_window_g{g}")
                                 for g, dg in enumerate(dils)])
            x = _even_out(x, g1, o_at, o_dil, l_dil, even_w_out[i].astype(BF16), mlp_args, tm_out)
        else:
            lam_init = 0.8 - 0.6 * math.exp(-0.3 * layer)
            win, wvt = _odd_weights(odd_w_in[i])
            qc, kc, qm, qmf, km, vct, vmt, kmean = _odd_in(
                x, sh1, sc1, _row(norm_mix[layer]), win, wvt,
                _gains64(diff_q_norm[i]), _gains64(diff_k_norm[i]), _gains64(moba_q_norm[i]),
                _gains64(moba_k_norm[i]),
                c64, s64, tm_in, t_attn)
            o_ct = _diff_attention(qc, kc, vct, diff_lambda[i].astype(F32),
                                   diff_subln[i].reshape(-1, 1).astype(F32), lam_init, DIFF_HEADS_PER_STEP)
            o_dt = _moba_attention(qm, qmf, km, vmt, kmean.reshape(b, s // MOBA_BLOCK, kmean.shape[-1]),
                                   MOBA_PAIRS_PER_STEP)
            x = _odd_out(x, g1, o_ct, o_dt, odd_w_out[i].astype(BF16), mlp_args, tm_out)
    return x
```

```python
import functools
import math

import jax
import jax.numpy as jnp
import numpy as np
from jax import lax
from jax.experimental import pallas as pl
from jax.experimental.pallas import tpu as pltpu

F32 = jnp.float32
BF16 = jnp.bfloat16

LANES = 128
HEAD_DIM = 64
ROPE_THETA = 10000.0
NORM_EPS = 1e-6
NEG_INF = -1e30
LOG2E = math.log2(math.e)

MLA_HEADS = 8
MLA_Q_RANK = 384
MLA_KV_RANK = 256
MLA_NOPE = 64
MLA_ROPE = 32
MLA_QK = MLA_NOPE + MLA_ROPE
DIL_CONFIGS = ((128, 1), (512, 4), (2048, 16))
DIL_HEADS = 4
DIL_W = DIL_HEADS * HEAD_DIM
DIFF_HEADS = 4
MOBA_HEADS = 8
MOBA_BLOCK = 256
MOBA_TOPK = 3
SW_BLOCK = 128
ONES_ROWS = 16

VMEM_LIMIT = 56 * 1024 * 1024

ATTN_BLOCK = MOBA_BLOCK
TOKEN_TILE = 2 * ATTN_BLOCK
MLP_CHUNK = 1024
ADALN_TILE = 2048
SW_BLOCKS_PER_STEP = 16
MLA_HEADS_PER_STEP = 4
DIFF_HEADS_PER_STEP = 2
MOBA_PAIRS_PER_STEP = 2


def _cparams(sem):
    return pltpu.CompilerParams(dimension_semantics=sem, vmem_limit_bytes=VMEM_LIMIT)


def _nt_dot(a, b):
    return lax.dot_general(a, b, (((1,), (1,)), ((), ())), preferred_element_type=F32)


def _tn_dot(a, b):
    return lax.dot_general(a, b, (((0,), (0,)), ((), ())), preferred_element_type=F32)


def _rms(x, w):
    return x * lax.rsqrt(jnp.mean(x * x, axis=-1, keepdims=True) + NORM_EPS) * w


def _lane_iota(shape):
    return lax.broadcasted_iota(jnp.int32, shape, len(shape) - 1)


def _adaln_kernel(c_ref, w_ref, b_ref, o_ref):
    c = c_ref[...]
    cond = c * (1.0 / (1.0 + jnp.exp(-c)))
    o_ref[0] = jnp.dot(cond, w_ref[0], preferred_element_type=F32,
                       precision=lax.Precision.HIGHEST) + b_ref[0]


def _adaln(c, ada_w, ada_b):
    depth, d, n = ada_w.shape
    b = c.shape[0]
    tn = ADALN_TILE
    return pl.pallas_call(
        _adaln_kernel,
        grid=(depth, n // tn),
        in_specs=[pl.BlockSpec((b, d), lambda l, j: (0, 0)),
                  pl.BlockSpec((1, d, tn), lambda l, j: (l, 0, j)),
                  pl.BlockSpec((1, 1, tn), lambda l, j: (l, 0, j))],
        out_specs=pl.BlockSpec((1, b, tn), lambda l, j: (l, 0, j)),
        out_shape=jax.ShapeDtypeStruct((depth, b, n), F32),
        compiler_params=_cparams(("parallel", "parallel")),
        name="adaln",
    )(c, ada_w, ada_b.reshape(depth, 1, n))


def _inv_rms64(x):
    lane = _lane_iota((1, LANES))
    lo = lane < HEAD_DIM
    sq = x * x
    s_lo = jnp.sum(jnp.where(lo, sq, 0.0), axis=-1, keepdims=True)
    s_hi = jnp.sum(jnp.where(lo, 0.0, sq), axis=-1, keepdims=True)
    return lax.rsqrt(jnp.where(lo, s_lo, s_hi) * (1.0 / HEAD_DIM) + NORM_EPS)


def _store_t_blocks(out_ref, xt):
    t = out_ref.shape[-1]
    for j in range(out_ref.shape[1]):
        out_ref[0, j] = xt[:, t * j:t * (j + 1)]


def _store_col_t(out_ref, col, x):
    t = out_ref.shape[-1]
    for j in range(out_ref.shape[1]):
        out_ref[0, j, LANES * col:LANES * (col + 1), :] = x[t * j:t * (j + 1)].T.astype(out_ref.dtype)


def _t_blocks(b, s, rows, tm, t):
    return (pl.BlockSpec((1, tm // t, rows, t), lambda bi, i: (bi, i, 0, 0)),
            jax.ShapeDtypeStruct((b, s // t, rows, t), BF16))


def _even_in_kernel(x_ref, sh_ref, sc_ref, nw_ref, win_ref, wuq_ref, wuk_ref, wvt_ref,
                    qlat_ref, kvlat_ref, qn_ref, kn_ref, dqn_ref, dkn_ref,
                    cm_ref, sm_ref, c64_ref, s64_ref,
                    qt_out, k_out, vt_out, d0_out, d1_out, d2_out, dscr):
    x = x_ref[0]
    tm = x.shape[0]
    h = _rms(x, nw_ref[...]) * (1.0 + sc_ref[0]) + sh_ref[0]
    u = jnp.dot(h.astype(BF16), win_ref[...], preferred_element_type=F32)

    o1 = MLA_Q_RANK
    o2 = o1 + MLA_KV_RANK
    o_kr, o_krr, o_dil = o2, o2 + LANES, o2 + 2 * LANES
    n_dil = 3 * DIL_W * len(DIL_CONFIGS)
    o_rot = o_dil + n_dil
    nq = MLA_HEADS * LANES
    cqn = _rms(u[:, :o1], qlat_ref[...]).astype(BF16)
    qp = jnp.dot(cqn, wuq_ref[...], preferred_element_type=F32)
    ckvn = _rms(u[:, o1:o2], kvlat_ref[...]).astype(BF16)
    kvp = jnp.dot(ckvn, wuk_ref[...], preferred_element_type=F32)
    _store_t_blocks(vt_out, _nt_dot(wvt_ref[...], ckvn).astype(BF16))
    kr, kr_rot = u[:, o_kr:o_kr + LANES], u[:, o_krr:o_krr + LANES]

    cm, sm = cm_ref[0], sm_ref[0]
    q_scale = MLA_QK ** -0.5 * LOG2E
    qgc, qgs = qn_ref[0:1] * cm * q_scale, qn_ref[1:2] * sm * q_scale
    kgc, kgs = kn_ref[0:1] * cm, kn_ref[1:2] * sm
    kr_term = kr_rot * kgs
    for hd in range(MLA_HEADS):
        sl = slice(LANES * hd, LANES * (hd + 1))
        qh = qp[:, sl]
        inv = lax.rsqrt(jnp.sum(qh * qh, -1, keepdims=True) * (1.0 / MLA_QK) + NORM_EPS)
        _store_col_t(qt_out, hd, inv * (qh * qgc + qp[:, nq + LANES * hd:nq + LANES * (hd + 1)] * qgs))
        kh = kvp[:, sl] + kr
        inv = lax.rsqrt(jnp.sum(kh * kh, -1, keepdims=True) * (1.0 / MLA_QK) + NORM_EPS)
        k_out[0, :, sl] = (inv * (kh * kgc + kr_term)).astype(BF16)

    c64, s64 = c64_ref[0], s64_ref[0]
    d_scale = HEAD_DIM ** -0.5 * LOG2E
    dqc, dqs = dqn_ref[0:1] * c64 * d_scale, dqn_ref[1:2] * s64 * d_scale
    dkc, dks = dkn_ref[0:1] * c64, dkn_ref[1:2] * s64
    ncol = DIL_W // LANES
    for g, d_out in enumerate((d0_out, d1_out, d2_out)):
        _, r = DIL_CONFIGS[g]
        base = o_dil + 3 * DIL_W * g
        rbase = o_rot + 2 * DIL_W * g
        for j in range(3 * ncol):
            xc = u[:, base + LANES * j:base + LANES * (j + 1)]
            if j < 2 * ncol:
                xr = u[:, rbase + LANES * j:rbase + LANES * (j + 1)]
                gc, gs = (dqc, dqs) if j < ncol else (dkc, dks)
                xc = _inv_rms64(xc) * (xc * gc + xr * gs)
            if r == 1:
                d_out[0, :, LANES * j:LANES * (j + 1)] = xc.astype(BF16)
            else:
                dscr[j] = xc
                for c in range(r):
                    d_out[0, c, :, LANES * j:LANES * (j + 1)] = dscr[j, pl.ds(c, tm // r, stride=r), :].astype(BF16)


def _row(v):
    return v.reshape(1, -1).astype(F32)


def _even_in(x, sh, sc, nw, win, wuq, wuk, wvt, qlat, kvlat, qn, kn, dqn, dkn, cm, sm, c64, s64, tm, t):
    b, s, d = x.shape
    tok = lambda w: pl.BlockSpec((1, tm, w), lambda bi, i: (bi, i, 0))
    per_b = pl.BlockSpec((1, 1, d), lambda bi, i: (bi, 0, 0))
    full = lambda a: pl.BlockSpec(a.shape, lambda bi, i: (0,) * a.ndim, pipeline_mode=pl.Buffered(1))
    vt_spec, vt_shape = _t_blocks(b, s, wvt.shape[0], tm, t)
    qt_spec, qt_shape = _t_blocks(b, s, MLA_HEADS * LANES, tm, t)
    dil_specs, dil_shapes = [], []
    for _, r in DIL_CONFIGS:
        if r == 1:
            dil_specs.append(tok(3 * DIL_W))
            dil_shapes.append(jax.ShapeDtypeStruct((b, s, 3 * DIL_W), BF16))
        else:
            dil_specs.append(pl.BlockSpec((1, r, tm // r, 3 * DIL_W), lambda bi, i: (bi, 0, i, 0)))
            dil_shapes.append(jax.ShapeDtypeStruct((b, r, s // r, 3 * DIL_W), BF16))
    return pl.pallas_call(
        _even_in_kernel,
        grid=(b, s // tm),
        in_specs=[tok(d), per_b, per_b, full(nw), full(win), full(wuq), full(wuk), full(wvt),
                  full(qlat), full(kvlat), full(qn), full(kn), full(dqn), full(dkn),
                  tok(LANES), tok(LANES), tok(LANES), tok(LANES)],
        out_specs=[qt_spec, tok(MLA_HEADS * LANES), vt_spec] + dil_specs,
        out_shape=[qt_shape, jax.ShapeDtypeStruct((b, s, MLA_HEADS * LANES), BF16), vt_shape] + dil_shapes,
        scratch_shapes=[pltpu.VMEM((3 * DIL_W // LANES, tm, LANES), F32)],
        compiler_params=_cparams(("parallel", "arbitrary")),
        name="even_in_proj",
    )(x, sh, sc, nw, win, wuq, wuk, wvt, qlat, kvlat, qn, kn, dqn, dkn, cm, sm, c64, s64)


def _odd_in_kernel(x_ref, sh_ref, sc_ref, nw_ref, win_ref, wvt_ref, dqn_ref, dkn_ref, mqn_ref, mkn_ref,
                   c64_ref, s64_ref,
                   qc_out, kc_out, qm_out, qmf_out, km_out, vct_out, vmt_out, kmean_out):
    x = x_ref[0]
    tm = x.shape[0]
    h = (_rms(x, nw_ref[...]) * (1.0 + sc_ref[0]) + sh_ref[0]).astype(BF16)
    u = jnp.dot(h, win_ref[...], preferred_element_type=F32)
    vt = _nt_dot(wvt_ref[...], h).astype(BF16)
    nvc = vct_out.shape[2]
    _store_t_blocks(vct_out, vt[:nvc])
    _store_t_blocks(vmt_out, vt[nvc:])
    c64, s64 = c64_ref[0], s64_ref[0]
    scale = HEAD_DIM ** -0.5 * LOG2E
    nqk = DIFF_HEADS * LANES
    nm = MOBA_HEADS * HEAD_DIM
    rot = 2 * nqk + 2 * nm

    def tables(gain_ref, mult):
        return gain_ref[0:1] * c64 * mult, gain_ref[1:2] * s64 * mult

    def roped(off, col, tab):
        a = off + LANES * col
        xc, xr = u[:, a:a + LANES], u[:, rot + a:rot + a + LANES]
        return _inv_rms64(xc) * (xc * tab[0] + xr * tab[1])

    t_qc, t_kc, t_qm, t_km = tables(dqn_ref, scale), tables(dkn_ref, 1.0), tables(mqn_ref, 1.0), tables(mkn_ref, 1.0)
    for col in range(nqk // LANES):
        sl = slice(LANES * col, LANES * (col + 1))
        qc_out[0, :, sl] = roped(0, col, t_qc).astype(BF16)
        kc_out[0, :, sl] = roped(nqk, col, t_kc).astype(BF16)

    base = 2 * nqk
    for col in range(nm // LANES):
        sl = slice(LANES * col, LANES * (col + 1))
        qm = roped(base, col, t_qm)
        qmf_out[0, :, sl] = qm
        qm_out[0, :, sl] = (qm * scale).astype(BF16)
        km = roped(base + nm, col, t_km)
        km_out[0, :, sl] = km.astype(BF16)
        for blk in range(tm // MOBA_BLOCK):
            rows = slice(MOBA_BLOCK * blk, MOBA_BLOCK * (blk + 1))
            kmean_out[0, 0, blk:blk + 1, sl] = jnp.mean(km[rows], axis=0, keepdims=True)


def _odd_in(x, sh, sc, nw, win, wvt, dqn, dkn, mqn, mkn, c64, s64, tm, t):
    b, s, d = x.shape
    tok = lambda w: pl.BlockSpec((1, tm, w), lambda bi, i: (bi, i, 0))
    per_b = pl.BlockSpec((1, 1, d), lambda bi, i: (bi, 0, 0))
    full = lambda a: pl.BlockSpec(a.shape, lambda bi, i: (0,) * a.ndim, pipeline_mode=pl.Buffered(1))
    w = DIFF_HEADS * LANES
    nblk = tm // MOBA_BLOCK
    shp = lambda dt: jax.ShapeDtypeStruct((b, s, w), dt)
    vt_spec, vt_shape = _t_blocks(b, s, wvt.shape[0] // 2, tm, t)
    return pl.pallas_call(
        _odd_in_kernel,
        grid=(b, s // tm),
        in_specs=[tok(d), per_b, per_b, full(nw), full(win), full(wvt), full(dqn), full(dkn), full(mqn), full(mkn),
                  tok(LANES), tok(LANES)],
        out_specs=[tok(w)] * 5 + [vt_spec, vt_spec, pl.BlockSpec((1, 1, nblk, w), lambda bi, i: (bi, i, 0, 0))],
        out_shape=[shp(BF16), shp(BF16), shp(BF16), shp(F32), shp(BF16), vt_shape, vt_shape,
                   jax.ShapeDtypeStruct((b, s // tm, nblk, w), F32)],
        compiler_params=_cparams(("parallel", "arbitrary")),
        name="odd_in_proj",
    )(x, sh, sc, nw, win, wvt, dqn, dkn, mqn, mkn, c64, s64)


def _qt_halves(q):
    qt = q.astype(F32).T
    row = lax.broadcasted_iota(jnp.int32, qt.shape, 0)
    return [jnp.where((row >= HEAD_DIM) == bool(half), qt, 0.0).astype(BF16) for half in range(2)]


def _flash_t(qts, k_ref, vt_ref, k_slices, v_rows, scr, i, t, past_mask=None):
    m_scr, acc_scr, s_scr, p_scr, a_scr, b_scr = scr
    n = len(qts)
    ones = jnp.ones((ONES_ROWS, t), BF16)

    def qk(blk, s):
        b0 = pl.multiple_of(blk * t, t)
        return jnp.dot(k_ref[0, pl.ds(b0, t), k_slices[s]], qts[s], preferred_element_type=F32)

    def pv(blk, s, p):
        vt = jnp.concatenate([vt_ref[0, blk, v_rows[s], :], ones], axis=0)
        return jnp.dot(vt, p, preferred_element_type=F32)

    def store_scores(slot, s, st):
        s_scr[slot, s] = st
        b_scr[slot, s] = jnp.max(st, axis=0, keepdims=True)

    krow = lax.broadcasted_iota(jnp.int32, (t, t), 0)
    qcol = lax.broadcasted_iota(jnp.int32, (t, t), 1)
    causal = krow <= qcol
    last_past = jnp.maximum(i - 1, 0)
    for s in range(n):
        st = jnp.where(causal, qk(i, s), NEG_INF)
        m = jnp.max(st, axis=0, keepdims=True)
        m_scr[s] = m
        p_scr[0, s] = jnp.exp2(st - m).astype(BF16)
        p_scr[1, s] = jnp.zeros((t, t), BF16)
        a_scr[0, s] = jnp.ones_like(m)
        a_scr[1, s] = jnp.ones_like(m)
        acc_scr[s] = jnp.zeros(acc_scr.shape[1:], F32)
        store_scores(0, s, qk(0, s))
        store_scores(1, s, qk(jnp.minimum(1, last_past), s))

    def softmax_stage(slot, valid, rs, ws, blk):
        for s in range(n):
            cmax = b_scr[2 * rs + slot, s]
            keep = None if past_mask is None else past_mask(s, blk)
            if keep is not None:
                cmax = jnp.where(keep, cmax, NEG_INF)
            if valid is not None:
                cmax = jnp.where(valid, cmax, NEG_INF)
            m_prev = m_scr[s]
            m_new = jnp.maximum(m_prev, cmax)
            a_scr[2 * ws + slot, s] = jnp.exp2(m_prev - m_new)
            p = jnp.exp2(s_scr[2 * rs + slot, s] - m_new).astype(BF16)
            if keep is not None:
                p = jnp.where(keep, p, jnp.zeros_like(p))
            p_scr[2 * ws + slot, s] = p
            m_scr[s] = m_new

    def body(kk, rs, ws):
        first = kk == 0
        pa_blk = jnp.where(first, i, 2 * kk - 2)
        pb_blk = jnp.where(first, i, 2 * kk - 1)
        p_prev = [[p_scr[2 * rs + sl, s] for s in range(n)] for sl in range(2)]
        a_prev = [[a_scr[2 * rs + sl, s] for s in range(n)] for sl in range(2)]
        s_next = [[qk(jnp.minimum(2 * kk + 2 + sl, last_past), s) for s in range(n)] for sl in range(2)]
        pvs = [[pv(blk, s, p_prev[sl][s]) for s in range(n)] for sl, blk in enumerate((pa_blk, pb_blk))]
        softmax_stage(0, None, rs, ws, 2 * kk)
        softmax_stage(1, 2 * kk + 1 < i, rs, ws, jnp.minimum(2 * kk + 1, last_past))
        for sl in range(2):
            for s in range(n):
                store_scores(2 * ws + sl, s, s_next[sl][s])
        for s in range(n):
            acc_scr[s] = a_prev[1][s] * (a_prev[0][s] * acc_scr[s] + pvs[0][s]) + pvs[1][s]

    def two_bodies(kp, carry):
        body(2 * kp, 0, 1)
        body(2 * kp + 1, 1, 0)
        return carry

    nbody = (i + 1) // 2
    lax.fori_loop(0, nbody // 2, two_bodies, 0)

    @pl.when(nbody % 2 == 1)
    def _():
        body(nbody - 1, 0, 1)

    fs = nbody % 2
    kl = jnp.maximum(nbody - 1, 0)
    none = i == 0
    fa_blk = jnp.where(none, i, 2 * kl)
    fb_blk = jnp.where(none, i, jnp.minimum(2 * kl + 1, last_past))
    fb_scale = jnp.where(jnp.logical_or(none, i % 2 == 0), 1.0, 0.0)
    for s in range(n):
        acc_scr[s] = (a_scr[2 * fs + 1, s] * (a_scr[2 * fs, s] * acc_scr[s] + pv(fa_blk, s, p_scr[2 * fs, s]))
                      + fb_scale * pv(fb_blk, s, p_scr[2 * fs + 1, s]))


def _flash_scratch(n, nv, t):
    return [pltpu.VMEM((n, 1, t), F32), pltpu.VMEM((n, nv + ONES_ROWS, t), F32), pltpu.VMEM((4, n, t, t), F32),
            pltpu.VMEM((4, n, t, t), BF16), pltpu.VMEM((4, n, 1, t), F32), pltpu.VMEM((4, n, 1, t), F32)]


def _normalised(acc_scr, s, nv):
    return acc_scr[s, :nv, :] / acc_scr[s, nv:nv + 1, :]


def _half_mask(x, half):
    lane = _lane_iota((1, LANES))
    return jnp.where((lane >= HEAD_DIM) == bool(half), x, jnp.zeros_like(x))


def _mla_kernel(qt_ref, k_ref, vt_ref, ot_ref, *scr, t, nh):
    i = pl.program_id(2)
    ks = [slice(LANES * s, LANES * (s + 1)) for s in range(nh)]
    qs = [qt_ref[0, 0, ks[s], :] for s in range(nh)]
    vr = [slice(HEAD_DIM * s, HEAD_DIM * (s + 1)) for s in range(nh)]
    _flash_t(qs, k_ref, vt_ref, ks, vr, scr, i, t)
    for s in range(nh):
        ot_ref[0, 0, vr[s], :] = _normalised(scr[1], s, HEAD_DIM).astype(ot_ref.dtype)


def _mla_attention(qt, k, vt, nh):
    b, s, _ = k.shape
    _, nb, nv, t = vt.shape
    return pl.pallas_call(
        functools.partial(_mla_kernel, t=t, nh=nh),
        grid=(b, MLA_HEADS // nh, nb),
        in_specs=[pl.BlockSpec((1, 1, nh * LANES, t), lambda bi, p, i: (bi, i, p, 0)),
                  pl.BlockSpec((1, s, nh * LANES), lambda bi, p, i: (bi, 0, p)),
                  pl.BlockSpec((1, nb, nh * HEAD_DIM, t), lambda bi, p, i: (bi, 0, p, 0))],
        out_specs=pl.BlockSpec((1, 1, nh * HEAD_DIM, t), lambda bi, p, i: (bi, i, p, 0)),
        out_shape=jax.ShapeDtypeStruct((b, nb, nv, t), BF16),
        scratch_shapes=_flash_scratch(nh, HEAD_DIM, t),
        compiler_params=_cparams(("parallel", "parallel", "arbitrary")),
        name="mla_attention",
    )(qt, k, vt)


def _diff_kernel(q_ref, k_ref, vt_ref, lam_ref, sub_ref, ot_ref, *scr, t, nh, lam_init):
    i = pl.program_id(2)
    qs, ks, vr = [], [], []
    for hd in range(nh):
        cols = slice(LANES * hd, LANES * (hd + 1))
        qs += _qt_halves(q_ref[0, :, cols])
        ks += [cols, cols]
        vr += [cols, cols]
    _flash_t(qs, k_ref, vt_ref, ks, vr, scr, i, t)
    lv = lam_ref[...]
    lam = (jnp.exp(jnp.sum(lv[0:1] * lv[1:2], keepdims=True))
           - jnp.exp(jnp.sum(lv[2:3] * lv[3:4], keepdims=True)) + lam_init)
    for hd in range(nh):
        o = _normalised(scr[1], 2 * hd, LANES) - lam * _normalised(scr[1], 2 * hd + 1, LANES)
        o = o * lax.rsqrt(jnp.mean(o * o, axis=0, keepdims=True) + NORM_EPS) * sub_ref[...]
        ot_ref[0, 0, LANES * hd:LANES * (hd + 1), :] = (o * (1.0 - lam_init)).astype(ot_ref.dtype)


def _diff_attention(q, k, vt, lam_rows, subln_col, lam_init, nh):
    b, s, w = q.shape
    _, nb, _, t = vt.shape
    return pl.pallas_call(
        functools.partial(_diff_kernel, t=t, nh=nh, lam_init=lam_init),
        grid=(b, DIFF_HEADS // nh, nb),
        in_specs=[pl.BlockSpec((1, t, nh * LANES), lambda bi, p, i: (bi, i, p)),
                  pl.BlockSpec((1, s, nh * LANES), lambda bi, p, i: (bi, 0, p)),
                  pl.BlockSpec((1, nb, nh * LANES, t), lambda bi, p, i: (bi, 0, p, 0)),
                  pl.BlockSpec(lam_rows.shape, lambda bi, p, i: (0, 0)),
                  pl.BlockSpec(subln_col.shape, lambda bi, p, i: (0, 0))],
        out_specs=pl.BlockSpec((1, 1, nh * LANES, t), lambda bi, p, i: (bi, i, p, 0)),
        out_shape=jax.ShapeDtypeStruct((b, nb, w, t), BF16),
        scratch_shapes=_flash_scratch(2 * nh, LANES, t),
        compiler_params=_cparams(("parallel", "parallel", "arbitrary")),
        name="diff_attention",
    )(q, k, vt, lam_rows, subln_col)


def _moba_kernel(q_ref, qf_ref, k_ref, vt_ref, kmean_ref, ot_ref, sel_scr, *scr, t, npair):
    i = pl.program_id(2)
    nb = kmean_ref.shape[1]
    brow = lax.broadcasted_iota(jnp.int32, (nb, t), 0).astype(F32)
    past = brow < i.astype(F32)
    qs, ks, vr = [], [], []
    for pr in range(npair):
        cols = slice(LANES * pr, LANES * (pr + 1))
        qf, kmean = qf_ref[0, :, cols], kmean_ref[0, :, cols]
        qs += _qt_halves(q_ref[0, :, cols])
        for half in range(2):
            s = 2 * pr + half
            gate = lax.dot_general(kmean, _half_mask(qf, half), (((1,), (1,)), ((), ())),
                                   preferred_element_type=F32, precision=lax.Precision.HIGHEST)
            gate = jnp.where(past, gate, NEG_INF)
            sel = jnp.zeros((nb, t), F32)
            for _ in range(MOBA_TOPK):
                top = jnp.max(gate, axis=0, keepdims=True)
                first = jnp.min(jnp.where(gate == top, brow, float(nb)), axis=0, keepdims=True)
                pick = brow == first
                sel = jnp.where(pick, 1.0, sel)
                gate = jnp.where(pick, 2 * NEG_INF, gate)
            sel_scr[s] = jnp.where(past, sel, 0.0)
            ks.append(cols)
            vr.append(slice(HEAD_DIM * s, HEAD_DIM * (s + 1)))
    _flash_t(qs, k_ref, vt_ref, ks, vr, scr, i, t,
             past_mask=lambda s, jj: sel_scr[s, pl.ds(jj, 1), :] > 0.5)
    for s in range(2 * npair):
        ot_ref[0, 0, vr[s], :] = _normalised(scr[1], s, HEAD_DIM).astype(ot_ref.dtype)


def _moba_attention(q, qf, k, vt, kmean, npair):
    b, s, w = q.shape
    _, nb, _, t = vt.shape
    nkm = kmean.shape[1]
    lanes = npair * LANES
    return pl.pallas_call(
        functools.partial(_moba_kernel, t=t, npair=npair),
        grid=(b, w // lanes, nb),
        in_specs=[pl.BlockSpec((1, t, lanes), lambda bi, p, i: (bi, i, p)),
                  pl.BlockSpec((1, t, lanes), lambda bi, p, i: (bi, i, p)),
                  pl.BlockSpec((1, s, lanes), lambda bi, p, i: (bi, 0, p)),
                  pl.BlockSpec((1, nb, lanes, t), lambda bi, p, i: (bi, 0, p, 0)),
                  pl.BlockSpec((1, nkm, lanes), lambda bi, p, i: (bi, 0, p))],
        out_specs=pl.BlockSpec((1, 1, lanes, t), lambda bi, p, i: (bi, i, p, 0)),
        out_shape=jax.ShapeDtypeStruct((b, nb, w, t), BF16),
        scratch_shapes=[pltpu.VMEM((2 * npair, nkm, t), F32)] + _flash_scratch(2 * npair, HEAD_DIM, t),
        compiler_params=_cparams(("parallel", "parallel", "arbitrary")),
        name="moba_attention",
    )(q, qf, k, vt, kmean)


def _sw_kernel(q_ref, kc_ref, vc_ref, kp_ref, vp_ref, o_ref, l_ref, *, nbk):
    i = pl.program_id(1)
    blk = SW_BLOCK
    qi = lax.broadcasted_iota(jnp.int32, (blk, 2 * blk), 0)
    kj = lax.broadcasted_iota(jnp.int32, (blk, 2 * blk), 1)
    band = (kj >= qi) & (kj <= qi + blk)
    lane = _lane_iota((1, LANES))
    heads = [(pair, half) for pair in range(DIL_W // LANES) for half in range(2)]

    def scores(sq, n):
        rows = slice(blk * n, blk * (n + 1))
        q = q_ref[sq, rows, :]
        if n == 0:
            kprev, valid = kp_ref[sq], band & ((kj >= blk) | (i > 0))
        else:
            kprev, valid = kc_ref[sq, blk * (n - 1):blk * n, :], band
        kk = jnp.concatenate([kprev, kc_ref[sq, rows, :]], axis=0)
        out = []
        for pair, half in heads:
            sl = slice(LANES * pair, LANES * (pair + 1))
            out.append(jnp.where(valid, _nt_dot(_half_mask(q[:, sl], half), kk[:, sl]), NEG_INF))
        return out

    work = [(sq, n) for sq in range(q_ref.shape[0]) for n in range(nbk)]
    nxt = scores(*work[0])
    for w, (sq, n) in enumerate(work):
        cur, rows = nxt, slice(blk * n, blk * (n + 1))
        if w + 1 < len(work):
            nxt = scores(*work[w + 1])
        vprev = vp_ref[sq] if n == 0 else vc_ref[sq, blk * (n - 1):blk * n, :]
        vv = jnp.concatenate([vprev, vc_ref[sq, rows, :]], axis=0)
        outs, lses = [], []
        for (pair, half), sc in zip(heads, cur):
            m = jnp.max(sc, axis=-1, keepdims=True)
            p = jnp.exp2(sc - m)
            l = jnp.sum(p, axis=-1, keepdims=True)
            sl = slice(LANES * pair, LANES * (pair + 1))
            outs.append(jnp.dot(p.astype(BF16), vv[:, sl], preferred_element_type=F32) / l)
            lses.append(m + jnp.log2(l))
        for pair in range(DIL_W // LANES):
            sl = slice(LANES * pair, LANES * (pair + 1))
            o_ref[sq, rows, sl] = jnp.where(lane < HEAD_DIM, outs[2 * pair], outs[2 * pair + 1])
            l_ref[sq, rows, sl] = jnp.where(lane < HEAD_DIM, lses[2 * pair], lses[2 * pair + 1])


def _sliding_window(dil, nbk, name):
    lead, (m, w3) = dil.shape[:-2], dil.shape[-2:]
    seqs = dil.reshape((-1, m, w3))
    per_seq = min(nbk, m // SW_BLOCK)
    nsq = max(1, nbk // per_seq)
    nbk = per_seq
    rows = SW_BLOCK * nbk
    cur = lambda off: pl.BlockSpec((nsq, rows, DIL_W), lambda n, i: (n, i, off))
    prev = lambda off: pl.BlockSpec((nsq, SW_BLOCK, DIL_W), lambda n, i: (n, jnp.maximum(i * nbk - 1, 0), off))
    out_spec = pl.BlockSpec((nsq, rows, DIL_W), lambda n, i: (n, i, 0))
    o, l = pl.pallas_call(
        functools.partial(_sw_kernel, nbk=nbk),
        grid=(seqs.shape[0] // nsq, m // rows),
        in_specs=[cur(0), cur(1), cur(2), prev(1), prev(2)],
        out_specs=[out_spec, out_spec],
        out_shape=[jax.ShapeDtypeStruct((seqs.shape[0], m, DIL_W), F32)] * 2,
        compiler_params=_cparams(("parallel", "arbitrary")),
        name=name,
    )(seqs, seqs, seqs, seqs, seqs)
    return o.reshape(lead + (m, DIL_W)), l.reshape(lead + (m, DIL_W))


def _mlp_tail(y, mlp_refs):
    sh_ref, sc_ref, g_ref, nw_ref, w1_ref, w2_ref = mlp_refs
    h = (_rms(y, nw_ref[...]) * (1.0 + sc_ref[0]) + sh_ref[0]).astype(BF16)
    acc = jnp.zeros(y.shape, F32)
    for c in range(w1_ref.shape[1] // MLP_CHUNK):
        cols = slice(MLP_CHUNK * c, MLP_CHUNK * (c + 1))
        a = jnp.maximum(jnp.dot(h, w1_ref[:, cols], preferred_element_type=F32), 0.0)
        acc = acc + jnp.dot((a * a).astype(BF16), w2_ref[cols, :], preferred_element_type=F32)
    return y + g_ref[0] * acc


def _proj_t(ot_ref, w_rows):
    return jnp.concatenate([_tn_dot(ot_ref[0, j], w_rows) for j in range(ot_ref.shape[1])], axis=0)


def _mlp_specs(mlp_args, d):
    per_b = pl.BlockSpec((1, 1, d), lambda bi, i: (bi, 0, 0))
    const = lambda a: pl.BlockSpec(a.shape, lambda bi, i: (0, 0), pipeline_mode=pl.Buffered(1))
    sh, sc, g2, nw, w1, w2 = mlp_args
    return [per_b, per_b, per_b, pl.BlockSpec(nw.shape, lambda bi, i: (0, 0)), const(w1), const(w2)]


def _even_out_kernel(x_ref, g_ref, oat_ref, o0_ref, o1_ref, o2_ref, l0_ref, l1_ref, l2_ref, w_ref,
                     *rest):
    *mlp_refs, y_ref, tscr = rest
    tm = x_ref.shape[1]
    ncol = DIL_W // LANES

    def token_major(ref, slot):
        r = ref.shape[1]
        for c in range(r):
            for j in range(ncol):
                tscr[slot * ncol + j, pl.ds(c, tm // r, stride=r), :] = ref[0, c, :, LANES * j:LANES * (j + 1)]
        return jnp.concatenate([tscr[slot * ncol + j] for j in range(ncol)], axis=1)

    o0, l0 = o0_ref[0], l0_ref[0]
    o1, l1 = token_major(o1_ref, 0), token_major(l1_ref, 1)
    o2, l2 = token_major(o2_ref, 2), token_major(l2_ref, 3)
    top = jnp.maximum(jnp.maximum(l0, l1), l2)
    w0, w1, w2 = jnp.exp2(l0 - top), jnp.exp2(l1 - top), jnp.exp2(l2 - top)
    ob = (w0 * o0 + w1 * o1 + w2 * o2) / (w0 + w1 + w2)
    na = oat_ref.shape[2]
    y = _proj_t(oat_ref, w_ref[:na, :]) + jnp.dot(ob.astype(BF16), w_ref[na:, :], preferred_element_type=F32)
    y_ref[0] = _mlp_tail(x_ref[0] + g_ref[0] * y, mlp_refs)


def _t_spec(a, tm):
    return pl.BlockSpec((1, tm // a.shape[-1]) + a.shape[2:], lambda bi, i: (bi, i, 0, 0))


def _even_out(x, g1, oat, o_dil, l_dil, w_out, mlp_args, tm):
    b, s, d = x.shape
    tok = lambda w: pl.BlockSpec((1, tm, w), lambda bi, i: (bi, i, 0))

    def dil_spec(a):
        if a.ndim == 3:
            return tok(DIL_W)
        r = a.shape[1]
        return pl.BlockSpec((1, r, tm // r, DIL_W), lambda bi, i: (bi, 0, i, 0))

    return pl.pallas_call(
        _even_out_kernel,
        grid=(b, s // tm),
        in_specs=[tok(d), pl.BlockSpec((1, 1, d), lambda bi, i: (bi, 0, 0)), _t_spec(oat, tm)]
                 + [dil_spec(a) for a in (*o_dil, *l_dil)]
                 + [pl.BlockSpec(w_out.shape, lambda bi, i: (0, 0), pipeline_mode=pl.Buffered(1))]
                 + _mlp_specs(mlp_args, d),
        out_specs=tok(d),
        out_shape=jax.ShapeDtypeStruct((b, s, d), F32),
        scratch_shapes=[pltpu.VMEM((4 * DIL_W // LANES, tm, LANES), F32)],
        compiler_params=_cparams(("parallel", "arbitrary")),
        name="even_out_mlp",
    )(x, g1, oat, *o_dil, *l_dil, w_out, *mlp_args)


def _odd_out_kernel(x_ref, g_ref, oct_ref, odt_ref, w_ref, *rest):
    *mlp_refs, y_ref = rest
    nc = oct_ref.shape[2]
    y = _proj_t(oct_ref, w_ref[:nc, :]) + _proj_t(odt_ref, w_ref[nc:, :])
    y_ref[0] = _mlp_tail(x_ref[0] + g_ref[0] * y, mlp_refs)


def _odd_out(x, g1, oct, odt, w_out, mlp_args, tm):
    b, s, d = x.shape
    tok = lambda w: pl.BlockSpec((1, tm, w), lambda bi, i: (bi, i, 0))
    return pl.pallas_call(
        _odd_out_kernel,
        grid=(b, s // tm),
        in_specs=[tok(d), pl.BlockSpec((1, 1, d), lambda bi, i: (bi, 0, 0)), _t_spec(oct, tm), _t_spec(odt, tm),
                  pl.BlockSpec(w_out.shape, lambda bi, i: (0, 0), pipeline_mode=pl.Buffered(1))]
                 + _mlp_specs(mlp_args, d),
        out_specs=tok(d),
        out_shape=jax.ShapeDtypeStruct((b, s, d), F32),
        compiler_params=_cparams(("parallel", "arbitrary")),
        name="odd_out_mlp",
    )(x, g1, oct, odt, w_out, *mlp_args)


def _rope_tables(positions):
    pos = positions.astype(F32)[:, :, None]
    half = HEAD_DIM // 2
    inv = ROPE_THETA ** (-jnp.arange(half, dtype=F32) / half)
    ang = pos * inv[_freq_order()]
    c, s = jnp.cos(ang), jnp.sin(ang)

    def tiled(x, width):
        reps = LANES // width
        return jnp.broadcast_to(x[:, :, None, :], x.shape[:2] + (reps, width)).reshape(x.shape[:2] + (LANES,))

    lane = np.arange(LANES)
    sign = np.where(lane % HEAD_DIM < half, -1.0, 1.0).astype(np.float32)
    c64, s64 = tiled(c, half), tiled(s, half) * sign
    hm = MLA_ROPE // 2
    rope = (lane >= MLA_NOPE) & (lane < MLA_QK)
    sign_m = np.where(lane < MLA_NOPE + hm, -1.0, 1.0).astype(np.float32)
    cm = jnp.where(rope, tiled(c[..., :hm], hm), 1.0)
    sm = jnp.where(rope, tiled(s[..., :hm], hm) * sign_m, 0.0)
    return cm, sm, c64, s64


def _freq_order():
    half = HEAD_DIM // 2
    return np.concatenate([np.arange(0, half, 2), np.arange(1, half, 2)])


def _head_perm(n):
    within = np.concatenate([_freq_order(), HEAD_DIM // 2 + _freq_order()])
    idx = np.arange(n)
    return idx // HEAD_DIM * HEAD_DIM + within[idx % HEAD_DIM]


def _partner(n, width):
    idx = np.arange(n)
    return np.where(idx % width < width // 2, idx + width // 2, idx - width // 2)


def _even_weights(w_in, w_uq, w_ukv, qn, kn):
    w_in, w_uq, w_ukv = w_in.astype(BF16), w_uq.astype(BF16), w_ukv.astype(BF16)
    o2 = MLA_Q_RANK + MLA_KV_RANK
    o3 = o2 + MLA_ROPE
    d = w_in.shape[0]
    p_rope = _partner(MLA_ROPE, MLA_ROPE)
    zeros = lambda n: jnp.zeros((d, n), w_in.dtype)
    kr = w_in[:, o2:o3]
    tail = LANES - MLA_QK
    krp = jnp.concatenate([zeros(MLA_NOPE), kr, zeros(tail)], axis=1)
    krp_rot = jnp.concatenate([zeros(MLA_NOPE), kr[:, p_rope], zeros(tail)], axis=1)
    dil = w_in[:, o3:].reshape(d, 3, len(DIL_CONFIGS), DIL_W)
    dil = jnp.concatenate([dil[:, :2][..., _head_perm(DIL_W)], dil[:, 2:]], axis=1)
    dil_main = dil.transpose(0, 2, 1, 3).reshape(d, -1)
    dil_rot = dil[:, :2][..., _partner(DIL_W, HEAD_DIM)].transpose(0, 2, 1, 3).reshape(d, -1)
    win = jnp.concatenate([w_in[:, :o2], krp, krp_rot, dil_main, dil_rot], axis=1)

    padq = ((0, 0), (0, 0), (0, tail))
    wuq_rot = jnp.concatenate([w_uq[:, :, :MLA_NOPE], w_uq[:, :, MLA_NOPE:][:, :, p_rope]], axis=-1)
    wuq = jnp.concatenate([jnp.pad(w_uq, padq).reshape(MLA_Q_RANK, -1),
                           jnp.pad(wuq_rot, padq).reshape(MLA_Q_RANK, -1)], axis=1)
    wuk = jnp.pad(w_ukv[:, :, :MLA_NOPE], ((0, 0), (0, 0), (0, LANES - MLA_NOPE))).reshape(MLA_KV_RANK, -1)
    wvt = w_ukv[:, :, MLA_NOPE:].reshape(MLA_KV_RANK, -1).T

    def gains(g):
        rot = jnp.concatenate([g[:MLA_NOPE], g[MLA_NOPE:][p_rope]])
        return jnp.stack([jnp.pad(g, (0, tail)), jnp.pad(rot, (0, tail))]).astype(F32)

    return win.astype(BF16), wuq.astype(BF16), wuk.astype(BF16), wvt.astype(BF16), gains(qn), gains(kn)


def _odd_weights(w_in):
    w_in = w_in.astype(BF16)
    nqk = DIFF_HEADS * LANES
    nm = MOBA_HEADS * HEAD_DIM
    main = jnp.concatenate([w_in[:, :2 * nqk], w_in[:, 3 * nqk:3 * nqk + 2 * nm]], axis=1)
    main = main[:, _head_perm(main.shape[1])]
    win =jnp.concatenate([main, main[:, _partner(main.shape[1], HEAD_DIM)]], axis=1)
    wvt = jnp.concatenate([w_in[:, 2 * nqk:3 * nqk], w_in[:, 3 * nqk + 2 * nm:]], axis=1).T
    return win.astype(BF16), wvt.astype(BF16)


def _gains64(g):
    g = g[_head_perm(HEAD_DIM)]
    rot = g[_partner(HEAD_DIM, HEAD_DIM)]
    return jnp.stack([jnp.concatenate([g, g]), jnp.concatenate([rot, rot])]).astype(F32)


def kernel(x, c, positions, ada_w, ada_b, norm_mix, norm_mlp, mlp_w1, mlp_w2, even_w_in, even_w_out, mla_q_lat_norm, mla_kv_lat_norm, mla_w_uq, mla_w_ukv, mla_q_norm, mla_k_norm, dil_q_norm, dil_k_norm, odd_w_in, odd_w_out, diff_q_norm, diff_k_norm, diff_lambda, diff_subln, moba_q_norm, moba_k_norm):
    b, s, d = x.shape
    depth = ada_w.shape[0]
    t_attn, tm_in, tm_out = ATTN_BLOCK, TOKEN_TILE, TOKEN_TILE

    mod = _adaln(c, ada_w, ada_b)
    cm, sm, c64, s64 = _rope_tables(positions)

    for layer in range(depth):
        sh1, sc1, g1, sh2, sc2, g2 = [mod[layer, :, d * t:d * (t + 1)].reshape(b, 1, d) for t in range(6)]
        mlp_args = (sh2, sc2, g2, _row(norm_mlp[layer]), mlp_w1[layer].astype(BF16), mlp_w2[layer].astype(BF16))
        i = layer // 2
        if layer % 2 == 0:
            win, wuq, wuk, wvt, qn, kn = _even_weights(even_w_in[i], mla_w_uq[i], mla_w_ukv[i],
                                                       mla_q_norm[i], mla_k_norm[i])
            q, k, vt, *dils = _even_in(x, sh1, sc1, _row(norm_mix[layer]), win, wuq, wuk, wvt,
                                       _row(mla_q_lat_norm[i]), _row(mla_kv_lat_norm[i]), qn, kn,
                                       _gains64(dil_q_norm[i]), _gains64(dil_k_norm[i]), cm, sm, c64, s64, tm_in, t_attn)
            o_at = _mla_attention(q, k, vt, MLA_HEADS_PER_STEP)
            o_dil, l_dil = zip(*[_sliding_window(dg, SW_BLOCKS_PER_STEP, f"sliding_window_g{g}")
                                 for g, dg in enumerate(dils)])
            x = _even_out(x, g1, o_at, o_dil, l_dil, even_w_out[i].astype(BF16), mlp_args, tm_out)
        else:
            lam_init = 0.8 - 0.6 * math.exp(-0.3 * layer)
            win, wvt = _odd_weights(odd_w_in[i])
            qc, kc, qm, qmf, km, vct, vmt, kmean = _odd_in(
                x, sh1, sc1, _row(norm_mix[layer]), win, wvt,
                _gains64(diff_q_norm[i]), _gains64(diff_k_norm[i]), _gains64(moba_q_norm[i]),
                _gains64(moba_k_norm[i]),
                c64, s64, tm_in, t_attn)
            o_ct = _diff_attention(qc, kc, vct, diff_lambda[i].astype(F32),
                                   diff_subln[i].reshape(-1, 1).astype(F32), lam_init, DIFF_HEADS_PER_STEP)
            o_dt = _moba_attention(qm, qmf, km, vmt, kmean.reshape(b, s // MOBA_BLOCK, kmean.shape[-1]),
                                   MOBA_PAIRS_PER_STEP)
            x = _odd_out(x, g1, o_ct, o_dt, odd_w_out[i].astype(BF16), mlp_args, tm_out)
    return x
```

```python
import functools
import math

import jax
import jax.numpy as jnp
import numpy as np
from jax import lax
from jax.experimental import pallas as pl
from jax.experimental.pallas import tpu as pltpu

F32 = jnp.float32
BF16 = jnp.bfloat16

LANES = 128
HEAD_DIM = 64
ROPE_THETA = 10000.0
NORM_EPS = 1e-6
NEG_INF = -1e30
LOG2E = math.log2(math.e)

MLA_HEADS = 8
MLA_Q_RANK = 384
MLA_KV_RANK = 256
MLA_NOPE = 64
MLA_ROPE = 32
MLA_QK = MLA_NOPE + MLA_ROPE
DIL_CONFIGS = ((128, 1), (512, 4), (2048, 16))
DIL_HEADS = 4
DIL_W = DIL_HEADS * HEAD_DIM
DIFF_HEADS = 4
MOBA_HEADS = 8
MOBA_BLOCK = 256
MOBA_TOPK = 3
SW_BLOCK = 128
ONES_ROWS = 16

VMEM_LIMIT = 56 * 1024 * 1024

ATTN_BLOCK = MOBA_BLOCK
TOKEN_TILE = 2 * ATTN_BLOCK
MLP_CHUNK = 1024
ADALN_TILE = 2048
SW_BLOCKS_PER_STEP = 16
MLA_HEADS_PER_STEP = 4
DIFF_HEADS_PER_STEP = 2
MOBA_PAIRS_PER_STEP = 2


def _cparams(sem):
    return pltpu.CompilerParams(dimension_semantics=sem, vmem_limit_bytes=VMEM_LIMIT)


def _nt_dot(a, b):
    return lax.dot_general(a, b, (((1,), (1,)), ((), ())), preferred_element_type=F32)


def _tn_dot(a, b):
    return lax.dot_general(a, b, (((0,), (0,)), ((), ())), preferred_element_type=F32)


def _rms(x, w):
    return x * lax.rsqrt(jnp.mean(x * x, axis=-1, keepdims=True) + NORM_EPS) * w


def _lane_iota(shape):
    return lax.broadcasted_iota(jnp.int32, shape, len(shape) - 1)


def _adaln_kernel(c_ref, w_ref, b_ref, o_ref):
    c = c_ref[...]
    cond = c * (1.0 / (1.0 + jnp.exp(-c)))
    o_ref[0] = jnp.dot(cond, w_ref[0], preferred_element_type=F32,
                       precision=lax.Precision.HIGHEST) + b_ref[0]


def _adaln(c, ada_w, ada_b):
    depth, d, n = ada_w.shape
    b = c.shape[0]
    tn = ADALN_TILE
    return pl.pallas_call(
        _adaln_kernel,
        grid=(depth, n // tn),
        in_specs=[pl.BlockSpec((b, d), lambda l, j: (0, 0)),
                  pl.BlockSpec((1, d, tn), lambda l, j: (l, 0, j)),
                  pl.BlockSpec((1, 1, tn), lambda l, j: (l, 0, j))],
        out_specs=pl.BlockSpec((1, b, tn), lambda l, j: (l, 0, j)),
        out_shape=jax.ShapeDtypeStruct((depth, b, n), F32),
        compiler_params=_cparams(("parallel", "parallel")),
        name="adaln",
    )(c, ada_w, ada_b.reshape(depth, 1, n))


def _inv_rms64(x):
    lane = _lane_iota((1, LANES))
    lo = lane < HEAD_DIM
    sq = x * x
    s_lo = jnp.sum(jnp.where(lo, sq, 0.0), axis=-1, keepdims=True)
    s_hi = jnp.sum(jnp.where(lo, 0.0, sq), axis=-1, keepdims=True)
    return lax.rsqrt(jnp.where(lo, s_lo, s_hi) * (1.0 / HEAD_DIM) + NORM_EPS)


def _store_t_blocks(out_ref, xt):
    t = out_ref.shape[-1]
    for j in range(out_ref.shape[1]):
        out_ref[0, j] = xt[:, t * j:t * (j + 1)]


def _store_col_t(out_ref, col, x):
    t = out_ref.shape[-1]
    for j in range(out_ref.shape[1]):
        out_ref[0, j, LANES * col:LANES * (col + 1), :] = x[t * j:t * (j + 1)].T.astype(out_ref.dtype)


def _t_blocks(b, s, rows, tm, t):
    return (pl.BlockSpec((1, tm // t, rows, t), lambda bi, i: (bi, i, 0, 0)),
            jax.ShapeDtypeStruct((b, s // t, rows, t), BF16))


def _even_in_kernel(x_ref, sh_ref, sc_ref, nw_ref, win_ref, wuq_ref, wuk_ref, wvt_ref,
                    qlat_ref, kvlat_ref, qn_ref, kn_ref, dqn_ref, dkn_ref,
                    cm_ref, sm_ref, c64_ref, s64_ref,
                    qt_out, k_out, vt_out, d0_out, d1_out, d2_out, dscr):
    x = x_ref[0]
    tm = x.shape[0]
    h = _rms(x, nw_ref[...]) * (1.0 + sc_ref[0]) + sh_ref[0]
    u = jnp.dot(h.astype(BF16), win_ref[...], preferred_element_type=F32)

    o1 = MLA_Q_RANK
    o2 = o1 + MLA_KV_RANK
    o_kr, o_krr, o_dil = o2, o2 + LANES, o2 + 2 * LANES
    n_dil = 3 * DIL_W * len(DIL_CONFIGS)
    o_rot = o_dil + n_dil
    nq = MLA_HEADS * LANES
    cqn = _rms(u[:, :o1], qlat_ref[...]).astype(BF16)
    qp = jnp.dot(cqn, wuq_ref[...], preferred_element_type=F32)
    ckvn = _rms(u[:, o1:o2], kvlat_ref[...]).astype(BF16)
    kvp = jnp.dot(ckvn, wuk_ref[...], preferred_element_type=F32)
    _store_t_blocks(vt_out, _nt_dot(wvt_ref[...], ckvn).astype(BF16))
    kr, kr_rot = u[:, o_kr:o_kr + LANES], u[:, o_krr:o_krr + LANES]

    cm, sm = cm_ref[0], sm_ref[0]
    q_scale = MLA_QK ** -0.5 * LOG2E
    qgc, qgs = qn_ref[0:1] * cm * q_scale, qn_ref[1:2] * sm * q_scale
    kgc, kgs = kn_ref[0:1] * cm, kn_ref[1:2] * sm
    kr_term = kr_rot * kgs
    for hd in range(MLA_HEADS):
        sl = slice(LANES * hd, LANES * (hd + 1))
        qh = qp[:, sl]
        inv = lax.rsqrt(jnp.sum(qh * qh, -1, keepdims=True) * (1.0 / MLA_QK) + NORM_EPS)
        _store_col_t(qt_out, hd, inv * (qh * qgc + qp[:, nq + LANES * hd:nq + LANES * (hd + 1)] * qgs))
        kh = kvp[:, sl] + kr
        inv = lax.rsqrt(jnp.sum(kh * kh, -1, keepdims=True) * (1.0 / MLA_QK) + NORM_EPS)
        k_out[0, :, sl] = (inv * (kh * kgc + kr_term)).astype(BF16)

    c64, s64 = c64_ref[0], s64_ref[0]
    d_scale = HEAD_DIM ** -0.5 * LOG2E
    dqc, dqs = dqn_ref[0:1] * c64 * d_scale, dqn_ref[1:2] * s64 * d_scale
    dkc, dks = dkn_ref[0:1] * c64, dkn_ref[1:2] * s64
    ncol = DIL_W // LANES
    for g, d_out in enumerate((d0_out, d1_out, d2_out)):
        _, r = DIL_CONFIGS[g]
        base = o_dil + 3 * DIL_W * g
        rbase = o_rot + 2 * DIL_W * g
        for j in range(3 * ncol):
            xc = u[:, base + LANES * j:base + LANES * (j + 1)]
            if j < 2 * ncol:
                xr = u[:, rbase + LANES * j:rbase + LANES * (j + 1)]
                gc, gs = (dqc, dqs) if j < ncol else (dkc, dks)
                xc = _inv_rms64(xc) * (xc * gc + xr * gs)
            if r == 1:
                d_out[0, :, LANES * j:LANES * (j + 1)] = xc.astype(BF16)
            else:
                dscr[j] = xc
                for c in range(r):
                    d_out[0, c, :, LANES * j:LANES * (j + 1)] = dscr[j, pl.ds(c, tm // r, stride=r), :].astype(BF16)


def _row(v):
    return v.reshape(1, -1).astype(F32)


def _even_in(x, sh, sc, nw, win, wuq, wuk, wvt, qlat, kvlat, qn, kn, dqn, dkn, cm, sm, c64, s64, tm, t):
    b, s, d = x.shape
    tok = lambda w: pl.BlockSpec((1, tm, w), lambda bi, i: (bi, i, 0))
    per_b = pl.BlockSpec((1, 1, d), lambda bi, i: (bi, 0, 0))
    full = lambda a: pl.BlockSpec(a.shape, lambda bi, i: (0,) * a.ndim, pipeline_mode=pl.Buffered(1))
    vt_spec, vt_shape = _t_blocks(b, s, wvt.shape[0], tm, t)
    qt_spec, qt_shape = _t_blocks(b, s, MLA_HEADS * LANES, tm, t)
    dil_specs, dil_shapes = [], []
    for _, r in DIL_CONFIGS:
        if r == 1:
            dil_specs.append(tok(3 * DIL_W))
            dil_shapes.append(jax.ShapeDtypeStruct((b, s, 3 * DIL_W), BF16))
        else:
            dil_specs.append(pl.BlockSpec((1, r, tm // r, 3 * DIL_W), lambda bi, i: (bi, 0, i, 0)))
            dil_shapes.append(jax.ShapeDtypeStruct((b, r, s // r, 3 * DIL_W), BF16))
    return pl.pallas_call(
        _even_in_kernel,
        grid=(b, s // tm),
        in_specs=[tok(d), per_b, per_b, full(nw), full(win), full(wuq), full(wuk), full(wvt),
                  full(qlat), full(kvlat), full(qn), full(kn), full(dqn), full(dkn),
                  tok(LANES), tok(LANES), tok(LANES), tok(LANES)],
        out_specs=[qt_spec, tok(MLA_HEADS * LANES), vt_spec] + dil_specs,
        out_shape=[qt_shape, jax.ShapeDtypeStruct((b, s, MLA_HEADS * LANES), BF16), vt_shape] + dil_shapes,
        scratch_shapes=[pltpu.VMEM((3 * DIL_W // LANES, tm, LANES), F32)],
        compiler_params=_cparams(("parallel", "arbitrary")),
        name="even_in_proj",
    )(x, sh, sc, nw, win, wuq, wuk, wvt, qlat, kvlat, qn, kn, dqn, dkn, cm, sm, c64, s64)


def _odd_in_kernel(x_ref, sh_ref, sc_ref, nw_ref, win_ref, wvt_ref, dqn_ref, dkn_ref, mqn_ref, mkn_ref,
                   c64_ref, s64_ref,
                   qc_out, kc_out, qm_out, qmf_out, km_out, vct_out, vmt_out, kmean_out):
    x = x_ref[0]
    tm = x.shape[0]
    h = (_rms(x, nw_ref[...]) * (1.0 + sc_ref[0]) + sh_ref[0]).astype(BF16)
    u = jnp.dot(h, win_ref[...], preferred_element_type=F32)
    vt = _nt_dot(wvt_ref[...], h).astype(BF16)
    nvc = vct_out.shape[2]
    _store_t_blocks(vct_out, vt[:nvc])
    _store_t_blocks(vmt_out, vt[nvc:])
    c64, s64 = c64_ref[0], s64_ref[0]
    scale = HEAD_DIM ** -0.5 * LOG2E
    nqk = DIFF_HEADS * LANES
    nm = MOBA_HEADS * HEAD_DIM
    rot = 2 * nqk + 2 * nm

    def tables(gain_ref, mult):
        return gain_ref[0:1] * c64 * mult, gain_ref[1:2] * s64 * mult

    def roped(off, col, tab):
        a = off + LANES * col
        xc, xr = u[:, a:a + LANES], u[:, rot + a:rot + a + LANES]
        return _inv_rms64(xc) * (xc * tab[0] + xr * tab[1])

    t_qc, t_kc, t_qm, t_km = tables(dqn_ref, scale), tables(dkn_ref, 1.0), tables(mqn_ref, 1.0), tables(mkn_ref, 1.0)
    for col in range(nqk // LANES):
        sl = slice(LANES * col, LANES * (col + 1))
        qc_out[0, :, sl] = roped(0, col, t_qc).astype(BF16)
        kc_out[0, :, sl] = roped(nqk, col, t_kc).astype(BF16)

    base = 2 * nqk
    for col in range(nm // LANES):
        sl = slice(LANES * col, LANES * (col + 1))
        qm = roped(base, col, t_qm)
        qmf_out[0, :, sl] = qm
        qm_out[0, :, sl] = (qm * scale).astype(BF16)
        km = roped(base + nm, col, t_km)
        km_out[0, :, sl] = km.astype(BF16)
        for blk in range(tm // MOBA_BLOCK):
            rows = slice(MOBA_BLOCK * blk, MOBA_BLOCK * (blk + 1))
            kmean_out[0, 0, blk:blk + 1, sl] = jnp.mean(km[rows], axis=0, keepdims=True)


def _odd_in(x, sh, sc, nw, win, wvt, dqn, dkn, mqn, mkn, c64, s64, tm, t):
    b, s, d = x.shape
    tok = lambda w: pl.BlockSpec((1, tm, w), lambda bi, i: (bi, i, 0))
    per_b = pl.BlockSpec((1, 1, d), lambda bi, i: (bi, 0, 0))
    full = lambda a: pl.BlockSpec(a.shape, lambda bi, i: (0,) * a.ndim, pipeline_mode=pl.Buffered(1))
    w = DIFF_HEADS * LANES
    nblk = tm // MOBA_BLOCK
    shp = lambda dt: jax.ShapeDtypeStruct((b, s, w), dt)
    vt_spec, vt_shape = _t_blocks(b, s, wvt.shape[0] // 2, tm, t)
    return pl.pallas_call(
        _odd_in_kernel,
        grid=(b, s // tm),
        in_specs=[tok(d), per_b, per_b, full(nw), full(win), full(wvt), full(dqn), full(dkn), full(mqn), full(mkn),
                  tok(LANES), tok(LANES)],
        out_specs=[tok(w)] * 5 + [vt_spec, vt_spec, pl.BlockSpec((1, 1, nblk, w), lambda bi, i: (bi, i, 0, 0))],
        out_shape=[shp(BF16), shp(BF16), shp(BF16), shp(F32), shp(BF16), vt_shape, vt_shape,
                   jax.ShapeDtypeStruct((b, s // tm, nblk, w), F32)],
        compiler_params=_cparams(("parallel", "arbitrary")),
        name="odd_in_proj",
    )(x, sh, sc, nw, win, wvt, dqn, dkn, mqn, mkn, c64, s64)


def _qt_halves(q):
    qt = q.astype(F32).T
    row = lax.broadcasted_iota(jnp.int32, qt.shape, 0)
    return [jnp.where((row >= HEAD_DIM) == bool(half), qt, 0.0).astype(BF16) for half in range(2)]


def _flash_t(qts, k_ref, vt_ref, k_slices, v_rows, scr, i, t, past_mask=None):
    m_scr, acc_scr, s_scr, p_scr, a_scr, b_scr = scr
    n = len(qts)
    ones = jnp.ones((ONES_ROWS, t), BF16)

    def qk(blk, s):
        b0 = pl.multiple_of(blk * t, t)
        return jnp.dot(k_ref[0, pl.ds(b0, t), k_slices[s]], qts[s], preferred_element_type=F32)

    def pv(blk, s, p):
        vt = jnp.concatenate([vt_ref[0, blk, v_rows[s], :], ones], axis=0)
        return jnp.dot(vt, p, preferred_element_type=F32)

    def store_scores(slot, s, st):
        s_scr[slot, s] = st
        b_scr[slot, s] = jnp.max(st, axis=0, keepdims=True)

    krow = lax.broadcasted_iota(jnp.int32, (t, t), 0)
    qcol = lax.broadcasted_iota(jnp.int32, (t, t), 1)
    causal = krow <= qcol
    last_past = jnp.maximum(i - 1, 0)
    for s in range(n):
        st = jnp.where(causal, qk(i, s), NEG_INF)
        m = jnp.max(st, axis=0, keepdims=True)
        m_scr[s] = m
        p_scr[0, s] = jnp.exp2(st - m).astype(BF16)
        p_scr[1, s] = jnp.zeros((t, t), BF16)
        a_scr[0, s] = jnp.ones_like(m)
        a_scr[1, s] = jnp.ones_like(m)
        acc_scr[s] = jnp.zeros(acc_scr.shape[1:], F32)
        store_scores(0, s, qk(0, s))
        store_scores(1, s, qk(jnp.minimum(1, last_past), s))

    def softmax_stage(slot, valid, rs, ws, blk):
        for s in range(n):
            cmax = b_scr[2 * rs + slot, s]
            keep = None if past_mask is None else past_mask(s, blk)
            if keep is not None:
                cmax = jnp.where(keep, cmax, NEG_INF)
            if valid is not None:
                cmax = jnp.where(valid, cmax, NEG_INF)
            m_prev = m_scr[s]
            m_new = jnp.maximum(m_prev, cmax)
            a_scr[2 * ws + slot, s] = jnp.exp2(m_prev - m_new)
            p = jnp.exp2(s_scr[2 * rs + slot, s] - m_new).astype(BF16)
            if keep is not None:
                p = jnp.where(keep, p, jnp.zeros_like(p))
            p_scr[2 * ws + slot, s] = p
            m_scr[s] = m_new

    def body(kk, rs, ws):
        first = kk == 0
        pa_blk = jnp.where(first, i, 2 * kk - 2)
        pb_blk = jnp.where(first, i, 2 * kk - 1)
        p_prev = [[p_scr[2 * rs + sl, s] for s in range(n)] for sl in range(2)]
        a_prev = [[a_scr[2 * rs + sl, s] for s in range(n)] for sl in range(2)]
        s_next = [[qk(jnp.minimum(2 * kk + 2 + sl, last_past), s) for s in range(n)] for sl in range(2)]
        pvs = [[pv(blk, s, p_prev[sl][s]) for s in range(n)] for sl, blk in enumerate((pa_blk, pb_blk))]
        softmax_stage(0, None, rs, ws, 2 * kk)
        softmax_stage(1, 2 * kk + 1 < i, rs, ws, jnp.minimum(2 * kk + 1, last_past))
        for sl in range(2):
            for s in range(n):
                store_scores(2 * ws + sl, s, s_next[sl][s])
        for s in range(n):
            acc_scr[s] = a_prev[1][s] * (a_prev[0][s] * acc_scr[s] + pvs[0][s]) + pvs[1][s]

    def two_bodies(kp, carry):
        body(2 * kp, 0, 1)
        body(2 * kp + 1, 1, 0)
        return carry

    nbody = (i + 1) // 2
    lax.fori_loop(0, nbody // 2, two_bodies, 0)

    @pl.when(nbody % 2 == 1)
    def _():
        body(nbody - 1, 0, 1)

    fs = nbody % 2
    kl = jnp.maximum(nbody - 1, 0)
    none = i == 0
    fa_blk = jnp.where(none, i, 2 * kl)
    fb_blk = jnp.where(none, i, jnp.minimum(2 * kl + 1, last_past))
    fb_scale = jnp.where(jnp.logical_or(none, i % 2 == 0), 1.0, 0.0)
    for s in range(n):
        acc_scr[s] = (a_scr[2 * fs + 1, s] * (a_scr[2 * fs, s] * acc_scr[s] + pv(fa_blk, s, p_scr[2 * fs, s]))
                      + fb_scale * pv(fb_blk, s, p_scr[2 * fs + 1, s]))


def _flash_scratch(n, nv, t):
    return [pltpu.VMEM((n, 1, t), F32), pltpu.VMEM((n, nv + ONES_ROWS, t), F32), pltpu.VMEM((4, n, t, t), F32),
            pltpu.VMEM((4, n, t, t), BF16), pltpu.VMEM((4, n, 1, t), F32), pltpu.VMEM((4, n, 1, t), F32)]


def _normalised(acc_scr, s, nv):
    return acc_scr[s, :nv, :] / acc_scr[s, nv:nv + 1, :]


def _half_mask(x, half):
    lane = _lane_iota((1, LANES))
    return jnp.where((lane >= HEAD_DIM) == bool(half), x, jnp.zeros_like(x))


def _mla_kernel(qt_ref, k_ref, vt_ref, ot_ref, *scr, t, nh):
    i = pl.program_id(2)
    ks = [slice(LANES * s, LANES * (s + 1)) for s in range(nh)]
    qs = [qt_ref[0, 0, ks[s], :] for s in range(nh)]
    vr = [slice(HEAD_DIM * s, HEAD_DIM * (s + 1)) for s in range(nh)]
    _flash_t(qs, k_ref, vt_ref, ks, vr, scr, i, t)
    for s in range(nh):
        ot_ref[0, 0, vr[s], :] = _normalised(scr[1], s, HEAD_DIM).astype(ot_ref.dtype)


def _mla_attention(qt, k, vt, nh):
    b, s, _ = k.shape
    _, nb, nv, t = vt.shape
    return pl.pallas_call(
        functools.partial(_mla_kernel, t=t, nh=nh),
        grid=(b, MLA_HEADS // nh, nb),
        in_specs=[pl.BlockSpec((1, 1, nh * LANES, t), lambda bi, p, i: (bi, i, p, 0)),
                  pl.BlockSpec((1, s, nh * LANES), lambda bi, p, i: (bi, 0, p)),
                  pl.BlockSpec((1, nb, nh * HEAD_DIM, t), lambda bi, p, i: (bi, 0, p, 0))],
        out_specs=pl.BlockSpec((1, 1, nh * HEAD_DIM, t), lambda bi, p, i: (bi, i, p, 0)),
        out_shape=jax.ShapeDtypeStruct((b, nb, nv, t), BF16),
        scratch_shapes=_flash_scratch(nh, HEAD_DIM, t),
        compiler_params=_cparams(("parallel", "parallel", "arbitrary")),
        name="mla_attention",
    )(qt, k, vt)


def _diff_kernel(q_ref, k_ref, vt_ref, lam_ref, sub_ref, ot_ref, *scr, t, nh, lam_init):
    i = pl.program_id(2)
    qs, ks, vr = [], [], []
    for hd in range(nh):
        cols = slice(LANES * hd, LANES * (hd + 1))
        qs += _qt_halves(q_ref[0, :, cols])
        ks += [cols, cols]
        vr += [cols, cols]
    _flash_t(qs, k_ref, vt_ref, ks, vr, scr, i, t)
    lv = lam_ref[...]
    lam = (jnp.exp(jnp.sum(lv[0:1] * lv[1:2], keepdims=True))
           - jnp.exp(jnp.sum(lv[2:3] * lv[3:4], keepdims=True)) + lam_init)
    for hd in range(nh):
        o = _normalised(scr[1], 2 * hd, LANES) - lam * _normalised(scr[1], 2 * hd + 1, LANES)
        o = o * lax.rsqrt(jnp.mean(o * o, axis=0, keepdims=True) + NORM_EPS) * sub_ref[...]
        ot_ref[0, 0, LANES * hd:LANES * (hd + 1), :] = (o * (1.0 - lam_init)).astype(ot_ref.dtype)


def _diff_attention(q, k, vt, lam_rows, subln_col, lam_init, nh):
    b, s, w = q.shape
    _, nb, _, t = vt.shape
    return pl.pallas_call(
        functools.partial(_diff_kernel, t=t, nh=nh, lam_init=lam_init),
        grid=(b, DIFF_HEADS // nh, nb),
        in_specs=[pl.BlockSpec((1, t, nh * LANES), lambda bi, p, i: (bi, i, p)),
                  pl.BlockSpec((1, s, nh * LANES), lambda bi, p, i: (bi, 0, p)),
                  pl.BlockSpec((1, nb, nh * LANES, t), lambda bi, p, i: (bi, 0, p, 0)),
                  pl.BlockSpec(lam_rows.shape, lambda bi, p, i: (0, 0)),
                  pl.BlockSpec(subln_col.shape, lambda bi, p, i: (0, 0))],
        out_specs=pl.BlockSpec((1, 1, nh * LANES, t), lambda bi, p, i: (bi, i, p, 0)),
        out_shape=jax.ShapeDtypeStruct((b, nb, w, t), BF16),
        scratch_shapes=_flash_scratch(2 * nh, LANES, t),
        compiler_params=_cparams(("parallel", "parallel", "arbitrary")),
        name="diff_attention",
    )(q, k, vt, lam_rows, subln_col)


def _moba_kernel(q_ref, qf_ref, k_ref, vt_ref, kmean_ref, ot_ref, sel_scr, *scr, t, npair):
    i = pl.program_id(2)
    nb = kmean_ref.shape[1]
    brow = lax.broadcasted_iota(jnp.int32, (nb, t), 0).astype(F32)
    past = brow < i.astype(F32)
    qs, ks, vr = [], [], []
    for pr in range(npair):
        cols = slice(LANES * pr, LANES * (pr + 1))
        qf, kmean = qf_ref[0, :, cols], kmean_ref[0, :, cols]
        qs += _qt_halves(q_ref[0, :, cols])
        for half in range(2):
            s = 2 * pr + half
            gate = lax.dot_general(kmean, _half_mask(qf, half), (((1,), (1,)), ((), ())),
                                   preferred_element_type=F32, precision=lax.Precision.HIGHEST)
            gate = jnp.where(past, gate, NEG_INF)
            sel = jnp.zeros((nb, t), F32)
            for _ in range(MOBA_TOPK):
                top = jnp.max(gate, axis=0, keepdims=True)
                first = jnp.min(jnp.where(gate == top, brow, float(nb)), axis=0, keepdims=True)
                pick = brow == first
                sel = jnp.where(pick, 1.0, sel)
                gate = jnp.where(pick, 2 * NEG_INF, gate)
            sel_scr[s] = jnp.where(past, sel, 0.0)
            ks.append(cols)
            vr.append(slice(HEAD_DIM * s, HEAD_DIM * (s + 1)))
    _flash_t(qs, k_ref, vt_ref, ks, vr, scr, i, t,
             past_mask=lambda s, jj: sel_scr[s, pl.ds(jj, 1), :] > 0.5)
    for s in range(2 * npair):
        ot_ref[0, 0, vr[s], :] = _normalised(scr[1], s, HEAD_DIM).astype(ot_ref.dtype)


def _moba_attention(q, qf, k, vt, kmean, npair):
    b, s, w = q.shape
    _, nb, _, t = vt.shape
    nkm = kmean.shape[1]
    lanes = npair * LANES
    return pl.pallas_call(
        functools.partial(_moba_kernel, t=t, npair=npair),
        grid=(b, w // lanes, nb),
        in_specs=[pl.BlockSpec((1, t, lanes), lambda bi, p, i: (bi, i, p)),
                  pl.BlockSpec((1, t, lanes), lambda bi, p, i: (bi, i, p)),
                  pl.BlockSpec((1, s, lanes), lambda bi, p, i: (bi, 0, p)),
                  pl.BlockSpec((1, nb, lanes, t), lambda bi, p, i: (bi, 0, p, 0)),
                  pl.BlockSpec((1, nkm, lanes), lambda bi, p, i: (bi, 0, p))],
        out_specs=pl.BlockSpec((1, 1, lanes, t), lambda bi, p, i: (bi, i, p, 0)),
        out_shape=jax.ShapeDtypeStruct((b, nb, w, t), BF16),
        scratch_shapes=[pltpu.VMEM((2 * npair, nkm, t), F32)] + _flash_scratch(2 * npair, HEAD_DIM, t),
        compiler_params=_cparams(("parallel", "parallel", "arbitrary")),
        name="moba_attention",
    )(q, qf, k, vt, kmean)


def _sw_kernel(q_ref, kc_ref, vc_ref, kp_ref, vp_ref, o_ref, l_ref, *, nbk):
    i = pl.program_id(1)
    blk = SW_BLOCK
    qi = lax.broadcasted_iota(jnp.int32, (blk, 2 * blk), 0)
    kj = lax.broadcasted_iota(jnp.int32, (blk, 2 * blk), 1)
    band = (kj >= qi) & (kj <= qi + blk)
    lane = _lane_iota((1, LANES))
    heads = [(pair, half) for pair in range(DIL_W // LANES) for half in range(2)]

    def scores(sq, n):
        rows = slice(blk * n, blk * (n + 1))
        q = q_ref[sq, rows, :]
        if n == 0:
            kprev, valid = kp_ref[sq], band & ((kj >= blk) | (i > 0))
        else:
            kprev, valid = kc_ref[sq, blk * (n - 1):blk * n, :], band
        kk = jnp.concatenate([kprev, kc_ref[sq, rows, :]], axis=0)
        out = []
        for pair, half in heads:
            sl = slice(LANES * pair, LANES * (pair + 1))
            out.append(jnp.where(valid, _nt_dot(_half_mask(q[:, sl], half), kk[:, sl]), NEG_INF))
        return out

    work = [(sq, n) for sq in range(q_ref.shape[0]) for n in range(nbk)]
    nxt = scores(*work[0])
    for w, (sq, n) in enumerate(work):
        cur, rows = nxt, slice(blk * n, blk * (n + 1))
        if w + 1 < len(work):
            nxt = scores(*work[w + 1])
        vprev = vp_ref[sq] if n == 0 else vc_ref[sq, blk * (n - 1):blk * n, :]
        vv = jnp.concatenate([vprev, vc_ref[sq, rows, :]], axis=0)
        outs, lses = [], []
        for (pair, half), sc in zip(heads, cur):
            m = jnp.max(sc, axis=-1, keepdims=True)
            p = jnp.exp2(sc - m)
            l = jnp.sum(p, axis=-1, keepdims=True)
            sl = slice(LANES * pair, LANES * (pair + 1))
            outs.append(jnp.dot(p.astype(BF16), vv[:, sl], preferred_element_type=F32) / l)
            lses.append(m + jnp.log2(l))
        for pair in range(DIL_W // LANES):
            sl = slice(LANES * pair, LANES * (pair + 1))
            o_ref[sq, rows, sl] = jnp.where(lane < HEAD_DIM, outs[2 * pair], outs[2 * pair + 1])
            l_ref[sq, rows, sl] = jnp.where(lane < HEAD_DIM, lses[2 * pair], lses[2 * pair + 1])


def _sliding_window(dil, nbk, name):
    lead, (m, w3) = dil.shape[:-2], dil.shape[-2:]
    seqs = dil.reshape((-1, m, w3))
    per_seq = min(nbk, m // SW_BLOCK)
    nsq = max(1, nbk // per_seq)
    nbk = per_seq
    rows = SW_BLOCK * nbk
    cur = lambda off: pl.BlockSpec((nsq, rows, DIL_W), lambda n, i: (n, i, off))
    prev = lambda off: pl.BlockSpec((nsq, SW_BLOCK, DIL_W), lambda n, i: (n, jnp.maximum(i * nbk - 1, 0), off))
    out_spec = pl.BlockSpec((nsq, rows, DIL_W), lambda n, i: (n, i, 0))
    o, l = pl.pallas_call(
        functools.partial(_sw_kernel, nbk=nbk),
        grid=(seqs.shape[0] // nsq, m // rows),
        in_specs=[cur(0), cur(1), cur(2), prev(1), prev(2)],
        out_specs=[out_spec, out_spec],
        out_shape=[jax.ShapeDtypeStruct((seqs.shape[0], m, DIL_W), F32)] * 2,
        compiler_params=_cparams(("parallel", "arbitrary")),
        name=name,
    )(seqs, seqs, seqs, seqs, seqs)
    return o.reshape(lead + (m, DIL_W)), l.reshape(lead + (m, DIL_W))


def _mlp_tail(y, mlp_refs):
    sh_ref, sc_ref, g_ref, nw_ref, w1_ref, w2_ref = mlp_refs
    h = (_rms(y, nw_ref[...]) * (1.0 + sc_ref[0]) + sh_ref[0]).astype(BF16)
    acc = jnp.zeros(y.shape, F32)
    for c in range(w1_ref.shape[1] // MLP_CHUNK):
        cols = slice(MLP_CHUNK * c, MLP_CHUNK * (c + 1))
        a = jnp.maximum(jnp.dot(h, w1_ref[:, cols], preferred_element_type=F32), 0.0)
        acc = acc + jnp.dot((a * a).astype(BF16), w2_ref[cols, :], preferred_element_type=F32)
    return y + g_ref[0] * acc


def _proj_t(ot_ref, w_rows):
    return jnp.concatenate([_tn_dot(ot_ref[0, j], w_rows) for j in range(ot_ref.shape[1])], axis=0)


def _mlp_specs(mlp_args, d):
    per_b = pl.BlockSpec((1, 1, d), lambda bi, i: (bi, 0, 0))
    const = lambda a: pl.BlockSpec(a.shape, lambda bi, i: (0, 0), pipeline_mode=pl.Buffered(1))
    sh, sc, g2, nw, w1, w2 = mlp_args
    return [per_b, per_b, per_b, pl.BlockSpec(nw.shape, lambda bi, i: (0, 0)), const(w1), const(w2)]


def _even_out_kernel(x_ref, g_ref, oat_ref, o0_ref, o1_ref, o2_ref, l0_ref, l1_ref, l2_ref, w_ref,
                     *rest):
    *mlp_refs, y_ref, tscr = rest
    tm = x_ref.shape[1]
    ncol = DIL_W // LANES

    def token_major(ref, slot):
        r = ref.shape[1]
        for c in range(r):
            for j in range(ncol):
                tscr[slot * ncol + j, pl.ds(c, tm // r, stride=r), :] = ref[0, c, :, LANES * j:LANES * (j + 1)]
        return jnp.concatenate([tscr[slot * ncol + j] for j in range(ncol)], axis=1)

    o0, l0 = o0_ref[0], l0_ref[0]
    o1, l1 = token_major(o1_ref, 0), token_major(l1_ref, 1)
    o2, l2 = token_major(o2_ref, 2), token_major(l2_ref, 3)
    top = jnp.maximum(jnp.maximum(l0, l1), l2)
    w0, w1, w2 = jnp.exp2(l0 - top), jnp.exp2(l1 - top), jnp.exp2(l2 - top)
    ob = (w0 * o0 + w1 * o1 + w2 * o2) / (w0 + w1 + w2)
    na = oat_ref.shape[2]
    y = _proj_t(oat_ref, w_ref[:na, :]) + jnp.dot(ob.astype(BF16), w_ref[na:, :], preferred_element_type=F32)
    y_ref[0] = _mlp_tail(x_ref[0] + g_ref[0] * y, mlp_refs)


def _t_spec(a, tm):
    return pl.BlockSpec((1, tm // a.shape[-1]) + a.shape[2:], lambda bi, i: (bi, i, 0, 0))


def _even_out(x, g1, oat, o_dil, l_dil, w_out, mlp_args, tm):
    b, s, d = x.shape
    tok = lambda w: pl.BlockSpec((1, tm, w), lambda bi, i: (bi, i, 0))

    def dil_spec(a):
        if a.ndim == 3:
            return tok(DIL_W)
        r = a.shape[1]
        return pl.BlockSpec((1, r, tm // r, DIL_W), lambda bi, i: (bi, 0, i, 0))

    return pl.pallas_call(
        _even_out_kernel,
        grid=(b, s // tm),
        in_specs=[tok(d), pl.BlockSpec((1, 1, d), lambda bi, i: (bi, 0, 0)), _t_spec(oat, tm)]
                 + [dil_spec(a) for a in (*o_dil, *l_dil)]
                 + [pl.BlockSpec(w_out.shape, lambda bi, i: (0, 0), pipeline_mode=pl.Buffered(1))]
                 + _mlp_specs(mlp_args, d),
        out_specs=tok(d),
        out_shape=jax.ShapeDtypeStruct((b, s, d), F32),
        scratch_shapes=[pltpu.VMEM((4 * DIL_W // LANES, tm, LANES), F32)],
        compiler_params=_cparams(("parallel", "arbitrary")),
        name="even_out_mlp",
    )(x, g1, oat, *o_dil, *l_dil, w_out, *mlp_args)


def _odd_out_kernel(x_ref, g_ref, oct_ref, odt_ref, w_ref, *rest):
    *mlp_refs, y_ref = rest
    nc = oct_ref.shape[2]
    y = _proj_t(oct_ref, w_ref[:nc, :]) + _proj_t(odt_ref, w_ref[nc:, :])
    y_ref[0] = _mlp_tail(x_ref[0] + g_ref[0] * y, mlp_refs)


def _odd_out(x, g1, oct, odt, w_out, mlp_args, tm):
    b, s, d = x.shape
    tok = lambda w: pl.BlockSpec((1, tm, w), lambda bi, i: (bi, i, 0))
    return pl.pallas_call(
        _odd_out_kernel,
        grid=(b, s // tm),
        in_specs=[tok(d), pl.BlockSpec((1, 1, d), lambda bi, i: (bi, 0, 0)), _t_spec(oct, tm), _t_spec(odt, tm),
                  pl.BlockSpec(w_out.shape, lambda bi, i: (0, 0), pipeline_mode=pl.Buffered(1))]
                 + _mlp_specs(mlp_args, d),
        out_specs=tok(d),
        out_shape=jax.ShapeDtypeStruct((b, s, d), F32),
        compiler_params=_cparams(("parallel", "arbitrary")),
        name="odd_out_mlp",
    )(x, g1, oct, odt, w_out, *mlp_args)


def _rope_tables(positions):
    pos = positions.astype(F32)[:, :, None]
    half = HEAD_DIM // 2
    inv = ROPE_THETA ** (-jnp.arange(half, dtype=F32) / half)
    ang = pos * inv[_freq_order()]
    c, s = jnp.cos(ang), jnp.sin(ang)

    def tiled(x, width):
        reps = LANES // width
        return jnp.broadcast_to(x[:, :, None, :], x.shape[:2] + (reps, width)).reshape(x.shape[:2] + (LANES,))

    lane = np.arange(LANES)
    sign = np.where(lane % HEAD_DIM < half, -1.0, 1.0).astype(np.float32)
    c64, s64 = tiled(c, half), tiled(s, half) * sign
    hm = MLA_ROPE // 2
    rope = (lane >= MLA_NOPE) & (lane < MLA_QK)
    sign_m = np.where(lane < MLA_NOPE + hm, -1.0, 1.0).astype(np.float32)
    cm = jnp.where(rope, tiled(c[..., :hm], hm), 1.0)
    sm = jnp.where(rope, tiled(s[..., :hm], hm) * sign_m, 0.0)
    return cm, sm, c64, s64


def _freq_order():
    half = HEAD_DIM // 2
    return np.concatenate([np.arange(0, half, 2), np.arange(1, half, 2)])


def _head_perm(n):
    within = np.concatenate([_freq_order(), HEAD_DIM // 2 + _freq_order()])
    idx = np.arange(n)
    return idx // HEAD_DIM * HEAD_DIM + within[idx % HEAD_DIM]


def _partner(n, width):
    idx = np.arange(n)
    return np.where(idx % width < width // 2, idx + width // 2, idx - width // 2)


def _gather_cols(w, cols):
    w = jnp.pad(w.astype(BF16), ((0, 0), (0, 1)))
    return jnp.take(w, jnp.asarray(cols, jnp.int32), axis=1)


def _even_weights(w_in, w_uq, w_ukv, qn, kn):
    o2 = MLA_Q_RANK + MLA_KV_RANK
    o3 = o2 + MLA_ROPE
    p_rope = _partner(MLA_ROPE, MLA_ROPE)
    tail = LANES - MLA_QK
    hp, partner = _head_perm(DIL_W), _partner(DIL_W, HEAD_DIM)
    ngroup = len(DIL_CONFIGS)

    zero = w_in.shape[1]
    kr = np.full(LANES, zero)
    kr_rot = kr.copy()
    kr[MLA_NOPE:MLA_QK] = o2 + np.arange(MLA_ROPE)
    kr_rot[MLA_NOPE:MLA_QK] = o2 + p_rope
    dil = lambda t, g, cols: o3 + (t * ngroup + g) * DIL_W + cols
    main = [dil(t, g, hp if t < 2 else np.arange(DIL_W)) for g in range(ngroup) for t in range(3)]
    rot = [dil(t, g, hp[partner]) for g in range(ngroup) for t in range(2)]
    win = _gather_cols(w_in, np.concatenate([np.arange(o2), kr, kr_rot] + main + rot))

    nqk = MLA_HEADS * MLA_QK
    head = np.full((MLA_HEADS, LANES), nqk)
    head[:, :MLA_QK] = np.arange(nqk).reshape(MLA_HEADS, MLA_QK)
    head_rot = head.copy()
    head_rot[:, MLA_NOPE:MLA_QK] = head[:, MLA_NOPE:MLA_QK][:, p_rope]
    wuq = _gather_cols(w_uq.reshape(MLA_Q_RANK, nqk), np.concatenate([head.ravel(), head_rot.ravel()]))

    nkv = w_ukv.shape[1] * w_ukv.shape[2]
    kcols = np.full((MLA_HEADS, LANES), nkv)
    kcols[:, :MLA_NOPE] = (np.arange(MLA_HEADS) * w_ukv.shape[2])[:, None] + np.arange(MLA_NOPE)
    wuk = _gather_cols(w_ukv.reshape(MLA_KV_RANK, nkv), kcols.ravel())
    wvt = w_ukv[:, :, MLA_NOPE:].reshape(MLA_KV_RANK, -1).T.astype(BF16)

    def gains(g):
        rot = jnp.concatenate([g[:MLA_NOPE], g[MLA_NOPE:][p_rope]])
        return jnp.stack([jnp.pad(g, (0, tail)), jnp.pad(rot, (0, tail))]).astype(F32)

    return win, wuq, wuk, wvt, gains(qn), gains(kn)


def _odd_weights(w_in):
    nqk = DIFF_HEADS * LANES
    nm = MOBA_HEADS * HEAD_DIM
    main = np.concatenate([np.arange(2 * nqk), 3 * nqk + np.arange(2 * nm)])
    main = main[_head_perm(main.size)]
    win = _gather_cols(w_in, np.concatenate([main, main[_partner(main.size, HEAD_DIM)]]))
    values = np.concatenate([2 * nqk + np.arange(nqk), 3 * nqk + 2 * nm + np.arange(nm)])
    return win, _gather_cols(w_in, values).T


def _gains64(g):
    g = g[_head_perm(HEAD_DIM)]
    rot = g[_partner(HEAD_DIM, HEAD_DIM)]
    return jnp.stack([jnp.concatenate([g, g]), jnp.concatenate([rot, rot])]).astype(F32)


def kernel(x, c, positions, ada_w, ada_b, norm_mix, norm_mlp, mlp_w1, mlp_w2, even_w_in, even_w_out, mla_q_lat_norm, mla_kv_lat_norm, mla_w_uq, mla_w_ukv, mla_q_norm, mla_k_norm, dil_q_norm, dil_k_norm, odd_w_in, odd_w_out, diff_q_norm, diff_k_norm, diff_lambda, diff_subln, moba_q_norm, moba_k_norm):
    b, s, d = x.shape
    depth = ada_w.shape[0]
    t_attn, tm_in, tm_out = ATTN_BLOCK, TOKEN_TILE, TOKEN_TILE

    mod = _adaln(c, ada_w, ada_b)
    cm, sm, c64, s64 = _rope_tables(positions)

    for layer in range(depth):
        sh1, sc1, g1, sh2, sc2, g2 = [mod[layer, :, d * t:d * (t + 1)].reshape(b, 1, d) for t in range(6)]
        mlp_args = (sh2, sc2, g2, _row(norm_mlp[layer]), mlp_w1[layer].astype(BF16), mlp_w2[layer].astype(BF16))
        i = layer // 2
        if layer % 2 == 0:
            win, wuq, wuk, wvt, qn, kn = _even_weights(even_w_in[i], mla_w_uq[i], mla_w_ukv[i],
                                                       mla_q_norm[i], mla_k_norm[i])
            q, k, vt, *dils = _even_in(x, sh1, sc1, _row(norm_mix[layer]), win, wuq, wuk, wvt,
                                       _row(mla_q_lat_norm[i]), _row(mla_kv_lat_norm[i]), qn, kn,
                                       _gains64(dil_q_norm[i]), _gains64(dil_k_norm[i]), cm, sm, c64, s64, tm_in, t_attn)
            o_at = _mla_attention(q, k, vt, MLA_HEADS_PER_STEP)
            o_dil, l_dil = zip(*[_sliding_window(dg, SW_BLOCKS_PER_STEP, f"sliding_window_g{g}")
                                 for g, dg in enumerate(dils)])
            x = _even_out(x, g1, o_at, o_dil, l_dil, even_w_out[i].astype(BF16), mlp_args, tm_out)
        else:
            lam_init = 0.8 - 0.6 * math.exp(-0.3 * layer)
            win, wvt = _odd_weights(odd_w_in[i])
            qc, kc, qm, qmf, km, vct, vmt, kmean = _odd_in(
                x, sh1, sc1, _row(norm_mix[layer]), win, wvt,
                _gains64(diff_q_norm[i]), _gains64(diff_k_norm[i]), _gains64(moba_q_norm[i]),
                _gains64(moba_k_norm[i]),
                c64, s64, tm_in, t_attn)
            o_ct = _diff_attention(qc, kc, vct, diff_lambda[i].astype(F32),
                                   diff_subln[i].reshape(-1, 1).astype(F32), lam_init, DIFF_HEADS_PER_STEP)
            o_dt = _moba_attention(qm, qmf, km, vmt, kmean.reshape(b, s // MOBA_BLOCK, kmean.shape[-1]),
                                   MOBA_PAIRS_PER_STEP)
            x = _odd_out(x, g1, o_ct, o_dt, odd_w_out[i].astype(BF16), mlp_args, tm_out)
    return x
```

```python
import functools
import math

import jax
import jax.numpy as jnp
import numpy as np
from jax import lax
from jax.experimental import pallas as pl
from jax.experimental.pallas import tpu as pltpu

F32 = jnp.float32
BF16 = jnp.bfloat16

LANES = 128
HEAD_DIM = 64
ROPE_THETA = 10000.0
NORM_EPS = 1e-6
NEG_INF = -1e30
LOG2E = math.log2(math.e)

MLA_HEADS = 8
MLA_Q_RANK = 384
MLA_KV_RANK = 256
MLA_NOPE = 64
MLA_ROPE = 32
MLA_QK = MLA_NOPE + MLA_ROPE
DIL_CONFIGS = ((128, 1), (512, 4), (2048, 16))
DIL_HEADS = 4
DIL_W = DIL_HEADS * HEAD_DIM
DIFF_HEADS = 4
MOBA_HEADS = 8
MOBA_BLOCK = 256
MOBA_TOPK = 3
SW_BLOCK = 128
ONES_ROWS = 16

VMEM_LIMIT = 56 * 1024 * 1024

ATTN_BLOCK = MOBA_BLOCK
TOKEN_TILE = 2 * ATTN_BLOCK
MLP_CHUNK = 1024
ADALN_TILE = 2048
SW_BLOCKS_PER_STEP = 16
MLA_HEADS_PER_STEP = 4
DIFF_HEADS_PER_STEP = 2
MOBA_PAIRS_PER_STEP = 2


def _cparams(sem):
    return pltpu.CompilerParams(dimension_semantics=sem, vmem_limit_bytes=VMEM_LIMIT)


def _nt_dot(a, b):
    return lax.dot_general(a, b, (((1,), (1,)), ((), ())), preferred_element_type=F32)


def _tn_dot(a, b):
    return lax.dot_general(a, b, (((0,), (0,)), ((), ())), preferred_element_type=F32)


def _rms(x, w):
    return x * lax.rsqrt(jnp.mean(x * x, axis=-1, keepdims=True) + NORM_EPS) * w


def _lane_iota(shape):
    return lax.broadcasted_iota(jnp.int32, shape, len(shape) - 1)


def _adaln_kernel(c_ref, w_ref, b_ref, o_ref):
    c = c_ref[...]
    cond = c * (1.0 / (1.0 + jnp.exp(-c)))
    o_ref[0] = jnp.dot(cond, w_ref[0], preferred_element_type=F32,
                       precision=lax.Precision.HIGHEST) + b_ref[0]


def _adaln(c, ada_w, ada_b):
    depth, d, n = ada_w.shape
    b = c.shape[0]
    tn = ADALN_TILE
    return pl.pallas_call(
        _adaln_kernel,
        grid=(depth, n // tn),
        in_specs=[pl.BlockSpec((b, d), lambda l, j: (0, 0)),
                  pl.BlockSpec((1, d, tn), lambda l, j: (l, 0, j)),
                  pl.BlockSpec((1, 1, tn), lambda l, j: (l, 0, j))],
        out_specs=pl.BlockSpec((1, b, tn), lambda l, j: (l, 0, j)),
        out_shape=jax.ShapeDtypeStruct((depth, b, n), F32),
        compiler_params=_cparams(("parallel", "parallel")),
        name="adaln",
    )(c, ada_w, ada_b.reshape(depth, 1, n))


def _inv_rms64(x):
    lane = _lane_iota((1, LANES))
    lo = lane < HEAD_DIM
    sq = x * x
    s_lo = jnp.sum(jnp.where(lo, sq, 0.0), axis=-1, keepdims=True)
    s_hi = jnp.sum(jnp.where(lo, 0.0, sq), axis=-1, keepdims=True)
    return lax.rsqrt(jnp.where(lo, s_lo, s_hi) * (1.0 / HEAD_DIM) + NORM_EPS)


def _rope_lanes64(tab):
    lane = _lane_iota((1, LANES))
    half = HEAD_DIM // 2
    c = jnp.where(lane < half, tab, 0.0)
    s = pltpu.roll(jnp.where((lane >= half) & (lane < 2 * half), tab, 0.0), LANES - half, 1)
    tile = lambda x: x + pltpu.roll(x, half, 1) + pltpu.roll(x, 2 * half, 1) + pltpu.roll(x, 3 * half, 1)
    return tile(c), tile(s) * jnp.where(lane % HEAD_DIM < half, -1.0, 1.0)


def _rope_lanes_mla(tab):
    lane = _lane_iota((1, LANES))
    half, hm = HEAD_DIM // 2, MLA_ROPE // 2
    c = jnp.where(lane < hm, tab, 0.0)
    s = jnp.where((lane >= half) & (lane < half + hm), tab, 0.0)
    rope = (lane >= MLA_NOPE) & (lane < MLA_QK)
    cos = jnp.where(rope, pltpu.roll(c, MLA_NOPE, 1) + pltpu.roll(c, MLA_NOPE + hm, 1), 1.0)
    sin = pltpu.roll(s, MLA_NOPE + hm - half, 1) - pltpu.roll(s, MLA_NOPE - half, 1)
    return cos, sin


def _store_t_blocks(out_ref, xt):
    t = out_ref.shape[-1]
    for j in range(out_ref.shape[1]):
        out_ref[0, j] = xt[:, t * j:t * (j + 1)]


def _store_col_t(out_ref, col, x):
    t = out_ref.shape[-1]
    for j in range(out_ref.shape[1]):
        out_ref[0, j, LANES * col:LANES * (col + 1), :] = x[t * j:t * (j + 1)].T.astype(out_ref.dtype)


def _t_blocks(b, s, rows, tm, t):
    return (pl.BlockSpec((1, tm // t, rows, t), lambda bi, i: (bi, i, 0, 0)),
            jax.ShapeDtypeStruct((b, s // t, rows, t), BF16))


def _even_in_kernel(x_ref, sh_ref, sc_ref, nw_ref, win_ref, wuq_ref, wuk_ref, wvt_ref,
                    qlat_ref, kvlat_ref, qn_ref, kn_ref, dqn_ref, dkn_ref,
                    rope_ref,
                    qt_out, k_out, vt_out, d0_out, d1_out, d2_out, dscr):
    x = x_ref[0]
    tm = x.shape[0]
    h = _rms(x, nw_ref[...]) * (1.0 + sc_ref[0]) + sh_ref[0]
    u = jnp.dot(h.astype(BF16), win_ref[...], preferred_element_type=F32)

    o1 = MLA_Q_RANK
    o2 = o1 + MLA_KV_RANK
    o_kr, o_krr, o_dil = o2, o2 + LANES, o2 + 2 * LANES
    n_dil = 3 * DIL_W * len(DIL_CONFIGS)
    o_rot = o_dil + n_dil
    nq = MLA_HEADS * LANES
    cqn = _rms(u[:, :o1], qlat_ref[...]).astype(BF16)
    qp = jnp.dot(cqn, wuq_ref[...], preferred_element_type=F32)
    ckvn = _rms(u[:, o1:o2], kvlat_ref[...]).astype(BF16)
    kvp = jnp.dot(ckvn, wuk_ref[...], preferred_element_type=F32)
    _store_t_blocks(vt_out, _nt_dot(wvt_ref[...], ckvn).astype(BF16))
    kr, kr_rot = u[:, o_kr:o_kr + LANES], u[:, o_krr:o_krr + LANES]

    cm, sm = _rope_lanes_mla(rope_ref[0])
    q_scale = MLA_QK ** -0.5 * LOG2E
    qgc, qgs = qn_ref[0:1] * cm * q_scale, qn_ref[1:2] * sm * q_scale
    kgc, kgs = kn_ref[0:1] * cm, kn_ref[1:2] * sm
    kr_term = kr_rot * kgs
    for hd in range(MLA_HEADS):
        sl = slice(LANES * hd, LANES * (hd + 1))
        qh = qp[:, sl]
        inv = lax.rsqrt(jnp.sum(qh * qh, -1, keepdims=True) * (1.0 / MLA_QK) + NORM_EPS)
        _store_col_t(qt_out, hd, inv * (qh * qgc + qp[:, nq + LANES * hd:nq + LANES * (hd + 1)] * qgs))
        kh = kvp[:, sl] + kr
        inv = lax.rsqrt(jnp.sum(kh * kh, -1, keepdims=True) * (1.0 / MLA_QK) + NORM_EPS)
        k_out[0, :, sl] = (inv * (kh * kgc + kr_term)).astype(BF16)

    c64, s64 = _rope_lanes64(rope_ref[0])
    d_scale = HEAD_DIM ** -0.5 * LOG2E
    dqc, dqs = dqn_ref[0:1] * c64 * d_scale, dqn_ref[1:2] * s64 * d_scale
    dkc, dks = dkn_ref[0:1] * c64, dkn_ref[1:2] * s64
    ncol = DIL_W // LANES
    for g, d_out in enumerate((d0_out, d1_out, d2_out)):
        _, r = DIL_CONFIGS[g]
        base = o_dil + 3 * DIL_W * g
        rbase = o_rot + 2 * DIL_W * g
        for j in range(3 * ncol):
            xc = u[:, base + LANES * j:base + LANES * (j + 1)]
            if j < 2 * ncol:
                xr = u[:, rbase + LANES * j:rbase + LANES * (j + 1)]
                gc, gs = (dqc, dqs) if j < ncol else (dkc, dks)
                xc = _inv_rms64(xc) * (xc * gc + xr * gs)
            if r == 1:
                d_out[0, :, LANES * j:LANES * (j + 1)] = xc.astype(BF16)
            else:
                dscr[j] = xc
                for c in range(r):
                    d_out[0, c, :, LANES * j:LANES * (j + 1)] = dscr[j, pl.ds(c, tm // r, stride=r), :].astype(BF16)


def _row(v):
    return v.reshape(1, -1).astype(F32)


def _even_in(x, sh, sc, nw, win, wuq, wuk, wvt, qlat, kvlat, qn, kn, dqn, dkn, rope, tm, t):
    b, s, d = x.shape
    tok = lambda w: pl.BlockSpec((1, tm, w), lambda bi, i: (bi, i, 0))
    per_b = pl.BlockSpec((1, 1, d), lambda bi, i: (bi, 0, 0))
    full = lambda a: pl.BlockSpec(a.shape, lambda bi, i: (0,) * a.ndim, pipeline_mode=pl.Buffered(1))
    vt_spec, vt_shape = _t_blocks(b, s, wvt.shape[0], tm, t)
    qt_spec, qt_shape = _t_blocks(b, s, MLA_HEADS * LANES, tm, t)
    dil_specs, dil_shapes = [], []
    for _, r in DIL_CONFIGS:
        if r == 1:
            dil_specs.append(tok(3 * DIL_W))
            dil_shapes.append(jax.ShapeDtypeStruct((b, s, 3 * DIL_W), BF16))
        else:
            dil_specs.append(pl.BlockSpec((1, r, tm // r, 3 * DIL_W), lambda bi, i: (bi, 0, i, 0)))
            dil_shapes.append(jax.ShapeDtypeStruct((b, r, s // r, 3 * DIL_W), BF16))
    return pl.pallas_call(
        _even_in_kernel,
        grid=(b, s // tm),
        in_specs=[tok(d), per_b, per_b, full(nw), full(win), full(wuq), full(wuk), full(wvt),
                  full(qlat), full(kvlat), full(qn), full(kn), full(dqn), full(dkn),
                  tok(LANES)],
        out_specs=[qt_spec, tok(MLA_HEADS * LANES), vt_spec] + dil_specs,
        out_shape=[qt_shape, jax.ShapeDtypeStruct((b, s, MLA_HEADS * LANES), BF16), vt_shape] + dil_shapes,
        scratch_shapes=[pltpu.VMEM((3 * DIL_W // LANES, tm, LANES), F32)],
        compiler_params=_cparams(("parallel", "arbitrary")),
        name="even_in_proj",
    )(x, sh, sc, nw, win, wuq, wuk, wvt, qlat, kvlat, qn, kn, dqn, dkn, rope)


def _odd_in_kernel(x_ref, sh_ref, sc_ref, nw_ref, win_ref, wvt_ref, dqn_ref, dkn_ref, mqn_ref, mkn_ref,
                   rope_ref,
                   qc_out, kc_out, qm_out, qmf_out, km_out, vct_out, vmt_out, kmean_out):
    x = x_ref[0]
    tm = x.shape[0]
    h = (_rms(x, nw_ref[...]) * (1.0 + sc_ref[0]) + sh_ref[0]).astype(BF16)
    u = jnp.dot(h, win_ref[...], preferred_element_type=F32)
    vt = _nt_dot(wvt_ref[...], h).astype(BF16)
    nvc = vct_out.shape[2]
    _store_t_blocks(vct_out, vt[:nvc])
    _store_t_blocks(vmt_out, vt[nvc:])
    c64, s64 = _rope_lanes64(rope_ref[0])
    scale = HEAD_DIM ** -0.5 * LOG2E
    nqk = DIFF_HEADS * LANES
    nm = MOBA_HEADS * HEAD_DIM
    rot = 2 * nqk + 2 * nm

    def tables(gain_ref, mult):
        return gain_ref[0:1] * c64 * mult, gain_ref[1:2] * s64 * mult

    def roped(off, col, tab):
        a = off + LANES * col
        xc, xr = u[:, a:a + LANES], u[:, rot + a:rot + a + LANES]
        return _inv_rms64(xc) * (xc * tab[0] + xr * tab[1])

    t_qc, t_kc, t_qm, t_km = tables(dqn_ref, scale), tables(dkn_ref, 1.0), tables(mqn_ref, 1.0), tables(mkn_ref, 1.0)
    for col in range(nqk // LANES):
        sl = slice(LANES * col, LANES * (col + 1))
        qc_out[0, :, sl] = roped(0, col, t_qc).astype(BF16)
        kc_out[0, :, sl] = roped(nqk, col, t_kc).astype(BF16)

    base = 2 * nqk
    for col in range(nm // LANES):
        sl = slice(LANES * col, LANES * (col + 1))
        qm = roped(base, col, t_qm)
        qmf_out[0, :, sl] = qm
        qm_out[0, :, sl] = (qm * scale).astype(BF16)
        km = roped(base + nm, col, t_km)
        km_out[0, :, sl] = km.astype(BF16)
        for blk in range(tm // MOBA_BLOCK):
            rows = slice(MOBA_BLOCK * blk, MOBA_BLOCK * (blk + 1))
            kmean_out[0, 0, blk:blk + 1, sl] = jnp.mean(km[rows], axis=0, keepdims=True)


def _odd_in(x, sh, sc, nw, win, wvt, dqn, dkn, mqn, mkn, rope, tm, t):
    b, s, d = x.shape
    tok = lambda w: pl.BlockSpec((1, tm, w), lambda bi, i: (bi, i, 0))
    per_b = pl.BlockSpec((1, 1, d), lambda bi, i: (bi, 0, 0))
    full = lambda a: pl.BlockSpec(a.shape, lambda bi, i: (0,) * a.ndim, pipeline_mode=pl.Buffered(1))
    w = DIFF_HEADS * LANES
    nblk = tm // MOBA_BLOCK
    shp = lambda dt: jax.ShapeDtypeStruct((b, s, w), dt)
    vt_spec, vt_shape = _t_blocks(b, s, wvt.shape[0] // 2, tm, t)
    return pl.pallas_call(
        _odd_in_kernel,
        grid=(b, s // tm),
        in_specs=[tok(d), per_b, per_b, full(nw), full(win), full(wvt), full(dqn), full(dkn), full(mqn), full(mkn),
                  tok(LANES)],
        out_specs=[tok(w)] * 5 + [vt_spec, vt_spec, pl.BlockSpec((1, 1, nblk, w), lambda bi, i: (bi, i, 0, 0))],
        out_shape=[shp(BF16), shp(BF16), shp(BF16), shp(F32), shp(BF16), vt_shape, vt_shape,
                   jax.ShapeDtypeStruct((b, s // tm, nblk, w), F32)],
        compiler_params=_cparams(("parallel", "arbitrary")),
        name="odd_in_proj",
    )(x, sh, sc, nw, win, wvt, dqn, dkn, mqn, mkn, rope)


def _qt_halves(q):
    qt = q.astype(F32).T
    row = lax.broadcasted_iota(jnp.int32, qt.shape, 0)
    return [jnp.where((row >= HEAD_DIM) == bool(half), qt, 0.0).astype(BF16) for half in range(2)]


def _flash_t(qts, k_ref, vt_ref, k_slices, v_rows, scr, i, t, past_mask=None):
    m_scr, acc_scr, s_scr, p_scr, a_scr, b_scr = scr
    n = len(qts)
    ones = jnp.ones((ONES_ROWS, t), BF16)

    def qk(blk, s):
        b0 = pl.multiple_of(blk * t, t)
        return jnp.dot(k_ref[0, pl.ds(b0, t), k_slices[s]], qts[s], preferred_element_type=F32)

    def pv(blk, s, p):
        vt = jnp.concatenate([vt_ref[0, blk, v_rows[s], :], ones], axis=0)
        return jnp.dot(vt, p, preferred_element_type=F32)

    def store_scores(slot, s, st):
        s_scr[slot, s] = st
        b_scr[slot, s] = jnp.max(st, axis=0, keepdims=True)

    krow = lax.broadcasted_iota(jnp.int32, (t, t), 0)
    qcol = lax.broadcasted_iota(jnp.int32, (t, t), 1)
    causal = krow <= qcol
    last_past = jnp.maximum(i - 1, 0)
    for s in range(n):
        st = jnp.where(causal, qk(i, s), NEG_INF)
        m = jnp.max(st, axis=0, keepdims=True)
        m_scr[s] = m
        p_scr[0, s] = jnp.exp2(st - m).astype(BF16)
        p_scr[1, s] = jnp.zeros((t, t), BF16)
        a_scr[0, s] = jnp.ones_like(m)
        a_scr[1, s] = jnp.ones_like(m)
        acc_scr[s] = jnp.zeros(acc_scr.shape[1:], F32)
        store_scores(0, s, qk(0, s))
        store_scores(1, s, qk(jnp.minimum(1, last_past), s))

    def softmax_stage(slot, valid, rs, ws, blk):
        for s in range(n):
            cmax = b_scr[2 * rs + slot, s]
            keep = None if past_mask is None else past_mask(s, blk)
            if keep is not None:
                cmax = jnp.where(keep, cmax, NEG_INF)
            if valid is not None:
                cmax = jnp.where(valid, cmax, NEG_INF)
            m_prev = m_scr[s]
            m_new = jnp.maximum(m_prev, cmax)
            a_scr[2 * ws + slot, s] = jnp.exp2(m_prev - m_new)
            p = jnp.exp2(s_scr[2 * rs + slot, s] - m_new).astype(BF16)
            if keep is not None:
                p = jnp.where(keep, p, jnp.zeros_like(p))
            p_scr[2 * ws + slot, s] = p
            m_scr[s] = m_new

    def body(kk, rs, ws):
        first = kk == 0
        pa_blk = jnp.where(first, i, 2 * kk - 2)
        pb_blk = jnp.where(first, i, 2 * kk - 1)
        p_prev = [[p_scr[2 * rs + sl, s] for s in range(n)] for sl in range(2)]
        a_prev = [[a_scr[2 * rs + sl, s] for s in range(n)] for sl in range(2)]
        s_next = [[qk(jnp.minimum(2 * kk + 2 + sl, last_past), s) for s in range(n)] for sl in range(2)]
        pvs = [[pv(blk, s, p_prev[sl][s]) for s in range(n)] for sl, blk in enumerate((pa_blk, pb_blk))]
        softmax_stage(0, None, rs, ws, 2 * kk)
        softmax_stage(1, 2 * kk + 1 < i, rs, ws, jnp.minimum(2 * kk + 1, last_past))
        for sl in range(2):
            for s in range(n):
                store_scores(2 * ws + sl, s, s_next[sl][s])
        for s in range(n):
            acc_scr[s] = a_prev[1][s] * (a_prev[0][s] * acc_scr[s] + pvs[0][s]) + pvs[1][s]

    def two_bodies(kp, carry):
        body(2 * kp, 0, 1)
        body(2 * kp + 1, 1, 0)
        return carry

    nbody = (i + 1) // 2
    lax.fori_loop(0, nbody // 2, two_bodies, 0)

    @pl.when(nbody % 2 == 1)
    def _():
        body(nbody - 1, 0, 1)

    fs = nbody % 2
    kl = jnp.maximum(nbody - 1, 0)
    none = i == 0
    fa_blk = jnp.where(none, i, 2 * kl)
    fb_blk = jnp.where(none, i, jnp.minimum(2 * kl + 1, last_past))
    fb_scale = jnp.where(jnp.logical_or(none, i % 2 == 0), 1.0, 0.0)
    for s in range(n):
        acc_scr[s] = (a_scr[2 * fs + 1, s] * (a_scr[2 * fs, s] * acc_scr[s] + pv(fa_blk, s, p_scr[2 * fs, s]))
                      + fb_scale * pv(fb_blk, s, p_scr[2 * fs + 1, s]))


def _flash_scratch(n, nv, t):
    return [pltpu.VMEM((n, 1, t), F32), pltpu.VMEM((n, nv + ONES_ROWS, t), F32), pltpu.VMEM((4, n, t, t), F32),
            pltpu.VMEM((4, n, t, t), BF16), pltpu.VMEM((4, n, 1, t), F32), pltpu.VMEM((4, n, 1, t), F32)]


def _normalised(acc_scr, s, nv):
    return acc_scr[s, :nv, :] / acc_scr[s, nv:nv + 1, :]


def _half_mask(x, half):
    lane = _lane_iota((1, LANES))
    return jnp.where((lane >= HEAD_DIM) == bool(half), x, jnp.zeros_like(x))


def _mla_kernel(qt_ref, k_ref, vt_ref, ot_ref, *scr, t, nh):
    i = pl.program_id(2)
    ks = [slice(LANES * s, LANES * (s + 1)) for s in range(nh)]
    qs = [qt_ref[0, 0, ks[s], :] for s in range(nh)]
    vr = [slice(HEAD_DIM * s, HEAD_DIM * (s + 1)) for s in range(nh)]
    _flash_t(qs, k_ref, vt_ref, ks, vr, scr, i, t)
    for s in range(nh):
        ot_ref[0, 0, vr[s], :] = _normalised(scr[1], s, HEAD_DIM).astype(ot_ref.dtype)


def _mla_attention(qt, k, vt, nh):
    b, s, _ = k.shape
    _, nb, nv, t = vt.shape
    return pl.pallas_call(
        functools.partial(_mla_kernel, t=t, nh=nh),
        grid=(b, MLA_HEADS // nh, nb),
        in_specs=[pl.BlockSpec((1, 1, nh * LANES, t), lambda bi, p, i: (bi, i, p, 0)),
                  pl.BlockSpec((1, s, nh * LANES), lambda bi, p, i: (bi, 0, p)),
                  pl.BlockSpec((1, nb, nh * HEAD_DIM, t), lambda bi, p, i: (bi, 0, p, 0))],
        out_specs=pl.BlockSpec((1, 1, nh * HEAD_DIM, t), lambda bi, p, i: (bi, i, p, 0)),
        out_shape=jax.ShapeDtypeStruct((b, nb, nv, t), BF16),
        scratch_shapes=_flash_scratch(nh, HEAD_DIM, t),
        compiler_params=_cparams(("parallel", "parallel", "arbitrary")),
        name="mla_attention",
    )(qt, k, vt)


def _diff_kernel(q_ref, k_ref, vt_ref, lam_ref, sub_ref, ot_ref, *scr, t, nh, lam_init):
    i = pl.program_id(2)
    qs, ks, vr = [], [], []
    for hd in range(nh):
        cols = slice(LANES * hd, LANES * (hd + 1))
        qs += _qt_halves(q_ref[0, :, cols])
        ks += [cols, cols]
        vr += [cols, cols]
    _flash_t(qs, k_ref, vt_ref, ks, vr, scr, i, t)
    lv = lam_ref[...]
    lam = (jnp.exp(jnp.sum(lv[0:1] * lv[1:2], keepdims=True))
           - jnp.exp(jnp.sum(lv[2:3] * lv[3:4], keepdims=True)) + lam_init)
    for hd in range(nh):
        o = _normalised(scr[1], 2 * hd, LANES) - lam * _normalised(scr[1], 2 * hd + 1, LANES)
        o = o * lax.rsqrt(jnp.mean(o * o, axis=0, keepdims=True) + NORM_EPS) * sub_ref[...]
        ot_ref[0, 0, LANES * hd:LANES * (hd + 1), :] = (o * (1.0 - lam_init)).astype(ot_ref.dtype)


def _diff_attention(q, k, vt, lam_rows, subln_col, lam_init, nh):
    b, s, w = q.shape
    _, nb, _, t = vt.shape
    return pl.pallas_call(
        functools.partial(_diff_kernel, t=t, nh=nh, lam_init=lam_init),
        grid=(b, DIFF_HEADS // nh, nb),
        in_specs=[pl.BlockSpec((1, t, nh * LANES), lambda bi, p, i: (bi, i, p)),
                  pl.BlockSpec((1, s, nh * LANES), lambda bi, p, i: (bi, 0, p)),
                  pl.BlockSpec((1, nb, nh * LANES, t), lambda bi, p, i: (bi, 0, p, 0)),
                  pl.BlockSpec(lam_rows.shape, lambda bi, p, i: (0, 0)),
                  pl.BlockSpec(subln_col.shape, lambda bi, p, i: (0, 0))],
        out_specs=pl.BlockSpec((1, 1, nh * LANES, t), lambda bi, p, i: (bi, i, p, 0)),
        out_shape=jax.ShapeDtypeStruct((b, nb, w, t), BF16),
        scratch_shapes=_flash_scratch(2 * nh, LANES, t),
        compiler_params=_cparams(("parallel", "parallel", "arbitrary")),
        name="diff_attention",
    )(q, k, vt, lam_rows, subln_col)


def _moba_kernel(q_ref, qf_ref, k_ref, vt_ref, kmean_ref, ot_ref, sel_scr, *scr, t, npair):
    i = pl.program_id(2)
    nb = kmean_ref.shape[1]
    brow = lax.broadcasted_iota(jnp.int32, (nb, t), 0).astype(F32)
    past = brow < i.astype(F32)
    qs, ks, vr = [], [], []
    for pr in range(npair):
        cols = slice(LANES * pr, LANES * (pr + 1))
        qf, kmean = qf_ref[0, :, cols], kmean_ref[0, :, cols]
        qs += _qt_halves(q_ref[0, :, cols])
        for half in range(2):
            s = 2 * pr + half
            gate = lax.dot_general(kmean, _half_mask(qf, half), (((1,), (1,)), ((), ())),
                                   preferred_element_type=F32, precision=lax.Precision.HIGHEST)
            gate = jnp.where(past, gate, NEG_INF)
            sel = jnp.zeros((nb, t), F32)
            for _ in range(MOBA_TOPK):
                top = jnp.max(gate, axis=0, keepdims=True)
                first = jnp.min(jnp.where(gate == top, brow, float(nb)), axis=0, keepdims=True)
                pick = brow == first
                sel = jnp.where(pick, 1.0, sel)
                gate = jnp.where(pick, 2 * NEG_INF, gate)
            sel_scr[s] = jnp.where(past, sel, 0.0)
            ks.append(cols)
            vr.append(slice(HEAD_DIM * s, HEAD_DIM * (s + 1)))
    _flash_t(qs, k_ref, vt_ref, ks, vr, scr, i, t,
             past_mask=lambda s, jj: sel_scr[s, pl.ds(jj, 1), :] > 0.5)
    for s in range(2 * npair):
        ot_ref[0, 0, vr[s], :] = _normalised(scr[1], s, HEAD_DIM).astype(ot_ref.dtype)


def _moba_attention(q, qf, k, vt, kmean, npair):
    b, s, w = q.shape
    _, nb, _, t = vt.shape
    nkm = kmean.shape[1]
    lanes = npair * LANES
    return pl.pallas_call(
        functools.partial(_moba_kernel, t=t, npair=npair),
        grid=(b, w // lanes, nb),
        in_specs=[pl.BlockSpec((1, t, lanes), lambda bi, p, i: (bi, i, p)),
                  pl.BlockSpec((1, t, lanes), lambda bi, p, i: (bi, i, p)),
                  pl.BlockSpec((1, s, lanes), lambda bi, p, i: (bi, 0, p)),
                  pl.BlockSpec((1, nb, lanes, t), lambda bi, p, i: (bi, 0, p, 0)),
                  pl.BlockSpec((1, nkm, lanes), lambda bi, p, i: (bi, 0, p))],
        out_specs=pl.BlockSpec((1, 1, lanes, t), lambda bi, p, i: (bi, i, p, 0)),
        out_shape=jax.ShapeDtypeStruct((b, nb, w, t), BF16),
        scratch_shapes=[pltpu.VMEM((2 * npair, nkm, t), F32)] + _flash_scratch(2 * npair, HEAD_DIM, t),
        compiler_params=_cparams(("parallel", "parallel", "arbitrary")),
        name="moba_attention",
    )(q, qf, k, vt, kmean)


def _sw_kernel(q_ref, kc_ref, vc_ref, kp_ref, vp_ref, o_ref, l_ref, *, nbk):
    i = pl.program_id(1)
    blk = SW_BLOCK
    qi = lax.broadcasted_iota(jnp.int32, (blk, 2 * blk), 0)
    kj = lax.broadcasted_iota(jnp.int32, (blk, 2 * blk), 1)
    band = (kj >= qi) & (kj <= qi + blk)
    lane = _lane_iota((1, LANES))
    heads = [(pair, half) for pair in range(DIL_W // LANES) for half in range(2)]

    def scores(sq, n):
        rows = slice(blk * n, blk * (n + 1))
        q = q_ref[sq, rows, :]
        if n == 0:
            kprev, valid = kp_ref[sq], band & ((kj >= blk) | (i > 0))
        else:
            kprev, valid = kc_ref[sq, blk * (n - 1):blk * n, :], band
        kk = jnp.concatenate([kprev, kc_ref[sq, rows, :]], axis=0)
        out = []
        for pair, half in heads:
            sl = slice(LANES * pair, LANES * (pair + 1))
            out.append(jnp.where(valid, _nt_dot(_half_mask(q[:, sl], half), kk[:, sl]), NEG_INF))
        return out

    work = [(sq, n) for sq in range(q_ref.shape[0]) for n in range(nbk)]
    nxt = scores(*work[0])
    for w, (sq, n) in enumerate(work):
        cur, rows = nxt, slice(blk * n, blk * (n + 1))
        if w + 1 < len(work):
            nxt = scores(*work[w + 1])
        vprev = vp_ref[sq] if n == 0 else vc_ref[sq, blk * (n - 1):blk * n, :]
        vv = jnp.concatenate([vprev, vc_ref[sq, rows, :]], axis=0)
        outs, lses = [], []
        for (pair, half), sc in zip(heads, cur):
            m = jnp.max(sc, axis=-1, keepdims=True)
            p = jnp.exp2(sc - m)
            l = jnp.sum(p, axis=-1, keepdims=True)
            sl = slice(LANES * pair, LANES * (pair + 1))
            outs.append(jnp.dot(p.astype(BF16), vv[:, sl], preferred_element_type=F32) / l)
            lses.append(m + jnp.log2(l))
        for pair in range(DIL_W // LANES):
            sl = slice(LANES * pair, LANES * (pair + 1))
            o_ref[sq, rows, sl] = jnp.where(lane < HEAD_DIM, outs[2 * pair], outs[2 * pair + 1])
            l_ref[sq, rows, sl] = jnp.where(lane < HEAD_DIM, lses[2 * pair], lses[2 * pair + 1])


def _sliding_window(dil, nbk, name):
    lead, (m, w3) = dil.shape[:-2], dil.shape[-2:]
    seqs = dil.reshape((-1, m, w3))
    per_seq = min(nbk, m // SW_BLOCK)
    nsq = max(1, nbk // per_seq)
    nbk = per_seq
    rows = SW_BLOCK * nbk
    cur = lambda off: pl.BlockSpec((nsq, rows, DIL_W), lambda n, i: (n, i, off))
    prev = lambda off: pl.BlockSpec((nsq, SW_BLOCK, DIL_W), lambda n, i: (n, jnp.maximum(i * nbk - 1, 0), off))
    out_spec = pl.BlockSpec((nsq, rows, DIL_W), lambda n, i: (n, i, 0))
    o, l = pl.pallas_call(
        functools.partial(_sw_kernel, nbk=nbk),
        grid=(seqs.shape[0] // nsq, m // rows),
        in_specs=[cur(0), cur(1), cur(2), prev(1), prev(2)],
        out_specs=[out_spec, out_spec],
        out_shape=[jax.ShapeDtypeStruct((seqs.shape[0], m, DIL_W), F32)] * 2,
        compiler_params=_cparams(("parallel", "arbitrary")),
        name=name,
    )(seqs, seqs, seqs, seqs, seqs)
    return o.reshape(lead + (m, DIL_W)), l.reshape(lead + (m, DIL_W))


def _mlp_tail(y, mlp_refs):
    sh_ref, sc_ref, g_ref, nw_ref, w1_ref, w2_ref = mlp_refs
    h = (_rms(y, nw_ref[...]) * (1.0 + sc_ref[0]) + sh_ref[0]).astype(BF16)
    acc = jnp.zeros(y.shape, F32)
    for c in range(w1_ref.shape[1] // MLP_CHUNK):
        cols = slice(MLP_CHUNK * c, MLP_CHUNK * (c + 1))
        a = jnp.maximum(jnp.dot(h, w1_ref[:, cols], preferred_element_type=F32), 0.0)
        acc = acc + jnp.dot((a * a).astype(BF16), w2_ref[cols, :], preferred_element_type=F32)
    return y + g_ref[0] * acc


def _proj_t(ot_ref, w_rows):
    return jnp.concatenate([_tn_dot(ot_ref[0, j], w_rows) for j in range(ot_ref.shape[1])], axis=0)


def _mlp_specs(mlp_args, d):
    per_b = pl.BlockSpec((1, 1, d), lambda bi, i: (bi, 0, 0))
    const = lambda a: pl.BlockSpec(a.shape, lambda bi, i: (0, 0), pipeline_mode=pl.Buffered(1))
    sh, sc, g2, nw, w1, w2 = mlp_args
    return [per_b, per_b, per_b, pl.BlockSpec(nw.shape, lambda bi, i: (0, 0)), const(w1), const(w2)]


def _even_out_kernel(x_ref, g_ref, oat_ref, o0_ref, o1_ref, o2_ref, l0_ref, l1_ref, l2_ref, w_ref,
                     *rest):
    *mlp_refs, y_ref, tscr = rest
    tm = x_ref.shape[1]
    ncol = DIL_W // LANES

    def token_major(ref, slot):
        r = ref.shape[1]
        for c in range(r):
            for j in range(ncol):
                tscr[slot * ncol + j, pl.ds(c, tm // r, stride=r), :] = ref[0, c, :, LANES * j:LANES * (j + 1)]
        return jnp.concatenate([tscr[slot * ncol + j] for j in range(ncol)], axis=1)

    o0, l0 = o0_ref[0], l0_ref[0]
    o1, l1 = token_major(o1_ref, 0), token_major(l1_ref, 1)
    o2, l2 = token_major(o2_ref, 2), token_major(l2_ref, 3)
    top = jnp.maximum(jnp.maximum(l0, l1), l2)
    w0, w1, w2 = jnp.exp2(l0 - top), jnp.exp2(l1 - top), jnp.exp2(l2 - top)
    ob = (w0 * o0 + w1 * o1 + w2 * o2) / (w0 + w1 + w2)
    na = oat_ref.shape[2]
    y = _proj_t(oat_ref, w_ref[:na, :]) + jnp.dot(ob.astype(BF16), w_ref[na:, :], preferred_element_type=F32)
    y_ref[0] = _mlp_tail(x_ref[0] + g_ref[0] * y, mlp_refs)


def _t_spec(a, tm):
    return pl.BlockSpec((1, tm // a.shape[-1]) + a.shape[2:], lambda bi, i: (bi, i, 0, 0))


def _even_out(x, g1, oat, o_dil, l_dil, w_out, mlp_args, tm):
    b, s, d = x.shape
    tok = lambda w: pl.BlockSpec((1, tm, w), lambda bi, i: (bi, i, 0))

    def dil_spec(a):
        if a.ndim == 3:
            return tok(DIL_W)
        r = a.shape[1]
        return pl.BlockSpec((1, r, tm // r, DIL_W), lambda bi, i: (bi, 0, i, 0))

    return pl.pallas_call(
        _even_out_kernel,
        grid=(b, s // tm),
        in_specs=[tok(d), pl.BlockSpec((1, 1, d), lambda bi, i: (bi, 0, 0)), _t_spec(oat, tm)]
                 + [dil_spec(a) for a in (*o_dil, *l_dil)]
                 + [pl.BlockSpec(w_out.shape, lambda bi, i: (0, 0), pipeline_mode=pl.Buffered(1))]
                 + _mlp_specs(mlp_args, d),
        out_specs=tok(d),
        out_shape=jax.ShapeDtypeStruct((b, s, d), F32),
        scratch_shapes=[pltpu.VMEM((4 * DIL_W // LANES, tm, LANES), F32)],
        compiler_params=_cparams(("parallel", "arbitrary")),
        name="even_out_mlp",
    )(x, g1, oat, *o_dil, *l_dil, w_out, *mlp_args)


def _odd_out_kernel(x_ref, g_ref, oct_ref, odt_ref, w_ref, *rest):
    *mlp_refs, y_ref = rest
    nc = oct_ref.shape[2]
    y = _proj_t(oct_ref, w_ref[:nc, :]) + _proj_t(odt_ref, w_ref[nc:, :])
    y_ref[0] = _mlp_tail(x_ref[0] + g_ref[0] * y, mlp_refs)


def _odd_out(x, g1, oct, odt, w_out, mlp_args, tm):
    b, s, d = x.shape
    tok = lambda w: pl.BlockSpec((1, tm, w), lambda bi, i: (bi, i, 0))
    return pl.pallas_call(
        _odd_out_kernel,
        grid=(b, s // tm),
        in_specs=[tok(d), pl.BlockSpec((1, 1, d), lambda bi, i: (bi, 0, 0)), _t_spec(oct, tm), _t_spec(odt, tm),
                  pl.BlockSpec(w_out.shape, lambda bi, i: (0, 0), pipeline_mode=pl.Buffered(1))]
                 + _mlp_specs(mlp_args, d),
        out_specs=tok(d),
        out_shape=jax.ShapeDtypeStruct((b, s, d), F32),
        compiler_params=_cparams(("parallel", "arbitrary")),
        name="odd_out_mlp",
    )(x, g1, oct, odt, w_out, *mlp_args)


def _rope_tables(positions):
    pos = positions.astype(F32)[:, :, None]
    half = HEAD_DIM // 2
    inv = ROPE_THETA ** (-jnp.arange(half, dtype=F32) / half)
    ang = pos * inv[_freq_order()]
    zeros = jnp.zeros(ang.shape[:2] + (LANES - 2 * half,), F32)
    return jnp.concatenate([jnp.cos(ang), jnp.sin(ang), zeros], axis=-1)


def _freq_order():
    half = HEAD_DIM // 2
    return np.concatenate([np.arange(0, half, 2), np.arange(1, half, 2)])


def _head_perm(n):
    within = np.concatenate([_freq_order(), HEAD_DIM // 2 + _freq_order()])
    idx = np.arange(n)
    return idx // HEAD_DIM * HEAD_DIM + within[idx % HEAD_DIM]


def _partner(n, width):
    idx = np.arange(n)
    return np.where(idx % width < width // 2, idx + width // 2, idx - width // 2)


def _gather_cols(w, cols):
    w = jnp.pad(w.astype(BF16), ((0, 0), (0, 1)))
    return jnp.take(w, jnp.asarray(cols, jnp.int32), axis=1)


def _even_weights(w_in, w_uq, w_ukv, qn, kn):
    o2 = MLA_Q_RANK + MLA_KV_RANK
    o3 = o2 + MLA_ROPE
    p_rope = _partner(MLA_ROPE, MLA_ROPE)
    tail = LANES - MLA_QK
    hp, partner = _head_perm(DIL_W), _partner(DIL_W, HEAD_DIM)
    ngroup = len(DIL_CONFIGS)

    zero = w_in.shape[1]
    kr = np.full(LANES, zero)
    kr_rot = kr.copy()
    kr[MLA_NOPE:MLA_QK] = o2 + np.arange(MLA_ROPE)
    kr_rot[MLA_NOPE:MLA_QK] = o2 + p_rope
    dil = lambda t, g, cols: o3 + (t * ngroup + g) * DIL_W + cols
    main = [dil(t, g, hp if t < 2 else np.arange(DIL_W)) for g in range(ngroup) for t in range(3)]
    rot = [dil(t, g, hp[partner]) for g in range(ngroup) for t in range(2)]
    win = _gather_cols(w_in, np.concatenate([np.arange(o2), kr, kr_rot] + main + rot))

    nqk = MLA_HEADS * MLA_QK
    head = np.full((MLA_HEADS, LANES), nqk)
    head[:, :MLA_QK] = np.arange(nqk).reshape(MLA_HEADS, MLA_QK)
    head_rot = head.copy()
    head_rot[:, MLA_NOPE:MLA_QK] = head[:, MLA_NOPE:MLA_QK][:, p_rope]
    wuq = _gather_cols(w_uq.reshape(MLA_Q_RANK, nqk), np.concatenate([head.ravel(), head_rot.ravel()]))

    nkv = w_ukv.shape[1] * w_ukv.shape[2]
    kcols = np.full((MLA_HEADS, LANES), nkv)
    kcols[:, :MLA_NOPE] = (np.arange(MLA_HEADS) * w_ukv.shape[2])[:, None] + np.arange(MLA_NOPE)
    wuk = _gather_cols(w_ukv.reshape(MLA_KV_RANK, nkv), kcols.ravel())
    wvt = w_ukv[:, :, MLA_NOPE:].reshape(MLA_KV_RANK, -1).T.astype(BF16)

    def gains(g):
        rot = jnp.concatenate([g[:MLA_NOPE], g[MLA_NOPE:][p_rope]])
        return jnp.stack([jnp.pad(g, (0, tail)), jnp.pad(rot, (0, tail))]).astype(F32)

    return win, wuq, wuk, wvt, gains(qn), gains(kn)


def _odd_weights(w_in):
    nqk = DIFF_HEADS * LANES
    nm = MOBA_HEADS * HEAD_DIM
    main = np.concatenate([np.arange(2 * nqk), 3 * nqk + np.arange(2 * nm)])
    main = main[_head_perm(main.size)]
    win = _gather_cols(w_in, np.concatenate([main, main[_partner(main.size, HEAD_DIM)]]))
    values = np.concatenate([2 * nqk + np.arange(nqk), 3 * nqk + 2 * nm + np.arange(nm)])
    return win, _gather_cols(w_in, values).T


def _gains64(g):
    g = g[_head_perm(HEAD_DIM)]
    rot = g[_partner(HEAD_DIM, HEAD_DIM)]
    return jnp.stack([jnp.concatenate([g, g]), jnp.concatenate([rot, rot])]).astype(F32)


def kernel(x, c, positions, ada_w, ada_b, norm_mix, norm_mlp, mlp_w1, mlp_w2, even_w_in, even_w_out, mla_q_lat_norm, mla_kv_lat_norm, mla_w_uq, mla_w_ukv, mla_q_norm, mla_k_norm, dil_q_norm, dil_k_norm, odd_w_in, odd_w_out, diff_q_norm, diff_k_norm, diff_lambda, diff_subln, moba_q_norm, moba_k_norm):
    b, s, d = x.shape
    depth = ada_w.shape[0]
    t_attn, tm_in, tm_out = ATTN_BLOCK, TOKEN_TILE, TOKEN_TILE

    mod = _adaln(c, ada_w, ada_b)
    rope = _rope_tables(positions)

    for layer in range(depth):
        sh1, sc1, g1, sh2, sc2, g2 = [mod[layer, :, d * t:d * (t + 1)].reshape(b, 1, d) for t in range(6)]
        mlp_args = (sh2, sc2, g2, _row(norm_mlp[layer]), mlp_w1[layer].astype(BF16), mlp_w2[layer].astype(BF16))
        i = layer // 2
        if layer % 2 == 0:
            win, wuq, wuk, wvt, qn, kn = _even_weights(even_w_in[i], mla_w_uq[i], mla_w_ukv[i],
                                                       mla_q_norm[i], mla_k_norm[i])
            q, k, vt, *dils = _even_in(x, sh1, sc1, _row(norm_mix[layer]), win, wuq, wuk, wvt,
                                       _row(mla_q_lat_norm[i]), _row(mla_kv_lat_norm[i]), qn, kn,
                                       _gains64(dil_q_norm[i]), _gains64(dil_k_norm[i]), rope, tm_in, t_attn)
            o_at = _mla_attention(q, k, vt, MLA_HEADS_PER_STEP)
            o_dil, l_dil = zip(*[_sliding_window(dg, SW_BLOCKS_PER_STEP, f"sliding_window_g{g}")
                                 for g, dg in enumerate(dils)])
            x = _even_out(x, g1, o_at, o_dil, l_dil, even_w_out[i].astype(BF16), mlp_args, tm_out)
        else:
            lam_init = 0.8 - 0.6 * math.exp(-0.3 * layer)
            win, wvt = _odd_weights(odd_w_in[i])
            qc, kc, qm, qmf, km, vct, vmt, kmean = _odd_in(
                x, sh1, sc1, _row(norm_mix[layer]), win, wvt,
                _gains64(diff_q_norm[i]), _gains64(diff_k_norm[i]), _gains64(moba_q_norm[i]),
                _gains64(moba_k_norm[i]),
                rope, tm_in, t_attn)
            o_ct = _diff_attention(qc, kc, vct, diff_lambda[i].astype(F32),
                                   diff_subln[i].reshape(-1, 1).astype(F32), lam_init, DIFF_HEADS_PER_STEP)
            o_dt = _moba_attention(qm, qmf, km, vmt, kmean.reshape(b, s // MOBA_BLOCK, kmean.shape[-1]),
                                   MOBA_PAIRS_PER_STEP)
            x = _odd_out(x, g1, o_ct, o_dt, odd_w_out[i].astype(BF16), mlp_args, tm_out)
    return x
```

```python
import functools
import math

import jax
import jax.numpy as jnp
import numpy as np
from jax import lax
from jax.experimental import pallas as pl
from jax.experimental.pallas import tpu as pltpu

F32 = jnp.float32
BF16 = jnp.bfloat16

LANES = 128
HEAD_DIM = 64
ROPE_THETA = 10000.0
NORM_EPS = 1e-6
NEG_INF = -1e30
LOG2E = math.log2(math.e)

MLA_HEADS = 8
MLA_Q_RANK = 384
MLA_KV_RANK = 256
MLA_NOPE = 64
MLA_ROPE = 32
MLA_QK = MLA_NOPE + MLA_ROPE
DIL_CONFIGS = ((128, 1), (512, 4), (2048, 16))
DIL_HEADS = 4
DIL_W = DIL_HEADS * HEAD_DIM
DIFF_HEADS = 4
MOBA_HEADS = 8
MOBA_BLOCK = 256
MOBA_TOPK = 3
SW_BLOCK = 128
ONES_ROWS = 16

VMEM_LIMIT = 56 * 1024 * 1024

ATTN_BLOCK = MOBA_BLOCK
TOKEN_TILE = 2 * ATTN_BLOCK
MLP_CHUNK = 1024
ADALN_TILE = 2048
SW_BLOCKS_PER_STEP = 16
MLA_HEADS_PER_STEP = 4
DIFF_HEADS_PER_STEP = 2
MOBA_PAIRS_PER_STEP = 2


def _cparams(sem):
    return pltpu.CompilerParams(dimension_semantics=sem, vmem_limit_bytes=VMEM_LIMIT)


def _nt_dot(a, b):
    return lax.dot_general(a, b, (((1,), (1,)), ((), ())), preferred_element_type=F32)


def _tn_dot(a, b):
    return lax.dot_general(a, b, (((0,), (0,)), ((), ())), preferred_element_type=F32)


def _rms(x, w):
    return x * lax.rsqrt(jnp.mean(x * x, axis=-1, keepdims=True) + NORM_EPS) * w


def _lane_iota(shape):
    return lax.broadcasted_iota(jnp.int32, shape, len(shape) - 1)


def _adaln_kernel(c_ref, w_ref, b_ref, o_ref):
    c = c_ref[...]
    cond = c * (1.0 / (1.0 + jnp.exp(-c)))
    o_ref[0] = jnp.dot(cond, w_ref[0], preferred_element_type=F32,
                       precision=lax.Precision.HIGHEST) + b_ref[0]


def _adaln(c, ada_w, ada_b):
    depth, d, n = ada_w.shape
    b = c.shape[0]
    tn = ADALN_TILE
    return pl.pallas_call(
        _adaln_kernel,
        grid=(depth, n // tn),
        in_specs=[pl.BlockSpec((b, d), lambda l, j: (0, 0)),
                  pl.BlockSpec((1, d, tn), lambda l, j: (l, 0, j)),
                  pl.BlockSpec((1, 1, tn), lambda l, j: (l, 0, j))],
        out_specs=pl.BlockSpec((1, b, tn), lambda l, j: (l, 0, j)),
        out_shape=jax.ShapeDtypeStruct((depth, b, n), F32),
        compiler_params=_cparams(("parallel", "parallel")),
        name="adaln",
    )(c, ada_w, ada_b.reshape(depth, 1, n))


def _inv_rms64(x):
    lane = _lane_iota((1, LANES))
    lo = lane < HEAD_DIM
    sq = x * x
    s_lo = jnp.sum(jnp.where(lo, sq, 0.0), axis=-1, keepdims=True)
    s_hi = jnp.sum(jnp.where(lo, 0.0, sq), axis=-1, keepdims=True)
    return lax.rsqrt(jnp.where(lo, s_lo, s_hi) * (1.0 / HEAD_DIM) + NORM_EPS)


def _rope_lanes64(tab):
    lane = _lane_iota((1, LANES))
    half = HEAD_DIM // 2
    c = jnp.where(lane < half, tab, 0.0)
    s = pltpu.roll(jnp.where((lane >= half) & (lane < 2 * half), tab, 0.0), LANES - half, 1)
    tile = lambda x: x + pltpu.roll(x, half, 1) + pltpu.roll(x, 2 * half, 1) + pltpu.roll(x, 3 * half, 1)
    return tile(c), tile(s) * jnp.where(lane % HEAD_DIM < half, -1.0, 1.0)


def _rope_lanes_mla(tab):
    lane = _lane_iota((1, LANES))
    half, hm = HEAD_DIM // 2, MLA_ROPE // 2
    c = jnp.where(lane < hm, tab, 0.0)
    s = jnp.where((lane >= half) & (lane < half + hm), tab, 0.0)
    rope = (lane >= MLA_NOPE) & (lane < MLA_QK)
    cos = jnp.where(rope, pltpu.roll(c, MLA_NOPE, 1) + pltpu.roll(c, MLA_NOPE + hm, 1), 1.0)
    sin = pltpu.roll(s, MLA_NOPE + hm - half, 1) - pltpu.roll(s, MLA_NOPE - half, 1)
    return cos, sin


def _store_t_blocks(out_ref, xt):
    t = out_ref.shape[-1]
    for j in range(out_ref.shape[1]):
        out_ref[0, j] = xt[:, t * j:t * (j + 1)]


def _store_col_t(out_ref, col, x):
    t = out_ref.shape[-1]
    for j in range(out_ref.shape[1]):
        out_ref[0, j, LANES * col:LANES * (col + 1), :] = x[t * j:t * (j + 1)].T.astype(out_ref.dtype)


def _t_blocks(b, s, rows, tm, t):
    return (pl.BlockSpec((1, tm // t, rows, t), lambda bi, i: (bi, i, 0, 0)),
            jax.ShapeDtypeStruct((b, s // t, rows, t), BF16))


def _even_in_kernel(x_ref, sh_ref, sc_ref, nw_ref, win_ref, wuq_ref, wuk_ref, wvt_ref,
                    qlat_ref, kvlat_ref, qn_ref, kn_ref, dqn_ref, dkn_ref,
                    rope_ref,
                    qt_out, k_out, vt_out, d0_out, d1_out, d2_out, dscr):
    x = x_ref[0]
    tm = x.shape[0]
    h = _rms(x, nw_ref[...]) * (1.0 + sc_ref[0]) + sh_ref[0]
    u = jnp.dot(h.astype(BF16), win_ref[...], preferred_element_type=F32)

    o1 = MLA_Q_RANK
    o2 = o1 + MLA_KV_RANK
    o_kr, o_krr, o_dil = o2, o2 + LANES, o2 + 2 * LANES
    n_dil = 3 * DIL_W * len(DIL_CONFIGS)
    o_rot = o_dil + n_dil
    nq = MLA_HEADS * LANES
    cqn = _rms(u[:, :o1], qlat_ref[...]).astype(BF16)
    qp = jnp.dot(cqn, wuq_ref[...], preferred_element_type=F32)
    ckvn = _rms(u[:, o1:o2], kvlat_ref[...]).astype(BF16)
    kvp = jnp.dot(ckvn, wuk_ref[...], preferred_element_type=F32)
    _store_t_blocks(vt_out, _nt_dot(wvt_ref[...], ckvn).astype(BF16))
    kr, kr_rot = u[:, o_kr:o_kr + LANES], u[:, o_krr:o_krr + LANES]

    cm, sm = _rope_lanes_mla(rope_ref[0])
    q_scale = MLA_QK ** -0.5 * LOG2E
    qgc, qgs = qn_ref[0:1] * cm * q_scale, qn_ref[1:2] * sm * q_scale
    kgc, kgs = kn_ref[0:1] * cm, kn_ref[1:2] * sm
    kr_term = kr_rot * kgs
    for hd in range(MLA_HEADS):
        sl = slice(LANES * hd, LANES * (hd + 1))
        qh = qp[:, sl]
        inv = lax.rsqrt(jnp.sum(qh * qh, -1, keepdims=True) * (1.0 / MLA_QK) + NORM_EPS)
        _store_col_t(qt_out, hd, inv * (qh * qgc + qp[:, nq + LANES * hd:nq + LANES * (hd + 1)] * qgs))
        kh = kvp[:, sl] + kr
        inv = lax.rsqrt(jnp.sum(kh * kh, -1, keepdims=True) * (1.0 / MLA_QK) + NORM_EPS)
        k_out[0, :, sl] = (inv * (kh * kgc + kr_term)).astype(BF16)

    c64, s64 = _rope_lanes64(rope_ref[0])
    d_scale = HEAD_DIM ** -0.5 * LOG2E
    dqc, dqs = dqn_ref[0:1] * c64 * d_scale, dqn_ref[1:2] * s64 * d_scale
    dkc, dks = dkn_ref[0:1] * c64, dkn_ref[1:2] * s64
    ncol = DIL_W // LANES
    for g, d_out in enumerate((d0_out, d1_out, d2_out)):
        _, r = DIL_CONFIGS[g]
        base = o_dil + 3 * DIL_W * g
        rbase = o_rot + 2 * DIL_W * g
        for j in range(3 * ncol):
            xc = u[:, base + LANES * j:base + LANES * (j + 1)]
            if j < 2 * ncol:
                xr = u[:, rbase + LANES * j:rbase + LANES * (j + 1)]
                gc, gs = (dqc, dqs) if j < ncol else (dkc, dks)
                xc = _inv_rms64(xc) * (xc * gc + xr * gs)
            if r == 1:
                d_out[0, :, LANES * j:LANES * (j + 1)] = xc.astype(BF16)
            else:
                dscr[j] = xc
                for c in range(r):
                    d_out[0, c, :, LANES * j:LANES * (j + 1)] = dscr[j, pl.ds(c, tm // r, stride=r), :].astype(BF16)


def _row(v):
    return v.reshape(1, -1).astype(F32)


def _even_in(x, sh, sc, nw, win, wuq, wuk, wvt, qlat, kvlat, qn, kn, dqn, dkn, rope, tm, t):
    b, s, d = x.shape
    tok = lambda w: pl.BlockSpec((1, tm, w), lambda bi, i: (bi, i, 0))
    per_b = pl.BlockSpec((1, 1, d), lambda bi, i: (bi, 0, 0))
    full = lambda a: pl.BlockSpec(a.shape, lambda bi, i: (0,) * a.ndim, pipeline_mode=pl.Buffered(1))
    vt_spec, vt_shape = _t_blocks(b, s, wvt.shape[0], tm, t)
    qt_spec, qt_shape = _t_blocks(b, s, MLA_HEADS * LANES, tm, t)
    dil_specs, dil_shapes = [], []
    for _, r in DIL_CONFIGS:
        if r == 1:
            dil_specs.append(tok(3 * DIL_W))
            dil_shapes.append(jax.ShapeDtypeStruct((b, s, 3 * DIL_W), BF16))
        else:
            dil_specs.append(pl.BlockSpec((1, r, tm // r, 3 * DIL_W), lambda bi, i: (bi, 0, i, 0)))
            dil_shapes.append(jax.ShapeDtypeStruct((b, r, s // r, 3 * DIL_W), BF16))
    return pl.pallas_call(
        _even_in_kernel,
        grid=(b, s // tm),
        in_specs=[tok(d), per_b, per_b, full(nw), full(win), full(wuq), full(wuk), full(wvt),
                  full(qlat), full(kvlat), full(qn), full(kn), full(dqn), full(dkn),
                  tok(LANES)],
        out_specs=[qt_spec, tok(MLA_HEADS * LANES), vt_spec] + dil_specs,
        out_shape=[qt_shape, jax.ShapeDtypeStruct((b, s, MLA_HEADS * LANES), BF16), vt_shape] + dil_shapes,
        scratch_shapes=[pltpu.VMEM((3 * DIL_W // LANES, tm, LANES), F32)],
        compiler_params=_cparams(("parallel", "arbitrary")),
        name="even_in_proj",
    )(x, sh, sc, nw, win, wuq, wuk, wvt, qlat, kvlat, qn, kn, dqn, dkn, rope)


def _odd_in_kernel(x_ref, sh_ref, sc_ref, nw_ref, win_ref, wvt_ref, dqn_ref, dkn_ref, mqn_ref, mkn_ref,
                   rope_ref,
                   qc_out, kc_out, qm_out, qmf_out, km_out, vct_out, vmt_out, kmean_out):
    x = x_ref[0]
    tm = x.shape[0]
    h = (_rms(x, nw_ref[...]) * (1.0 + sc_ref[0]) + sh_ref[0]).astype(BF16)
    u = jnp.dot(h, win_ref[...], preferred_element_type=F32)
    vt = _nt_dot(wvt_ref[...], h).astype(BF16)
    nvc = vct_out.shape[2]
    _store_t_blocks(vct_out, vt[:nvc])
    _store_t_blocks(vmt_out, vt[nvc:])
    c64, s64 = _rope_lanes64(rope_ref[0])
    scale = HEAD_DIM ** -0.5 * LOG2E
    nqk = DIFF_HEADS * LANES
    nm = MOBA_HEADS * HEAD_DIM
    rot = 2 * nqk + 2 * nm

    def tables(gain_ref, mult):
        return gain_ref[0:1] * c64 * mult, gain_ref[1:2] * s64 * mult

    def roped(off, col, tab):
        a = off + LANES * col
        xc, xr = u[:, a:a + LANES], u[:, rot + a:rot + a + LANES]
        return _inv_rms64(xc) * (xc * tab[0] + xr * tab[1])

    t_qc, t_kc, t_qm, t_km = tables(dqn_ref, scale), tables(dkn_ref, 1.0), tables(mqn_ref, 1.0), tables(mkn_ref, 1.0)
    for col in range(nqk // LANES):
        sl = slice(LANES * col, LANES * (col + 1))
        qc_out[0, :, sl] = roped(0, col, t_qc).astype(BF16)
        kc_out[0, :, sl] = roped(nqk, col, t_kc).astype(BF16)

    base = 2 * nqk
    for col in range(nm // LANES):
        sl = slice(LANES * col, LANES * (col + 1))
        qm = roped(base, col, t_qm)
        qmf_out[0, :, sl] = qm
        qm_out[0, :, sl] = (qm * scale).astype(BF16)
        km = roped(base + nm, col, t_km)
        km_out[0, :, sl] = km.astype(BF16)
        for blk in range(tm // MOBA_BLOCK):
            rows = slice(MOBA_BLOCK * blk, MOBA_BLOCK * (blk + 1))
            kmean_out[0, 0, blk:blk + 1, sl] = jnp.mean(km[rows], axis=0, keepdims=True)


def _odd_in(x, sh, sc, nw, win, wvt, dqn, dkn, mqn, mkn, rope, tm, t):
    b, s, d = x.shape
    tok = lambda w: pl.BlockSpec((1, tm, w), lambda bi, i: (bi, i, 0))
    per_b = pl.BlockSpec((1, 1, d), lambda bi, i: (bi, 0, 0))
    full = lambda a: pl.BlockSpec(a.shape, lambda bi, i: (0,) * a.ndim, pipeline_mode=pl.Buffered(1))
    w = DIFF_HEADS * LANES
    nblk = tm // MOBA_BLOCK
    shp = lambda dt: jax.ShapeDtypeStruct((b, s, w), dt)
    vt_spec, vt_shape = _t_blocks(b, s, wvt.shape[0] // 2, tm, t)
    return pl.pallas_call(
        _odd_in_kernel,
        grid=(b, s // tm),
        in_specs=[tok(d), per_b, per_b, full(nw), full(win), full(wvt), full(dqn), full(dkn), full(mqn), full(mkn),
                  tok(LANES)],
        out_specs=[tok(w)] * 5 + [vt_spec, vt_spec, pl.BlockSpec((1, 1, nblk, w), lambda bi, i: (bi, i, 0, 0))],
        out_shape=[shp(BF16), shp(BF16), shp(BF16), shp(F32), shp(BF16), vt_shape, vt_shape,
                   jax.ShapeDtypeStruct((b, s // tm, nblk, w), F32)],
        compiler_params=_cparams(("parallel", "arbitrary")),
        name="odd_in_proj",
    )(x, sh, sc, nw, win, wvt, dqn, dkn, mqn, mkn, rope)


def _qt_halves(q):
    qt = q.astype(F32).T
    row = lax.broadcasted_iota(jnp.int32, qt.shape, 0)
    return [jnp.where((row >= HEAD_DIM) == bool(half), qt, 0.0).astype(BF16) for half in range(2)]


def _flash_t(qts, k_ref, vt_ref, k_slices, v_rows, scr, i, t, past_mask=None):
    m_scr, acc_scr, s_scr, p_scr, a_scr, b_scr = scr
    n = len(qts)
    ones = jnp.ones((ONES_ROWS, t), BF16)

    def qk(blk, s):
        b0 = pl.multiple_of(blk * t, t)
        return jnp.dot(k_ref[0, pl.ds(b0, t), k_slices[s]], qts[s], preferred_element_type=F32)

    def pv(blk, s, p):
        vt = jnp.concatenate([vt_ref[0, blk, v_rows[s], :], ones], axis=0)
        return jnp.dot(vt, p, preferred_element_type=F32)

    def kept_pv(blk, s, p, is_diag):
        out = pv(blk, s, p)
        if past_mask is not None:
            out = jnp.where(jnp.logical_or(is_diag, past_mask(s, blk)), out, 0.0)
        return out

    def store_scores(slot, s, st):
        s_scr[slot, s] = st
        b_scr[slot, s] = jnp.max(st, axis=0, keepdims=True)

    krow = lax.broadcasted_iota(jnp.int32, (t, t), 0)
    qcol = lax.broadcasted_iota(jnp.int32, (t, t), 1)
    causal = krow <= qcol
    last_past = jnp.maximum(i - 1, 0)
    for s in range(n):
        st = jnp.where(causal, qk(i, s), NEG_INF)
        m = jnp.max(st, axis=0, keepdims=True)
        m_scr[s] = m
        p_scr[0, s] = jnp.exp2(st - m).astype(BF16)
        p_scr[1, s] = jnp.zeros((t, t), BF16)
        a_scr[0, s] = jnp.ones_like(m)
        a_scr[1, s] = jnp.ones_like(m)
        acc_scr[s] = jnp.zeros(acc_scr.shape[1:], F32)
        store_scores(0, s, qk(0, s))
        store_scores(1, s, qk(jnp.minimum(1, last_past), s))

    def softmax_stage(slot, valid, rs, ws, blk):
        for s in range(n):
            cmax = b_scr[2 * rs + slot, s]
            if past_mask is not None:
                cmax = jnp.where(past_mask(s, blk), cmax, NEG_INF)
            if valid is not None:
                cmax = jnp.where(valid, cmax, NEG_INF)
            m_prev = m_scr[s]
            m_new = jnp.maximum(m_prev, cmax)
            a_scr[2 * ws + slot, s] = jnp.exp2(m_prev - m_new)
            p_scr[2 * ws + slot, s] = jnp.exp2(s_scr[2 * rs + slot, s] - m_new).astype(BF16)
            m_scr[s] = m_new

    def body(kk, rs, ws):
        first = kk == 0
        pa_blk = jnp.where(first, i, 2 * kk - 2)
        pb_blk = jnp.where(first, i, 2 * kk - 1)
        p_prev = [[p_scr[2 * rs + sl, s] for s in range(n)] for sl in range(2)]
        a_prev = [[a_scr[2 * rs + sl, s] for s in range(n)] for sl in range(2)]
        s_next = [[qk(jnp.minimum(2 * kk + 2 + sl, last_past), s) for s in range(n)] for sl in range(2)]
        pvs = [[kept_pv(blk, s, p_prev[sl][s], first) for s in range(n)] for sl, blk in enumerate((pa_blk, pb_blk))]
        softmax_stage(0, None, rs, ws, 2 * kk)
        softmax_stage(1, 2 * kk + 1 < i, rs, ws, jnp.minimum(2 * kk + 1, last_past))
        for sl in range(2):
            for s in range(n):
                store_scores(2 * ws + sl, s, s_next[sl][s])
        for s in range(n):
            acc_scr[s] = a_prev[1][s] * (a_prev[0][s] * acc_scr[s] + pvs[0][s]) + pvs[1][s]

    def two_bodies(kp, carry):
        body(2 * kp, 0, 1)
        body(2 * kp + 1, 1, 0)
        return carry

    nbody = (i + 1) // 2
    lax.fori_loop(0, nbody // 2, two_bodies, 0)

    @pl.when(nbody % 2 == 1)
    def _():
        body(nbody - 1, 0, 1)

    fs = nbody % 2
    kl = jnp.maximum(nbody - 1, 0)
    none = i == 0
    fa_blk = jnp.where(none, i, 2 * kl)
    fb_blk = jnp.where(none, i, jnp.minimum(2 * kl + 1, last_past))
    fb_valid = jnp.logical_or(none, i % 2 == 0)
    for s in range(n):
        pv_b = jnp.where(fb_valid, kept_pv(fb_blk, s, p_scr[2 * fs + 1, s], none), 0.0)
        acc_scr[s] = (a_scr[2 * fs + 1, s] * (a_scr[2 * fs, s] * acc_scr[s]
                                              + kept_pv(fa_blk, s, p_scr[2 * fs, s], none)) + pv_b)


def _flash_scratch(n, nv, t):
    return [pltpu.VMEM((n, 1, t), F32), pltpu.VMEM((n, nv + ONES_ROWS, t), F32), pltpu.VMEM((4, n, t, t), F32),
            pltpu.VMEM((4, n, t, t), BF16), pltpu.VMEM((4, n, 1, t), F32), pltpu.VMEM((4, n, 1, t), F32)]


def _normalised(acc_scr, s, nv):
    return acc_scr[s, :nv, :] / acc_scr[s, nv:nv + 1, :]


def _half_mask(x, half):
    lane = _lane_iota((1, LANES))
    return jnp.where((lane >= HEAD_DIM) == bool(half), x, jnp.zeros_like(x))


def _mla_kernel(qt_ref, k_ref, vt_ref, ot_ref, *scr, t, nh):
    i = pl.program_id(2)
    ks = [slice(LANES * s, LANES * (s + 1)) for s in range(nh)]
    qs = [qt_ref[0, 0, ks[s], :] for s in range(nh)]
    vr = [slice(HEAD_DIM * s, HEAD_DIM * (s + 1)) for s in range(nh)]
    _flash_t(qs, k_ref, vt_ref, ks, vr, scr, i, t)
    for s in range(nh):
        ot_ref[0, 0, vr[s], :] = _normalised(scr[1], s, HEAD_DIM).astype(ot_ref.dtype)


def _mla_attention(qt, k, vt, nh):
    b, s, _ = k.shape
    _, nb, nv, t = vt.shape
    return pl.pallas_call(
        functools.partial(_mla_kernel, t=t, nh=nh),
        grid=(b, MLA_HEADS // nh, nb),
        in_specs=[pl.BlockSpec((1, 1, nh * LANES, t), lambda bi, p, i: (bi, i, p, 0)),
                  pl.BlockSpec((1, s, nh * LANES), lambda bi, p, i: (bi, 0, p)),
                  pl.BlockSpec((1, nb, nh * HEAD_DIM, t), lambda bi, p, i: (bi, 0, p, 0))],
        out_specs=pl.BlockSpec((1, 1, nh * HEAD_DIM, t), lambda bi, p, i: (bi, i, p, 0)),
        out_shape=jax.ShapeDtypeStruct((b, nb, nv, t), BF16),
        scratch_shapes=_flash_scratch(nh, HEAD_DIM, t),
        compiler_params=_cparams(("parallel", "parallel", "arbitrary")),
        name="mla_attention",
    )(qt, k, vt)


def _diff_kernel(q_ref, k_ref, vt_ref, lam_ref, sub_ref, ot_ref, *scr, t, nh, lam_init):
    i = pl.program_id(2)
    qs, ks, vr = [], [], []
    for hd in range(nh):
        cols = slice(LANES * hd, LANES * (hd + 1))
        qs += _qt_halves(q_ref[0, :, cols])
        ks += [cols, cols]
        vr += [cols, cols]
    _flash_t(qs, k_ref, vt_ref, ks, vr, scr, i, t)
    lv = lam_ref[...]
    lam = (jnp.exp(jnp.sum(lv[0:1] * lv[1:2], keepdims=True))
           - jnp.exp(jnp.sum(lv[2:3] * lv[3:4], keepdims=True)) + lam_init)
    for hd in range(nh):
        o = _normalised(scr[1], 2 * hd, LANES) - lam * _normalised(scr[1], 2 * hd + 1, LANES)
        o = o * lax.rsqrt(jnp.mean(o * o, axis=0, keepdims=True) + NORM_EPS) * sub_ref[...]
        ot_ref[0, 0, LANES * hd:LANES * (hd + 1), :] = (o * (1.0 - lam_init)).astype(ot_ref.dtype)


def _diff_attention(q, k, vt, lam_rows, subln_col, lam_init, nh):
    b, s, w = q.shape
    _, nb, _, t = vt.shape
    return pl.pallas_call(
        functools.partial(_diff_kernel, t=t, nh=nh, lam_init=lam_init),
        grid=(b, DIFF_HEADS // nh, nb),
        in_specs=[pl.BlockSpec((1, t, nh * LANES), lambda bi, p, i: (bi, i, p)),
                  pl.BlockSpec((1, s, nh * LANES), lambda bi, p, i: (bi, 0, p)),
                  pl.BlockSpec((1, nb, nh * LANES, t), lambda bi, p, i: (bi, 0, p, 0)),
                  pl.BlockSpec(lam_rows.shape, lambda bi, p, i: (0, 0)),
                  pl.BlockSpec(subln_col.shape, lambda bi, p, i: (0, 0))],
        out_specs=pl.BlockSpec((1, 1, nh * LANES, t), lambda bi, p, i: (bi, i, p, 0)),
        out_shape=jax.ShapeDtypeStruct((b, nb, w, t), BF16),
        scratch_shapes=_flash_scratch(2 * nh, LANES, t),
        compiler_params=_cparams(("parallel", "parallel", "arbitrary")),
        name="diff_attention",
    )(q, k, vt, lam_rows, subln_col)


def _moba_kernel(q_ref, qf_ref, k_ref, vt_ref, kmean_ref, ot_ref, sel_scr, *scr, t, npair):
    i = pl.program_id(2)
    nb = kmean_ref.shape[1]
    brow = lax.broadcasted_iota(jnp.int32, (nb, t), 0).astype(F32)
    past = brow < i.astype(F32)
    qs, ks, vr = [], [], []
    for pr in range(npair):
        cols = slice(LANES * pr, LANES * (pr + 1))
        qf, kmean = qf_ref[0, :, cols], kmean_ref[0, :, cols]
        qs += _qt_halves(q_ref[0, :, cols])
        for half in range(2):
            s = 2 * pr + half
            gate = lax.dot_general(kmean, _half_mask(qf, half), (((1,), (1,)), ((), ())),
                                   preferred_element_type=F32, precision=lax.Precision.HIGHEST)
            gate = jnp.where(past, gate, NEG_INF)
            sel = jnp.zeros((nb, t), F32)
            for _ in range(MOBA_TOPK):
                top = jnp.max(gate, axis=0, keepdims=True)
                first = jnp.min(jnp.where(gate == top, brow, float(nb)), axis=0, keepdims=True)
                pick = brow == first
                sel = jnp.where(pick, 1.0, sel)
                gate = jnp.where(pick, 2 * NEG_INF, gate)
            sel_scr[s] = jnp.where(past, sel, 0.0)
            ks.append(cols)
            vr.append(slice(HEAD_DIM * s, HEAD_DIM * (s + 1)))
    _flash_t(qs, k_ref, vt_ref, ks, vr, scr, i, t,
             past_mask=lambda s, jj: sel_scr[s, pl.ds(jj, 1), :] > 0.5)
    for s in range(2 * npair):
        ot_ref[0, 0, vr[s], :] = _normalised(scr[1], s, HEAD_DIM).astype(ot_ref.dtype)


def _moba_attention(q, qf, k, vt, kmean, npair):
    b, s, w = q.shape
    _, nb, _, t = vt.shape
    nkm = kmean.shape[1]
    lanes = npair * LANES
    return pl.pallas_call(
        functools.partial(_moba_kernel, t=t, npair=npair),
        grid=(b, w // lanes, nb),
        in_specs=[pl.BlockSpec((1, t, lanes), lambda bi, p, i: (bi, i, p)),
                  pl.BlockSpec((1, t, lanes), lambda bi, p, i: (bi, i, p)),
                  pl.BlockSpec((1, s, lanes), lambda bi, p, i: (bi, 0, p)),
                  pl.BlockSpec((1, nb, lanes, t), lambda bi, p, i: (bi, 0, p, 0)),
                  pl.BlockSpec((1, nkm, lanes), lambda bi, p, i: (bi, 0, p))],
        out_specs=pl.BlockSpec((1, 1, lanes, t), lambda bi, p, i: (bi, i, p, 0)),
        out_shape=jax.ShapeDtypeStruct((b, nb, w, t), BF16),
        scratch_shapes=[pltpu.VMEM((2 * npair, nkm, t), F32)] + _flash_scratch(2 * npair, HEAD_DIM, t),
        compiler_params=_cparams(("parallel", "parallel", "arbitrary")),
        name="moba_attention",
    )(q, qf, k, vt, kmean)


def _sw_kernel(q_ref, kc_ref, vc_ref, kp_ref, vp_ref, o_ref, l_ref, *, nbk):
    i = pl.program_id(1)
    blk = SW_BLOCK
    qi = lax.broadcasted_iota(jnp.int32, (blk, 2 * blk), 0)
    kj = lax.broadcasted_iota(jnp.int32, (blk, 2 * blk), 1)
    band = (kj >= qi) & (kj <= qi + blk)
    lane = _lane_iota((1, LANES))
    heads = [(pair, half) for pair in range(DIL_W // LANES) for half in range(2)]

    def scores(sq, n):
        rows = slice(blk * n, blk * (n + 1))
        q = q_ref[sq, rows, :]
        if n == 0:
            kprev, valid = kp_ref[sq], band & ((kj >= blk) | (i > 0))
        else:
            kprev, valid = kc_ref[sq, blk * (n - 1):blk * n, :], band
        kk = jnp.concatenate([kprev, kc_ref[sq, rows, :]], axis=0)
        out = []
        for pair, half in heads:
            sl = slice(LANES * pair, LANES * (pair + 1))
            out.append(jnp.where(valid, _nt_dot(_half_mask(q[:, sl], half), kk[:, sl]), NEG_INF))
        return out

    work = [(sq, n) for sq in range(q_ref.shape[0]) for n in range(nbk)]
    nxt = scores(*work[0])
    for w, (sq, n) in enumerate(work):
        cur, rows = nxt, slice(blk * n, blk * (n + 1))
        if w + 1 < len(work):
            nxt = scores(*work[w + 1])
        vprev = vp_ref[sq] if n == 0 else vc_ref[sq, blk * (n - 1):blk * n, :]
        vv = jnp.concatenate([vprev, vc_ref[sq, rows, :]], axis=0)
        outs, lses = [], []
        for (pair, half), sc in zip(heads, cur):
            m = jnp.max(sc, axis=-1, keepdims=True)
            p = jnp.exp2(sc - m)
            l = jnp.sum(p, axis=-1, keepdims=True)
            sl = slice(LANES * pair, LANES * (pair + 1))
            outs.append(jnp.dot(p.astype(BF16), vv[:, sl], preferred_element_type=F32) / l)
            lses.append(m + jnp.log2(l))
        for pair in range(DIL_W // LANES):
            sl = slice(LANES * pair, LANES * (pair + 1))
            o_ref[sq, rows, sl] = jnp.where(lane < HEAD_DIM, outs[2 * pair], outs[2 * pair + 1])
            l_ref[sq, rows, sl] = jnp.where(lane < HEAD_DIM, lses[2 * pair], lses[2 * pair + 1])


def _sliding_window(dil, nbk, name):
    lead, (m, w3) = dil.shape[:-2], dil.shape[-2:]
    seqs = dil.reshape((-1, m, w3))
    per_seq = min(nbk, m // SW_BLOCK)
    nsq = max(1, nbk // per_seq)
    nbk = per_seq
    rows = SW_BLOCK * nbk
    cur = lambda off: pl.BlockSpec((nsq, rows, DIL_W), lambda n, i: (n, i, off))
    prev = lambda off: pl.BlockSpec((nsq, SW_BLOCK, DIL_W), lambda n, i: (n, jnp.maximum(i * nbk - 1, 0), off))
    out_spec = pl.BlockSpec((nsq, rows, DIL_W), lambda n, i: (n, i, 0))
    o, l = pl.pallas_call(
        functools.partial(_sw_kernel, nbk=nbk),
        grid=(seqs.shape[0] // nsq, m // rows),
        in_specs=[cur(0), cur(1), cur(2), prev(1), prev(2)],
        out_specs=[out_spec, out_spec],
        out_shape=[jax.ShapeDtypeStruct((seqs.shape[0], m, DIL_W), F32)] * 2,
        compiler_params=_cparams(("parallel", "arbitrary")),
        name=name,
    )(seqs, seqs, seqs, seqs, seqs)
    return o.reshape(lead + (m, DIL_W)), l.reshape(lead + (m, DIL_W))


def _mlp_tail(y, mlp_refs):
    sh_ref, sc_ref, g_ref, nw_ref, w1_ref, w2_ref = mlp_refs
    h = (_rms(y, nw_ref[...]) * (1.0 + sc_ref[0]) + sh_ref[0]).astype(BF16)
    acc = jnp.zeros(y.shape, F32)
    for c in range(w1_ref.shape[1] // MLP_CHUNK):
        cols = slice(MLP_CHUNK * c, MLP_CHUNK * (c + 1))
        a = jnp.maximum(jnp.dot(h, w1_ref[:, cols], preferred_element_type=F32), 0.0)
        acc = acc + jnp.dot((a * a).astype(BF16), w2_ref[cols, :], preferred_element_type=F32)
    return y + g_ref[0] * acc


def _proj_t(ot_ref, w_rows):
    return jnp.concatenate([_tn_dot(ot_ref[0, j], w_rows) for j in range(ot_ref.shape[1])], axis=0)


def _mlp_specs(mlp_args, d):
    per_b = pl.BlockSpec((1, 1, d), lambda bi, i: (bi, 0, 0))
    const = lambda a: pl.BlockSpec(a.shape, lambda bi, i: (0, 0), pipeline_mode=pl.Buffered(1))
    sh, sc, g2, nw, w1, w2 = mlp_args
    return [per_b, per_b, per_b, pl.BlockSpec(nw.shape, lambda bi, i: (0, 0)), const(w1), const(w2)]


def _even_out_kernel(x_ref, g_ref, oat_ref, o0_ref, o1_ref, o2_ref, l0_ref, l1_ref, l2_ref, w_ref,
                     *rest):
    *mlp_refs, y_ref, tscr = rest
    tm = x_ref.shape[1]
    ncol = DIL_W // LANES

    def token_major(ref, slot):
        r = ref.shape[1]
        for c in range(r):
            for j in range(ncol):
                tscr[slot * ncol + j, pl.ds(c, tm // r, stride=r), :] = ref[0, c, :, LANES * j:LANES * (j + 1)]
        return jnp.concatenate([tscr[slot * ncol + j] for j in range(ncol)], axis=1)

    o0, l0 = o0_ref[0], l0_ref[0]
    o1, l1 = token_major(o1_ref, 0), token_major(l1_ref, 1)
    o2, l2 = token_major(o2_ref, 2), token_major(l2_ref, 3)
    top = jnp.maximum(jnp.maximum(l0, l1), l2)
    w0, w1, w2 = jnp.exp2(l0 - top), jnp.exp2(l1 - top), jnp.exp2(l2 - top)
    ob = (w0 * o0 + w1 * o1 + w2 * o2) / (w0 + w1 + w2)
    na = oat_ref.shape[2]
    y = _proj_t(oat_ref, w_ref[:na, :]) + jnp.dot(ob.astype(BF16), w_ref[na:, :], preferred_element_type=F32)
    y_ref[0] = _mlp_tail(x_ref[0] + g_ref[0] * y, mlp_refs)


def _t_spec(a, tm):
    return pl.BlockSpec((1, tm // a.shape[-1]) + a.shape[2:], lambda bi, i: (bi, i, 0, 0))


def _even_out(x, g1, oat, o_dil, l_dil, w_out, mlp_args, tm):
    b, s, d = x.shape
    tok = lambda w: pl.BlockSpec((1, tm, w), lambda bi, i: (bi, i, 0))

    def dil_spec(a):
        if a.ndim == 3:
            return tok(DIL_W)
        r = a.shape[1]
        return pl.BlockSpec((1, r, tm // r, DIL_W), lambda bi, i: (bi, 0, i, 0))

    return pl.pallas_call(
        _even_out_kernel,
        grid=(b, s // tm),
        in_specs=[tok(d), pl.BlockSpec((1, 1, d), lambda bi, i: (bi, 0, 0)), _t_spec(oat, tm)]
                 + [dil_spec(a) for a in (*o_dil, *l_dil)]
                 + [pl.BlockSpec(w_out.shape, lambda bi, i: (0, 0), pipeline_mode=pl.Buffered(1))]
                 + _mlp_specs(mlp_args, d),
        out_specs=tok(d),
        out_shape=jax.ShapeDtypeStruct((b, s, d), F32),
        scratch_shapes=[pltpu.VMEM((4 * DIL_W // LANES, tm, LANES), F32)],
        compiler_params=_cparams(("parallel", "arbitrary")),
        name="even_out_mlp",
    )(x, g1, oat, *o_dil, *l_dil, w_out, *mlp_args)


def _odd_out_kernel(x_ref, g_ref, oct_ref, odt_ref, w_ref, *rest):
    *mlp_refs, y_ref = rest
    nc = oct_ref.shape[2]
    y = _proj_t(oct_ref, w_ref[:nc, :]) + _proj_t(odt_ref, w_ref[nc:, :])
    y_ref[0] = _mlp_tail(x_ref[0] + g_ref[0] * y, mlp_refs)


def _odd_out(x, g1, oct, odt, w_out, mlp_args, tm):
    b, s, d = x.shape
    tok = lambda w: pl.BlockSpec((1, tm, w), lambda bi, i: (bi, i, 0))
    return pl.pallas_call(
        _odd_out_kernel,
        grid=(b, s // tm),
        in_specs=[tok(d), pl.BlockSpec((1, 1, d), lambda bi, i: (bi, 0, 0)), _t_spec(oct, tm), _t_spec(odt, tm),
                  pl.BlockSpec(w_out.shape, lambda bi, i: (0, 0), pipeline_mode=pl.Buffered(1))]
                 + _mlp_specs(mlp_args, d),
        out_specs=tok(d),
        out_shape=jax.ShapeDtypeStruct((b, s, d), F32),
        compiler_params=_cparams(("parallel", "arbitrary")),
        name="odd_out_mlp",
    )(x, g1, oct, odt, w_out, *mlp_args)


def _rope_tables(positions):
    pos = positions.astype(F32)[:, :, None]
    half = HEAD_DIM // 2
    inv = ROPE_THETA ** (-jnp.arange(half, dtype=F32) / half)
    ang = pos * inv[_freq_order()]
    zeros = jnp.zeros(ang.shape[:2] + (LANES - 2 * half,), F32)
    return jnp.concatenate([jnp.cos(ang), jnp.sin(ang), zeros], axis=-1)


def _freq_order():
    half = HEAD_DIM // 2
    return np.concatenate([np.arange(0, half, 2), np.arange(1, half, 2)])


def _head_perm(n):
    within = np.concatenate([_freq_order(), HEAD_DIM // 2 + _freq_order()])
    idx = np.arange(n)
    return idx // HEAD_DIM * HEAD_DIM + within[idx % HEAD_DIM]


def _partner(n, width):
    idx = np.arange(n)
    return np.where(idx % width < width // 2, idx + width // 2, idx - width // 2)


def _gather_cols(w, cols):
    w = jnp.pad(w.astype(BF16), ((0, 0), (0, 1)))
    return jnp.take(w, jnp.asarray(cols, jnp.int32), axis=1)


def _even_weights(w_in, w_uq, w_ukv, qn, kn):
    o2 = MLA_Q_RANK + MLA_KV_RANK
    o3 = o2 + MLA_ROPE
    p_rope = _partner(MLA_ROPE, MLA_ROPE)
    tail = LANES - MLA_QK
    hp, partner = _head_perm(DIL_W), _partner(DIL_W, HEAD_DIM)
    ngroup = len(DIL_CONFIGS)

    zero = w_in.shape[1]
    kr = np.full(LANES, zero)
    kr_rot = kr.copy()
    kr[MLA_NOPE:MLA_QK] = o2 + np.arange(MLA_ROPE)
    kr_rot[MLA_NOPE:MLA_QK] = o2 + p_rope
    dil = lambda t, g, cols: o3 + (t * ngroup + g) * DIL_W + cols
    main = [dil(t, g, hp if t < 2 else np.arange(DIL_W)) for g in range(ngroup) for t in range(3)]
    rot = [dil(t, g, hp[partner]) for g in range(ngroup) for t in range(2)]
    win = _gather_cols(w_in, np.concatenate([np.arange(o2), kr, kr_rot] + main + rot))

    nqk = MLA_HEADS * MLA_QK
    head = np.full((MLA_HEADS, LANES), nqk)
    head[:, :MLA_QK] = np.arange(nqk).reshape(MLA_HEADS, MLA_QK)
    head_rot = head.copy()
    head_rot[:, MLA_NOPE:MLA_QK] = head[:, MLA_NOPE:MLA_QK][:, p_rope]
    wuq = _gather_cols(w_uq.reshape(MLA_Q_RANK, nqk), np.concatenate([head.ravel(), head_rot.ravel()]))

    nkv = w_ukv.shape[1] * w_ukv.shape[2]
    kcols = np.full((MLA_HEADS, LANES), nkv)
    kcols[:, :MLA_NOPE] = (np.arange(MLA_HEADS) * w_ukv.shape[2])[:, None] + np.arange(MLA_NOPE)
    wuk = _gather_cols(w_ukv.reshape(MLA_KV_RANK, nkv), kcols.ravel())
    wvt = w_ukv[:, :, MLA_NOPE:].reshape(MLA_KV_RANK, -1).T.astype(BF16)

    def gains(g):
        rot = jnp.concatenate([g[:MLA_NOPE], g[MLA_NOPE:][p_rope]])
        return jnp.stack([jnp.pad(g, (0, tail)), jnp.pad(rot, (0, tail))]).astype(F32)

    return win, wuq, wuk, wvt, gains(qn), gains(kn)


def _odd_weights(w_in):
    nqk = DIFF_HEADS * LANES
    nm = MOBA_HEADS * HEAD_DIM
    main = np.concatenate([np.arange(2 * nqk), 3 * nqk + np.arange(2 * nm)])
    main = main[_head_perm(main.size)]
    win = _gather_cols(w_in, np.concatenate([main, main[_partner(main.size, HEAD_DIM)]]))
    values = np.concatenate([2 * nqk + np.arange(nqk), 3 * nqk + 2 * nm + np.arange(nm)])
    return win, _gather_cols(w_in, values).T


def _gains64(g):
    g = g[_head_perm(HEAD_DIM)]
    rot = g[_partner(HEAD_DIM, HEAD_DIM)]
    return jnp.stack([jnp.concatenate([g, g]), jnp.concatenate([rot, rot])]).astype(F32)


def kernel(x, c, positions, ada_w, ada_b, norm_mix, norm_mlp, mlp_w1, mlp_w2, even_w_in, even_w_out, mla_q_lat_norm, mla_kv_lat_norm, mla_w_uq, mla_w_ukv, mla_q_norm, mla_k_norm, dil_q_norm, dil_k_norm, odd_w_in, odd_w_out, diff_q_norm, diff_k_norm, diff_lambda, diff_subln, moba_q_norm, moba_k_norm):
    b, s, d = x.shape
    depth = ada_w.shape[0]
    t_attn, tm_in, tm_out = ATTN_BLOCK, TOKEN_TILE, TOKEN_TILE

    mod = _adaln(c, ada_w, ada_b)
    rope = _rope_tables(positions)

    for layer in range(depth):
        sh1, sc1, g1, sh2, sc2, g2 = [mod[layer, :, d * t:d * (t + 1)].reshape(b, 1, d) for t in range(6)]
        mlp_args = (sh2, sc2, g2, _row(norm_mlp[layer]), mlp_w1[layer].astype(BF16), mlp_w2[layer].astype(BF16))
        i = layer // 2
        if layer % 2 == 0:
            win, wuq, wuk, wvt, qn, kn = _even_weights(even_w_in[i], mla_w_uq[i], mla_w_ukv[i],
                                                       mla_q_norm[i], mla_k_norm[i])
            q, k, vt, *dils = _even_in(x, sh1, sc1, _row(norm_mix[layer]), win, wuq, wuk, wvt,
                                       _row(mla_q_lat_norm[i]), _row(mla_kv_lat_norm[i]), qn, kn,
                                       _gains64(dil_q_norm[i]), _gains64(dil_k_norm[i]), rope, tm_in, t_attn)
            o_at = _mla_attention(q, k, vt, MLA_HEADS_PER_STEP)
            o_dil, l_dil = zip(*[_sliding_window(dg, SW_BLOCKS_PER_STEP, f"sliding_window_g{g}")
                                 for g, dg in enumerate(dils)])
            x = _even_out(x, g1, o_at, o_dil, l_dil, even_w_out[i].astype(BF16), mlp_args, tm_out)
        else:
            lam_init = 0.8 - 0.6 * math.exp(-0.3 * layer)
            win, wvt = _odd_weights(odd_w_in[i])
            qc, kc, qm, qmf, km, vct, vmt, kmean = _odd_in(
                x, sh1, sc1, _row(norm_mix[layer]), win, wvt,
                _gains64(diff_q_norm[i]), _gains64(diff_k_norm[i]), _gains64(moba_q_norm[i]),
                _gains64(moba_k_norm[i]),
                rope, tm_in, t_attn)
            o_ct = _diff_attention(qc, kc, vct, diff_lambda[i].astype(F32),
                                   diff_subln[i].reshape(-1, 1).astype(F32), lam_init, DIFF_HEADS_PER_STEP)
            o_dt = _moba_attention(qm, qmf, km, vmt, kmean.reshape(b, s // MOBA_BLOCK, kmean.shape[-1]),
                                   MOBA_PAIRS_PER_STEP)
            x = _odd_out(x, g1, o_ct, o_dt, odd_w_out[i].astype(BF16), mlp_args, tm_out)
    return x
```

```python
import functools
import math

import jax
import jax.numpy as jnp
import numpy as np
from jax import lax
from jax.experimental import pallas as pl
from jax.experimental.pallas import tpu as pltpu

F32 = jnp.float32
BF16 = jnp.bfloat16

LANES = 128
HEAD_DIM = 64
ROPE_THETA = 10000.0
NORM_EPS = 1e-6
NEG_INF = -1e30
LOG2E = math.log2(math.e)

MLA_HEADS = 8
MLA_Q_RANK = 384
MLA_KV_RANK = 256
MLA_NOPE = 64
MLA_ROPE = 32
MLA_QK = MLA_NOPE + MLA_ROPE
DIL_CONFIGS = ((128, 1), (512, 4), (2048, 16))
DIL_HEADS = 4
DIL_W = DIL_HEADS * HEAD_DIM
DIFF_HEADS = 4
MOBA_HEADS = 8
MOBA_BLOCK = 256
MOBA_TOPK = 3
SW_BLOCK = 128
ONES_ROWS = 16

VMEM_LIMIT = 56 * 1024 * 1024

ATTN_BLOCK = MOBA_BLOCK
TOKEN_TILE = 2 * ATTN_BLOCK
MLP_CHUNK = 1024
ADALN_TILE = 2048
SW_BLOCKS_PER_STEP = 16
MLA_HEADS_PER_STEP = 4
DIFF_HEADS_PER_STEP = 2
MOBA_PAIRS_PER_STEP = 2


def _cparams(sem):
    return pltpu.CompilerParams(dimension_semantics=sem, vmem_limit_bytes=VMEM_LIMIT)


def _nt_dot(a, b):
    return lax.dot_general(a, b, (((1,), (1,)), ((), ())), preferred_element_type=F32)


def _tn_dot(a, b):
    return lax.dot_general(a, b, (((0,), (0,)), ((), ())), preferred_element_type=F32)


def _rms(x, w):
    return x * lax.rsqrt(jnp.mean(x * x, axis=-1, keepdims=True) + NORM_EPS) * w


def _lane_iota(shape):
    return lax.broadcasted_iota(jnp.int32, shape, len(shape) - 1)


def _adaln_kernel(c_ref, w_ref, b_ref, o_ref):
    c = c_ref[...]
    cond = c * (1.0 / (1.0 + jnp.exp(-c)))
    o_ref[0] = jnp.dot(cond, w_ref[0], preferred_element_type=F32,
                       precision=lax.Precision.HIGHEST) + b_ref[0]


def _adaln(c, ada_w, ada_b):
    depth, d, n = ada_w.shape
    b = c.shape[0]
    tn = ADALN_TILE
    return pl.pallas_call(
        _adaln_kernel,
        grid=(depth, n // tn),
        in_specs=[pl.BlockSpec((b, d), lambda l, j: (0, 0)),
                  pl.BlockSpec((1, d, tn), lambda l, j: (l, 0, j)),
                  pl.BlockSpec((1, 1, tn), lambda l, j: (l, 0, j))],
        out_specs=pl.BlockSpec((1, b, tn), lambda l, j: (l, 0, j)),
        out_shape=jax.ShapeDtypeStruct((depth, b, n), F32),
        compiler_params=_cparams(("parallel", "parallel")),
        name="adaln",
    )(c, ada_w, ada_b.reshape(depth, 1, n))


def _inv_rms64(x):
    lane = _lane_iota((1, LANES))
    lo = lane < HEAD_DIM
    sq = x * x
    s_lo = jnp.sum(jnp.where(lo, sq, 0.0), axis=-1, keepdims=True)
    s_hi = jnp.sum(jnp.where(lo, 0.0, sq), axis=-1, keepdims=True)
    return lax.rsqrt(jnp.where(lo, s_lo, s_hi) * (1.0 / HEAD_DIM) + NORM_EPS)


def _rope_lanes64(tab):
    lane = _lane_iota((1, LANES))
    half = HEAD_DIM // 2
    c = jnp.where(lane < half, tab, 0.0)
    s = pltpu.roll(jnp.where((lane >= half) & (lane < 2 * half), tab, 0.0), LANES - half, 1)
    tile = lambda x: x + pltpu.roll(x, half, 1) + pltpu.roll(x, 2 * half, 1) + pltpu.roll(x, 3 * half, 1)
    return tile(c), tile(s) * jnp.where(lane % HEAD_DIM < half, -1.0, 1.0)


def _rope_lanes_mla(tab):
    lane = _lane_iota((1, LANES))
    half, hm = HEAD_DIM // 2, MLA_ROPE // 2
    c = jnp.where(lane < hm, tab, 0.0)
    s = jnp.where((lane >= half) & (lane < half + hm), tab, 0.0)
    rope = (lane >= MLA_NOPE) & (lane < MLA_QK)
    cos = jnp.where(rope, pltpu.roll(c, MLA_NOPE, 1) + pltpu.roll(c, MLA_NOPE + hm, 1), 1.0)
    sin = pltpu.roll(s, MLA_NOPE + hm - half, 1) - pltpu.roll(s, MLA_NOPE - half, 1)
    return cos, sin


def _store_t_blocks(out_ref, xt):
    t = out_ref.shape[-1]
    for j in range(out_ref.shape[1]):
        out_ref[0, j] = xt[:, t * j:t * (j + 1)]


def _store_col_t(out_ref, col, x):
    t = out_ref.shape[-1]
    for j in range(out_ref.shape[1]):
        out_ref[0, j, LANES * col:LANES * (col + 1), :] = x[t * j:t * (j + 1)].T.astype(out_ref.dtype)


def _t_blocks(b, s, rows, tm, t):
    return (pl.BlockSpec((1, tm // t, rows, t), lambda bi, i: (bi, i, 0, 0)),
            jax.ShapeDtypeStruct((b, s // t, rows, t), BF16))


def _even_in_kernel(x_ref, sh_ref, sc_ref, nw_ref, win_ref, wuq_ref, wuk_ref, wvt_ref,
                    qlat_ref, kvlat_ref, qn_ref, kn_ref, dqn_ref, dkn_ref,
                    rope_ref,
                    qt_out, k_out, vt_out, d0_out, d1_out, d2_out, dscr):
    x = x_ref[0]
    tm = x.shape[0]
    h = _rms(x, nw_ref[...]) * (1.0 + sc_ref[0]) + sh_ref[0]
    u = jnp.dot(h.astype(BF16), win_ref[...], preferred_element_type=F32)

    o1 = MLA_Q_RANK
    o2 = o1 + MLA_KV_RANK
    o_kr, o_krr, o_dil = o2, o2 + LANES, o2 + 2 * LANES
    n_dil = 3 * DIL_W * len(DIL_CONFIGS)
    o_rot = o_dil + n_dil
    nq = MLA_HEADS * LANES
    cqn = _rms(u[:, :o1], qlat_ref[...]).astype(BF16)
    qp = jnp.dot(cqn, wuq_ref[...], preferred_element_type=F32)
    ckvn = _rms(u[:, o1:o2], kvlat_ref[...]).astype(BF16)
    kvp = jnp.dot(ckvn, wuk_ref[...], preferred_element_type=F32)
    _store_t_blocks(vt_out, _nt_dot(wvt_ref[...], ckvn).astype(BF16))
    kr, kr_rot = u[:, o_kr:o_kr + LANES], u[:, o_krr:o_krr + LANES]

    cm, sm = _rope_lanes_mla(rope_ref[0])
    q_scale = MLA_QK ** -0.5 * LOG2E
    qgc, qgs = qn_ref[0:1] * cm * q_scale, qn_ref[1:2] * sm * q_scale
    kgc, kgs = kn_ref[0:1] * cm, kn_ref[1:2] * sm
    kr_term = kr_rot * kgs
    for hd in range(MLA_HEADS):
        sl = slice(LANES * hd, LANES * (hd + 1))
        qh = qp[:, sl]
        inv = lax.rsqrt(jnp.sum(qh * qh, -1, keepdims=True) * (1.0 / MLA_QK) + NORM_EPS)
        _store_col_t(qt_out, hd, inv * (qh * qgc + qp[:, nq + LANES * hd:nq + LANES * (hd + 1)] * qgs))
        kh = kvp[:, sl] + kr
        inv = lax.rsqrt(jnp.sum(kh * kh, -1, keepdims=True) * (1.0 / MLA_QK) + NORM_EPS)
        k_out[0, :, sl] = (inv * (kh * kgc + kr_term)).astype(BF16)

    c64, s64 = _rope_lanes64(rope_ref[0])
    d_scale = HEAD_DIM ** -0.5 * LOG2E
    dqc, dqs = dqn_ref[0:1] * c64 * d_scale, dqn_ref[1:2] * s64 * d_scale
    dkc, dks = dkn_ref[0:1] * c64, dkn_ref[1:2] * s64
    ncol = DIL_W // LANES
    for g, d_out in enumerate((d0_out, d1_out, d2_out)):
        _, r = DIL_CONFIGS[g]
        base = o_dil + 3 * DIL_W * g
        rbase = o_rot + 2 * DIL_W * g
        for j in range(3 * ncol):
            xc = u[:, base + LANES * j:base + LANES * (j + 1)]
            if j < 2 * ncol:
                xr = u[:, rbase + LANES * j:rbase + LANES * (j + 1)]
                gc, gs = (dqc, dqs) if j < ncol else (dkc, dks)
                xc = _inv_rms64(xc) * (xc * gc + xr * gs)
            if r == 1:
                d_out[0, :, LANES * j:LANES * (j + 1)] = xc.astype(BF16)
            else:
                dscr[j] = xc
                for c in range(r):
                    d_out[0, c, :, LANES * j:LANES * (j + 1)] = dscr[j, pl.ds(c, tm // r, stride=r), :].astype(BF16)


def _row(v):
    return v.reshape(1, -1).astype(F32)


def _even_in(x, sh, sc, nw, win, wuq, wuk, wvt, qlat, kvlat, qn, kn, dqn, dkn, rope, tm, t):
    b, s, d = x.shape
    tok = lambda w: pl.BlockSpec((1, tm, w), lambda bi, i: (bi, i, 0))
    per_b = pl.BlockSpec((1, 1, d), lambda bi, i: (bi, 0, 0))
    full = lambda a: pl.BlockSpec(a.shape, lambda bi, i: (0,) * a.ndim, pipeline_mode=pl.Buffered(1))
    vt_spec, vt_shape = _t_blocks(b, s, wvt.shape[0], tm, t)
    qt_spec, qt_shape = _t_blocks(b, s, MLA_HEADS * LANES, tm, t)
    dil_specs, dil_shapes = [], []
    for _, r in DIL_CONFIGS:
        if r == 1:
            dil_specs.append(tok(3 * DIL_W))
            dil_shapes.append(jax.ShapeDtypeStruct((b, s, 3 * DIL_W), BF16))
        else:
            dil_specs.append(pl.BlockSpec((1, r, tm // r, 3 * DIL_W), lambda bi, i: (bi, 0, i, 0)))
            dil_shapes.append(jax.ShapeDtypeStruct((b, r, s // r, 3 * DIL_W), BF16))
    return pl.pallas_call(
        _even_in_kernel,
        grid=(b, s // tm),
        in_specs=[tok(d), per_b, per_b, full(nw), full(win), full(wuq), full(wuk), full(wvt),
                  full(qlat), full(kvlat), full(qn), full(kn), full(dqn), full(dkn),
                  tok(LANES)],
        out_specs=[qt_spec, tok(MLA_HEADS * LANES), vt_spec] + dil_specs,
        out_shape=[qt_shape, jax.ShapeDtypeStruct((b, s, MLA_HEADS * LANES), BF16), vt_shape] + dil_shapes,
        scratch_shapes=[pltpu.VMEM((3 * DIL_W // LANES, tm, LANES), F32)],
        compiler_params=_cparams(("parallel", "arbitrary")),
        name="even_in_proj",
    )(x, sh, sc, nw, win, wuq, wuk, wvt, qlat, kvlat, qn, kn, dqn, dkn, rope)


def _odd_in_kernel(x_ref, sh_ref, sc_ref, nw_ref, win_ref, wvt_ref, dqn_ref, dkn_ref, mqn_ref, mkn_ref,
                   rope_ref,
                   qc_out, kc_out, qm_out, qmf_out, km_out, vct_out, vmt_out, kmean_out):
    x = x_ref[0]
    tm = x.shape[0]
    h = (_rms(x, nw_ref[...]) * (1.0 + sc_ref[0]) + sh_ref[0]).astype(BF16)
    u = jnp.dot(h, win_ref[...], preferred_element_type=F32)
    vt = _nt_dot(wvt_ref[...], h).astype(BF16)
    nvc = vct_out.shape[2]
    _store_t_blocks(vct_out, vt[:nvc])
    _store_t_blocks(vmt_out, vt[nvc:])
    c64, s64 = _rope_lanes64(rope_ref[0])
    scale = HEAD_DIM ** -0.5 * LOG2E
    nqk = DIFF_HEADS * LANES
    nm = MOBA_HEADS * HEAD_DIM
    rot = 2 * nqk + 2 * nm

    def tables(gain_ref, mult):
        return gain_ref[0:1] * c64 * mult, gain_ref[1:2] * s64 * mult

    def roped(off, col, tab):
        a = off + LANES * col
        xc, xr = u[:, a:a + LANES], u[:, rot + a:rot + a + LANES]
        return _inv_rms64(xc) * (xc * tab[0] + xr * tab[1])

    t_qc, t_kc, t_qm, t_km = tables(dqn_ref, scale), tables(dkn_ref, 1.0), tables(mqn_ref, 1.0), tables(mkn_ref, 1.0)
    for col in range(nqk // LANES):
        sl = slice(LANES * col, LANES * (col + 1))
        qc_out[0, :, sl] = roped(0, col, t_qc).astype(BF16)
        kc_out[0, :, sl] = roped(nqk, col, t_kc).astype(BF16)

    base = 2 * nqk
    for col in range(nm // LANES):
        sl = slice(LANES * col, LANES * (col + 1))
        qm = roped(base, col, t_qm)
        qmf_out[0, :, sl] = qm
        qm_out[0, :, sl] = (qm * scale).astype(BF16)
        km = roped(base + nm, col, t_km)
        km_out[0, :, sl] = km.astype(BF16)
        for blk in range(tm // MOBA_BLOCK):
            rows = slice(MOBA_BLOCK * blk, MOBA_BLOCK * (blk + 1))
            kmean_out[0, 0, blk:blk + 1, sl] = jnp.mean(km[rows], axis=0, keepdims=True)


def _odd_in(x, sh, sc, nw, win, wvt, dqn, dkn, mqn, mkn, rope, tm, t):
    b, s, d = x.shape
    tok = lambda w: pl.BlockSpec((1, tm, w), lambda bi, i: (bi, i, 0))
    per_b = pl.BlockSpec((1, 1, d), lambda bi, i: (bi, 0, 0))
    full = lambda a: pl.BlockSpec(a.shape, lambda bi, i: (0,) * a.ndim, pipeline_mode=pl.Buffered(1))
    w = DIFF_HEADS * LANES
    nblk = tm // MOBA_BLOCK
    shp = lambda dt: jax.ShapeDtypeStruct((b, s, w), dt)
    vt_spec, vt_shape = _t_blocks(b, s, wvt.shape[0] // 2, tm, t)
    return pl.pallas_call(
        _odd_in_kernel,
        grid=(b, s // tm),
        in_specs=[tok(d), per_b, per_b, full(nw), full(win), full(wvt), full(dqn), full(dkn), full(mqn), full(mkn),
                  tok(LANES)],
        out_specs=[tok(w)] * 5 + [vt_spec, vt_spec, pl.BlockSpec((1, 1, nblk, w), lambda bi, i: (bi, i, 0, 0))],
        out_shape=[shp(BF16), shp(BF16), shp(BF16), shp(F32), shp(BF16), vt_shape, vt_shape,
                   jax.ShapeDtypeStruct((b, s // tm, nblk, w), F32)],
        compiler_params=_cparams(("parallel", "arbitrary")),
        name="odd_in_proj",
    )(x, sh, sc, nw, win, wvt, dqn, dkn, mqn, mkn, rope)


def _qt_halves(q):
    qt = q.astype(F32).T
    row = lax.broadcasted_iota(jnp.int32, qt.shape, 0)
    return [jnp.where((row >= HEAD_DIM) == bool(half), qt, 0.0).astype(BF16) for half in range(2)]


def _flash_t(qts, k_ref, vt_ref, k_slices, v_rows, scr, i, t, past_mask=None):
    m_scr, acc_scr, s_scr, p_scr, a_scr, b_scr = scr
    n = len(qts)
    ones = jnp.ones((ONES_ROWS, t), BF16)

    def qk(blk, s):
        b0 = pl.multiple_of(blk * t, t)
        return jnp.dot(k_ref[0, pl.ds(b0, t), k_slices[s]], qts[s], preferred_element_type=F32)

    def pv(blk, s, p):
        vt = jnp.concatenate([vt_ref[0, blk, v_rows[s], :], ones], axis=0)
        return jnp.dot(vt, p, preferred_element_type=F32)

    def kept_pv(blk, s, p, is_diag):
        out = pv(blk, s, p)
        if past_mask is not None:
            out = jnp.where(jnp.logical_or(is_diag, past_mask(s, blk)), out, 0.0)
        return out

    def store_scores(slot, s, st):
        s_scr[slot, s] = st
        b_scr[slot, s] = jnp.max(st, axis=0, keepdims=True)

    krow = lax.broadcasted_iota(jnp.int32, (t, t), 0)
    qcol = lax.broadcasted_iota(jnp.int32, (t, t), 1)
    causal = krow <= qcol
    last_past = jnp.maximum(i - 1, 0)
    for s in range(n):
        st = jnp.where(causal, qk(i, s), NEG_INF)
        m = jnp.max(st, axis=0, keepdims=True)
        m_scr[s] = m
        p_scr[0, s] = jnp.exp2(st - m).astype(BF16)
        p_scr[1, s] = jnp.zeros((t, t), BF16)
        a_scr[0, s] = jnp.ones_like(m)
        a_scr[1, s] = jnp.ones_like(m)
        acc_scr[s] = jnp.zeros(acc_scr.shape[1:], F32)
        store_scores(0, s, qk(0, s))
        store_scores(1, s, qk(jnp.minimum(1, last_past), s))

    def softmax_stage(slot, valid, rs, ws, blk):
        for s in range(n):
            cmax = b_scr[2 * rs + slot, s]
            if past_mask is not None:
                cmax = jnp.where(past_mask(s, blk), cmax, NEG_INF)
            if valid is not None:
                cmax = jnp.where(valid, cmax, NEG_INF)
            m_prev = m_scr[s]
            m_new = jnp.maximum(m_prev, cmax)
            a_scr[2 * ws + slot, s] = jnp.exp2(m_prev - m_new)
            p_scr[2 * ws + slot, s] = jnp.exp2(s_scr[2 * rs + slot, s] - m_new).astype(BF16)
            m_scr[s] = m_new

    def body(kk, rs, ws):
        first = kk == 0
        pa_blk = jnp.where(first, i, 2 * kk - 2)
        pb_blk = jnp.where(first, i, 2 * kk - 1)
        p_prev = [[p_scr[2 * rs + sl, s] for s in range(n)] for sl in range(2)]
        a_prev = [[a_scr[2 * rs + sl, s] for s in range(n)] for sl in range(2)]
        s_next = [[qk(jnp.minimum(2 * kk + 2 + sl, last_past), s) for s in range(n)] for sl in range(2)]
        pvs = [[kept_pv(blk, s, p_prev[sl][s], first) for s in range(n)] for sl, blk in enumerate((pa_blk, pb_blk))]
        softmax_stage(0, None, rs, ws, 2 * kk)
        softmax_stage(1, 2 * kk + 1 < i, rs, ws, jnp.minimum(2 * kk + 1, last_past))
        for sl in range(2):
            for s in range(n):
                store_scores(2 * ws + sl, s, s_next[sl][s])
        for s in range(n):
            acc_scr[s] = a_prev[1][s] * (a_prev[0][s] * acc_scr[s] + pvs[0][s]) + pvs[1][s]

    def two_bodies(kp, carry):
        body(2 * kp, 0, 1)
        body(2 * kp + 1, 1, 0)
        return carry

    nbody = (i + 1) // 2
    lax.fori_loop(0, nbody // 2, two_bodies, 0)

    @pl.when(nbody % 2 == 1)
    def _():
        body(nbody - 1, 0, 1)

    fs = nbody % 2
    kl = jnp.maximum(nbody - 1, 0)
    none = i == 0
    fa_blk = jnp.where(none, i, 2 * kl)
    fb_blk = jnp.where(none, i, jnp.minimum(2 * kl + 1, last_past))
    fb_valid = jnp.logical_or(none, i % 2 == 0)
    for s in range(n):
        pv_b = jnp.where(fb_valid, kept_pv(fb_blk, s, p_scr[2 * fs + 1, s], none), 0.0)
        acc_scr[s] = (a_scr[2 * fs + 1, s] * (a_scr[2 * fs, s] * acc_scr[s]
                                              + kept_pv(fa_blk, s, p_scr[2 * fs, s], none)) + pv_b)


def _flash_scratch(n, nv, t):
    return [pltpu.VMEM((n, 1, t), F32), pltpu.VMEM((n, nv + ONES_ROWS, t), F32), pltpu.VMEM((4, n, t, t), F32),
            pltpu.VMEM((4, n, t, t), BF16), pltpu.VMEM((4, n, 1, t), F32), pltpu.VMEM((4, n, 1, t), F32)]


def _normalised(acc_scr, s, nv):
    return acc_scr[s, :nv, :] / acc_scr[s, nv:nv + 1, :]


def _half_mask(x, half):
    lane = _lane_iota((1, LANES))
    return jnp.where((lane >= HEAD_DIM) == bool(half), x, jnp.zeros_like(x))


def _mla_kernel(qt_ref, k_ref, vt_ref, ot_ref, *scr, t, nh):
    i = pl.program_id(2)
    ks = [slice(LANES * s, LANES * (s + 1)) for s in range(nh)]
    qs = [qt_ref[0, 0, ks[s], :] for s in range(nh)]
    vr = [slice(HEAD_DIM * s, HEAD_DIM * (s + 1)) for s in range(nh)]
    _flash_t(qs, k_ref, vt_ref, ks, vr, scr, i, t)
    for s in range(nh):
        ot_ref[0, 0, vr[s], :] = _normalised(scr[1], s, HEAD_DIM).astype(ot_ref.dtype)


def _mla_attention(qt, k, vt, nh):
    b, s, _ = k.shape
    _, nb, nv, t = vt.shape
    return pl.pallas_call(
        functools.partial(_mla_kernel, t=t, nh=nh),
        grid=(b, MLA_HEADS // nh, nb),
        in_specs=[pl.BlockSpec((1, 1, nh * LANES, t), lambda bi, p, i: (bi, i, p, 0)),
                  pl.BlockSpec((1, s, nh * LANES), lambda bi, p, i: (bi, 0, p)),
                  pl.BlockSpec((1, nb, nh * HEAD_DIM, t), lambda bi, p, i: (bi, 0, p, 0))],
        out_specs=pl.BlockSpec((1, 1, nh * HEAD_DIM, t), lambda bi, p, i: (bi, i, p, 0)),
        out_shape=jax.ShapeDtypeStruct((b, nb, nv, t), BF16),
        scratch_shapes=_flash_scratch(nh, HEAD_DIM, t),
        compiler_params=_cparams(("parallel", "parallel", "arbitrary")),
        name="mla_attention",
    )(qt, k, vt)


def _diff_kernel(q_ref, k_ref, vt_ref, lam_ref, sub_ref, ot_ref, *scr, t, nh, lam_init):
    i = pl.program_id(2)
    qs, ks, vr = [], [], []
    for hd in range(nh):
        cols = slice(LANES * hd, LANES * (hd + 1))
        qs += _qt_halves(q_ref[0, :, cols])
        ks += [cols, cols]
        vr += [cols, cols]
    _flash_t(qs, k_ref, vt_ref, ks, vr, scr, i, t)
    lv = lam_ref[...]
    lam = (jnp.exp(jnp.sum(lv[0:1] * lv[1:2], keepdims=True))
           - jnp.exp(jnp.sum(lv[2:3] * lv[3:4], keepdims=True)) + lam_init)
    for hd in range(nh):
        o = _normalised(scr[1], 2 * hd, LANES) - lam * _normalised(scr[1], 2 * hd + 1, LANES)
        o = o * lax.rsqrt(jnp.mean(o * o, axis=0, keepdims=True) + NORM_EPS) * sub_ref[...]
        ot_ref[0, 0, LANES * hd:LANES * (hd + 1), :] = (o * (1.0 - lam_init)).astype(ot_ref.dtype)


def _diff_attention(q, k, vt, lam_rows, subln_col, lam_init, nh):
    b, s, w = q.shape
    _, nb, _, t = vt.shape
    return pl.pallas_call(
        functools.partial(_diff_kernel, t=t, nh=nh, lam_init=lam_init),
        grid=(b, DIFF_HEADS // nh, nb),
        in_specs=[pl.BlockSpec((1, t, nh * LANES), lambda bi, p, i: (bi, i, p)),
                  pl.BlockSpec((1, s, nh * LANES), lambda bi, p, i: (bi, 0, p)),
                  pl.BlockSpec((1, nb, nh * LANES, t), lambda bi, p, i: (bi, 0, p, 0)),
                  pl.BlockSpec(lam_rows.shape, lambda bi, p, i: (0, 0)),
                  pl.BlockSpec(subln_col.shape, lambda bi, p, i: (0, 0))],
        out_specs=pl.BlockSpec((1, 1, nh * LANES, t), lambda bi, p, i: (bi, i, p, 0)),
        out_shape=jax.ShapeDtypeStruct((b, nb, w, t), BF16),
        scratch_shapes=_flash_scratch(2 * nh, LANES, t),
        compiler_params=_cparams(("parallel", "parallel", "arbitrary")),
        name="diff_attention",
    )(q, k, vt, lam_rows, subln_col)


def _moba_kernel(q_ref, qf_ref, k_ref, vt_ref, kmean_ref, ot_ref, sel_scr, *scr, t, npair):
    i = pl.program_id(2)
    nb = kmean_ref.shape[1]
    brow = lax.broadcasted_iota(jnp.int32, (nb, t), 0).astype(F32)
    past = brow < i.astype(F32)
    qs, ks, vr = [], [], []
    for pr in range(npair):
        cols = slice(LANES * pr, LANES * (pr + 1))
        qf, kmean = qf_ref[0, :, cols], kmean_ref[0, :, cols]
        qs += _qt_halves(q_ref[0, :, cols])
        kmean_halves = jnp.concatenate([_half_mask(kmean, 0), _half_mask(kmean, 1)], axis=0)
        gates = lax.dot_general(kmean_halves, qf, (((1,), (1,)), ((), ())),
                                preferred_element_type=F32, precision=lax.Precision.HIGHEST)
        for half in range(2):
            s = 2 * pr + half
            gate = jnp.where(past, gates[nb * half:nb * (half + 1)], NEG_INF)
            sel = jnp.zeros((nb, t), F32)
            for _ in range(MOBA_TOPK):
                top = jnp.max(gate, axis=0, keepdims=True)
                first = jnp.min(jnp.where(gate == top, brow, float(nb)), axis=0, keepdims=True)
                pick = brow == first
                sel = jnp.where(pick, 1.0, sel)
                gate = jnp.where(pick, 2 * NEG_INF, gate)
            sel_scr[s] = jnp.where(past, sel, 0.0)
            ks.append(cols)
            vr.append(slice(HEAD_DIM * s, HEAD_DIM * (s + 1)))
    _flash_t(qs, k_ref, vt_ref, ks, vr, scr, i, t,
             past_mask=lambda s, jj: sel_scr[s, pl.ds(jj, 1), :] > 0.5)
    for s in range(2 * npair):
        ot_ref[0, 0, vr[s], :] = _normalised(scr[1], s, HEAD_DIM).astype(ot_ref.dtype)


def _moba_attention(q, qf, k, vt, kmean, npair):
    b, s, w = q.shape
    _, nb, _, t = vt.shape
    nkm = kmean.shape[1]
    lanes = npair * LANES
    return pl.pallas_call(
        functools.partial(_moba_kernel, t=t, npair=npair),
        grid=(b, w // lanes, nb),
        in_specs=[pl.BlockSpec((1, t, lanes), lambda bi, p, i: (bi, i, p)),
                  pl.BlockSpec((1, t, lanes), lambda bi, p, i: (bi, i, p)),
                  pl.BlockSpec((1, s, lanes), lambda bi, p, i: (bi, 0, p)),
                  pl.BlockSpec((1, nb, lanes, t), lambda bi, p, i: (bi, 0, p, 0)),
                  pl.BlockSpec((1, nkm, lanes), lambda bi, p, i: (bi, 0, p))],
        out_specs=pl.BlockSpec((1, 1, lanes, t), lambda bi, p, i: (bi, i, p, 0)),
        out_shape=jax.ShapeDtypeStruct((b, nb, w, t), BF16),
        scratch_shapes=[pltpu.VMEM((2 * npair, nkm, t), F32)] + _flash_scratch(2 * npair, HEAD_DIM, t),
        compiler_params=_cparams(("parallel", "parallel", "arbitrary")),
        name="moba_attention",
    )(q, qf, k, vt, kmean)


def _sw_kernel(q_ref, kc_ref, vc_ref, kp_ref, vp_ref, o_ref, l_ref, *, nbk):
    i = pl.program_id(1)
    blk = SW_BLOCK
    qi = lax.broadcasted_iota(jnp.int32, (blk, 2 * blk), 0)
    kj = lax.broadcasted_iota(jnp.int32, (blk, 2 * blk), 1)
    band = (kj >= qi) & (kj <= qi + blk)
    lane = _lane_iota((1, LANES))
    heads = [(pair, half) for pair in range(DIL_W // LANES) for half in range(2)]

    def scores(sq, n):
        rows = slice(blk * n, blk * (n + 1))
        q = q_ref[sq, rows, :]
        if n == 0:
            kprev, valid = kp_ref[sq], band & ((kj >= blk) | (i > 0))
        else:
            kprev, valid = kc_ref[sq, blk * (n - 1):blk * n, :], band
        kk = jnp.concatenate([kprev, kc_ref[sq, rows, :]], axis=0)
        out = []
        for pair, half in heads:
            sl = slice(LANES * pair, LANES * (pair + 1))
            out.append(jnp.where(valid, _nt_dot(_half_mask(q[:, sl], half), kk[:, sl]), NEG_INF))
        return out

    work = [(sq, n) for sq in range(q_ref.shape[0]) for n in range(nbk)]
    nxt = scores(*work[0])
    for w, (sq, n) in enumerate(work):
        cur, rows = nxt, slice(blk * n, blk * (n + 1))
        if w + 1 < len(work):
            nxt = scores(*work[w + 1])
        vprev = vp_ref[sq] if n == 0 else vc_ref[sq, blk * (n - 1):blk * n, :]
        vv = jnp.concatenate([vprev, vc_ref[sq, rows, :]], axis=0)
        outs, lses = [], []
        for (pair, half), sc in zip(heads, cur):
            m = jnp.max(sc, axis=-1, keepdims=True)
            p = jnp.exp2(sc - m)
            l = jnp.sum(p, axis=-1, keepdims=True)
            sl = slice(LANES * pair, LANES * (pair + 1))
            outs.append(jnp.dot(p.astype(BF16), vv[:, sl], preferred_element_type=F32) / l)
            lses.append(m + jnp.log2(l))
        for pair in range(DIL_W // LANES):
            sl = slice(LANES * pair, LANES * (pair + 1))
            o_ref[sq, rows, sl] = jnp.where(lane < HEAD_DIM, outs[2 * pair], outs[2 * pair + 1])
            l_ref[sq, rows, sl] = jnp.where(lane < HEAD_DIM, lses[2 * pair], lses[2 * pair + 1])


def _sliding_window(dil, nbk, name):
    lead, (m, w3) = dil.shape[:-2], dil.shape[-2:]
    seqs = dil.reshape((-1, m, w3))
    per_seq = min(nbk, m // SW_BLOCK)
    nsq = max(1, nbk // per_seq)
    nbk = per_seq
    rows = SW_BLOCK * nbk
    cur = lambda off: pl.BlockSpec((nsq, rows, DIL_W), lambda n, i: (n, i, off))
    prev = lambda off: pl.BlockSpec((nsq, SW_BLOCK, DIL_W), lambda n, i: (n, jnp.maximum(i * nbk - 1, 0), off))
    out_spec = pl.BlockSpec((nsq, rows, DIL_W), lambda n, i: (n, i, 0))
    o, l = pl.pallas_call(
        functools.partial(_sw_kernel, nbk=nbk),
        grid=(seqs.shape[0] // nsq, m // rows),
        in_specs=[cur(0), cur(1), cur(2), prev(1), prev(2)],
        out_specs=[out_spec, out_spec],
        out_shape=[jax.ShapeDtypeStruct((seqs.shape[0], m, DIL_W), F32)] * 2,
        compiler_params=_cparams(("parallel", "arbitrary")),
        name=name,
    )(seqs, seqs, seqs, seqs, seqs)
    return o.reshape(lead + (m, DIL_W)), l.reshape(lead + (m, DIL_W))


def _mlp_tail(y, mlp_refs):
    sh_ref, sc_ref, g_ref, nw_ref, w1_ref, w2_ref = mlp_refs
    h = (_rms(y, nw_ref[...]) * (1.0 + sc_ref[0]) + sh_ref[0]).astype(BF16)
    acc = jnp.zeros(y.shape, F32)
    for c in range(w1_ref.shape[1] // MLP_CHUNK):
        cols = slice(MLP_CHUNK * c, MLP_CHUNK * (c + 1))
        a = jnp.maximum(jnp.dot(h, w1_ref[:, cols], preferred_element_type=F32), 0.0)
        acc = acc + jnp.dot((a * a).astype(BF16), w2_ref[cols, :], preferred_element_type=F32)
    return y + g_ref[0] * acc


def _proj_t(ot_ref, w_rows):
    return jnp.concatenate([_tn_dot(ot_ref[0, j], w_rows) for j in range(ot_ref.shape[1])], axis=0)


def _mlp_specs(mlp_args, d):
    per_b = pl.BlockSpec((1, 1, d), lambda bi, i: (bi, 0, 0))
    const = lambda a: pl.BlockSpec(a.shape, lambda bi, i: (0, 0), pipeline_mode=pl.Buffered(1))
    sh, sc, g2, nw, w1, w2 = mlp_args
    return [per_b, per_b, per_b, pl.BlockSpec(nw.shape, lambda bi, i: (0, 0)), const(w1), const(w2)]


def _even_out_kernel(x_ref, g_ref, oat_ref, o0_ref, o1_ref, o2_ref, l0_ref, l1_ref, l2_ref, w_ref,
                     *rest):
    *mlp_refs, y_ref, tscr = rest
    tm = x_ref.shape[1]
    ncol = DIL_W // LANES

    def token_major(ref, slot):
        r = ref.shape[1]
        for c in range(r):
            for j in range(ncol):
                tscr[slot * ncol + j, pl.ds(c, tm // r, stride=r), :] = ref[0, c, :, LANES * j:LANES * (j + 1)]
        return jnp.concatenate([tscr[slot * ncol + j] for j in range(ncol)], axis=1)

    o0, l0 = o0_ref[0], l0_ref[0]
    o1, l1 = token_major(o1_ref, 0), token_major(l1_ref, 1)
    o2, l2 = token_major(o2_ref, 2), token_major(l2_ref, 3)
    top = jnp.maximum(jnp.maximum(l0, l1), l2)
    w0, w1, w2 = jnp.exp2(l0 - top), jnp.exp2(l1 - top), jnp.exp2(l2 - top)
    ob = (w0 * o0 + w1 * o1 + w2 * o2) / (w0 + w1 + w2)
    na = oat_ref.shape[2]
    y = _proj_t(oat_ref, w_ref[:na, :]) + jnp.dot(ob.astype(BF16), w_ref[na:, :], preferred_element_type=F32)
    y_ref[0] = _mlp_tail(x_ref[0] + g_ref[0] * y, mlp_refs)


def _t_spec(a, tm):
    return pl.BlockSpec((1, tm // a.shape[-1]) + a.shape[2:], lambda bi, i: (bi, i, 0, 0))


def _even_out(x, g1, oat, o_dil, l_dil, w_out, mlp_args, tm):
    b, s, d = x.shape
    tok = lambda w: pl.BlockSpec((1, tm, w), lambda bi, i: (bi, i, 0))

    def dil_spec(a):
        if a.ndim == 3:
            return tok(DIL_W)
        r = a.shape[1]
        return pl.BlockSpec((1, r, tm // r, DIL_W), lambda bi, i: (bi, 0, i, 0))

    return pl.pallas_call(
        _even_out_kernel,
        grid=(b, s // tm),
        in_specs=[tok(d), pl.BlockSpec((1, 1, d), lambda bi, i: (bi, 0, 0)), _t_spec(oat, tm)]
                 + [dil_spec(a) for a in (*o_dil, *l_dil)]
                 + [pl.BlockSpec(w_out.shape, lambda bi, i: (0, 0), pipeline_mode=pl.Buffered(1))]
                 + _mlp_specs(mlp_args, d),
        out_specs=tok(d),
        out_shape=jax.ShapeDtypeStruct((b, s, d), F32),
        scratch_shapes=[pltpu.VMEM((4 * DIL_W // LANES, tm, LANES), F32)],
        compiler_params=_cparams(("parallel", "arbitrary")),
        name="even_out_mlp",
    )(x, g1, oat, *o_dil, *l_dil, w_out, *mlp_args)


def _odd_out_kernel(x_ref, g_ref, oct_ref, odt_ref, w_ref, *rest):
    *mlp_refs, y_ref = rest
    nc = oct_ref.shape[2]
    y = _proj_t(oct_ref, w_ref[:nc, :]) + _proj_t(odt_ref, w_ref[nc:, :])
    y_ref[0] = _mlp_tail(x_ref[0] + g_ref[0] * y, mlp_refs)


def _odd_out(x, g1, oct, odt, w_out, mlp_args, tm):
    b, s, d = x.shape
    tok = lambda w: pl.BlockSpec((1, tm, w), lambda bi, i: (bi, i, 0))
    return pl.pallas_call(
        _odd_out_kernel,
        grid=(b, s // tm),
        in_specs=[tok(d), pl.BlockSpec((1, 1, d), lambda bi, i: (bi, 0, 0)), _t_spec(oct, tm), _t_spec(odt, tm),
                  pl.BlockSpec(w_out.shape, lambda bi, i: (0, 0), pipeline_mode=pl.Buffered(1))]
                 + _mlp_specs(mlp_args, d),
        out_specs=tok(d),
        out_shape=jax.ShapeDtypeStruct((b, s, d), F32),
        compiler_params=_cparams(("parallel", "arbitrary")),
        name="odd_out_mlp",
    )(x, g1, oct, odt, w_out, *mlp_args)


def _rope_tables(positions):
    pos = positions.astype(F32)[:, :, None]
    half = HEAD_DIM // 2
    inv = ROPE_THETA ** (-jnp.arange(half, dtype=F32) / half)
    ang = pos * inv[_freq_order()]
    zeros = jnp.zeros(ang.shape[:2] + (LANES - 2 * half,), F32)
    return jnp.concatenate([jnp.cos(ang), jnp.sin(ang), zeros], axis=-1)


def _freq_order():
    half = HEAD_DIM // 2
    return np.concatenate([np.arange(0, half, 2), np.arange(1, half, 2)])


def _head_perm(n):
    within = np.concatenate([_freq_order(), HEAD_DIM // 2 + _freq_order()])
    idx = np.arange(n)
    return idx // HEAD_DIM * HEAD_DIM + within[idx % HEAD_DIM]


def _partner(n, width):
    idx = np.arange(n)
    return np.where(idx % width < width // 2, idx + width // 2, idx - width // 2)


def _gather_cols(w, cols):
    w = jnp.pad(w.astype(BF16), ((0, 0), (0, 1)))
    return jnp.take(w, jnp.asarray(cols, jnp.int32), axis=1)


def _even_weights(w_in, w_uq, w_ukv, qn, kn):
    o2 = MLA_Q_RANK + MLA_KV_RANK
    o3 = o2 + MLA_ROPE
    p_rope = _partner(MLA_ROPE, MLA_ROPE)
    tail = LANES - MLA_QK
    hp, partner = _head_perm(DIL_W), _partner(DIL_W, HEAD_DIM)
    ngroup = len(DIL_CONFIGS)

    zero = w_in.shape[1]
    kr = np.full(LANES, zero)
    kr_rot = kr.copy()
    kr[MLA_NOPE:MLA_QK] = o2 + np.arange(MLA_ROPE)
    kr_rot[MLA_NOPE:MLA_QK] = o2 + p_rope
    dil = lambda t, g, cols: o3 + (t * ngroup + g) * DIL_W + cols
    main = [dil(t, g, hp if t < 2 else np.arange(DIL_W)) for g in range(ngroup) for t in range(3)]
    rot = [dil(t, g, hp[partner]) for g in range(ngroup) for t in range(2)]
    win = _gather_cols(w_in, np.concatenate([np.arange(o2), kr, kr_rot] + main + rot))

    nqk = MLA_HEADS * MLA_QK
    head = np.full((MLA_HEADS, LANES), nqk)
    head[:, :MLA_QK] = np.arange(nqk).reshape(MLA_HEADS, MLA_QK)
    head_rot = head.copy()
    head_rot[:, MLA_NOPE:MLA_QK] = head[:, MLA_NOPE:MLA_QK][:, p_rope]
    wuq = _gather_cols(w_uq.reshape(MLA_Q_RANK, nqk), np.concatenate([head.ravel(), head_rot.ravel()]))

    nkv = w_ukv.shape[1] * w_ukv.shape[2]
    kcols = np.full((MLA_HEADS, LANES), nkv)
    kcols[:, :MLA_NOPE] = (np.arange(MLA_HEADS) * w_ukv.shape[2])[:, None] + np.arange(MLA_NOPE)
    wuk = _gather_cols(w_ukv.reshape(MLA_KV_RANK, nkv), kcols.ravel())
    wvt = w_ukv[:, :, MLA_NOPE:].reshape(MLA_KV_RANK, -1).T.astype(BF16)

    def gains(g):
        rot = jnp.concatenate([g[:MLA_NOPE], g[MLA_NOPE:][p_rope]])
        return jnp.stack([jnp.pad(g, (0, tail)), jnp.pad(rot, (0, tail))]).astype(F32)

    return win, wuq, wuk, wvt, gains(qn), gains(kn)


def _odd_weights(w_in):
    nqk = DIFF_HEADS * LANES
    nm = MOBA_HEADS * HEAD_DIM
    main = np.concatenate([np.arange(2 * nqk), 3 * nqk + np.arange(2 * nm)])
    main = main[_head_perm(main.size)]
    win = _gather_cols(w_in, np.concatenate([main, main[_partner(main.size, HEAD_DIM)]]))
    values = np.concatenate([2 * nqk + np.arange(nqk), 3 * nqk + 2 * nm + np.arange(nm)])
    return win, _gather_cols(w_in, values).T


def _gains64(g):
    g = g[_head_perm(HEAD_DIM)]
    rot = g[_partner(HEAD_DIM, HEAD_DIM)]
    return jnp.stack([jnp.concatenate([g, g]), jnp.concatenate([rot, rot])]).astype(F32)


def kernel(x, c, positions, ada_w, ada_b, norm_mix, norm_mlp, mlp_w1, mlp_w2, even_w_in, even_w_out, mla_q_lat_norm, mla_kv_lat_norm, mla_w_uq, mla_w_ukv, mla_q_norm, mla_k_norm, dil_q_norm, dil_k_norm, odd_w_in, odd_w_out, diff_q_norm, diff_k_norm, diff_lambda, diff_subln, moba_q_norm, moba_k_norm):
    b, s, d = x.shape
    depth = ada_w.shape[0]
    t_attn, tm_in, tm_out = ATTN_BLOCK, TOKEN_TILE, TOKEN_TILE

    mod = _adaln(c, ada_w, ada_b)
    rope = _rope_tables(positions)

    for layer in range(depth):
        sh1, sc1, g1, sh2, sc2, g2 = [mod[layer, :, d * t:d * (t + 1)].reshape(b, 1, d) for t in range(6)]
        mlp_args = (sh2, sc2, g2, _row(norm_mlp[layer]), mlp_w1[layer].astype(BF16), mlp_w2[layer].astype(BF16))
        i = layer // 2
        if layer % 2 == 0:
            win, wuq, wuk, wvt, qn, kn = _even_weights(even_w_in[i], mla_w_uq[i], mla_w_ukv[i],
                                                       mla_q_norm[i], mla_k_norm[i])
            q, k, vt, *dils = _even_in(x, sh1, sc1, _row(norm_mix[layer]), win, wuq, wuk, wvt,
                                       _row(mla_q_lat_norm[i]), _row(mla_kv_lat_norm[i]), qn, kn,
                                       _gains64(dil_q_norm[i]), _gains64(dil_k_norm[i]), rope, tm_in, t_attn)
            o_at = _mla_attention(q, k, vt, MLA_HEADS_PER_STEP)
            o_dil, l_dil = zip(*[_sliding_window(dg, SW_BLOCKS_PER_STEP, f"sliding_window_g{g}")
                                 for g, dg in enumerate(dils)])
            x = _even_out(x, g1, o_at, o_dil, l_dil, even_w_out[i].astype(BF16), mlp_args, tm_out)
        else:
            lam_init = 0.8 - 0.6 * math.exp(-0.3 * layer)
            win, wvt = _odd_weights(odd_w_in[i])
            qc, kc, qm, qmf, km, vct, vmt, kmean = _odd_in(
                x, sh1, sc1, _row(norm_mix[layer]), win, wvt,
                _gains64(diff_q_norm[i]), _gains64(diff_k_norm[i]), _gains64(moba_q_norm[i]),
                _gains64(moba_k_norm[i]),
                rope, tm_in, t_attn)
            o_ct = _diff_attention(qc, kc, vct, diff_lambda[i].astype(F32),
                                   diff_subln[i].reshape(-1, 1).astype(F32), lam_init, DIFF_HEADS_PER_STEP)
            o_dt = _moba_attention(qm, qmf, km, vmt, kmean.reshape(b, s // MOBA_BLOCK, kmean.shape[-1]),
                                   MOBA_PAIRS_PER_STEP)
            x = _odd_out(x, g1, o_ct, o_dt, odd_w_out[i].astype(BF16), mlp_args, tm_out)
    return x
```

```python
import functools
import math

import jax
import jax.numpy as jnp
import numpy as np
from jax import lax
from jax.experimental import pallas as pl
from jax.experimental.pallas import tpu as pltpu

F32 = jnp.float32
BF16 = jnp.bfloat16

LANES = 128
HEAD_DIM = 64
ROPE_THETA = 10000.0
NORM_EPS = 1e-6
NEG_INF = -1e30
LOG2E = math.log2(math.e)

MLA_HEADS = 8
MLA_Q_RANK = 384
MLA_KV_RANK = 256
MLA_NOPE = 64
MLA_ROPE = 32
MLA_QK = MLA_NOPE + MLA_ROPE
DIL_CONFIGS = ((128, 1), (512, 4), (2048, 16))
DIL_HEADS = 4
DIL_W = DIL_HEADS * HEAD_DIM
DIFF_HEADS = 4
MOBA_HEADS = 8
MOBA_BLOCK = 256
MOBA_TOPK = 3
SW_BLOCK = 128
ONES_ROWS = 16

VMEM_LIMIT = 56 * 1024 * 1024

ATTN_BLOCK = MOBA_BLOCK
TOKEN_TILE = 2 * ATTN_BLOCK
MLP_CHUNK = 1024
ADALN_TILE = 2048
SW_BLOCKS_PER_STEP = 16
MLA_HEADS_PER_STEP = 4
DIFF_HEADS_PER_STEP = 2
MOBA_PAIRS_PER_STEP = 2


def _cparams(sem):
    return pltpu.CompilerParams(dimension_semantics=sem, vmem_limit_bytes=VMEM_LIMIT)


def _nt_dot(a, b):
    return lax.dot_general(a, b, (((1,), (1,)), ((), ())), preferred_element_type=F32)


def _tn_dot(a, b):
    return lax.dot_general(a, b, (((0,), (0,)), ((), ())), preferred_element_type=F32)


def _rms(x, w):
    return x * lax.rsqrt(jnp.mean(x * x, axis=-1, keepdims=True) + NORM_EPS) * w


def _lane_iota(shape):
    return lax.broadcasted_iota(jnp.int32, shape, len(shape) - 1)


def _adaln_kernel(c_ref, w_ref, b_ref, o_ref):
    c = c_ref[...]
    cond = c * (1.0 / (1.0 + jnp.exp(-c)))
    o_ref[0] = jnp.dot(cond, w_ref[0], preferred_element_type=F32,
                       precision=lax.Precision.HIGHEST) + b_ref[0]


def _adaln(c, ada_w, ada_b):
    depth, d, n = ada_w.shape
    b = c.shape[0]
    tn = ADALN_TILE
    return pl.pallas_call(
        _adaln_kernel,
        grid=(depth, n // tn),
        in_specs=[pl.BlockSpec((b, d), lambda l, j: (0, 0)),
                  pl.BlockSpec((1, d, tn), lambda l, j: (l, 0, j)),
                  pl.BlockSpec((1, 1, tn), lambda l, j: (l, 0, j))],
        out_specs=pl.BlockSpec((1, b, tn), lambda l, j: (l, 0, j)),
        out_shape=jax.ShapeDtypeStruct((depth, b, n), F32),
        compiler_params=_cparams(("parallel", "parallel")),
        name="adaln",
    )(c, ada_w, ada_b.reshape(depth, 1, n))


def _inv_rms64(x):
    lane = _lane_iota((1, LANES))
    lo = lane < HEAD_DIM
    sq = x * x
    s_lo = jnp.sum(jnp.where(lo, sq, 0.0), axis=-1, keepdims=True)
    s_hi = jnp.sum(jnp.where(lo, 0.0, sq), axis=-1, keepdims=True)
    return lax.rsqrt(jnp.where(lo, s_lo, s_hi) * (1.0 / HEAD_DIM) + NORM_EPS)


def _rope_lanes64(tab):
    lane = _lane_iota((1, LANES))
    half = HEAD_DIM // 2
    c = jnp.where(lane < half, tab, 0.0)
    s = pltpu.roll(jnp.where((lane >= half) & (lane < 2 * half), tab, 0.0), LANES - half, 1)
    tile = lambda x: x + pltpu.roll(x, half, 1) + pltpu.roll(x, 2 * half, 1) + pltpu.roll(x, 3 * half, 1)
    return tile(c), tile(s) * jnp.where(lane % HEAD_DIM < half, -1.0, 1.0)


def _rope_lanes_mla(tab):
    lane = _lane_iota((1, LANES))
    half, hm = HEAD_DIM // 2, MLA_ROPE // 2
    c = jnp.where(lane < hm, tab, 0.0)
    s = jnp.where((lane >= half) & (lane < half + hm), tab, 0.0)
    rope = (lane >= MLA_NOPE) & (lane < MLA_QK)
    cos = jnp.where(rope, pltpu.roll(c, MLA_NOPE, 1) + pltpu.roll(c, MLA_NOPE + hm, 1), 1.0)
    sin = pltpu.roll(s, MLA_NOPE + hm - half, 1) - pltpu.roll(s, MLA_NOPE - half, 1)
    return cos, sin


def _store_t_blocks(out_ref, xt):
    t = out_ref.shape[-1]
    for j in range(out_ref.shape[1]):
        out_ref[0, j] = xt[:, t * j:t * (j + 1)]


def _store_col_t(out_ref, col, x):
    t = out_ref.shape[-1]
    for j in range(out_ref.shape[1]):
        out_ref[0, j, LANES * col:LANES * (col + 1), :] = x[t * j:t * (j + 1)].T.astype(out_ref.dtype)


def _t_blocks(b, s, rows, tm, t):
    return (pl.BlockSpec((1, tm // t, rows, t), lambda bi, i: (bi, i, 0, 0)),
            jax.ShapeDtypeStruct((b, s // t, rows, t), BF16))


def _even_in_kernel(x_ref, sh_ref, sc_ref, nw_ref, win_ref, wuq_ref, wuk_ref, wvt_ref,
                    qlat_ref, kvlat_ref, qn_ref, kn_ref, dqn_ref, dkn_ref,
                    rope_ref,
                    qt_out, k_out, vt_out, d0_out, d1_out, d2_out, dscr):
    x = x_ref[0]
    tm = x.shape[0]
    h = _rms(x, nw_ref[...]) * (1.0 + sc_ref[0]) + sh_ref[0]
    u = jnp.dot(h.astype(BF16), win_ref[...], preferred_element_type=F32)

    o1 = MLA_Q_RANK
    o2 = o1 + MLA_KV_RANK
    o_kr, o_krr, o_dil = o2, o2 + LANES, o2 + 2 * LANES
    n_dil = 3 * DIL_W * len(DIL_CONFIGS)
    o_rot = o_dil + n_dil
    nq = MLA_HEADS * LANES
    cqn = _rms(u[:, :o1], qlat_ref[...]).astype(BF16)
    qp = jnp.dot(cqn, wuq_ref[...], preferred_element_type=F32)
    ckvn = _rms(u[:, o1:o2], kvlat_ref[...]).astype(BF16)
    kvp = jnp.dot(ckvn, wuk_ref[...], preferred_element_type=F32)
    _store_t_blocks(vt_out, _nt_dot(wvt_ref[...], ckvn).astype(BF16))
    kr, kr_rot = u[:, o_kr:o_kr + LANES], u[:, o_krr:o_krr + LANES]

    cm, sm = _rope_lanes_mla(rope_ref[0])
    q_scale = MLA_QK ** -0.5 * LOG2E
    qgc, qgs = qn_ref[0:1] * cm * q_scale, qn_ref[1:2] * sm * q_scale
    kgc, kgs = kn_ref[0:1] * cm, kn_ref[1:2] * sm
    kr_term = kr_rot * kgs
    for hd in range(MLA_HEADS):
        sl = slice(LANES * hd, LANES * (hd + 1))
        qh = qp[:, sl]
        inv = lax.rsqrt(jnp.sum(qh * qh, -1, keepdims=True) * (1.0 / MLA_QK) + NORM_EPS)
        _store_col_t(qt_out, hd, inv * (qh * qgc + qp[:, nq + LANES * hd:nq + LANES * (hd + 1)] * qgs))
        kh = kvp[:, sl] + kr
        inv = lax.rsqrt(jnp.sum(kh * kh, -1, keepdims=True) * (1.0 / MLA_QK) + NORM_EPS)
        k_out[0, hd] = (inv * (kh * kgc + kr_term)).astype(BF16)

    c64, s64 = _rope_lanes64(rope_ref[0])
    d_scale = HEAD_DIM ** -0.5 * LOG2E
    dqc, dqs = dqn_ref[0:1] * c64 * d_scale, dqn_ref[1:2] * s64 * d_scale
    dkc, dks = dkn_ref[0:1] * c64, dkn_ref[1:2] * s64
    ncol = DIL_W // LANES
    for g, d_out in enumerate((d0_out, d1_out, d2_out)):
        _, r = DIL_CONFIGS[g]
        base = o_dil + 3 * DIL_W * g
        rbase = o_rot + 2 * DIL_W * g
        for j in range(3 * ncol):
            xc = u[:, base + LANES * j:base + LANES * (j + 1)]
            if j < 2 * ncol:
                xr = u[:, rbase + LANES * j:rbase + LANES * (j + 1)]
                gc, gs = (dqc, dqs) if j < ncol else (dkc, dks)
                xc = _inv_rms64(xc) * (xc * gc + xr * gs)
            if r == 1:
                d_out[0, :, LANES * j:LANES * (j + 1)] = xc.astype(BF16)
            else:
                dscr[j] = xc
                for c in range(r):
                    d_out[0, c, :, LANES * j:LANES * (j + 1)] = dscr[j, pl.ds(c, tm // r, stride=r), :].astype(BF16)


def _row(v):
    return v.reshape(1, -1).astype(F32)


def _even_in(x, sh, sc, nw, win, wuq, wuk, wvt, qlat, kvlat, qn, kn, dqn, dkn, rope, tm, t):
    b, s, d = x.shape
    tok = lambda w: pl.BlockSpec((1, tm, w), lambda bi, i: (bi, i, 0))
    per_b = pl.BlockSpec((1, 1, d), lambda bi, i: (bi, 0, 0))
    full = lambda a: pl.BlockSpec(a.shape, lambda bi, i: (0,) * a.ndim, pipeline_mode=pl.Buffered(1))
    vt_spec, vt_shape = _t_blocks(b, s, wvt.shape[0], tm, t)
    qt_spec, qt_shape = _t_blocks(b, s, MLA_HEADS * LANES, tm, t)
    dil_specs, dil_shapes = [], []
    for _, r in DIL_CONFIGS:
        if r == 1:
            dil_specs.append(tok(3 * DIL_W))
            dil_shapes.append(jax.ShapeDtypeStruct((b, s, 3 * DIL_W), BF16))
        else:
            dil_specs.append(pl.BlockSpec((1, r, tm // r, 3 * DIL_W), lambda bi, i: (bi, 0, i, 0)))
            dil_shapes.append(jax.ShapeDtypeStruct((b, r, s // r, 3 * DIL_W), BF16))
    return pl.pallas_call(
        _even_in_kernel,
        grid=(b, s // tm),
        in_specs=[tok(d), per_b, per_b, full(nw), full(win), full(wuq), full(wuk), full(wvt),
                  full(qlat), full(kvlat), full(qn), full(kn), full(dqn), full(dkn),
                  tok(LANES)],
        out_specs=[qt_spec, pl.BlockSpec((1, MLA_HEADS, tm, LANES), lambda bi, i: (bi, 0, i, 0)), vt_spec] + dil_specs,
        out_shape=[qt_shape, jax.ShapeDtypeStruct((b, MLA_HEADS, s, LANES), BF16), vt_shape] + dil_shapes,
        scratch_shapes=[pltpu.VMEM((3 * DIL_W // LANES, tm, LANES), F32)],
        compiler_params=_cparams(("parallel", "arbitrary")),
        name="even_in_proj",
    )(x, sh, sc, nw, win, wuq, wuk, wvt, qlat, kvlat, qn, kn, dqn, dkn, rope)


def _odd_in_kernel(x_ref, sh_ref, sc_ref, nw_ref, win_ref, wvt_ref, dqn_ref, dkn_ref, mqn_ref, mkn_ref,
                   rope_ref,
                   qc_out, kc_out, qm_out, qmf_out, km_out, vct_out, vmt_out, kmean_out):
    x = x_ref[0]
    tm = x.shape[0]
    h = (_rms(x, nw_ref[...]) * (1.0 + sc_ref[0]) + sh_ref[0]).astype(BF16)
    u = jnp.dot(h, win_ref[...], preferred_element_type=F32)
    vt = _nt_dot(wvt_ref[...], h).astype(BF16)
    nvc = vct_out.shape[2]
    _store_t_blocks(vct_out, vt[:nvc])
    _store_t_blocks(vmt_out, vt[nvc:])
    c64, s64 = _rope_lanes64(rope_ref[0])
    scale = HEAD_DIM ** -0.5 * LOG2E
    nqk = DIFF_HEADS * LANES
    nm = MOBA_HEADS * HEAD_DIM
    rot = 2 * nqk + 2 * nm

    def tables(gain_ref, mult):
        return gain_ref[0:1] * c64 * mult, gain_ref[1:2] * s64 * mult

    def roped(off, col, tab):
        a = off + LANES * col
        xc, xr = u[:, a:a + LANES], u[:, rot + a:rot + a + LANES]
        return _inv_rms64(xc) * (xc * tab[0] + xr * tab[1])

    t_qc, t_kc, t_qm, t_km = tables(dqn_ref, scale), tables(dkn_ref, 1.0), tables(mqn_ref, 1.0), tables(mkn_ref, 1.0)
    for col in range(nqk // LANES):
        sl = slice(LANES * col, LANES * (col + 1))
        qc_out[0, :, sl] = roped(0, col, t_qc).astype(BF16)
        kc_out[0, :, sl] = roped(nqk, col, t_kc).astype(BF16)

    base = 2 * nqk
    for col in range(nm // LANES):
        sl = slice(LANES * col, LANES * (col + 1))
        qm = roped(base, col, t_qm)
        qmf_out[0, :, sl] = qm
        qm_out[0, :, sl] = (qm * scale).astype(BF16)
        km = roped(base + nm, col, t_km)
        km_out[0, :, sl] = km.astype(BF16)
        for blk in range(tm // MOBA_BLOCK):
            rows = slice(MOBA_BLOCK * blk, MOBA_BLOCK * (blk + 1))
            kmean_out[0, 0, blk:blk + 1, sl] = jnp.mean(km[rows], axis=0, keepdims=True)


def _odd_in(x, sh, sc, nw, win, wvt, dqn, dkn, mqn, mkn, rope, tm, t):
    b, s, d = x.shape
    tok = lambda w: pl.BlockSpec((1, tm, w), lambda bi, i: (bi, i, 0))
    per_b = pl.BlockSpec((1, 1, d), lambda bi, i: (bi, 0, 0))
    full = lambda a: pl.BlockSpec(a.shape, lambda bi, i: (0,) * a.ndim, pipeline_mode=pl.Buffered(1))
    w = DIFF_HEADS * LANES
    nblk = tm // MOBA_BLOCK
    shp = lambda dt: jax.ShapeDtypeStruct((b, s, w), dt)
    vt_spec, vt_shape = _t_blocks(b, s, wvt.shape[0] // 2, tm, t)
    return pl.pallas_call(
        _odd_in_kernel,
        grid=(b, s // tm),
        in_specs=[tok(d), per_b, per_b, full(nw), full(win), full(wvt), full(dqn), full(dkn), full(mqn), full(mkn),
                  tok(LANES)],
        out_specs=[tok(w)] * 5 + [vt_spec, vt_spec, pl.BlockSpec((1, 1, nblk, w), lambda bi, i: (bi, i, 0, 0))],
        out_shape=[shp(BF16), shp(BF16), shp(BF16), shp(F32), shp(BF16), vt_shape, vt_shape,
                   jax.ShapeDtypeStruct((b, s // tm, nblk, w), F32)],
        compiler_params=_cparams(("parallel", "arbitrary")),
        name="odd_in_proj",
    )(x, sh, sc, nw, win, wvt, dqn, dkn, mqn, mkn, rope)


def _qt_halves(q):
    qt = q.astype(F32).T
    row = lax.broadcasted_iota(jnp.int32, qt.shape, 0)
    return [jnp.where((row >= HEAD_DIM) == bool(half), qt, 0.0).astype(BF16) for half in range(2)]


def _flash_t(qts, keys, vt_ref, v_rows, scr, i, t, past_mask=None):
    m_scr, acc_scr, s_scr, p_scr, a_scr, b_scr = scr
    n = len(qts)
    ones = jnp.ones((ONES_ROWS, t), BF16)

    def qk(blk, s):
        b0 = pl.multiple_of(blk * t, t)
        return jnp.dot(keys[s][pl.ds(b0, t), :], qts[s], preferred_element_type=F32)

    def pv(blk, s, p):
        vt = jnp.concatenate([vt_ref[0, blk, v_rows[s], :], ones], axis=0)
        return jnp.dot(vt, p, preferred_element_type=F32)

    def kept_pv(blk, s, p, is_diag):
        out = pv(blk, s, p)
        if past_mask is not None:
            out = jnp.where(jnp.logical_or(is_diag, past_mask(s, blk)), out, 0.0)
        return out

    def store_scores(slot, s, st):
        s_scr[slot, s] = st
        b_scr[slot, s] = jnp.max(st, axis=0, keepdims=True)

    krow = lax.broadcasted_iota(jnp.int32, (t, t), 0)
    qcol = lax.broadcasted_iota(jnp.int32, (t, t), 1)
    causal = krow <= qcol
    last_past = jnp.maximum(i - 1, 0)
    for s in range(n):
        st = jnp.where(causal, qk(i, s), NEG_INF)
        m = jnp.max(st, axis=0, keepdims=True)
        m_scr[s] = m
        p_scr[0, s] = jnp.exp2(st - m).astype(BF16)
        p_scr[1, s] = jnp.zeros((t, t), BF16)
        a_scr[0, s] = jnp.ones_like(m)
        a_scr[1, s] = jnp.ones_like(m)
        acc_scr[s] = jnp.zeros(acc_scr.shape[1:], F32)
        store_scores(0, s, qk(0, s))
        store_scores(1, s, qk(jnp.minimum(1, last_past), s))

    def softmax_stage(slot, valid, rs, ws, blk):
        for s in range(n):
            cmax = b_scr[2 * rs + slot, s]
            if past_mask is not None:
                cmax = jnp.where(past_mask(s, blk), cmax, NEG_INF)
            if valid is not None:
                cmax = jnp.where(valid, cmax, NEG_INF)
            m_prev = m_scr[s]
            m_new = jnp.maximum(m_prev, cmax)
            a_scr[2 * ws + slot, s] = jnp.exp2(m_prev - m_new)
            p_scr[2 * ws + slot, s] = jnp.exp2(s_scr[2 * rs + slot, s] - m_new).astype(BF16)
            m_scr[s] = m_new

    def body(kk, rs, ws):
        first = kk == 0
        pa_blk = jnp.where(first, i, 2 * kk - 2)
        pb_blk = jnp.where(first, i, 2 * kk - 1)
        p_prev = [[p_scr[2 * rs + sl, s] for s in range(n)] for sl in range(2)]
        a_prev = [[a_scr[2 * rs + sl, s] for s in range(n)] for sl in range(2)]
        s_next = [[qk(jnp.minimum(2 * kk + 2 + sl, last_past), s) for s in range(n)] for sl in range(2)]
        pvs = [[kept_pv(blk, s, p_prev[sl][s], first) for s in range(n)] for sl, blk in enumerate((pa_blk, pb_blk))]
        softmax_stage(0, None, rs, ws, 2 * kk)
        softmax_stage(1, 2 * kk + 1 < i, rs, ws, jnp.minimum(2 * kk + 1, last_past))
        for sl in range(2):
            for s in range(n):
                store_scores(2 * ws + sl, s, s_next[sl][s])
        for s in range(n):
            acc_scr[s] = a_prev[1][s] * (a_prev[0][s] * acc_scr[s] + pvs[0][s]) + pvs[1][s]

    def two_bodies(kp, carry):
        body(2 * kp, 0, 1)
        body(2 * kp + 1, 1, 0)
        return carry

    nbody = (i + 1) // 2
    lax.fori_loop(0, nbody // 2, two_bodies, 0)

    @pl.when(nbody % 2 == 1)
    def _():
        body(nbody - 1, 0, 1)

    fs = nbody % 2
    kl = jnp.maximum(nbody - 1, 0)
    none = i == 0
    fa_blk = jnp.where(none, i, 2 * kl)
    fb_blk = jnp.where(none, i, jnp.minimum(2 * kl + 1, last_past))
    fb_valid = jnp.logical_or(none, i % 2 == 0)
    for s in range(n):
        pv_b = jnp.where(fb_valid, kept_pv(fb_blk, s, p_scr[2 * fs + 1, s], none), 0.0)
        acc_scr[s] = (a_scr[2 * fs + 1, s] * (a_scr[2 * fs, s] * acc_scr[s]
                                              + kept_pv(fa_blk, s, p_scr[2 * fs, s], none)) + pv_b)


def _flash_scratch(n, nv, t):
    return [pltpu.VMEM((n, 1, t), F32), pltpu.VMEM((n, nv + ONES_ROWS, t), F32), pltpu.VMEM((4, n, t, t), F32),
            pltpu.VMEM((4, n, t, t), BF16), pltpu.VMEM((4, n, 1, t), F32), pltpu.VMEM((4, n, 1, t), F32)]


def _normalised(acc_scr, s, nv):
    return acc_scr[s, :nv, :] / acc_scr[s, nv:nv + 1, :]


def _half_mask(x, half):
    lane = _lane_iota((1, LANES))
    return jnp.where((lane >= HEAD_DIM) == bool(half), x, jnp.zeros_like(x))


def _mla_kernel(qt_ref, k_ref, vt_ref, ot_ref, *scr, t, nh):
    i = pl.program_id(2)
    ks = [slice(LANES * s, LANES * (s + 1)) for s in range(nh)]
    qs = [qt_ref[0, 0, ks[s], :] for s in range(nh)]
    vr = [slice(HEAD_DIM * s, HEAD_DIM * (s + 1)) for s in range(nh)]
    _flash_t(qs, [k_ref.at[0, s] for s in range(nh)], vt_ref, vr, scr, i, t)
    for s in range(nh):
        ot_ref[0, 0, vr[s], :] = _normalised(scr[1], s, HEAD_DIM).astype(ot_ref.dtype)


def _mla_attention(qt, k, vt, nh):
    b, _, s, _ = k.shape
    _, nb, nv, t = vt.shape
    return pl.pallas_call(
        functools.partial(_mla_kernel, t=t, nh=nh),
        grid=(b, MLA_HEADS // nh, nb),
        in_specs=[pl.BlockSpec((1, 1, nh * LANES, t), lambda bi, p, i: (bi, i, p, 0)),
                  pl.BlockSpec((1, nh, s, LANES), lambda bi, p, i: (bi, p, 0, 0)),
                  pl.BlockSpec((1, nb, nh * HEAD_DIM, t), lambda bi, p, i: (bi, 0, p, 0))],
        out_specs=pl.BlockSpec((1, 1, nh * HEAD_DIM, t), lambda bi, p, i: (bi, i, p, 0)),
        out_shape=jax.ShapeDtypeStruct((b, nb, nv, t), BF16),
        scratch_shapes=_flash_scratch(nh, HEAD_DIM, t),
        compiler_params=_cparams(("parallel", "parallel", "arbitrary")),
        name="mla_attention",
    )(qt, k, vt)


def _diff_kernel(q_ref, k_ref, vt_ref, lam_ref, sub_ref, ot_ref, *scr, t, nh, lam_init):
    i = pl.program_id(2)
    qs, ks, vr = [], [], []
    for hd in range(nh):
        cols = slice(LANES * hd, LANES * (hd + 1))
        qs += _qt_halves(q_ref[0, :, cols])
        ks += [k_ref.at[0, :, cols]] * 2
        vr += [cols, cols]
    _flash_t(qs, ks, vt_ref, vr, scr, i, t)
    lv = lam_ref[...]
    lam = (jnp.exp(jnp.sum(lv[0:1] * lv[1:2], keepdims=True))
           - jnp.exp(jnp.sum(lv[2:3] * lv[3:4], keepdims=True)) + lam_init)
    for hd in range(nh):
        o = _normalised(scr[1], 2 * hd, LANES) - lam * _normalised(scr[1], 2 * hd + 1, LANES)
        o = o * lax.rsqrt(jnp.mean(o * o, axis=0, keepdims=True) + NORM_EPS) * sub_ref[...]
        ot_ref[0, 0, LANES * hd:LANES * (hd + 1), :] = (o * (1.0 - lam_init)).astype(ot_ref.dtype)


def _diff_attention(q, k, vt, lam_rows, subln_col, lam_init, nh):
    b, s, w = q.shape
    _, nb, _, t = vt.shape
    return pl.pallas_call(
        functools.partial(_diff_kernel, t=t, nh=nh, lam_init=lam_init),
        grid=(b, DIFF_HEADS // nh, nb),
        in_specs=[pl.BlockSpec((1, t, nh * LANES), lambda bi, p, i: (bi, i, p)),
                  pl.BlockSpec((1, s, nh * LANES), lambda bi, p, i: (bi, 0, p)),
                  pl.BlockSpec((1, nb, nh * LANES, t), lambda bi, p, i: (bi, 0, p, 0)),
                  pl.BlockSpec(lam_rows.shape, lambda bi, p, i: (0, 0)),
                  pl.BlockSpec(subln_col.shape, lambda bi, p, i: (0, 0))],
        out_specs=pl.BlockSpec((1, 1, nh * LANES, t), lambda bi, p, i: (bi, i, p, 0)),
        out_shape=jax.ShapeDtypeStruct((b, nb, w, t), BF16),
        scratch_shapes=_flash_scratch(2 * nh, LANES, t),
        compiler_params=_cparams(("parallel", "parallel", "arbitrary")),
        name="diff_attention",
    )(q, k, vt, lam_rows, subln_col)


def _moba_kernel(q_ref, qf_ref, k_ref, vt_ref, kmean_ref, ot_ref, sel_scr, *scr, t, npair):
    i = pl.program_id(2)
    nb = kmean_ref.shape[1]
    brow = lax.broadcasted_iota(jnp.int32, (nb, t), 0).astype(F32)
    past = brow < i.astype(F32)
    qs, ks, vr = [], [], []
    for pr in range(npair):
        cols = slice(LANES * pr, LANES * (pr + 1))
        qf, kmean = qf_ref[0, :, cols], kmean_ref[0, :, cols]
        qs += _qt_halves(q_ref[0, :, cols])
        kmean_halves = jnp.concatenate([_half_mask(kmean, 0), _half_mask(kmean, 1)], axis=0)
        gates = lax.dot_general(kmean_halves, qf, (((1,), (1,)), ((), ())),
                                preferred_element_type=F32, precision=lax.Precision.HIGHEST)
        for half in range(2):
            s = 2 * pr + half
            gate = jnp.where(past, gates[nb * half:nb * (half + 1)], NEG_INF)
            sel = jnp.zeros((nb, t), F32)
            for _ in range(MOBA_TOPK):
                top = jnp.max(gate, axis=0, keepdims=True)
                first = jnp.min(jnp.where(gate == top, brow, float(nb)), axis=0, keepdims=True)
                pick = brow == first
                sel = jnp.where(pick, 1.0, sel)
                gate = jnp.where(pick, 2 * NEG_INF, gate)
            sel_scr[s] = jnp.where(past, sel, 0.0)
            ks.append(k_ref.at[0, :, cols])
            vr.append(slice(HEAD_DIM * s, HEAD_DIM * (s + 1)))
    _flash_t(qs, ks, vt_ref, vr, scr, i, t,
             past_mask=lambda s, jj: sel_scr[s, pl.ds(jj, 1), :] > 0.5)
    for s in range(2 * npair):
        ot_ref[0, 0, vr[s], :] = _normalised(scr[1], s, HEAD_DIM).astype(ot_ref.dtype)


def _moba_attention(q, qf, k, vt, kmean, npair):
    b, s, w = q.shape
    _, nb, _, t = vt.shape
    nkm = kmean.shape[1]
    lanes = npair * LANES
    return pl.pallas_call(
        functools.partial(_moba_kernel, t=t, npair=npair),
        grid=(b, w // lanes, nb),
        in_specs=[pl.BlockSpec((1, t, lanes), lambda bi, p, i: (bi, i, p)),
                  pl.BlockSpec((1, t, lanes), lambda bi, p, i: (bi, i, p)),
                  pl.BlockSpec((1, s, lanes), lambda bi, p, i: (bi, 0, p)),
                  pl.BlockSpec((1, nb, lanes, t), lambda bi, p, i: (bi, 0, p, 0)),
                  pl.BlockSpec((1, nkm, lanes), lambda bi, p, i: (bi, 0, p))],
        out_specs=pl.BlockSpec((1, 1, lanes, t), lambda bi, p, i: (bi, i, p, 0)),
        out_shape=jax.ShapeDtypeStruct((b, nb, w, t), BF16),
        scratch_shapes=[pltpu.VMEM((2 * npair, nkm, t), F32)] + _flash_scratch(2 * npair, HEAD_DIM, t),
        compiler_params=_cparams(("parallel", "parallel", "arbitrary")),
        name="moba_attention",
    )(q, qf, k, vt, kmean)


def _sw_kernel(q_ref, kc_ref, vc_ref, kp_ref, vp_ref, o_ref, l_ref, *, nbk):
    i = pl.program_id(1)
    blk = SW_BLOCK
    qi = lax.broadcasted_iota(jnp.int32, (blk, 2 * blk), 0)
    kj = lax.broadcasted_iota(jnp.int32, (blk, 2 * blk), 1)
    band = (kj >= qi) & (kj <= qi + blk)
    lane = _lane_iota((1, LANES))
    heads = [(pair, half) for pair in range(DIL_W // LANES) for half in range(2)]

    def scores(sq, n):
        rows = slice(blk * n, blk * (n + 1))
        q = q_ref[sq, rows, :]
        if n == 0:
            kprev, valid = kp_ref[sq], band & ((kj >= blk) | (i > 0))
        else:
            kprev, valid = kc_ref[sq, blk * (n - 1):blk * n, :], band
        kk = jnp.concatenate([kprev, kc_ref[sq, rows, :]], axis=0)
        out = []
        for pair, half in heads:
            sl = slice(LANES * pair, LANES * (pair + 1))
            out.append(jnp.where(valid, _nt_dot(_half_mask(q[:, sl], half), kk[:, sl]), NEG_INF))
        return out

    work = [(sq, n) for sq in range(q_ref.shape[0]) for n in range(nbk)]
    nxt = scores(*work[0])
    for w, (sq, n) in enumerate(work):
        cur, rows = nxt, slice(blk * n, blk * (n + 1))
        if w + 1 < len(work):
            nxt = scores(*work[w + 1])
        vprev = vp_ref[sq] if n == 0 else vc_ref[sq, blk * (n - 1):blk * n, :]
        vv = jnp.concatenate([vprev, vc_ref[sq, rows, :]], axis=0)
        outs, lses = [], []
        for (pair, half), sc in zip(heads, cur):
            m = jnp.max(sc, axis=-1, keepdims=True)
            p = jnp.exp2(sc - m)
            l = jnp.sum(p, axis=-1, keepdims=True)
            sl = slice(LANES * pair, LANES * (pair + 1))
            outs.append(jnp.dot(p.astype(BF16), vv[:, sl], preferred_element_type=F32) / l)
            lses.append(m + jnp.log2(l))
        for pair in range(DIL_W // LANES):
            sl = slice(LANES * pair, LANES * (pair + 1))
            o_ref[sq, rows, sl] = jnp.where(lane < HEAD_DIM, outs[2 * pair], outs[2 * pair + 1])
            l_ref[sq, rows, sl] = jnp.where(lane < HEAD_DIM, lses[2 * pair], lses[2 * pair + 1])


def _sliding_window(dil, nbk, name):
    lead, (m, w3) = dil.shape[:-2], dil.shape[-2:]
    seqs = dil.reshape((-1, m, w3))
    per_seq = min(nbk, m // SW_BLOCK)
    nsq = max(1, nbk // per_seq)
    nbk = per_seq
    rows = SW_BLOCK * nbk
    cur = lambda off: pl.BlockSpec((nsq, rows, DIL_W), lambda n, i: (n, i, off))
    prev = lambda off: pl.BlockSpec((nsq, SW_BLOCK, DIL_W), lambda n, i: (n, jnp.maximum(i * nbk - 1, 0), off))
    out_spec = pl.BlockSpec((nsq, rows, DIL_W), lambda n, i: (n, i, 0))
    o, l = pl.pallas_call(
        functools.partial(_sw_kernel, nbk=nbk),
        grid=(seqs.shape[0] // nsq, m // rows),
        in_specs=[cur(0), cur(1), cur(2), prev(1), prev(2)],
        out_specs=[out_spec, out_spec],
        out_shape=[jax.ShapeDtypeStruct((seqs.shape[0], m, DIL_W), F32)] * 2,
        compiler_params=_cparams(("parallel", "arbitrary")),
        name=name,
    )(seqs, seqs, seqs, seqs, seqs)
    return o.reshape(lead + (m, DIL_W)), l.reshape(lead + (m, DIL_W))


def _mlp_tail(y, mlp_refs):
    sh_ref, sc_ref, g_ref, nw_ref, w1_ref, w2_ref = mlp_refs
    h = (_rms(y, nw_ref[...]) * (1.0 + sc_ref[0]) + sh_ref[0]).astype(BF16)
    acc = jnp.zeros(y.shape, F32)
    for c in range(w1_ref.shape[1] // MLP_CHUNK):
        cols = slice(MLP_CHUNK * c, MLP_CHUNK * (c + 1))
        a = jnp.maximum(jnp.dot(h, w1_ref[:, cols], preferred_element_type=F32), 0.0)
        acc = acc + jnp.dot((a * a).astype(BF16), w2_ref[cols, :], preferred_element_type=F32)
    return y + g_ref[0] * acc


def _proj_t(ot_ref, w_rows):
    return jnp.concatenate([_tn_dot(ot_ref[0, j], w_rows) for j in range(ot_ref.shape[1])], axis=0)


def _mlp_specs(mlp_args, d):
    per_b = pl.BlockSpec((1, 1, d), lambda bi, i: (bi, 0, 0))
    const = lambda a: pl.BlockSpec(a.shape, lambda bi, i: (0, 0), pipeline_mode=pl.Buffered(1))
    sh, sc, g2, nw, w1, w2 = mlp_args
    return [per_b, per_b, per_b, pl.BlockSpec(nw.shape, lambda bi, i: (0, 0)), const(w1), const(w2)]


def _even_out_kernel(x_ref, g_ref, oat_ref, o0_ref, o1_ref, o2_ref, l0_ref, l1_ref, l2_ref, w_ref,
                     *rest):
    *mlp_refs, y_ref, tscr = rest
    tm = x_ref.shape[1]
    ncol = DIL_W // LANES

    def token_major(ref, slot):
        r = ref.shape[1]
        for c in range(r):
            for j in range(ncol):
                tscr[slot * ncol + j, pl.ds(c, tm // r, stride=r), :] = ref[0, c, :, LANES * j:LANES * (j + 1)]
        return jnp.concatenate([tscr[slot * ncol + j] for j in range(ncol)], axis=1)

    o0, l0 = o0_ref[0], l0_ref[0]
    o1, l1 = token_major(o1_ref, 0), token_major(l1_ref, 1)
    o2, l2 = token_major(o2_ref, 2), token_major(l2_ref, 3)
    top = jnp.maximum(jnp.maximum(l0, l1), l2)
    w0, w1, w2 = jnp.exp2(l0 - top), jnp.exp2(l1 - top), jnp.exp2(l2 - top)
    ob = (w0 * o0 + w1 * o1 + w2 * o2) / (w0 + w1 + w2)
    na = oat_ref.shape[2]
    y = _proj_t(oat_ref, w_ref[:na, :]) + jnp.dot(ob.astype(BF16), w_ref[na:, :], preferred_element_type=F32)
    y_ref[0] = _mlp_tail(x_ref[0] + g_ref[0] * y, mlp_refs)


def _t_spec(a, tm):
    return pl.BlockSpec((1, tm // a.shape[-1]) + a.shape[2:], lambda bi, i: (bi, i, 0, 0))


def _even_out(x, g1, oat, o_dil, l_dil, w_out, mlp_args, tm):
    b, s, d = x.shape
    tok = lambda w: pl.BlockSpec((1, tm, w), lambda bi, i: (bi, i, 0))

    def dil_spec(a):
        if a.ndim == 3:
            return tok(DIL_W)
        r = a.shape[1]
        return pl.BlockSpec((1, r, tm // r, DIL_W), lambda bi, i: (bi, 0, i, 0))

    return pl.pallas_call(
        _even_out_kernel,
        grid=(b, s // tm),
        in_specs=[tok(d), pl.BlockSpec((1, 1, d), lambda bi, i: (bi, 0, 0)), _t_spec(oat, tm)]
                 + [dil_spec(a) for a in (*o_dil, *l_dil)]
                 + [pl.BlockSpec(w_out.shape, lambda bi, i: (0, 0), pipeline_mode=pl.Buffered(1))]
                 + _mlp_specs(mlp_args, d),
        out_specs=tok(d),
        out_shape=jax.ShapeDtypeStruct((b, s, d), F32),
        scratch_shapes=[pltpu.VMEM((4 * DIL_W // LANES, tm, LANES), F32)],
        compiler_params=_cparams(("parallel", "arbitrary")),
        name="even_out_mlp",
    )(x, g1, oat, *o_dil, *l_dil, w_out, *mlp_args)


def _odd_out_kernel(x_ref, g_ref, oct_ref, odt_ref, w_ref, *rest):
    *mlp_refs, y_ref = rest
    nc = oct_ref.shape[2]
    y = _proj_t(oct_ref, w_ref[:nc, :]) + _proj_t(odt_ref, w_ref[nc:, :])
    y_ref[0] = _mlp_tail(x_ref[0] + g_ref[0] * y, mlp_refs)


def _odd_out(x, g1, oct, odt, w_out, mlp_args, tm):
    b, s, d = x.shape
    tok = lambda w: pl.BlockSpec((1, tm, w), lambda bi, i: (bi, i, 0))
    return pl.pallas_call(
        _odd_out_kernel,
        grid=(b, s // tm),
        in_specs=[tok(d), pl.BlockSpec((1, 1, d), lambda bi, i: (bi, 0, 0)), _t_spec(oct, tm), _t_spec(odt, tm),
                  pl.BlockSpec(w_out.shape, lambda bi, i: (0, 0), pipeline_mode=pl.Buffered(1))]
                 + _mlp_specs(mlp_args, d),
        out_specs=tok(d),
        out_shape=jax.ShapeDtypeStruct((b, s, d), F32),
        compiler_params=_cparams(("parallel", "arbitrary")),
        name="odd_out_mlp",
    )(x, g1, oct, odt, w_out, *mlp_args)


def _rope_tables(positions):
    pos = positions.astype(F32)[:, :, None]
    half = HEAD_DIM // 2
    inv = ROPE_THETA ** (-jnp.arange(half, dtype=F32) / half)
    ang = pos * inv[_freq_order()]
    zeros = jnp.zeros(ang.shape[:2] + (LANES - 2 * half,), F32)
    return jnp.concatenate([jnp.cos(ang), jnp.sin(ang), zeros], axis=-1)


def _freq_order():
    half = HEAD_DIM // 2
    return np.concatenate([np.arange(0, half, 2), np.arange(1, half, 2)])


def _head_perm(n):
    within = np.concatenate([_freq_order(), HEAD_DIM // 2 + _freq_order()])
    idx = np.arange(n)
    return idx // HEAD_DIM * HEAD_DIM + within[idx % HEAD_DIM]


def _partner(n, width):
    idx = np.arange(n)
    return np.where(idx % width < width // 2, idx + width // 2, idx - width // 2)


def _gather_cols(w, cols):
    w = jnp.pad(w.astype(BF16), ((0, 0), (0, 1)))
    return jnp.take(w, jnp.asarray(cols, jnp.int32), axis=1)


def _even_weights(w_in, w_uq, w_ukv, qn, kn):
    o2 = MLA_Q_RANK + MLA_KV_RANK
    o3 = o2 + MLA_ROPE
    p_rope = _partner(MLA_ROPE, MLA_ROPE)
    tail = LANES - MLA_QK
    hp, partner = _head_perm(DIL_W), _partner(DIL_W, HEAD_DIM)
    ngroup = len(DIL_CONFIGS)

    zero = w_in.shape[1]
    kr = np.full(LANES, zero)
    kr_rot = kr.copy()
    kr[MLA_NOPE:MLA_QK] = o2 + np.arange(MLA_ROPE)
    kr_rot[MLA_NOPE:MLA_QK] = o2 + p_rope
    dil = lambda t, g, cols: o3 + (t * ngroup + g) * DIL_W + cols
    main = [dil(t, g, hp if t < 2 else np.arange(DIL_W)) for g in range(ngroup) for t in range(3)]
    rot = [dil(t, g, hp[partner]) for g in range(ngroup) for t in range(2)]
    win = _gather_cols(w_in, np.concatenate([np.arange(o2), kr, kr_rot] + main + rot))

    nqk = MLA_HEADS * MLA_QK
    head = np.full((MLA_HEADS, LANES), nqk)
    head[:, :MLA_QK] = np.arange(nqk).reshape(MLA_HEADS, MLA_QK)
    head_rot = head.copy()
    head_rot[:, MLA_NOPE:MLA_QK] = head[:, MLA_NOPE:MLA_QK][:, p_rope]
    wuq = _gather_cols(w_uq.reshape(MLA_Q_RANK, nqk), np.concatenate([head.ravel(), head_rot.ravel()]))

    nkv = w_ukv.shape[1] * w_ukv.shape[2]
    kcols = np.full((MLA_HEADS, LANES), nkv)
    kcols[:, :MLA_NOPE] = (np.arange(MLA_HEADS) * w_ukv.shape[2])[:, None] + np.arange(MLA_NOPE)
    wuk = _gather_cols(w_ukv.reshape(MLA_KV_RANK, nkv), kcols.ravel())
    wvt = w_ukv[:, :, MLA_NOPE:].reshape(MLA_KV_RANK, -1).T.astype(BF16)

    def gains(g):
        rot = jnp.concatenate([g[:MLA_NOPE], g[MLA_NOPE:][p_rope]])
        return jnp.stack([jnp.pad(g, (0, tail)), jnp.pad(rot, (0, tail))]).astype(F32)

    return win, wuq, wuk, wvt, gains(qn), gains(kn)


def _odd_weights(w_in):
    nqk = DIFF_HEADS * LANES
    nm = MOBA_HEADS * HEAD_DIM
    main = np.concatenate([np.arange(2 * nqk), 3 * nqk + np.arange(2 * nm)])
    main = main[_head_perm(main.size)]
    win = _gather_cols(w_in, np.concatenate([main, main[_partner(main.size, HEAD_DIM)]]))
    values = np.concatenate([2 * nqk + np.arange(nqk), 3 * nqk + 2 * nm + np.arange(nm)])
    return win, _gather_cols(w_in, values).T


def _gains64(g):
    g = g[_head_perm(HEAD_DIM)]
    rot = g[_partner(HEAD_DIM, HEAD_DIM)]
    return jnp.stack([jnp.concatenate([g, g]), jnp.concatenate([rot, rot])]).astype(F32)


def kernel(x, c, positions, ada_w, ada_b, norm_mix, norm_mlp, mlp_w1, mlp_w2, even_w_in, even_w_out, mla_q_lat_norm, mla_kv_lat_norm, mla_w_uq, mla_w_ukv, mla_q_norm, mla_k_norm, dil_q_norm, dil_k_norm, odd_w_in, odd_w_out, diff_q_norm, diff_k_norm, diff_lambda, diff_subln, moba_q_norm, moba_k_norm):
    b, s, d = x.shape
    depth = ada_w.shape[0]
    t_attn, tm_in, tm_out = ATTN_BLOCK, TOKEN_TILE, TOKEN_TILE

    mod = _adaln(c, ada_w, ada_b)
    rope = _rope_tables(positions)

    for layer in range(depth):
        sh1, sc1, g1, sh2, sc2, g2 = [mod[layer, :, d * t:d * (t + 1)].reshape(b, 1, d) for t in range(6)]
        mlp_args = (sh2, sc2, g2, _row(norm_mlp[layer]), mlp_w1[layer].astype(BF16), mlp_w2[layer].astype(BF16))
        i = layer // 2
        if layer % 2 == 0:
            win, wuq, wuk, wvt, qn, kn = _even_weights(even_w_in[i], mla_w_uq[i], mla_w_ukv[i],
                                                       mla_q_norm[i], mla_k_norm[i])
            q, k, vt, *dils = _even_in(x, sh1, sc1, _row(norm_mix[layer]), win, wuq, wuk, wvt,
                                       _row(mla_q_lat_norm[i]), _row(mla_kv_lat_norm[i]), qn, kn,
                                       _gains64(dil_q_norm[i]), _gains64(dil_k_norm[i]), rope, tm_in, t_attn)
            o_at = _mla_attention(q, k, vt, MLA_HEADS_PER_STEP)
            o_dil, l_dil = zip(*[_sliding_window(dg, SW_BLOCKS_PER_STEP, f"sliding_window_g{g}")
                                 for g, dg in enumerate(dils)])
            x = _even_out(x, g1, o_at, o_dil, l_dil, even_w_out[i].astype(BF16), mlp_args, tm_out)
        else:
            lam_init = 0.8 - 0.6 * math.exp(-0.3 * layer)
            win, wvt = _odd_weights(odd_w_in[i])
            qc, kc, qm, qmf, km, vct, vmt, kmean = _odd_in(
                x, sh1, sc1, _row(norm_mix[layer]), win, wvt,
                _gains64(diff_q_norm[i]), _gains64(diff_k_norm[i]), _gains64(moba_q_norm[i]),
                _gains64(moba_k_norm[i]),
                rope, tm_in, t_attn)
            o_ct = _diff_attention(qc, kc, vct, diff_lambda[i].astype(F32),
                                   diff_subln[i].reshape(-1, 1).astype(F32), lam_init, DIFF_HEADS_PER_STEP)
            o_dt = _moba_attention(qm, qmf, km, vmt, kmean.reshape(b, s // MOBA_BLOCK, kmean.shape[-1]),
                                   MOBA_PAIRS_PER_STEP)
            x = _odd_out(x, g1, o_ct, o_dt, odd_w_out[i].astype(BF16), mlp_args, tm_out)
    return x
```
